```python
import math
import jax
import jax.numpy as jnp
from jax import lax
import numpy as np

D_MODEL = 1024
BATCH = 2
SEQ = 16384
DEPTH = 2

N_EVEN = (DEPTH + 1) // 2
N_ODD = DEPTH // 2
EPS = 1e-6
NEG = -1e30
FORCE_SCORE = 1e9
Q_BLOCK = 128

FOX_HEADS = 8
FOX_DH = 64
GDN_HEADS = 4
GDN_DH = 128
GDN_CONV = 4
GDN_CHUNK = 64
NSA_HEADS = 8
NSA_KV_HEADS = 2
NSA_GROUP = NSA_HEADS // NSA_KV_HEADS
NSA_DH = 64
CMP_LEN = 32
CMP_STRIDE = 16
CMP_HIDDEN = 256
SLC_LEN = 64
SLC_TOPK = 16
WINDOW = 512
GLA_HEADS = 4
GLA_DK = 64
GLA_DV = 128
GLA_GATE_RANK = 16
GLA_TAU = 16.0
GLA_CHUNK = 64
REL_BUCKETS = 32
REL_MAX_DIST = 128
N_EXPERTS = 64
TOP_K = 6
D_EXPERT = 256
D_SHARED = 256
N_GROUPS = 8
TOPK_GROUPS = 4
ROUTE_SCALE = 2.5
MOE_BLOCK = 128

FOX_W = FOX_HEADS * FOX_DH
GDN_W = GDN_HEADS * GDN_DH
NSA_W = NSA_HEADS * NSA_DH
NSA_KV_W = NSA_KV_HEADS * NSA_DH
GLA_KW = GLA_HEADS * GLA_DK
GLA_W = GLA_HEADS * GLA_DV
EV_SIZES = (FOX_W, FOX_W, FOX_W, FOX_HEADS, 3 * GDN_W, GDN_HEADS, GDN_HEADS, GDN_W)
OD_SIZES = (NSA_W,) + (NSA_KV_W,) * 6 + (3 * NSA_HEADS, GLA_KW, GLA_KW, GLA_W, GLA_GATE_RANK, GLA_W)
EV_IN = sum(EV_SIZES)
OD_IN = sum(OD_SIZES)
EV_OUT = FOX_W + GDN_W
OD_OUT = NSA_W + GLA_W

kernel_name = 'hybrid_fox_gdn_nsa_gla_moe_trunk'

F32 = jnp.float32


def _split(a, sizes):
    cuts = [int(s) for s in np.cumsum(sizes)[:-1]]
    return jnp.split(a, cuts, axis=-1)


def rms_norm(x, g):
    xf = x.astype(F32)
    y = xf * lax.rsqrt(jnp.mean(xf * xf, axis=-1, keepdims=True) + EPS)
    return (y * g.astype(F32)).astype(x.dtype)


def l2_norm(x):
    xf = x.astype(F32)
    return xf * lax.rsqrt(jnp.sum(xf * xf, axis=-1, keepdims=True) + EPS)


def masked_softmax(s, mask, axis=-1):
    p = jax.nn.softmax(jnp.where(mask, s, NEG), axis=axis)
    return jnp.where(mask, p, 0.0)


def t5_bucket(dist):
    n = jnp.maximum(dist, 0)
    exact = REL_BUCKETS // 2
    nf = jnp.maximum(n, 1).astype(F32)
    large = exact + (jnp.log(nf / exact) / math.log(REL_MAX_DIST / exact) * (REL_BUCKETS - exact)).astype(jnp.int32)
    large = jnp.minimum(large, REL_BUCKETS - 1)
    return jnp.where(n < exact, n, large)


def causal_depthwise_conv(x, w):
    k = w.shape[0]
    xp = jnp.pad(x, ((0, 0), (k - 1, 0), (0, 0)))
    return lax.conv_general_dilated(xp, w[:, None, :].astype(x.dtype), window_strides=(1,), padding='VALID',
                                    dimension_numbers=('NWC', 'WIO', 'NWC'), feature_group_count=x.shape[-1])


def _to_chunks(a, c):
    b, t = a.shape[:2]
    a = a.reshape(b, t // c, c, *a.shape[2:])
    return jnp.swapaxes(a, 2, 3)


def _from_chunks(o):
    n, b, h, c, d = o.shape
    return o.transpose(1, 0, 3, 2, 4).reshape(b, n * c, h, d)


def fox_attention(q, k, v, log_f):
    b, t, h, dh = q.shape
    nb = t // Q_BLOCK
    cum = jnp.cumsum(log_f, axis=1)
    cum_k = jnp.swapaxes(cum, 1, 2)
    q_blocks = q.reshape(b, nb, Q_BLOCK, h, dh).transpose(1, 0, 3, 2, 4)
    c_blocks = cum.reshape(b, nb, Q_BLOCK, h).transpose(1, 0, 3, 2)
    k_pos = jnp.arange(t)
    scale = dh ** -0.5

    def block(args):
        i, q_i, c_i = args
        q_pos = i * Q_BLOCK + jnp.arange(Q_BLOCK)
        s = jnp.einsum('bhqd,bthd->bhqt', q_i, k, preferred_element_type=F32) * scale
        s = s + c_i[..., None] - cum_k[:, :, None, :]
        p = masked_softmax(s, k_pos[None, :] <= q_pos[:, None])
        return jnp.einsum('bhqt,bthd->bqhd', p.astype(v.dtype), v)

    o = lax.map(block, (jnp.arange(nb), q_blocks, c_blocks))
    return o.transpose(1, 0, 2, 3, 4).reshape(b, t, h * dh)


def gated_delta_rule(q, k, v, g, beta):
    dk = q.shape[-1]
    dv = v.shape[-1]
    c = GDN_CHUNK
    q = _to_chunks(q.astype(F32), c) * dk ** -0.5
    k = _to_chunks(k.astype(F32), c)
    v = _to_chunks(v.astype(F32), c)
    beta = _to_chunks(beta.astype(F32), c)
    g = jnp.cumsum(_to_chunks(g.astype(F32), c), axis=-1)
    causal = jnp.tril(jnp.ones((c, c), dtype=bool))
    strict = jnp.tril(jnp.ones((c, c), dtype=bool), k=-1)
    decay = jnp.exp(jnp.where(causal, g[..., :, None] - g[..., None, :], NEG))
    kb = k * beta[..., None]
    lower = jnp.where(strict, jnp.einsum('bnhid,bnhjd->bnhij', kb, k) * decay, 0.0)
    rhs = jnp.concatenate([v * beta[..., None], kb * jnp.exp(g)[..., None]], axis=-1)
    sol = lax.linalg.triangular_solve(lower + jnp.eye(c, dtype=F32), rhs, left_side=True, lower=True,
                                      unit_diagonal=True)
    u, w = sol[..., :dv], sol[..., dv:]
    attn = jnp.where(causal, jnp.einsum('bnhid,bnhjd->bnhij', q, k) * decay, 0.0)
    q_dec = q * jnp.exp(g)[..., None]
    g_last = g[..., -1]
    k_dec = k * jnp.exp(g_last[..., None] - g)[..., None]

    def step(S, xs):
        u_n, w_n, a_n, q_n, k_n, gl_n = xs
        v_new = u_n - jnp.einsum('bhck,bhkv->bhcv', w_n, S)
        o = jnp.einsum('bhck,bhkv->bhcv', q_n, S) + jnp.einsum('bhij,bhjv->bhiv', a_n, v_new)
        S = S * jnp.exp(gl_n)[..., None, None] + jnp.einsum('bhck,bhcv->bhkv', k_n, v_new)
        return S, o

    xs = tuple(jnp.swapaxes(a, 0, 1) for a in (u, w, attn, q_dec, k_dec, g_last))
    b, _, h = q.shape[:3]
    _, o = lax.scan(step, jnp.zeros((b, h, dk, dv), F32), xs)
    return _from_chunks(o)


def gla_chunked(q, k, v, log_a):
    dk = q.shape[-1]
    dv = v.shape[-1]
    c = GLA_CHUNK
    q = _to_chunks(q.astype(F32), c) * dk ** -0.5
    k = _to_chunks(k.astype(F32), c)
    v = _to_chunks(v.astype(F32), c)
    G = jnp.cumsum(_to_chunks(log_a.astype(F32), c), axis=-2)
    q_dec = q * jnp.exp(G)
    causal = jnp.tril(jnp.ones((c, c), dtype=bool))
    attn = jnp.where(causal, jnp.einsum('bnhid,bnhjd->bnhij', q_dec, k * jnp.exp(-G)), 0.0)
    o_intra = jnp.einsum('bnhij,bnhjv->bnhiv', attn, v)
    G_last = G[..., -1, :]
    k_dec = k * jnp.exp(G_last[..., None, :] - G)

    def step(S, xs):
        q_n, k_n, v_n, gl_n = xs
        o = jnp.einsum('bhck,bhkv->bhcv', q_n, S)
        S = S * jnp.exp(gl_n)[..., None] + jnp.einsum('bhck,bhcv->bhkv', k_n, v_n)
        return S, o

    xs = tuple(jnp.swapaxes(a, 0, 1) for a in (q_dec, k_dec, v, G_last))
    b, _, h = q.shape[:3]
    _, o_inter = lax.scan(step, jnp.zeros((b, h, dk, dv), F32), xs)
    return _from_chunks(o_inter + jnp.swapaxes(o_intra, 0, 1))


def nsa_compress(kv, pos, w1, w2):
    b, t, kvh, dh = kv.shape
    n_cmp = (t - CMP_LEN) // CMP_STRIDE + 1
    idx = jnp.arange(n_cmp)[:, None] * CMP_STRIDE + jnp.arange(CMP_LEN)[None, :]
    blk = kv[:, idx] + pos[None, None, :, None, :].astype(kv.dtype)
    blk = blk.transpose(0, 1, 3, 2, 4).reshape(b, n_cmp, kvh, CMP_LEN * dh)
    out = jax.nn.silu(blk @ w1) @ w2
    return out.transpose(0, 2, 1, 3)


def nsa_attention(q, k_cmp, v_cmp, k_slc, v_slc, k_win, v_win, gate_logits, qn, kn, pos, cw1, cw2, rel_bias):
    b, t = q.shape[:2]
    kvh, grp, dh = NSA_KV_HEADS, NSA_GROUP, NSA_DH
    nb = t // Q_BLOCK
    n_cmp = (t - CMP_LEN) // CMP_STRIDE + 1
    n_slc = t // SLC_LEN
    top_n = min(SLC_TOPK, n_slc)
    scale = dh ** -0.5

    def heads(a):
        return a.reshape(b, t, kvh, dh)

    qh = rms_norm(q.reshape(b, t, NSA_HEADS, dh), qn)
    q_blocks = qh.reshape(b, nb, Q_BLOCK, kvh, grp, dh).transpose(1, 0, 3, 4, 2, 5)
    kc = rms_norm(nsa_compress(heads(k_cmp), pos[0], cw1[0], cw2[0]), kn)
    vc = nsa_compress(heads(v_cmp), pos[1], cw1[1], cw2[1])
    ks = rms_norm(heads(k_slc), kn).reshape(b, n_slc, SLC_LEN, kvh, dh).transpose(0, 3, 1, 2, 4)
    vs = heads(v_slc).reshape(b, n_slc, SLC_LEN, kvh, dh).transpose(0, 3, 1, 2, 4)
    pad = ((0, 0), (0, 0), (WINDOW, 0), (0, 0))
    kw = jnp.pad(jnp.swapaxes(rms_norm(heads(k_win), kn), 1, 2), pad)
    vw = jnp.pad(jnp.swapaxes(heads(v_win), 1, 2), pad)
    cmp_start = jnp.arange(n_cmp) * CMP_STRIDE
    cmp_end = cmp_start + CMP_LEN - 1
    slc_start = jnp.arange(n_slc) * SLC_LEN
    overlap = ((cmp_start[:, None] < slc_start[None, :] + SLC_LEN)
               & (cmp_start[:, None] + CMP_LEN > slc_start[None, :])).astype(F32)
    table_h = rel_bias.T.reshape(kvh, grp, REL_BUCKETS).astype(F32)
    table_kg = jnp.transpose(rel_bias.reshape(REL_BUCKETS, kvh, grp), (1, 0, 2)).astype(F32)
    gather_blocks = jax.vmap(jax.vmap(lambda blk, ix: blk[ix]))
    blk_id = jnp.arange(n_slc)

    def block(args):
        i, q_i = args
        q_pos = i * Q_BLOCK + jnp.arange(Q_BLOCK)
        s = jnp.einsum('bkgqd,bknd->bkgqn', q_i, kc, preferred_element_type=F32) * scale
        s = s + table_h[:, :, t5_bucket(q_pos[:, None] - cmp_end[None, :])]
        p_cmp = masked_softmax(s, cmp_end[None, :] <= q_pos[:, None])
        o_cmp = jnp.einsum('bkgqn,bknd->bkgqd', p_cmp.astype(vc.dtype), vc)
        imp = jnp.einsum('bkgqn,ns->bkqs', p_cmp, overlap)
        cur = q_pos // SLC_LEN
        forced = (blk_id[None, :] == 0) | (blk_id[None, :] == cur[:, None]) | (blk_id[None, :] == cur[:, None] - 1)
        imp = jnp.where(forced, FORCE_SCORE, jnp.where(blk_id[None, :] <= cur[:, None], imp, NEG))
        _, idx = lax.top_k(imp, top_n)
        k_sel = gather_blocks(ks, idx)
        v_sel = gather_blocks(vs, idx)
        sel_pos = idx[..., None] * SLC_LEN + jnp.arange(SLC_LEN)
        dist = q_pos[:, None, None] - sel_pos
        bias = jax.vmap(lambda tb, bk: tb[bk], in_axes=(0, 1), out_axes=1)(table_kg, t5_bucket(dist))
        s = jnp.einsum('bkgqd,bkqnld->bkgqnl', q_i, k_sel, preferred_element_type=F32) * scale
        s = s + jnp.moveaxis(bias, -1, 2)
        p_sel = masked_softmax(s, (dist >= 0)[:, :, None], axis=(-2, -1))
        o_slc = jnp.einsum('bkgqnl,bkqnld->bkgqd', p_sel.astype(v_sel.dtype), v_sel)
        k_i = lax.dynamic_slice_in_dim(kw, i * Q_BLOCK, Q_BLOCK + WINDOW, axis=2)
        v_i = lax.dynamic_slice_in_dim(vw, i * Q_BLOCK, Q_BLOCK + WINDOW, axis=2)
        w_pos = i * Q_BLOCK - WINDOW + jnp.arange(Q_BLOCK + WINDOW)
        dist_w = q_pos[:, None] - w_pos[None, :]
        s = jnp.einsum('bkgqd,bkwd->bkgqw', q_i, k_i, preferred_element_type=F32) * scale
        s = s + table_h[:, :, t5_bucket(dist_w)]
        p_win = masked_softmax(s, (dist_w >= 0) & (dist_w < WINDOW) & (w_pos[None, :] >= 0))
        o_win = jnp.einsum('bkgqw,bkwd->bkgqd', p_win.astype(v_i.dtype), v_i)
        return jnp.stack([o_cmp, o_slc, o_win], axis=-1)

    o = lax.map(block, (jnp.arange(nb), q_blocks))
    o = o.transpose(1, 0, 4, 2, 3, 5, 6).reshape(b, t, NSA_HEADS, dh, 3)
    gates = jax.nn.sigmoid(gate_logits.astype(F32)).reshape(b, t, NSA_HEADS, 1, 3)
    return jnp.sum(o.astype(F32) * gates, axis=-1).reshape(b, t, NSA_W)


def even_mixer(h, w_in, w_out, fox_fb, fox_qn, fox_kn, gdn_conv, gdn_a_log, gdn_dt_bias, gdn_on):
    b, t, _ = h.shape
    fq, fk, fv, ff, gqkv, ga, gb, gz = _split(h @ w_in, EV_SIZES)

    def heads(a, n, d):
        return a.reshape(b, t, n, d)

    log_f = jax.nn.log_sigmoid((ff + fox_fb).astype(F32))
    o_fox = fox_attention(rms_norm(heads(fq, FOX_HEADS, FOX_DH), fox_qn),
                          rms_norm(heads(fk, FOX_HEADS, FOX_DH), fox_kn),
                          heads(fv, FOX_HEADS, FOX_DH), log_f)
    gq, gk, gv = jnp.split(jax.nn.silu(causal_depthwise_conv(gqkv, gdn_conv)), 3, axis=-1)
    g = -jnp.exp(gdn_a_log.astype(F32)) * jax.nn.softplus((ga + gdn_dt_bias).astype(F32))
    beta = jax.nn.sigmoid(gb.astype(F32))
    o = gated_delta_rule(l2_norm(heads(gq, GDN_HEADS, GDN_DH)), l2_norm(heads(gk, GDN_HEADS, GDN_DH)),
                         heads(gv, GDN_HEADS, GDN_DH), g, beta)
    o_gdn = rms_norm(o, gdn_on) * jax.nn.silu(heads(gz, GDN_HEADS, GDN_DH).astype(F32))
    y = jnp.concatenate([o_fox, o_gdn.reshape(b, t, GDN_W).astype(h.dtype)], axis=-1)
    return y @ w_out


def odd_mixer(h, w_in, w_out, nsa_qn, nsa_kn, nsa_pos, nsa_cmp_w1, nsa_cmp_w2, gla_wg_up, gla_bg, gla_on, rel_bias):
    b, t, _ = h.shape
    nq, kc, vc, ks, vs, kw, vw, ng, lq, lk, lv, lg, lr = _split(h @ w_in, OD_SIZES)
    o_nsa = nsa_attention(nq, kc, vc, ks, vs, kw, vw, ng, nsa_qn, nsa_kn, nsa_pos, nsa_cmp_w1, nsa_cmp_w2, rel_bias)
    log_a = jax.nn.log_sigmoid((lg @ gla_wg_up + gla_bg).astype(F32)) / GLA_TAU
    o = gla_chunked(lq.reshape(b, t, GLA_HEADS, GLA_DK), lk.reshape(b, t, GLA_HEADS, GLA_DK),
                    lv.reshape(b, t, GLA_HEADS, GLA_DV), log_a.reshape(b, t, GLA_HEADS, GLA_DK))
    o_gla = rms_norm(o, gla_on) * jax.nn.silu(lr.reshape(b, t, GLA_HEADS, GLA_DV).astype(F32))
    y = jnp.concatenate([o_nsa.astype(h.dtype), o_gla.reshape(b, t, GLA_W).astype(h.dtype)], axis=-1)
    return y @ w_out


def swiglu(x, wg, wu, wd):
    return (jax.nn.silu(x @ wg) * (x @ wu)) @ wd


def moe_ffn(h, router, e_bias, wg, wu, wd, sg, su, sd):
    b, t, d = h.shape
    xt = h.reshape(b * t, d)
    nt = b * t
    scores = jax.nn.sigmoid((xt @ router).astype(F32))
    biased = scores + e_bias.astype(F32)
    grp_score = lax.top_k(biased.reshape(nt, N_GROUPS, N_EXPERTS // N_GROUPS), 2)[0].sum(-1)
    _, top_g = lax.top_k(grp_score, TOPK_GROUPS)
    gmask = jax.nn.one_hot(top_g, N_GROUPS, dtype=F32).sum(axis=1) > 0
    emask = jnp.repeat(gmask, N_EXPERTS // N_GROUPS, axis=1)
    _, top_e = lax.top_k(jnp.where(emask, biased, -jnp.inf), TOP_K)
    w = jnp.take_along_axis(scores, top_e, axis=1)
    w = w / jnp.sum(w, axis=-1, keepdims=True) * ROUTE_SCALE
    n_assign = nt * TOP_K
    eid = top_e.reshape(-1)
    tok = jnp.repeat(jnp.arange(nt, dtype=jnp.int32), TOP_K)
    wts = w.reshape(-1)
    order = jnp.argsort(eid)
    e_sorted = eid[order]
    counts = jnp.zeros((N_EXPERTS,), jnp.int32).at[eid].add(1)
    starts = jnp.cumsum(counts) - counts
    pcounts = (counts + MOE_BLOCK - 1) // MOE_BLOCK * MOE_BLOCK
    pends = jnp.cumsum(pcounts)
    pstarts = pends - pcounts
    dest = pstarts[e_sorted] + jnp.arange(n_assign, dtype=jnp.int32) - starts[e_sorted]
    nblk = -(-(n_assign + N_EXPERTS * (MOE_BLOCK - 1)) // MOE_BLOCK)
    rows = nblk * MOE_BLOCK
    row_tok = jnp.full((rows,), nt, jnp.int32).at[dest].set(tok[order])
    row_w = jnp.zeros((rows,), F32).at[dest].set(wts[order])
    blk_e = jnp.minimum(jnp.sum(jnp.arange(nblk)[:, None] * MOE_BLOCK >= pends[None, :], axis=1), N_EXPERTS - 1)
    x_pad = jnp.concatenate([xt, jnp.zeros((1, d), xt.dtype)], axis=0)

    def step(acc, xs):
        rt, rw, e = xs
        y = swiglu(x_pad[rt], wg[e], wu[e], wd[e])
        return acc.at[rt].add(y * rw[:, None].astype(y.dtype)), None

    acc, _ = lax.scan(step, jnp.zeros((nt + 1, d), h.dtype),
                      (row_tok.reshape(nblk, MOE_BLOCK), row_w.reshape(nblk, MOE_BLOCK), blk_e))
    out = acc[:nt] + swiglu(xt, sg, su, sd)
    return out.reshape(b, t, d)


def setup_inputs(seed: int = 0) -> dict:
    key = jax.random.key(seed)
    ks = iter(jax.random.split(key, 40))

    def nrm(shape, std):
        return std * jax.random.normal(next(ks), shape, F32)

    def gain(shape):
        return 1.0 + 0.05 * jax.random.normal(next(ks), shape, F32)

    D = D_MODEL
    dt = jnp.exp(jax.random.uniform(next(ks), (N_EVEN, GDN_HEADS), F32, math.log(1e-3), math.log(1e-1)))
    return {
        'x': nrm((BATCH, SEQ, D), 1.0),
        'c': nrm((BATCH, D), 1.0),
        'ada_w': nrm((DEPTH, D, 6 * D), 0.5 * D ** -0.5),
        'ada_b': nrm((DEPTH, 6 * D), 0.02),
        'norm_mix': gain((DEPTH, D)),
        'norm_ffn': gain((DEPTH, D)),
        'rel_bias': nrm((REL_BUCKETS, NSA_HEADS), 0.2),
        'ev_w_in': nrm((N_EVEN, D, EV_IN), D ** -0.5),
        'ev_w_out': nrm((N_EVEN, EV_OUT, D), EV_OUT ** -0.5),
        'fox_fb': 3.0 + 0.5 * jax.random.normal(next(ks), (N_EVEN, FOX_HEADS), F32),
        'fox_qn': gain((N_EVEN, FOX_DH)),
        'fox_kn': gain((N_EVEN, FOX_DH)),
        'gdn_conv': nrm((N_EVEN, GDN_CONV, 3 * GDN_W), GDN_CONV ** -0.5),
        'gdn_a_log': jnp.log(jax.random.uniform(next(ks), (N_EVEN, GDN_HEADS), F32, 1.0, 16.0)),
        'gdn_dt_bias': dt + jnp.log(-jnp.expm1(-dt)),
        'gdn_on': gain((N_EVEN, GDN_DH)),
        'od_w_in': nrm((N_ODD, D, OD_IN), D ** -0.5),
        'od_w_out': nrm((N_ODD, OD_OUT, D), OD_OUT ** -0.5),
        'nsa_qn': gain((N_ODD, NSA_DH)),
        'nsa_kn': gain((N_ODD, NSA_DH)),
        'nsa_pos': nrm((N_ODD, 2, CMP_LEN, NSA_DH), 0.1),
        'nsa_cmp_w1': nrm((N_ODD, 2, CMP_LEN * NSA_DH, CMP_HIDDEN), (CMP_LEN * NSA_DH) ** -0.5),
        'nsa_cmp_w2': nrm((N_ODD, 2, CMP_HIDDEN, NSA_DH), CMP_HIDDEN ** -0.5),
        'gla_wg_up': nrm((N_ODD, GLA_GATE_RANK, GLA_KW), GLA_GATE_RANK ** -0.5),
        'gla_bg': nrm((N_ODD, GLA_KW), 0.1),
        'gla_on': gain((N_ODD, GLA_DV)),
        'moe_router': nrm((DEPTH, D, N_EXPERTS), D ** -0.5),
        'moe_bias': nrm((DEPTH, N_EXPERTS), 0.01),
        'moe_wg': nrm((DEPTH, N_EXPERTS, D, D_EXPERT), D ** -0.5),
        'moe_wu': nrm((DEPTH, N_EXPERTS, D, D_EXPERT), D ** -0.5),
        'moe_wd': nrm((DEPTH, N_EXPERTS, D_EXPERT, D), D_EXPERT ** -0.5),
        'sh_wg': nrm((DEPTH, D, D_SHARED), D ** -0.5),
        'sh_wu': nrm((DEPTH, D, D_SHARED), D ** -0.5),
        'sh_wd': nrm((DEPTH, D_SHARED, D), D_SHARED ** -0.5),
    }


def reference(x, c, ada_w, ada_b, norm_mix, norm_ffn, rel_bias, ev_w_in, ev_w_out, fox_fb, fox_qn, fox_kn,
              gdn_conv, gdn_a_log, gdn_dt_bias, gdn_on, od_w_in, od_w_out, nsa_qn, nsa_kn, nsa_pos, nsa_cmp_w1,
              nsa_cmp_w2, gla_wg_up, gla_bg, gla_on, moe_router, moe_bias, moe_wg, moe_wu, moe_wd, sh_wg, sh_wu,
              sh_wd):
    cond = jax.nn.silu(c)
    for layer in range(DEPTH):
        mod = (cond @ ada_w[layer] + ada_b[layer])[:, None, :]
        sh1, sc1, g1, sh2, sc2, g2 = jnp.split(mod, 6, axis=-1)
        h = rms_norm(x, norm_mix[layer]) * (1.0 + sc1) + sh1
        j = layer // 2
        if layer % 2 == 0:
            y = even_mixer(h, ev_w_in[j], ev_w_out[j], fox_fb[j], fox_qn[j], fox_kn[j], gdn_conv[j],
                           gdn_a_log[j], gdn_dt_bias[j], gdn_on[j])
        else:
            y = odd_mixer(h, od_w_in[j], od_w_out[j], nsa_qn[j], nsa_kn[j], nsa_pos[j], nsa_cmp_w1[j],
                          nsa_cmp_w2[j], gla_wg_up[j], gla_bg[j], gla_on[j], rel_bias)
        x = x + g1 * y
        h = rms_norm(x, norm_ffn[layer]) * (1.0 + sc2) + sh2
        x = x + g2 * moe_ffn(h, moe_router[layer], moe_bias[layer], moe_wg[layer], moe_wu[layer],
                             moe_wd[layer], sh_wg[layer], sh_wu[layer], sh_wd[layer])
    return x
```

```python
import functools
import math

import numpy as np
import jax
import jax.numpy as jnp
from jax import lax
from jax.experimental import pallas as pl
from jax.experimental.pallas import tpu as pltpu

F32 = jnp.float32
BF16 = jnp.bfloat16
HI = lax.Precision.HIGHEST

EPS = 1e-6
NEG = -1e30
FORCE_SCORE = 1e9

FOX_HEADS, FOX_DH = 8, 64
GDN_HEADS, GDN_DH, GDN_CONV, GDN_CHUNK = 4, 128, 4, 64
NSA_HEADS, NSA_KV_HEADS, NSA_DH = 8, 2, 64
NSA_GROUP = NSA_HEADS // NSA_KV_HEADS
CMP_LEN, CMP_STRIDE, CMP_HIDDEN = 32, 16, 256
SLC_LEN, SLC_TOPK, WINDOW = 64, 16, 512
GLA_HEADS, GLA_DK, GLA_DV, GLA_GATE_RANK, GLA_TAU, GLA_CHUNK = 4, 64, 128, 16, 16.0, 64
REL_BUCKETS, REL_MAX_DIST = 32, 128
N_EXPERTS, TOP_K, D_EXPERT, D_SHARED = 64, 6, 256, 256
N_GROUPS, TOPK_GROUPS, ROUTE_SCALE = 8, 4, 2.5

FOX_W = FOX_HEADS * FOX_DH
GDN_W = GDN_HEADS * GDN_DH
NSA_W = NSA_HEADS * NSA_DH
NSA_KV_W = NSA_KV_HEADS * NSA_DH
GLA_KW = GLA_HEADS * GLA_DK
GLA_W = GLA_HEADS * GLA_DV
EV_SIZES = (FOX_W, FOX_W, FOX_W, FOX_HEADS, 3 * GDN_W, GDN_HEADS, GDN_HEADS, GDN_W)
OD_SIZES = (NSA_W,) + (NSA_KV_W,) * 6 + (3 * NSA_HEADS, GLA_KW, GLA_KW, GLA_W, GLA_GATE_RANK, GLA_W)

LANES = 128
VMEM_LIMIT = 56 * 1024 * 1024


def _cparams(sem):
    return pltpu.CompilerParams(dimension_semantics=sem, vmem_limit_bytes=VMEM_LIMIT)


def _full(shape):
    n = len(shape)
    return pl.BlockSpec(shape, lambda *_: (0,) * n)


def _dot(a, b):
    return jnp.dot(a, b, preferred_element_type=F32)


def _dot_hi(a, b):
    return jnp.dot(a, b, precision=HI, preferred_element_type=F32)


def _dot_nt(a, b, precision=None):
    return lax.dot_general(a, b, (((1,), (1,)), ((), ())), precision=precision, preferred_element_type=F32)


def _dot_tn(a, b, precision=None):
    return lax.dot_general(a, b, (((0,), (0,)), ((), ())), precision=precision, preferred_element_type=F32)


def _sigmoid(x):
    return 1.0 / (1.0 + jnp.exp(-x))


def _silu(x):
    return x * _sigmoid(x)


def _softplus(x):
    return jnp.maximum(x, 0.0) + jnp.log(1.0 + jnp.exp(-jnp.abs(x)))


def _log_sigmoid(x):
    return -_softplus(-x)


def _adaln_kernel(c_ref, w_ref, b_ref, o_ref):
    c = c_ref[...]
    o_ref[0] = _dot_hi(_silu(c), w_ref[0]) + b_ref[0]


def adaln(c, ada_w, ada_b):
    depth, d, n = ada_w.shape
    b = c.shape[0]
    cp = jnp.zeros((8, d), F32).at[:b].set(c)
    tn = 1536
    out = pl.pallas_call(
        _adaln_kernel,
        grid=(depth, n // tn),
        in_specs=[_full((8, d)),
                  pl.BlockSpec((1, d, tn), lambda l, j: (l, 0, j)),
                  pl.BlockSpec((1, 1, tn), lambda l, j: (l, 0, j))],
        out_specs=pl.BlockSpec((1, 8, tn), lambda l, j: (l, 0, j)),
        out_shape=jax.ShapeDtypeStruct((depth, 8, n), F32),
        compiler_params=_cparams(("arbitrary", "arbitrary")),
        name="adaln",
    )(cp, ada_w, ada_b.reshape(depth, 1, n))
    return out[:, :b]


def _ln_kernel(x_ref, g_ref, sc_ref, sh_ref, o_ref):
    x = x_ref[0]
    y = x * lax.rsqrt(jnp.mean(x * x, axis=-1, keepdims=True) + EPS) * g_ref[...]
    o_ref[0] = (y * (1.0 + sc_ref[0]) + sh_ref[0]).astype(o_ref.dtype)


def ln_mod(x, g, sc, sh, out_dtype, tm=512):
    b, t, d = x.shape
    return pl.pallas_call(
        _ln_kernel,
        grid=(b, t // tm),
        in_specs=[pl.BlockSpec((1, tm, d), lambda i, j: (i, j, 0)),
                  _full((1, d)),
                  pl.BlockSpec((1, 1, d), lambda i, j: (i, 0, 0)),
                  pl.BlockSpec((1, 1, d), lambda i, j: (i, 0, 0))],
        out_specs=pl.BlockSpec((1, tm, d), lambda i, j: (i, j, 0)),
        out_shape=jax.ShapeDtypeStruct((b, t, d), out_dtype),
        compiler_params=_cparams(("arbitrary", "arbitrary")),
        name="ln_mod",
    )(x, g.reshape(1, d), sc.reshape(b, 1, d), sh.reshape(b, 1, d))


def proj(h, w, out_dtype, epilogue=None, extras=(), tm=512, name="proj"):
    b, t, d = h.shape
    n = w.shape[1]

    def kern(h_ref, w_ref, *rest):
        o_ref = rest[-1]
        y = _dot(h_ref[0], w_ref[...])
        if epilogue is not None:
            y = epilogue(y, *[e[...] for e in rest[:-1]])
        o_ref[0] = y.astype(out_dtype)

    return pl.pallas_call(
        kern,
        grid=(b, t // tm),
        in_specs=[pl.BlockSpec((1, tm, d), lambda i, j: (i, j, 0)), _full((d, n))]
                 + [_full(e.shape) for e in extras],
        out_specs=pl.BlockSpec((1, tm, n), lambda i, j: (i, j, 0)),
        out_shape=jax.ShapeDtypeStruct((b, t, n), out_dtype),
        compiler_params=_cparams(("arbitrary", "arbitrary")),
        name=name,
    )(h, w, *extras)


def _head_norm_epilogue(dh):
    inv = 1.0 / dh

    def ep(y, bd, gain):
        ssq = _dot((y * y).astype(BF16), bd)
        return y * lax.rsqrt(ssq * inv + EPS) * gain

    return ep


def _block_diag_ones(n, dh):
    i = np.arange(n) // dh
    return jnp.asarray((i[:, None] == i[None, :]).astype(np.float32), dtype=BF16)


def _outproj_kernel(y1_ref, y2_ref, wa_ref, wb_ref, x_ref, g_ref, o_ref):
    y = _dot(y1_ref[0], wa_ref[...]) + _dot(y2_ref[0], wb_ref[...])
    o_ref[0] = x_ref[0] + g_ref[0] * y


def out_proj(y1, y2, w_out, x, gate, tm=512):
    b, t, d = x.shape
    n1, n2 = y1.shape[-1], y2.shape[-1]
    wa = w_out[:n1].astype(BF16)
    wb = w_out[n1:].astype(BF16)
    return pl.pallas_call(
        _outproj_kernel,
        grid=(b, t // tm),
        in_specs=[pl.BlockSpec((1, tm, n1), lambda i, j: (i, j, 0)),
                  pl.BlockSpec((1, tm, n2), lambda i, j: (i, j, 0)),
                  _full((n1, d)), _full((n2, d)),
                  pl.BlockSpec((1, tm, d), lambda i, j: (i, j, 0)),
                  pl.BlockSpec((1, 1, d), lambda i, j: (i, 0, 0))],
        out_specs=pl.BlockSpec((1, tm, d), lambda i, j: (i, j, 0)),
        out_shape=jax.ShapeDtypeStruct((b, t, d), F32),
        compiler_params=_cparams(("arbitrary", "arbitrary")),
        name="out_proj",
    )(y1, y2, wa, wb, x, gate.reshape(b, 1, d))


def _decay_kernel(s_ref, fb_ref, tril_ref, o_ref, carry):
    @pl.when(pl.program_id(1) == 0)
    def _():
        carry[...] = jnp.zeros_like(carry)

    tm = s_ref.shape[1]
    lf = _log_sigmoid(s_ref[0] + fb_ref[...])
    cum = _dot_hi(tril_ref[...], lf) + carry[...]
    carry[...] = cum[tm - 1:tm, :]
    o_ref[0] = cum.T[0:8, :]


def fox_decay(small, fox_fb, tm=512):
    b, t, _ = small.shape
    fb = jnp.zeros((1, LANES), F32).at[0, :FOX_HEADS].set(fox_fb)
    tril = jnp.asarray(np.tril(np.ones((tm, tm), np.float32)))
    return pl.pallas_call(
        _decay_kernel,
        grid=(b, t // tm),
        in_specs=[pl.BlockSpec((1, tm, LANES), lambda i, j: (i, j, 0)), _full((1, LANES)), _full((tm, tm))],
        out_specs=pl.BlockSpec((1, 8, tm), lambda i, j: (i, 0, j)),
        out_shape=jax.ShapeDtypeStruct((b, 8, t), F32),
        scratch_shapes=[pltpu.VMEM((1, LANES), F32)],
        compiler_params=_cparams(("arbitrary", "arbitrary")),
        name="fox_decay",
    )(small, fb, tril)


def _fox_kernel(q_ref, k_ref, v_ref, ck_ref, o_ref, *, tq):
    i = pl.program_id(2)
    lo = lax.broadcasted_iota(jnp.int32, (1, LANES), 1) < FOX_DH
    q = q_ref[0]
    zero = jnp.zeros_like(q)
    qs = (jnp.where(lo, q, zero), jnp.where(lo, zero, q))
    causal = (lax.broadcasted_iota(jnp.int32, (tq, tq), 1) <= lax.broadcasted_iota(jnp.int32, (tq, tq), 0))

    def step(j, carry, diag):
        start = pl.multiple_of(j * tq, tq)
        kt = k_ref[0, pl.ds(start, tq), :]
        vt = v_ref[0, pl.ds(start, tq), :]
        ck = ck_ref[0, 0, j]
        new = []
        for hh in range(2):
            m, l, acc = carry[hh]
            s = _dot_nt(qs[hh], kt) - ck[hh:hh + 1, :]
            if diag:
                s = jnp.where(causal, s, NEG)
            m_new = jnp.maximum(m, jnp.max(s, axis=1, keepdims=True))
            alpha = jnp.exp(m - m_new)
            p = jnp.exp(s - m_new)
            l = alpha * l + jnp.sum(p, axis=1, keepdims=True)
            acc = alpha * acc + _dot(p.astype(BF16), vt)
            new.append((m_new, l, acc))
        return tuple(new)

    init = tuple((jnp.full((tq, 1), NEG, F32), jnp.zeros((tq, 1), F32), jnp.zeros((tq, LANES), F32))
                 for _ in range(2))
    carry = lax.fori_loop(0, i, lambda j, c: step(j, c, False), init)
    carry = step(i, carry, True)
    o0 = carry[0][2] / carry[0][1]
    o1 = carry[1][2] / carry[1][1]
    o_ref[0] = jnp.where(lo, o0, o1).astype(o_ref.dtype)


def fox_attention(q, k, v, cum_t, tq=512):
    b, t, w = q.shape
    npair = w // LANES
    nt = t // tq
    ck = cum_t.reshape(b, npair, 2, nt, tq).transpose(0, 1, 3, 2, 4)
    return pl.pallas_call(
        functools.partial(_fox_kernel, tq=tq),
        grid=(b, npair, nt),
        in_specs=[pl.BlockSpec((1, tq, LANES), lambda bi, p, i: (bi, i, p)),
                  pl.BlockSpec((1, t, LANES), lambda bi, p, i: (bi, 0, p)),
                  pl.BlockSpec((1, t, LANES), lambda bi, p, i: (bi, 0, p)),
                  pl.BlockSpec((1, 1, nt, 2, tq), lambda bi, p, i: (bi, p, 0, 0, 0))],
        out_specs=pl.BlockSpec((1, tq, LANES), lambda bi, p, i: (bi, i, p)),
        out_shape=jax.ShapeDtypeStruct((b, t, w), BF16),
        compiler_params=_cparams(("arbitrary", "arbitrary", "arbitrary")),
        name="fox_attn",
    )(q, k, v, ck)


def _neumann_inverse(a):
    n = a.shape[0]
    eye = (lax.broadcasted_iota(jnp.int32, (n, n), 0) == lax.broadcasted_iota(jnp.int32, (n, n), 1)).astype(F32)
    p = eye - a
    pw = a
    k = 1
    while 2 * k < n:
        pw = _dot_hi(pw, pw)
        p = p + _dot_hi(p, pw)
        k *= 2
    return p


def _gdn_kernel(x_ref, sm_ref, z_ref, cw_ref, ega_ref, egb_ref, alog_ref, dtb_ref, on_ref, tril_ref,
                o_ref, s_scr, prev_scr, q_scr, k_scr, v_scr, gc_scr, be_scr, *, tc):
    c = GDN_CHUNK
    w = GDN_W

    @pl.when(pl.program_id(1) == 0)
    def _():
        s_scr[...] = jnp.zeros_like(s_scr)
        prev_scr[...] = jnp.zeros_like(prev_scr)

    x = x_ref[0]
    prev = prev_scr[...]
    row8 = lax.broadcasted_iota(jnp.int32, (8, 1), 0)
    acc = x * cw_ref[GDN_CONV - 1:GDN_CONV, :]
    for s in range(1, GDN_CONV):
        rolled = pltpu.roll(x, s, 0)
        head = jnp.where(row8 < s, pltpu.roll(prev, s, 0), rolled[0:8])
        shifted = jnp.concatenate([head, rolled[8:]], axis=0)
        acc = acc + shifted * cw_ref[GDN_CONV - 1 - s:GDN_CONV - s, :]
    prev_scr[...] = x[tc - 8:tc]
    xc = _silu(acc)

    sm = sm_ref[0]
    g_raw = _dot_hi(sm, ega_ref[...])
    b_raw = _dot_hi(sm, egb_ref[...])
    g = -jnp.exp(alog_ref[...]) * _softplus(g_raw + dtb_ref[...])
    be_scr[...] = _sigmoid(b_raw)
    gc_scr[...] = _dot_hi(tril_ref[...], g)

    for h in range(GDN_HEADS):
        ln = slice(h * GDN_DH, (h + 1) * GDN_DH)
        qh = xc[:, h * GDN_DH:(h + 1) * GDN_DH]
        kh = xc[:, w + h * GDN_DH:w + (h + 1) * GDN_DH]
        q_scr[:, ln] = qh * lax.rsqrt(jnp.sum(qh * qh, axis=-1, keepdims=True) + EPS) * (GDN_DH ** -0.5)
        k_scr[:, ln] = kh * lax.rsqrt(jnp.sum(kh * kh, axis=-1, keepdims=True) + EPS)
    v_scr[...] = xc[:, 2 * w:3 * w]

    ri = lax.broadcasted_iota(jnp.int32, (c, c), 0)
    ci = lax.broadcasted_iota(jnp.int32, (c, c), 1)
    causal = ci <= ri
    strict = ci < ri
    lane = lax.broadcasted_iota(jnp.int32, (1, GDN_DH), 1)

    def chunk(n, _):
        sl = pl.ds(pl.multiple_of(n * c, c), c)
        for h in range(GDN_HEADS):
            ln = slice(h * GDN_DH, (h + 1) * GDN_DH)
            q = q_scr[sl, ln]
            k = k_scr[sl, ln]
            v = v_scr[sl, ln]
            gc = gc_scr[sl, ln]
            beta = be_scr[sl, ln]
            xa = jnp.where(lane == 0, gc, jnp.where(lane == 1, 1.0, 0.0))
            ya = jnp.where(lane == 0, 1.0, jnp.where(lane == 1, -gc, 0.0))
            d = _dot_nt(xa, ya, HI)
            decay = jnp.exp(jnp.where(causal, d, NEG))
            kb = k * beta
            a = jnp.where(strict, _dot_nt(kb.astype(BF16), k.astype(BF16)) * decay, 0.0)
            tm = _neumann_inverse(a)
            egc = jnp.exp(gc)
            u = _dot_hi(tm, v * beta)
            wm = _dot_hi(tm, kb * egc)
            attn = jnp.where(causal, _dot_nt(q.astype(BF16), k.astype(BF16)) * decay, 0.0)
            q_dec = q * egc
            gl = gc[c - 1:c, :]
            k_dec = k * jnp.exp(gl - gc)
            s = s_scr[h]
            v_new = u - _dot(wm.astype(BF16), s.astype(BF16))
            o = _dot(q_dec.astype(BF16), s.astype(BF16)) + _dot(attn.astype(BF16), v_new.astype(BF16))
            s_scr[h] = s * jnp.exp(gl) + _dot_tn(k_dec.astype(BF16), v_new.astype(BF16))
            on = o * lax.rsqrt(jnp.mean(o * o, axis=-1, keepdims=True) + EPS) * on_ref[...]
            o_ref[0, sl, ln] = (on * _silu(z_ref[0, sl, ln])).astype(o_ref.dtype)
        return 0

    lax.fori_loop(0, tc // c, chunk, 0)


def gated_delta_net(x, small, z, conv_w, a_log, dt_bias, on_gain, tc=512):
    b, t, _ = x.shape
    w = GDN_W
    ega = np.zeros((LANES, w), np.float32)
    egb = np.zeros((LANES, w), np.float32)
    for h in range(GDN_HEADS):
        ega[FOX_HEADS + h, h * GDN_DH:(h + 1) * GDN_DH] = 1.0
        egb[FOX_HEADS + GDN_HEADS + h, h * GDN_DH:(h + 1) * GDN_DH] = 1.0
    alog = jnp.repeat(a_log, GDN_DH).reshape(1, w)
    dtb = jnp.repeat(dt_bias, GDN_DH).reshape(1, w)
    idx = np.arange(tc)
    tril = ((idx[:, None] >= idx[None, :]) & (idx[:, None] // GDN_CHUNK == idx[None, :] // GDN_CHUNK))
    row = lambda n: pl.BlockSpec((1, tc, n), lambda i, j: (i, j, 0))
    return pl.pallas_call(
        functools.partial(_gdn_kernel, tc=tc),
        grid=(b, t // tc),
        in_specs=[row(3 * w), row(LANES), row(w), _full((GDN_CONV, 3 * w)), _full((LANES, w)), _full((LANES, w)),
                  _full((1, w)), _full((1, w)), _full((1, GDN_DH)), _full((tc, tc))],
        out_specs=row(w),
        out_shape=jax.ShapeDtypeStruct((b, t, w), BF16),
        scratch_shapes=[pltpu.VMEM((GDN_HEADS, GDN_DH, GDN_DH), F32), pltpu.VMEM((8, 3 * w), F32),
                        pltpu.VMEM((tc, w), F32), pltpu.VMEM((tc, w), F32), pltpu.VMEM((tc, w), F32),
                        pltpu.VMEM((tc, w), F32), pltpu.VMEM((tc, w), F32)],
        compiler_params=_cparams(("arbitrary", "arbitrary")),
        name="gdn",
    )(x, small, z, conv_w, jnp.asarray(ega), jnp.asarray(egb), alog, dtb, on_gain.reshape(1, GDN_DH),
      jnp.asarray(tril.astype(np.float32)))


def even_mixer(h, w_in, fox_fb, fox_qn, fox_kn, gdn_conv, gdn_a_log, gdn_dt_bias, gdn_on):
    cuts = np.cumsum((0,) + EV_SIZES)
    col = lambda i: w_in[:, cuts[i]:cuts[i + 1]]
    wb = lambda a: a.astype(BF16)
    bd = _block_diag_ones(FOX_W, FOX_DH)
    ep = _head_norm_epilogue(FOX_DH)
    qg = (jnp.tile(fox_qn, FOX_HEADS) * FOX_DH ** -0.5).reshape(1, FOX_W)
    kg = jnp.tile(fox_kn, FOX_HEADS).reshape(1, FOX_W)
    fq = proj(h, wb(col(0)), BF16, ep, (bd, qg), name="proj_fq")
    fk = proj(h, wb(col(1)), BF16, ep, (bd, kg), name="proj_fk")
    fv = proj(h, wb(col(2)), BF16, name="proj_fv")
    w_small = jnp.zeros((w_in.shape[0], LANES), F32)
    w_small = w_small.at[:, 0:8].set(col(3)).at[:, 8:12].set(col(5)).at[:, 12:16].set(col(6))
    small = proj(h, wb(w_small), F32, name="proj_ev_small")
    gqkv = proj(h, wb(col(4)), F32, name="proj_gqkv")
    gz = proj(h, wb(col(7)), F32, name="proj_gz")
    cum_t = fox_decay(small, fox_fb)
    o_fox = fox_attention(fq, fk, fv, cum_t)
    o_gdn = gated_delta_net(gqkv, small, gz, gdn_conv, gdn_a_log, gdn_dt_bias, gdn_on)
    return o_fox, o_gdn


def _t5_bucket_np(dist):
    n = np.maximum(dist, 0)
    exact = REL_BUCKETS // 2
    nf = np.maximum(n, 1).astype(np.float32)
    large = exact + (np.log(nf / np.float32(exact)) / np.float32(math.log(REL_MAX_DIST / exact))
                     * np.float32(REL_BUCKETS - exact)).astype(np.int32)
    large = np.minimum(large, REL_BUCKETS - 1)
    return np.where(n < exact, n, large)


def _bias_table(rel_bias, dist, valid):
    shifted = rel_bias - rel_bias[REL_BUCKETS - 1:REL_BUCKETS]
    tb = jnp.moveaxis(shifted[jnp.asarray(_t5_bucket_np(dist))], -1, 0)
    return jnp.where(jnp.asarray(valid)[None], tb, NEG)


def _cmp_kernel(r_ref, pos_ref, w1_ref, w2_ref, kn_ref, o_ref):
    m = r_ref.shape[3]
    half = r_ref.shape[4]
    r = r_ref[0, 0, 0].astype(BF16)
    a = _dot(r, w1_ref[0, :half, :])
    bm = _dot(r, w1_ref[0, half:, :])
    c = _dot(pos_ref[0].astype(BF16), w1_ref[0])
    hid = a + pltpu.roll(bm, m - 1, 0) + c[0:1, :]
    out = _dot(_silu(hid).astype(BF16), w2_ref[0])
    normed = out * lax.rsqrt(jnp.mean(out * out, axis=-1, keepdims=True) + EPS) * kn_ref[...]
    o_ref[0, 0, 0] = jnp.where(pl.program_id(0) == 0, normed, out).astype(o_ref.dtype)


def nsa_compress(kcvc, pos, w1, w2, kn):
    b, t, _ = kcvc.shape
    m = t // CMP_STRIDE
    half = CMP_STRIDE * NSA_DH
    r = kcvc.reshape(b, m, CMP_STRIDE, 2, NSA_KV_HEADS, NSA_DH).transpose(3, 0, 4, 1, 2, 5).reshape(2, b, 2, m, half)
    posf = jnp.zeros((2, 8, 2 * half), F32).at[:, 0].set(pos.reshape(2, 2 * half))
    w2d = jnp.concatenate([w2, w2], axis=-1).astype(BF16)
    knd = jnp.tile(kn, 2).reshape(1, LANES)
    return pl.pallas_call(
        _cmp_kernel,
        grid=(2, b, NSA_KV_HEADS),
        in_specs=[pl.BlockSpec((1, 1, 1, m, half), lambda s, i, k: (s, i, k, 0, 0)),
                  pl.BlockSpec((1, 8, 2 * half), lambda s, i, k: (s, 0, 0)),
                  pl.BlockSpec((1, 2 * half, CMP_HIDDEN), lambda s, i, k: (s, 0, 0)),
                  pl.BlockSpec((1, CMP_HIDDEN, LANES), lambda s, i, k: (s, 0, 0)),
                  _full((1, LANES))],
        out_specs=pl.BlockSpec((1, 1, 1, m, LANES), lambda s, i, k: (s, i, k, 0, 0)),
        out_shape=jax.ShapeDtypeStruct((2, b, NSA_KV_HEADS, m, LANES), BF16),
        compiler_params=_cparams(("arbitrary", "arbitrary", "arbitrary")),
        name="nsa_compress",
    )(r, posf, w1.astype(BF16), w2d, knd)


def _dot_split(a, b):
    hi = a.astype(BF16)
    lo = (a - hi.astype(F32)).astype(BF16)
    return _dot(hi, b) + _dot(lo, b)


def _head_q(q_ref, hh, lo):
    blk = q_ref[0, :, (hh // 2) * LANES:(hh // 2 + 1) * LANES]
    keep = lo if hh % 2 == 0 else jnp.logical_not(lo)
    return jnp.where(keep, blk, jnp.zeros_like(blk))


def _pair_heads(o, lo):
    return jnp.concatenate([jnp.where(lo, o[0], o[1]), jnp.where(lo, o[2], o[3])], axis=1)


def _nsa_sel_kernel(q_ref, kc_ref, vc_ref, ov_ref, bt_ref, o_ref, sel_ref, *, tq, nband, n_slc):
    i = pl.program_id(2)
    ncp = kc_ref.shape[3]
    nsp = sel_ref.shape[3]
    per = tq // CMP_STRIDE
    var = jnp.minimum(i, 1)
    bs = pl.multiple_of(per * jnp.maximum(i - 1, 0), per)
    lo = lax.broadcasted_iota(jnp.int32, (1, LANES), 1) < NSA_DH
    kc = kc_ref[0, 0, 0]
    vc = vc_ref[0, 0, 0]
    kcb = kc_ref[0, 0, 0, pl.ds(bs, nband), :]
    vcb = vc_ref[0, 0, 0, pl.ds(bs, nband), :]
    far_ok = lax.broadcasted_iota(jnp.int32, (1, ncp), 1) < per * (i - 1)
    ps_far = jnp.zeros((tq, ncp), F32)
    ps_band = jnp.zeros((tq, nband), F32)
    outs = []
    for hh in range(NSA_GROUP):
        qh = _head_q(q_ref, hh, lo)
        s_far = jnp.where(far_ok, _dot_nt(qh, kc), NEG)
        s_band = _dot_nt(qh, kcb) + bt_ref[var, hh]
        m = jnp.maximum(jnp.max(s_far, axis=1, keepdims=True), jnp.max(s_band, axis=1, keepdims=True))
        m = jnp.where(m < 0.5 * NEG, 0.0, m)
        p_far = jnp.exp(s_far - m)
        p_band = jnp.exp(s_band - m)
        l = jnp.sum(p_far, axis=1, keepdims=True) + jnp.sum(p_band, axis=1, keepdims=True)
        inv = 1.0 / jnp.where(l == 0.0, 1.0, l)
        outs.append((_dot(p_far.astype(BF16), vc) + _dot(p_band.astype(BF16), vcb)) * inv)
        ps_far = ps_far + p_far * inv
        ps_band = ps_band + p_band * inv
    o_ref[0] = _pair_heads(outs, lo).astype(o_ref.dtype)

    imp = _dot_split(ps_far, ov_ref[...]) + _dot_split(ps_band, ov_ref[pl.ds(bs, nband), :])
    blk = lax.broadcasted_iota(jnp.int32, (1, nsp), 1)
    blk_f = blk.astype(F32)
    qpos = i * tq + lax.broadcasted_iota(jnp.int32, (tq, 1), 0)
    cur = lax.shift_right_logical(qpos, int(math.log2(SLC_LEN)))
    forced = (blk == 0) | (blk == cur) | (blk == cur - 1)
    work = jnp.where(forced, FORCE_SCORE, jnp.where(blk <= cur, imp, NEG))
    work = jnp.where(blk < n_slc, work, -jnp.inf)
    sel = jnp.zeros((tq, nsp), F32)
    for _ in range(min(SLC_TOPK, n_slc)):
        m = jnp.max(work, axis=1, keepdims=True)
        first = jnp.min(jnp.where(work == m, blk_f, float(nsp)), axis=1, keepdims=True)
        pick = blk_f == first
        sel = jnp.where(pick, 1.0, sel)
        work = jnp.where(pick, -jnp.inf, work)
    sel_ref[0, 0] = sel.astype(sel_ref.dtype)


def nsa_select(q, cmp_kv, rel_bias, tq=256):
    b, t, _ = q.shape
    ncp = t // CMP_STRIDE
    n_cmp = ncp - 1
    n_slc = t // SLC_LEN
    nsp = max(LANES, n_slc)
    per = tq // CMP_STRIDE
    nband = 2 * per
    n = np.arange(ncp)[:, None]
    s = np.arange(nsp)[None, :]
    ov = ((CMP_STRIDE * n < SLC_LEN * s + SLC_LEN) & (CMP_STRIDE * n + CMP_LEN > SLC_LEN * s)
          & (n < n_cmp) & (s < n_slc)).astype(np.float32)
    qi = np.arange(tq)[:, None]
    nj = np.arange(nband)[None, :]
    end = CMP_STRIDE * nj + CMP_LEN - 1
    dist = np.stack([qi - end, tq + qi - end])
    bt = _bias_table(rel_bias, dist, dist >= 0)
    bt = bt.reshape(NSA_KV_HEADS, NSA_GROUP, 2, tq, nband).transpose(0, 2, 1, 3, 4)
    bt = bt.reshape(NSA_KV_HEADS * 2, NSA_GROUP, tq, nband)
    gw = NSA_GROUP * NSA_DH
    return pl.pallas_call(
        functools.partial(_nsa_sel_kernel, tq=tq, nband=nband, n_slc=n_slc),
        grid=(b, NSA_KV_HEADS, t // tq),
        in_specs=[pl.BlockSpec((1, tq, gw), lambda bi, k, i: (bi, i, k)),
                  pl.BlockSpec((1, 1, 1, ncp, LANES), lambda bi, k, i: (0, bi, k, 0, 0)),
                  pl.BlockSpec((1, 1, 1, ncp, LANES), lambda bi, k, i: (1, bi, k, 0, 0)),
                  _full((ncp, nsp)),
                  pl.BlockSpec((2, NSA_GROUP, tq, nband), lambda bi, k, i: (k, 0, 0, 0))],
        out_specs=[pl.BlockSpec((1, tq, gw), lambda bi, k, i: (bi, i, k)),
                   pl.BlockSpec((1, 1, tq, nsp), lambda bi, k, i: (bi, k, i, 0))],
        out_shape=[jax.ShapeDtypeStruct((b, t, NSA_W), BF16),
                   jax.ShapeDtypeStruct((b, NSA_KV_HEADS, t, nsp), BF16)],
        compiler_params=_cparams(("arbitrary", "arbitrary", "arbitrary")),
        name="nsa_select",
    )(q, cmp_kv, cmp_kv, jnp.asarray(ov, dtype=BF16), bt)


def _nsa_main_kernel(q_ref, ks_ref, vs_ref, kw0_ref, kw1_ref, kw2_ref, vw0_ref, vw1_ref, vw2_ref, sel_ref,
                     ocmp_ref, gate_ref, tb_ref, wm_ref, eg_ref, o_ref, *, tq):
    i = pl.program_id(2)
    nsp = sel_ref.shape[3]
    g = NSA_GROUP
    lo = lax.broadcasted_iota(jnp.int32, (1, LANES), 1) < NSA_DH
    qst = jnp.concatenate([_head_q(q_ref, hh, lo) for hh in range(g)], axis=0)
    sel = sel_ref[0, 0]
    blk_row = lax.broadcasted_iota(jnp.int32, (nsp, 1), 0)
    col_blk = lax.shift_right_logical(lax.broadcasted_iota(jnp.int32, (1, tq), 1), int(math.log2(SLC_LEN)))
    per = tq // SLC_LEN
    causal = (lax.broadcasted_iota(jnp.int32, (tq, tq), 1) <= lax.broadcasted_iota(jnp.int32, (tq, tq), 0))

    def sel_step(jt, carry, near):
        m, l, acc = carry
        start = pl.multiple_of(jnp.maximum(jt, 0) * tq, tq)
        kt = ks_ref[0, pl.ds(start, tq), :]
        vt = vs_ref[0, pl.ds(start, tq), :]
        expand = jnp.where(blk_row - per * jt == col_blk, 1.0, 0.0).astype(BF16)
        keep = _dot(sel, expand)
        madd = (keep - 1.0) * (-NEG)
        if near == 2:
            madd = jnp.where(causal, madd, NEG)
        s = _dot_nt(qst, kt).reshape(g, tq, tq) + madd[None]
        if near is not None:
            s = s + tb_ref[:, :, near * tq:(near + 1) * tq]
        s = s.reshape(g * tq, tq)
        m_new = jnp.maximum(m, jnp.max(s, axis=1, keepdims=True))
        alpha = jnp.exp(m - m_new)
        p = jnp.exp(s - m_new)
        l = alpha * l + jnp.sum(p, axis=1, keepdims=True)
        acc = alpha * acc + _dot(p.astype(BF16), vt)
        return m_new, l, acc

    carry = (jnp.full((g * tq, 1), NEG, F32), jnp.zeros((g * tq, 1), F32), jnp.zeros((g * tq, LANES), F32))
    carry = lax.fori_loop(0, jnp.maximum(i - 2, 0), lambda j, c: sel_step(j, c, None), carry)
    for near in range(3):
        carry = sel_step(i - 2 + near, carry, near)
    o_slc = (carry[2] / carry[1]).reshape(g, tq, LANES)

    var = jnp.minimum(i, 2)
    kws = (kw0_ref, kw1_ref, kw2_ref)
    vws = (vw0_ref, vw1_ref, vw2_ref)
    sw = []
    for near in range(3):
        s = _dot_nt(qst, kws[near][0]).reshape(g, tq, tq)
        s = s + tb_ref[:, :, near * tq:(near + 1) * tq] + wm_ref[var, :, near * tq:(near + 1) * tq][None]
        sw.append(s.reshape(g * tq, tq))
    m = jnp.maximum(jnp.maximum(jnp.max(sw[0], axis=1, keepdims=True), jnp.max(sw[1], axis=1, keepdims=True)),
                    jnp.max(sw[2], axis=1, keepdims=True))
    l = jnp.zeros((g * tq, 1), F32)
    acc = jnp.zeros((g * tq, LANES), F32)
    for near in range(3):
        p = jnp.exp(sw[near] - m)
        l = l + jnp.sum(p, axis=1, keepdims=True)
        acc = acc + _dot(p.astype(BF16), vws[near][0])
    o_win = (acc / l).reshape(g, tq, LANES)

    gates = _dot_hi(_sigmoid(gate_ref[0]), eg_ref[0])
    gw = g * NSA_DH
    out = (gates[:, 0:gw] * ocmp_ref[0].astype(F32)
           + gates[:, gw:2 * gw] * _pair_heads([o_slc[hh] for hh in range(g)], lo)
           + gates[:, 2 * gw:3 * gw] * _pair_heads([o_win[hh] for hh in range(g)], lo))
    o_ref[0] = out.astype(o_ref.dtype)


def nsa_main(q, ksw, vsw, sel, o_cmp, small, rel_bias, tq=256):
    b, t, _ = q.shape
    nsp = sel.shape[-1]
    g = NSA_GROUP
    gw = g * NSA_DH
    qi = np.arange(tq)[:, None]
    c = np.arange(3 * tq)[None, :]
    dist = qi + 2 * tq - c
    tb = _bias_table(rel_bias, dist, np.ones_like(dist, bool))
    wm = np.zeros((3, tq, 3 * tq), np.float32)
    for var in range(3):
        exists = c >= tq * (2 - var)
        wm[var] = np.where((dist >= 0) & (dist < WINDOW) & exists, 0.0, NEG)
    eg = np.zeros((NSA_KV_HEADS, LANES, 3 * gw), np.float32)
    for k in range(NSA_KV_HEADS):
        for hh in range(g):
            for br in range(3):
                eg[k, (k * g + hh) * 3 + br, br * gw + hh * NSA_DH:br * gw + (hh + 1) * NSA_DH] = 1.0
    near = lambda off, col: pl.BlockSpec(
        (1, tq, LANES), lambda bi, k, i: (bi, jnp.maximum(i - off, 0), col + k))
    return pl.pallas_call(
        functools.partial(_nsa_main_kernel, tq=tq),
        grid=(b, NSA_KV_HEADS, t // tq),
        in_specs=[pl.BlockSpec((1, tq, gw), lambda bi, k, i: (bi, i, k)),
                  pl.BlockSpec((1, t, LANES), lambda bi, k, i: (bi, 0, k)),
                  pl.BlockSpec((1, t, LANES), lambda bi, k, i: (bi, 0, k)),
                  near(2, 2), near(1, 2), near(0, 2), near(2, 2), near(1, 2), near(0, 2),
                  pl.BlockSpec((1, 1, tq, nsp), lambda bi, k, i: (bi, k, i, 0)),
                  pl.BlockSpec((1, tq, gw), lambda bi, k, i: (bi, i, k)),
                  pl.BlockSpec((1, tq, LANES), lambda bi, k, i: (bi, i, 0)),
                  pl.BlockSpec((g, tq, 3 * tq), lambda bi, k, i: (k, 0, 0)),
                  _full((3, tq, 3 * tq)),
                  pl.BlockSpec((1, LANES, 3 * gw), lambda bi, k, i: (k, 0, 0))],
        out_specs=pl.BlockSpec((1, tq, gw), lambda bi, k, i: (bi, i, k)),
        out_shape=jax.ShapeDtypeStruct((b, t, NSA_W), BF16),
        compiler_params=_cparams(("arbitrary", "arbitrary", "arbitrary")),
        name="nsa_main",
    )(q, ksw, vsw, ksw, ksw, ksw, vsw, vsw, vsw, sel, o_cmp, small, tb, jnp.asarray(wm), jnp.asarray(eg))


def _gla_kernel(qk_ref, v_ref, r_ref, sm_ref, wg_ref, bg_ref, on_ref, tril_ref, o_ref, s_scr, q_scr, k_scr, kd_scr,
                *, tc):
    c = GLA_CHUNK

    @pl.when(pl.program_id(1) == 0)
    def _():
        s_scr[...] = jnp.zeros_like(s_scr)

    kw = GLA_KW
    log_a = _log_sigmoid(_dot_hi(sm_ref[0], wg_ref[...]) + bg_ref[...]) * (1.0 / GLA_TAU)
    gcum = _dot_hi(tril_ref[...], log_a)
    q = qk_ref[0, :, 0:kw] * (GLA_DK ** -0.5)
    k = qk_ref[0, :, kw:2 * kw]
    q_scr[...] = q * jnp.exp(gcum)
    k_scr[...] = k * jnp.exp(-gcum)
    ri = lax.broadcasted_iota(jnp.int32, (c, c), 0)
    ci = lax.broadcasted_iota(jnp.int32, (c, c), 1)
    causal = ci <= ri
    lo = lax.broadcasted_iota(jnp.int32, (1, LANES), 1) < GLA_DK

    def chunk(n, _):
        sl = pl.ds(pl.multiple_of(n * c, c), c)
        for p in range(GLA_HEADS // 2):
            pl_ = slice(p * LANES, (p + 1) * LANES)
            qd = q_scr[sl, pl_]
            ki = k_scr[sl, pl_]
            kd = kd_scr[sl, pl_]
            for hh in range(2):
                h = 2 * p + hh
                keep = lo if hh == 0 else jnp.logical_not(lo)
                qm = jnp.where(keep, qd, 0.0).astype(BF16)
                v = v_ref[0, sl, h * GLA_DV:(h + 1) * GLA_DV]
                attn = jnp.where(causal, _dot_nt(qm, ki.astype(BF16)), 0.0)
                st = s_scr[h]
                o = _dot(attn.astype(BF16), v.astype(BF16)) + _dot_nt(qm, st.astype(BF16))
                on = o * lax.rsqrt(jnp.mean(o * o, axis=-1, keepdims=True) + EPS) * on_ref[...]
                o_ref[0, sl, h * GLA_DV:(h + 1) * GLA_DV] = (
                    on * _silu(r_ref[0, sl, h * GLA_DV:(h + 1) * GLA_DV])).astype(o_ref.dtype)
                s_scr[h] = st * kd_scr[pl.ds(tc + n * 8, 1), pl_] + _dot_tn(v.astype(BF16), kd.astype(BF16))
        return 0

    for n in range(tc // c):
        gl = gcum[n * c + c - 1:n * c + c, :]
        kd_scr[n * c:(n + 1) * c, :] = k[n * c:(n + 1) * c, :] * jnp.exp(gl - gcum[n * c:(n + 1) * c, :])
        kd_scr[tc + n * 8:tc + n * 8 + 8, :] = jnp.broadcast_to(jnp.exp(gl), (8, kw))
    lax.fori_loop(0, tc // c, chunk, 0)


def gated_linear_attention(qkvr, small, wg_up, bg, on_gain, tc=512):
    b, t, _ = qkvr.shape
    wg = jnp.zeros((LANES, GLA_KW), F32).at[3 * NSA_HEADS:3 * NSA_HEADS + GLA_GATE_RANK].set(wg_up)
    idx = np.arange(tc)
    tril = ((idx[:, None] >= idx[None, :]) & (idx[:, None] // GLA_CHUNK == idx[None, :] // GLA_CHUNK))
    nchunk = tc // GLA_CHUNK
    return pl.pallas_call(
        functools.partial(_gla_kernel, tc=tc),
        grid=(b, t // tc),
        in_specs=[pl.BlockSpec((1, tc, 2 * GLA_KW), lambda i, j: (i, j, 0)),
                  pl.BlockSpec((1, tc, GLA_W), lambda i, j: (i, j, 1)),
                  pl.BlockSpec((1, tc, GLA_W), lambda i, j: (i, j, 2)),
                  pl.BlockSpec((1, tc, LANES), lambda i, j: (i, j, 0)),
                  _full((LANES, GLA_KW)), _full((1, GLA_KW)), _full((1, GLA_DV)), _full((tc, tc))],
        out_specs=pl.BlockSpec((1, tc, GLA_W), lambda i, j: (i, j, 0)),
        out_shape=jax.ShapeDtypeStruct((b, t, GLA_W), BF16),
        scratch_shapes=[pltpu.VMEM((GLA_HEADS, GLA_DV, LANES), F32), pltpu.VMEM((tc, GLA_KW), F32),
                        pltpu.VMEM((tc, GLA_KW), F32), pltpu.VMEM((tc + 8 * nchunk, GLA_KW), F32)],
        compiler_params=_cparams(("arbitrary", "arbitrary")),
        name="gla",
    )(qkvr, qkvr, qkvr, small, wg, bg.reshape(1, GLA_KW), on_gain.reshape(1, GLA_DV),
      jnp.asarray(tril.astype(np.float32)))


def odd_mixer(h, w_in, nsa_qn, nsa_kn, nsa_pos, nsa_cmp_w1, nsa_cmp_w2, gla_wg_up, gla_bg, gla_on, rel_bias):
    cuts = np.cumsum((0,) + OD_SIZES)
    col = lambda i: w_in[:, cuts[i]:cuts[i + 1]]
    wb = lambda a: a.astype(BF16)
    dup = lambda a: jnp.concatenate([a[:, :NSA_DH], a[:, :NSA_DH], a[:, NSA_DH:], a[:, NSA_DH:]], axis=1)
    ep = _head_norm_epilogue(NSA_DH)
    bd = _block_diag_ones(NSA_W, NSA_DH)
    qg = (jnp.tile(nsa_qn, NSA_HEADS) * NSA_DH ** -0.5).reshape(1, NSA_W)
    kg = jnp.tile(nsa_kn, NSA_HEADS).reshape(1, NSA_W)
    nq = proj(h, wb(col(0)), BF16, ep, (bd, qg), name="proj_nq")
    kcvc = proj(h, wb(jnp.concatenate([col(1), col(2)], axis=1)), F32, name="proj_kcvc")
    ksw = proj(h, wb(jnp.concatenate([dup(col(3)), dup(col(5))], axis=1)), BF16, ep, (bd, kg), name="proj_ksw")
    vsw = proj(h, wb(jnp.concatenate([dup(col(4)), dup(col(6))], axis=1)), BF16, name="proj_vsw")
    w_small = jnp.zeros((w_in.shape[0], LANES), F32)
    w_small = w_small.at[:, 0:24].set(col(7)).at[:, 24:40].set(col(11))
    small = proj(h, wb(w_small), F32, name="proj_od_small")
    qkvr = proj(h, wb(jnp.concatenate([col(8), col(9), col(10), col(12)], axis=1)), F32, name="proj_gla")
    cmp_kv = nsa_compress(kcvc, nsa_pos, nsa_cmp_w1, nsa_cmp_w2, nsa_kn)
    o_cmp, sel = nsa_select(nq, cmp_kv, rel_bias)
    o_nsa = nsa_main(nq, ksw, vsw, sel, o_cmp, small, rel_bias)
    o_gla = gated_linear_attention(qkvr, small, gla_wg_up, gla_bg, gla_on)
    return o_nsa, o_gla


MOE_TM = 256
MOE_ROWS = 256


def _first_index(mask_val, idx, big, axis):
    return jnp.min(jnp.where(mask_val, idx, big), axis=axis, keepdims=True)


def _route_kernel(h_ref, rt_ref, b_ref, up_ref, eid_ref, rank_ref, w_ref, cnt_ref, run):
    tm = h_ref.shape[0]
    ne = N_EXPERTS
    gsz = ne // N_GROUPS

    @pl.when(pl.program_id(0) == 0)
    def _():
        run[...] = jnp.zeros_like(run)

    scores = _sigmoid(_dot_nt(rt_ref[...], h_ref[...], HI))
    biased = scores + b_ref[...]
    b3 = biased.reshape(N_GROUPS, gsz, tm)
    i3 = lax.broadcasted_iota(jnp.int32, (1, gsz, 1), 1).astype(F32)
    m1 = jnp.max(b3, axis=1, keepdims=True)
    f1 = _first_index(b3 == m1, i3, float(gsz), 1)
    m2 = jnp.max(jnp.where(i3 == f1, -jnp.inf, b3), axis=1, keepdims=True)
    gs = (m1 + m2).reshape(N_GROUPS, tm)
    gidx = lax.broadcasted_iota(jnp.int32, (N_GROUPS, 1), 0).astype(F32)
    gmask = jnp.zeros((N_GROUPS, tm), F32)
    for _ in range(TOPK_GROUPS):
        m = jnp.max(gs, axis=0, keepdims=True)
        pick = gidx == _first_index(gs == m, gidx, float(N_GROUPS), 0)
        gmask = jnp.where(pick, 1.0, gmask)
        gs = jnp.where(pick, -jnp.inf, gs)
    emask = jnp.broadcast_to(gmask.reshape(N_GROUPS, 1, tm), (N_GROUPS, gsz, tm)).reshape(ne, tm)
    work = jnp.where(emask > 0.5, biased, -jnp.inf)
    eidx = lax.broadcasted_iota(jnp.int32, (ne, 1), 0).astype(F32)
    picks, eids, ws = [], [], []
    for _ in range(TOP_K):
        m = jnp.max(work, axis=0, keepdims=True)
        first = _first_index(work == m, eidx, float(ne), 0)
        pick = eidx == first
        picks.append(pick)
        eids.append(first)
        ws.append(jnp.sum(jnp.where(pick, scores, 0.0), axis=0, keepdims=True))
        work = jnp.where(pick, -jnp.inf, work)
    wsum = ws[0]
    for k in range(1, TOP_K):
        wsum = wsum + ws[k]
    chosen = jnp.zeros((ne, tm), F32)
    for pick in picks:
        chosen = jnp.where(pick, 1.0, chosen)
    pos = run[...] + _dot(chosen.astype(BF16), up_ref[...])
    run[...] = run[...] + jnp.sum(chosen, axis=1, keepdims=True)
    cnt_ref[...] = run[...]
    row = lax.broadcasted_iota(jnp.int32, (8, 1), 0)
    eid_o = jnp.zeros((8, tm), F32)
    rank_o = jnp.zeros((8, tm), F32)
    w_o = jnp.zeros((LANES, tm), F32)
    rowl = lax.broadcasted_iota(jnp.int32, (LANES, 1), 0)
    for k in range(TOP_K):
        rk = jnp.sum(jnp.where(picks[k], pos, 0.0), axis=0, keepdims=True)
        eid_o = jnp.where(row == k, eids[k], eid_o)
        rank_o = jnp.where(row == k, rk, rank_o)
        w_o = jnp.where(rowl == k, ws[k] / wsum * ROUTE_SCALE, w_o)
    eid_ref[0] = eid_o.astype(jnp.int32)
    rank_ref[0] = rank_o.astype(jnp.int32)
    w_ref[...] = w_o.T


def moe_route(h2, router, e_bias, tm=MOE_TM):
    nt, d = h2.shape
    ne = N_EXPERTS
    up = jnp.asarray(np.triu(np.ones((tm, tm), np.float32), 1), dtype=BF16)
    nb = nt // tm
    return pl.pallas_call(
        _route_kernel,
        grid=(nb,),
        in_specs=[pl.BlockSpec((tm, d), lambda i: (i, 0)), _full((ne, d)), _full((ne, 1)), _full((tm, tm))],
        out_specs=[pl.BlockSpec((1, 8, tm), lambda i: (i, 0, 0)),
                   pl.BlockSpec((1, 8, tm), lambda i: (i, 0, 0)),
                   pl.BlockSpec((tm, LANES), lambda i: (i, 0)),
                   _full((ne, 1))],
        out_shape=[jax.ShapeDtypeStruct((nb, 8, tm), jnp.int32), jax.ShapeDtypeStruct((nb, 8, tm), jnp.int32),
                   jax.ShapeDtypeStruct((nt, LANES), F32), jax.ShapeDtypeStruct((ne, 1), F32)],
        scratch_shapes=[pltpu.VMEM((ne, 1), F32)],
        compiler_params=_cparams(("arbitrary",)),
        name="moe_route",
    )(h2, router.T, e_bias.reshape(ne, 1), up)


def _dispatch_kernel(starts_ref, eid_ref, rank_ref, h_ref, xs_ref, sem):
    tm = h_ref.shape[0]

    def copy(t, k):
        dest = starts_ref[eid_ref[0, k, t]] + rank_ref[0, k, t]
        return pltpu.make_async_copy(h_ref.at[pl.ds(t, 1), :], xs_ref.at[pl.ds(dest, 1), :], sem)

    def issue(t, _):
        for k in range(TOP_K):
            copy(t, k).start()
        return 0

    def drain(t, _):
        for k in range(TOP_K):
            copy(t, k).wait()
        return 0

    lax.fori_loop(0, tm, issue, 0)
    lax.fori_loop(0, tm, drain, 0)


def moe_dispatch(h2, eid, rank, starts, tm=MOE_TM):
    nt, d = h2.shape
    smem = lambda: pl.BlockSpec((1, 8, tm), lambda i, s: (i, 0, 0), memory_space=pltpu.SMEM)
    return pl.pallas_call(
        _dispatch_kernel,
        grid_spec=pltpu.PrefetchScalarGridSpec(
            num_scalar_prefetch=1,
            grid=(nt // tm,),
            in_specs=[smem(), smem(), pl.BlockSpec((tm, d), lambda i, s: (i, 0))],
            out_specs=pl.BlockSpec(memory_space=pl.ANY),
            scratch_shapes=[pltpu.SemaphoreType.DMA(())]),
        out_shape=jax.ShapeDtypeStruct((nt * TOP_K, d), F32),
        compiler_params=_cparams(("arbitrary",)),
        name="moe_dispatch",
    )(starts, eid, rank, h2)


def _ffn_kernel(blk_ref, exp_ref, lo_ref, hi_ref, first_ref, valid_ref, x_ref, wg_ref, wu_ref, wd_ref, o_ref):
    i = pl.program_id(0)
    rows = x_ref.shape[0]

    @pl.when(valid_ref[i] == 1)
    def _():
        x = x_ref[...].astype(BF16)
        a = _dot(x, wg_ref[0].astype(BF16))
        u = _dot(x, wu_ref[0].astype(BF16))
        y = _dot((_silu(a) * u).astype(BF16), wd_ref[0].astype(BF16))
        r = blk_ref[i] * rows + lax.broadcasted_iota(jnp.int32, (rows, 1), 0)
        y = jnp.where((r >= lo_ref[i]) & (r < hi_ref[i]), y, 0.0)

        @pl.when(first_ref[i] == 1)
        def _():
            o_ref[...] = y

        @pl.when(first_ref[i] == 0)
        def _():
            o_ref[...] = o_ref[...] + y


def _ffn_items(counts, n_rows, rows):
    ne = N_EXPERTS
    nblk = n_rows // rows
    n_items = nblk + ne - 1
    ends = jnp.cumsum(counts)
    starts = ends - counts
    first_blk = starts // rows
    last_blk = jnp.maximum(ends - 1, 0) // rows
    per_e = jnp.where(counts > 0, last_blk - first_blk + 1, 0)
    item_end = jnp.cumsum(per_e)
    item_start = item_end - per_e
    total = item_end[-1]
    i = jnp.arange(n_items, dtype=jnp.int32)
    ic = jnp.minimum(i, total - 1)
    e = jnp.searchsorted(item_end, ic, side="right").astype(jnp.int32)
    blk = first_blk[e] + ic - item_start[e]
    lo = jnp.maximum(starts[e], blk * rows)
    hi = jnp.minimum(ends[e], (blk + 1) * rows)
    valid = (i < total).astype(jnp.int32)
    first = (lo == blk * rows).astype(jnp.int32)
    return starts, (blk.astype(jnp.int32), e, lo.astype(jnp.int32), hi.astype(jnp.int32), first, valid)


def moe_ffn_sorted(xs, items, wg, wu, wd, rows=MOE_ROWS):
    n_rows, d = xs.shape
    n_items = items[0].shape[0]
    de = wg.shape[-1]
    return pl.pallas_call(
        _ffn_kernel,
        grid_spec=pltpu.PrefetchScalarGridSpec(
            num_scalar_prefetch=6,
            grid=(n_items,),
            in_specs=[pl.BlockSpec((rows, d), lambda i, blk, e, *_: (blk[i], 0)),
                      pl.BlockSpec((1, d, de), lambda i, blk, e, *_: (e[i], 0, 0)),
                      pl.BlockSpec((1, d, de), lambda i, blk, e, *_: (e[i], 0, 0)),
                      pl.BlockSpec((1, de, d), lambda i, blk, e, *_: (e[i], 0, 0))],
            out_specs=pl.BlockSpec((rows, d), lambda i, blk, e, *_: (blk[i], 0))),
        out_shape=jax.ShapeDtypeStruct((n_rows, d), F32),
        compiler_params=_cparams(("arbitrary",)),
        name="moe_ffn",
    )(*items, xs, wg, wu, wd)


def _combine_kernel(starts_ref, eid_ref, rank_ref, ys_ref, w_ref, h_ref, x_ref, g_ref, sg_ref, su_ref, sd_ref,
                    o_ref, buf, sem):
    tm = h_ref.shape[0]

    def copy(t, k):
        src = starts_ref[eid_ref[0, k, t]] + rank_ref[0, k, t]
        return pltpu.make_async_copy(ys_ref.at[pl.ds(src, 1), :], buf.at[k, pl.ds(t, 1), :], sem)

    def issue(t, _):
        for k in range(TOP_K):
            copy(t, k).start()
        return 0

    def drain(t, _):
        for k in range(TOP_K):
            copy(t, k).wait()
        return 0

    lax.fori_loop(0, tm, issue, 0)
    hb = h_ref[...].astype(BF16)
    y = _dot((_silu(_dot(hb, sg_ref[...])) * _dot(hb, su_ref[...])).astype(BF16), sd_ref[...])
    lax.fori_loop(0, tm, drain, 0)
    w = w_ref[...]
    for k in range(TOP_K):
        y = y + w[:, k:k + 1] * buf[k]
    o_ref[...] = x_ref[...] + g_ref[0] * y


def moe_combine(ys, eid, rank, starts, w, h2, x2, gate, sg, su, sd, seq, tm=MOE_TM):
    nt, d = h2.shape
    ds_ = sg.shape[-1]
    per_b = seq // tm
    smem = lambda: pl.BlockSpec((1, 8, tm), lambda i, s: (i, 0, 0), memory_space=pltpu.SMEM)
    tile = lambda: pl.BlockSpec((tm, d), lambda i, s: (i, 0))
    return pl.pallas_call(
        _combine_kernel,
        grid_spec=pltpu.PrefetchScalarGridSpec(
            num_scalar_prefetch=1,
            grid=(nt // tm,),
            in_specs=[smem(), smem(), pl.BlockSpec(memory_space=pl.ANY),
                      pl.BlockSpec((tm, LANES), lambda i, s: (i, 0)), tile(), tile(),
                      pl.BlockSpec((1, 1, d), lambda i, s: (i // per_b, 0, 0)),
                      pl.BlockSpec((d, ds_), lambda i, s: (0, 0)), pl.BlockSpec((d, ds_), lambda i, s: (0, 0)),
                      pl.BlockSpec((ds_, d), lambda i, s: (0, 0))],
            out_specs=tile(),
            scratch_shapes=[pltpu.VMEM((TOP_K, tm, d), F32), pltpu.SemaphoreType.DMA(())]),
        out_shape=jax.ShapeDtypeStruct((nt, d), F32),
        compiler_params=_cparams(("arbitrary",)),
        name="moe_combine",
    )(starts, eid, rank, ys, w, h2, x2, gate, sg.astype(BF16), su.astype(BF16), sd.astype(BF16))


def moe_layer(x, g_norm, sc, sh, gate, router, e_bias, wg, wu, wd, sg, su, sd):
    b, t, d = x.shape
    nt = b * t
    h = ln_mod(x, g_norm, sc, sh, F32)
    h2 = h.reshape(nt, d)
    eid, rank, w, counts = moe_route(h2, router, e_bias)
    starts, items = _ffn_items(counts.reshape(-1).astype(jnp.int32), nt * TOP_K, MOE_ROWS)
    starts = starts.astype(jnp.int32)
    xs = moe_dispatch(h2, eid, rank, starts)
    ys = moe_ffn_sorted(xs, items, wg, wu, wd)
    out = moe_combine(ys, eid, rank, starts, w, h2, x.reshape(nt, d), gate.reshape(b, 1, d), sg, su, sd, t)
    return out.reshape(b, t, d)


def kernel(x, c, ada_w, ada_b, norm_mix, norm_ffn, rel_bias, ev_w_in, ev_w_out, fox_fb, fox_qn, fox_kn, gdn_conv, gdn_a_log, gdn_dt_bias, gdn_on, od_w_in, od_w_out, nsa_qn, nsa_kn, nsa_pos, nsa_cmp_w1, nsa_cmp_w2, gla_wg_up, gla_bg, gla_on, moe_router, moe_bias, moe_wg, moe_wu, moe_wd, sh_wg, sh_wu, sh_wd):
    d = x.shape[-1]
    depth = ada_w.shape[0]
    mod = adaln(c, ada_w, ada_b)
    for layer in range(depth):
        sh1, sc1, g1, sh2, sc2, g2 = [mod[layer, :, i * d:(i + 1) * d] for i in range(6)]
        h = ln_mod(x, norm_mix[layer], sc1, sh1, BF16)
        j = layer // 2
        if layer % 2 == 0:
            y1, y2 = even_mixer(h, ev_w_in[j], fox_fb[j], fox_qn[j], fox_kn[j], gdn_conv[j], gdn_a_log[j],
                                gdn_dt_bias[j], gdn_on[j])
            w_out = ev_w_out[j]
        else:
            y1, y2 = odd_mixer(h, od_w_in[j], nsa_qn[j], nsa_kn[j], nsa_pos[j], nsa_cmp_w1[j], nsa_cmp_w2[j],
                               gla_wg_up[j], gla_bg[j], gla_on[j], rel_bias)
            w_out = od_w_out[j]
        x = out_proj(y1, y2, w_out, x, g1)
        x = moe_layer(x, norm_ffn[layer], sc2, sh2, g2, moe_router[layer], moe_bias[layer], moe_wg[layer],
                      moe_wu[layer], moe_wd[layer], sh_wg[layer], sh_wu[layer], sh_wd[layer])
    return x
```

```python
import functools
import math

import numpy as np
import jax
import jax.numpy as jnp
from jax import lax
from jax.experimental import pallas as pl
from jax.experimental.pallas import tpu as pltpu

F32 = jnp.float32
BF16 = jnp.bfloat16
HI = lax.Precision.HIGHEST

EPS = 1e-6
LOG2E = math.log2(math.e)
NEG = -1e30
FORCE_SCORE = 1e9

FOX_HEADS, FOX_DH = 8, 64
GDN_HEADS, GDN_DH, GDN_CONV, GDN_CHUNK = 4, 128, 4, 64
NSA_HEADS, NSA_KV_HEADS, NSA_DH = 8, 2, 64
NSA_GROUP = NSA_HEADS // NSA_KV_HEADS
CMP_LEN, CMP_STRIDE, CMP_HIDDEN = 32, 16, 256
SLC_LEN, SLC_TOPK, WINDOW = 64, 16, 512
GLA_HEADS, GLA_DK, GLA_DV, GLA_GATE_RANK, GLA_TAU, GLA_CHUNK = 4, 64, 128, 16, 16.0, 64
REL_BUCKETS, REL_MAX_DIST = 32, 128
N_EXPERTS, TOP_K, D_EXPERT, D_SHARED = 64, 6, 256, 256
N_GROUPS, TOPK_GROUPS, ROUTE_SCALE = 8, 4, 2.5

FOX_W = FOX_HEADS * FOX_DH
GDN_W = GDN_HEADS * GDN_DH
NSA_W = NSA_HEADS * NSA_DH
NSA_KV_W = NSA_KV_HEADS * NSA_DH
GLA_KW = GLA_HEADS * GLA_DK
GLA_W = GLA_HEADS * GLA_DV
EV_SIZES = (FOX_W, FOX_W, FOX_W, FOX_HEADS, 3 * GDN_W, GDN_HEADS, GDN_HEADS, GDN_W)
OD_SIZES = (NSA_W,) + (NSA_KV_W,) * 6 + (3 * NSA_HEADS, GLA_KW, GLA_KW, GLA_W, GLA_GATE_RANK, GLA_W)

LANES = 128
VMEM_LIMIT = 56 * 1024 * 1024


def _cparams(sem, flags=None):
    return pltpu.CompilerParams(dimension_semantics=sem, vmem_limit_bytes=VMEM_LIMIT, flags=flags)


def _full(shape):
    n = len(shape)
    return pl.BlockSpec(shape, lambda *_: (0,) * n)


def _dot(a, b):
    return jnp.dot(a, b, preferred_element_type=F32)


def _dot_hi(a, b):
    return jnp.dot(a, b, precision=HI, preferred_element_type=F32)


def _dot_nt(a, b, precision=None):
    return lax.dot_general(a, b, (((1,), (1,)), ((), ())), precision=precision, preferred_element_type=F32)


def _dot_tn(a, b, precision=None):
    return lax.dot_general(a, b, (((0,), (0,)), ((), ())), precision=precision, preferred_element_type=F32)


def _sigmoid(x):
    return 1.0 / (1.0 + jnp.exp(-x))


def _silu(x):
    return x * _sigmoid(x)


def _softplus(x):
    return jnp.maximum(x, 0.0) + jnp.log(1.0 + jnp.exp(-jnp.abs(x)))


def _log_sigmoid(x):
    return -_softplus(-x)


def _adaln_kernel(c_ref, w_ref, b_ref, o_ref):
    c = c_ref[...]
    o_ref[0] = _dot_hi(_silu(c), w_ref[0]) + b_ref[0]


def adaln(c, ada_w, ada_b):
    depth, d, n = ada_w.shape
    b = c.shape[0]
    cp = jnp.zeros((8, d), F32).at[:b].set(c)
    tn = 1536
    out = pl.pallas_call(
        _adaln_kernel,
        grid=(depth, n // tn),
        in_specs=[_full((8, d)),
                  pl.BlockSpec((1, d, tn), lambda l, j: (l, 0, j)),
                  pl.BlockSpec((1, 1, tn), lambda l, j: (l, 0, j))],
        out_specs=pl.BlockSpec((1, 8, tn), lambda l, j: (l, 0, j)),
        out_shape=jax.ShapeDtypeStruct((depth, 8, n), F32),
        compiler_params=_cparams(("arbitrary", "arbitrary")),
        name="adaln",
    )(cp, ada_w, ada_b.reshape(depth, 1, n))
    return out[:, :b]


def _ln_kernel(x_ref, g_ref, sc_ref, sh_ref, o_ref):
    x = x_ref[0]
    y = x * lax.rsqrt(jnp.mean(x * x, axis=-1, keepdims=True) + EPS) * g_ref[...]
    o_ref[0] = (y * (1.0 + sc_ref[0]) + sh_ref[0]).astype(o_ref.dtype)


def ln_mod(x, g, sc, sh, out_dtype, tm=512):
    b, t, d = x.shape
    return pl.pallas_call(
        _ln_kernel,
        grid=(b, t // tm),
        in_specs=[pl.BlockSpec((1, tm, d), lambda i, j: (i, j, 0)),
                  _full((1, d)),
                  pl.BlockSpec((1, 1, d), lambda i, j: (i, 0, 0)),
                  pl.BlockSpec((1, 1, d), lambda i, j: (i, 0, 0))],
        out_specs=pl.BlockSpec((1, tm, d), lambda i, j: (i, j, 0)),
        out_shape=jax.ShapeDtypeStruct((b, t, d), out_dtype),
        compiler_params=_cparams(("arbitrary", "arbitrary")),
        name="ln_mod",
    )(x, g.reshape(1, d), sc.reshape(b, 1, d), sh.reshape(b, 1, d))


def proj(h, w, out_dtype, epilogue=None, extras=(), tm=512, name="proj"):
    b, t, d = h.shape
    n = w.shape[1]

    def kern(h_ref, w_ref, *rest):
        o_ref = rest[-1]
        y = _dot(h_ref[0], w_ref[...])
        if epilogue is not None:
            y = epilogue(y, *[e[...] for e in rest[:-1]])
        o_ref[0] = y.astype(out_dtype)

    return pl.pallas_call(
        kern,
        grid=(b, t // tm),
        in_specs=[pl.BlockSpec((1, tm, d), lambda i, j: (i, j, 0)), _full((d, n))]
                 + [_full(e.shape) for e in extras],
        out_specs=pl.BlockSpec((1, tm, n), lambda i, j: (i, j, 0)),
        out_shape=jax.ShapeDtypeStruct((b, t, n), out_dtype),
        compiler_params=_cparams(("arbitrary", "arbitrary")),
        name=name,
    )(h, w, *extras)


def _head_norm_epilogue(dh):
    inv = 1.0 / dh

    def ep(y, bd, gain):
        ssq = _dot((y * y).astype(BF16), bd)
        return y * lax.rsqrt(ssq * inv + EPS) * gain

    return ep


def _block_diag_ones(n, dh):
    i = np.arange(n) // dh
    return jnp.asarray((i[:, None] == i[None, :]).astype(np.float32), dtype=BF16)


def _outproj_kernel(y1_ref, y2_ref, wa_ref, wb_ref, x_ref, g_ref, o_ref):
    y = _dot(y1_ref[0], wa_ref[...]) + _dot(y2_ref[0], wb_ref[...])
    o_ref[0] = x_ref[0] + g_ref[0] * y


def out_proj(y1, y2, w_out, x, gate, tm=512):
    b, t, d = x.shape
    n1, n2 = y1.shape[-1], y2.shape[-1]
    wa = w_out[:n1].astype(BF16)
    wb = w_out[n1:].astype(BF16)
    return pl.pallas_call(
        _outproj_kernel,
        grid=(b, t // tm),
        in_specs=[pl.BlockSpec((1, tm, n1), lambda i, j: (i, j, 0)),
                  pl.BlockSpec((1, tm, n2), lambda i, j: (i, j, 0)),
                  _full((n1, d)), _full((n2, d)),
                  pl.BlockSpec((1, tm, d), lambda i, j: (i, j, 0)),
                  pl.BlockSpec((1, 1, d), lambda i, j: (i, 0, 0))],
        out_specs=pl.BlockSpec((1, tm, d), lambda i, j: (i, j, 0)),
        out_shape=jax.ShapeDtypeStruct((b, t, d), F32),
        compiler_params=_cparams(("arbitrary", "arbitrary")),
        name="out_proj",
    )(y1, y2, wa, wb, x, gate.reshape(b, 1, d))


def _decay_kernel(s_ref, fb_ref, tril_ref, place_ref, o_ref, carry):
    @pl.when(pl.program_id(1) == 0)
    def _():
        carry[...] = jnp.zeros_like(carry)

    tm = s_ref.shape[1]
    lf = _log_sigmoid(s_ref[0] + fb_ref[...])
    cum = _dot_hi(tril_ref[...], lf) + carry[...]
    carry[...] = cum[tm - 1:tm, :]
    x = cum * LOG2E
    hi = x.astype(BF16)
    r1 = x - hi.astype(F32)
    mid = r1.astype(BF16)
    low = (r1 - mid.astype(F32)).astype(BF16)
    o_ref[0] = _dot(jnp.concatenate([hi, mid, low], axis=1), place_ref[...]).astype(o_ref.dtype)


def fox_decay(small, fox_fb, tm=512):
    b, t, _ = small.shape
    fb = jnp.zeros((1, LANES), F32).at[0, :FOX_HEADS].set(fox_fb)
    tril = jnp.asarray(np.tril(np.ones((tm, tm), np.float32)))
    place = np.zeros((3 * LANES, FOX_W), np.float32)
    for h in range(FOX_HEADS):
        for j in range(3):
            place[j * LANES + h, (h // 2) * LANES + (FOX_DH if h % 2 == 0 else 0) + j] = 1.0
    return pl.pallas_call(
        _decay_kernel,
        grid=(b, t // tm),
        in_specs=[pl.BlockSpec((1, tm, LANES), lambda i, j: (i, j, 0)), _full((1, LANES)), _full((tm, tm)),
                  _full((3 * LANES, FOX_W))],
        out_specs=pl.BlockSpec((1, tm, FOX_W), lambda i, j: (i, j, 0)),
        out_shape=jax.ShapeDtypeStruct((b, t, FOX_W), BF16),
        scratch_shapes=[pltpu.VMEM((1, LANES), F32)],
        compiler_params=_cparams(("arbitrary", "arbitrary")),
        name="fox_decay",
    )(small, fb, tril, jnp.asarray(place, dtype=BF16))


def _fox_kernel(q_ref, k_ref, v_ref, f_ref, o_ref, *, tq):
    i = pl.program_id(2)
    lane = lax.broadcasted_iota(jnp.int32, (1, LANES), 1)
    lo = lane < FOX_DH
    coef = jnp.where((lane & (FOX_DH - 1)) < 3, -1.0, 0.0).astype(BF16)
    q = q_ref[0]
    qs = (jnp.where(lo, q, coef), jnp.where(lo, coef, q))
    causal = (lax.broadcasted_iota(jnp.int32, (tq, tq), 1) <= lax.broadcasted_iota(jnp.int32, (tq, tq), 0))

    def step(j, carry, diag):
        start = pl.multiple_of(j * tq, tq)
        kt = k_ref[0, pl.ds(start, tq), :]
        vt = v_ref[0, pl.ds(start, tq), :]
        ft = f_ref[0, pl.ds(start, tq), :]
        ks = (jnp.where(lo, kt, ft), jnp.where(lo, ft, kt))
        new = []
        for hh in range(2):
            m, l, acc = carry[hh]
            s = _dot_nt(qs[hh], ks[hh])
            if diag:
                s = jnp.where(causal, s, NEG)
            m_new = jnp.maximum(m, jnp.max(s, axis=1, keepdims=True))
            alpha = jnp.exp2(m - m_new)
            p = jnp.exp2(s - m_new)
            l = alpha * l + jnp.sum(p, axis=1, keepdims=True)
            acc = alpha * acc + _dot(p.astype(BF16), vt)
            new.append((m_new, l, acc))
        return tuple(new)

    init = tuple((jnp.full((tq, 1), NEG, F32), jnp.zeros((tq, 1), F32), jnp.zeros((tq, LANES), F32))
                 for _ in range(2))
    carry = lax.fori_loop(0, i, lambda j, c: step(j, c, False), init)
    carry = step(i, carry, True)
    o0 = carry[0][2] / carry[0][1]
    o1 = carry[1][2] / carry[1][1]
    o_ref[0] = jnp.where(lo, o0, o1).astype(o_ref.dtype)


def fox_attention(q, k, v, feat, tq=512):
    b, t, w = q.shape
    npair = w // LANES
    nt = t // tq
    whole = lambda: pl.BlockSpec((1, t, LANES), lambda bi, p, i: (bi, 0, p))
    return pl.pallas_call(
        functools.partial(_fox_kernel, tq=tq),
        grid=(b, npair, nt),
        in_specs=[pl.BlockSpec((1, tq, LANES), lambda bi, p, i: (bi, i, p)), whole(), whole(), whole()],
        out_specs=pl.BlockSpec((1, tq, LANES), lambda bi, p, i: (bi, i, p)),
        out_shape=jax.ShapeDtypeStruct((b, t, w), BF16),
        compiler_params=_cparams(("arbitrary", "arbitrary", "arbitrary")),
        name="fox_attn",
    )(q, k, v, feat)


def _mm(a, b):
    return _dot(a.astype(BF16), b.astype(BF16))


def _mm3(a, b):
    ah = a.astype(BF16)
    bh = b.astype(BF16)
    al = (a - ah.astype(F32)).astype(BF16)
    bl = (b - bh.astype(F32)).astype(BF16)
    return _dot(jnp.concatenate([ah, ah, al], axis=1), jnp.concatenate([bh, bl, bh], axis=0))


def _tril_solve(a, rhs, ri, ci):
    n = a[0].shape[0]
    both = lambda f, x, y: [f(p, q) for p, q in zip(x, y)]
    eye = (ri == ci).astype(F32)
    same = lambda b: (lax.shift_right_logical(ri, int(math.log2(b)))
                      == lax.shift_right_logical(ci, int(math.log2(b))))
    base = 16
    d = [jnp.where(same(base), p, 0.0) for p in a]
    d2 = both(_mm3, d, d)
    d4 = both(_mm3, d2, d2)
    r1 = [eye - p + p2 - t for p, p2, t in zip(d, d2, both(_mm3, d, d2))]
    d8 = both(_mm3, d4, d4)
    r2 = [eye + p4 + p8 + t for p4, p8, t in zip(d4, d8, both(_mm3, d4, d8))]
    t = both(_mm3, r1, r2)
    b = base
    while b < n:
        join = same(2 * b) & jnp.logical_not(same(b))
        low = [jnp.where(join, p, 0.0) for p in a]
        t = [p - q for p, q in zip(t, both(_mm3, both(_mm3, t, low), t))]
        b *= 2
    return both(_mm3, t, rhs)


GDN_BLOCK = 128


def _gdn_kernel(x_ref, sm_ref, z_ref, cw_ref, ega_ref, egb_ref, alog_ref, dtb_ref, on_ref, tril_ref,
                o_ref, s_scr, prev_scr, *, tc):
    c = GDN_BLOCK
    w = GDN_W

    @pl.when(pl.program_id(1) == 0)
    def _():
        s_scr[...] = jnp.zeros_like(s_scr)
        prev_scr[...] = jnp.zeros_like(prev_scr)

    x = x_ref[0]
    prev = prev_scr[...]
    row8 = lax.broadcasted_iota(jnp.int32, (8, 1), 0)
    acc = x * cw_ref[GDN_CONV - 1:GDN_CONV, :]
    for s in range(1, GDN_CONV):
        rolled = pltpu.roll(x, s, 0)
        head = jnp.where(row8 < s, pltpu.roll(prev, s, 0), rolled[0:8])
        shifted = jnp.concatenate([head, rolled[8:]], axis=0)
        acc = acc + shifted * cw_ref[GDN_CONV - 1 - s:GDN_CONV - s, :]
    prev_scr[...] = x[tc - 8:tc]
    xc = _silu(acc)

    sm = sm_ref[0]
    g_raw = _dot_hi(sm, ega_ref[...])
    b_raw = _dot_hi(sm, egb_ref[...])
    g = -jnp.exp(alog_ref[...]) * _softplus(g_raw + dtb_ref[...])
    beta_all = _sigmoid(b_raw)
    gc_all = _dot_hi(tril_ref[...], g)

    ri = lax.broadcasted_iota(jnp.int32, (c, c), 0)
    ci = lax.broadcasted_iota(jnp.int32, (c, c), 1)
    causal = ci <= ri
    strict = ci < ri

    nblk = tc // c
    a_l, attn_l, rhs_l, qd_l, kd_l, egl_l = [], [], [], [], [], []
    for h in range(GDN_HEADS):
        ln = slice(h * GDN_DH, (h + 1) * GDN_DH)
        qh = xc[:, h * GDN_DH:(h + 1) * GDN_DH]
        kh = xc[:, w + h * GDN_DH:w + (h + 1) * GDN_DH]
        qh = qh * lax.rsqrt(jnp.sum(qh * qh, axis=-1, keepdims=True) + EPS) * (GDN_DH ** -0.5)
        kh = kh * lax.rsqrt(jnp.sum(kh * kh, axis=-1, keepdims=True) + EPS)
        vh = xc[:, 2 * w + h * GDN_DH:2 * w + (h + 1) * GDN_DH]
        gch = gc_all[:, ln]
        gct = gch.T
        egc = jnp.exp(gch)
        bh = beta_all[:, ln]
        for n in range(nblk):
            sl = slice(n * c, (n + 1) * c)
            q, k, v, gc, be = qh[sl], kh[sl], vh[sl], gch[sl], bh[sl]
            decay = jnp.exp(jnp.where(causal, gc - gct[:, sl], NEG))
            kb = k * be
            kk = _dot_nt(jnp.concatenate([kb, q], axis=0).astype(BF16), k.astype(BF16))
            a_l.append(jnp.where(strict, kk[:c] * decay, 0.0))
            attn_l.append(jnp.where(causal, kk[c:] * decay, 0.0))
            rhs_l.append(jnp.concatenate([v * be, kb * egc[sl]], axis=1))
            gl = gc[c - 1:c, :]
            qd_l.append(q * egc[sl])
            kd_l.append(k * jnp.exp(gl - gc))
            egl_l.append(jnp.exp(gl))
    uw_l = _tril_solve(a_l, rhs_l, ri, ci)

    states = [s_scr[h] for h in range(GDN_HEADS)]
    for n in range(nblk):
        sl = slice(n * c, (n + 1) * c)
        idx = [h * nblk + n for h in range(GDN_HEADS)]
        ws = [_mm(jnp.concatenate([uw_l[i][:, GDN_DH:], qd_l[i]], axis=0), states[h])
              for h, i in enumerate(idx)]
        v_new = [uw_l[i][:, :GDN_DH] - ws[h][:c] for h, i in enumerate(idx)]
        o = [ws[h][c:] + _mm(attn_l[i], v_new[h]) for h, i in enumerate(idx)]
        states = [states[h] * egl_l[i] + _dot_tn(kd_l[i].astype(BF16), v_new[h].astype(BF16))
                  for h, i in enumerate(idx)]
        for h in range(GDN_HEADS):
            ln = slice(h * GDN_DH, (h + 1) * GDN_DH)
            on = o[h] * lax.rsqrt(jnp.mean(o[h] * o[h], axis=-1, keepdims=True) + EPS) * on_ref[...]
            o_ref[0, sl, ln] = (on * _silu(z_ref[0, sl, ln])).astype(o_ref.dtype)
    for h in range(GDN_HEADS):
        s_scr[h] = states[h]


def gated_delta_net(x, small, z, conv_w, a_log, dt_bias, on_gain, tc=512):
    b, t, _ = x.shape
    w = GDN_W
    ega = np.zeros((LANES, w), np.float32)
    egb = np.zeros((LANES, w), np.float32)
    for h in range(GDN_HEADS):
        ega[FOX_HEADS + h, h * GDN_DH:(h + 1) * GDN_DH] = 1.0
        egb[FOX_HEADS + GDN_HEADS + h, h * GDN_DH:(h + 1) * GDN_DH] = 1.0
    alog = jnp.repeat(a_log, GDN_DH).reshape(1, w)
    dtb = jnp.repeat(dt_bias, GDN_DH).reshape(1, w)
    idx = np.arange(tc)
    tril = ((idx[:, None] >= idx[None, :]) & (idx[:, None] // GDN_BLOCK == idx[None, :] // GDN_BLOCK))
    row = lambda n: pl.BlockSpec((1, tc, n), lambda i, j: (i, j, 0))
    return pl.pallas_call(
        functools.partial(_gdn_kernel, tc=tc),
        grid=(b, t // tc),
        in_specs=[row(3 * w), row(LANES), row(w), _full((GDN_CONV, 3 * w)), _full((LANES, w)), _full((LANES, w)),
                  _full((1, w)), _full((1, w)), _full((1, GDN_DH)), _full((tc, tc))],
        out_specs=row(w),
        out_shape=jax.ShapeDtypeStruct((b, t, w), BF16),
        scratch_shapes=[pltpu.VMEM((GDN_HEADS, GDN_DH, GDN_DH), F32), pltpu.VMEM((8, 3 * w), F32)],
        compiler_params=_cparams(("arbitrary", "arbitrary")),
        name="gdn",
    )(x, small, z, conv_w, jnp.asarray(ega), jnp.asarray(egb), alog, dtb, on_gain.reshape(1, GDN_DH),
      jnp.asarray(tril.astype(np.float32)))


def even_mixer(h, w_in, fox_fb, fox_qn, fox_kn, gdn_conv, gdn_a_log, gdn_dt_bias, gdn_on):
    cuts = np.cumsum((0,) + EV_SIZES)
    col = lambda i: w_in[:, cuts[i]:cuts[i + 1]]
    wb = lambda a: a.astype(BF16)
    bd = _block_diag_ones(FOX_W, FOX_DH)
    ep = _head_norm_epilogue(FOX_DH)
    qg = (jnp.tile(fox_qn, FOX_HEADS) * (FOX_DH ** -0.5 * LOG2E)).reshape(1, FOX_W)
    kg = jnp.tile(fox_kn, FOX_HEADS).reshape(1, FOX_W)
    fq = proj(h, wb(col(0)), BF16, ep, (bd, qg), name="proj_fq")
    fk = proj(h, wb(col(1)), BF16, ep, (bd, kg), name="proj_fk")
    fv = proj(h, wb(col(2)), BF16, name="proj_fv")
    w_small = jnp.zeros((w_in.shape[0], LANES), F32)
    w_small = w_small.at[:, 0:8].set(col(3)).at[:, 8:12].set(col(5)).at[:, 12:16].set(col(6))
    small = proj(h, wb(w_small), F32, name="proj_ev_small")
    gqkv = proj(h, wb(col(4)), F32, name="proj_gqkv")
    gz = proj(h, wb(col(7)), F32, name="proj_gz")
    cum_t = fox_decay(small, fox_fb)
    o_fox = fox_attention(fq, fk, fv, cum_t)
    o_gdn = gated_delta_net(gqkv, small, gz, gdn_conv, gdn_a_log, gdn_dt_bias, gdn_on)
    return o_fox, o_gdn


def _t5_bucket_np(dist):
    n = np.maximum(dist, 0)
    exact = REL_BUCKETS // 2
    nf = np.maximum(n, 1).astype(np.float32)
    large = exact + (np.log(nf / np.float32(exact)) / np.float32(math.log(REL_MAX_DIST / exact))
                     * np.float32(REL_BUCKETS - exact)).astype(np.int32)
    large = np.minimum(large, REL_BUCKETS - 1)
    return np.where(n < exact, n, large)


def _bias_table(rel_bias, dist, valid):
    shifted = rel_bias - rel_bias[REL_BUCKETS - 1:REL_BUCKETS]
    bucket = jnp.asarray(_t5_bucket_np(dist).astype(np.int32))
    onehot = (bucket[..., None] == jnp.arange(REL_BUCKETS, dtype=jnp.int32)).astype(F32)
    tb = jnp.einsum("...k,kh->h...", onehot, shifted, precision=HI)
    return jnp.where(jnp.asarray(valid)[None], tb, NEG)


def _cmp_kernel(r_ref, pos_ref, w1_ref, w2_ref, kn_ref, o_ref):
    m = r_ref.shape[3]
    half = r_ref.shape[4]
    r = r_ref[0, 0, 0].astype(BF16)
    a = _dot(r, w1_ref[0, :half, :])
    bm = _dot(r, w1_ref[0, half:, :])
    c = _dot(pos_ref[0].astype(BF16), w1_ref[0])
    hid = a + pltpu.roll(bm, m - 1, 0) + c[0:1, :]
    out = _dot(_silu(hid).astype(BF16), w2_ref[0])
    normed = out * lax.rsqrt(jnp.mean(out * out, axis=-1, keepdims=True) + EPS) * kn_ref[...]
    o_ref[0, 0, 0] = jnp.where(pl.program_id(0) == 0, normed, out).astype(o_ref.dtype)


def nsa_compress(kcvc, pos, w1, w2, kn):
    b, t, _ = kcvc.shape
    m = t // CMP_STRIDE
    half = CMP_STRIDE * NSA_DH
    r = kcvc.reshape(b, m, CMP_STRIDE, 2, NSA_KV_HEADS, NSA_DH).transpose(3, 0, 4, 1, 2, 5).reshape(2, b, 2, m, half)
    posf = jnp.zeros((2, 8, 2 * half), F32).at[:, 0].set(pos.reshape(2, 2 * half))
    w2d = jnp.concatenate([w2, w2], axis=-1).astype(BF16)
    knd = jnp.tile(kn, 2).reshape(1, LANES)
    return pl.pallas_call(
        _cmp_kernel,
        grid=(2, b, NSA_KV_HEADS),
        in_specs=[pl.BlockSpec((1, 1, 1, m, half), lambda s, i, k: (s, i, k, 0, 0)),
                  pl.BlockSpec((1, 8, 2 * half), lambda s, i, k: (s, 0, 0)),
                  pl.BlockSpec((1, 2 * half, CMP_HIDDEN), lambda s, i, k: (s, 0, 0)),
                  pl.BlockSpec((1, CMP_HIDDEN, LANES), lambda s, i, k: (s, 0, 0)),
                  _full((1, LANES))],
        out_specs=pl.BlockSpec((1, 1, 1, m, LANES), lambda s, i, k: (s, i, k, 0, 0)),
        out_shape=jax.ShapeDtypeStruct((2, b, NSA_KV_HEADS, m, LANES), BF16),
        compiler_params=_cparams(("arbitrary", "arbitrary", "arbitrary")),
        name="nsa_compress",
    )(r, posf, w1.astype(BF16), w2d, knd)


def _dot_split(a, b):
    hi = a.astype(BF16)
    lo = (a - hi.astype(F32)).astype(BF16)
    return _dot(hi, b) + _dot(lo, b)


def _head_q(q_ref, hh, lo):
    blk = q_ref[0, :, (hh // 2) * LANES:(hh // 2 + 1) * LANES]
    keep = lo if hh % 2 == 0 else jnp.logical_not(lo)
    return jnp.where(keep, blk, jnp.zeros_like(blk))


def _pair_heads(o, lo):
    return jnp.concatenate([jnp.where(lo, o[0], o[1]), jnp.where(lo, o[2], o[3])], axis=1)


def _nsa_sel_kernel(q_ref, kc_ref, vc_ref, ov_ref, bt_ref, o_ref, sel_ref, *, tq, nband, n_slc):
    i = pl.program_id(2)
    ncp = kc_ref.shape[3]
    nsp = sel_ref.shape[3]
    per = tq // CMP_STRIDE
    var = jnp.minimum(i, 1)
    bs = pl.multiple_of(per * jnp.maximum(i - 1, 0), per)
    lo = lax.broadcasted_iota(jnp.int32, (1, LANES), 1) < NSA_DH
    kc = kc_ref[0, 0, 0]
    vc = vc_ref[0, 0, 0]
    kcb = kc_ref[0, 0, 0, pl.ds(bs, nband), :]
    vcb = vc_ref[0, 0, 0, pl.ds(bs, nband), :]
    far_ok = lax.broadcasted_iota(jnp.int32, (1, ncp), 1) < per * (i - 1)
    ps_far = jnp.zeros((tq, ncp), F32)
    ps_band = jnp.zeros((tq, nband), F32)
    outs = []
    for hh in range(NSA_GROUP):
        qh = _head_q(q_ref, hh, lo)
        s_far = jnp.where(far_ok, _dot_nt(qh, kc), NEG)
        s_band = _dot_nt(qh, kcb) + bt_ref[var, hh]
        m = jnp.maximum(jnp.max(s_far, axis=1, keepdims=True), jnp.max(s_band, axis=1, keepdims=True))
        m = jnp.where(m < 0.5 * NEG, 0.0, m)
        p_far = jnp.exp(s_far - m)
        p_band = jnp.exp(s_band - m)
        l = jnp.sum(p_far, axis=1, keepdims=True) + jnp.sum(p_band, axis=1, keepdims=True)
        inv = 1.0 / jnp.where(l == 0.0, 1.0, l)
        outs.append((_dot(p_far.astype(BF16), vc) + _dot(p_band.astype(BF16), vcb)) * inv)
        ps_far = ps_far + p_far * inv
        ps_band = ps_band + p_band * inv
    o_ref[0] = _pair_heads(outs, lo).astype(o_ref.dtype)

    imp = _dot_split(ps_far, ov_ref[...]) + _dot_split(ps_band, ov_ref[pl.ds(bs, nband), :])
    blk = lax.broadcasted_iota(jnp.int32, (1, nsp), 1)
    blk_f = blk.astype(F32)
    qpos = i * tq + lax.broadcasted_iota(jnp.int32, (tq, 1), 0)
    cur = lax.shift_right_logical(qpos, int(math.log2(SLC_LEN)))
    forced = (blk == 0) | (blk == cur) | (blk == cur - 1)
    work = jnp.where(forced, FORCE_SCORE, jnp.where(blk <= cur, imp, NEG))
    work = jnp.where(blk < n_slc, work, -jnp.inf)
    sel = jnp.zeros((tq, nsp), F32)
    for _ in range(min(SLC_TOPK, n_slc)):
        m = jnp.max(work, axis=1, keepdims=True)
        first = jnp.min(jnp.where(work == m, blk_f, float(nsp)), axis=1, keepdims=True)
        pick = blk_f == first
        sel = jnp.where(pick, 1.0, sel)
        work = jnp.where(pick, -jnp.inf, work)
    sel_ref[0, 0] = sel.astype(sel_ref.dtype)


def nsa_select(q, cmp_kv, rel_bias, tq=256):
    b, t, _ = q.shape
    ncp = t // CMP_STRIDE
    n_cmp = ncp - 1
    n_slc = t // SLC_LEN
    nsp = max(LANES, n_slc)
    per = tq // CMP_STRIDE
    nband = 2 * per
    n = np.arange(ncp)[:, None]
    s = np.arange(nsp)[None, :]
    ov = ((CMP_STRIDE * n < SLC_LEN * s + SLC_LEN) & (CMP_STRIDE * n + CMP_LEN > SLC_LEN * s)
          & (n < n_cmp) & (s < n_slc)).astype(np.float32)
    qi = np.arange(tq)[:, None]
    nj = np.arange(nband)[None, :]
    end = CMP_STRIDE * nj + CMP_LEN - 1
    dist = np.stack([qi - end, tq + qi - end])
    bt = _bias_table(rel_bias, dist, dist >= 0)
    bt = bt.reshape(NSA_KV_HEADS, NSA_GROUP, 2, tq, nband).transpose(0, 2, 1, 3, 4)
    bt = bt.reshape(NSA_KV_HEADS * 2, NSA_GROUP, tq, nband)
    gw = NSA_GROUP * NSA_DH
    return pl.pallas_call(
        functools.partial(_nsa_sel_kernel, tq=tq, nband=nband, n_slc=n_slc),
        grid=(b, NSA_KV_HEADS, t // tq),
        in_specs=[pl.BlockSpec((1, tq, gw), lambda bi, k, i: (bi, i, k)),
                  pl.BlockSpec((1, 1, 1, ncp, LANES), lambda bi, k, i: (0, bi, k, 0, 0)),
                  pl.BlockSpec((1, 1, 1, ncp, LANES), lambda bi, k, i: (1, bi, k, 0, 0)),
                  _full((ncp, nsp)),
                  pl.BlockSpec((2, NSA_GROUP, tq, nband), lambda bi, k, i: (k, 0, 0, 0))],
        out_specs=[pl.BlockSpec((1, tq, gw), lambda bi, k, i: (bi, i, k)),
                   pl.BlockSpec((1, 1, tq, nsp), lambda bi, k, i: (bi, k, i, 0))],
        out_shape=[jax.ShapeDtypeStruct((b, t, NSA_W), BF16),
                   jax.ShapeDtypeStruct((b, NSA_KV_HEADS, t, nsp), BF16)],
        compiler_params=_cparams(("arbitrary", "arbitrary", "arbitrary")),
        name="nsa_select",
    )(q, cmp_kv, cmp_kv, jnp.asarray(ov, dtype=BF16), bt)


def _nsa_main_kernel(q_ref, ks_ref, vs_ref, kw0_ref, kw1_ref, kw2_ref, vw0_ref, vw1_ref, vw2_ref, sel_ref,
                     ocmp_ref, gate_ref, tb_ref, wm_ref, eg_ref, o_ref, *, tq):
    i = pl.program_id(2)
    nsp = sel_ref.shape[3]
    g = NSA_GROUP
    lo = lax.broadcasted_iota(jnp.int32, (1, LANES), 1) < NSA_DH
    qst = jnp.concatenate([_head_q(q_ref, hh, lo) for hh in range(g)], axis=0)
    sel = sel_ref[0, 0]
    blk_row = lax.broadcasted_iota(jnp.int32, (nsp, 1), 0)
    col_blk = lax.shift_right_logical(lax.broadcasted_iota(jnp.int32, (1, tq), 1), int(math.log2(SLC_LEN)))
    per = tq // SLC_LEN
    causal = (lax.broadcasted_iota(jnp.int32, (tq, tq), 1) <= lax.broadcasted_iota(jnp.int32, (tq, tq), 0))

    def sel_step(jt, carry, near):
        m, l, acc = carry
        start = pl.multiple_of(jnp.maximum(jt, 0) * tq, tq)
        kt = ks_ref[0, pl.ds(start, tq), :]
        vt = vs_ref[0, pl.ds(start, tq), :]
        expand = jnp.where(blk_row - per * jt == col_blk, 1.0, 0.0).astype(BF16)
        keep = _dot(sel, expand)
        madd = (keep - 1.0) * (-NEG)
        if near == 2:
            madd = jnp.where(causal, madd, NEG)
        s = _dot_nt(qst, kt).reshape(g, tq, tq) + madd[None]
        if near is not None:
            s = s + tb_ref[:, :, near * tq:(near + 1) * tq]
        s = s.reshape(g * tq, tq)
        m_new = jnp.maximum(m, jnp.max(s, axis=1, keepdims=True))
        alpha = jnp.exp(m - m_new)
        p = jnp.exp(s - m_new)
        l = alpha * l + jnp.sum(p, axis=1, keepdims=True)
        acc = alpha * acc + _dot(p.astype(BF16), vt)
        return m_new, l, acc

    carry = (jnp.full((g * tq, 1), NEG, F32), jnp.zeros((g * tq, 1), F32), jnp.zeros((g * tq, LANES), F32))
    carry = lax.fori_loop(0, jnp.maximum(i - 2, 0), lambda j, c: sel_step(j, c, None), carry)
    for near in range(3):
        carry = sel_step(i - 2 + near, carry, near)
    o_slc = (carry[2] / carry[1]).reshape(g, tq, LANES)

    var = jnp.minimum(i, 2)
    kws = (kw0_ref, kw1_ref, kw2_ref)
    vws = (vw0_ref, vw1_ref, vw2_ref)
    sw = []
    for near in range(3):
        s = _dot_nt(qst, kws[near][0]).reshape(g, tq, tq)
        s = s + tb_ref[:, :, near * tq:(near + 1) * tq] + wm_ref[var, :, near * tq:(near + 1) * tq][None]
        sw.append(s.reshape(g * tq, tq))
    m = jnp.maximum(jnp.maximum(jnp.max(sw[0], axis=1, keepdims=True), jnp.max(sw[1], axis=1, keepdims=True)),
                    jnp.max(sw[2], axis=1, keepdims=True))
    l = jnp.zeros((g * tq, 1), F32)
    acc = jnp.zeros((g * tq, LANES), F32)
    for near in range(3):
        p = jnp.exp(sw[near] - m)
        l = l + jnp.sum(p, axis=1, keepdims=True)
        acc = acc + _dot(p.astype(BF16), vws[near][0])
    o_win = (acc / l).reshape(g, tq, LANES)

    gates = _dot_hi(_sigmoid(gate_ref[0]), eg_ref[0])
    gw = g * NSA_DH
    out = (gates[:, 0:gw] * ocmp_ref[0].astype(F32)
           + gates[:, gw:2 * gw] * _pair_heads([o_slc[hh] for hh in range(g)], lo)
           + gates[:, 2 * gw:3 * gw] * _pair_heads([o_win[hh] for hh in range(g)], lo))
    o_ref[0] = out.astype(o_ref.dtype)


def nsa_main(q, ksw, vsw, sel, o_cmp, small, rel_bias, tq=256):
    b, t, _ = q.shape
    nsp = sel.shape[-1]
    g = NSA_GROUP
    gw = g * NSA_DH
    qi = np.arange(tq)[:, None]
    c = np.arange(3 * tq)[None, :]
    dist = qi + 2 * tq - c
    tb = _bias_table(rel_bias, dist, np.ones_like(dist, bool))
    wm = np.zeros((3, tq, 3 * tq), np.float32)
    for var in range(3):
        exists = c >= tq * (2 - var)
        wm[var] = np.where((dist >= 0) & (dist < WINDOW) & exists, 0.0, NEG)
    eg = np.zeros((NSA_KV_HEADS, LANES, 3 * gw), np.float32)
    for k in range(NSA_KV_HEADS):
        for hh in range(g):
            for br in range(3):
                eg[k, (k * g + hh) * 3 + br, br * gw + hh * NSA_DH:br * gw + (hh + 1) * NSA_DH] = 1.0
    near = lambda off, col: pl.BlockSpec(
        (1, tq, LANES), lambda bi, k, i: (bi, jnp.maximum(i - off, 0), col + k))
    return pl.pallas_call(
        functools.partial(_nsa_main_kernel, tq=tq),
        grid=(b, NSA_KV_HEADS, t // tq),
        in_specs=[pl.BlockSpec((1, tq, gw), lambda bi, k, i: (bi, i, k)),
                  pl.BlockSpec((1, t, LANES), lambda bi, k, i: (bi, 0, k)),
                  pl.BlockSpec((1, t, LANES), lambda bi, k, i: (bi, 0, k)),
                  near(2, 2), near(1, 2), near(0, 2), near(2, 2), near(1, 2), near(0, 2),
                  pl.BlockSpec((1, 1, tq, nsp), lambda bi, k, i: (bi, k, i, 0)),
                  pl.BlockSpec((1, tq, gw), lambda bi, k, i: (bi, i, k)),
                  pl.BlockSpec((1, tq, LANES), lambda bi, k, i: (bi, i, 0)),
                  pl.BlockSpec((g, tq, 3 * tq), lambda bi, k, i: (k, 0, 0)),
                  _full((3, tq, 3 * tq)),
                  pl.BlockSpec((1, LANES, 3 * gw), lambda bi, k, i: (k, 0, 0))],
        out_specs=pl.BlockSpec((1, tq, gw), lambda bi, k, i: (bi, i, k)),
        out_shape=jax.ShapeDtypeStruct((b, t, NSA_W), BF16),
        compiler_params=_cparams(("arbitrary", "arbitrary", "arbitrary")),
        name="nsa_main",
    )(q, ksw, vsw, ksw, ksw, ksw, vsw, vsw, vsw, sel, o_cmp, small, tb, jnp.asarray(wm), jnp.asarray(eg))


def _gla_kernel(qk_ref, v_ref, r_ref, sm_ref, wg_ref, bg_ref, on_ref, tril_ref, o_ref, s_scr, q_scr, k_scr, kd_scr,
                *, tc):
    c = GLA_CHUNK

    @pl.when(pl.program_id(1) == 0)
    def _():
        s_scr[...] = jnp.zeros_like(s_scr)

    kw = GLA_KW
    log_a = _log_sigmoid(_dot_hi(sm_ref[0], wg_ref[...]) + bg_ref[...]) * (1.0 / GLA_TAU)
    gcum = _dot_hi(tril_ref[...], log_a)
    q = qk_ref[0, :, 0:kw] * (GLA_DK ** -0.5)
    k = qk_ref[0, :, kw:2 * kw]
    q_scr[...] = q * jnp.exp(gcum)
    k_scr[...] = k * jnp.exp(-gcum)
    ri = lax.broadcasted_iota(jnp.int32, (c, c), 0)
    ci = lax.broadcasted_iota(jnp.int32, (c, c), 1)
    causal = ci <= ri
    lo = lax.broadcasted_iota(jnp.int32, (1, LANES), 1) < GLA_DK

    def chunk(n, _):
        sl = pl.ds(pl.multiple_of(n * c, c), c)
        for p in range(GLA_HEADS // 2):
            pl_ = slice(p * LANES, (p + 1) * LANES)
            qd = q_scr[sl, pl_]
            ki = k_scr[sl, pl_]
            kd = kd_scr[sl, pl_]
            for hh in range(2):
                h = 2 * p + hh
                keep = lo if hh == 0 else jnp.logical_not(lo)
                qm = jnp.where(keep, qd, 0.0).astype(BF16)
                v = v_ref[0, sl, h * GLA_DV:(h + 1) * GLA_DV]
                attn = jnp.where(causal, _dot_nt(qm, ki.astype(BF16)), 0.0)
                st = s_scr[h]
                o = _dot(attn.astype(BF16), v.astype(BF16)) + _dot_nt(qm, st.astype(BF16))
                on = o * lax.rsqrt(jnp.mean(o * o, axis=-1, keepdims=True) + EPS) * on_ref[...]
                o_ref[0, sl, h * GLA_DV:(h + 1) * GLA_DV] = (
                    on * _silu(r_ref[0, sl, h * GLA_DV:(h + 1) * GLA_DV])).astype(o_ref.dtype)
                s_scr[h] = st * kd_scr[pl.ds(tc + n * 8, 1), pl_] + _dot_tn(v.astype(BF16), kd.astype(BF16))
        return 0

    for n in range(tc // c):
        gl = gcum[n * c + c - 1:n * c + c, :]
        kd_scr[n * c:(n + 1) * c, :] = k[n * c:(n + 1) * c, :] * jnp.exp(gl - gcum[n * c:(n + 1) * c, :])
        kd_scr[tc + n * 8:tc + n * 8 + 8, :] = jnp.broadcast_to(jnp.exp(gl), (8, kw))
    lax.fori_loop(0, tc // c, chunk, 0)


def gated_linear_attention(qkvr, small, wg_up, bg, on_gain, tc=512):
    b, t, _ = qkvr.shape
    wg = jnp.zeros((LANES, GLA_KW), F32).at[3 * NSA_HEADS:3 * NSA_HEADS + GLA_GATE_RANK].set(wg_up)
    idx = np.arange(tc)
    tril = ((idx[:, None] >= idx[None, :]) & (idx[:, None] // GLA_CHUNK == idx[None, :] // GLA_CHUNK))
    nchunk = tc // GLA_CHUNK
    return pl.pallas_call(
        functools.partial(_gla_kernel, tc=tc),
        grid=(b, t // tc),
        in_specs=[pl.BlockSpec((1, tc, 2 * GLA_KW), lambda i, j: (i, j, 0)),
                  pl.BlockSpec((1, tc, GLA_W), lambda i, j: (i, j, 1)),
                  pl.BlockSpec((1, tc, GLA_W), lambda i, j: (i, j, 2)),
                  pl.BlockSpec((1, tc, LANES), lambda i, j: (i, j, 0)),
                  _full((LANES, GLA_KW)), _full((1, GLA_KW)), _full((1, GLA_DV)), _full((tc, tc))],
        out_specs=pl.BlockSpec((1, tc, GLA_W), lambda i, j: (i, j, 0)),
        out_shape=jax.ShapeDtypeStruct((b, t, GLA_W), BF16),
        scratch_shapes=[pltpu.VMEM((GLA_HEADS, GLA_DV, LANES), F32), pltpu.VMEM((tc, GLA_KW), F32),
                        pltpu.VMEM((tc, GLA_KW), F32), pltpu.VMEM((tc + 8 * nchunk, GLA_KW), F32)],
        compiler_params=_cparams(("arbitrary", "arbitrary")),
        name="gla",
    )(qkvr, qkvr, qkvr, small, wg, bg.reshape(1, GLA_KW), on_gain.reshape(1, GLA_DV),
      jnp.asarray(tril.astype(np.float32)))


def odd_mixer(h, w_in, nsa_qn, nsa_kn, nsa_pos, nsa_cmp_w1, nsa_cmp_w2, gla_wg_up, gla_bg, gla_on, rel_bias):
    cuts = np.cumsum((0,) + OD_SIZES)
    col = lambda i: w_in[:, cuts[i]:cuts[i + 1]]
    wb = lambda a: a.astype(BF16)
    dup = lambda a: jnp.concatenate([a[:, :NSA_DH], a[:, :NSA_DH], a[:, NSA_DH:], a[:, NSA_DH:]], axis=1)
    ep = _head_norm_epilogue(NSA_DH)
    bd = _block_diag_ones(NSA_W, NSA_DH)
    qg = (jnp.tile(nsa_qn, NSA_HEADS) * NSA_DH ** -0.5).reshape(1, NSA_W)
    kg = jnp.tile(nsa_kn, NSA_HEADS).reshape(1, NSA_W)
    nq = proj(h, wb(col(0)), BF16, ep, (bd, qg), name="proj_nq")
    kcvc = proj(h, wb(jnp.concatenate([col(1), col(2)], axis=1)), F32, name="proj_kcvc")
    ksw = proj(h, wb(jnp.concatenate([dup(col(3)), dup(col(5))], axis=1)), BF16, ep, (bd, kg), name="proj_ksw")
    vsw = proj(h, wb(jnp.concatenate([dup(col(4)), dup(col(6))], axis=1)), BF16, name="proj_vsw")
    w_small = jnp.zeros((w_in.shape[0], LANES), F32)
    w_small = w_small.at[:, 0:24].set(col(7)).at[:, 24:40].set(col(11))
    small = proj(h, wb(w_small), F32, name="proj_od_small")
    qkvr = proj(h, wb(jnp.concatenate([col(8), col(9), col(10), col(12)], axis=1)), F32, name="proj_gla")
    cmp_kv = nsa_compress(kcvc, nsa_pos, nsa_cmp_w1, nsa_cmp_w2, nsa_kn)
    o_cmp, sel = nsa_select(nq, cmp_kv, rel_bias)
    o_nsa = nsa_main(nq, ksw, vsw, sel, o_cmp, small, rel_bias)
    o_gla = gated_linear_attention(qkvr, small, gla_wg_up, gla_bg, gla_on)
    return o_nsa, o_gla


MOE_TM = 256
MOE_ROWS = 256


def _first_index(mask_val, idx, big, axis):
    return jnp.min(jnp.where(mask_val, idx, big), axis=axis, keepdims=True)


def _route_kernel(h_ref, rt_ref, b_ref, up_ref, eid_ref, rank_ref, w_ref, cnt_ref, run):
    tm = h_ref.shape[0]
    ne = N_EXPERTS
    gsz = ne // N_GROUPS

    @pl.when(pl.program_id(0) == 0)
    def _():
        run[...] = jnp.zeros_like(run)

    scores = _sigmoid(_dot_nt(rt_ref[...], h_ref[...], HI))
    biased = scores + b_ref[...]
    b3 = biased.reshape(N_GROUPS, gsz, tm)
    i3 = lax.broadcasted_iota(jnp.int32, (1, gsz, 1), 1).astype(F32)
    m1 = jnp.max(b3, axis=1, keepdims=True)
    f1 = _first_index(b3 == m1, i3, float(gsz), 1)
    m2 = jnp.max(jnp.where(i3 == f1, -jnp.inf, b3), axis=1, keepdims=True)
    gs = (m1 + m2).reshape(N_GROUPS, tm)
    gidx = lax.broadcasted_iota(jnp.int32, (N_GROUPS, 1), 0).astype(F32)
    gmask = jnp.zeros((N_GROUPS, tm), F32)
    for _ in range(TOPK_GROUPS):
        m = jnp.max(gs, axis=0, keepdims=True)
        pick = gidx == _first_index(gs == m, gidx, float(N_GROUPS), 0)
        gmask = jnp.where(pick, 1.0, gmask)
        gs = jnp.where(pick, -jnp.inf, gs)
    emask = jnp.broadcast_to(gmask.reshape(N_GROUPS, 1, tm), (N_GROUPS, gsz, tm)).reshape(ne, tm)
    work = jnp.where(emask > 0.5, biased, -jnp.inf)
    eidx = lax.broadcasted_iota(jnp.int32, (ne, 1), 0).astype(F32)
    picks, eids, ws = [], [], []
    for _ in range(TOP_K):
        m = jnp.max(work, axis=0, keepdims=True)
        first = _first_index(work == m, eidx, float(ne), 0)
        pick = eidx == first
        picks.append(pick)
        eids.append(first)
        ws.append(jnp.sum(jnp.where(pick, scores, 0.0), axis=0, keepdims=True))
        work = jnp.where(pick, -jnp.inf, work)
    wsum = ws[0]
    for k in range(1, TOP_K):
        wsum = wsum + ws[k]
    chosen = jnp.zeros((ne, tm), F32)
    for pick in picks:
        chosen = jnp.where(pick, 1.0, chosen)
    pos = run[...] + _dot(chosen.astype(BF16), up_ref[...])
    run[...] = run[...] + jnp.sum(chosen, axis=1, keepdims=True)
    cnt_ref[...] = run[...]
    row = lax.broadcasted_iota(jnp.int32, (8, 1), 0)
    eid_o = jnp.zeros((8, tm), F32)
    rank_o = jnp.zeros((8, tm), F32)
    w_o = jnp.zeros((LANES, tm), F32)
    rowl = lax.broadcasted_iota(jnp.int32, (LANES, 1), 0)
    for k in range(TOP_K):
        rk = jnp.sum(jnp.where(picks[k], pos, 0.0), axis=0, keepdims=True)
        eid_o = jnp.where(row == k, eids[k], eid_o)
        rank_o = jnp.where(row == k, rk, rank_o)
        w_o = jnp.where(rowl == k, ws[k] / wsum * ROUTE_SCALE, w_o)
    eid_ref[0] = eid_o.astype(jnp.int32)
    rank_ref[0] = rank_o.astype(jnp.int32)
    w_ref[...] = w_o.T


def moe_route(h2, router, e_bias, tm=MOE_TM):
    nt, d = h2.shape
    ne = N_EXPERTS
    up = jnp.asarray(np.triu(np.ones((tm, tm), np.float32), 1), dtype=BF16)
    nb = nt // tm
    return pl.pallas_call(
        _route_kernel,
        grid=(nb,),
        in_specs=[pl.BlockSpec((tm, d), lambda i: (i, 0)), _full((ne, d)), _full((ne, 1)), _full((tm, tm))],
        out_specs=[pl.BlockSpec((1, 8, tm), lambda i: (i, 0, 0)),
                   pl.BlockSpec((1, 8, tm), lambda i: (i, 0, 0)),
                   pl.BlockSpec((tm, LANES), lambda i: (i, 0)),
                   _full((ne, 1))],
        out_shape=[jax.ShapeDtypeStruct((nb, 8, tm), jnp.int32), jax.ShapeDtypeStruct((nb, 8, tm), jnp.int32),
                   jax.ShapeDtypeStruct((nt, LANES), F32), jax.ShapeDtypeStruct((ne, 1), F32)],
        scratch_shapes=[pltpu.VMEM((ne, 1), F32)],
        compiler_params=_cparams(("arbitrary",)),
        name="moe_route",
    )(h2, router.T, e_bias.reshape(ne, 1), up)


def _dispatch_kernel(dest_ref, h_ref, xs_ref, sem):
    tm = h_ref.shape[0]

    def copy(t, row):
        return pltpu.make_async_copy(h_ref.at[pl.ds(t, 1), :], xs_ref.at[pl.ds(row, 1), :], sem)

    def issue(t, _):
        for k in range(TOP_K):
            copy(t, dest_ref[0, k, t]).start()
        return 0

    def drain(t, _):
        for k in range(TOP_K):
            copy(0, 0).wait()
        return 0

    lax.fori_loop(0, tm, issue, 0, unroll=4)
    lax.fori_loop(0, tm, drain, 0, unroll=4)


def moe_dispatch(h2, dest, tm=MOE_TM):
    nt, d = h2.shape
    return pl.pallas_call(
        _dispatch_kernel,
        grid=(nt // tm,),
        in_specs=[pl.BlockSpec((1, 8, tm), lambda i: (i, 0, 0), memory_space=pltpu.SMEM),
                  pl.BlockSpec((tm, d), lambda i: (i, 0))],
        out_specs=pl.BlockSpec(memory_space=pl.ANY),
        scratch_shapes=[pltpu.SemaphoreType.DMA(())],
        out_shape=jax.ShapeDtypeStruct((nt * TOP_K, d), F32),
        compiler_params=_cparams(("arbitrary",)),
        name="moe_dispatch",
    )(dest, h2)


def _ffn_kernel(blk_ref, exp_ref, lo_ref, hi_ref, first_ref, valid_ref, x_ref, wg_ref, wu_ref, wd_ref, o_ref):
    i = pl.program_id(0)
    rows = x_ref.shape[0]

    @pl.when(valid_ref[i] == 1)
    def _():
        x = x_ref[...].astype(BF16)
        a = _dot(x, wg_ref[0].astype(BF16))
        u = _dot(x, wu_ref[0].astype(BF16))
        y = _dot((_silu(a) * u).astype(BF16), wd_ref[0].astype(BF16))
        r = blk_ref[i] * rows + lax.broadcasted_iota(jnp.int32, (rows, 1), 0)
        y = jnp.where((r >= lo_ref[i]) & (r < hi_ref[i]), y, 0.0)

        @pl.when(first_ref[i] == 1)
        def _():
            o_ref[...] = y

        @pl.when(first_ref[i] == 0)
        def _():
            o_ref[...] = o_ref[...] + y


def _ffn_items(counts, n_rows, rows):
    ne = N_EXPERTS
    nblk = n_rows // rows
    n_items = nblk + ne - 1
    ends = jnp.cumsum(counts)
    starts = ends - counts
    first_blk = starts // rows
    last_blk = jnp.maximum(ends - 1, 0) // rows
    per_e = jnp.where(counts > 0, last_blk - first_blk + 1, 0)
    item_end = jnp.cumsum(per_e)
    item_start = item_end - per_e
    total = item_end[-1]
    i = jnp.arange(n_items, dtype=jnp.int32)
    ic = jnp.minimum(i, total - 1)
    e = jnp.sum((item_end[None, :] <= ic[:, None]).astype(jnp.int32), axis=1)
    blk = first_blk[e] + ic - item_start[e]
    lo = jnp.maximum(starts[e], blk * rows)
    hi = jnp.minimum(ends[e], (blk + 1) * rows)
    valid = (i < total).astype(jnp.int32)
    first = (lo == blk * rows).astype(jnp.int32)
    return starts, (blk.astype(jnp.int32), e, lo.astype(jnp.int32), hi.astype(jnp.int32), first, valid)


def moe_ffn_sorted(xs, items, wg, wu, wd, rows=MOE_ROWS):
    n_rows, d = xs.shape
    n_items = items[0].shape[0]
    de = wg.shape[-1]
    return pl.pallas_call(
        _ffn_kernel,
        grid_spec=pltpu.PrefetchScalarGridSpec(
            num_scalar_prefetch=6,
            grid=(n_items,),
            in_specs=[pl.BlockSpec((rows, d), lambda i, blk, e, *_: (blk[i], 0)),
                      pl.BlockSpec((1, d, de), lambda i, blk, e, *_: (e[i], 0, 0)),
                      pl.BlockSpec((1, d, de), lambda i, blk, e, *_: (e[i], 0, 0)),
                      pl.BlockSpec((1, de, d), lambda i, blk, e, *_: (e[i], 0, 0))],
            out_specs=pl.BlockSpec((rows, d), lambda i, blk, e, *_: (blk[i], 0))),
        out_shape=jax.ShapeDtypeStruct((n_rows, d), F32),
        compiler_params=_cparams(("arbitrary",)),
        name="moe_ffn",
    )(*items, xs, wg, wu, wd)


def _combine_kernel(dest_ref, ys_ref, w_ref, h_ref, x_ref, g_ref, sg_ref, su_ref, sd_ref, o_ref, buf, sem):
    tm = h_ref.shape[0]

    def copy(t, k, row):
        return pltpu.make_async_copy(ys_ref.at[pl.ds(row, 1), :], buf.at[k, pl.ds(t, 1), :], sem)

    def issue(t, _):
        for k in range(TOP_K):
            copy(t, k, dest_ref[0, k, t]).start()
        return 0

    def drain(t, _):
        for k in range(TOP_K):
            copy(0, 0, 0).wait()
        return 0

    lax.fori_loop(0, tm, issue, 0, unroll=4)
    hb = h_ref[...].astype(BF16)
    y = _dot((_silu(_dot(hb, sg_ref[...])) * _dot(hb, su_ref[...])).astype(BF16), sd_ref[...])
    lax.fori_loop(0, tm, drain, 0, unroll=4)
    w = w_ref[...]
    for k in range(TOP_K):
        y = y + w[:, k:k + 1] * buf[k]
    o_ref[...] = x_ref[...] + g_ref[0] * y


def moe_combine(ys, dest, w, h2, x2, gate, sg, su, sd, seq, tm=MOE_TM):
    nt, d = h2.shape
    ds_ = sg.shape[-1]
    per_b = seq // tm
    tile = lambda: pl.BlockSpec((tm, d), lambda i: (i, 0))
    return pl.pallas_call(
        _combine_kernel,
        grid=(nt // tm,),
        in_specs=[pl.BlockSpec((1, 8, tm), lambda i: (i, 0, 0), memory_space=pltpu.SMEM),
                  pl.BlockSpec(memory_space=pl.ANY),
                  pl.BlockSpec((tm, LANES), lambda i: (i, 0)), tile(), tile(),
                  pl.BlockSpec((1, 1, d), lambda i: (i // per_b, 0, 0)),
                  _full((d, ds_)), _full((d, ds_)), _full((ds_, d))],
        out_specs=tile(),
        scratch_shapes=[pltpu.VMEM((TOP_K, tm, d), F32), pltpu.SemaphoreType.DMA(())],
        out_shape=jax.ShapeDtypeStruct((nt, d), F32),
        compiler_params=_cparams(("arbitrary",)),
        name="moe_combine",
    )(dest, ys, w, h2, x2, gate, sg.astype(BF16), su.astype(BF16), sd.astype(BF16))


def moe_layer(x, g_norm, sc, sh, gate, router, e_bias, wg, wu, wd, sg, su, sd):
    b, t, d = x.shape
    nt = b * t
    h = ln_mod(x, g_norm, sc, sh, F32)
    h2 = h.reshape(nt, d)
    eid, rank, w, counts = moe_route(h2, router, e_bias)
    starts, items = _ffn_items(counts.reshape(-1).astype(jnp.int32), nt * TOP_K, MOE_ROWS)
    hit = eid[..., None] == jnp.arange(N_EXPERTS, dtype=jnp.int32)
    dest = jnp.sum(jnp.where(hit, starts.astype(jnp.int32), 0), axis=-1) + rank
    xs = moe_dispatch(h2, dest)
    ys = moe_ffn_sorted(xs, items, wg, wu, wd)
    out = moe_combine(ys, dest, w, h2, x.reshape(nt, d), gate.reshape(b, 1, d), sg, su, sd, t)
    return out.reshape(b, t, d)


def kernel(x, c, ada_w, ada_b, norm_mix, norm_ffn, rel_bias, ev_w_in, ev_w_out, fox_fb, fox_qn, fox_kn, gdn_conv, gdn_a_log, gdn_dt_bias, gdn_on, od_w_in, od_w_out, nsa_qn, nsa_kn, nsa_pos, nsa_cmp_w1, nsa_cmp_w2, gla_wg_up, gla_bg, gla_on, moe_router, moe_bias, moe_wg, moe_wu, moe_wd, sh_wg, sh_wu, sh_wd):
    d = x.shape[-1]
    depth = ada_w.shape[0]
    mod = adaln(c, ada_w, ada_b)
    for layer in range(depth):
        sh1, sc1, g1, sh2, sc2, g2 = [mod[layer, :, i * d:(i + 1) * d] for i in range(6)]
        h = ln_mod(x, norm_mix[layer], sc1, sh1, BF16)
        j = layer // 2
        if layer % 2 == 0:
            y1, y2 = even_mixer(h, ev_w_in[j], fox_fb[j], fox_qn[j], fox_kn[j], gdn_conv[j], gdn_a_log[j],
                                gdn_dt_bias[j], gdn_on[j])
            w_out = ev_w_out[j]
        else:
            y1, y2 = odd_mixer(h, od_w_in[j], nsa_qn[j], nsa_kn[j], nsa_pos[j], nsa_cmp_w1[j], nsa_cmp_w2[j],
                               gla_wg_up[j], gla_bg[j], gla_on[j], rel_bias)
            w_out = od_w_out[j]
        x = out_proj(y1, y2, w_out, x, g1)
        x = moe_layer(x, norm_ffn[layer], sc2, sh2, g2, moe_router[layer], moe_bias[layer], moe_wg[layer],
                      moe_wu[layer], moe_wd[layer], sh_wg[layer], sh_wu[layer], sh_wd[layer])
    return x
```

```python
import functools
import math

import numpy as np
import jax
import jax.numpy as jnp
from jax import lax
from jax.experimental import pallas as pl
from jax.experimental.pallas import tpu as pltpu

F32 = jnp.float32
BF16 = jnp.bfloat16
HI = lax.Precision.HIGHEST

EPS = 1e-6
LOG2E = math.log2(math.e)
NEG = -1e30
FORCE_SCORE = 1e9

FOX_HEADS, FOX_DH = 8, 64
GDN_HEADS, GDN_DH, GDN_CONV, GDN_CHUNK = 4, 128, 4, 64
NSA_HEADS, NSA_KV_HEADS, NSA_DH = 8, 2, 64
NSA_GROUP = NSA_HEADS // NSA_KV_HEADS
CMP_LEN, CMP_STRIDE, CMP_HIDDEN = 32, 16, 256
SLC_LEN, SLC_TOPK, WINDOW = 64, 16, 512
GLA_HEADS, GLA_DK, GLA_DV, GLA_GATE_RANK, GLA_TAU, GLA_CHUNK = 4, 64, 128, 16, 16.0, 64
REL_BUCKETS, REL_MAX_DIST = 32, 128
N_EXPERTS, TOP_K, D_EXPERT, D_SHARED = 64, 6, 256, 256
N_GROUPS, TOPK_GROUPS, ROUTE_SCALE = 8, 4, 2.5

FOX_W = FOX_HEADS * FOX_DH
GDN_W = GDN_HEADS * GDN_DH
NSA_W = NSA_HEADS * NSA_DH
NSA_KV_W = NSA_KV_HEADS * NSA_DH
GLA_KW = GLA_HEADS * GLA_DK
GLA_W = GLA_HEADS * GLA_DV
EV_SIZES = (FOX_W, FOX_W, FOX_W, FOX_HEADS, 3 * GDN_W, GDN_HEADS, GDN_HEADS, GDN_W)
OD_SIZES = (NSA_W,) + (NSA_KV_W,) * 6 + (3 * NSA_HEADS, GLA_KW, GLA_KW, GLA_W, GLA_GATE_RANK, GLA_W)

LANES = 128
ROW_GROUP = 64
VMEM_LIMIT = 56 * 1024 * 1024


def _cparams(sem, flags=None):
    return pltpu.CompilerParams(dimension_semantics=sem, vmem_limit_bytes=VMEM_LIMIT, flags=flags)


def _full(shape):
    n = len(shape)
    return pl.BlockSpec(shape, lambda *_: (0,) * n)


def _dot(a, b):
    return jnp.dot(a, b, preferred_element_type=F32)


def _dot_hi(a, b):
    return jnp.dot(a, b, precision=HI, preferred_element_type=F32)


def _dot_nt(a, b, precision=None):
    return lax.dot_general(a, b, (((1,), (1,)), ((), ())), precision=precision, preferred_element_type=F32)


def _dot_tn(a, b, precision=None):
    return lax.dot_general(a, b, (((0,), (0,)), ((), ())), precision=precision, preferred_element_type=F32)


def _sigmoid(x):
    return 1.0 / (1.0 + jnp.exp(-x))


def _silu(x):
    return x * _sigmoid(x)


def _softplus(x):
    return jnp.maximum(x, 0.0) + jnp.log(1.0 + jnp.exp(-jnp.abs(x)))


def _log_sigmoid(x):
    return -_softplus(-x)


def _adaln_kernel(c_ref, w_ref, b_ref, o_ref):
    c = c_ref[...]
    o_ref[0] = _dot_hi(_silu(c), w_ref[0]) + b_ref[0]


def adaln(c, ada_w, ada_b):
    depth, d, n = ada_w.shape
    b = c.shape[0]
    cp = jnp.zeros((8, d), F32).at[:b].set(c)
    tn = 1536
    out = pl.pallas_call(
        _adaln_kernel,
        grid=(depth, n // tn),
        in_specs=[_full((8, d)),
                  pl.BlockSpec((1, d, tn), lambda l, j: (l, 0, j)),
                  pl.BlockSpec((1, 1, tn), lambda l, j: (l, 0, j))],
        out_specs=pl.BlockSpec((1, 8, tn), lambda l, j: (l, 0, j)),
        out_shape=jax.ShapeDtypeStruct((depth, 8, n), F32),
        compiler_params=_cparams(("arbitrary", "arbitrary")),
        name="adaln",
    )(cp, ada_w, ada_b.reshape(depth, 1, n))
    return out[:, :b]


def _ln_kernel(x_ref, g_ref, sc_ref, sh_ref, o_ref):
    x = x_ref[0]
    y = x * lax.rsqrt(jnp.mean(x * x, axis=-1, keepdims=True) + EPS) * g_ref[...]
    o_ref[0] = (y * (1.0 + sc_ref[0]) + sh_ref[0]).astype(o_ref.dtype)


def ln_mod(x, g, sc, sh, out_dtype, tm=512):
    b, t, d = x.shape
    return pl.pallas_call(
        _ln_kernel,
        grid=(b, t // tm),
        in_specs=[pl.BlockSpec((1, tm, d), lambda i, j: (i, j, 0)),
                  _full((1, d)),
                  pl.BlockSpec((1, 1, d), lambda i, j: (i, 0, 0)),
                  pl.BlockSpec((1, 1, d), lambda i, j: (i, 0, 0))],
        out_specs=pl.BlockSpec((1, tm, d), lambda i, j: (i, j, 0)),
        out_shape=jax.ShapeDtypeStruct((b, t, d), out_dtype),
        compiler_params=_cparams(("arbitrary", "arbitrary")),
        name="ln_mod",
    )(x, g.reshape(1, d), sc.reshape(b, 1, d), sh.reshape(b, 1, d))


def proj(h, w, out_dtype, epilogue=None, extras=(), tm=512, name="proj"):
    b, t, d = h.shape
    n = w.shape[1]

    def kern(h_ref, w_ref, *rest):
        o_ref = rest[-1]
        y = _dot(h_ref[0], w_ref[...])
        if epilogue is not None:
            y = epilogue(y, *[e[...] for e in rest[:-1]])
        o_ref[0] = y.astype(out_dtype)

    return pl.pallas_call(
        kern,
        grid=(b, t // tm),
        in_specs=[pl.BlockSpec((1, tm, d), lambda i, j: (i, j, 0)), _full((d, n))]
                 + [_full(e.shape) for e in extras],
        out_specs=pl.BlockSpec((1, tm, n), lambda i, j: (i, j, 0)),
        out_shape=jax.ShapeDtypeStruct((b, t, n), out_dtype),
        compiler_params=_cparams(("arbitrary", "arbitrary")),
        name=name,
    )(h, w, *extras)


def _head_norm_epilogue(dh):
    inv = 1.0 / dh

    def ep(y, bd, gain):
        ssq = _dot((y * y).astype(BF16), bd)
        return y * lax.rsqrt(ssq * inv + EPS) * gain

    return ep


def _block_diag_ones(n, dh):
    i = np.arange(n) // dh
    return jnp.asarray((i[:, None] == i[None, :]).astype(np.float32), dtype=BF16)


def _outproj_kernel(y1_ref, y2_ref, wa_ref, wb_ref, x_ref, g_ref, o_ref):
    y = _dot(y1_ref[0], wa_ref[...]) + _dot(y2_ref[0], wb_ref[...])
    o_ref[0] = x_ref[0] + g_ref[0] * y


def out_proj(y1, y2, w_out, x, gate, tm=512):
    b, t, d = x.shape
    n1, n2 = y1.shape[-1], y2.shape[-1]
    wa = w_out[:n1].astype(BF16)
    wb = w_out[n1:].astype(BF16)
    return pl.pallas_call(
        _outproj_kernel,
        grid=(b, t // tm),
        in_specs=[pl.BlockSpec((1, tm, n1), lambda i, j: (i, j, 0)),
                  pl.BlockSpec((1, tm, n2), lambda i, j: (i, j, 0)),
                  _full((n1, d)), _full((n2, d)),
                  pl.BlockSpec((1, tm, d), lambda i, j: (i, j, 0)),
                  pl.BlockSpec((1, 1, d), lambda i, j: (i, 0, 0))],
        out_specs=pl.BlockSpec((1, tm, d), lambda i, j: (i, j, 0)),
        out_shape=jax.ShapeDtypeStruct((b, t, d), F32),
        compiler_params=_cparams(("arbitrary", "arbitrary")),
        name="out_proj",
    )(y1, y2, wa, wb, x, gate.reshape(b, 1, d))


def _decay_kernel(s_ref, fb_ref, tril_ref, place_ref, o_ref, carry):
    @pl.when(pl.program_id(1) == 0)
    def _():
        carry[...] = jnp.zeros_like(carry)

    tm = s_ref.shape[1]
    lf = _log_sigmoid(s_ref[0] + fb_ref[...])
    cum = _dot_hi(tril_ref[...], lf) + carry[...]
    carry[...] = cum[tm - 1:tm, :]
    x = cum * LOG2E
    hi = x.astype(BF16)
    r1 = x - hi.astype(F32)
    mid = r1.astype(BF16)
    low = (r1 - mid.astype(F32)).astype(BF16)
    o_ref[0] = _dot(jnp.concatenate([hi, mid, low], axis=1), place_ref[...]).astype(o_ref.dtype)


def fox_decay(small, fox_fb, tm=512):
    b, t, _ = small.shape
    fb = jnp.zeros((1, LANES), F32).at[0, :FOX_HEADS].set(fox_fb)
    tril = jnp.asarray(np.tril(np.ones((tm, tm), np.float32)))
    place = np.zeros((3 * LANES, FOX_W), np.float32)
    for h in range(FOX_HEADS):
        for j in range(3):
            place[j * LANES + h, (h // 2) * LANES + (FOX_DH if h % 2 == 0 else 0) + j] = 1.0
    return pl.pallas_call(
        _decay_kernel,
        grid=(b, t // tm),
        in_specs=[pl.BlockSpec((1, tm, LANES), lambda i, j: (i, j, 0)), _full((1, LANES)), _full((tm, tm)),
                  _full((3 * LANES, FOX_W))],
        out_specs=pl.BlockSpec((1, tm, FOX_W), lambda i, j: (i, j, 0)),
        out_shape=jax.ShapeDtypeStruct((b, t, FOX_W), BF16),
        scratch_shapes=[pltpu.VMEM((1, LANES), F32)],
        compiler_params=_cparams(("arbitrary", "arbitrary")),
        name="fox_decay",
    )(small, fb, tril, jnp.asarray(place, dtype=BF16))


def _fox_kernel(q_ref, k_ref, v_ref, f_ref, o_ref, *, tq, tk):
    i = pl.program_id(2)
    lane = lax.broadcasted_iota(jnp.int32, (1, LANES), 1)
    lo = lane < FOX_DH
    coef = jnp.where((lane & (FOX_DH - 1)) < 3, -1.0, 0.0).astype(BF16)
    q = q_ref[0]
    qs = (jnp.where(lo, q, coef), jnp.where(lo, coef, q))
    n_full = (i * tq) // tk
    causal = (lax.broadcasted_iota(jnp.int32, (tq, tk), 1)
              <= lax.broadcasted_iota(jnp.int32, (tq, tk), 0) + (i * tq - n_full * tk))

    def scores(j):
        start = pl.multiple_of(j * tk, tk)
        kt = k_ref[0, pl.ds(start, tk), :]
        ft = f_ref[0, pl.ds(start, tk), :]
        return _dot_nt(qs[0], jnp.where(lo, kt, ft)), _dot_nt(qs[1], jnp.where(lo, ft, kt))

    one = jnp.ones((1, LANES), BF16)

    def update(j, s_pair, carry, diag):
        vt = v_ref[0, pl.ds(pl.multiple_of(j * tk, tk), tk), :]
        vs = (jnp.where(lo, vt, one), jnp.where(lo, one, vt))
        new = []
        for hh in range(2):
            m, acc = carry[hh]
            s = s_pair[hh]
            if diag:
                s = jnp.where(causal, s, NEG)
            m_new = jnp.maximum(m, jnp.max(s, axis=1, keepdims=True))
            p = jnp.exp2(s - m_new)
            acc = jnp.exp2(m - m_new) * acc + _dot(p.astype(BF16), vs[hh])
            new.append((m_new, acc))
        return tuple(new)

    init = tuple((jnp.full((tq, 1), NEG, F32), jnp.zeros((tq, LANES), F32)) for _ in range(2))
    carry = lax.fori_loop(0, n_full, lambda j, c: update(j, scores(j), c, False), init)
    carry = update(n_full, scores(n_full), carry, True)
    acc = jnp.where(lo, carry[0][1], carry[1][1])
    den = jnp.where(lo, carry[1][1], carry[0][1])
    o_ref[0] = (acc / pltpu.roll(den, FOX_DH, 1)).astype(o_ref.dtype)


def fox_attention(q, k, v, feat, tq=512, tk=512):
    b, t, w = q.shape
    npair = w // LANES
    nt = t // tq
    tk = min(tk, t)
    whole = lambda: pl.BlockSpec((1, t, LANES), lambda bi, p, i: (bi, 0, p))
    return pl.pallas_call(
        functools.partial(_fox_kernel, tq=tq, tk=tk),
        grid=(b, npair, nt),
        in_specs=[pl.BlockSpec((1, tq, LANES), lambda bi, p, i: (bi, i, p)), whole(), whole(), whole()],
        out_specs=pl.BlockSpec((1, tq, LANES), lambda bi, p, i: (bi, i, p)),
        out_shape=jax.ShapeDtypeStruct((b, t, w), BF16),
        compiler_params=_cparams(("arbitrary", "arbitrary", "arbitrary")),
        name="fox_attn",
    )(q, k, v, feat)


def _mm(a, b):
    return _dot(a.astype(BF16), b.astype(BF16))


def _mm3(a, b):
    ah = a.astype(BF16)
    bh = b.astype(BF16)
    al = (a - ah.astype(F32)).astype(BF16)
    bl = (b - bh.astype(F32)).astype(BF16)
    return _dot(jnp.concatenate([ah, ah, al], axis=1), jnp.concatenate([bh, bl, bh], axis=0))


def _tril_solve(a, rhs, ri, ci):
    n = a[0].shape[0]
    both = lambda f, x, y: [f(p, q) for p, q in zip(x, y)]
    eye = (ri == ci).astype(F32)
    same = lambda b: (lax.shift_right_logical(ri, int(math.log2(b)))
                      == lax.shift_right_logical(ci, int(math.log2(b))))
    base = 16
    d = [jnp.where(same(base), p, 0.0) for p in a]
    d2 = both(_mm3, d, d)
    d4 = both(_mm3, d2, d2)
    r1 = [eye - p + p2 - t for p, p2, t in zip(d, d2, both(_mm3, d, d2))]
    d8 = both(_mm3, d4, d4)
    r2 = [eye + p4 + p8 + t for p4, p8, t in zip(d4, d8, both(_mm3, d4, d8))]
    t = both(_mm3, r1, r2)
    b = base
    while b < n:
        join = same(2 * b) & jnp.logical_not(same(b))
        low = [jnp.where(join, p, 0.0) for p in a]
        t = [p - q for p, q in zip(t, both(_mm3, both(_mm3, t, low), t))]
        b *= 2
    return both(_mm3, t, rhs)


GDN_BLOCK = 128


def _gdn_kernel(x_ref, sm_ref, z_ref, cw_ref, ega_ref, egb_ref, alog_ref, dtb_ref, on_ref, tril_ref,
                o_ref, s_scr, prev_scr, *, tc):
    c = GDN_BLOCK
    w = GDN_W

    @pl.when(pl.program_id(1) == 0)
    def _():
        s_scr[...] = jnp.zeros_like(s_scr)
        prev_scr[...] = jnp.zeros_like(prev_scr)

    x = x_ref[0]
    prev = prev_scr[...]
    row8 = lax.broadcasted_iota(jnp.int32, (8, 1), 0)
    acc = x * cw_ref[GDN_CONV - 1:GDN_CONV, :]
    for s in range(1, GDN_CONV):
        rolled = pltpu.roll(x, s, 0)
        head = jnp.where(row8 < s, pltpu.roll(prev, s, 0), rolled[0:8])
        shifted = jnp.concatenate([head, rolled[8:]], axis=0)
        acc = acc + shifted * cw_ref[GDN_CONV - 1 - s:GDN_CONV - s, :]
    prev_scr[...] = x[tc - 8:tc]
    xc = _silu(acc)

    sm = sm_ref[0]
    g_raw = _dot_hi(sm, ega_ref[...])
    b_raw = _dot_hi(sm, egb_ref[...])
    g = -jnp.exp(alog_ref[...]) * _softplus(g_raw + dtb_ref[...])
    beta_all = _sigmoid(b_raw)
    gc_all = _dot_hi(tril_ref[...], g)

    ri = lax.broadcasted_iota(jnp.int32, (c, c), 0)
    ci = lax.broadcasted_iota(jnp.int32, (c, c), 1)
    causal = ci <= ri
    strict = ci < ri

    nblk = tc // c
    a_l, attn_l, rhs_l, qd_l, kd_l, egl_l = [], [], [], [], [], []
    for h in range(GDN_HEADS):
        ln = slice(h * GDN_DH, (h + 1) * GDN_DH)
        qh = xc[:, h * GDN_DH:(h + 1) * GDN_DH]
        kh = xc[:, w + h * GDN_DH:w + (h + 1) * GDN_DH]
        qh = qh * lax.rsqrt(jnp.sum(qh * qh, axis=-1, keepdims=True) + EPS) * (GDN_DH ** -0.5)
        kh = kh * lax.rsqrt(jnp.sum(kh * kh, axis=-1, keepdims=True) + EPS)
        vh = xc[:, 2 * w + h * GDN_DH:2 * w + (h + 1) * GDN_DH]
        gch = gc_all[:, ln]
        gct = gch.T
        egc = jnp.exp(gch)
        bh = beta_all[:, ln]
        for n in range(nblk):
            sl = slice(n * c, (n + 1) * c)
            q, k, v, gc, be = qh[sl], kh[sl], vh[sl], gch[sl], bh[sl]
            decay = jnp.exp(jnp.where(causal, gc - gct[:, sl], NEG))
            kb = k * be
            kk = _dot_nt(jnp.concatenate([kb, q], axis=0).astype(BF16), k.astype(BF16))
            a_l.append(jnp.where(strict, kk[:c] * decay, 0.0))
            attn_l.append(jnp.where(causal, kk[c:] * decay, 0.0))
            rhs_l.append(jnp.concatenate([v * be, kb * egc[sl]], axis=1))
            gl = gc[c - 1:c, :]
            qd_l.append(q * egc[sl])
            kd_l.append(k * jnp.exp(gl - gc))
            egl_l.append(jnp.exp(gl))
    uw_l = _tril_solve(a_l, rhs_l, ri, ci)

    states = [s_scr[h] for h in range(GDN_HEADS)]
    for n in range(nblk):
        sl = slice(n * c, (n + 1) * c)
        idx = [h * nblk + n for h in range(GDN_HEADS)]
        ws = [_mm(jnp.concatenate([uw_l[i][:, GDN_DH:], qd_l[i]], axis=0), states[h])
              for h, i in enumerate(idx)]
        v_new = [uw_l[i][:, :GDN_DH] - ws[h][:c] for h, i in enumerate(idx)]
        o = [ws[h][c:] + _mm(attn_l[i], v_new[h]) for h, i in enumerate(idx)]
        states = [states[h] * egl_l[i] + _dot_tn(kd_l[i].astype(BF16), v_new[h].astype(BF16))
                  for h, i in enumerate(idx)]
        for h in range(GDN_HEADS):
            ln = slice(h * GDN_DH, (h + 1) * GDN_DH)
            on = o[h] * lax.rsqrt(jnp.mean(o[h] * o[h], axis=-1, keepdims=True) + EPS) * on_ref[...]
            o_ref[0, sl, ln] = (on * _silu(z_ref[0, sl, ln])).astype(o_ref.dtype)
    for h in range(GDN_HEADS):
        s_scr[h] = states[h]


def gated_delta_net(x, small, z, conv_w, a_log, dt_bias, on_gain, tc=512):
    b, t, _ = x.shape
    w = GDN_W
    ega = np.zeros((LANES, w), np.float32)
    egb = np.zeros((LANES, w), np.float32)
    for h in range(GDN_HEADS):
        ega[FOX_HEADS + h, h * GDN_DH:(h + 1) * GDN_DH] = 1.0
        egb[FOX_HEADS + GDN_HEADS + h, h * GDN_DH:(h + 1) * GDN_DH] = 1.0
    alog = jnp.repeat(a_log, GDN_DH).reshape(1, w)
    dtb = jnp.repeat(dt_bias, GDN_DH).reshape(1, w)
    idx = np.arange(tc)
    tril = ((idx[:, None] >= idx[None, :]) & (idx[:, None] // GDN_BLOCK == idx[None, :] // GDN_BLOCK))
    row = lambda n: pl.BlockSpec((1, tc, n), lambda i, j: (i, j, 0))
    return pl.pallas_call(
        functools.partial(_gdn_kernel, tc=tc),
        grid=(b, t // tc),
        in_specs=[row(3 * w), row(LANES), row(w), _full((GDN_CONV, 3 * w)), _full((LANES, w)), _full((LANES, w)),
                  _full((1, w)), _full((1, w)), _full((1, GDN_DH)), _full((tc, tc))],
        out_specs=row(w),
        out_shape=jax.ShapeDtypeStruct((b, t, w), BF16),
        scratch_shapes=[pltpu.VMEM((GDN_HEADS, GDN_DH, GDN_DH), F32), pltpu.VMEM((8, 3 * w), F32)],
        compiler_params=_cparams(("arbitrary", "arbitrary")),
        name="gdn",
    )(x, small, z, conv_w, jnp.asarray(ega), jnp.asarray(egb), alog, dtb, on_gain.reshape(1, GDN_DH),
      jnp.asarray(tril.astype(np.float32)))


def even_mixer(h, w_in, fox_fb, fox_qn, fox_kn, gdn_conv, gdn_a_log, gdn_dt_bias, gdn_on):
    cuts = np.cumsum((0,) + EV_SIZES)
    col = lambda i: w_in[:, cuts[i]:cuts[i + 1]]
    wb = lambda a: a.astype(BF16)
    bd = _block_diag_ones(FOX_W, FOX_DH)
    ep = _head_norm_epilogue(FOX_DH)
    qg = (jnp.tile(fox_qn, FOX_HEADS) * (FOX_DH ** -0.5 * LOG2E)).reshape(1, FOX_W)
    kg = jnp.tile(fox_kn, FOX_HEADS).reshape(1, FOX_W)
    fq = proj(h, wb(col(0)), BF16, ep, (bd, qg), name="proj_fq")
    fk = proj(h, wb(col(1)), BF16, ep, (bd, kg), name="proj_fk")
    fv = proj(h, wb(col(2)), BF16, name="proj_fv")
    w_small = jnp.zeros((w_in.shape[0], LANES), F32)
    w_small = w_small.at[:, 0:8].set(col(3)).at[:, 8:12].set(col(5)).at[:, 12:16].set(col(6))
    small = proj(h, wb(w_small), F32, name="proj_ev_small")
    gqkv = proj(h, wb(col(4)), F32, name="proj_gqkv")
    gz = proj(h, wb(col(7)), F32, name="proj_gz")
    cum_t = fox_decay(small, fox_fb)
    o_fox = fox_attention(fq, fk, fv, cum_t)
    o_gdn = gated_delta_net(gqkv, small, gz, gdn_conv, gdn_a_log, gdn_dt_bias, gdn_on)
    return o_fox, o_gdn


def _t5_bucket_np(dist):
    n = np.maximum(dist, 0)
    exact = REL_BUCKETS // 2
    nf = np.maximum(n, 1).astype(np.float32)
    large = exact + (np.log(nf / np.float32(exact)) / np.float32(math.log(REL_MAX_DIST / exact))
                     * np.float32(REL_BUCKETS - exact)).astype(np.int32)
    large = np.minimum(large, REL_BUCKETS - 1)
    return np.where(n < exact, n, large)


def _bias_table(rel_bias, dist, valid):
    shifted = rel_bias - rel_bias[REL_BUCKETS - 1:REL_BUCKETS]
    bucket = jnp.asarray(_t5_bucket_np(dist).astype(np.int32))
    onehot = (bucket[..., None] == jnp.arange(REL_BUCKETS, dtype=jnp.int32)).astype(F32)
    tb = jnp.einsum("...k,kh->h...", onehot, shifted * LOG2E, precision=HI)
    return jnp.where(jnp.asarray(valid)[None], tb, NEG)


def _cmp_kernel(r_ref, pos_ref, w1_ref, w2_ref, kn_ref, o_ref):
    m = r_ref.shape[3]
    half = r_ref.shape[4]
    r = r_ref[0, 0, 0].astype(BF16)
    a = _dot(r, w1_ref[0, :half, :])
    bm = _dot(r, w1_ref[0, half:, :])
    c = _dot(pos_ref[0].astype(BF16), w1_ref[0])
    hid = a + pltpu.roll(bm, m - 1, 0) + c[0:1, :]
    out = _dot(_silu(hid).astype(BF16), w2_ref[0])
    normed = out * lax.rsqrt(jnp.mean(out * out, axis=-1, keepdims=True) + EPS) * kn_ref[...]
    o_ref[0, 0, 0] = jnp.where(pl.program_id(0) == 0, normed, out).astype(o_ref.dtype)


def nsa_compress(kcvc, pos, w1, w2, kn):
    b, t, _ = kcvc.shape
    m = t // CMP_STRIDE
    half = CMP_STRIDE * NSA_DH
    r = kcvc.reshape(b, m, CMP_STRIDE, 2, NSA_KV_HEADS, NSA_DH).transpose(3, 0, 4, 1, 2, 5).reshape(2, b, 2, m, half)
    posf = jnp.zeros((2, 8, 2 * half), F32).at[:, 0].set(pos.reshape(2, 2 * half))
    w2d = jnp.concatenate([w2, w2], axis=-1).astype(BF16)
    knd = jnp.tile(kn, 2).reshape(1, LANES)
    return pl.pallas_call(
        _cmp_kernel,
        grid=(2, b, NSA_KV_HEADS),
        in_specs=[pl.BlockSpec((1, 1, 1, m, half), lambda s, i, k: (s, i, k, 0, 0)),
                  pl.BlockSpec((1, 8, 2 * half), lambda s, i, k: (s, 0, 0)),
                  pl.BlockSpec((1, 2 * half, CMP_HIDDEN), lambda s, i, k: (s, 0, 0)),
                  pl.BlockSpec((1, CMP_HIDDEN, LANES), lambda s, i, k: (s, 0, 0)),
                  _full((1, LANES))],
        out_specs=pl.BlockSpec((1, 1, 1, m, LANES), lambda s, i, k: (s, i, k, 0, 0)),
        out_shape=jax.ShapeDtypeStruct((2, b, NSA_KV_HEADS, m, LANES), BF16),
        compiler_params=_cparams(("arbitrary", "arbitrary", "arbitrary")),
        name="nsa_compress",
    )(r, posf, w1.astype(BF16), w2d, knd)


def _dot_split(a, b):
    hi = a.astype(BF16)
    lo = (a - hi.astype(F32)).astype(BF16)
    return _dot(hi, b) + _dot(lo, b)


def _head_q(q_ref, hh, lo):
    blk = q_ref[0, :, (hh // 2) * LANES:(hh // 2 + 1) * LANES]
    keep = lo if hh % 2 == 0 else jnp.logical_not(lo)
    return jnp.where(keep, blk, jnp.zeros_like(blk))


def _pair_heads(o, lo):
    return jnp.concatenate([jnp.where(lo, o[0], o[1]), jnp.where(lo, o[2], o[3])], axis=1)


def _nsa_sel_kernel(q_ref, kc_ref, vc_ref, ov_ref, bt_ref, o_ref, sel_ref, *, tq, nband, n_slc):
    i = pl.program_id(2)
    ncp = kc_ref.shape[3]
    nsp = sel_ref.shape[3]
    per = tq // CMP_STRIDE
    var = jnp.minimum(i, 1)
    bs = pl.multiple_of(per * jnp.maximum(i - 1, 0), per)
    lo = lax.broadcasted_iota(jnp.int32, (1, LANES), 1) < NSA_DH
    kc = kc_ref[0, 0, 0]
    vc = vc_ref[0, 0, 0]
    kcb = kc_ref[0, 0, 0, pl.ds(bs, nband), :]
    vcb = vc_ref[0, 0, 0, pl.ds(bs, nband), :]
    far_ok = lax.broadcasted_iota(jnp.int32, (1, ncp), 1) < per * (i - 1)
    ps_far = jnp.zeros((tq, ncp), F32)
    ps_band = jnp.zeros((tq, nband), F32)
    outs = []
    for hh in range(NSA_GROUP):
        qh = _head_q(q_ref, hh, lo)
        s_far = jnp.where(far_ok, _dot_nt(qh, kc), NEG)
        s_band = _dot_nt(qh, kcb) + bt_ref[var, hh]
        m = jnp.maximum(jnp.max(s_far, axis=1, keepdims=True), jnp.max(s_band, axis=1, keepdims=True))
        m = jnp.where(m < 0.5 * NEG, 0.0, m)
        p_far = jnp.exp2(s_far - m)
        p_band = jnp.exp2(s_band - m)
        l = jnp.sum(p_far, axis=1, keepdims=True) + jnp.sum(p_band, axis=1, keepdims=True)
        inv = 1.0 / jnp.where(l == 0.0, 1.0, l)
        outs.append((_dot(p_far.astype(BF16), vc) + _dot(p_band.astype(BF16), vcb)) * inv)
        ps_far = ps_far + p_far * inv
        ps_band = ps_band + p_band * inv
    o_ref[0] = _pair_heads(outs, lo).astype(o_ref.dtype)

    imp = _dot_split(ps_far, ov_ref[...]) + _dot_split(ps_band, ov_ref[pl.ds(bs, nband), :])
    blk = lax.broadcasted_iota(jnp.int32, (1, nsp), 1)
    blk_f = blk.astype(F32)
    qpos = i * tq + lax.broadcasted_iota(jnp.int32, (tq, 1), 0)
    cur = lax.shift_right_logical(qpos, int(math.log2(SLC_LEN)))
    forced = (blk == 0) | (blk == cur) | (blk == cur - 1)
    work = jnp.where(forced, FORCE_SCORE, jnp.where(blk <= cur, imp, NEG))
    work = jnp.where(blk < n_slc, work, -jnp.inf)
    sel = jnp.zeros((tq, nsp), F32)
    for _ in range(min(SLC_TOPK, n_slc)):
        m = jnp.max(work, axis=1, keepdims=True)
        first = jnp.min(jnp.where(work == m, blk_f, float(nsp)), axis=1, keepdims=True)
        pick = blk_f == first
        sel = jnp.where(pick, 1.0, sel)
        work = jnp.where(pick, -jnp.inf, work)
    sel_ref[0, 0] = sel.astype(sel_ref.dtype)


def nsa_select(q, cmp_kv, rel_bias, tq=256):
    b, t, _ = q.shape
    ncp = t // CMP_STRIDE
    n_cmp = ncp - 1
    n_slc = t // SLC_LEN
    nsp = max(LANES, n_slc)
    per = tq // CMP_STRIDE
    nband = 2 * per
    n = np.arange(ncp)[:, None]
    s = np.arange(nsp)[None, :]
    ov = ((CMP_STRIDE * n < SLC_LEN * s + SLC_LEN) & (CMP_STRIDE * n + CMP_LEN > SLC_LEN * s)
          & (n < n_cmp) & (s < n_slc)).astype(np.float32)
    qi = np.arange(tq)[:, None]
    nj = np.arange(nband)[None, :]
    end = CMP_STRIDE * nj + CMP_LEN - 1
    dist = np.stack([qi - end, tq + qi - end])
    bt = _bias_table(rel_bias, dist, dist >= 0)
    bt = bt.reshape(NSA_KV_HEADS, NSA_GROUP, 2, tq, nband).transpose(0, 2, 1, 3, 4)
    bt = bt.reshape(NSA_KV_HEADS * 2, NSA_GROUP, tq, nband)
    gw = NSA_GROUP * NSA_DH
    return pl.pallas_call(
        functools.partial(_nsa_sel_kernel, tq=tq, nband=nband, n_slc=n_slc),
        grid=(b, NSA_KV_HEADS, t // tq),
        in_specs=[pl.BlockSpec((1, tq, gw), lambda bi, k, i: (bi, i, k)),
                  pl.BlockSpec((1, 1, 1, ncp, LANES), lambda bi, k, i: (0, bi, k, 0, 0)),
                  pl.BlockSpec((1, 1, 1, ncp, LANES), lambda bi, k, i: (1, bi, k, 0, 0)),
                  _full((ncp, nsp)),
                  pl.BlockSpec((2, NSA_GROUP, tq, nband), lambda bi, k, i: (k, 0, 0, 0))],
        out_specs=[pl.BlockSpec((1, tq, gw), lambda bi, k, i: (bi, i, k)),
                   pl.BlockSpec((1, 1, tq, nsp), lambda bi, k, i: (bi, k, i, 0))],
        out_shape=[jax.ShapeDtypeStruct((b, t, NSA_W), BF16),
                   jax.ShapeDtypeStruct((b, NSA_KV_HEADS, t, nsp), BF16)],
        compiler_params=_cparams(("arbitrary", "arbitrary", "arbitrary")),
        name="nsa_select",
    )(q, cmp_kv, cmp_kv, jnp.asarray(ov, dtype=BF16), bt)


def _nsa_main_kernel(q_ref, ks_ref, vs_ref, kw0_ref, kw1_ref, kw2_ref, vw0_ref, vw1_ref, vw2_ref, sel_ref,
                     ocmp_ref, gate_ref, tb_ref, wm_ref, eg_ref, o_ref, *, tq):
    i = pl.program_id(2)
    nsp = sel_ref.shape[3]
    g = NSA_GROUP
    lo = lax.broadcasted_iota(jnp.int32, (1, LANES), 1) < NSA_DH
    qst = jnp.concatenate([_head_q(q_ref, hh, lo) for hh in range(g)], axis=0)
    sel = sel_ref[0, 0]
    blk_row = lax.broadcasted_iota(jnp.int32, (nsp, 1), 0)
    causal = (lax.broadcasted_iota(jnp.int32, (tq, tq), 1) <= lax.broadcasted_iota(jnp.int32, (tq, tq), 0))
    one = jnp.ones((1, LANES), BF16)

    def sel_step(jt, carry, near, tw):
        m, acc = carry
        start = pl.multiple_of(jnp.maximum(jt, 0) * tw, tw)
        kt = ks_ref[0, pl.ds(start, tw), :]
        vt = jnp.where(lo, vs_ref[0, pl.ds(start, tw), :], one)
        col_blk = lax.shift_right_logical(lax.broadcasted_iota(jnp.int32, (1, tw), 1), int(math.log2(SLC_LEN)))
        expand = jnp.where(blk_row - (tw // SLC_LEN) * jt == col_blk, 1.0, 0.0).astype(BF16)
        keep = _dot(sel, expand)
        madd = (keep - 1.0) * (-NEG)
        if near == 2:
            madd = jnp.where(causal, madd, NEG)
        s = _dot_nt(qst, kt).reshape(g, tq, tw) + madd[None]
        if near is not None:
            s = s + tb_ref[:, :, near * tq:(near + 1) * tq]
        s = s.reshape(g * tq, tw)
        m_new = jnp.maximum(m, jnp.max(s, axis=1, keepdims=True))
        p = jnp.exp2(s - m_new)
        acc = jnp.exp2(m - m_new) * acc + _dot(p.astype(BF16), vt)
        return m_new, acc

    n_far = jnp.maximum(i - 2, 0)
    carry = (jnp.full((g * tq, 1), NEG, F32), jnp.zeros((g * tq, LANES), F32))
    carry = lax.fori_loop(0, lax.shift_right_logical(n_far, 1), lambda j, c: sel_step(j, c, None, 2 * tq), carry)
    carry = sel_step(jnp.where((n_far & 1) == 1, i - 3, -nsp), carry, None, tq)
    for near in range(3):
        carry = sel_step(i - 2 + near, carry, near, tq)
    o_slc = (carry[1] / pltpu.roll(carry[1], NSA_DH, 1)).reshape(g, tq, LANES)

    var = jnp.minimum(i, 2)
    kws = (kw0_ref, kw1_ref, kw2_ref)
    vws = (vw0_ref, vw1_ref, vw2_ref)
    sw = []
    for near in range(3):
        s = _dot_nt(qst, kws[near][0]).reshape(g, tq, tq)
        s = s + tb_ref[:, :, near * tq:(near + 1) * tq] + wm_ref[var, :, near * tq:(near + 1) * tq][None]
        sw.append(s.reshape(g * tq, tq))
    m = jnp.maximum(jnp.maximum(jnp.max(sw[0], axis=1, keepdims=True), jnp.max(sw[1], axis=1, keepdims=True)),
                    jnp.max(sw[2], axis=1, keepdims=True))
    acc = jnp.zeros((g * tq, LANES), F32)
    for near in range(3):
        p = jnp.exp2(sw[near] - m)
        acc = acc + _dot(p.astype(BF16), jnp.where(lo, vws[near][0], one))
    o_win = (acc / pltpu.roll(acc, NSA_DH, 1)).reshape(g, tq, LANES)

    pair = lambda o: jnp.concatenate([jnp.where(lo, o[0], pltpu.roll(o[1], NSA_DH, 1)),
                                      jnp.where(lo, o[2], pltpu.roll(o[3], NSA_DH, 1))], axis=1)
    gates = _dot_hi(_sigmoid(gate_ref[0]), eg_ref[0])
    gw = g * NSA_DH
    out = (gates[:, 0:gw] * ocmp_ref[0].astype(F32)
           + gates[:, gw:2 * gw] * pair(o_slc) + gates[:, 2 * gw:3 * gw] * pair(o_win))
    o_ref[0] = out.astype(o_ref.dtype)


def nsa_main(q, ksw, vsw, sel, o_cmp, small, rel_bias, tq=256):
    b, t, _ = q.shape
    nsp = sel.shape[-1]
    g = NSA_GROUP
    gw = g * NSA_DH
    qi = np.arange(tq)[:, None]
    c = np.arange(3 * tq)[None, :]
    dist = qi + 2 * tq - c
    tb = _bias_table(rel_bias, dist, np.ones_like(dist, bool))
    wm = np.zeros((3, tq, 3 * tq), np.float32)
    for var in range(3):
        exists = c >= tq * (2 - var)
        wm[var] = np.where((dist >= 0) & (dist < WINDOW) & exists, 0.0, NEG)
    eg = np.zeros((NSA_KV_HEADS, LANES, 3 * gw), np.float32)
    for k in range(NSA_KV_HEADS):
        for hh in range(g):
            for br in range(3):
                eg[k, (k * g + hh) * 3 + br, br * gw + hh * NSA_DH:br * gw + (hh + 1) * NSA_DH] = 1.0
    near = lambda off, col: pl.BlockSpec(
        (1, tq, LANES), lambda bi, k, i: (bi, jnp.maximum(i - off, 0), col + k))
    return pl.pallas_call(
        functools.partial(_nsa_main_kernel, tq=tq),
        grid=(b, NSA_KV_HEADS, t // tq),
        in_specs=[pl.BlockSpec((1, tq, gw), lambda bi, k, i: (bi, i, k)),
                  pl.BlockSpec((1, t, LANES), lambda bi, k, i: (bi, 0, k)),
                  pl.BlockSpec((1, t, LANES), lambda bi, k, i: (bi, 0, k)),
                  near(2, 2), near(1, 2), near(0, 2), near(2, 2), near(1, 2), near(0, 2),
                  pl.BlockSpec((1, 1, tq, nsp), lambda bi, k, i: (bi, k, i, 0)),
                  pl.BlockSpec((1, tq, gw), lambda bi, k, i: (bi, i, k)),
                  pl.BlockSpec((1, tq, LANES), lambda bi, k, i: (bi, i, 0)),
                  pl.BlockSpec((g, tq, 3 * tq), lambda bi, k, i: (k, 0, 0)),
                  _full((3, tq, 3 * tq)),
                  pl.BlockSpec((1, LANES, 3 * gw), lambda bi, k, i: (k, 0, 0))],
        out_specs=pl.BlockSpec((1, tq, gw), lambda bi, k, i: (bi, i, k)),
        out_shape=jax.ShapeDtypeStruct((b, t, NSA_W), BF16),
        compiler_params=_cparams(("arbitrary", "arbitrary", "arbitrary")),
        name="nsa_main",
    )(q, ksw, vsw, ksw, ksw, ksw, vsw, vsw, vsw, sel, o_cmp, small, tb, jnp.asarray(wm), jnp.asarray(eg))


def _gla_kernel(qk_ref, v_ref, r_ref, sm_ref, wg_ref, bg_ref, on_ref, tril_ref, o_ref, s_scr, q_scr, k_scr, kd_scr,
                *, tc):
    c = GLA_CHUNK

    @pl.when(pl.program_id(1) == 0)
    def _():
        s_scr[...] = jnp.zeros_like(s_scr)

    kw = GLA_KW
    log_a = _log_sigmoid(_dot_hi(sm_ref[0], wg_ref[...]) + bg_ref[...]) * (1.0 / GLA_TAU)
    gcum = _dot_hi(tril_ref[...], log_a)
    q = qk_ref[0, :, 0:kw] * (GLA_DK ** -0.5)
    k = qk_ref[0, :, kw:2 * kw]
    q_scr[...] = q * jnp.exp(gcum)
    k_scr[...] = k * jnp.exp(-gcum)
    ri = lax.broadcasted_iota(jnp.int32, (c, c), 0)
    ci = lax.broadcasted_iota(jnp.int32, (c, c), 1)
    causal = ci <= ri
    lo = lax.broadcasted_iota(jnp.int32, (1, LANES), 1) < GLA_DK

    def chunk(n, _):
        sl = pl.ds(pl.multiple_of(n * c, c), c)
        for p in range(GLA_HEADS // 2):
            pl_ = slice(p * LANES, (p + 1) * LANES)
            qd = q_scr[sl, pl_]
            ki = k_scr[sl, pl_]
            kd = kd_scr[sl, pl_]
            for hh in range(2):
                h = 2 * p + hh
                keep = lo if hh == 0 else jnp.logical_not(lo)
                qm = jnp.where(keep, qd, 0.0).astype(BF16)
                v = v_ref[0, sl, h * GLA_DV:(h + 1) * GLA_DV]
                attn = jnp.where(causal, _dot_nt(qm, ki.astype(BF16)), 0.0)
                st = s_scr[h]
                o = _dot(attn.astype(BF16), v.astype(BF16)) + _dot_nt(qm, st.astype(BF16))
                on = o * lax.rsqrt(jnp.mean(o * o, axis=-1, keepdims=True) + EPS) * on_ref[...]
                o_ref[0, sl, h * GLA_DV:(h + 1) * GLA_DV] = (
                    on * _silu(r_ref[0, sl, h * GLA_DV:(h + 1) * GLA_DV])).astype(o_ref.dtype)
                s_scr[h] = st * kd_scr[pl.ds(tc + n * 8, 1), pl_] + _dot_tn(v.astype(BF16), kd.astype(BF16))
        return 0

    for n in range(tc // c):
        gl = gcum[n * c + c - 1:n * c + c, :]
        kd_scr[n * c:(n + 1) * c, :] = k[n * c:(n + 1) * c, :] * jnp.exp(gl - gcum[n * c:(n + 1) * c, :])
        kd_scr[tc + n * 8:tc + n * 8 + 8, :] = jnp.broadcast_to(jnp.exp(gl), (8, kw))
    lax.fori_loop(0, tc // c, chunk, 0)


def gated_linear_attention(qkvr, small, wg_up, bg, on_gain, tc=512):
    b, t, _ = qkvr.shape
    wg = jnp.zeros((LANES, GLA_KW), F32).at[3 * NSA_HEADS:3 * NSA_HEADS + GLA_GATE_RANK].set(wg_up)
    idx = np.arange(tc)
    tril = ((idx[:, None] >= idx[None, :]) & (idx[:, None] // GLA_CHUNK == idx[None, :] // GLA_CHUNK))
    nchunk = tc // GLA_CHUNK
    return pl.pallas_call(
        functools.partial(_gla_kernel, tc=tc),
        grid=(b, t // tc),
        in_specs=[pl.BlockSpec((1, tc, 2 * GLA_KW), lambda i, j: (i, j, 0)),
                  pl.BlockSpec((1, tc, GLA_W), lambda i, j: (i, j, 1)),
                  pl.BlockSpec((1, tc, GLA_W), lambda i, j: (i, j, 2)),
                  pl.BlockSpec((1, tc, LANES), lambda i, j: (i, j, 0)),
                  _full((LANES, GLA_KW)), _full((1, GLA_KW)), _full((1, GLA_DV)), _full((tc, tc))],
        out_specs=pl.BlockSpec((1, tc, GLA_W), lambda i, j: (i, j, 0)),
        out_shape=jax.ShapeDtypeStruct((b, t, GLA_W), BF16),
        scratch_shapes=[pltpu.VMEM((GLA_HEADS, GLA_DV, LANES), F32), pltpu.VMEM((tc, GLA_KW), F32),
                        pltpu.VMEM((tc, GLA_KW), F32), pltpu.VMEM((tc + 8 * nchunk, GLA_KW), F32)],
        compiler_params=_cparams(("arbitrary", "arbitrary")),
        name="gla",
    )(qkvr, qkvr, qkvr, small, wg, bg.reshape(1, GLA_KW), on_gain.reshape(1, GLA_DV),
      jnp.asarray(tril.astype(np.float32)))


def odd_mixer(h, w_in, nsa_qn, nsa_kn, nsa_pos, nsa_cmp_w1, nsa_cmp_w2, gla_wg_up, gla_bg, gla_on, rel_bias):
    cuts = np.cumsum((0,) + OD_SIZES)
    col = lambda i: w_in[:, cuts[i]:cuts[i + 1]]
    wb = lambda a: a.astype(BF16)
    dup = lambda a: jnp.concatenate([a[:, :NSA_DH], a[:, :NSA_DH], a[:, NSA_DH:], a[:, NSA_DH:]], axis=1)
    ep = _head_norm_epilogue(NSA_DH)
    bd = _block_diag_ones(NSA_W, NSA_DH)
    qg = (jnp.tile(nsa_qn, NSA_HEADS) * (NSA_DH ** -0.5 * LOG2E)).reshape(1, NSA_W)
    kg = jnp.tile(nsa_kn, NSA_HEADS).reshape(1, NSA_W)
    nq = proj(h, wb(col(0)), BF16, ep, (bd, qg), name="proj_nq")
    kcvc = proj(h, wb(jnp.concatenate([col(1), col(2)], axis=1)), F32, name="proj_kcvc")
    ksw = proj(h, wb(jnp.concatenate([dup(col(3)), dup(col(5))], axis=1)), BF16, ep, (bd, kg), name="proj_ksw")
    vsw = proj(h, wb(jnp.concatenate([dup(col(4)), dup(col(6))], axis=1)), BF16, name="proj_vsw")
    w_small = jnp.zeros((w_in.shape[0], LANES), F32)
    w_small = w_small.at[:, 0:24].set(col(7)).at[:, 24:40].set(col(11))
    small = proj(h, wb(w_small), F32, name="proj_od_small")
    qkvr = proj(h, wb(jnp.concatenate([col(8), col(9), col(10), col(12)], axis=1)), F32, name="proj_gla")
    cmp_kv = nsa_compress(kcvc, nsa_pos, nsa_cmp_w1, nsa_cmp_w2, nsa_kn)
    o_cmp, sel = nsa_select(nq, cmp_kv, rel_bias)
    o_nsa = nsa_main(nq, ksw, vsw, sel, o_cmp, small, rel_bias)
    o_gla = gated_linear_attention(qkvr, small, gla_wg_up, gla_bg, gla_on)
    return o_nsa, o_gla


MOE_TM = 256
MOE_ROWS = 256


def _first_index(mask_val, idx, big, axis):
    return jnp.min(jnp.where(mask_val, idx, big), axis=axis, keepdims=True)


def _route_kernel(h_ref, rt_ref, b_ref, up_ref, eid_ref, rank_ref, w_ref, cnt_ref, run):
    tm = h_ref.shape[0]
    ne = N_EXPERTS
    gsz = ne // N_GROUPS

    @pl.when(pl.program_id(0) == 0)
    def _():
        run[...] = jnp.zeros_like(run)

    scores = _sigmoid(_dot_nt(rt_ref[...], h_ref[...], HI))
    biased = scores + b_ref[...]
    b3 = biased.reshape(N_GROUPS, gsz, tm)
    i3 = lax.broadcasted_iota(jnp.int32, (1, gsz, 1), 1).astype(F32)
    m1 = jnp.max(b3, axis=1, keepdims=True)
    f1 = _first_index(b3 == m1, i3, float(gsz), 1)
    m2 = jnp.max(jnp.where(i3 == f1, -jnp.inf, b3), axis=1, keepdims=True)
    gs = (m1 + m2).reshape(N_GROUPS, tm)
    gidx = lax.broadcasted_iota(jnp.int32, (N_GROUPS, 1), 0).astype(F32)
    gmask = jnp.zeros((N_GROUPS, tm), F32)
    for _ in range(TOPK_GROUPS):
        m = jnp.max(gs, axis=0, keepdims=True)
        pick = gidx == _first_index(gs == m, gidx, float(N_GROUPS), 0)
        gmask = jnp.where(pick, 1.0, gmask)
        gs = jnp.where(pick, -jnp.inf, gs)
    emask = jnp.broadcast_to(gmask.reshape(N_GROUPS, 1, tm), (N_GROUPS, gsz, tm)).reshape(ne, tm)
    work = jnp.where(emask > 0.5, biased, -jnp.inf)
    eidx = lax.broadcasted_iota(jnp.int32, (ne, 1), 0).astype(F32)
    picks, eids, ws = [], [], []
    for _ in range(TOP_K):
        m = jnp.max(work, axis=0, keepdims=True)
        first = _first_index(work == m, eidx, float(ne), 0)
        pick = eidx == first
        picks.append(pick)
        eids.append(first)
        ws.append(jnp.sum(jnp.where(pick, scores, 0.0), axis=0, keepdims=True))
        work = jnp.where(pick, -jnp.inf, work)
    wsum = ws[0]
    for k in range(1, TOP_K):
        wsum = wsum + ws[k]
    chosen = jnp.zeros((ne, tm), F32)
    for pick in picks:
        chosen = jnp.where(pick, 1.0, chosen)
    pos = run[...] + _dot(chosen.astype(BF16), up_ref[...])
    run[...] = run[...] + jnp.sum(chosen, axis=1, keepdims=True)
    cnt_ref[...] = run[...]
    row = lax.broadcasted_iota(jnp.int32, (8, 1), 0)
    eid_o = jnp.zeros((8, tm), F32)
    rank_o = jnp.zeros((8, tm), F32)
    w_o = jnp.zeros((LANES, tm), F32)
    rowl = lax.broadcasted_iota(jnp.int32, (LANES, 1), 0)
    for k in range(TOP_K):
        rk = jnp.sum(jnp.where(picks[k], pos, 0.0), axis=0, keepdims=True)
        eid_o = jnp.where(row == k, eids[k], eid_o)
        rank_o = jnp.where(row == k, rk, rank_o)
        w_o = jnp.where(rowl == k, ws[k] / wsum * ROUTE_SCALE, w_o)
    eid_ref[0] = eid_o.astype(jnp.int32)
    rank_ref[0] = rank_o.astype(jnp.int32)
    w_ref[...] = w_o.T


def moe_route(h2, router, e_bias, tm=MOE_TM):
    nt, d = h2.shape
    ne = N_EXPERTS
    up = jnp.asarray(np.triu(np.ones((tm, tm), np.float32), 1), dtype=BF16)
    nb = nt // tm
    return pl.pallas_call(
        _route_kernel,
        grid=(nb,),
        in_specs=[pl.BlockSpec((tm, d), lambda i: (i, 0)), _full((ne, d)), _full((ne, 1)), _full((tm, tm))],
        out_specs=[pl.BlockSpec((1, 8, tm), lambda i: (i, 0, 0)),
                   pl.BlockSpec((1, 8, tm), lambda i: (i, 0, 0)),
                   pl.BlockSpec((tm, LANES), lambda i: (i, 0)),
                   _full((ne, 1))],
        out_shape=[jax.ShapeDtypeStruct((nb, 8, tm), jnp.int32), jax.ShapeDtypeStruct((nb, 8, tm), jnp.int32),
                   jax.ShapeDtypeStruct((nt, LANES), F32), jax.ShapeDtypeStruct((ne, 1), F32)],
        scratch_shapes=[pltpu.VMEM((ne, 1), F32)],
        compiler_params=_cparams(("arbitrary",)),
        name="moe_route",
    )(h2, router.T, e_bias.reshape(ne, 1), up)


def _dispatch_kernel(dest_ref, h_ref, xs_ref, sem):
    tm = h_ref.shape[0]

    def copy(t, row):
        return pltpu.make_async_copy(h_ref.at[pl.ds(t, 1), :], xs_ref.at[pl.ds(row, 1), :], sem)

    def issue(t, _):
        for k in range(TOP_K):
            copy(t, dest_ref[0, k, t]).start()
        return 0

    def drain(t, _):
        for k in range(TOP_K):
            copy(0, 0).wait()
        return 0

    lax.fori_loop(0, tm, issue, 0, unroll=4)
    lax.fori_loop(0, tm, drain, 0, unroll=4)


def moe_dispatch(h2, dest, tm=MOE_TM):
    nt, d = h2.shape
    return pl.pallas_call(
        _dispatch_kernel,
        grid=(nt // tm,),
        in_specs=[pl.BlockSpec((1, 8, tm), lambda i: (i, 0, 0), memory_space=pltpu.SMEM),
                  pl.BlockSpec((tm, d), lambda i: (i, 0))],
        out_specs=pl.BlockSpec(memory_space=pl.ANY),
        scratch_shapes=[pltpu.SemaphoreType.DMA(())],
        out_shape=jax.ShapeDtypeStruct((nt * TOP_K, d), F32),
        compiler_params=_cparams(("arbitrary",)),
        name="moe_dispatch",
    )(dest, h2)


def _ffn_kernel(blk_ref, exp_ref, lo_ref, hi_ref, first_ref, valid_ref, x_ref, wg_ref, wu_ref, wd_ref, o_ref):
    i = pl.program_id(0)
    rows = x_ref.shape[0]

    @pl.when(valid_ref[i] == 1)
    def _():
        x = x_ref[...].astype(BF16)
        a = _dot(x, wg_ref[0].astype(BF16))
        u = _dot(x, wu_ref[0].astype(BF16))
        y = _dot((_silu(a) * u).astype(BF16), wd_ref[0].astype(BF16))
        r = blk_ref[i] * rows + lax.broadcasted_iota(jnp.int32, (rows, 1), 0)
        y = jnp.where((r >= lo_ref[i]) & (r < hi_ref[i]), y, 0.0)

        @pl.when(first_ref[i] == 1)
        def _():
            o_ref[...] = y

        @pl.when(first_ref[i] == 0)
        def _():
            o_ref[...] = o_ref[...] + y


def _ffn_items(counts, n_rows, rows):
    ne = N_EXPERTS
    nblk = n_rows // rows
    n_items = nblk + ne - 1
    ends = jnp.cumsum(counts)
    starts = ends - counts
    first_blk = starts // rows
    last_blk = jnp.maximum(ends - 1, 0) // rows
    per_e = jnp.where(counts > 0, last_blk - first_blk + 1, 0)
    item_end = jnp.cumsum(per_e)
    item_start = item_end - per_e
    total = item_end[-1]
    i = jnp.arange(n_items, dtype=jnp.int32)
    ic = jnp.minimum(i, total - 1)
    e = jnp.sum((item_end[None, :] <= ic[:, None]).astype(jnp.int32), axis=1)
    blk = first_blk[e] + ic - item_start[e]
    lo = jnp.maximum(starts[e], blk * rows)
    hi = jnp.minimum(ends[e], (blk + 1) * rows)
    valid = (i < total).astype(jnp.int32)
    first = (lo == blk * rows).astype(jnp.int32)
    return starts, (blk.astype(jnp.int32), e, lo.astype(jnp.int32), hi.astype(jnp.int32), first, valid)


def moe_ffn_sorted(xs, items, wg, wu, wd, rows=MOE_ROWS):
    n_rows, d = xs.shape
    n_items = items[0].shape[0]
    de = wg.shape[-1]
    return pl.pallas_call(
        _ffn_kernel,
        grid_spec=pltpu.PrefetchScalarGridSpec(
            num_scalar_prefetch=6,
            grid=(n_items,),
            in_specs=[pl.BlockSpec((rows, d), lambda i, blk, e, *_: (blk[i], 0)),
                      pl.BlockSpec((1, d, de), lambda i, blk, e, *_: (e[i], 0, 0)),
                      pl.BlockSpec((1, d, de), lambda i, blk, e, *_: (e[i], 0, 0)),
                      pl.BlockSpec((1, de, d), lambda i, blk, e, *_: (e[i], 0, 0))],
            out_specs=pl.BlockSpec((rows, d), lambda i, blk, e, *_: (blk[i], 0))),
        out_shape=jax.ShapeDtypeStruct((n_rows, d), F32),
        compiler_params=_cparams(("arbitrary",)),
        name="moe_ffn",
    )(*items, xs, wg, wu, wd)


def _combine_kernel(dest_ref, ys_ref, w_ref, h_ref, x_ref, g_ref, sg_ref, su_ref, sd_ref, o_ref, buf, sem):
    tm = h_ref.shape[0]

    def copy(t, k, row):
        return pltpu.make_async_copy(ys_ref.at[pl.ds(row, 1), :], buf.at[k, pl.ds(t, 1), :], sem)

    def issue(t, _):
        for k in range(TOP_K):
            copy(t, k, dest_ref[0, k, t]).start()
        return 0

    def drain(t, _):
        for k in range(TOP_K):
            copy(0, 0, 0).wait()
        return 0

    lax.fori_loop(0, tm, issue, 0, unroll=4)
    hb = h_ref[...].astype(BF16)
    y = _dot((_silu(_dot(hb, sg_ref[...])) * _dot(hb, su_ref[...])).astype(BF16), sd_ref[...])
    lax.fori_loop(0, tm, drain, 0, unroll=4)
    w = w_ref[...]
    for k in range(TOP_K):
        y = y + w[:, k:k + 1] * buf[k]
    o_ref[...] = x_ref[...] + g_ref[0] * y


def moe_combine(ys, dest, w, h2, x2, gate, sg, su, sd, seq, tm=MOE_TM):
    nt, d = h2.shape
    ds_ = sg.shape[-1]
    per_b = seq // tm
    tile = lambda: pl.BlockSpec((tm, d), lambda i: (i, 0))
    return pl.pallas_call(
        _combine_kernel,
        grid=(nt // tm,),
        in_specs=[pl.BlockSpec((1, 8, tm), lambda i: (i, 0, 0), memory_space=pltpu.SMEM),
                  pl.BlockSpec(memory_space=pl.ANY),
                  pl.BlockSpec((tm, LANES), lambda i: (i, 0)), tile(), tile(),
                  pl.BlockSpec((1, 1, d), lambda i: (i // per_b, 0, 0)),
                  _full((d, ds_)), _full((d, ds_)), _full((ds_, d))],
        out_specs=tile(),
        scratch_shapes=[pltpu.VMEM((TOP_K, tm, d), F32), pltpu.SemaphoreType.DMA(())],
        out_shape=jax.ShapeDtypeStruct((nt, d), F32),
        compiler_params=_cparams(("arbitrary",)),
        name="moe_combine",
    )(dest, ys, w, h2, x2, gate, sg.astype(BF16), su.astype(BF16), sd.astype(BF16))


def moe_layer(x, g_norm, sc, sh, gate, router, e_bias, wg, wu, wd, sg, su, sd):
    b, t, d = x.shape
    nt = b * t
    h = ln_mod(x, g_norm, sc, sh, F32)
    h2 = h.reshape(nt, d)
    eid, rank, w, counts = moe_route(h2, router, e_bias)
    starts, items = _ffn_items(counts.reshape(-1).astype(jnp.int32), nt * TOP_K, MOE_ROWS)
    hit = eid[..., None] == jnp.arange(N_EXPERTS, dtype=jnp.int32)
    dest = jnp.sum(jnp.where(hit, starts.astype(jnp.int32), 0), axis=-1) + rank
    xs = moe_dispatch(h2, dest)
    ys = moe_ffn_sorted(xs, items, wg, wu, wd)
    out = moe_combine(ys, dest, w, h2, x.reshape(nt, d), gate.reshape(b, 1, d), sg, su, sd, t)
    return out.reshape(b, t, d)


def kernel(x, c, ada_w, ada_b, norm_mix, norm_ffn, rel_bias, ev_w_in, ev_w_out, fox_fb, fox_qn, fox_kn, gdn_conv, gdn_a_log, gdn_dt_bias, gdn_on, od_w_in, od_w_out, nsa_qn, nsa_kn, nsa_pos, nsa_cmp_w1, nsa_cmp_w2, gla_wg_up, gla_bg, gla_on, moe_router, moe_bias, moe_wg, moe_wu, moe_wd, sh_wg, sh_wu, sh_wd):
    d = x.shape[-1]
    depth = ada_w.shape[0]
    mod = adaln(c, ada_w, ada_b)
    for layer in range(depth):
        sh1, sc1, g1, sh2, sc2, g2 = [mod[layer, :, i * d:(i + 1) * d] for i in range(6)]
        h = ln_mod(x, norm_mix[layer], sc1, sh1, BF16)
        j = layer // 2
        if layer % 2 == 0:
            y1, y2 = even_mixer(h, ev_w_in[j], fox_fb[j], fox_qn[j], fox_kn[j], gdn_conv[j], gdn_a_log[j],
                                gdn_dt_bias[j], gdn_on[j])
            w_out = ev_w_out[j]
        else:
            y1, y2 = odd_mixer(h, od_w_in[j], nsa_qn[j], nsa_kn[j], nsa_pos[j], nsa_cmp_w1[j], nsa_cmp_w2[j],
                               gla_wg_up[j], gla_bg[j], gla_on[j], rel_bias)
            w_out = od_w_out[j]
        x = out_proj(y1, y2, w_out, x, g1)
        x = moe_layer(x, norm_ffn[layer], sc2, sh2, g2, moe_router[layer], moe_bias[layer], moe_wg[layer],
                      moe_wu[layer], moe_wd[layer], sh_wg[layer], sh_wu[layer], sh_wd[layer])
    return x
```

```python
import functools
import math

import numpy as np
import jax
import jax.numpy as jnp
from jax import lax
from jax.experimental import pallas as pl
from jax.experimental.pallas import tpu as pltpu

F32 = jnp.float32
BF16 = jnp.bfloat16
HI = lax.Precision.HIGHEST

EPS = 1e-6
LOG2E = math.log2(math.e)
NEG = -1e30
FORCE_SCORE = 1e9

FOX_HEADS, FOX_DH = 8, 64
GDN_HEADS, GDN_DH, GDN_CONV, GDN_CHUNK = 4, 128, 4, 64
NSA_HEADS, NSA_KV_HEADS, NSA_DH = 8, 2, 64
NSA_GROUP = NSA_HEADS // NSA_KV_HEADS
CMP_LEN, CMP_STRIDE, CMP_HIDDEN = 32, 16, 256
SLC_LEN, SLC_TOPK, WINDOW = 64, 16, 512
GLA_HEADS, GLA_DK, GLA_DV, GLA_GATE_RANK, GLA_TAU, GLA_CHUNK = 4, 64, 128, 16, 16.0, 64
REL_BUCKETS, REL_MAX_DIST = 32, 128
N_EXPERTS, TOP_K, D_EXPERT, D_SHARED = 64, 6, 256, 256
N_GROUPS, TOPK_GROUPS, ROUTE_SCALE = 8, 4, 2.5

FOX_W = FOX_HEADS * FOX_DH
GDN_W = GDN_HEADS * GDN_DH
NSA_W = NSA_HEADS * NSA_DH
NSA_KV_W = NSA_KV_HEADS * NSA_DH
GLA_KW = GLA_HEADS * GLA_DK
GLA_W = GLA_HEADS * GLA_DV
EV_SIZES = (FOX_W, FOX_W, FOX_W, FOX_HEADS, 3 * GDN_W, GDN_HEADS, GDN_HEADS, GDN_W)
OD_SIZES = (NSA_W,) + (NSA_KV_W,) * 6 + (3 * NSA_HEADS, GLA_KW, GLA_KW, GLA_W, GLA_GATE_RANK, GLA_W)

LANES = 128
ROW_GROUP = 64
VMEM_LIMIT = 56 * 1024 * 1024


def _cparams(sem, flags=None):
    return pltpu.CompilerParams(dimension_semantics=sem, vmem_limit_bytes=VMEM_LIMIT, flags=flags)


def _full(shape):
    n = len(shape)
    return pl.BlockSpec(shape, lambda *_: (0,) * n)


def _dot(a, b):
    return jnp.dot(a, b, preferred_element_type=F32)


def _dot_hi(a, b):
    return jnp.dot(a, b, precision=HI, preferred_element_type=F32)


def _dot_nt(a, b, precision=None):
    return lax.dot_general(a, b, (((1,), (1,)), ((), ())), precision=precision, preferred_element_type=F32)


def _dot_tn(a, b, precision=None):
    return lax.dot_general(a, b, (((0,), (0,)), ((), ())), precision=precision, preferred_element_type=F32)


def _sigmoid(x):
    return 1.0 / (1.0 + jnp.exp(-x))


def _silu(x):
    return x * _sigmoid(x)


def _softplus(x):
    return jnp.maximum(x, 0.0) + jnp.log(1.0 + jnp.exp(-jnp.abs(x)))


def _log_sigmoid(x):
    return -_softplus(-x)


def _adaln_kernel(c_ref, w_ref, b_ref, o_ref):
    c = c_ref[...]
    o_ref[0] = _dot_hi(_silu(c), w_ref[0]) + b_ref[0]


def adaln(c, ada_w, ada_b):
    depth, d, n = ada_w.shape
    b = c.shape[0]
    cp = jnp.zeros((8, d), F32).at[:b].set(c)
    tn = 1536
    out = pl.pallas_call(
        _adaln_kernel,
        grid=(depth, n // tn),
        in_specs=[_full((8, d)),
                  pl.BlockSpec((1, d, tn), lambda l, j: (l, 0, j)),
                  pl.BlockSpec((1, 1, tn), lambda l, j: (l, 0, j))],
        out_specs=pl.BlockSpec((1, 8, tn), lambda l, j: (l, 0, j)),
        out_shape=jax.ShapeDtypeStruct((depth, 8, n), F32),
        compiler_params=_cparams(("arbitrary", "arbitrary")),
        name="adaln",
    )(cp, ada_w, ada_b.reshape(depth, 1, n))
    return out[:, :b]


def _ln_kernel(x_ref, g_ref, sc_ref, sh_ref, o_ref):
    x = x_ref[0]
    y = x * lax.rsqrt(jnp.mean(x * x, axis=-1, keepdims=True) + EPS) * g_ref[...]
    o_ref[0] = (y * (1.0 + sc_ref[0]) + sh_ref[0]).astype(o_ref.dtype)


def ln_mod(x, g, sc, sh, out_dtype, tm=512):
    b, t, d = x.shape
    return pl.pallas_call(
        _ln_kernel,
        grid=(b, t // tm),
        in_specs=[pl.BlockSpec((1, tm, d), lambda i, j: (i, j, 0)),
                  _full((1, d)),
                  pl.BlockSpec((1, 1, d), lambda i, j: (i, 0, 0)),
                  pl.BlockSpec((1, 1, d), lambda i, j: (i, 0, 0))],
        out_specs=pl.BlockSpec((1, tm, d), lambda i, j: (i, j, 0)),
        out_shape=jax.ShapeDtypeStruct((b, t, d), out_dtype),
        compiler_params=_cparams(("arbitrary", "arbitrary")),
        name="ln_mod",
    )(x, g.reshape(1, d), sc.reshape(b, 1, d), sh.reshape(b, 1, d))


def proj(h, w, out_dtype, epilogue=None, extras=(), tm=512, name="proj"):
    b, t, d = h.shape
    n = w.shape[1]

    def kern(h_ref, w_ref, *rest):
        o_ref = rest[-1]
        y = _dot(h_ref[0], w_ref[...])
        if epilogue is not None:
            y = epilogue(y, *[e[...] for e in rest[:-1]])
        o_ref[0] = y.astype(out_dtype)

    return pl.pallas_call(
        kern,
        grid=(b, t // tm),
        in_specs=[pl.BlockSpec((1, tm, d), lambda i, j: (i, j, 0)), _full((d, n))]
                 + [_full(e.shape) for e in extras],
        out_specs=pl.BlockSpec((1, tm, n), lambda i, j: (i, j, 0)),
        out_shape=jax.ShapeDtypeStruct((b, t, n), out_dtype),
        compiler_params=_cparams(("arbitrary", "arbitrary")),
        name=name,
    )(h, w, *extras)


def _head_norm_epilogue(dh):
    inv = 1.0 / dh

    def ep(y, bd, gain):
        ssq = _dot((y * y).astype(BF16), bd)
        return y * lax.rsqrt(ssq * inv + EPS) * gain

    return ep


def _block_diag_ones(n, dh):
    i = np.arange(n) // dh
    return jnp.asarray((i[:, None] == i[None, :]).astype(np.float32), dtype=BF16)


def _outproj_kernel(y1_ref, y2_ref, wa_ref, wb_ref, x_ref, g_ref, o_ref):
    y = _dot(y1_ref[0], wa_ref[...]) + _dot(y2_ref[0], wb_ref[...])
    o_ref[0] = x_ref[0] + g_ref[0] * y


def out_proj(y1, y2, w_out, x, gate, tm=512):
    b, t, d = x.shape
    n1, n2 = y1.shape[-1], y2.shape[-1]
    wa = w_out[:n1].astype(BF16)
    wb = w_out[n1:].astype(BF16)
    return pl.pallas_call(
        _outproj_kernel,
        grid=(b, t // tm),
        in_specs=[pl.BlockSpec((1, tm, n1), lambda i, j: (i, j, 0)),
                  pl.BlockSpec((1, tm, n2), lambda i, j: (i, j, 0)),
                  _full((n1, d)), _full((n2, d)),
                  pl.BlockSpec((1, tm, d), lambda i, j: (i, j, 0)),
                  pl.BlockSpec((1, 1, d), lambda i, j: (i, 0, 0))],
        out_specs=pl.BlockSpec((1, tm, d), lambda i, j: (i, j, 0)),
        out_shape=jax.ShapeDtypeStruct((b, t, d), F32),
        compiler_params=_cparams(("arbitrary", "arbitrary")),
        name="out_proj",
    )(y1, y2, wa, wb, x, gate.reshape(b, 1, d))


def _decay_kernel(s_ref, fb_ref, tril_ref, place_ref, o_ref, carry):
    @pl.when(pl.program_id(1) == 0)
    def _():
        carry[...] = jnp.zeros_like(carry)

    tm = s_ref.shape[1]
    lf = _log_sigmoid(s_ref[0] + fb_ref[...])
    cum = _dot_hi(tril_ref[...], lf) + carry[...]
    carry[...] = cum[tm - 1:tm, :]
    x = cum * LOG2E
    hi = x.astype(BF16)
    r1 = x - hi.astype(F32)
    mid = r1.astype(BF16)
    low = (r1 - mid.astype(F32)).astype(BF16)
    o_ref[0] = _dot(jnp.concatenate([hi, mid, low], axis=1), place_ref[...]).astype(o_ref.dtype)


def fox_decay(small, fox_fb, tm=512):
    b, t, _ = small.shape
    fb = jnp.zeros((1, LANES), F32).at[0, :FOX_HEADS].set(fox_fb)
    tril = jnp.asarray(np.tril(np.ones((tm, tm), np.float32)))
    place = np.zeros((3 * LANES, FOX_W), np.float32)
    for h in range(FOX_HEADS):
        for j in range(3):
            place[j * LANES + h, (h // 2) * LANES + (FOX_DH if h % 2 == 0 else 0) + j] = 1.0
    return pl.pallas_call(
        _decay_kernel,
        grid=(b, t // tm),
        in_specs=[pl.BlockSpec((1, tm, LANES), lambda i, j: (i, j, 0)), _full((1, LANES)), _full((tm, tm)),
                  _full((3 * LANES, FOX_W))],
        out_specs=pl.BlockSpec((1, tm, FOX_W), lambda i, j: (i, j, 0)),
        out_shape=jax.ShapeDtypeStruct((b, t, FOX_W), BF16),
        scratch_shapes=[pltpu.VMEM((1, LANES), F32)],
        compiler_params=_cparams(("arbitrary", "arbitrary")),
        name="fox_decay",
    )(small, fb, tril, jnp.asarray(place, dtype=BF16))


def _fox_kernel(q_ref, k_ref, v_ref, f_ref, o_ref, *, tq, tk):
    i = pl.program_id(2)
    lane = lax.broadcasted_iota(jnp.int32, (1, LANES), 1)
    lo = lane < FOX_DH
    coef = jnp.where((lane & (FOX_DH - 1)) < 3, -1.0, 0.0).astype(BF16)
    q = q_ref[0]
    qs = (jnp.where(lo, q, coef), jnp.where(lo, coef, q))
    n_full = (i * tq) // tk
    causal = (lax.broadcasted_iota(jnp.int32, (tq, tk), 1)
              <= lax.broadcasted_iota(jnp.int32, (tq, tk), 0) + (i * tq - n_full * tk))

    def scores(j):
        start = pl.multiple_of(j * tk, tk)
        kt = k_ref[0, pl.ds(start, tk), :]
        ft = f_ref[0, pl.ds(start, tk), :]
        return _dot_nt(qs[0], jnp.where(lo, kt, ft)), _dot_nt(qs[1], jnp.where(lo, ft, kt))

    one = jnp.ones((1, LANES), BF16)

    def update(j, s_pair, carry, diag):
        vt = v_ref[0, pl.ds(pl.multiple_of(j * tk, tk), tk), :]
        vs = (jnp.where(lo, vt, one), jnp.where(lo, one, vt))
        new = []
        for hh in range(2):
            m, acc = carry[hh]
            s = s_pair[hh]
            if diag:
                s = jnp.where(causal, s, NEG)
            m_new = jnp.maximum(m, jnp.max(s, axis=1, keepdims=True))
            p = jnp.exp2(s - m_new)
            acc = jnp.exp2(m - m_new) * acc + _dot(p.astype(BF16), vs[hh])
            new.append((m_new, acc))
        return tuple(new)

    init = tuple((jnp.full((tq, 1), NEG, F32), jnp.zeros((tq, LANES), F32)) for _ in range(2))
    carry = lax.fori_loop(0, n_full, lambda j, c: update(j, scores(j), c, False), init)
    carry = update(n_full, scores(n_full), carry, True)
    acc = jnp.where(lo, carry[0][1], carry[1][1])
    den = jnp.where(lo, carry[1][1], carry[0][1])
    o_ref[0] = (acc / pltpu.roll(den, FOX_DH, 1)).astype(o_ref.dtype)


def fox_attention(q, k, v, feat, tq=512, tk=1024):
    b, t, w = q.shape
    npair = w // LANES
    nt = t // tq
    tk = min(tk, t)
    whole = lambda: pl.BlockSpec((1, t, LANES), lambda bi, p, i: (bi, 0, p))
    return pl.pallas_call(
        functools.partial(_fox_kernel, tq=tq, tk=tk),
        grid=(b, npair, nt),
        in_specs=[pl.BlockSpec((1, tq, LANES), lambda bi, p, i: (bi, i, p)), whole(), whole(), whole()],
        out_specs=pl.BlockSpec((1, tq, LANES), lambda bi, p, i: (bi, i, p)),
        out_shape=jax.ShapeDtypeStruct((b, t, w), BF16),
        compiler_params=_cparams(("arbitrary", "arbitrary", "arbitrary")),
        name="fox_attn",
    )(q, k, v, feat)


def _mm(a, b):
    return _dot(a.astype(BF16), b.astype(BF16))


def _mm3(a, b):
    ah = a.astype(BF16)
    bh = b.astype(BF16)
    al = (a - ah.astype(F32)).astype(BF16)
    bl = (b - bh.astype(F32)).astype(BF16)
    return _dot(jnp.concatenate([ah, ah, al], axis=1), jnp.concatenate([bh, bl, bh], axis=0))


def _tril_solve(a, rhs, ri, ci):
    n = a[0].shape[0]
    both = lambda f, x, y: [f(p, q) for p, q in zip(x, y)]
    eye = (ri == ci).astype(F32)
    same = lambda b: (lax.shift_right_logical(ri, int(math.log2(b)))
                      == lax.shift_right_logical(ci, int(math.log2(b))))
    base = 16
    d = [jnp.where(same(base), p, 0.0) for p in a]
    d2 = both(_mm3, d, d)
    d4 = both(_mm3, d2, d2)
    r1 = [eye - p + p2 - t for p, p2, t in zip(d, d2, both(_mm3, d, d2))]
    d8 = both(_mm3, d4, d4)
    r2 = [eye + p4 + p8 + t for p4, p8, t in zip(d4, d8, both(_mm3, d4, d8))]
    t = both(_mm3, r1, r2)
    b = base
    while b < n:
        join = same(2 * b) & jnp.logical_not(same(b))
        low = [jnp.where(join, p, 0.0) for p in a]
        t = [p - q for p, q in zip(t, both(_mm3, both(_mm3, t, low), t))]
        b *= 2
    return both(_mm3, t, rhs)


GDN_BLOCK = 128


def _gdn_kernel(x_ref, sm_ref, z_ref, cw_ref, ega_ref, egb_ref, alog_ref, dtb_ref, on_ref, tril_ref,
                o_ref, s_scr, prev_scr, *, tc):
    c = GDN_BLOCK
    w = GDN_W

    @pl.when(pl.program_id(1) == 0)
    def _():
        s_scr[...] = jnp.zeros_like(s_scr)
        prev_scr[...] = jnp.zeros_like(prev_scr)

    x = x_ref[0]
    prev = prev_scr[...]
    row8 = lax.broadcasted_iota(jnp.int32, (8, 1), 0)
    acc = x * cw_ref[GDN_CONV - 1:GDN_CONV, :]
    for s in range(1, GDN_CONV):
        rolled = pltpu.roll(x, s, 0)
        head = jnp.where(row8 < s, pltpu.roll(prev, s, 0), rolled[0:8])
        shifted = jnp.concatenate([head, rolled[8:]], axis=0)
        acc = acc + shifted * cw_ref[GDN_CONV - 1 - s:GDN_CONV - s, :]
    prev_scr[...] = x[tc - 8:tc]
    xc = _silu(acc)

    sm = sm_ref[0]
    g_raw = _dot_hi(sm, ega_ref[...])
    b_raw = _dot_hi(sm, egb_ref[...])
    g = -jnp.exp(alog_ref[...]) * _softplus(g_raw + dtb_ref[...])
    beta_all = _sigmoid(b_raw)
    gc_all = _dot_hi(tril_ref[...], g)

    ri = lax.broadcasted_iota(jnp.int32, (c, c), 0)
    ci = lax.broadcasted_iota(jnp.int32, (c, c), 1)
    causal = ci <= ri
    strict = ci < ri

    nblk = tc // c
    a_l, attn_l, rhs_l, qd_l, kd_l, egl_l = [], [], [], [], [], []
    for h in range(GDN_HEADS):
        ln = slice(h * GDN_DH, (h + 1) * GDN_DH)
        qh = xc[:, h * GDN_DH:(h + 1) * GDN_DH]
        kh = xc[:, w + h * GDN_DH:w + (h + 1) * GDN_DH]
        qh = qh * lax.rsqrt(jnp.sum(qh * qh, axis=-1, keepdims=True) + EPS) * (GDN_DH ** -0.5)
        kh = kh * lax.rsqrt(jnp.sum(kh * kh, axis=-1, keepdims=True) + EPS)
        vh = xc[:, 2 * w + h * GDN_DH:2 * w + (h + 1) * GDN_DH]
        gch = gc_all[:, ln]
        gct = gch.T
        egc = jnp.exp(gch)
        bh = beta_all[:, ln]
        for n in range(nblk):
            sl = slice(n * c, (n + 1) * c)
            q, k, v, gc, be = qh[sl], kh[sl], vh[sl], gch[sl], bh[sl]
            decay = jnp.exp(jnp.where(causal, gc - gct[:, sl], NEG))
            kb = k * be
            kk = _dot_nt(jnp.concatenate([kb, q], axis=0).astype(BF16), k.astype(BF16))
            a_l.append(jnp.where(strict, kk[:c] * decay, 0.0))
            attn_l.append(jnp.where(causal, kk[c:] * decay, 0.0))
            rhs_l.append(jnp.concatenate([v * be, kb * egc[sl]], axis=1))
            gl = gc[c - 1:c, :]
            qd_l.append(q * egc[sl])
            kd_l.append(k * jnp.exp(gl - gc))
            egl_l.append(jnp.exp(gl))
    uw_l = _tril_solve(a_l, rhs_l, ri, ci)

    states = [s_scr[h] for h in range(GDN_HEADS)]
    for n in range(nblk):
        sl = slice(n * c, (n + 1) * c)
        idx = [h * nblk + n for h in range(GDN_HEADS)]
        ws = [_mm(jnp.concatenate([uw_l[i][:, GDN_DH:], qd_l[i]], axis=0), states[h])
              for h, i in enumerate(idx)]
        v_new = [uw_l[i][:, :GDN_DH] - ws[h][:c] for h, i in enumerate(idx)]
        o = [ws[h][c:] + _mm(attn_l[i], v_new[h]) for h, i in enumerate(idx)]
        states = [states[h] * egl_l[i] + _dot_tn(kd_l[i].astype(BF16), v_new[h].astype(BF16))
                  for h, i in enumerate(idx)]
        for h in range(GDN_HEADS):
            ln = slice(h * GDN_DH, (h + 1) * GDN_DH)
            on = o[h] * lax.rsqrt(jnp.mean(o[h] * o[h], axis=-1, keepdims=True) + EPS) * on_ref[...]
            o_ref[0, sl, ln] = (on * _silu(z_ref[0, sl, ln])).astype(o_ref.dtype)
    for h in range(GDN_HEADS):
        s_scr[h] = states[h]


def gated_delta_net(x, small, z, conv_w, a_log, dt_bias, on_gain, tc=512):
    b, t, _ = x.shape
    w = GDN_W
    ega = np.zeros((LANES, w), np.float32)
    egb = np.zeros((LANES, w), np.float32)
    for h in range(GDN_HEADS):
        ega[FOX_HEADS + h, h * GDN_DH:(h + 1) * GDN_DH] = 1.0
        egb[FOX_HEADS + GDN_HEADS + h, h * GDN_DH:(h + 1) * GDN_DH] = 1.0
    alog = jnp.repeat(a_log, GDN_DH).reshape(1, w)
    dtb = jnp.repeat(dt_bias, GDN_DH).reshape(1, w)
    idx = np.arange(tc)
    tril = ((idx[:, None] >= idx[None, :]) & (idx[:, None] // GDN_BLOCK == idx[None, :] // GDN_BLOCK))
    row = lambda n: pl.BlockSpec((1, tc, n), lambda i, j: (i, j, 0))
    return pl.pallas_call(
        functools.partial(_gdn_kernel, tc=tc),
        grid=(b, t // tc),
        in_specs=[row(3 * w), row(LANES), row(w), _full((GDN_CONV, 3 * w)), _full((LANES, w)), _full((LANES, w)),
                  _full((1, w)), _full((1, w)), _full((1, GDN_DH)), _full((tc, tc))],
        out_specs=row(w),
        out_shape=jax.ShapeDtypeStruct((b, t, w), BF16),
        scratch_shapes=[pltpu.VMEM((GDN_HEADS, GDN_DH, GDN_DH), F32), pltpu.VMEM((8, 3 * w), F32)],
        compiler_params=_cparams(("arbitrary", "arbitrary")),
        name="gdn",
    )(x, small, z, conv_w, jnp.asarray(ega), jnp.asarray(egb), alog, dtb, on_gain.reshape(1, GDN_DH),
      jnp.asarray(tril.astype(np.float32)))


def even_mixer(h, w_in, fox_fb, fox_qn, fox_kn, gdn_conv, gdn_a_log, gdn_dt_bias, gdn_on):
    cuts = np.cumsum((0,) + EV_SIZES)
    col = lambda i: w_in[:, cuts[i]:cuts[i + 1]]
    wb = lambda a: a.astype(BF16)
    bd = _block_diag_ones(FOX_W, FOX_DH)
    ep = _head_norm_epilogue(FOX_DH)
    qg = (jnp.tile(fox_qn, FOX_HEADS) * (FOX_DH ** -0.5 * LOG2E)).reshape(1, FOX_W)
    kg = jnp.tile(fox_kn, FOX_HEADS).reshape(1, FOX_W)
    fq = proj(h, wb(col(0)), BF16, ep, (bd, qg), name="proj_fq")
    fk = proj(h, wb(col(1)), BF16, ep, (bd, kg), name="proj_fk")
    fv = proj(h, wb(col(2)), BF16, name="proj_fv")
    w_small = jnp.zeros((w_in.shape[0], LANES), F32)
    w_small = w_small.at[:, 0:8].set(col(3)).at[:, 8:12].set(col(5)).at[:, 12:16].set(col(6))
    small = proj(h, wb(w_small), F32, name="proj_ev_small")
    gqkv = proj(h, wb(col(4)), F32, name="proj_gqkv")
    gz = proj(h, wb(col(7)), F32, name="proj_gz")
    cum_t = fox_decay(small, fox_fb)
    o_fox = fox_attention(fq, fk, fv, cum_t)
    o_gdn = gated_delta_net(gqkv, small, gz, gdn_conv, gdn_a_log, gdn_dt_bias, gdn_on)
    return o_fox, o_gdn


def _t5_bucket_np(dist):
    n = np.maximum(dist, 0)
    exact = REL_BUCKETS // 2
    nf = np.maximum(n, 1).astype(np.float32)
    large = exact + (np.log(nf / np.float32(exact)) / np.float32(math.log(REL_MAX_DIST / exact))
                     * np.float32(REL_BUCKETS - exact)).astype(np.int32)
    large = np.minimum(large, REL_BUCKETS - 1)
    return np.where(n < exact, n, large)


def _bias_table(rel_bias, dist, valid):
    shifted = rel_bias - rel_bias[REL_BUCKETS - 1:REL_BUCKETS]
    bucket = jnp.asarray(_t5_bucket_np(dist).astype(np.int32))
    onehot = (bucket[..., None] == jnp.arange(REL_BUCKETS, dtype=jnp.int32)).astype(F32)
    tb = jnp.einsum("...k,kh->h...", onehot, shifted * LOG2E, precision=HI)
    return jnp.where(jnp.asarray(valid)[None], tb, NEG)


def _cmp_kernel(r_ref, pos_ref, w1_ref, w2_ref, kn_ref, o_ref):
    m = r_ref.shape[3]
    half = r_ref.shape[4]
    r = r_ref[0, 0, 0].astype(BF16)
    a = _dot(r, w1_ref[0, :half, :])
    bm = _dot(r, w1_ref[0, half:, :])
    c = _dot(pos_ref[0].astype(BF16), w1_ref[0])
    hid = a + pltpu.roll(bm, m - 1, 0) + c[0:1, :]
    out = _dot(_silu(hid).astype(BF16), w2_ref[0])
    normed = out * lax.rsqrt(jnp.mean(out * out, axis=-1, keepdims=True) + EPS) * kn_ref[...]
    o_ref[0, 0, 0] = jnp.where(pl.program_id(0) == 0, normed, out).astype(o_ref.dtype)


def nsa_compress(kcvc, pos, w1, w2, kn):
    b, t, _ = kcvc.shape
    m = t // CMP_STRIDE
    half = CMP_STRIDE * NSA_DH
    r = kcvc.reshape(b, m, CMP_STRIDE, 2, NSA_KV_HEADS, NSA_DH).transpose(3, 0, 4, 1, 2, 5).reshape(2, b, 2, m, half)
    posf = jnp.zeros((2, 8, 2 * half), F32).at[:, 0].set(pos.reshape(2, 2 * half))
    w2d = jnp.concatenate([w2, w2], axis=-1).astype(BF16)
    knd = jnp.tile(kn, 2).reshape(1, LANES)
    return pl.pallas_call(
        _cmp_kernel,
        grid=(2, b, NSA_KV_HEADS),
        in_specs=[pl.BlockSpec((1, 1, 1, m, half), lambda s, i, k: (s, i, k, 0, 0)),
                  pl.BlockSpec((1, 8, 2 * half), lambda s, i, k: (s, 0, 0)),
                  pl.BlockSpec((1, 2 * half, CMP_HIDDEN), lambda s, i, k: (s, 0, 0)),
                  pl.BlockSpec((1, CMP_HIDDEN, LANES), lambda s, i, k: (s, 0, 0)),
                  _full((1, LANES))],
        out_specs=pl.BlockSpec((1, 1, 1, m, LANES), lambda s, i, k: (s, i, k, 0, 0)),
        out_shape=jax.ShapeDtypeStruct((2, b, NSA_KV_HEADS, m, LANES), BF16),
        compiler_params=_cparams(("arbitrary", "arbitrary", "arbitrary")),
        name="nsa_compress",
    )(r, posf, w1.astype(BF16), w2d, knd)


def _dot_split(a, b):
    hi = a.astype(BF16)
    lo = (a - hi.astype(F32)).astype(BF16)
    return _dot(hi, b) + _dot(lo, b)


def _head_q(q_ref, hh, lo):
    blk = q_ref[0, :, (hh // 2) * LANES:(hh // 2 + 1) * LANES]
    keep = lo if hh % 2 == 0 else jnp.logical_not(lo)
    return jnp.where(keep, blk, jnp.zeros_like(blk))


def _pair_heads(o, lo):
    return jnp.concatenate([jnp.where(lo, o[0], o[1]), jnp.where(lo, o[2], o[3])], axis=1)


def _nsa_sel_kernel(q_ref, kc_ref, vc_ref, ov_ref, bt_ref, o_ref, sel_ref, *, tq, nband, n_slc):
    i = pl.program_id(2)
    ncp = kc_ref.shape[3]
    nsp = sel_ref.shape[3]
    per = tq // CMP_STRIDE
    var = jnp.minimum(i, 1)
    bs = pl.multiple_of(per * jnp.maximum(i - 1, 0), per)
    lo = lax.broadcasted_iota(jnp.int32, (1, LANES), 1) < NSA_DH
    kc = kc_ref[0, 0, 0]
    vc = vc_ref[0, 0, 0]
    kcb = kc_ref[0, 0, 0, pl.ds(bs, nband), :]
    vcb = vc_ref[0, 0, 0, pl.ds(bs, nband), :]
    far_ok = lax.broadcasted_iota(jnp.int32, (1, ncp), 1) < per * (i - 1)
    hs = range(NSA_GROUP)
    qh = [_head_q(q_ref, hh, lo) for hh in hs]
    s_far = [jnp.where(far_ok, _dot_nt(q, kc), NEG) for q in qh]
    s_band = [_dot_nt(qh[hh], kcb) + bt_ref[var, hh] for hh in hs]
    m = [jnp.maximum(jnp.max(a, axis=1, keepdims=True), jnp.max(b, axis=1, keepdims=True))
         for a, b in zip(s_far, s_band)]
    m = [jnp.where(x < 0.5 * NEG, 0.0, x) for x in m]
    p_far = [jnp.exp2(a - x) for a, x in zip(s_far, m)]
    p_band = [jnp.exp2(b - x) for b, x in zip(s_band, m)]
    l = [jnp.sum(a, axis=1, keepdims=True) + jnp.sum(b, axis=1, keepdims=True) for a, b in zip(p_far, p_band)]
    inv = [1.0 / jnp.where(x == 0.0, 1.0, x) for x in l]
    outs = [(_dot(a.astype(BF16), vc) + _dot(b.astype(BF16), vcb)) * x for a, b, x in zip(p_far, p_band, inv)]
    ps_far = p_far[0] * inv[0]
    ps_band = p_band[0] * inv[0]
    for hh in range(1, NSA_GROUP):
        ps_far = ps_far + p_far[hh] * inv[hh]
        ps_band = ps_band + p_band[hh] * inv[hh]
    o_ref[0] = _pair_heads(outs, lo).astype(o_ref.dtype)

    imp = _dot_split(ps_far, ov_ref[...]) + _dot_split(ps_band, ov_ref[pl.ds(bs, nband), :])
    blk = lax.broadcasted_iota(jnp.int32, (1, nsp), 1)
    blk_f = blk.astype(F32)
    qpos = i * tq + lax.broadcasted_iota(jnp.int32, (tq, 1), 0)
    cur = lax.shift_right_logical(qpos, int(math.log2(SLC_LEN)))
    forced = (blk == 0) | (blk == cur) | (blk == cur - 1)
    work = jnp.where(forced, FORCE_SCORE, jnp.where(blk <= cur, imp, NEG))
    work = jnp.where(blk < n_slc, work, -jnp.inf)
    ngrp = 4
    rg = tq // ngrp
    works = [work[r * rg:(r + 1) * rg] for r in range(ngrp)]
    sels = [jnp.zeros((rg, nsp), F32) for _ in range(ngrp)]
    for _ in range(min(SLC_TOPK, n_slc)):
        ms = [jnp.max(w, axis=1, keepdims=True) for w in works]
        firsts = [jnp.min(jnp.where(w == m, blk_f, float(nsp)), axis=1, keepdims=True) for w, m in zip(works, ms)]
        picks = [blk_f == f for f in firsts]
        sels = [jnp.where(p, 1.0, s) for p, s in zip(picks, sels)]
        works = [jnp.where(p, -jnp.inf, w) for p, w in zip(picks, works)]
    sel_ref[0, 0] = jnp.concatenate(sels, axis=0).astype(sel_ref.dtype)


def nsa_select(q, cmp_kv, rel_bias, tq=512):
    b, t, _ = q.shape
    ncp = t // CMP_STRIDE
    n_cmp = ncp - 1
    n_slc = t // SLC_LEN
    nsp = max(LANES, n_slc)
    per = tq // CMP_STRIDE
    nband = 2 * per
    n = np.arange(ncp)[:, None]
    s = np.arange(nsp)[None, :]
    ov = ((CMP_STRIDE * n < SLC_LEN * s + SLC_LEN) & (CMP_STRIDE * n + CMP_LEN > SLC_LEN * s)
          & (n < n_cmp) & (s < n_slc)).astype(np.float32)
    qi = np.arange(tq)[:, None]
    nj = np.arange(nband)[None, :]
    end = CMP_STRIDE * nj + CMP_LEN - 1
    dist = np.stack([qi - end, tq + qi - end])
    bt = _bias_table(rel_bias, dist, dist >= 0)
    bt = bt.reshape(NSA_KV_HEADS, NSA_GROUP, 2, tq, nband).transpose(0, 2, 1, 3, 4)
    bt = bt.reshape(NSA_KV_HEADS * 2, NSA_GROUP, tq, nband)
    gw = NSA_GROUP * NSA_DH
    return pl.pallas_call(
        functools.partial(_nsa_sel_kernel, tq=tq, nband=nband, n_slc=n_slc),
        grid=(b, NSA_KV_HEADS, t // tq),
        in_specs=[pl.BlockSpec((1, tq, gw), lambda bi, k, i: (bi, i, k)),
                  pl.BlockSpec((1, 1, 1, ncp, LANES), lambda bi, k, i: (0, bi, k, 0, 0)),
                  pl.BlockSpec((1, 1, 1, ncp, LANES), lambda bi, k, i: (1, bi, k, 0, 0)),
                  _full((ncp, nsp)),
                  pl.BlockSpec((2, NSA_GROUP, tq, nband), lambda bi, k, i: (k, 0, 0, 0))],
        out_specs=[pl.BlockSpec((1, tq, gw), lambda bi, k, i: (bi, i, k)),
                   pl.BlockSpec((1, 1, tq, nsp), lambda bi, k, i: (bi, k, i, 0))],
        out_shape=[jax.ShapeDtypeStruct((b, t, NSA_W), BF16),
                   jax.ShapeDtypeStruct((b, NSA_KV_HEADS, t, nsp), BF16)],
        compiler_params=_cparams(("arbitrary", "arbitrary", "arbitrary")),
        name="nsa_select",
    )(q, cmp_kv, cmp_kv, jnp.asarray(ov, dtype=BF16), bt)


def _nsa_main_kernel(q_ref, ks_ref, vs_ref, kw0_ref, kw1_ref, kw2_ref, vw0_ref, vw1_ref, vw2_ref, sel_ref,
                     ocmp_ref, gate_ref, tb_ref, wm_ref, eg_ref, o_ref, *, tq):
    i = pl.program_id(2)
    nsp = sel_ref.shape[3]
    g = NSA_GROUP
    lo = lax.broadcasted_iota(jnp.int32, (1, LANES), 1) < NSA_DH
    qst = jnp.concatenate([_head_q(q_ref, hh, lo) for hh in range(g)], axis=0)
    sel = sel_ref[0, 0]
    blk_row = lax.broadcasted_iota(jnp.int32, (nsp, 1), 0)
    causal = (lax.broadcasted_iota(jnp.int32, (tq, tq), 1) <= lax.broadcasted_iota(jnp.int32, (tq, tq), 0))
    one = jnp.ones((1, LANES), BF16)

    def sel_step(jt, carry, near, tw):
        m, acc = carry
        start = pl.multiple_of(jnp.maximum(jt, 0) * tw, tw)
        kt = ks_ref[0, pl.ds(start, tw), :]
        vt = jnp.where(lo, vs_ref[0, pl.ds(start, tw), :], one)
        col_blk = lax.shift_right_logical(lax.broadcasted_iota(jnp.int32, (1, tw), 1), int(math.log2(SLC_LEN)))
        expand = jnp.where(blk_row - (tw // SLC_LEN) * jt == col_blk, 1.0, 0.0).astype(BF16)
        keep = _dot(sel, expand)
        madd = (keep - 1.0) * (-NEG)
        if near == 2:
            madd = jnp.where(causal, madd, NEG)
        s = _dot_nt(qst, kt).reshape(g, tq, tw) + madd[None]
        if near is not None:
            s = s + tb_ref[:, :, near * tq:(near + 1) * tq]
        s = s.reshape(g * tq, tw)
        m_new = jnp.maximum(m, jnp.max(s, axis=1, keepdims=True))
        p = jnp.exp2(s - m_new)
        acc = jnp.exp2(m - m_new) * acc + _dot(p.astype(BF16), vt)
        return m_new, acc

    n_far = jnp.maximum(i - 2, 0)
    carry = (jnp.full((g * tq, 1), NEG, F32), jnp.zeros((g * tq, LANES), F32))
    carry = lax.fori_loop(0, lax.shift_right_logical(n_far, 1), lambda j, c: sel_step(j, c, None, 2 * tq), carry)
    carry = sel_step(jnp.where((n_far & 1) == 1, i - 3, -nsp), carry, None, tq)
    for near in range(3):
        carry = sel_step(i - 2 + near, carry, near, tq)
    o_slc = (carry[1] / pltpu.roll(carry[1], NSA_DH, 1)).reshape(g, tq, LANES)

    var = jnp.minimum(i, 2)
    kws = (kw0_ref, kw1_ref, kw2_ref)
    vws = (vw0_ref, vw1_ref, vw2_ref)
    sw = []
    for near in range(3):
        s = _dot_nt(qst, kws[near][0]).reshape(g, tq, tq)
        s = s + tb_ref[:, :, near * tq:(near + 1) * tq] + wm_ref[var, :, near * tq:(near + 1) * tq][None]
        sw.append(s.reshape(g * tq, tq))
    m = jnp.maximum(jnp.maximum(jnp.max(sw[0], axis=1, keepdims=True), jnp.max(sw[1], axis=1, keepdims=True)),
                    jnp.max(sw[2], axis=1, keepdims=True))
    acc = jnp.zeros((g * tq, LANES), F32)
    for near in range(3):
        p = jnp.exp2(sw[near] - m)
        acc = acc + _dot(p.astype(BF16), jnp.where(lo, vws[near][0], one))
    o_win = (acc / pltpu.roll(acc, NSA_DH, 1)).reshape(g, tq, LANES)

    pair = lambda o: jnp.concatenate([jnp.where(lo, o[0], pltpu.roll(o[1], NSA_DH, 1)),
                                      jnp.where(lo, o[2], pltpu.roll(o[3], NSA_DH, 1))], axis=1)
    gates = _dot_hi(_sigmoid(gate_ref[0]), eg_ref[0])
    gw = g * NSA_DH
    out = (gates[:, 0:gw] * ocmp_ref[0].astype(F32)
           + gates[:, gw:2 * gw] * pair(o_slc) + gates[:, 2 * gw:3 * gw] * pair(o_win))
    o_ref[0] = out.astype(o_ref.dtype)


def nsa_main(q, ksw, vsw, sel, o_cmp, small, rel_bias, tq=256):
    b, t, _ = q.shape
    nsp = sel.shape[-1]
    g = NSA_GROUP
    gw = g * NSA_DH
    qi = np.arange(tq)[:, None]
    c = np.arange(3 * tq)[None, :]
    dist = qi + 2 * tq - c
    tb = _bias_table(rel_bias, dist, np.ones_like(dist, bool))
    wm = np.zeros((3, tq, 3 * tq), np.float32)
    for var in range(3):
        exists = c >= tq * (2 - var)
        wm[var] = np.where((dist >= 0) & (dist < WINDOW) & exists, 0.0, NEG)
    eg = np.zeros((NSA_KV_HEADS, LANES, 3 * gw), np.float32)
    for k in range(NSA_KV_HEADS):
        for hh in range(g):
            for br in range(3):
                eg[k, (k * g + hh) * 3 + br, br * gw + hh * NSA_DH:br * gw + (hh + 1) * NSA_DH] = 1.0
    near = lambda off, col: pl.BlockSpec(
        (1, tq, LANES), lambda bi, k, i: (bi, jnp.maximum(i - off, 0), col + k))
    return pl.pallas_call(
        functools.partial(_nsa_main_kernel, tq=tq),
        grid=(b, NSA_KV_HEADS, t // tq),
        in_specs=[pl.BlockSpec((1, tq, gw), lambda bi, k, i: (bi, i, k)),
                  pl.BlockSpec((1, t, LANES), lambda bi, k, i: (bi, 0, k)),
                  pl.BlockSpec((1, t, LANES), lambda bi, k, i: (bi, 0, k)),
                  near(2, 2), near(1, 2), near(0, 2), near(2, 2), near(1, 2), near(0, 2),
                  pl.BlockSpec((1, 1, tq, nsp), lambda bi, k, i: (bi, k, i, 0)),
                  pl.BlockSpec((1, tq, gw), lambda bi, k, i: (bi, i, k)),
                  pl.BlockSpec((1, tq, LANES), lambda bi, k, i: (bi, i, 0)),
                  pl.BlockSpec((g, tq, 3 * tq), lambda bi, k, i: (k, 0, 0)),
                  _full((3, tq, 3 * tq)),
                  pl.BlockSpec((1, LANES, 3 * gw), lambda bi, k, i: (k, 0, 0))],
        out_specs=pl.BlockSpec((1, tq, gw), lambda bi, k, i: (bi, i, k)),
        out_shape=jax.ShapeDtypeStruct((b, t, NSA_W), BF16),
        compiler_params=_cparams(("arbitrary", "arbitrary", "arbitrary")),
        name="nsa_main",
    )(q, ksw, vsw, ksw, ksw, ksw, vsw, vsw, vsw, sel, o_cmp, small, tb, jnp.asarray(wm), jnp.asarray(eg))


def _gla_kernel(qk_ref, v_ref, r_ref, sm_ref, wg_ref, bg_ref, on_ref, tril_ref, o_ref, s_scr, q_scr, k_scr, kd_scr,
                *, tc):
    c = GLA_CHUNK

    @pl.when(pl.program_id(1) == 0)
    def _():
        s_scr[...] = jnp.zeros_like(s_scr)

    kw = GLA_KW
    log_a = _log_sigmoid(_dot_hi(sm_ref[0], wg_ref[...]) + bg_ref[...]) * (1.0 / GLA_TAU)
    gcum = _dot_hi(tril_ref[...], log_a)
    q = qk_ref[0, :, 0:kw] * (GLA_DK ** -0.5)
    k = qk_ref[0, :, kw:2 * kw]
    q_scr[...] = q * jnp.exp(gcum)
    k_scr[...] = k * jnp.exp(-gcum)
    ri = lax.broadcasted_iota(jnp.int32, (c, c), 0)
    ci = lax.broadcasted_iota(jnp.int32, (c, c), 1)
    causal = ci <= ri
    lo = lax.broadcasted_iota(jnp.int32, (1, LANES), 1) < GLA_DK

    def chunk(n, _):
        sl = pl.ds(pl.multiple_of(n * c, c), c)
        for p in range(GLA_HEADS // 2):
            pl_ = slice(p * LANES, (p + 1) * LANES)
            qd = q_scr[sl, pl_]
            ki = k_scr[sl, pl_]
            kd = kd_scr[sl, pl_]
            for hh in range(2):
                h = 2 * p + hh
                keep = lo if hh == 0 else jnp.logical_not(lo)
                qm = jnp.where(keep, qd, 0.0).astype(BF16)
                v = v_ref[0, sl, h * GLA_DV:(h + 1) * GLA_DV]
                attn = jnp.where(causal, _dot_nt(qm, ki.astype(BF16)), 0.0)
                st = s_scr[h]
                o = _dot(attn.astype(BF16), v.astype(BF16)) + _dot_nt(qm, st.astype(BF16))
                on = o * lax.rsqrt(jnp.mean(o * o, axis=-1, keepdims=True) + EPS) * on_ref[...]
                o_ref[0, sl, h * GLA_DV:(h + 1) * GLA_DV] = (
                    on * _silu(r_ref[0, sl, h * GLA_DV:(h + 1) * GLA_DV])).astype(o_ref.dtype)
                s_scr[h] = st * kd_scr[pl.ds(tc + n * 8, 1), pl_] + _dot_tn(v.astype(BF16), kd.astype(BF16))
        return 0

    for n in range(tc // c):
        gl = gcum[n * c + c - 1:n * c + c, :]
        kd_scr[n * c:(n + 1) * c, :] = k[n * c:(n + 1) * c, :] * jnp.exp(gl - gcum[n * c:(n + 1) * c, :])
        kd_scr[tc + n * 8:tc + n * 8 + 8, :] = jnp.broadcast_to(jnp.exp(gl), (8, kw))
    lax.fori_loop(0, tc // c, chunk, 0)


def gated_linear_attention(qkvr, small, wg_up, bg, on_gain, tc=512):
    b, t, _ = qkvr.shape
    wg = jnp.zeros((LANES, GLA_KW), F32).at[3 * NSA_HEADS:3 * NSA_HEADS + GLA_GATE_RANK].set(wg_up)
    idx = np.arange(tc)
    tril = ((idx[:, None] >= idx[None, :]) & (idx[:, None] // GLA_CHUNK == idx[None, :] // GLA_CHUNK))
    nchunk = tc // GLA_CHUNK
    return pl.pallas_call(
        functools.partial(_gla_kernel, tc=tc),
        grid=(b, t // tc),
        in_specs=[pl.BlockSpec((1, tc, 2 * GLA_KW), lambda i, j: (i, j, 0)),
                  pl.BlockSpec((1, tc, GLA_W), lambda i, j: (i, j, 1)),
                  pl.BlockSpec((1, tc, GLA_W), lambda i, j: (i, j, 2)),
                  pl.BlockSpec((1, tc, LANES), lambda i, j: (i, j, 0)),
                  _full((LANES, GLA_KW)), _full((1, GLA_KW)), _full((1, GLA_DV)), _full((tc, tc))],
        out_specs=pl.BlockSpec((1, tc, GLA_W), lambda i, j: (i, j, 0)),
        out_shape=jax.ShapeDtypeStruct((b, t, GLA_W), BF16),
        scratch_shapes=[pltpu.VMEM((GLA_HEADS, GLA_DV, LANES), F32), pltpu.VMEM((tc, GLA_KW), F32),
                        pltpu.VMEM((tc, GLA_KW), F32), pltpu.VMEM((tc + 8 * nchunk, GLA_KW), F32)],
        compiler_params=_cparams(("arbitrary", "arbitrary")),
        name="gla",
    )(qkvr, qkvr, qkvr, small, wg, bg.reshape(1, GLA_KW), on_gain.reshape(1, GLA_DV),
      jnp.asarray(tril.astype(np.float32)))


def odd_mixer(h, w_in, nsa_qn, nsa_kn, nsa_pos, nsa_cmp_w1, nsa_cmp_w2, gla_wg_up, gla_bg, gla_on, rel_bias):
    cuts = np.cumsum((0,) + OD_SIZES)
    col = lambda i: w_in[:, cuts[i]:cuts[i + 1]]
    wb = lambda a: a.astype(BF16)
    dup = lambda a: jnp.concatenate([a[:, :NSA_DH], a[:, :NSA_DH], a[:, NSA_DH:], a[:, NSA_DH:]], axis=1)
    ep = _head_norm_epilogue(NSA_DH)
    bd = _block_diag_ones(NSA_W, NSA_DH)
    qg = (jnp.tile(nsa_qn, NSA_HEADS) * (NSA_DH ** -0.5 * LOG2E)).reshape(1, NSA_W)
    kg = jnp.tile(nsa_kn, NSA_HEADS).reshape(1, NSA_W)
    nq = proj(h, wb(col(0)), BF16, ep, (bd, qg), name="proj_nq")
    kcvc = proj(h, wb(jnp.concatenate([col(1), col(2)], axis=1)), F32, name="proj_kcvc")
    ksw = proj(h, wb(jnp.concatenate([dup(col(3)), dup(col(5))], axis=1)), BF16, ep, (bd, kg), name="proj_ksw")
    vsw = proj(h, wb(jnp.concatenate([dup(col(4)), dup(col(6))], axis=1)), BF16, name="proj_vsw")
    w_small = jnp.zeros((w_in.shape[0], LANES), F32)
    w_small = w_small.at[:, 0:24].set(col(7)).at[:, 24:40].set(col(11))
    small = proj(h, wb(w_small), F32, name="proj_od_small")
    qkvr = proj(h, wb(jnp.concatenate([col(8), col(9), col(10), col(12)], axis=1)), F32, name="proj_gla")
    cmp_kv = nsa_compress(kcvc, nsa_pos, nsa_cmp_w1, nsa_cmp_w2, nsa_kn)
    o_cmp, sel = nsa_select(nq, cmp_kv, rel_bias)
    o_nsa = nsa_main(nq, ksw, vsw, sel, o_cmp, small, rel_bias)
    o_gla = gated_linear_attention(qkvr, small, gla_wg_up, gla_bg, gla_on)
    return o_nsa, o_gla


MOE_TM = 256
MOE_ROWS = 256


def _first_index(mask_val, idx, big, axis):
    return jnp.min(jnp.where(mask_val, idx, big), axis=axis, keepdims=True)


def _route_kernel(h_ref, rt_ref, b_ref, up_ref, eid_ref, rank_ref, w_ref, cnt_ref, run):
    tm = h_ref.shape[0]
    ne = N_EXPERTS
    gsz = ne // N_GROUPS

    @pl.when(pl.program_id(0) == 0)
    def _():
        run[...] = jnp.zeros_like(run)

    scores = _sigmoid(_dot_nt(rt_ref[...], h_ref[...], HI))
    biased = scores + b_ref[...]
    b3 = biased.reshape(N_GROUPS, gsz, tm)
    i3 = lax.broadcasted_iota(jnp.int32, (1, gsz, 1), 1).astype(F32)
    m1 = jnp.max(b3, axis=1, keepdims=True)
    f1 = _first_index(b3 == m1, i3, float(gsz), 1)
    m2 = jnp.max(jnp.where(i3 == f1, -jnp.inf, b3), axis=1, keepdims=True)
    gs = (m1 + m2).reshape(N_GROUPS, tm)
    gidx = lax.broadcasted_iota(jnp.int32, (N_GROUPS, 1), 0).astype(F32)
    gmask = jnp.zeros((N_GROUPS, tm), F32)
    for _ in range(TOPK_GROUPS):
        m = jnp.max(gs, axis=0, keepdims=True)
        pick = gidx == _first_index(gs == m, gidx, float(N_GROUPS), 0)
        gmask = jnp.where(pick, 1.0, gmask)
        gs = jnp.where(pick, -jnp.inf, gs)
    emask = jnp.broadcast_to(gmask.reshape(N_GROUPS, 1, tm), (N_GROUPS, gsz, tm)).reshape(ne, tm)
    work = jnp.where(emask > 0.5, biased, -jnp.inf)
    eidx = lax.broadcasted_iota(jnp.int32, (ne, 1), 0).astype(F32)
    picks, eids, ws = [], [], []
    for _ in range(TOP_K):
        m = jnp.max(work, axis=0, keepdims=True)
        first = _first_index(work == m, eidx, float(ne), 0)
        pick = eidx == first
        picks.append(pick)
        eids.append(first)
        ws.append(jnp.sum(jnp.where(pick, scores, 0.0), axis=0, keepdims=True))
        work = jnp.where(pick, -jnp.inf, work)
    wsum = ws[0]
    for k in range(1, TOP_K):
        wsum = wsum + ws[k]
    chosen = jnp.zeros((ne, tm), F32)
    for pick in picks:
        chosen = jnp.where(pick, 1.0, chosen)
    pos = run[...] + _dot(chosen.astype(BF16), up_ref[...])
    run[...] = run[...] + jnp.sum(chosen, axis=1, keepdims=True)
    cnt_ref[...] = run[...]
    row = lax.broadcasted_iota(jnp.int32, (8, 1), 0)
    eid_o = jnp.zeros((8, tm), F32)
    rank_o = jnp.zeros((8, tm), F32)
    w_o = jnp.zeros((LANES, tm), F32)
    rowl = lax.broadcasted_iota(jnp.int32, (LANES, 1), 0)
    for k in range(TOP_K):
        rk = jnp.sum(jnp.where(picks[k], pos, 0.0), axis=0, keepdims=True)
        eid_o = jnp.where(row == k, eids[k], eid_o)
        rank_o = jnp.where(row == k, rk, rank_o)
        w_o = jnp.where(rowl == k, ws[k] / wsum * ROUTE_SCALE, w_o)
    eid_ref[0] = eid_o.astype(jnp.int32)
    rank_ref[0] = rank_o.astype(jnp.int32)
    w_ref[...] = w_o.T


def moe_route(h2, router, e_bias, tm=MOE_TM):
    nt, d = h2.shape
    ne = N_EXPERTS
    up = jnp.asarray(np.triu(np.ones((tm, tm), np.float32), 1), dtype=BF16)
    nb = nt // tm
    return pl.pallas_call(
        _route_kernel,
        grid=(nb,),
        in_specs=[pl.BlockSpec((tm, d), lambda i: (i, 0)), _full((ne, d)), _full((ne, 1)), _full((tm, tm))],
        out_specs=[pl.BlockSpec((1, 8, tm), lambda i: (i, 0, 0)),
                   pl.BlockSpec((1, 8, tm), lambda i: (i, 0, 0)),
                   pl.BlockSpec((tm, LANES), lambda i: (i, 0)),
                   _full((ne, 1))],
        out_shape=[jax.ShapeDtypeStruct((nb, 8, tm), jnp.int32), jax.ShapeDtypeStruct((nb, 8, tm), jnp.int32),
                   jax.ShapeDtypeStruct((nt, LANES), F32), jax.ShapeDtypeStruct((ne, 1), F32)],
        scratch_shapes=[pltpu.VMEM((ne, 1), F32)],
        compiler_params=_cparams(("arbitrary",)),
        name="moe_route",
    )(h2, router.T, e_bias.reshape(ne, 1), up)


def _dispatch_kernel(dest_ref, h_ref, xs_ref, sem):
    tm = h_ref.shape[0]

    def copy(t, row):
        return pltpu.make_async_copy(h_ref.at[pl.ds(t, 1), :], xs_ref.at[pl.ds(row, 1), :], sem)

    def issue(t, _):
        for k in range(TOP_K):
            copy(t, dest_ref[0, k, t]).start()
        return 0

    def drain(t, _):
        for k in range(TOP_K):
            copy(0, 0).wait()
        return 0

    lax.fori_loop(0, tm, issue, 0, unroll=4)
    lax.fori_loop(0, tm, drain, 0, unroll=4)


def moe_dispatch(h2, dest, tm=MOE_TM):
    nt, d = h2.shape
    return pl.pallas_call(
        _dispatch_kernel,
        grid=(nt // tm,),
        in_specs=[pl.BlockSpec((1, 8, tm), lambda i: (i, 0, 0), memory_space=pltpu.SMEM),
                  pl.BlockSpec((tm, d), lambda i: (i, 0))],
        out_specs=pl.BlockSpec(memory_space=pl.ANY),
        scratch_shapes=[pltpu.SemaphoreType.DMA(())],
        out_shape=jax.ShapeDtypeStruct((nt * TOP_K, d), F32),
        compiler_params=_cparams(("arbitrary",)),
        name="moe_dispatch",
    )(dest, h2)


def _ffn_kernel(blk_ref, exp_ref, lo_ref, hi_ref, first_ref, valid_ref, x_ref, wg_ref, wu_ref, wd_ref, o_ref,
                wg_b, wu_b, wd_b):
    i = pl.program_id(0)
    rows = x_ref.shape[0]

    @pl.when((i == 0) | (exp_ref[i] != exp_ref[jnp.maximum(i - 1, 0)]))
    def _():
        wg_b[...] = wg_ref[0].astype(BF16)
        wu_b[...] = wu_ref[0].astype(BF16)
        wd_b[...] = wd_ref[0].astype(BF16)

    @pl.when(valid_ref[i] == 1)
    def _():
        x = x_ref[...].astype(BF16)
        a = _dot(x, wg_b[...])
        u = _dot(x, wu_b[...])
        y = _dot((_silu(a) * u).astype(BF16), wd_b[...])
        r = blk_ref[i] * rows + lax.broadcasted_iota(jnp.int32, (rows, 1), 0)
        y = jnp.where((r >= lo_ref[i]) & (r < hi_ref[i]), y, 0.0)

        @pl.when(first_ref[i] == 1)
        def _():
            o_ref[...] = y

        @pl.when(first_ref[i] == 0)
        def _():
            o_ref[...] = o_ref[...] + y


def _ffn_items(counts, n_rows, rows):
    ne = N_EXPERTS
    nblk = n_rows // rows
    n_items = nblk + ne - 1
    ends = jnp.cumsum(counts)
    starts = ends - counts
    first_blk = starts // rows
    last_blk = jnp.maximum(ends - 1, 0) // rows
    per_e = jnp.where(counts > 0, last_blk - first_blk + 1, 0)
    item_end = jnp.cumsum(per_e)
    item_start = item_end - per_e
    total = item_end[-1]
    i = jnp.arange(n_items, dtype=jnp.int32)
    ic = jnp.minimum(i, total - 1)
    e = jnp.sum((item_end[None, :] <= ic[:, None]).astype(jnp.int32), axis=1)
    blk = first_blk[e] + ic - item_start[e]
    lo = jnp.maximum(starts[e], blk * rows)
    hi = jnp.minimum(ends[e], (blk + 1) * rows)
    valid = (i < total).astype(jnp.int32)
    first = (lo == blk * rows).astype(jnp.int32)
    return starts, (blk.astype(jnp.int32), e, lo.astype(jnp.int32), hi.astype(jnp.int32), first, valid)


def moe_ffn_sorted(xs, items, wg, wu, wd, rows=MOE_ROWS):
    n_rows, d = xs.shape
    n_items = items[0].shape[0]
    de = wg.shape[-1]
    return pl.pallas_call(
        _ffn_kernel,
        grid_spec=pltpu.PrefetchScalarGridSpec(
            num_scalar_prefetch=6,
            grid=(n_items,),
            in_specs=[pl.BlockSpec((rows, d), lambda i, blk, e, *_: (blk[i], 0)),
                      pl.BlockSpec((1, d, de), lambda i, blk, e, *_: (e[i], 0, 0)),
                      pl.BlockSpec((1, d, de), lambda i, blk, e, *_: (e[i], 0, 0)),
                      pl.BlockSpec((1, de, d), lambda i, blk, e, *_: (e[i], 0, 0))],
            out_specs=pl.BlockSpec((rows, d), lambda i, blk, e, *_: (blk[i], 0)),
            scratch_shapes=[pltpu.VMEM((d, de), BF16), pltpu.VMEM((d, de), BF16), pltpu.VMEM((de, d), BF16)]),
        out_shape=jax.ShapeDtypeStruct((n_rows, d), F32),
        compiler_params=_cparams(("arbitrary",)),
        name="moe_ffn",
    )(*items, xs, wg, wu, wd)


def _combine_kernel(dest_ref, ys_ref, w_ref, h_ref, x_ref, g_ref, sg_ref, su_ref, sd_ref, o_ref, buf, sem):
    tm = h_ref.shape[0]

    def copy(t, k, row):
        return pltpu.make_async_copy(ys_ref.at[pl.ds(row, 1), :], buf.at[k, pl.ds(t, 1), :], sem)

    def issue(t, _):
        for k in range(TOP_K):
            copy(t, k, dest_ref[0, k, t]).start()
        return 0

    def drain(t, _):
        for k in range(TOP_K):
            copy(0, 0, 0).wait()
        return 0

    lax.fori_loop(0, tm, issue, 0, unroll=4)
    hb = h_ref[...].astype(BF16)
    y = _dot((_silu(_dot(hb, sg_ref[...])) * _dot(hb, su_ref[...])).astype(BF16), sd_ref[...])
    lax.fori_loop(0, tm, drain, 0, unroll=4)
    w = w_ref[...]
    for k in range(TOP_K):
        y = y + w[:, k:k + 1] * buf[k]
    o_ref[...] = x_ref[...] + g_ref[0] * y


def moe_combine(ys, dest, w, h2, x2, gate, sg, su, sd, seq, tm=MOE_TM):
    nt, d = h2.shape
    ds_ = sg.shape[-1]
    per_b = seq // tm
    tile = lambda: pl.BlockSpec((tm, d), lambda i: (i, 0))
    return pl.pallas_call(
        _combine_kernel,
        grid=(nt // tm,),
        in_specs=[pl.BlockSpec((1, 8, tm), lambda i: (i, 0, 0), memory_space=pltpu.SMEM),
                  pl.BlockSpec(memory_space=pl.ANY),
                  pl.BlockSpec((tm, LANES), lambda i: (i, 0)), tile(), tile(),
                  pl.BlockSpec((1, 1, d), lambda i: (i // per_b, 0, 0)),
                  _full((d, ds_)), _full((d, ds_)), _full((ds_, d))],
        out_specs=tile(),
        scratch_shapes=[pltpu.VMEM((TOP_K, tm, d), F32), pltpu.SemaphoreType.DMA(())],
        out_shape=jax.ShapeDtypeStruct((nt, d), F32),
        compiler_params=_cparams(("arbitrary",)),
        name="moe_combine",
    )(dest, ys, w, h2, x2, gate, sg.astype(BF16), su.astype(BF16), sd.astype(BF16))


def moe_layer(x, g_norm, sc, sh, gate, router, e_bias, wg, wu, wd, sg, su, sd):
    b, t, d = x.shape
    nt = b * t
    h = ln_mod(x, g_norm, sc, sh, F32)
    h2 = h.reshape(nt, d)
    eid, rank, w, counts = moe_route(h2, router, e_bias)
    starts, items = _ffn_items(counts.reshape(-1).astype(jnp.int32), nt * TOP_K, MOE_ROWS)
    hit = eid[..., None] == jnp.arange(N_EXPERTS, dtype=jnp.int32)
    dest = jnp.sum(jnp.where(hit, starts.astype(jnp.int32), 0), axis=-1) + rank
    xs = moe_dispatch(h2, dest)
    ys = moe_ffn_sorted(xs, items, wg, wu, wd)
    out = moe_combine(ys, dest, w, h2, x.reshape(nt, d), gate.reshape(b, 1, d), sg, su, sd, t)
    return out.reshape(b, t, d)


def kernel(x, c, ada_w, ada_b, norm_mix, norm_ffn, rel_bias, ev_w_in, ev_w_out, fox_fb, fox_qn, fox_kn, gdn_conv, gdn_a_log, gdn_dt_bias, gdn_on, od_w_in, od_w_out, nsa_qn, nsa_kn, nsa_pos, nsa_cmp_w1, nsa_cmp_w2, gla_wg_up, gla_bg, gla_on, moe_router, moe_bias, moe_wg, moe_wu, moe_wd, sh_wg, sh_wu, sh_wd):
    d = x.shape[-1]
    depth = ada_w.shape[0]
    mod = adaln(c, ada_w, ada_b)
    for layer in range(depth):
        sh1, sc1, g1, sh2, sc2, g2 = [mod[layer, :, i * d:(i + 1) * d] for i in range(6)]
        h = ln_mod(x, norm_mix[layer], sc1, sh1, BF16)
        j = layer // 2
        if layer % 2 == 0:
            y1, y2 = even_mixer(h, ev_w_in[j], fox_fb[j], fox_qn[j], fox_kn[j], gdn_conv[j], gdn_a_log[j],
                                gdn_dt_bias[j], gdn_on[j])
            w_out = ev_w_out[j]
        else:
            y1, y2 = odd_mixer(h, od_w_in[j], nsa_qn[j], nsa_kn[j], nsa_pos[j], nsa_cmp_w1[j], nsa_cmp_w2[j],
                               gla_wg_up[j], gla_bg[j], gla_on[j], rel_bias)
            w_out = od_w_out[j]
        x = out_proj(y1, y2, w_out, x, g1)
        x = moe_layer(x, norm_ffn[layer], sc2, sh2, g2, moe_router[layer], moe_bias[layer], moe_wg[layer],
                      moe_wu[layer], moe_wd[layer], sh_wg[layer], sh_wu[layer], sh_wd[layer])
    return x
```

```python
import functools
import math

import numpy as np
import jax
import jax.numpy as jnp
from jax import lax
from jax.experimental import pallas as pl
from jax.experimental.pallas import tpu as pltpu

F32 = jnp.float32
BF16 = jnp.bfloat16
HI = lax.Precision.HIGHEST

EPS = 1e-6
LOG2E = math.log2(math.e)
NEG = -1e30
FORCE_SCORE = 1e9

FOX_HEADS, FOX_DH = 8, 64
GDN_HEADS, GDN_DH, GDN_CONV, GDN_CHUNK = 4, 128, 4, 64
NSA_HEADS, NSA_KV_HEADS, NSA_DH = 8, 2, 64
NSA_GROUP = NSA_HEADS // NSA_KV_HEADS
CMP_LEN, CMP_STRIDE, CMP_HIDDEN = 32, 16, 256
SLC_LEN, SLC_TOPK, WINDOW = 64, 16, 512
GLA_HEADS, GLA_DK, GLA_DV, GLA_GATE_RANK, GLA_TAU, GLA_CHUNK = 4, 64, 128, 16, 16.0, 64
REL_BUCKETS, REL_MAX_DIST = 32, 128
N_EXPERTS, TOP_K, D_EXPERT, D_SHARED = 64, 6, 256, 256
N_GROUPS, TOPK_GROUPS, ROUTE_SCALE = 8, 4, 2.5

FOX_W = FOX_HEADS * FOX_DH
GDN_W = GDN_HEADS * GDN_DH
NSA_W = NSA_HEADS * NSA_DH
NSA_KV_W = NSA_KV_HEADS * NSA_DH
GLA_KW = GLA_HEADS * GLA_DK
GLA_W = GLA_HEADS * GLA_DV
EV_SIZES = (FOX_W, FOX_W, FOX_W, FOX_HEADS, 3 * GDN_W, GDN_HEADS, GDN_HEADS, GDN_W)
OD_SIZES = (NSA_W,) + (NSA_KV_W,) * 6 + (3 * NSA_HEADS, GLA_KW, GLA_KW, GLA_W, GLA_GATE_RANK, GLA_W)

LANES = 128
ROW_GROUP = 64
VMEM_LIMIT = 56 * 1024 * 1024


def _cparams(sem, flags=None):
    return pltpu.CompilerParams(dimension_semantics=sem, vmem_limit_bytes=VMEM_LIMIT, flags=flags)


def _full(shape):
    n = len(shape)
    return pl.BlockSpec(shape, lambda *_: (0,) * n)


def _dot(a, b):
    return jnp.dot(a, b, preferred_element_type=F32)


def _dot_hi(a, b):
    return jnp.dot(a, b, precision=HI, preferred_element_type=F32)


def _dot_nt(a, b, precision=None):
    return lax.dot_general(a, b, (((1,), (1,)), ((), ())), precision=precision, preferred_element_type=F32)


def _dot_tn(a, b, precision=None):
    return lax.dot_general(a, b, (((0,), (0,)), ((), ())), precision=precision, preferred_element_type=F32)


def _sigmoid(x):
    return 1.0 / (1.0 + jnp.exp(-x))


def _silu(x):
    return x * _sigmoid(x)


def _softplus(x):
    return jnp.maximum(x, 0.0) + jnp.log(1.0 + jnp.exp(-jnp.abs(x)))


def _log_sigmoid(x):
    return -_softplus(-x)


def _adaln_kernel(c_ref, w_ref, b_ref, o_ref):
    c = c_ref[...]
    o_ref[0] = _dot_hi(_silu(c), w_ref[0]) + b_ref[0]


def adaln(c, ada_w, ada_b):
    depth, d, n = ada_w.shape
    b = c.shape[0]
    cp = jnp.zeros((8, d), F32).at[:b].set(c)
    tn = 1536
    out = pl.pallas_call(
        _adaln_kernel,
        grid=(depth, n // tn),
        in_specs=[_full((8, d)),
                  pl.BlockSpec((1, d, tn), lambda l, j: (l, 0, j)),
                  pl.BlockSpec((1, 1, tn), lambda l, j: (l, 0, j))],
        out_specs=pl.BlockSpec((1, 8, tn), lambda l, j: (l, 0, j)),
        out_shape=jax.ShapeDtypeStruct((depth, 8, n), F32),
        compiler_params=_cparams(("arbitrary", "arbitrary")),
        name="adaln",
    )(cp, ada_w, ada_b.reshape(depth, 1, n))
    return out[:, :b]


def _ln_kernel(x_ref, g_ref, sc_ref, sh_ref, o_ref):
    x = x_ref[0]
    y = x * lax.rsqrt(jnp.mean(x * x, axis=-1, keepdims=True) + EPS) * g_ref[...]
    o_ref[0] = (y * (1.0 + sc_ref[0]) + sh_ref[0]).astype(o_ref.dtype)


def ln_mod(x, g, sc, sh, out_dtype, tm=512):
    b, t, d = x.shape
    return pl.pallas_call(
        _ln_kernel,
        grid=(b, t // tm),
        in_specs=[pl.BlockSpec((1, tm, d), lambda i, j: (i, j, 0)),
                  _full((1, d)),
                  pl.BlockSpec((1, 1, d), lambda i, j: (i, 0, 0)),
                  pl.BlockSpec((1, 1, d), lambda i, j: (i, 0, 0))],
        out_specs=pl.BlockSpec((1, tm, d), lambda i, j: (i, j, 0)),
        out_shape=jax.ShapeDtypeStruct((b, t, d), out_dtype),
        compiler_params=_cparams(("arbitrary", "arbitrary")),
        name="ln_mod",
    )(x, g.reshape(1, d), sc.reshape(b, 1, d), sh.reshape(b, 1, d))


def proj(h, w, out_dtype, epilogue=None, extras=(), tm=512, name="proj"):
    b, t, d = h.shape
    n = w.shape[1]

    def kern(h_ref, w_ref, *rest):
        o_ref = rest[-1]
        y = _dot(h_ref[0], w_ref[...])
        if epilogue is not None:
            y = epilogue(y, *[e[...] for e in rest[:-1]])
        o_ref[0] = y.astype(out_dtype)

    return pl.pallas_call(
        kern,
        grid=(b, t // tm),
        in_specs=[pl.BlockSpec((1, tm, d), lambda i, j: (i, j, 0)), _full((d, n))]
                 + [_full(e.shape) for e in extras],
        out_specs=pl.BlockSpec((1, tm, n), lambda i, j: (i, j, 0)),
        out_shape=jax.ShapeDtypeStruct((b, t, n), out_dtype),
        compiler_params=_cparams(("arbitrary", "arbitrary")),
        name=name,
    )(h, w, *extras)


def proj_multi(h, groups, tm=512, name="proj"):
    b, t, d = h.shape
    widths = [g[0].shape[1] for g in groups]
    starts = np.cumsum([0] + widths)
    w_cat = jnp.concatenate([g[0] for g in groups], axis=1).astype(BF16)
    extras = [e for g in groups for e in g[3]]
    n_ex = [len(g[3]) for g in groups]
    n_out = len(groups)

    def kern(h_ref, w_ref, *rest):
        ex_refs = rest[:len(extras)]
        o_refs = rest[len(extras):]
        y = _dot(h_ref[0], w_ref[...])
        pos = 0
        for gi, (_, out_dtype, epilogue, _) in enumerate(groups):
            yg = y[:, starts[gi]:starts[gi + 1]]
            if epilogue is not None:
                yg = epilogue(yg, *[e[...] for e in ex_refs[pos:pos + n_ex[gi]]])
            pos += n_ex[gi]
            o_refs[gi][0] = yg.astype(out_dtype)

    return pl.pallas_call(
        kern,
        grid=(b, t // tm),
        in_specs=[pl.BlockSpec((1, tm, d), lambda i, j: (i, j, 0)), _full((d, int(starts[-1])))]
                 + [_full(e.shape) for e in extras],
        out_specs=[pl.BlockSpec((1, tm, n), lambda i, j: (i, j, 0)) for n in widths],
        out_shape=[jax.ShapeDtypeStruct((b, t, n), g[1]) for n, g in zip(widths, groups)],
        compiler_params=_cparams(("arbitrary", "arbitrary")),
        name=name,
    )(h, w_cat, *extras)


def _head_norm_epilogue(dh):
    inv = 1.0 / dh

    def ep(y, bd, gain):
        ssq = _dot((y * y).astype(BF16), bd)
        return y * lax.rsqrt(ssq * inv + EPS) * gain

    return ep


def _block_diag_ones(n, dh):
    i = np.arange(n) // dh
    return jnp.asarray((i[:, None] == i[None, :]).astype(np.float32), dtype=BF16)


def _outproj_kernel(y1_ref, y2_ref, wa_ref, wb_ref, x_ref, g_ref, o_ref):
    y = _dot(y1_ref[0], wa_ref[...]) + _dot(y2_ref[0], wb_ref[...])
    o_ref[0] = x_ref[0] + g_ref[0] * y


def out_proj(y1, y2, w_out, x, gate, tm=512):
    b, t, d = x.shape
    n1, n2 = y1.shape[-1], y2.shape[-1]
    wa = w_out[:n1].astype(BF16)
    wb = w_out[n1:].astype(BF16)
    return pl.pallas_call(
        _outproj_kernel,
        grid=(b, t // tm),
        in_specs=[pl.BlockSpec((1, tm, n1), lambda i, j: (i, j, 0)),
                  pl.BlockSpec((1, tm, n2), lambda i, j: (i, j, 0)),
                  _full((n1, d)), _full((n2, d)),
                  pl.BlockSpec((1, tm, d), lambda i, j: (i, j, 0)),
                  pl.BlockSpec((1, 1, d), lambda i, j: (i, 0, 0))],
        out_specs=pl.BlockSpec((1, tm, d), lambda i, j: (i, j, 0)),
        out_shape=jax.ShapeDtypeStruct((b, t, d), F32),
        compiler_params=_cparams(("arbitrary", "arbitrary")),
        name="out_proj",
    )(y1, y2, wa, wb, x, gate.reshape(b, 1, d))


def _decay_kernel(s_ref, fb_ref, tril_ref, place_ref, o_ref, carry):
    @pl.when(pl.program_id(1) == 0)
    def _():
        carry[...] = jnp.zeros_like(carry)

    tm = s_ref.shape[1]
    lf = _log_sigmoid(s_ref[0] + fb_ref[...])
    cum = _dot_hi(tril_ref[...], lf) + carry[...]
    carry[...] = cum[tm - 1:tm, :]
    x = cum * LOG2E
    hi = x.astype(BF16)
    r1 = x - hi.astype(F32)
    mid = r1.astype(BF16)
    low = (r1 - mid.astype(F32)).astype(BF16)
    o_ref[0] = _dot(jnp.concatenate([hi, mid, low], axis=1), place_ref[...]).astype(o_ref.dtype)


def fox_decay(small, fox_fb, tm=512):
    b, t, _ = small.shape
    fb = jnp.zeros((1, LANES), F32).at[0, :FOX_HEADS].set(fox_fb)
    tril = jnp.asarray(np.tril(np.ones((tm, tm), np.float32)))
    place = np.zeros((3 * LANES, FOX_W), np.float32)
    for h in range(FOX_HEADS):
        for j in range(3):
            place[j * LANES + h, (h // 2) * LANES + (FOX_DH if h % 2 == 0 else 0) + j] = 1.0
    return pl.pallas_call(
        _decay_kernel,
        grid=(b, t // tm),
        in_specs=[pl.BlockSpec((1, tm, LANES), lambda i, j: (i, j, 0)), _full((1, LANES)), _full((tm, tm)),
                  _full((3 * LANES, FOX_W))],
        out_specs=pl.BlockSpec((1, tm, FOX_W), lambda i, j: (i, j, 0)),
        out_shape=jax.ShapeDtypeStruct((b, t, FOX_W), BF16),
        scratch_shapes=[pltpu.VMEM((1, LANES), F32)],
        compiler_params=_cparams(("arbitrary", "arbitrary")),
        name="fox_decay",
    )(small, fb, tril, jnp.asarray(place, dtype=BF16))


def _fox_kernel(q_ref, k_ref, v_ref, f_ref, o_ref, *, tq, tk):
    i = pl.program_id(2)
    lane = lax.broadcasted_iota(jnp.int32, (1, LANES), 1)
    lo = lane < FOX_DH
    coef = jnp.where((lane & (FOX_DH - 1)) < 3, -1.0, 0.0).astype(BF16)
    q = q_ref[0]
    qs = (jnp.where(lo, q, coef), jnp.where(lo, coef, q))
    n_full = (i * tq) // tk
    causal = (lax.broadcasted_iota(jnp.int32, (tq, tk), 1)
              <= lax.broadcasted_iota(jnp.int32, (tq, tk), 0) + (i * tq - n_full * tk))

    def scores(j):
        start = pl.multiple_of(j * tk, tk)
        kt = k_ref[0, pl.ds(start, tk), :]
        ft = f_ref[0, pl.ds(start, tk), :]
        return _dot_nt(qs[0], jnp.where(lo, kt, ft)), _dot_nt(qs[1], jnp.where(lo, ft, kt))

    one = jnp.ones((1, LANES), BF16)

    def update(j, s_pair, carry, diag):
        vt = v_ref[0, pl.ds(pl.multiple_of(j * tk, tk), tk), :]
        vs = (jnp.where(lo, vt, one), jnp.where(lo, one, vt))
        new = []
        for hh in range(2):
            m, acc = carry[hh]
            s = s_pair[hh]
            if diag:
                s = jnp.where(causal, s, NEG)
            m_new = jnp.maximum(m, jnp.max(s, axis=1, keepdims=True))
            p = jnp.exp2(s - m_new)
            acc = jnp.exp2(m - m_new) * acc + _dot(p.astype(BF16), vs[hh])
            new.append((m_new, acc))
        return tuple(new)

    init = tuple((jnp.full((tq, 1), NEG, F32), jnp.zeros((tq, LANES), F32)) for _ in range(2))
    carry = lax.fori_loop(0, n_full, lambda j, c: update(j, scores(j), c, False), init)
    carry = update(n_full, scores(n_full), carry, True)
    acc = jnp.where(lo, carry[0][1], carry[1][1])
    den = jnp.where(lo, carry[1][1], carry[0][1])
    o_ref[0] = (acc / pltpu.roll(den, FOX_DH, 1)).astype(o_ref.dtype)


def fox_attention(q, k, v, feat, tq=512, tk=1024):
    b, t, w = q.shape
    npair = w // LANES
    nt = t // tq
    tk = min(tk, t)
    whole = lambda: pl.BlockSpec((1, t, LANES), lambda bi, p, i: (bi, 0, p))
    return pl.pallas_call(
        functools.partial(_fox_kernel, tq=tq, tk=tk),
        grid=(b, npair, nt),
        in_specs=[pl.BlockSpec((1, tq, LANES), lambda bi, p, i: (bi, i, p)), whole(), whole(), whole()],
        out_specs=pl.BlockSpec((1, tq, LANES), lambda bi, p, i: (bi, i, p)),
        out_shape=jax.ShapeDtypeStruct((b, t, w), BF16),
        compiler_params=_cparams(("arbitrary", "arbitrary", "arbitrary")),
        name="fox_attn",
    )(q, k, v, feat)


def _mm(a, b):
    return _dot(a.astype(BF16), b.astype(BF16))


def _mm3(a, b):
    ah = a.astype(BF16)
    bh = b.astype(BF16)
    al = (a - ah.astype(F32)).astype(BF16)
    bl = (b - bh.astype(F32)).astype(BF16)
    return _dot(jnp.concatenate([ah, ah, al], axis=1), jnp.concatenate([bh, bl, bh], axis=0))


def _tril_solve(a, rhs, ri, ci):
    n = a[0].shape[0]
    both = lambda f, x, y: [f(p, q) for p, q in zip(x, y)]
    eye = (ri == ci).astype(F32)
    same = lambda b: (lax.shift_right_logical(ri, int(math.log2(b)))
                      == lax.shift_right_logical(ci, int(math.log2(b))))
    base = 16
    d = [jnp.where(same(base), p, 0.0) for p in a]
    d2 = both(_mm3, d, d)
    d4 = both(_mm3, d2, d2)
    r1 = [eye - p + p2 - t for p, p2, t in zip(d, d2, both(_mm3, d, d2))]
    d8 = both(_mm3, d4, d4)
    r2 = [eye + p4 + p8 + t for p4, p8, t in zip(d4, d8, both(_mm3, d4, d8))]
    t = both(_mm3, r1, r2)
    b = base
    while b < n:
        join = same(2 * b) & jnp.logical_not(same(b))
        low = [jnp.where(join, p, 0.0) for p in a]
        t = [p - q for p, q in zip(t, both(_mm3, both(_mm3, t, low), t))]
        b *= 2
    return both(_mm3, t, rhs)


GDN_BLOCK = 128


def _gdn_kernel(x_ref, sm_ref, z_ref, cw_ref, ega_ref, egb_ref, alog_ref, dtb_ref, on_ref, tril_ref,
                o_ref, s_scr, prev_scr, *, tc):
    c = GDN_BLOCK
    w = GDN_W

    @pl.when(pl.program_id(1) == 0)
    def _():
        s_scr[...] = jnp.zeros_like(s_scr)
        prev_scr[...] = jnp.zeros_like(prev_scr)

    x = x_ref[0]
    prev = prev_scr[...]
    row8 = lax.broadcasted_iota(jnp.int32, (8, 1), 0)
    acc = x * cw_ref[GDN_CONV - 1:GDN_CONV, :]
    for s in range(1, GDN_CONV):
        rolled = pltpu.roll(x, s, 0)
        head = jnp.where(row8 < s, pltpu.roll(prev, s, 0), rolled[0:8])
        shifted = jnp.concatenate([head, rolled[8:]], axis=0)
        acc = acc + shifted * cw_ref[GDN_CONV - 1 - s:GDN_CONV - s, :]
    prev_scr[...] = x[tc - 8:tc]
    xc = _silu(acc)

    sm = sm_ref[0]
    g_raw = _dot_hi(sm, ega_ref[...])
    b_raw = _dot_hi(sm, egb_ref[...])
    g = -jnp.exp(alog_ref[...]) * _softplus(g_raw + dtb_ref[...])
    beta_all = _sigmoid(b_raw)
    gc_all = _dot_hi(tril_ref[...], g)

    ri = lax.broadcasted_iota(jnp.int32, (c, c), 0)
    ci = lax.broadcasted_iota(jnp.int32, (c, c), 1)
    causal = ci <= ri
    strict = ci < ri

    nblk = tc // c
    a_l, attn_l, rhs_l, qd_l, kd_l, egl_l = [], [], [], [], [], []
    for h in range(GDN_HEADS):
        ln = slice(h * GDN_DH, (h + 1) * GDN_DH)
        qh = xc[:, h * GDN_DH:(h + 1) * GDN_DH]
        kh = xc[:, w + h * GDN_DH:w + (h + 1) * GDN_DH]
        qh = qh * lax.rsqrt(jnp.sum(qh * qh, axis=-1, keepdims=True) + EPS) * (GDN_DH ** -0.5)
        kh = kh * lax.rsqrt(jnp.sum(kh * kh, axis=-1, keepdims=True) + EPS)
        vh = xc[:, 2 * w + h * GDN_DH:2 * w + (h + 1) * GDN_DH]
        gch = gc_all[:, ln]
        gct = gch.T
        egc = jnp.exp(gch)
        bh = beta_all[:, ln]
        for n in range(nblk):
            sl = slice(n * c, (n + 1) * c)
            q, k, v, gc, be = qh[sl], kh[sl], vh[sl], gch[sl], bh[sl]
            decay = jnp.exp(jnp.where(causal, gc - gct[:, sl], NEG))
            kb = k * be
            kk = _dot_nt(jnp.concatenate([kb, q], axis=0).astype(BF16), k.astype(BF16))
            a_l.append(jnp.where(strict, kk[:c] * decay, 0.0))
            attn_l.append(jnp.where(causal, kk[c:] * decay, 0.0))
            rhs_l.append(jnp.concatenate([v * be, kb * egc[sl]], axis=1))
            gl = gc[c - 1:c, :]
            qd_l.append(q * egc[sl])
            kd_l.append(k * jnp.exp(gl - gc))
            egl_l.append(jnp.exp(gl))
    uw_l = _tril_solve(a_l, rhs_l, ri, ci)

    states = [s_scr[h] for h in range(GDN_HEADS)]
    for n in range(nblk):
        sl = slice(n * c, (n + 1) * c)
        idx = [h * nblk + n for h in range(GDN_HEADS)]
        ws = [_mm(jnp.concatenate([uw_l[i][:, GDN_DH:], qd_l[i]], axis=0), states[h])
              for h, i in enumerate(idx)]
        v_new = [uw_l[i][:, :GDN_DH] - ws[h][:c] for h, i in enumerate(idx)]
        o = [ws[h][c:] + _mm(attn_l[i], v_new[h]) for h, i in enumerate(idx)]
        states = [states[h] * egl_l[i] + _dot_tn(kd_l[i].astype(BF16), v_new[h].astype(BF16))
                  for h, i in enumerate(idx)]
        for h in range(GDN_HEADS):
            ln = slice(h * GDN_DH, (h + 1) * GDN_DH)
            on = o[h] * lax.rsqrt(jnp.mean(o[h] * o[h], axis=-1, keepdims=True) + EPS) * on_ref[...]
            o_ref[0, sl, ln] = (on * _silu(z_ref[0, sl, ln])).astype(o_ref.dtype)
    for h in range(GDN_HEADS):
        s_scr[h] = states[h]


def gated_delta_net(x, small, z, conv_w, a_log, dt_bias, on_gain, tc=512):
    b, t, _ = x.shape
    w = GDN_W
    ega = np.zeros((LANES, w), np.float32)
    egb = np.zeros((LANES, w), np.float32)
    for h in range(GDN_HEADS):
        ega[FOX_HEADS + h, h * GDN_DH:(h + 1) * GDN_DH] = 1.0
        egb[FOX_HEADS + GDN_HEADS + h, h * GDN_DH:(h + 1) * GDN_DH] = 1.0
    alog = jnp.repeat(a_log, GDN_DH).reshape(1, w)
    dtb = jnp.repeat(dt_bias, GDN_DH).reshape(1, w)
    idx = np.arange(tc)
    tril = ((idx[:, None] >= idx[None, :]) & (idx[:, None] // GDN_BLOCK == idx[None, :] // GDN_BLOCK))
    row = lambda n: pl.BlockSpec((1, tc, n), lambda i, j: (i, j, 0))
    return pl.pallas_call(
        functools.partial(_gdn_kernel, tc=tc),
        grid=(b, t // tc),
        in_specs=[row(3 * w), row(LANES), row(w), _full((GDN_CONV, 3 * w)), _full((LANES, w)), _full((LANES, w)),
                  _full((1, w)), _full((1, w)), _full((1, GDN_DH)), _full((tc, tc))],
        out_specs=row(w),
        out_shape=jax.ShapeDtypeStruct((b, t, w), BF16),
        scratch_shapes=[pltpu.VMEM((GDN_HEADS, GDN_DH, GDN_DH), F32), pltpu.VMEM((8, 3 * w), F32)],
        compiler_params=_cparams(("arbitrary", "arbitrary")),
        name="gdn",
    )(x, small, z, conv_w, jnp.asarray(ega), jnp.asarray(egb), alog, dtb, on_gain.reshape(1, GDN_DH),
      jnp.asarray(tril.astype(np.float32)))


def even_mixer(h, w_in, fox_fb, fox_qn, fox_kn, gdn_conv, gdn_a_log, gdn_dt_bias, gdn_on):
    cuts = np.cumsum((0,) + EV_SIZES)
    col = lambda i: w_in[:, cuts[i]:cuts[i + 1]]
    wb = lambda a: a.astype(BF16)
    bd = _block_diag_ones(FOX_W, FOX_DH)
    ep = _head_norm_epilogue(FOX_DH)
    qg = (jnp.tile(fox_qn, FOX_HEADS) * (FOX_DH ** -0.5 * LOG2E)).reshape(1, FOX_W)
    kg = jnp.tile(fox_kn, FOX_HEADS).reshape(1, FOX_W)
    w_small = jnp.zeros((w_in.shape[0], LANES), F32)
    w_small = w_small.at[:, 0:8].set(col(3)).at[:, 8:12].set(col(5)).at[:, 12:16].set(col(6))
    fq, fk, fv, small, gqkv, gz = proj_multi(
        h, [(col(0), BF16, ep, (bd, qg)), (col(1), BF16, ep, (bd, kg)), (col(2), BF16, None, ()),
            (w_small, F32, None, ()), (col(4), F32, None, ()), (col(7), F32, None, ())], name="proj_even")
    feat = fox_decay(small, fox_fb)
    o_fox = fox_attention(fq, fk, fv, feat)
    o_gdn = gated_delta_net(gqkv, small, gz, gdn_conv, gdn_a_log, gdn_dt_bias, gdn_on)
    return o_fox, o_gdn


def _t5_bucket_np(dist):
    n = np.maximum(dist, 0)
    exact = REL_BUCKETS // 2
    nf = np.maximum(n, 1).astype(np.float32)
    large = exact + (np.log(nf / np.float32(exact)) / np.float32(math.log(REL_MAX_DIST / exact))
                     * np.float32(REL_BUCKETS - exact)).astype(np.int32)
    large = np.minimum(large, REL_BUCKETS - 1)
    return np.where(n < exact, n, large)


def _bias_table(rel_bias, dist, valid):
    shifted = rel_bias - rel_bias[REL_BUCKETS - 1:REL_BUCKETS]
    bucket = jnp.asarray(_t5_bucket_np(dist).astype(np.int32))
    onehot = (bucket[..., None] == jnp.arange(REL_BUCKETS, dtype=jnp.int32)).astype(F32)
    tb = jnp.einsum("...k,kh->h...", onehot, shifted * LOG2E, precision=HI)
    return jnp.where(jnp.asarray(valid)[None], tb, NEG)


def _cmp_kernel(r_ref, pos_ref, w1_ref, w2_ref, kn_ref, o_ref):
    m = r_ref.shape[3]
    half = r_ref.shape[4]
    r = r_ref[0, 0, 0].astype(BF16)
    a = _dot(r, w1_ref[0, :half, :])
    bm = _dot(r, w1_ref[0, half:, :])
    c = _dot(pos_ref[0].astype(BF16), w1_ref[0])
    hid = a + pltpu.roll(bm, m - 1, 0) + c[0:1, :]
    out = _dot(_silu(hid).astype(BF16), w2_ref[0])
    normed = out * lax.rsqrt(jnp.mean(out * out, axis=-1, keepdims=True) + EPS) * kn_ref[...]
    o_ref[0, 0, 0] = jnp.where(pl.program_id(0) == 0, normed, out).astype(o_ref.dtype)


def nsa_compress(kcvc, pos, w1, w2, kn):
    b, t, _ = kcvc.shape
    m = t // CMP_STRIDE
    half = CMP_STRIDE * NSA_DH
    r = kcvc.reshape(b, m, CMP_STRIDE, 2, NSA_KV_HEADS, NSA_DH).transpose(3, 0, 4, 1, 2, 5).reshape(2, b, 2, m, half)
    posf = jnp.zeros((2, 8, 2 * half), F32).at[:, 0].set(pos.reshape(2, 2 * half))
    w2d = jnp.concatenate([w2, w2], axis=-1).astype(BF16)
    knd = jnp.tile(kn, 2).reshape(1, LANES)
    return pl.pallas_call(
        _cmp_kernel,
        grid=(2, b, NSA_KV_HEADS),
        in_specs=[pl.BlockSpec((1, 1, 1, m, half), lambda s, i, k: (s, i, k, 0, 0)),
                  pl.BlockSpec((1, 8, 2 * half), lambda s, i, k: (s, 0, 0)),
                  pl.BlockSpec((1, 2 * half, CMP_HIDDEN), lambda s, i, k: (s, 0, 0)),
                  pl.BlockSpec((1, CMP_HIDDEN, LANES), lambda s, i, k: (s, 0, 0)),
                  _full((1, LANES))],
        out_specs=pl.BlockSpec((1, 1, 1, m, LANES), lambda s, i, k: (s, i, k, 0, 0)),
        out_shape=jax.ShapeDtypeStruct((2, b, NSA_KV_HEADS, m, LANES), BF16),
        compiler_params=_cparams(("arbitrary", "arbitrary", "arbitrary")),
        name="nsa_compress",
    )(r, posf, w1.astype(BF16), w2d, knd)


def _dot_split(a, b):
    hi = a.astype(BF16)
    lo = (a - hi.astype(F32)).astype(BF16)
    return _dot(hi, b) + _dot(lo, b)


def _head_q(q_ref, hh, lo):
    blk = q_ref[0, :, (hh // 2) * LANES:(hh // 2 + 1) * LANES]
    keep = lo if hh % 2 == 0 else jnp.logical_not(lo)
    return jnp.where(keep, blk, jnp.zeros_like(blk))


def _pair_heads(o, lo):
    return jnp.concatenate([jnp.where(lo, o[0], o[1]), jnp.where(lo, o[2], o[3])], axis=1)


def _nsa_sel_kernel(q_ref, kc_ref, vc_ref, ov_ref, bt_ref, o_ref, sel_ref, *, tq, nband, n_slc):
    i = pl.program_id(2)
    ncp = kc_ref.shape[3]
    nsp = sel_ref.shape[3]
    per = tq // CMP_STRIDE
    var = jnp.minimum(i, 1)
    bs = pl.multiple_of(per * jnp.maximum(i - 1, 0), per)
    lo = lax.broadcasted_iota(jnp.int32, (1, LANES), 1) < NSA_DH
    kc = kc_ref[0, 0, 0]
    vc = vc_ref[0, 0, 0]
    kcb = kc_ref[0, 0, 0, pl.ds(bs, nband), :]
    vcb = vc_ref[0, 0, 0, pl.ds(bs, nband), :]
    far_ok = lax.broadcasted_iota(jnp.int32, (1, ncp), 1) < per * (i - 1)
    hs = range(NSA_GROUP)
    qh = [_head_q(q_ref, hh, lo) for hh in hs]
    s_far = [jnp.where(far_ok, _dot_nt(q, kc), NEG) for q in qh]
    s_band = [_dot_nt(qh[hh], kcb) + bt_ref[var, hh] for hh in hs]
    m = [jnp.maximum(jnp.max(a, axis=1, keepdims=True), jnp.max(b, axis=1, keepdims=True))
         for a, b in zip(s_far, s_band)]
    m = [jnp.where(x < 0.5 * NEG, 0.0, x) for x in m]
    p_far = [jnp.exp2(a - x) for a, x in zip(s_far, m)]
    p_band = [jnp.exp2(b - x) for b, x in zip(s_band, m)]
    l = [jnp.sum(a, axis=1, keepdims=True) + jnp.sum(b, axis=1, keepdims=True) for a, b in zip(p_far, p_band)]
    inv = [1.0 / jnp.where(x == 0.0, 1.0, x) for x in l]
    outs = [(_dot(a.astype(BF16), vc) + _dot(b.astype(BF16), vcb)) * x for a, b, x in zip(p_far, p_band, inv)]
    ps_far = p_far[0] * inv[0]
    ps_band = p_band[0] * inv[0]
    for hh in range(1, NSA_GROUP):
        ps_far = ps_far + p_far[hh] * inv[hh]
        ps_band = ps_band + p_band[hh] * inv[hh]
    o_ref[0] = _pair_heads(outs, lo).astype(o_ref.dtype)

    imp = _dot_split(ps_far, ov_ref[...]) + _dot_split(ps_band, ov_ref[pl.ds(bs, nband), :])
    blk = lax.broadcasted_iota(jnp.int32, (1, nsp), 1)
    blk_f = blk.astype(F32)
    qpos = i * tq + lax.broadcasted_iota(jnp.int32, (tq, 1), 0)
    cur = lax.shift_right_logical(qpos, int(math.log2(SLC_LEN)))
    forced = (blk == 0) | (blk == cur) | (blk == cur - 1)
    work = jnp.where(forced, FORCE_SCORE, jnp.where(blk <= cur, imp, NEG))
    work = jnp.where(blk < n_slc, work, -jnp.inf)
    ngrp = 4
    rg = tq // ngrp
    works = [work[r * rg:(r + 1) * rg] for r in range(ngrp)]
    sels = [jnp.zeros((rg, nsp), F32) for _ in range(ngrp)]
    for _ in range(min(SLC_TOPK, n_slc)):
        ms = [jnp.max(w, axis=1, keepdims=True) for w in works]
        firsts = [jnp.min(jnp.where(w == m, blk_f, float(nsp)), axis=1, keepdims=True) for w, m in zip(works, ms)]
        picks = [blk_f == f for f in firsts]
        sels = [jnp.where(p, 1.0, s) for p, s in zip(picks, sels)]
        works = [jnp.where(p, -jnp.inf, w) for p, w in zip(picks, works)]
    sel_ref[0, 0] = jnp.concatenate(sels, axis=0).astype(sel_ref.dtype)


def nsa_select(q, cmp_kv, rel_bias, tq=512):
    b, t, _ = q.shape
    ncp = t // CMP_STRIDE
    n_cmp = ncp - 1
    n_slc = t // SLC_LEN
    nsp = max(LANES, n_slc)
    per = tq // CMP_STRIDE
    nband = 2 * per
    n = np.arange(ncp)[:, None]
    s = np.arange(nsp)[None, :]
    ov = ((CMP_STRIDE * n < SLC_LEN * s + SLC_LEN) & (CMP_STRIDE * n + CMP_LEN > SLC_LEN * s)
          & (n < n_cmp) & (s < n_slc)).astype(np.float32)
    qi = np.arange(tq)[:, None]
    nj = np.arange(nband)[None, :]
    end = CMP_STRIDE * nj + CMP_LEN - 1
    dist = np.stack([qi - end, tq + qi - end])
    bt = _bias_table(rel_bias, dist, dist >= 0)
    bt = bt.reshape(NSA_KV_HEADS, NSA_GROUP, 2, tq, nband).transpose(0, 2, 1, 3, 4)
    bt = bt.reshape(NSA_KV_HEADS * 2, NSA_GROUP, tq, nband)
    gw = NSA_GROUP * NSA_DH
    return pl.pallas_call(
        functools.partial(_nsa_sel_kernel, tq=tq, nband=nband, n_slc=n_slc),
        grid=(b, NSA_KV_HEADS, t // tq),
        in_specs=[pl.BlockSpec((1, tq, gw), lambda bi, k, i: (bi, i, k)),
                  pl.BlockSpec((1, 1, 1, ncp, LANES), lambda bi, k, i: (0, bi, k, 0, 0)),
                  pl.BlockSpec((1, 1, 1, ncp, LANES), lambda bi, k, i: (1, bi, k, 0, 0)),
                  _full((ncp, nsp)),
                  pl.BlockSpec((2, NSA_GROUP, tq, nband), lambda bi, k, i: (k, 0, 0, 0))],
        out_specs=[pl.BlockSpec((1, tq, gw), lambda bi, k, i: (bi, i, k)),
                   pl.BlockSpec((1, 1, tq, nsp), lambda bi, k, i: (bi, k, i, 0))],
        out_shape=[jax.ShapeDtypeStruct((b, t, NSA_W), BF16),
                   jax.ShapeDtypeStruct((b, NSA_KV_HEADS, t, nsp), BF16)],
        compiler_params=_cparams(("arbitrary", "arbitrary", "arbitrary")),
        name="nsa_select",
    )(q, cmp_kv, cmp_kv, jnp.asarray(ov, dtype=BF16), bt)


def _nsa_main_kernel(q_ref, ks_ref, vs_ref, kw0_ref, kw1_ref, kw2_ref, vw0_ref, vw1_ref, vw2_ref, sel_ref,
                     ocmp_ref, gate_ref, tb_ref, wm_ref, eg_ref, o_ref, *, tq):
    i = pl.program_id(2)
    nsp = sel_ref.shape[3]
    g = NSA_GROUP
    lo = lax.broadcasted_iota(jnp.int32, (1, LANES), 1) < NSA_DH
    qst = jnp.concatenate([_head_q(q_ref, hh, lo) for hh in range(g)], axis=0)
    sel = sel_ref[0, 0]
    blk_row = lax.broadcasted_iota(jnp.int32, (nsp, 1), 0)
    causal = (lax.broadcasted_iota(jnp.int32, (tq, tq), 1) <= lax.broadcasted_iota(jnp.int32, (tq, tq), 0))
    one = jnp.ones((1, LANES), BF16)

    def sel_step(jt, carry, near, tw):
        m, acc = carry
        start = pl.multiple_of(jnp.maximum(jt, 0) * tw, tw)
        kt = ks_ref[0, pl.ds(start, tw), :]
        vt = jnp.where(lo, vs_ref[0, pl.ds(start, tw), :], one)
        col_blk = lax.shift_right_logical(lax.broadcasted_iota(jnp.int32, (1, tw), 1), int(math.log2(SLC_LEN)))
        expand = jnp.where(blk_row - (tw // SLC_LEN) * jt == col_blk, 1.0, 0.0).astype(BF16)
        keep = _dot(sel, expand)
        madd = (keep - 1.0) * (-NEG)
        if near == 2:
            madd = jnp.where(causal, madd, NEG)
        s = _dot_nt(qst, kt).reshape(g, tq, tw) + madd[None]
        if near is not None:
            s = s + tb_ref[:, :, near * tq:(near + 1) * tq]
        s = s.reshape(g * tq, tw)
        m_new = jnp.maximum(m, jnp.max(s, axis=1, keepdims=True))
        p = jnp.exp2(s - m_new)
        acc = jnp.exp2(m - m_new) * acc + _dot(p.astype(BF16), vt)
        return m_new, acc

    n_far = jnp.maximum(i - 2, 0)
    carry = (jnp.full((g * tq, 1), NEG, F32), jnp.zeros((g * tq, LANES), F32))
    carry = lax.fori_loop(0, lax.shift_right_logical(n_far, 1), lambda j, c: sel_step(j, c, None, 2 * tq), carry)
    carry = sel_step(jnp.where((n_far & 1) == 1, i - 3, -nsp), carry, None, tq)
    for near in range(3):
        carry = sel_step(i - 2 + near, carry, near, tq)
    o_slc = (carry[1] / pltpu.roll(carry[1], NSA_DH, 1)).reshape(g, tq, LANES)

    var = jnp.minimum(i, 2)
    kws = (kw0_ref, kw1_ref, kw2_ref)
    vws = (vw0_ref, vw1_ref, vw2_ref)
    sw = []
    for near in range(3):
        s = _dot_nt(qst, kws[near][0]).reshape(g, tq, tq)
        s = s + tb_ref[:, :, near * tq:(near + 1) * tq] + wm_ref[var, :, near * tq:(near + 1) * tq][None]
        sw.append(s.reshape(g * tq, tq))
    m = jnp.maximum(jnp.maximum(jnp.max(sw[0], axis=1, keepdims=True), jnp.max(sw[1], axis=1, keepdims=True)),
                    jnp.max(sw[2], axis=1, keepdims=True))
    acc = jnp.zeros((g * tq, LANES), F32)
    for near in range(3):
        p = jnp.exp2(sw[near] - m)
        acc = acc + _dot(p.astype(BF16), jnp.where(lo, vws[near][0], one))
    o_win = (acc / pltpu.roll(acc, NSA_DH, 1)).reshape(g, tq, LANES)

    pair = lambda o: jnp.concatenate([jnp.where(lo, o[0], pltpu.roll(o[1], NSA_DH, 1)),
                                      jnp.where(lo, o[2], pltpu.roll(o[3], NSA_DH, 1))], axis=1)
    gates = _dot_hi(_sigmoid(gate_ref[0]), eg_ref[0])
    gw = g * NSA_DH
    out = (gates[:, 0:gw] * ocmp_ref[0].astype(F32)
           + gates[:, gw:2 * gw] * pair(o_slc) + gates[:, 2 * gw:3 * gw] * pair(o_win))
    o_ref[0] = out.astype(o_ref.dtype)


def nsa_main(q, ksw, vsw, sel, o_cmp, small, rel_bias, tq=256):
    b, t, _ = q.shape
    nsp = sel.shape[-1]
    g = NSA_GROUP
    gw = g * NSA_DH
    qi = np.arange(tq)[:, None]
    c = np.arange(3 * tq)[None, :]
    dist = qi + 2 * tq - c
    tb = _bias_table(rel_bias, dist, np.ones_like(dist, bool))
    wm = np.zeros((3, tq, 3 * tq), np.float32)
    for var in range(3):
        exists = c >= tq * (2 - var)
        wm[var] = np.where((dist >= 0) & (dist < WINDOW) & exists, 0.0, NEG)
    eg = np.zeros((NSA_KV_HEADS, LANES, 3 * gw), np.float32)
    for k in range(NSA_KV_HEADS):
        for hh in range(g):
            for br in range(3):
                eg[k, (k * g + hh) * 3 + br, br * gw + hh * NSA_DH:br * gw + (hh + 1) * NSA_DH] = 1.0
    near = lambda off, col: pl.BlockSpec(
        (1, tq, LANES), lambda bi, k, i: (bi, jnp.maximum(i - off, 0), col + k))
    return pl.pallas_call(
        functools.partial(_nsa_main_kernel, tq=tq),
        grid=(b, NSA_KV_HEADS, t // tq),
        in_specs=[pl.BlockSpec((1, tq, gw), lambda bi, k, i: (bi, i, k)),
                  pl.BlockSpec((1, t, LANES), lambda bi, k, i: (bi, 0, k)),
                  pl.BlockSpec((1, t, LANES), lambda bi, k, i: (bi, 0, k)),
                  near(2, 2), near(1, 2), near(0, 2), near(2, 2), near(1, 2), near(0, 2),
                  pl.BlockSpec((1, 1, tq, nsp), lambda bi, k, i: (bi, k, i, 0)),
                  pl.BlockSpec((1, tq, gw), lambda bi, k, i: (bi, i, k)),
                  pl.BlockSpec((1, tq, LANES), lambda bi, k, i: (bi, i, 0)),
                  pl.BlockSpec((g, tq, 3 * tq), lambda bi, k, i: (k, 0, 0)),
                  _full((3, tq, 3 * tq)),
                  pl.BlockSpec((1, LANES, 3 * gw), lambda bi, k, i: (k, 0, 0))],
        out_specs=pl.BlockSpec((1, tq, gw), lambda bi, k, i: (bi, i, k)),
        out_shape=jax.ShapeDtypeStruct((b, t, NSA_W), BF16),
        compiler_params=_cparams(("arbitrary", "arbitrary", "arbitrary")),
        name="nsa_main",
    )(q, ksw, vsw, ksw, ksw, ksw, vsw, vsw, vsw, sel, o_cmp, small, tb, jnp.asarray(wm), jnp.asarray(eg))


def _gla_kernel(qk_ref, v_ref, r_ref, sm_ref, wg_ref, bg_ref, on_ref, tril_ref, o_ref, s_scr, q_scr, k_scr, kd_scr,
                *, tc):
    c = GLA_CHUNK

    @pl.when(pl.program_id(1) == 0)
    def _():
        s_scr[...] = jnp.zeros_like(s_scr)

    kw = GLA_KW
    log_a = _log_sigmoid(_dot_hi(sm_ref[0], wg_ref[...]) + bg_ref[...]) * (1.0 / GLA_TAU)
    gcum = _dot_hi(tril_ref[...], log_a)
    q = qk_ref[0, :, 0:kw] * (GLA_DK ** -0.5)
    k = qk_ref[0, :, kw:2 * kw]
    q_scr[...] = q * jnp.exp(gcum)
    k_scr[...] = k * jnp.exp(-gcum)
    ri = lax.broadcasted_iota(jnp.int32, (c, c), 0)
    ci = lax.broadcasted_iota(jnp.int32, (c, c), 1)
    causal = ci <= ri
    lo = lax.broadcasted_iota(jnp.int32, (1, LANES), 1) < GLA_DK

    def chunk(n, _):
        sl = pl.ds(pl.multiple_of(n * c, c), c)
        for p in range(GLA_HEADS // 2):
            pl_ = slice(p * LANES, (p + 1) * LANES)
            qd = q_scr[sl, pl_]
            ki = k_scr[sl, pl_]
            kd = kd_scr[sl, pl_]
            for hh in range(2):
                h = 2 * p + hh
                keep = lo if hh == 0 else jnp.logical_not(lo)
                qm = jnp.where(keep, qd, 0.0).astype(BF16)
                v = v_ref[0, sl, h * GLA_DV:(h + 1) * GLA_DV]
                attn = jnp.where(causal, _dot_nt(qm, ki.astype(BF16)), 0.0)
                st = s_scr[h]
                o = _dot(attn.astype(BF16), v.astype(BF16)) + _dot_nt(qm, st.astype(BF16))
                on = o * lax.rsqrt(jnp.mean(o * o, axis=-1, keepdims=True) + EPS) * on_ref[...]
                o_ref[0, sl, h * GLA_DV:(h + 1) * GLA_DV] = (
                    on * _silu(r_ref[0, sl, h * GLA_DV:(h + 1) * GLA_DV])).astype(o_ref.dtype)
                s_scr[h] = st * kd_scr[pl.ds(tc + n * 8, 1), pl_] + _dot_tn(v.astype(BF16), kd.astype(BF16))
        return 0

    for n in range(tc // c):
        gl = gcum[n * c + c - 1:n * c + c, :]
        kd_scr[n * c:(n + 1) * c, :] = k[n * c:(n + 1) * c, :] * jnp.exp(gl - gcum[n * c:(n + 1) * c, :])
        kd_scr[tc + n * 8:tc + n * 8 + 8, :] = jnp.broadcast_to(jnp.exp(gl), (8, kw))
    lax.fori_loop(0, tc // c, chunk, 0)


def gated_linear_attention(qkvr, small, wg_up, bg, on_gain, tc=512):
    b, t, _ = qkvr.shape
    wg = jnp.zeros((LANES, GLA_KW), F32).at[3 * NSA_HEADS:3 * NSA_HEADS + GLA_GATE_RANK].set(wg_up)
    idx = np.arange(tc)
    tril = ((idx[:, None] >= idx[None, :]) & (idx[:, None] // GLA_CHUNK == idx[None, :] // GLA_CHUNK))
    nchunk = tc // GLA_CHUNK
    return pl.pallas_call(
        functools.partial(_gla_kernel, tc=tc),
        grid=(b, t // tc),
        in_specs=[pl.BlockSpec((1, tc, 2 * GLA_KW), lambda i, j: (i, j, 0)),
                  pl.BlockSpec((1, tc, GLA_W), lambda i, j: (i, j, 1)),
                  pl.BlockSpec((1, tc, GLA_W), lambda i, j: (i, j, 2)),
                  pl.BlockSpec((1, tc, LANES), lambda i, j: (i, j, 0)),
                  _full((LANES, GLA_KW)), _full((1, GLA_KW)), _full((1, GLA_DV)), _full((tc, tc))],
        out_specs=pl.BlockSpec((1, tc, GLA_W), lambda i, j: (i, j, 0)),
        out_shape=jax.ShapeDtypeStruct((b, t, GLA_W), BF16),
        scratch_shapes=[pltpu.VMEM((GLA_HEADS, GLA_DV, LANES), F32), pltpu.VMEM((tc, GLA_KW), F32),
                        pltpu.VMEM((tc, GLA_KW), F32), pltpu.VMEM((tc + 8 * nchunk, GLA_KW), F32)],
        compiler_params=_cparams(("arbitrary", "arbitrary")),
        name="gla",
    )(qkvr, qkvr, qkvr, small, wg, bg.reshape(1, GLA_KW), on_gain.reshape(1, GLA_DV),
      jnp.asarray(tril.astype(np.float32)))


def odd_mixer(h, w_in, nsa_qn, nsa_kn, nsa_pos, nsa_cmp_w1, nsa_cmp_w2, gla_wg_up, gla_bg, gla_on, rel_bias):
    cuts = np.cumsum((0,) + OD_SIZES)
    col = lambda i: w_in[:, cuts[i]:cuts[i + 1]]
    wb = lambda a: a.astype(BF16)
    dup = lambda a: jnp.concatenate([a[:, :NSA_DH], a[:, :NSA_DH], a[:, NSA_DH:], a[:, NSA_DH:]], axis=1)
    ep = _head_norm_epilogue(NSA_DH)
    bd = _block_diag_ones(NSA_W, NSA_DH)
    qg = (jnp.tile(nsa_qn, NSA_HEADS) * (NSA_DH ** -0.5 * LOG2E)).reshape(1, NSA_W)
    kg = jnp.tile(nsa_kn, NSA_HEADS).reshape(1, NSA_W)
    w_small = jnp.zeros((w_in.shape[0], LANES), F32)
    w_small = w_small.at[:, 0:24].set(col(7)).at[:, 24:40].set(col(11))
    nq, kcvc, ksw, vsw, small, qkvr = proj_multi(
        h, [(col(0), BF16, ep, (bd, qg)),
            (jnp.concatenate([col(1), col(2)], axis=1), F32, None, ()),
            (jnp.concatenate([dup(col(3)), dup(col(5))], axis=1), BF16, ep, (bd, kg)),
            (jnp.concatenate([dup(col(4)), dup(col(6))], axis=1), BF16, None, ()),
            (w_small, F32, None, ()),
            (jnp.concatenate([col(8), col(9), col(10), col(12)], axis=1), F32, None, ())], name="proj_odd")
    cmp_kv = nsa_compress(kcvc, nsa_pos, nsa_cmp_w1, nsa_cmp_w2, nsa_kn)
    o_cmp, sel = nsa_select(nq, cmp_kv, rel_bias)
    o_nsa = nsa_main(nq, ksw, vsw, sel, o_cmp, small, rel_bias)
    o_gla = gated_linear_attention(qkvr, small, gla_wg_up, gla_bg, gla_on)
    return o_nsa, o_gla


MOE_TM = 256
MOE_ROWS = 512


def _first_index(mask_val, idx, big, axis):
    return jnp.min(jnp.where(mask_val, idx, big), axis=axis, keepdims=True)


def _route_kernel(h_ref, rt_ref, b_ref, up_ref, eid_ref, rank_ref, w_ref, cnt_ref, run):
    tm = h_ref.shape[0]
    ne = N_EXPERTS
    gsz = ne // N_GROUPS

    @pl.when(pl.program_id(0) == 0)
    def _():
        run[...] = jnp.zeros_like(run)

    scores = _sigmoid(_dot_nt(rt_ref[...], h_ref[...], HI))
    biased = scores + b_ref[...]
    b3 = biased.reshape(N_GROUPS, gsz, tm)
    i3 = lax.broadcasted_iota(jnp.int32, (1, gsz, 1), 1).astype(F32)
    m1 = jnp.max(b3, axis=1, keepdims=True)
    f1 = _first_index(b3 == m1, i3, float(gsz), 1)
    m2 = jnp.max(jnp.where(i3 == f1, -jnp.inf, b3), axis=1, keepdims=True)
    gs = (m1 + m2).reshape(N_GROUPS, tm)
    gidx = lax.broadcasted_iota(jnp.int32, (N_GROUPS, 1), 0).astype(F32)
    gmask = jnp.zeros((N_GROUPS, tm), F32)
    for _ in range(TOPK_GROUPS):
        m = jnp.max(gs, axis=0, keepdims=True)
        pick = gidx == _first_index(gs == m, gidx, float(N_GROUPS), 0)
        gmask = jnp.where(pick, 1.0, gmask)
        gs = jnp.where(pick, -jnp.inf, gs)
    emask = jnp.broadcast_to(gmask.reshape(N_GROUPS, 1, tm), (N_GROUPS, gsz, tm)).reshape(ne, tm)
    work = jnp.where(emask > 0.5, biased, -jnp.inf)
    eidx = lax.broadcasted_iota(jnp.int32, (ne, 1), 0).astype(F32)
    picks, eids, ws = [], [], []
    for _ in range(TOP_K):
        m = jnp.max(work, axis=0, keepdims=True)
        first = _first_index(work == m, eidx, float(ne), 0)
        pick = eidx == first
        picks.append(pick)
        eids.append(first)
        ws.append(jnp.sum(jnp.where(pick, scores, 0.0), axis=0, keepdims=True))
        work = jnp.where(pick, -jnp.inf, work)
    wsum = ws[0]
    for k in range(1, TOP_K):
        wsum = wsum + ws[k]
    chosen = jnp.zeros((ne, tm), F32)
    for pick in picks:
        chosen = jnp.where(pick, 1.0, chosen)
    pos = run[...] + _dot(chosen.astype(BF16), up_ref[...])
    run[...] = run[...] + jnp.sum(chosen, axis=1, keepdims=True)
    cnt_ref[...] = run[...]
    row = lax.broadcasted_iota(jnp.int32, (8, 1), 0)
    eid_o = jnp.zeros((8, tm), F32)
    rank_o = jnp.zeros((8, tm), F32)
    w_o = jnp.zeros((LANES, tm), F32)
    rowl = lax.broadcasted_iota(jnp.int32, (LANES, 1), 0)
    for k in range(TOP_K):
        rk = jnp.sum(jnp.where(picks[k], pos, 0.0), axis=0, keepdims=True)
        eid_o = jnp.where(row == k, eids[k], eid_o)
        rank_o = jnp.where(row == k, rk, rank_o)
        w_o = jnp.where(rowl == k, ws[k] / wsum * ROUTE_SCALE, w_o)
    eid_ref[0] = eid_o.astype(jnp.int32)
    rank_ref[0] = rank_o.astype(jnp.int32)
    w_ref[...] = w_o.T


def moe_route(h2, router, e_bias, tm=MOE_TM):
    nt, d = h2.shape
    ne = N_EXPERTS
    up = jnp.asarray(np.triu(np.ones((tm, tm), np.float32), 1), dtype=BF16)
    nb = nt // tm
    return pl.pallas_call(
        _route_kernel,
        grid=(nb,),
        in_specs=[pl.BlockSpec((tm, d), lambda i: (i, 0)), _full((ne, d)), _full((ne, 1)), _full((tm, tm))],
        out_specs=[pl.BlockSpec((1, 8, tm), lambda i: (i, 0, 0)),
                   pl.BlockSpec((1, 8, tm), lambda i: (i, 0, 0)),
                   pl.BlockSpec((tm, LANES), lambda i: (i, 0)),
                   _full((ne, 1))],
        out_shape=[jax.ShapeDtypeStruct((nb, 8, tm), jnp.int32), jax.ShapeDtypeStruct((nb, 8, tm), jnp.int32),
                   jax.ShapeDtypeStruct((nt, LANES), F32), jax.ShapeDtypeStruct((ne, 1), F32)],
        scratch_shapes=[pltpu.VMEM((ne, 1), F32)],
        compiler_params=_cparams(("arbitrary",)),
        name="moe_route",
    )(h2, router.T, e_bias.reshape(ne, 1), up)


def _dispatch_kernel(dest_ref, h_ref, xs_ref, sem):
    tm = h_ref.shape[0]

    def copy(t, row):
        return pltpu.make_async_copy(h_ref.at[pl.ds(t, 1), :], xs_ref.at[pl.ds(row, 1), :], sem)

    def issue(t, _):
        for k in range(TOP_K):
            copy(t, dest_ref[0, k, t]).start()
        return 0

    def drain(t, _):
        for k in range(TOP_K):
            copy(0, 0).wait()
        return 0

    lax.fori_loop(0, tm, issue, 0, unroll=4)
    lax.fori_loop(0, tm, drain, 0, unroll=4)


def moe_dispatch(h2, dest, tm=MOE_TM):
    nt, d = h2.shape
    return pl.pallas_call(
        _dispatch_kernel,
        grid=(nt // tm,),
        in_specs=[pl.BlockSpec((1, 8, tm), lambda i: (i, 0, 0), memory_space=pltpu.SMEM),
                  pl.BlockSpec((tm, d), lambda i: (i, 0))],
        out_specs=pl.BlockSpec(memory_space=pl.ANY),
        scratch_shapes=[pltpu.SemaphoreType.DMA(())],
        out_shape=jax.ShapeDtypeStruct((nt * TOP_K, d), F32),
        compiler_params=_cparams(("arbitrary",)),
        name="moe_dispatch",
    )(dest, h2)


def _ffn_kernel(blk_ref, exp_ref, lo_ref, hi_ref, first_ref, valid_ref, x_ref, wg_ref, wu_ref, wd_ref, o_ref,
                wg_b, wu_b, wd_b):
    i = pl.program_id(0)
    rows = x_ref.shape[0]

    @pl.when((i == 0) | (exp_ref[i] != exp_ref[jnp.maximum(i - 1, 0)]))
    def _():
        wg_b[...] = wg_ref[0].astype(BF16)
        wu_b[...] = wu_ref[0].astype(BF16)
        wd_b[...] = wd_ref[0].astype(BF16)

    @pl.when(valid_ref[i] == 1)
    def _():
        x = x_ref[...].astype(BF16)
        a = _dot(x, wg_b[...])
        u = _dot(x, wu_b[...])
        y = _dot((_silu(a) * u).astype(BF16), wd_b[...])
        r = blk_ref[i] * rows + lax.broadcasted_iota(jnp.int32, (rows, 1), 0)
        y = jnp.where((r >= lo_ref[i]) & (r < hi_ref[i]), y, 0.0)

        @pl.when(first_ref[i] == 1)
        def _():
            o_ref[...] = y

        @pl.when(first_ref[i] == 0)
        def _():
            o_ref[...] = o_ref[...] + y


def _ffn_items(counts, n_rows, rows):
    ne = N_EXPERTS
    nblk = n_rows // rows
    n_items = nblk + ne - 1
    ends = jnp.cumsum(counts)
    starts = ends - counts
    first_blk = starts // rows
    last_blk = jnp.maximum(ends - 1, 0) // rows
    per_e = jnp.where(counts > 0, last_blk - first_blk + 1, 0)
    item_end = jnp.cumsum(per_e)
    item_start = item_end - per_e
    total = item_end[-1]
    i = jnp.arange(n_items, dtype=jnp.int32)
    ic = jnp.minimum(i, total - 1)
    e = jnp.sum((item_end[None, :] <= ic[:, None]).astype(jnp.int32), axis=1)
    blk = first_blk[e] + ic - item_start[e]
    lo = jnp.maximum(starts[e], blk * rows)
    hi = jnp.minimum(ends[e], (blk + 1) * rows)
    valid = (i < total).astype(jnp.int32)
    first = (lo == blk * rows).astype(jnp.int32)
    return starts, (blk.astype(jnp.int32), e, lo.astype(jnp.int32), hi.astype(jnp.int32), first, valid)


def moe_ffn_sorted(xs, items, wg, wu, wd, rows=MOE_ROWS):
    n_rows, d = xs.shape
    n_items = items[0].shape[0]
    de = wg.shape[-1]
    return pl.pallas_call(
        _ffn_kernel,
        grid_spec=pltpu.PrefetchScalarGridSpec(
            num_scalar_prefetch=6,
            grid=(n_items,),
            in_specs=[pl.BlockSpec((rows, d), lambda i, blk, e, *_: (blk[i], 0)),
                      pl.BlockSpec((1, d, de), lambda i, blk, e, *_: (e[i], 0, 0)),
                      pl.BlockSpec((1, d, de), lambda i, blk, e, *_: (e[i], 0, 0)),
                      pl.BlockSpec((1, de, d), lambda i, blk, e, *_: (e[i], 0, 0))],
            out_specs=pl.BlockSpec((rows, d), lambda i, blk, e, *_: (blk[i], 0)),
            scratch_shapes=[pltpu.VMEM((d, de), BF16), pltpu.VMEM((d, de), BF16), pltpu.VMEM((de, d), BF16)]),
        out_shape=jax.ShapeDtypeStruct((n_rows, d), F32),
        compiler_params=_cparams(("arbitrary",)),
        name="moe_ffn",
    )(*items, xs, wg, wu, wd)


def _combine_kernel(dest_ref, ys_ref, w_ref, h_ref, x_ref, g_ref, sg_ref, su_ref, sd_ref, o_ref, buf, sem):
    tm = h_ref.shape[0]

    def copy(t, k, row):
        return pltpu.make_async_copy(ys_ref.at[pl.ds(row, 1), :], buf.at[k, pl.ds(t, 1), :], sem)

    def issue(t, _):
        for k in range(TOP_K):
            copy(t, k, dest_ref[0, k, t]).start()
        return 0

    def drain(t, _):
        for k in range(TOP_K):
            copy(0, 0, 0).wait()
        return 0

    lax.fori_loop(0, tm, issue, 0, unroll=4)
    hb = h_ref[...].astype(BF16)
    y = _dot((_silu(_dot(hb, sg_ref[...])) * _dot(hb, su_ref[...])).astype(BF16), sd_ref[...])
    lax.fori_loop(0, tm, drain, 0, unroll=4)
    w = w_ref[...]
    for k in range(TOP_K):
        y = y + w[:, k:k + 1] * buf[k]
    o_ref[...] = x_ref[...] + g_ref[0] * y


def moe_combine(ys, dest, w, h2, x2, gate, sg, su, sd, seq, tm=MOE_TM):
    nt, d = h2.shape
    ds_ = sg.shape[-1]
    per_b = seq // tm
    tile = lambda: pl.BlockSpec((tm, d), lambda i: (i, 0))
    return pl.pallas_call(
        _combine_kernel,
        grid=(nt // tm,),
        in_specs=[pl.BlockSpec((1, 8, tm), lambda i: (i, 0, 0), memory_space=pltpu.SMEM),
                  pl.BlockSpec(memory_space=pl.ANY),
                  pl.BlockSpec((tm, LANES), lambda i: (i, 0)), tile(), tile(),
                  pl.BlockSpec((1, 1, d), lambda i: (i // per_b, 0, 0)),
                  _full((d, ds_)), _full((d, ds_)), _full((ds_, d))],
        out_specs=tile(),
        scratch_shapes=[pltpu.VMEM((TOP_K, tm, d), F32), pltpu.SemaphoreType.DMA(())],
        out_shape=jax.ShapeDtypeStruct((nt, d), F32),
        compiler_params=_cparams(("arbitrary",)),
        name="moe_combine",
    )(dest, ys, w, h2, x2, gate, sg.astype(BF16), su.astype(BF16), sd.astype(BF16))


def moe_layer(x, g_norm, sc, sh, gate, router, e_bias, wg, wu, wd, sg, su, sd):
    b, t, d = x.shape
    nt = b * t
    h = ln_mod(x, g_norm, sc, sh, F32)
    h2 = h.reshape(nt, d)
    eid, rank, w, counts = moe_route(h2, router, e_bias)
    starts, items = _ffn_items(counts.reshape(-1).astype(jnp.int32), nt * TOP_K, MOE_ROWS)
    hit = eid[..., None] == jnp.arange(N_EXPERTS, dtype=jnp.int32)
    dest = jnp.sum(jnp.where(hit, starts.astype(jnp.int32), 0), axis=-1) + rank
    xs = moe_dispatch(h2, dest)
    ys = moe_ffn_sorted(xs, items, wg, wu, wd)
    out = moe_combine(ys, dest, w, h2, x.reshape(nt, d), gate.reshape(b, 1, d), sg, su, sd, t)
    return out.reshape(b, t, d)


def kernel(x, c, ada_w, ada_b, norm_mix, norm_ffn, rel_bias, ev_w_in, ev_w_out, fox_fb, fox_qn, fox_kn, gdn_conv, gdn_a_log, gdn_dt_bias, gdn_on, od_w_in, od_w_out, nsa_qn, nsa_kn, nsa_pos, nsa_cmp_w1, nsa_cmp_w2, gla_wg_up, gla_bg, gla_on, moe_router, moe_bias, moe_wg, moe_wu, moe_wd, sh_wg, sh_wu, sh_wd):
    d = x.shape[-1]
    depth = ada_w.shape[0]
    mod = adaln(c, ada_w, ada_b)
    for layer in range(depth):
        sh1, sc1, g1, sh2, sc2, g2 = [mod[layer, :, i * d:(i + 1) * d] for i in range(6)]
        h = ln_mod(x, norm_mix[layer], sc1, sh1, BF16)
        j = layer // 2
        if layer % 2 == 0:
            y1, y2 = even_mixer(h, ev_w_in[j], fox_fb[j], fox_qn[j], fox_kn[j], gdn_conv[j], gdn_a_log[j],
                                gdn_dt_bias[j], gdn_on[j])
            w_out = ev_w_out[j]
        else:
            y1, y2 = odd_mixer(h, od_w_in[j], nsa_qn[j], nsa_kn[j], nsa_pos[j], nsa_cmp_w1[j], nsa_cmp_w2[j],
                               gla_wg_up[j], gla_bg[j], gla_on[j], rel_bias)
            w_out = od_w_out[j]
        x = out_proj(y1, y2, w_out, x, g1)
        x = moe_layer(x, norm_ffn[layer], sc2, sh2, g2, moe_router[layer], moe_bias[layer], moe_wg[layer],
                      moe_wu[layer], moe_wd[layer], sh_wg[layer], sh_wu[layer], sh_wd[layer])
    return x
```

```python
import functools
import math

import numpy as np
import jax
import jax.numpy as jnp
from jax import lax
from jax.experimental import pallas as pl
from jax.experimental.pallas import tpu as pltpu

F32 = jnp.float32
BF16 = jnp.bfloat16
HI = lax.Precision.HIGHEST

EPS = 1e-6
LOG2E = math.log2(math.e)
NEG = -1e30
FORCE_SCORE = 1e9

FOX_HEADS, FOX_DH = 8, 64
GDN_HEADS, GDN_DH, GDN_CONV, GDN_CHUNK = 4, 128, 4, 64
NSA_HEADS, NSA_KV_HEADS, NSA_DH = 8, 2, 64
NSA_GROUP = NSA_HEADS // NSA_KV_HEADS
CMP_LEN, CMP_STRIDE, CMP_HIDDEN = 32, 16, 256
SLC_LEN, SLC_TOPK, WINDOW = 64, 16, 512
GLA_HEADS, GLA_DK, GLA_DV, GLA_GATE_RANK, GLA_TAU, GLA_CHUNK = 4, 64, 128, 16, 16.0, 64
REL_BUCKETS, REL_MAX_DIST = 32, 128
N_EXPERTS, TOP_K, D_EXPERT, D_SHARED = 64, 6, 256, 256
N_GROUPS, TOPK_GROUPS, ROUTE_SCALE = 8, 4, 2.5

FOX_W = FOX_HEADS * FOX_DH
GDN_W = GDN_HEADS * GDN_DH
NSA_W = NSA_HEADS * NSA_DH
NSA_KV_W = NSA_KV_HEADS * NSA_DH
GLA_KW = GLA_HEADS * GLA_DK
GLA_W = GLA_HEADS * GLA_DV
EV_SIZES = (FOX_W, FOX_W, FOX_W, FOX_HEADS, 3 * GDN_W, GDN_HEADS, GDN_HEADS, GDN_W)
OD_SIZES = (NSA_W,) + (NSA_KV_W,) * 6 + (3 * NSA_HEADS, GLA_KW, GLA_KW, GLA_W, GLA_GATE_RANK, GLA_W)

LANES = 128
ROW_GROUP = 64
VMEM_LIMIT = 56 * 1024 * 1024


def _cparams(sem, flags=None):
    return pltpu.CompilerParams(dimension_semantics=sem, vmem_limit_bytes=VMEM_LIMIT, flags=flags)


def _full(shape):
    n = len(shape)
    return pl.BlockSpec(shape, lambda *_: (0,) * n)


def _dot(a, b):
    return jnp.dot(a, b, preferred_element_type=F32)


def _dot_hi(a, b):
    return jnp.dot(a, b, precision=HI, preferred_element_type=F32)


def _dot_nt(a, b, precision=None):
    return lax.dot_general(a, b, (((1,), (1,)), ((), ())), precision=precision, preferred_element_type=F32)


def _dot_tn(a, b, precision=None):
    return lax.dot_general(a, b, (((0,), (0,)), ((), ())), precision=precision, preferred_element_type=F32)


def _sigmoid(x):
    return 1.0 / (1.0 + jnp.exp(-x))


def _silu(x):
    return x * _sigmoid(x)


def _softplus(x):
    return jnp.maximum(x, 0.0) + jnp.log(1.0 + jnp.exp(-jnp.abs(x)))


def _log_sigmoid(x):
    return -_softplus(-x)


def _adaln_kernel(c_ref, w_ref, b_ref, o_ref):
    c = c_ref[...]
    o_ref[0] = _dot_hi(_silu(c), w_ref[0]) + b_ref[0]


def adaln(c, ada_w, ada_b):
    depth, d, n = ada_w.shape
    b = c.shape[0]
    cp = jnp.zeros((8, d), F32).at[:b].set(c)
    tn = 1536
    out = pl.pallas_call(
        _adaln_kernel,
        grid=(depth, n // tn),
        in_specs=[_full((8, d)),
                  pl.BlockSpec((1, d, tn), lambda l, j: (l, 0, j)),
                  pl.BlockSpec((1, 1, tn), lambda l, j: (l, 0, j))],
        out_specs=pl.BlockSpec((1, 8, tn), lambda l, j: (l, 0, j)),
        out_shape=jax.ShapeDtypeStruct((depth, 8, n), F32),
        compiler_params=_cparams(("arbitrary", "arbitrary")),
        name="adaln",
    )(cp, ada_w, ada_b.reshape(depth, 1, n))
    return out[:, :b]


def _ln_kernel(x_ref, g_ref, sc_ref, sh_ref, o_ref):
    x = x_ref[0]
    y = x * lax.rsqrt(jnp.mean(x * x, axis=-1, keepdims=True) + EPS) * g_ref[...]
    o_ref[0] = (y * (1.0 + sc_ref[0]) + sh_ref[0]).astype(o_ref.dtype)


def ln_mod(x, g, sc, sh, out_dtype, tm=512):
    b, t, d = x.shape
    return pl.pallas_call(
        _ln_kernel,
        grid=(b, t // tm),
        in_specs=[pl.BlockSpec((1, tm, d), lambda i, j: (i, j, 0)),
                  _full((1, d)),
                  pl.BlockSpec((1, 1, d), lambda i, j: (i, 0, 0)),
                  pl.BlockSpec((1, 1, d), lambda i, j: (i, 0, 0))],
        out_specs=pl.BlockSpec((1, tm, d), lambda i, j: (i, j, 0)),
        out_shape=jax.ShapeDtypeStruct((b, t, d), out_dtype),
        compiler_params=_cparams(("arbitrary", "arbitrary")),
        name="ln_mod",
    )(x, g.reshape(1, d), sc.reshape(b, 1, d), sh.reshape(b, 1, d))


def proj(h, w, out_dtype, epilogue=None, extras=(), tm=512, name="proj"):
    b, t, d = h.shape
    n = w.shape[1]

    def kern(h_ref, w_ref, *rest):
        o_ref = rest[-1]
        y = _dot(h_ref[0], w_ref[...])
        if epilogue is not None:
            y = epilogue(y, *[e[...] for e in rest[:-1]])
        o_ref[0] = y.astype(out_dtype)

    return pl.pallas_call(
        kern,
        grid=(b, t // tm),
        in_specs=[pl.BlockSpec((1, tm, d), lambda i, j: (i, j, 0)), _full((d, n))]
                 + [_full(e.shape) for e in extras],
        out_specs=pl.BlockSpec((1, tm, n), lambda i, j: (i, j, 0)),
        out_shape=jax.ShapeDtypeStruct((b, t, n), out_dtype),
        compiler_params=_cparams(("arbitrary", "arbitrary")),
        name=name,
    )(h, w, *extras)


def proj_multi(h, groups, tm=512, name="proj"):
    b, t, d = h.shape
    widths = [g[0].shape[1] for g in groups]
    starts = np.cumsum([0] + widths)
    w_cat = jnp.concatenate([g[0] for g in groups], axis=1).astype(BF16)
    extras = [e for g in groups for e in g[3]]
    n_ex = [len(g[3]) for g in groups]
    n_out = len(groups)

    def kern(h_ref, w_ref, *rest):
        ex_refs = rest[:len(extras)]
        o_refs = rest[len(extras):]
        y = _dot(h_ref[0], w_ref[...])
        pos = 0
        for gi, (_, out_dtype, epilogue, _) in enumerate(groups):
            yg = y[:, starts[gi]:starts[gi + 1]]
            if epilogue is not None:
                yg = epilogue(yg, *[e[...] for e in ex_refs[pos:pos + n_ex[gi]]])
            pos += n_ex[gi]
            o_refs[gi][0] = yg.astype(out_dtype)

    return pl.pallas_call(
        kern,
        grid=(b, t // tm),
        in_specs=[pl.BlockSpec((1, tm, d), lambda i, j: (i, j, 0)), _full((d, int(starts[-1])))]
                 + [_full(e.shape) for e in extras],
        out_specs=[pl.BlockSpec((1, tm, n), lambda i, j: (i, j, 0)) for n in widths],
        out_shape=[jax.ShapeDtypeStruct((b, t, n), g[1]) for n, g in zip(widths, groups)],
        compiler_params=_cparams(("arbitrary", "arbitrary")),
        name=name,
    )(h, w_cat, *extras)


def _head_norm_epilogue(dh):
    inv = 1.0 / dh

    def ep(y, bd, gain):
        ssq = _dot((y * y).astype(BF16), bd)
        return y * lax.rsqrt(ssq * inv + EPS) * gain

    return ep


def _block_diag_ones(n, dh):
    i = np.arange(n) // dh
    return jnp.asarray((i[:, None] == i[None, :]).astype(np.float32), dtype=BF16)


def _outproj_kernel(y1_ref, y2_ref, wa_ref, wb_ref, x_ref, g_ref, o_ref):
    y = _dot(y1_ref[0], wa_ref[...]) + _dot(y2_ref[0], wb_ref[...])
    o_ref[0] = x_ref[0] + g_ref[0] * y


def out_proj(y1, y2, w_out, x, gate, tm=512):
    b, t, d = x.shape
    n1, n2 = y1.shape[-1], y2.shape[-1]
    wa = w_out[:n1].astype(BF16)
    wb = w_out[n1:].astype(BF16)
    return pl.pallas_call(
        _outproj_kernel,
        grid=(b, t // tm),
        in_specs=[pl.BlockSpec((1, tm, n1), lambda i, j: (i, j, 0)),
                  pl.BlockSpec((1, tm, n2), lambda i, j: (i, j, 0)),
                  _full((n1, d)), _full((n2, d)),
                  pl.BlockSpec((1, tm, d), lambda i, j: (i, j, 0)),
                  pl.BlockSpec((1, 1, d), lambda i, j: (i, 0, 0))],
        out_specs=pl.BlockSpec((1, tm, d), lambda i, j: (i, j, 0)),
        out_shape=jax.ShapeDtypeStruct((b, t, d), F32),
        compiler_params=_cparams(("arbitrary", "arbitrary")),
        name="out_proj",
    )(y1, y2, wa, wb, x, gate.reshape(b, 1, d))


def _decay_kernel(s_ref, fb_ref, tril_ref, place_ref, o_ref, carry):
    @pl.when(pl.program_id(1) == 0)
    def _():
        carry[...] = jnp.zeros_like(carry)

    tm = s_ref.shape[1]
    lf = _log_sigmoid(s_ref[0] + fb_ref[...])
    cum = _dot_hi(tril_ref[...], lf) + carry[...]
    carry[...] = cum[tm - 1:tm, :]
    x = cum * LOG2E
    hi = x.astype(BF16)
    r1 = x - hi.astype(F32)
    mid = r1.astype(BF16)
    low = (r1 - mid.astype(F32)).astype(BF16)
    o_ref[0] = _dot(jnp.concatenate([hi, mid, low], axis=1), place_ref[...]).astype(o_ref.dtype)


def fox_decay(small, fox_fb, tm=512):
    b, t, _ = small.shape
    fb = jnp.zeros((1, LANES), F32).at[0, :FOX_HEADS].set(fox_fb)
    tril = jnp.asarray(np.tril(np.ones((tm, tm), np.float32)))
    place = np.zeros((3 * LANES, FOX_W), np.float32)
    for h in range(FOX_HEADS):
        for j in range(3):
            place[j * LANES + h, (h // 2) * LANES + (FOX_DH if h % 2 == 0 else 0) + j] = 1.0
    return pl.pallas_call(
        _decay_kernel,
        grid=(b, t // tm),
        in_specs=[pl.BlockSpec((1, tm, LANES), lambda i, j: (i, j, 0)), _full((1, LANES)), _full((tm, tm)),
                  _full((3 * LANES, FOX_W))],
        out_specs=pl.BlockSpec((1, tm, FOX_W), lambda i, j: (i, j, 0)),
        out_shape=jax.ShapeDtypeStruct((b, t, FOX_W), BF16),
        scratch_shapes=[pltpu.VMEM((1, LANES), F32)],
        compiler_params=_cparams(("arbitrary", "arbitrary")),
        name="fox_decay",
    )(small, fb, tril, jnp.asarray(place, dtype=BF16))


def _fox_kernel(q_ref, k_ref, v_ref, f_ref, o_ref, *, tq, tk):
    i = pl.program_id(2)
    lane = lax.broadcasted_iota(jnp.int32, (1, LANES), 1)
    lo = lane < FOX_DH
    coef = jnp.where((lane & (FOX_DH - 1)) < 3, -1.0, 0.0).astype(BF16)
    q = q_ref[0]
    qs = (jnp.where(lo, q, coef), jnp.where(lo, coef, q))
    n_full = (i * tq) // tk
    causal = (lax.broadcasted_iota(jnp.int32, (tq, tk), 1)
              <= lax.broadcasted_iota(jnp.int32, (tq, tk), 0) + (i * tq - n_full * tk))

    def scores(j):
        start = pl.multiple_of(j * tk, tk)
        kt = k_ref[0, pl.ds(start, tk), :]
        ft = f_ref[0, pl.ds(start, tk), :]
        return _dot_nt(qs[0], jnp.where(lo, kt, ft)), _dot_nt(qs[1], jnp.where(lo, ft, kt))

    one = jnp.ones((1, LANES), BF16)

    def update(j, s_pair, carry, diag):
        vt = v_ref[0, pl.ds(pl.multiple_of(j * tk, tk), tk), :]
        vs = (jnp.where(lo, vt, one), jnp.where(lo, one, vt))
        new = []
        for hh in range(2):
            m, acc = carry[hh]
            s = s_pair[hh]
            if diag:
                s = jnp.where(causal, s, NEG)
            m_new = jnp.maximum(m, jnp.max(s, axis=1, keepdims=True))
            p = jnp.exp2(s - m_new)
            acc = jnp.exp2(m - m_new) * acc + _dot(p.astype(BF16), vs[hh])
            new.append((m_new, acc))
        return tuple(new)

    init = tuple((jnp.full((tq, 1), NEG, F32), jnp.zeros((tq, LANES), F32)) for _ in range(2))
    carry = lax.fori_loop(0, n_full, lambda j, c: update(j, scores(j), c, False), init)
    carry = update(n_full, scores(n_full), carry, True)
    acc = jnp.where(lo, carry[0][1], carry[1][1])
    den = jnp.where(lo, carry[1][1], carry[0][1])
    o_ref[0] = (acc / pltpu.roll(den, FOX_DH, 1)).astype(o_ref.dtype)


def fox_attention(q, k, v, feat, tq=512, tk=1024):
    b, t, w = q.shape
    npair = w // LANES
    nt = t // tq
    tk = min(tk, t)
    whole = lambda: pl.BlockSpec((1, t, LANES), lambda bi, p, i: (bi, 0, p))
    return pl.pallas_call(
        functools.partial(_fox_kernel, tq=tq, tk=tk),
        grid=(b, npair, nt),
        in_specs=[pl.BlockSpec((1, tq, LANES), lambda bi, p, i: (bi, i, p)), whole(), whole(), whole()],
        out_specs=pl.BlockSpec((1, tq, LANES), lambda bi, p, i: (bi, i, p)),
        out_shape=jax.ShapeDtypeStruct((b, t, w), BF16),
        compiler_params=_cparams(("arbitrary", "arbitrary", "arbitrary")),
        name="fox_attn",
    )(q, k, v, feat)


def _mm(a, b):
    return _dot(a.astype(BF16), b.astype(BF16))


def _mm3(a, b):
    ah = a.astype(BF16)
    bh = b.astype(BF16)
    al = (a - ah.astype(F32)).astype(BF16)
    bl = (b - bh.astype(F32)).astype(BF16)
    return _dot(jnp.concatenate([ah, ah, al], axis=1), jnp.concatenate([bh, bl, bh], axis=0))


def _tril_solve(a, rhs, ri, ci):
    n = a[0].shape[0]
    both = lambda f, x, y: [f(p, q) for p, q in zip(x, y)]
    eye = (ri == ci).astype(F32)
    same = lambda b: (lax.shift_right_logical(ri, int(math.log2(b)))
                      == lax.shift_right_logical(ci, int(math.log2(b))))
    base = 16
    d = [jnp.where(same(base), p, 0.0) for p in a]
    d2 = both(_mm3, d, d)
    d4 = both(_mm3, d2, d2)
    r1 = [eye - p + p2 - t for p, p2, t in zip(d, d2, both(_mm3, d, d2))]
    d8 = both(_mm3, d4, d4)
    r2 = [eye + p4 + p8 + t for p4, p8, t in zip(d4, d8, both(_mm3, d4, d8))]
    t = both(_mm3, r1, r2)
    b = base
    while b < n:
        join = same(2 * b) & jnp.logical_not(same(b))
        low = [jnp.where(join, p, 0.0) for p in a]
        t = [p - q for p, q in zip(t, both(_mm3, both(_mm3, t, low), t))]
        b *= 2
    return both(_mm3, t, rhs)


GDN_BLOCK = 128


def _gdn_kernel(x_ref, sm_ref, z_ref, cw_ref, ega_ref, egb_ref, alog_ref, dtb_ref, on_ref, tril_ref,
                o_ref, s_scr, prev_scr, *, tc):
    c = GDN_BLOCK
    w = GDN_W

    @pl.when(pl.program_id(1) == 0)
    def _():
        s_scr[...] = jnp.zeros_like(s_scr)
        prev_scr[...] = jnp.zeros_like(prev_scr)

    x = x_ref[0]
    prev = prev_scr[...]
    row8 = lax.broadcasted_iota(jnp.int32, (8, 1), 0)
    acc = x * cw_ref[GDN_CONV - 1:GDN_CONV, :]
    for s in range(1, GDN_CONV):
        rolled = pltpu.roll(x, s, 0)
        head = jnp.where(row8 < s, pltpu.roll(prev, s, 0), rolled[0:8])
        shifted = jnp.concatenate([head, rolled[8:]], axis=0)
        acc = acc + shifted * cw_ref[GDN_CONV - 1 - s:GDN_CONV - s, :]
    prev_scr[...] = x[tc - 8:tc]
    xc = _silu(acc)

    sm = sm_ref[0]
    g_raw = _dot_hi(sm, ega_ref[...])
    b_raw = _dot_hi(sm, egb_ref[...])
    g = -jnp.exp(alog_ref[...]) * _softplus(g_raw + dtb_ref[...])
    beta_all = _sigmoid(b_raw)
    gc_all = _dot_hi(tril_ref[...], g)

    ri = lax.broadcasted_iota(jnp.int32, (c, c), 0)
    ci = lax.broadcasted_iota(jnp.int32, (c, c), 1)
    causal = ci <= ri
    strict = ci < ri

    nblk = tc // c
    a_l, attn_l, rhs_l, qd_l, kd_l, egl_l = [], [], [], [], [], []
    for h in range(GDN_HEADS):
        ln = slice(h * GDN_DH, (h + 1) * GDN_DH)
        qh = xc[:, h * GDN_DH:(h + 1) * GDN_DH]
        kh = xc[:, w + h * GDN_DH:w + (h + 1) * GDN_DH]
        qh = qh * lax.rsqrt(jnp.sum(qh * qh, axis=-1, keepdims=True) + EPS) * (GDN_DH ** -0.5)
        kh = kh * lax.rsqrt(jnp.sum(kh * kh, axis=-1, keepdims=True) + EPS)
        vh = xc[:, 2 * w + h * GDN_DH:2 * w + (h + 1) * GDN_DH]
        gch = gc_all[:, ln]
        gct = gch.T
        egc = jnp.exp(gch)
        bh = beta_all[:, ln]
        for n in range(nblk):
            sl = slice(n * c, (n + 1) * c)
            q, k, v, gc, be = qh[sl], kh[sl], vh[sl], gch[sl], bh[sl]
            decay = jnp.exp(jnp.where(causal, gc - gct[:, sl], NEG))
            kb = k * be
            kk = _dot_nt(jnp.concatenate([kb, q], axis=0).astype(BF16), k.astype(BF16))
            a_l.append(jnp.where(strict, kk[:c] * decay, 0.0))
            attn_l.append(jnp.where(causal, kk[c:] * decay, 0.0))
            rhs_l.append(jnp.concatenate([v * be, kb * egc[sl]], axis=1))
            gl = gc[c - 1:c, :]
            qd_l.append(q * egc[sl])
            kd_l.append(k * jnp.exp(gl - gc))
            egl_l.append(jnp.exp(gl))
    uw_l = _tril_solve(a_l, rhs_l, ri, ci)

    states = [s_scr[h] for h in range(GDN_HEADS)]
    for n in range(nblk):
        sl = slice(n * c, (n + 1) * c)
        idx = [h * nblk + n for h in range(GDN_HEADS)]
        ws = [_mm(jnp.concatenate([uw_l[i][:, GDN_DH:], qd_l[i]], axis=0), states[h])
              for h, i in enumerate(idx)]
        v_new = [uw_l[i][:, :GDN_DH] - ws[h][:c] for h, i in enumerate(idx)]
        o = [ws[h][c:] + _mm(attn_l[i], v_new[h]) for h, i in enumerate(idx)]
        states = [states[h] * egl_l[i] + _dot_tn(kd_l[i].astype(BF16), v_new[h].astype(BF16))
                  for h, i in enumerate(idx)]
        for h in range(GDN_HEADS):
            ln = slice(h * GDN_DH, (h + 1) * GDN_DH)
            on = o[h] * lax.rsqrt(jnp.mean(o[h] * o[h], axis=-1, keepdims=True) + EPS) * on_ref[...]
            o_ref[0, sl, ln] = (on * _silu(z_ref[0, sl, ln])).astype(o_ref.dtype)
    for h in range(GDN_HEADS):
        s_scr[h] = states[h]


def gated_delta_net(x, small, z, conv_w, a_log, dt_bias, on_gain, tc=512):
    b, t, _ = x.shape
    w = GDN_W
    ega = np.zeros((LANES, w), np.float32)
    egb = np.zeros((LANES, w), np.float32)
    for h in range(GDN_HEADS):
        ega[FOX_HEADS + h, h * GDN_DH:(h + 1) * GDN_DH] = 1.0
        egb[FOX_HEADS + GDN_HEADS + h, h * GDN_DH:(h + 1) * GDN_DH] = 1.0
    alog = jnp.repeat(a_log, GDN_DH).reshape(1, w)
    dtb = jnp.repeat(dt_bias, GDN_DH).reshape(1, w)
    idx = np.arange(tc)
    tril = ((idx[:, None] >= idx[None, :]) & (idx[:, None] // GDN_BLOCK == idx[None, :] // GDN_BLOCK))
    row = lambda n: pl.BlockSpec((1, tc, n), lambda i, j: (i, j, 0))
    return pl.pallas_call(
        functools.partial(_gdn_kernel, tc=tc),
        grid=(b, t // tc),
        in_specs=[row(3 * w), row(LANES), row(w), _full((GDN_CONV, 3 * w)), _full((LANES, w)), _full((LANES, w)),
                  _full((1, w)), _full((1, w)), _full((1, GDN_DH)), _full((tc, tc))],
        out_specs=row(w),
        out_shape=jax.ShapeDtypeStruct((b, t, w), BF16),
        scratch_shapes=[pltpu.VMEM((GDN_HEADS, GDN_DH, GDN_DH), F32), pltpu.VMEM((8, 3 * w), F32)],
        compiler_params=_cparams(("arbitrary", "arbitrary")),
        name="gdn",
    )(x, small, z, conv_w, jnp.asarray(ega), jnp.asarray(egb), alog, dtb, on_gain.reshape(1, GDN_DH),
      jnp.asarray(tril.astype(np.float32)))


def even_mixer(h, w_in, fox_fb, fox_qn, fox_kn, gdn_conv, gdn_a_log, gdn_dt_bias, gdn_on):
    cuts = np.cumsum((0,) + EV_SIZES)
    col = lambda i: w_in[:, cuts[i]:cuts[i + 1]]
    wb = lambda a: a.astype(BF16)
    bd = _block_diag_ones(FOX_W, FOX_DH)
    ep = _head_norm_epilogue(FOX_DH)
    qg = (jnp.tile(fox_qn, FOX_HEADS) * (FOX_DH ** -0.5 * LOG2E)).reshape(1, FOX_W)
    kg = jnp.tile(fox_kn, FOX_HEADS).reshape(1, FOX_W)
    w_small = jnp.zeros((w_in.shape[0], LANES), F32)
    w_small = w_small.at[:, 0:8].set(col(3)).at[:, 8:12].set(col(5)).at[:, 12:16].set(col(6))
    fq, fk, fv, small, gqkv, gz = proj_multi(
        h, [(col(0), BF16, ep, (bd, qg)), (col(1), BF16, ep, (bd, kg)), (col(2), BF16, None, ()),
            (w_small, F32, None, ()), (col(4), F32, None, ()), (col(7), F32, None, ())], name="proj_even")
    feat = fox_decay(small, fox_fb)
    o_fox = fox_attention(fq, fk, fv, feat)
    o_gdn = gated_delta_net(gqkv, small, gz, gdn_conv, gdn_a_log, gdn_dt_bias, gdn_on)
    return o_fox, o_gdn


def _t5_bucket_np(dist):
    n = np.maximum(dist, 0)
    exact = REL_BUCKETS // 2
    nf = np.maximum(n, 1).astype(np.float32)
    large = exact + (np.log(nf / np.float32(exact)) / np.float32(math.log(REL_MAX_DIST / exact))
                     * np.float32(REL_BUCKETS - exact)).astype(np.int32)
    large = np.minimum(large, REL_BUCKETS - 1)
    return np.where(n < exact, n, large)


def _bias_kernel(tbl_ref, bucket_ref, o_ref):
    h = pl.program_id(0)
    bucket = bucket_ref[...]
    acc = jnp.full(bucket.shape, NEG, F32)
    for b in range(REL_BUCKETS):
        acc = jnp.where(bucket == b, tbl_ref[b, h], acc)
    o_ref[0] = acc


def _bias_table(rel_bias, dist, valid):
    shifted = (rel_bias - rel_bias[REL_BUCKETS - 1:REL_BUCKETS]) * LOG2E
    bucket = np.where(valid, _t5_bucket_np(dist), -1).astype(np.int32)
    rows, cols = int(np.prod(bucket.shape[:-1])), bucket.shape[-1]
    nh = rel_bias.shape[1]
    tb = pl.pallas_call(
        _bias_kernel,
        grid=(nh,),
        in_specs=[pl.BlockSpec(memory_space=pltpu.SMEM), _full((rows, cols))],
        out_specs=pl.BlockSpec((1, rows, cols), lambda h: (h, 0, 0)),
        out_shape=jax.ShapeDtypeStruct((nh, rows, cols), F32),
        compiler_params=_cparams(("arbitrary",)),
        name="t5_bias",
    )(shifted, jnp.asarray(bucket.reshape(rows, cols)))
    return tb.reshape((nh,) + bucket.shape)


def _cmp_kernel(r_ref, pos_ref, w1_ref, w2_ref, kn_ref, o_ref):
    m = r_ref.shape[3]
    half = r_ref.shape[4]
    r = r_ref[0, 0, 0].astype(BF16)
    a = _dot(r, w1_ref[0, :half, :])
    bm = _dot(r, w1_ref[0, half:, :])
    c = _dot(pos_ref[0].astype(BF16), w1_ref[0])
    hid = a + pltpu.roll(bm, m - 1, 0) + c[0:1, :]
    out = _dot(_silu(hid).astype(BF16), w2_ref[0])
    normed = out * lax.rsqrt(jnp.mean(out * out, axis=-1, keepdims=True) + EPS) * kn_ref[...]
    o_ref[0, 0, 0] = jnp.where(pl.program_id(0) == 0, normed, out).astype(o_ref.dtype)


def nsa_compress(kcvc, pos, w1, w2, kn):
    b, t, _ = kcvc.shape
    m = t // CMP_STRIDE
    half = CMP_STRIDE * NSA_DH
    r = kcvc.reshape(b, m, CMP_STRIDE, 2, NSA_KV_HEADS, NSA_DH).transpose(3, 0, 4, 1, 2, 5).reshape(2, b, 2, m, half)
    posf = jnp.zeros((2, 8, 2 * half), F32).at[:, 0].set(pos.reshape(2, 2 * half))
    w2d = jnp.concatenate([w2, w2], axis=-1).astype(BF16)
    knd = jnp.tile(kn, 2).reshape(1, LANES)
    return pl.pallas_call(
        _cmp_kernel,
        grid=(2, b, NSA_KV_HEADS),
        in_specs=[pl.BlockSpec((1, 1, 1, m, half), lambda s, i, k: (s, i, k, 0, 0)),
                  pl.BlockSpec((1, 8, 2 * half), lambda s, i, k: (s, 0, 0)),
                  pl.BlockSpec((1, 2 * half, CMP_HIDDEN), lambda s, i, k: (s, 0, 0)),
                  pl.BlockSpec((1, CMP_HIDDEN, LANES), lambda s, i, k: (s, 0, 0)),
                  _full((1, LANES))],
        out_specs=pl.BlockSpec((1, 1, 1, m, LANES), lambda s, i, k: (s, i, k, 0, 0)),
        out_shape=jax.ShapeDtypeStruct((2, b, NSA_KV_HEADS, m, LANES), BF16),
        compiler_params=_cparams(("arbitrary", "arbitrary", "arbitrary")),
        name="nsa_compress",
    )(r, posf, w1.astype(BF16), w2d, knd)


def _dot_split(a, b):
    hi = a.astype(BF16)
    lo = (a - hi.astype(F32)).astype(BF16)
    return _dot(hi, b) + _dot(lo, b)


def _head_q(q_ref, hh, lo):
    blk = q_ref[0, :, (hh // 2) * LANES:(hh // 2 + 1) * LANES]
    keep = lo if hh % 2 == 0 else jnp.logical_not(lo)
    return jnp.where(keep, blk, jnp.zeros_like(blk))


def _pair_heads(o, lo):
    return jnp.concatenate([jnp.where(lo, o[0], o[1]), jnp.where(lo, o[2], o[3])], axis=1)


def _nsa_sel_kernel(q_ref, kc_ref, vc_ref, ov_ref, bt_ref, o_ref, sel_ref, *, tq, nband, n_slc):
    i = pl.program_id(2)
    ncp = kc_ref.shape[3]
    nsp = sel_ref.shape[3]
    per = tq // CMP_STRIDE
    var = jnp.minimum(i, 1)
    bs = pl.multiple_of(per * jnp.maximum(i - 1, 0), per)
    lo = lax.broadcasted_iota(jnp.int32, (1, LANES), 1) < NSA_DH
    kc = kc_ref[0, 0, 0]
    vc = vc_ref[0, 0, 0]
    kcb = kc_ref[0, 0, 0, pl.ds(bs, nband), :]
    vcb = vc_ref[0, 0, 0, pl.ds(bs, nband), :]
    far_ok = lax.broadcasted_iota(jnp.int32, (1, ncp), 1) < per * (i - 1)
    hs = range(NSA_GROUP)
    qh = [_head_q(q_ref, hh, lo) for hh in hs]
    s_far = [jnp.where(far_ok, _dot_nt(q, kc), NEG) for q in qh]
    s_band = [_dot_nt(qh[hh], kcb) + bt_ref[var, hh] for hh in hs]
    m = [jnp.maximum(jnp.max(a, axis=1, keepdims=True), jnp.max(b, axis=1, keepdims=True))
         for a, b in zip(s_far, s_band)]
    m = [jnp.where(x < 0.5 * NEG, 0.0, x) for x in m]
    p_far = [jnp.exp2(a - x) for a, x in zip(s_far, m)]
    p_band = [jnp.exp2(b - x) for b, x in zip(s_band, m)]
    l = [jnp.sum(a, axis=1, keepdims=True) + jnp.sum(b, axis=1, keepdims=True) for a, b in zip(p_far, p_band)]
    inv = [1.0 / jnp.where(x == 0.0, 1.0, x) for x in l]
    outs = [(_dot(a.astype(BF16), vc) + _dot(b.astype(BF16), vcb)) * x for a, b, x in zip(p_far, p_band, inv)]
    ps_far = p_far[0] * inv[0]
    ps_band = p_band[0] * inv[0]
    for hh in range(1, NSA_GROUP):
        ps_far = ps_far + p_far[hh] * inv[hh]
        ps_band = ps_band + p_band[hh] * inv[hh]
    o_ref[0] = _pair_heads(outs, lo).astype(o_ref.dtype)

    imp = _dot_split(ps_far, ov_ref[...]) + _dot_split(ps_band, ov_ref[pl.ds(bs, nband), :])
    blk = lax.broadcasted_iota(jnp.int32, (1, nsp), 1)
    blk_f = blk.astype(F32)
    qpos = i * tq + lax.broadcasted_iota(jnp.int32, (tq, 1), 0)
    cur = lax.shift_right_logical(qpos, int(math.log2(SLC_LEN)))
    forced = (blk == 0) | (blk == cur) | (blk == cur - 1)
    work = jnp.where(forced, FORCE_SCORE, jnp.where(blk <= cur, imp, NEG))
    work = jnp.where(blk < n_slc, work, -jnp.inf)
    ngrp = 4
    rg = tq // ngrp
    works = [work[r * rg:(r + 1) * rg] for r in range(ngrp)]
    sels = [jnp.zeros((rg, nsp), F32) for _ in range(ngrp)]
    for _ in range(min(SLC_TOPK, n_slc)):
        ms = [jnp.max(w, axis=1, keepdims=True) for w in works]
        firsts = [jnp.min(jnp.where(w == m, blk_f, float(nsp)), axis=1, keepdims=True) for w, m in zip(works, ms)]
        picks = [blk_f == f for f in firsts]
        sels = [jnp.where(p, 1.0, s) for p, s in zip(picks, sels)]
        works = [jnp.where(p, -jnp.inf, w) for p, w in zip(picks, works)]
    sel_ref[0, 0] = jnp.concatenate(sels, axis=0).astype(sel_ref.dtype)


def nsa_select(q, cmp_kv, rel_bias, tq=512):
    b, t, _ = q.shape
    ncp = t // CMP_STRIDE
    n_cmp = ncp - 1
    n_slc = t // SLC_LEN
    nsp = max(LANES, n_slc)
    per = tq // CMP_STRIDE
    nband = 2 * per
    n = np.arange(ncp)[:, None]
    s = np.arange(nsp)[None, :]
    ov = ((CMP_STRIDE * n < SLC_LEN * s + SLC_LEN) & (CMP_STRIDE * n + CMP_LEN > SLC_LEN * s)
          & (n < n_cmp) & (s < n_slc)).astype(np.float32)
    qi = np.arange(tq)[:, None]
    nj = np.arange(nband)[None, :]
    end = CMP_STRIDE * nj + CMP_LEN - 1
    dist = np.stack([qi - end, tq + qi - end])
    bt = _bias_table(rel_bias, dist, dist >= 0)
    bt = bt.reshape(NSA_KV_HEADS, NSA_GROUP, 2, tq, nband).transpose(0, 2, 1, 3, 4)
    bt = bt.reshape(NSA_KV_HEADS * 2, NSA_GROUP, tq, nband)
    gw = NSA_GROUP * NSA_DH
    return pl.pallas_call(
        functools.partial(_nsa_sel_kernel, tq=tq, nband=nband, n_slc=n_slc),
        grid=(b, NSA_KV_HEADS, t // tq),
        in_specs=[pl.BlockSpec((1, tq, gw), lambda bi, k, i: (bi, i, k)),
                  pl.BlockSpec((1, 1, 1, ncp, LANES), lambda bi, k, i: (0, bi, k, 0, 0)),
                  pl.BlockSpec((1, 1, 1, ncp, LANES), lambda bi, k, i: (1, bi, k, 0, 0)),
                  _full((ncp, nsp)),
                  pl.BlockSpec((2, NSA_GROUP, tq, nband), lambda bi, k, i: (k, 0, 0, 0))],
        out_specs=[pl.BlockSpec((1, tq, gw), lambda bi, k, i: (bi, i, k)),
                   pl.BlockSpec((1, 1, tq, nsp), lambda bi, k, i: (bi, k, i, 0))],
        out_shape=[jax.ShapeDtypeStruct((b, t, NSA_W), BF16),
                   jax.ShapeDtypeStruct((b, NSA_KV_HEADS, t, nsp), BF16)],
        compiler_params=_cparams(("arbitrary", "arbitrary", "arbitrary")),
        name="nsa_select",
    )(q, cmp_kv, cmp_kv, jnp.asarray(ov, dtype=BF16), bt)


def _nsa_main_kernel(q_ref, ks_ref, vs_ref, kw0_ref, kw1_ref, kw2_ref, vw0_ref, vw1_ref, vw2_ref, sel_ref,
                     ocmp_ref, gate_ref, tb_ref, wm_ref, eg_ref, o_ref, *, tq):
    i = pl.program_id(2)
    nsp = sel_ref.shape[3]
    g = NSA_GROUP
    lo = lax.broadcasted_iota(jnp.int32, (1, LANES), 1) < NSA_DH
    qst = jnp.concatenate([_head_q(q_ref, hh, lo) for hh in range(g)], axis=0)
    sel = sel_ref[0, 0]
    blk_row = lax.broadcasted_iota(jnp.int32, (nsp, 1), 0)
    causal = (lax.broadcasted_iota(jnp.int32, (tq, tq), 1) <= lax.broadcasted_iota(jnp.int32, (tq, tq), 0))
    one = jnp.ones((1, LANES), BF16)

    def sel_step(jt, carry, near, tw):
        m, acc = carry
        start = pl.multiple_of(jnp.maximum(jt, 0) * tw, tw)
        kt = ks_ref[0, pl.ds(start, tw), :]
        vt = jnp.where(lo, vs_ref[0, pl.ds(start, tw), :], one)
        col_blk = lax.shift_right_logical(lax.broadcasted_iota(jnp.int32, (1, tw), 1), int(math.log2(SLC_LEN)))
        expand = jnp.where(blk_row - (tw // SLC_LEN) * jt == col_blk, 1.0, 0.0).astype(BF16)
        keep = _dot(sel, expand)
        madd = (keep - 1.0) * (-NEG)
        if near == 2:
            madd = jnp.where(causal, madd, NEG)
        s = _dot_nt(qst, kt).reshape(g, tq, tw) + madd[None]
        if near is not None:
            s = s + tb_ref[:, :, near * tq:(near + 1) * tq]
        s = s.reshape(g * tq, tw)
        m_new = jnp.maximum(m, jnp.max(s, axis=1, keepdims=True))
        p = jnp.exp2(s - m_new)
        acc = jnp.exp2(m - m_new) * acc + _dot(p.astype(BF16), vt)
        return m_new, acc

    n_far = jnp.maximum(i - 2, 0)
    carry = (jnp.full((g * tq, 1), NEG, F32), jnp.zeros((g * tq, LANES), F32))
    carry = lax.fori_loop(0, lax.shift_right_logical(n_far, 1), lambda j, c: sel_step(j, c, None, 2 * tq), carry)
    carry = lax.cond((n_far & 1) == 1, lambda c: sel_step(i - 3, c, None, tq), lambda c: c, carry)
    for near in range(3):
        carry = sel_step(i - 2 + near, carry, near, tq)
    o_slc = (carry[1] / pltpu.roll(carry[1], NSA_DH, 1)).reshape(g, tq, LANES)

    var = jnp.minimum(i, 2)
    kws = (kw0_ref, kw1_ref, kw2_ref)
    vws = (vw0_ref, vw1_ref, vw2_ref)
    sw = []
    for near in range(3):
        s = _dot_nt(qst, kws[near][0]).reshape(g, tq, tq)
        s = s + tb_ref[:, :, near * tq:(near + 1) * tq] + wm_ref[var, :, near * tq:(near + 1) * tq][None]
        sw.append(s.reshape(g * tq, tq))
    m = jnp.maximum(jnp.maximum(jnp.max(sw[0], axis=1, keepdims=True), jnp.max(sw[1], axis=1, keepdims=True)),
                    jnp.max(sw[2], axis=1, keepdims=True))
    acc = jnp.zeros((g * tq, LANES), F32)
    for near in range(3):
        p = jnp.exp2(sw[near] - m)
        acc = acc + _dot(p.astype(BF16), jnp.where(lo, vws[near][0], one))
    o_win = (acc / pltpu.roll(acc, NSA_DH, 1)).reshape(g, tq, LANES)

    pair = lambda o: jnp.concatenate([jnp.where(lo, o[0], pltpu.roll(o[1], NSA_DH, 1)),
                                      jnp.where(lo, o[2], pltpu.roll(o[3], NSA_DH, 1))], axis=1)
    gates = _dot_hi(_sigmoid(gate_ref[0]), eg_ref[0])
    gw = g * NSA_DH
    out = (gates[:, 0:gw] * ocmp_ref[0].astype(F32)
           + gates[:, gw:2 * gw] * pair(o_slc) + gates[:, 2 * gw:3 * gw] * pair(o_win))
    o_ref[0] = out.astype(o_ref.dtype)


def nsa_main(q, ksw, vsw, sel, o_cmp, small, rel_bias, tq=256):
    b, t, _ = q.shape
    nsp = sel.shape[-1]
    g = NSA_GROUP
    gw = g * NSA_DH
    qi = np.arange(tq)[:, None]
    c = np.arange(3 * tq)[None, :]
    dist = qi + 2 * tq - c
    tb = _bias_table(rel_bias, dist, np.ones_like(dist, bool))
    wm = np.zeros((3, tq, 3 * tq), np.float32)
    for var in range(3):
        exists = c >= tq * (2 - var)
        wm[var] = np.where((dist >= 0) & (dist < WINDOW) & exists, 0.0, NEG)
    eg = np.zeros((NSA_KV_HEADS, LANES, 3 * gw), np.float32)
    for k in range(NSA_KV_HEADS):
        for hh in range(g):
            for br in range(3):
                eg[k, (k * g + hh) * 3 + br, br * gw + hh * NSA_DH:br * gw + (hh + 1) * NSA_DH] = 1.0
    near = lambda off, col: pl.BlockSpec(
        (1, tq, LANES), lambda bi, k, i: (bi, jnp.maximum(i - off, 0), col + k))
    return pl.pallas_call(
        functools.partial(_nsa_main_kernel, tq=tq),
        grid=(b, NSA_KV_HEADS, t // tq),
        in_specs=[pl.BlockSpec((1, tq, gw), lambda bi, k, i: (bi, i, k)),
                  pl.BlockSpec((1, t, LANES), lambda bi, k, i: (bi, 0, k)),
                  pl.BlockSpec((1, t, LANES), lambda bi, k, i: (bi, 0, k)),
                  near(2, 2), near(1, 2), near(0, 2), near(2, 2), near(1, 2), near(0, 2),
                  pl.BlockSpec((1, 1, tq, nsp), lambda bi, k, i: (bi, k, i, 0)),
                  pl.BlockSpec((1, tq, gw), lambda bi, k, i: (bi, i, k)),
                  pl.BlockSpec((1, tq, LANES), lambda bi, k, i: (bi, i, 0)),
                  pl.BlockSpec((g, tq, 3 * tq), lambda bi, k, i: (k, 0, 0)),
                  _full((3, tq, 3 * tq)),
                  pl.BlockSpec((1, LANES, 3 * gw), lambda bi, k, i: (k, 0, 0))],
        out_specs=pl.BlockSpec((1, tq, gw), lambda bi, k, i: (bi, i, k)),
        out_shape=jax.ShapeDtypeStruct((b, t, NSA_W), BF16),
        compiler_params=_cparams(("arbitrary", "arbitrary", "arbitrary")),
        name="nsa_main",
    )(q, ksw, vsw, ksw, ksw, ksw, vsw, vsw, vsw, sel, o_cmp, small, tb, jnp.asarray(wm), jnp.asarray(eg))


def _gla_kernel(qk_ref, v_ref, r_ref, sm_ref, wg_ref, bg_ref, on_ref, tril_ref, o_ref, s_scr, q_scr, k_scr, kd_scr,
                *, tc):
    c = GLA_CHUNK

    @pl.when(pl.program_id(1) == 0)
    def _():
        s_scr[...] = jnp.zeros_like(s_scr)

    kw = GLA_KW
    log_a = _log_sigmoid(_dot_hi(sm_ref[0], wg_ref[...]) + bg_ref[...]) * (1.0 / GLA_TAU)
    gcum = _dot_hi(tril_ref[...], log_a)
    q = qk_ref[0, :, 0:kw] * (GLA_DK ** -0.5)
    k = qk_ref[0, :, kw:2 * kw]
    q_scr[...] = q * jnp.exp(gcum)
    k_scr[...] = k * jnp.exp(-gcum)
    ri = lax.broadcasted_iota(jnp.int32, (c, c), 0)
    ci = lax.broadcasted_iota(jnp.int32, (c, c), 1)
    causal = ci <= ri
    lo = lax.broadcasted_iota(jnp.int32, (1, LANES), 1) < GLA_DK

    def chunk(n, _):
        sl = pl.ds(pl.multiple_of(n * c, c), c)
        for p in range(GLA_HEADS // 2):
            pl_ = slice(p * LANES, (p + 1) * LANES)
            qd = q_scr[sl, pl_]
            ki = k_scr[sl, pl_]
            kd = kd_scr[sl, pl_]
            for hh in range(2):
                h = 2 * p + hh
                keep = lo if hh == 0 else jnp.logical_not(lo)
                qm = jnp.where(keep, qd, 0.0).astype(BF16)
                v = v_ref[0, sl, h * GLA_DV:(h + 1) * GLA_DV]
                attn = jnp.where(causal, _dot_nt(qm, ki.astype(BF16)), 0.0)
                st = s_scr[h]
                o = _dot(attn.astype(BF16), v.astype(BF16)) + _dot_nt(qm, st.astype(BF16))
                on = o * lax.rsqrt(jnp.mean(o * o, axis=-1, keepdims=True) + EPS) * on_ref[...]
                o_ref[0, sl, h * GLA_DV:(h + 1) * GLA_DV] = (
                    on * _silu(r_ref[0, sl, h * GLA_DV:(h + 1) * GLA_DV])).astype(o_ref.dtype)
                s_scr[h] = st * kd_scr[pl.ds(tc + n * 8, 1), pl_] + _dot_tn(v.astype(BF16), kd.astype(BF16))
        return 0

    for n in range(tc // c):
        gl = gcum[n * c + c - 1:n * c + c, :]
        kd_scr[n * c:(n + 1) * c, :] = k[n * c:(n + 1) * c, :] * jnp.exp(gl - gcum[n * c:(n + 1) * c, :])
        kd_scr[tc + n * 8:tc + n * 8 + 8, :] = jnp.broadcast_to(jnp.exp(gl), (8, kw))
    lax.fori_loop(0, tc // c, chunk, 0)


def gated_linear_attention(qkvr, small, wg_up, bg, on_gain, tc=512):
    b, t, _ = qkvr.shape
    wg = jnp.zeros((LANES, GLA_KW), F32).at[3 * NSA_HEADS:3 * NSA_HEADS + GLA_GATE_RANK].set(wg_up)
    idx = np.arange(tc)
    tril = ((idx[:, None] >= idx[None, :]) & (idx[:, None] // GLA_CHUNK == idx[None, :] // GLA_CHUNK))
    nchunk = tc // GLA_CHUNK
    return pl.pallas_call(
        functools.partial(_gla_kernel, tc=tc),
        grid=(b, t // tc),
        in_specs=[pl.BlockSpec((1, tc, 2 * GLA_KW), lambda i, j: (i, j, 0)),
                  pl.BlockSpec((1, tc, GLA_W), lambda i, j: (i, j, 1)),
                  pl.BlockSpec((1, tc, GLA_W), lambda i, j: (i, j, 2)),
                  pl.BlockSpec((1, tc, LANES), lambda i, j: (i, j, 0)),
                  _full((LANES, GLA_KW)), _full((1, GLA_KW)), _full((1, GLA_DV)), _full((tc, tc))],
        out_specs=pl.BlockSpec((1, tc, GLA_W), lambda i, j: (i, j, 0)),
        out_shape=jax.ShapeDtypeStruct((b, t, GLA_W), BF16),
        scratch_shapes=[pltpu.VMEM((GLA_HEADS, GLA_DV, LANES), F32), pltpu.VMEM((tc, GLA_KW), F32),
                        pltpu.VMEM((tc, GLA_KW), F32), pltpu.VMEM((tc + 8 * nchunk, GLA_KW), F32)],
        compiler_params=_cparams(("arbitrary", "arbitrary")),
        name="gla",
    )(qkvr, qkvr, qkvr, small, wg, bg.reshape(1, GLA_KW), on_gain.reshape(1, GLA_DV),
      jnp.asarray(tril.astype(np.float32)))


def odd_mixer(h, w_in, nsa_qn, nsa_kn, nsa_pos, nsa_cmp_w1, nsa_cmp_w2, gla_wg_up, gla_bg, gla_on, rel_bias):
    cuts = np.cumsum((0,) + OD_SIZES)
    col = lambda i: w_in[:, cuts[i]:cuts[i + 1]]
    wb = lambda a: a.astype(BF16)
    dup = lambda a: jnp.concatenate([a[:, :NSA_DH], a[:, :NSA_DH], a[:, NSA_DH:], a[:, NSA_DH:]], axis=1)
    ep = _head_norm_epilogue(NSA_DH)
    bd = _block_diag_ones(NSA_W, NSA_DH)
    qg = (jnp.tile(nsa_qn, NSA_HEADS) * (NSA_DH ** -0.5 * LOG2E)).reshape(1, NSA_W)
    kg = jnp.tile(nsa_kn, NSA_HEADS).reshape(1, NSA_W)
    w_small = jnp.zeros((w_in.shape[0], LANES), F32)
    w_small = w_small.at[:, 0:24].set(col(7)).at[:, 24:40].set(col(11))
    nq, kcvc, ksw, vsw, small, qkvr = proj_multi(
        h, [(col(0), BF16, ep, (bd, qg)),
            (jnp.concatenate([col(1), col(2)], axis=1), F32, None, ()),
            (jnp.concatenate([dup(col(3)), dup(col(5))], axis=1), BF16, ep, (bd, kg)),
            (jnp.concatenate([dup(col(4)), dup(col(6))], axis=1), BF16, None, ()),
            (w_small, F32, None, ()),
            (jnp.concatenate([col(8), col(9), col(10), col(12)], axis=1), F32, None, ())], name="proj_odd")
    cmp_kv = nsa_compress(kcvc, nsa_pos, nsa_cmp_w1, nsa_cmp_w2, nsa_kn)
    o_cmp, sel = nsa_select(nq, cmp_kv, rel_bias)
    o_nsa = nsa_main(nq, ksw, vsw, sel, o_cmp, small, rel_bias)
    o_gla = gated_linear_attention(qkvr, small, gla_wg_up, gla_bg, gla_on)
    return o_nsa, o_gla


MOE_TM = 256
MOE_ROWS = 512


def _first_index(mask_val, idx, big, axis):
    return jnp.min(jnp.where(mask_val, idx, big), axis=axis, keepdims=True)


def _route_kernel(h_ref, rt_ref, b_ref, up_ref, eid_ref, rank_ref, w_ref, cnt_ref, run):
    tm = h_ref.shape[0]
    ne = N_EXPERTS
    gsz = ne // N_GROUPS

    @pl.when(pl.program_id(0) == 0)
    def _():
        run[...] = jnp.zeros_like(run)

    scores = _sigmoid(_dot_nt(rt_ref[...], h_ref[...], HI))
    biased = scores + b_ref[...]
    b3 = biased.reshape(N_GROUPS, gsz, tm)
    i3 = lax.broadcasted_iota(jnp.int32, (1, gsz, 1), 1).astype(F32)
    m1 = jnp.max(b3, axis=1, keepdims=True)
    f1 = _first_index(b3 == m1, i3, float(gsz), 1)
    m2 = jnp.max(jnp.where(i3 == f1, -jnp.inf, b3), axis=1, keepdims=True)
    gs = (m1 + m2).reshape(N_GROUPS, tm)
    gidx = lax.broadcasted_iota(jnp.int32, (N_GROUPS, 1), 0).astype(F32)
    gmask = jnp.zeros((N_GROUPS, tm), F32)
    for _ in range(TOPK_GROUPS):
        m = jnp.max(gs, axis=0, keepdims=True)
        pick = gidx == _first_index(gs == m, gidx, float(N_GROUPS), 0)
        gmask = jnp.where(pick, 1.0, gmask)
        gs = jnp.where(pick, -jnp.inf, gs)
    emask = jnp.broadcast_to(gmask.reshape(N_GROUPS, 1, tm), (N_GROUPS, gsz, tm)).reshape(ne, tm)
    work = jnp.where(emask > 0.5, biased, -jnp.inf)
    eidx = lax.broadcasted_iota(jnp.int32, (ne, 1), 0).astype(F32)
    picks, eids, ws = [], [], []
    for _ in range(TOP_K):
        m = jnp.max(work, axis=0, keepdims=True)
        first = _first_index(work == m, eidx, float(ne), 0)
        pick = eidx == first
        picks.append(pick)
        eids.append(first)
        ws.append(jnp.sum(jnp.where(pick, scores, 0.0), axis=0, keepdims=True))
        work = jnp.where(pick, -jnp.inf, work)
    wsum = ws[0]
    for k in range(1, TOP_K):
        wsum = wsum + ws[k]
    chosen = jnp.zeros((ne, tm), F32)
    for pick in picks:
        chosen = jnp.where(pick, 1.0, chosen)
    pos = run[...] + _dot(chosen.astype(BF16), up_ref[...])
    run[...] = run[...] + jnp.sum(chosen, axis=1, keepdims=True)
    cnt_ref[...] = run[...]
    row = lax.broadcasted_iota(jnp.int32, (8, 1), 0)
    eid_o = jnp.zeros((8, tm), F32)
    rank_o = jnp.zeros((8, tm), F32)
    w_o = jnp.zeros((LANES, tm), F32)
    rowl = lax.broadcasted_iota(jnp.int32, (LANES, 1), 0)
    for k in range(TOP_K):
        rk = jnp.sum(jnp.where(picks[k], pos, 0.0), axis=0, keepdims=True)
        eid_o = jnp.where(row == k, eids[k], eid_o)
        rank_o = jnp.where(row == k, rk, rank_o)
        w_o = jnp.where(rowl == k, ws[k] / wsum * ROUTE_SCALE, w_o)
    eid_ref[0] = eid_o.astype(jnp.int32)
    rank_ref[0] = rank_o.astype(jnp.int32)
    w_ref[...] = w_o.T


def moe_route(h2, router, e_bias, tm=MOE_TM):
    nt, d = h2.shape
    ne = N_EXPERTS
    up = jnp.asarray(np.triu(np.ones((tm, tm), np.float32), 1), dtype=BF16)
    nb = nt // tm
    return pl.pallas_call(
        _route_kernel,
        grid=(nb,),
        in_specs=[pl.BlockSpec((tm, d), lambda i: (i, 0)), _full((ne, d)), _full((ne, 1)), _full((tm, tm))],
        out_specs=[pl.BlockSpec((1, 8, tm), lambda i: (i, 0, 0)),
                   pl.BlockSpec((1, 8, tm), lambda i: (i, 0, 0)),
                   pl.BlockSpec((tm, LANES), lambda i: (i, 0)),
                   _full((ne, 1))],
        out_shape=[jax.ShapeDtypeStruct((nb, 8, tm), jnp.int32), jax.ShapeDtypeStruct((nb, 8, tm), jnp.int32),
                   jax.ShapeDtypeStruct((nt, LANES), F32), jax.ShapeDtypeStruct((ne, 1), F32)],
        scratch_shapes=[pltpu.VMEM((ne, 1), F32)],
        compiler_params=_cparams(("arbitrary",)),
        name="moe_route",
    )(h2, router.T, e_bias.reshape(ne, 1), up)


def _dispatch_kernel(dest_ref, h_ref, xs_ref, sem):
    tm = h_ref.shape[0]

    def copy(t, row):
        return pltpu.make_async_copy(h_ref.at[pl.ds(t, 1), :], xs_ref.at[pl.ds(row, 1), :], sem)

    def issue(t, _):
        for k in range(TOP_K):
            copy(t, dest_ref[0, k, t]).start()
        return 0

    def drain(t, _):
        for k in range(TOP_K):
            copy(0, 0).wait()
        return 0

    lax.fori_loop(0, tm, issue, 0, unroll=4)
    lax.fori_loop(0, tm, drain, 0, unroll=4)


def moe_dispatch(h2, dest, tm=MOE_TM):
    nt, d = h2.shape
    return pl.pallas_call(
        _dispatch_kernel,
        grid=(nt // tm,),
        in_specs=[pl.BlockSpec((1, 8, tm), lambda i: (i, 0, 0), memory_space=pltpu.SMEM),
                  pl.BlockSpec((tm, d), lambda i: (i, 0))],
        out_specs=pl.BlockSpec(memory_space=pl.ANY),
        scratch_shapes=[pltpu.SemaphoreType.DMA(())],
        out_shape=jax.ShapeDtypeStruct((nt * TOP_K, d), F32),
        compiler_params=_cparams(("arbitrary",)),
        name="moe_dispatch",
    )(dest, h2)


def _ffn_kernel(blk_ref, exp_ref, lo_ref, hi_ref, first_ref, valid_ref, x_ref, wg_ref, wu_ref, wd_ref, o_ref,
                wg_b, wu_b, wd_b):
    i = pl.program_id(0)
    rows = x_ref.shape[0]

    @pl.when((i == 0) | (exp_ref[i] != exp_ref[jnp.maximum(i - 1, 0)]))
    def _():
        wg_b[...] = wg_ref[0].astype(BF16)
        wu_b[...] = wu_ref[0].astype(BF16)
        wd_b[...] = wd_ref[0].astype(BF16)

    @pl.when(valid_ref[i] == 1)
    def _():
        x = x_ref[...].astype(BF16)
        a = _dot(x, wg_b[...])
        u = _dot(x, wu_b[...])
        y = _dot((_silu(a) * u).astype(BF16), wd_b[...])
        r = blk_ref[i] * rows + lax.broadcasted_iota(jnp.int32, (rows, 1), 0)
        y = jnp.where((r >= lo_ref[i]) & (r < hi_ref[i]), y, 0.0)

        @pl.when(first_ref[i] == 1)
        def _():
            o_ref[...] = y

        @pl.when(first_ref[i] == 0)
        def _():
            o_ref[...] = o_ref[...] + y


def _ffn_items(counts, n_rows, rows):
    ne = N_EXPERTS
    nblk = n_rows // rows
    n_items = nblk + ne - 1
    ends = jnp.cumsum(counts)
    starts = ends - counts
    first_blk = starts // rows
    last_blk = jnp.maximum(ends - 1, 0) // rows
    per_e = jnp.where(counts > 0, last_blk - first_blk + 1, 0)
    item_end = jnp.cumsum(per_e)
    item_start = item_end - per_e
    total = item_end[-1]
    i = jnp.arange(n_items, dtype=jnp.int32)
    ic = jnp.minimum(i, total - 1)
    e = jnp.sum((item_end[None, :] <= ic[:, None]).astype(jnp.int32), axis=1)
    blk = first_blk[e] + ic - item_start[e]
    lo = jnp.maximum(starts[e], blk * rows)
    hi = jnp.minimum(ends[e], (blk + 1) * rows)
    valid = (i < total).astype(jnp.int32)
    first = (lo == blk * rows).astype(jnp.int32)
    return starts, (blk.astype(jnp.int32), e, lo.astype(jnp.int32), hi.astype(jnp.int32), first, valid)


def moe_ffn_sorted(xs, items, wg, wu, wd, rows=MOE_ROWS):
    n_rows, d = xs.shape
    n_items = items[0].shape[0]
    de = wg.shape[-1]
    return pl.pallas_call(
        _ffn_kernel,
        grid_spec=pltpu.PrefetchScalarGridSpec(
            num_scalar_prefetch=6,
            grid=(n_items,),
            in_specs=[pl.BlockSpec((rows, d), lambda i, blk, e, *_: (blk[i], 0)),
                      pl.BlockSpec((1, d, de), lambda i, blk, e, *_: (e[i], 0, 0)),
                      pl.BlockSpec((1, d, de), lambda i, blk, e, *_: (e[i], 0, 0)),
                      pl.BlockSpec((1, de, d), lambda i, blk, e, *_: (e[i], 0, 0))],
            out_specs=pl.BlockSpec((rows, d), lambda i, blk, e, *_: (blk[i], 0)),
            scratch_shapes=[pltpu.VMEM((d, de), BF16), pltpu.VMEM((d, de), BF16), pltpu.VMEM((de, d), BF16)]),
        out_shape=jax.ShapeDtypeStruct((n_rows, d), F32),
        compiler_params=_cparams(("arbitrary",)),
        name="moe_ffn",
    )(*items, xs, wg, wu, wd)


def _combine_kernel(dest_ref, ys_ref, w_ref, h_ref, x_ref, g_ref, sg_ref, su_ref, sd_ref, o_ref, buf, sem):
    tm = h_ref.shape[0]

    def copy(t, k, row):
        return pltpu.make_async_copy(ys_ref.at[pl.ds(row, 1), :], buf.at[k, pl.ds(t, 1), :], sem)

    def issue(t, _):
        for k in range(TOP_K):
            copy(t, k, dest_ref[0, k, t]).start()
        return 0

    def drain(t, _):
        for k in range(TOP_K):
            copy(0, 0, 0).wait()
        return 0

    lax.fori_loop(0, tm, issue, 0, unroll=4)
    hb = h_ref[...].astype(BF16)
    y = _dot((_silu(_dot(hb, sg_ref[...])) * _dot(hb, su_ref[...])).astype(BF16), sd_ref[...])
    lax.fori_loop(0, tm, drain, 0, unroll=4)
    w = w_ref[...]
    for k in range(TOP_K):
        y = y + w[:, k:k + 1] * buf[k]
    o_ref[...] = x_ref[...] + g_ref[0] * y


def moe_combine(ys, dest, w, h2, x2, gate, sg, su, sd, seq, tm=MOE_TM):
    nt, d = h2.shape
    ds_ = sg.shape[-1]
    per_b = seq // tm
    tile = lambda: pl.BlockSpec((tm, d), lambda i: (i, 0))
    return pl.pallas_call(
        _combine_kernel,
        grid=(nt // tm,),
        in_specs=[pl.BlockSpec((1, 8, tm), lambda i: (i, 0, 0), memory_space=pltpu.SMEM),
                  pl.BlockSpec(memory_space=pl.ANY),
                  pl.BlockSpec((tm, LANES), lambda i: (i, 0)), tile(), tile(),
                  pl.BlockSpec((1, 1, d), lambda i: (i // per_b, 0, 0)),
                  _full((d, ds_)), _full((d, ds_)), _full((ds_, d))],
        out_specs=tile(),
        scratch_shapes=[pltpu.VMEM((TOP_K, tm, d), F32), pltpu.SemaphoreType.DMA(())],
        out_shape=jax.ShapeDtypeStruct((nt, d), F32),
        compiler_params=_cparams(("arbitrary",)),
        name="moe_combine",
    )(dest, ys, w, h2, x2, gate, sg.astype(BF16), su.astype(BF16), sd.astype(BF16))


def moe_layer(x, g_norm, sc, sh, gate, router, e_bias, wg, wu, wd, sg, su, sd):
    b, t, d = x.shape
    nt = b * t
    h = ln_mod(x, g_norm, sc, sh, F32)
    h2 = h.reshape(nt, d)
    eid, rank, w, counts = moe_route(h2, router, e_bias)
    starts, items = _ffn_items(counts.reshape(-1).astype(jnp.int32), nt * TOP_K, MOE_ROWS)
    hit = eid[..., None] == jnp.arange(N_EXPERTS, dtype=jnp.int32)
    dest = jnp.sum(jnp.where(hit, starts.astype(jnp.int32), 0), axis=-1) + rank
    xs = moe_dispatch(h2, dest)
    ys = moe_ffn_sorted(xs, items, wg, wu, wd)
    out = moe_combine(ys, dest, w, h2, x.reshape(nt, d), gate.reshape(b, 1, d), sg, su, sd, t)
    return out.reshape(b, t, d)


def kernel(x, c, ada_w, ada_b, norm_mix, norm_ffn, rel_bias, ev_w_in, ev_w_out, fox_fb, fox_qn, fox_kn, gdn_conv, gdn_a_log, gdn_dt_bias, gdn_on, od_w_in, od_w_out, nsa_qn, nsa_kn, nsa_pos, nsa_cmp_w1, nsa_cmp_w2, gla_wg_up, gla_bg, gla_on, moe_router, moe_bias, moe_wg, moe_wu, moe_wd, sh_wg, sh_wu, sh_wd):
    d = x.shape[-1]
    depth = ada_w.shape[0]
    mod = adaln(c, ada_w, ada_b)
    for layer in range(depth):
        sh1, sc1, g1, sh2, sc2, g2 = [mod[layer, :, i * d:(i + 1) * d] for i in range(6)]
        h = ln_mod(x, norm_mix[layer], sc1, sh1, BF16)
        j = layer // 2
        if layer % 2 == 0:
            y1, y2 = even_mixer(h, ev_w_in[j], fox_fb[j], fox_qn[j], fox_kn[j], gdn_conv[j], gdn_a_log[j],
                                gdn_dt_bias[j], gdn_on[j])
            w_out = ev_w_out[j]
        else:
            y1, y2 = odd_mixer(h, od_w_in[j], nsa_qn[j], nsa_kn[j], nsa_pos[j], nsa_cmp_w1[j], nsa_cmp_w2[j],
                               gla_wg_up[j], gla_bg[j], gla_on[j], rel_bias)
            w_out = od_w_out[j]
        x = out_proj(y1, y2, w_out, x, g1)
        x = moe_layer(x, norm_ffn[layer], sc2, sh2, g2, moe_router[layer], moe_bias[layer], moe_wg[layer],
                      moe_wu[layer], moe_wd[layer], sh_wg[layer], sh_wu[layer], sh_wd[layer])
    return x
```

```python
import functools
import math

import numpy as np
import jax
import jax.numpy as jnp
from jax import lax
from jax.experimental import pallas as pl
from jax.experimental.pallas import tpu as pltpu

F32 = jnp.float32
BF16 = jnp.bfloat16
HI = lax.Precision.HIGHEST

EPS = 1e-6
LOG2E = math.log2(math.e)
NEG = -1e30
FORCE_SCORE = 1e9

FOX_HEADS, FOX_DH = 8, 64
GDN_HEADS, GDN_DH, GDN_CONV, GDN_CHUNK = 4, 128, 4, 64
NSA_HEADS, NSA_KV_HEADS, NSA_DH = 8, 2, 64
NSA_GROUP = NSA_HEADS // NSA_KV_HEADS
CMP_LEN, CMP_STRIDE, CMP_HIDDEN = 32, 16, 256
SLC_LEN, SLC_TOPK, WINDOW = 64, 16, 512
GLA_HEADS, GLA_DK, GLA_DV, GLA_GATE_RANK, GLA_TAU, GLA_CHUNK = 4, 64, 128, 16, 16.0, 64
REL_BUCKETS, REL_MAX_DIST = 32, 128
N_EXPERTS, TOP_K, D_EXPERT, D_SHARED = 64, 6, 256, 256
N_GROUPS, TOPK_GROUPS, ROUTE_SCALE = 8, 4, 2.5

FOX_W = FOX_HEADS * FOX_DH
GDN_W = GDN_HEADS * GDN_DH
NSA_W = NSA_HEADS * NSA_DH
NSA_KV_W = NSA_KV_HEADS * NSA_DH
GLA_KW = GLA_HEADS * GLA_DK
GLA_W = GLA_HEADS * GLA_DV
EV_SIZES = (FOX_W, FOX_W, FOX_W, FOX_HEADS, 3 * GDN_W, GDN_HEADS, GDN_HEADS, GDN_W)
OD_SIZES = (NSA_W,) + (NSA_KV_W,) * 6 + (3 * NSA_HEADS, GLA_KW, GLA_KW, GLA_W, GLA_GATE_RANK, GLA_W)

LANES = 128
ROW_GROUP = 64
VMEM_LIMIT = 56 * 1024 * 1024


def _cparams(sem, flags=None):
    return pltpu.CompilerParams(dimension_semantics=sem, vmem_limit_bytes=VMEM_LIMIT, flags=flags)


def _full(shape):
    n = len(shape)
    return pl.BlockSpec(shape, lambda *_: (0,) * n)


def _dot(a, b):
    return jnp.dot(a, b, preferred_element_type=F32)


def _dot_hi(a, b):
    return jnp.dot(a, b, precision=HI, preferred_element_type=F32)


def _dot_nt(a, b, precision=None):
    return lax.dot_general(a, b, (((1,), (1,)), ((), ())), precision=precision, preferred_element_type=F32)


def _dot_tn(a, b, precision=None):
    return lax.dot_general(a, b, (((0,), (0,)), ((), ())), precision=precision, preferred_element_type=F32)


def _sigmoid(x):
    return 1.0 / (1.0 + jnp.exp(-x))


def _silu(x):
    return x * _sigmoid(x)


def _softplus(x):
    return jnp.maximum(x, 0.0) + jnp.log(1.0 + jnp.exp(-jnp.abs(x)))


def _log_sigmoid(x):
    return -_softplus(-x)


def _adaln_kernel(c_ref, w_ref, b_ref, o_ref):
    c = c_ref[...]
    o_ref[0] = _dot_hi(_silu(c), w_ref[0]) + b_ref[0]


def adaln(c, ada_w, ada_b):
    depth, d, n = ada_w.shape
    b = c.shape[0]
    cp = jnp.zeros((8, d), F32).at[:b].set(c)
    tn = 1536
    out = pl.pallas_call(
        _adaln_kernel,
        grid=(depth, n // tn),
        in_specs=[_full((8, d)),
                  pl.BlockSpec((1, d, tn), lambda l, j: (l, 0, j)),
                  pl.BlockSpec((1, 1, tn), lambda l, j: (l, 0, j))],
        out_specs=pl.BlockSpec((1, 8, tn), lambda l, j: (l, 0, j)),
        out_shape=jax.ShapeDtypeStruct((depth, 8, n), F32),
        compiler_params=_cparams(("arbitrary", "arbitrary")),
        name="adaln",
    )(cp, ada_w, ada_b.reshape(depth, 1, n))
    return out[:, :b]


def _ln_kernel(x_ref, g_ref, sc_ref, sh_ref, o_ref):
    x = x_ref[0]
    y = x * lax.rsqrt(jnp.mean(x * x, axis=-1, keepdims=True) + EPS) * g_ref[...]
    o_ref[0] = (y * (1.0 + sc_ref[0]) + sh_ref[0]).astype(o_ref.dtype)


def ln_mod(x, g, sc, sh, out_dtype, tm=512):
    b, t, d = x.shape
    return pl.pallas_call(
        _ln_kernel,
        grid=(b, t // tm),
        in_specs=[pl.BlockSpec((1, tm, d), lambda i, j: (i, j, 0)),
                  _full((1, d)),
                  pl.BlockSpec((1, 1, d), lambda i, j: (i, 0, 0)),
                  pl.BlockSpec((1, 1, d), lambda i, j: (i, 0, 0))],
        out_specs=pl.BlockSpec((1, tm, d), lambda i, j: (i, j, 0)),
        out_shape=jax.ShapeDtypeStruct((b, t, d), out_dtype),
        compiler_params=_cparams(("arbitrary", "arbitrary")),
        name="ln_mod",
    )(x, g.reshape(1, d), sc.reshape(b, 1, d), sh.reshape(b, 1, d))


def proj(h, w, out_dtype, epilogue=None, extras=(), tm=512, name="proj"):
    b, t, d = h.shape
    n = w.shape[1]

    def kern(h_ref, w_ref, *rest):
        o_ref = rest[-1]
        y = _dot(h_ref[0], w_ref[...])
        if epilogue is not None:
            y = epilogue(y, *[e[...] for e in rest[:-1]])
        o_ref[0] = y.astype(out_dtype)

    return pl.pallas_call(
        kern,
        grid=(b, t // tm),
        in_specs=[pl.BlockSpec((1, tm, d), lambda i, j: (i, j, 0)), _full((d, n))]
                 + [_full(e.shape) for e in extras],
        out_specs=pl.BlockSpec((1, tm, n), lambda i, j: (i, j, 0)),
        out_shape=jax.ShapeDtypeStruct((b, t, n), out_dtype),
        compiler_params=_cparams(("arbitrary", "arbitrary")),
        name=name,
    )(h, w, *extras)


def proj_multi(h, groups, tm=512, name="proj"):
    b, t, d = h.shape
    widths = [g[0].shape[1] for g in groups]
    starts = np.cumsum([0] + widths)
    w_cat = jnp.concatenate([g[0] for g in groups], axis=1).astype(BF16)
    extras = [e for g in groups for e in g[3]]
    n_ex = [len(g[3]) for g in groups]
    n_out = len(groups)

    def kern(h_ref, w_ref, *rest):
        ex_refs = rest[:len(extras)]
        o_refs = rest[len(extras):]
        y = _dot(h_ref[0], w_ref[...])
        pos = 0
        for gi, (_, out_dtype, epilogue, _) in enumerate(groups):
            yg = y[:, starts[gi]:starts[gi + 1]]
            if epilogue is not None:
                yg = epilogue(yg, *[e[...] for e in ex_refs[pos:pos + n_ex[gi]]])
            pos += n_ex[gi]
            o_refs[gi][0] = yg.astype(out_dtype)

    return pl.pallas_call(
        kern,
        grid=(b, t // tm),
        in_specs=[pl.BlockSpec((1, tm, d), lambda i, j: (i, j, 0)), _full((d, int(starts[-1])))]
                 + [_full(e.shape) for e in extras],
        out_specs=[pl.BlockSpec((1, tm, n), lambda i, j: (i, j, 0)) for n in widths],
        out_shape=[jax.ShapeDtypeStruct((b, t, n), g[1]) for n, g in zip(widths, groups)],
        compiler_params=_cparams(("arbitrary", "arbitrary")),
        name=name,
    )(h, w_cat, *extras)


def _head_norm_epilogue(dh):
    inv = 1.0 / dh

    def ep(y, bd, gain):
        ssq = _dot((y * y).astype(BF16), bd)
        return y * lax.rsqrt(ssq * inv + EPS) * gain

    return ep


def _block_diag_ones(n, dh):
    i = np.arange(n) // dh
    return jnp.asarray((i[:, None] == i[None, :]).astype(np.float32), dtype=BF16)


def _outproj_kernel(y1_ref, y2_ref, wa_ref, wb_ref, x_ref, g_ref, o_ref):
    y = _dot(y1_ref[0], wa_ref[...]) + _dot(y2_ref[0], wb_ref[...])
    o_ref[0] = x_ref[0] + g_ref[0] * y


def out_proj(y1, y2, w_out, x, gate, tm=512):
    b, t, d = x.shape
    n1, n2 = y1.shape[-1], y2.shape[-1]
    wa = w_out[:n1].astype(BF16)
    wb = w_out[n1:].astype(BF16)
    return pl.pallas_call(
        _outproj_kernel,
        grid=(b, t // tm),
        in_specs=[pl.BlockSpec((1, tm, n1), lambda i, j: (i, j, 0)),
                  pl.BlockSpec((1, tm, n2), lambda i, j: (i, j, 0)),
                  _full((n1, d)), _full((n2, d)),
                  pl.BlockSpec((1, tm, d), lambda i, j: (i, j, 0)),
                  pl.BlockSpec((1, 1, d), lambda i, j: (i, 0, 0))],
        out_specs=pl.BlockSpec((1, tm, d), lambda i, j: (i, j, 0)),
        out_shape=jax.ShapeDtypeStruct((b, t, d), F32),
        compiler_params=_cparams(("arbitrary", "arbitrary")),
        name="out_proj",
    )(y1, y2, wa, wb, x, gate.reshape(b, 1, d))


def _decay_kernel(s_ref, fb_ref, tril_ref, place_ref, o_ref, carry):
    @pl.when(pl.program_id(1) == 0)
    def _():
        carry[...] = jnp.zeros_like(carry)

    tm = s_ref.shape[1]
    lf = _log_sigmoid(s_ref[0] + fb_ref[...])
    cum = _dot_hi(tril_ref[...], lf) + carry[...]
    carry[...] = cum[tm - 1:tm, :]
    x = cum * LOG2E
    hi = x.astype(BF16)
    r1 = x - hi.astype(F32)
    mid = r1.astype(BF16)
    low = (r1 - mid.astype(F32)).astype(BF16)
    o_ref[0] = _dot(jnp.concatenate([hi, mid, low], axis=1), place_ref[...]).astype(o_ref.dtype)


def fox_decay(small, fox_fb, tm=512):
    b, t, _ = small.shape
    fb = jnp.zeros((1, LANES), F32).at[0, :FOX_HEADS].set(fox_fb)
    tril = jnp.asarray(np.tril(np.ones((tm, tm), np.float32)))
    place = np.zeros((3 * LANES, FOX_W), np.float32)
    for h in range(FOX_HEADS):
        for j in range(3):
            place[j * LANES + h, (h // 2) * LANES + (FOX_DH if h % 2 == 0 else 0) + j] = 1.0
    return pl.pallas_call(
        _decay_kernel,
        grid=(b, t // tm),
        in_specs=[pl.BlockSpec((1, tm, LANES), lambda i, j: (i, j, 0)), _full((1, LANES)), _full((tm, tm)),
                  _full((3 * LANES, FOX_W))],
        out_specs=pl.BlockSpec((1, tm, FOX_W), lambda i, j: (i, j, 0)),
        out_shape=jax.ShapeDtypeStruct((b, t, FOX_W), BF16),
        scratch_shapes=[pltpu.VMEM((1, LANES), F32)],
        compiler_params=_cparams(("arbitrary", "arbitrary")),
        name="fox_decay",
    )(small, fb, tril, jnp.asarray(place, dtype=BF16))


def _fox_kernel(q_ref, k_ref, v_ref, f_ref, o_ref, *, tq, tk):
    i = pl.program_id(2)
    lane = lax.broadcasted_iota(jnp.int32, (1, LANES), 1)
    lo = lane < FOX_DH
    coef = jnp.where((lane & (FOX_DH - 1)) < 3, -1.0, 0.0).astype(BF16)
    q = q_ref[0]
    qs = (jnp.where(lo, q, coef), jnp.where(lo, coef, q))
    n_full = (i * tq) // tk
    causal = (lax.broadcasted_iota(jnp.int32, (tq, tk), 1)
              <= lax.broadcasted_iota(jnp.int32, (tq, tk), 0) + (i * tq - n_full * tk))

    def scores(j):
        start = pl.multiple_of(j * tk, tk)
        kt = k_ref[0, pl.ds(start, tk), :]
        ft = f_ref[0, pl.ds(start, tk), :]
        return _dot_nt(qs[0], jnp.where(lo, kt, ft)), _dot_nt(qs[1], jnp.where(lo, ft, kt))

    one = jnp.ones((1, LANES), BF16)

    def update(j, s_pair, carry, diag):
        vt = v_ref[0, pl.ds(pl.multiple_of(j * tk, tk), tk), :]
        vs = (jnp.where(lo, vt, one), jnp.where(lo, one, vt))
        new = []
        for hh in range(2):
            m, acc = carry[hh]
            s = s_pair[hh]
            if diag:
                s = jnp.where(causal, s, NEG)
            m_new = jnp.maximum(m, jnp.max(s, axis=1, keepdims=True))
            p = jnp.exp2(s - m_new)
            acc = jnp.exp2(m - m_new) * acc + _dot(p.astype(BF16), vs[hh])
            new.append((m_new, acc))
        return tuple(new)

    init = tuple((jnp.full((tq, 1), NEG, F32), jnp.zeros((tq, LANES), F32)) for _ in range(2))
    carry = lax.fori_loop(0, n_full, lambda j, c: update(j, scores(j), c, False), init)
    carry = update(n_full, scores(n_full), carry, True)
    acc = jnp.where(lo, carry[0][1], carry[1][1])
    den = jnp.where(lo, carry[1][1], carry[0][1])
    o_ref[0] = (acc / pltpu.roll(den, FOX_DH, 1)).astype(o_ref.dtype)


def fox_attention(q, k, v, feat, tq=512, tk=1024):
    b, t, w = q.shape
    npair = w // LANES
    nt = t // tq
    tk = min(tk, t)
    whole = lambda: pl.BlockSpec((1, t, LANES), lambda bi, p, i: (bi, 0, p))
    return pl.pallas_call(
        functools.partial(_fox_kernel, tq=tq, tk=tk),
        grid=(b, npair, nt),
        in_specs=[pl.BlockSpec((1, tq, LANES), lambda bi, p, i: (bi, i, p)), whole(), whole(), whole()],
        out_specs=pl.BlockSpec((1, tq, LANES), lambda bi, p, i: (bi, i, p)),
        out_shape=jax.ShapeDtypeStruct((b, t, w), BF16),
        compiler_params=_cparams(("arbitrary", "arbitrary", "arbitrary")),
        name="fox_attn",
    )(q, k, v, feat)


def _mm(a, b):
    return _dot(a.astype(BF16), b.astype(BF16))


def _mm3(a, b):
    ah = a.astype(BF16)
    bh = b.astype(BF16)
    al = (a - ah.astype(F32)).astype(BF16)
    bl = (b - bh.astype(F32)).astype(BF16)
    return _dot(jnp.concatenate([ah, ah, al], axis=1), jnp.concatenate([bh, bl, bh], axis=0))


def _tril_solve(a, rhs, ri, ci):
    n = a[0].shape[0]
    both = lambda f, x, y: [f(p, q) for p, q in zip(x, y)]
    eye = (ri == ci).astype(F32)
    same = lambda b: (lax.shift_right_logical(ri, int(math.log2(b)))
                      == lax.shift_right_logical(ci, int(math.log2(b))))
    base = 16
    d = [jnp.where(same(base), p, 0.0) for p in a]
    d2 = both(_mm3, d, d)
    d4 = both(_mm3, d2, d2)
    r1 = [eye - p + p2 - t for p, p2, t in zip(d, d2, both(_mm3, d, d2))]
    d8 = both(_mm3, d4, d4)
    r2 = [eye + p4 + p8 + t for p4, p8, t in zip(d4, d8, both(_mm3, d4, d8))]
    t = both(_mm3, r1, r2)
    b = base
    while b < n:
        join = same(2 * b) & jnp.logical_not(same(b))
        low = [jnp.where(join, p, 0.0) for p in a]
        t = [p - q for p, q in zip(t, both(_mm3, both(_mm3, t, low), t))]
        b *= 2
    return both(_mm3, t, rhs)


GDN_BLOCK = 128


def _gdn_kernel(x_ref, sm_ref, z_ref, cw_ref, ega_ref, egb_ref, alog_ref, dtb_ref, on_ref, tril_ref,
                o_ref, s_scr, prev_scr, *, tc):
    c = GDN_BLOCK
    w = GDN_W

    @pl.when(pl.program_id(1) == 0)
    def _():
        s_scr[...] = jnp.zeros_like(s_scr)
        prev_scr[...] = jnp.zeros_like(prev_scr)

    x = x_ref[0]
    prev = prev_scr[...]
    row8 = lax.broadcasted_iota(jnp.int32, (8, 1), 0)
    acc = x * cw_ref[GDN_CONV - 1:GDN_CONV, :]
    for s in range(1, GDN_CONV):
        rolled = pltpu.roll(x, s, 0)
        head = jnp.where(row8 < s, pltpu.roll(prev, s, 0), rolled[0:8])
        shifted = jnp.concatenate([head, rolled[8:]], axis=0)
        acc = acc + shifted * cw_ref[GDN_CONV - 1 - s:GDN_CONV - s, :]
    prev_scr[...] = x[tc - 8:tc]
    xc = _silu(acc)

    sm = sm_ref[0]
    g_raw = _dot_hi(sm, ega_ref[...])
    b_raw = _dot_hi(sm, egb_ref[...])
    g = -jnp.exp(alog_ref[...]) * _softplus(g_raw + dtb_ref[...])
    beta_all = _sigmoid(b_raw)
    gc_all = _dot_hi(tril_ref[...], g)

    ri = lax.broadcasted_iota(jnp.int32, (c, c), 0)
    ci = lax.broadcasted_iota(jnp.int32, (c, c), 1)
    causal = ci <= ri
    strict = ci < ri

    nblk = tc // c
    a_l, attn_l, rhs_l, qd_l, kd_l, egl_l = [], [], [], [], [], []
    for h in range(GDN_HEADS):
        ln = slice(h * GDN_DH, (h + 1) * GDN_DH)
        qh = xc[:, h * GDN_DH:(h + 1) * GDN_DH]
        kh = xc[:, w + h * GDN_DH:w + (h + 1) * GDN_DH]
        qh = qh * lax.rsqrt(jnp.sum(qh * qh, axis=-1, keepdims=True) + EPS) * (GDN_DH ** -0.5)
        kh = kh * lax.rsqrt(jnp.sum(kh * kh, axis=-1, keepdims=True) + EPS)
        vh = xc[:, 2 * w + h * GDN_DH:2 * w + (h + 1) * GDN_DH]
        gch = gc_all[:, ln]
        gct = gch.T
        egc = jnp.exp(gch)
        bh = beta_all[:, ln]
        for n in range(nblk):
            sl = slice(n * c, (n + 1) * c)
            q, k, v, gc, be = qh[sl], kh[sl], vh[sl], gch[sl], bh[sl]
            decay = jnp.exp(jnp.where(causal, gc - gct[:, sl], NEG))
            kb = k * be
            kk = _dot_nt(jnp.concatenate([kb, q], axis=0).astype(BF16), k.astype(BF16))
            a_l.append(jnp.where(strict, kk[:c] * decay, 0.0))
            attn_l.append(jnp.where(causal, kk[c:] * decay, 0.0))
            rhs_l.append(jnp.concatenate([v * be, kb * egc[sl]], axis=1))
            gl = gc[c - 1:c, :]
            qd_l.append(q * egc[sl])
            kd_l.append(k * jnp.exp(gl - gc))
            egl_l.append(jnp.exp(gl))
    uw_l = _tril_solve(a_l, rhs_l, ri, ci)

    states = [s_scr[h] for h in range(GDN_HEADS)]
    for n in range(nblk):
        sl = slice(n * c, (n + 1) * c)
        idx = [h * nblk + n for h in range(GDN_HEADS)]
        ws = [_mm(jnp.concatenate([uw_l[i][:, GDN_DH:], qd_l[i]], axis=0), states[h])
              for h, i in enumerate(idx)]
        v_new = [uw_l[i][:, :GDN_DH] - ws[h][:c] for h, i in enumerate(idx)]
        o = [ws[h][c:] + _mm(attn_l[i], v_new[h]) for h, i in enumerate(idx)]
        states = [states[h] * egl_l[i] + _dot_tn(kd_l[i].astype(BF16), v_new[h].astype(BF16))
                  for h, i in enumerate(idx)]
        for h in range(GDN_HEADS):
            ln = slice(h * GDN_DH, (h + 1) * GDN_DH)
            on = o[h] * lax.rsqrt(jnp.mean(o[h] * o[h], axis=-1, keepdims=True) + EPS) * on_ref[...]
            o_ref[0, sl, ln] = (on * _silu(z_ref[0, sl, ln])).astype(o_ref.dtype)
    for h in range(GDN_HEADS):
        s_scr[h] = states[h]


def gated_delta_net(x, small, z, conv_w, a_log, dt_bias, on_gain, tc=512):
    b, t, _ = x.shape
    w = GDN_W
    ega = np.zeros((LANES, w), np.float32)
    egb = np.zeros((LANES, w), np.float32)
    for h in range(GDN_HEADS):
        ega[FOX_HEADS + h, h * GDN_DH:(h + 1) * GDN_DH] = 1.0
        egb[FOX_HEADS + GDN_HEADS + h, h * GDN_DH:(h + 1) * GDN_DH] = 1.0
    alog = jnp.repeat(a_log, GDN_DH).reshape(1, w)
    dtb = jnp.repeat(dt_bias, GDN_DH).reshape(1, w)
    idx = np.arange(tc)
    tril = ((idx[:, None] >= idx[None, :]) & (idx[:, None] // GDN_BLOCK == idx[None, :] // GDN_BLOCK))
    row = lambda n: pl.BlockSpec((1, tc, n), lambda i, j: (i, j, 0))
    return pl.pallas_call(
        functools.partial(_gdn_kernel, tc=tc),
        grid=(b, t // tc),
        in_specs=[row(3 * w), row(LANES), row(w), _full((GDN_CONV, 3 * w)), _full((LANES, w)), _full((LANES, w)),
                  _full((1, w)), _full((1, w)), _full((1, GDN_DH)), _full((tc, tc))],
        out_specs=row(w),
        out_shape=jax.ShapeDtypeStruct((b, t, w), BF16),
        scratch_shapes=[pltpu.VMEM((GDN_HEADS, GDN_DH, GDN_DH), F32), pltpu.VMEM((8, 3 * w), F32)],
        compiler_params=_cparams(("arbitrary", "arbitrary")),
        name="gdn",
    )(x, small, z, conv_w, jnp.asarray(ega), jnp.asarray(egb), alog, dtb, on_gain.reshape(1, GDN_DH),
      jnp.asarray(tril.astype(np.float32)))


def even_mixer(h, w_in, fox_fb, fox_qn, fox_kn, gdn_conv, gdn_a_log, gdn_dt_bias, gdn_on):
    cuts = np.cumsum((0,) + EV_SIZES)
    col = lambda i: w_in[:, cuts[i]:cuts[i + 1]]
    wb = lambda a: a.astype(BF16)
    bd = _block_diag_ones(FOX_W, FOX_DH)
    ep = _head_norm_epilogue(FOX_DH)
    qg = (jnp.tile(fox_qn, FOX_HEADS) * (FOX_DH ** -0.5 * LOG2E)).reshape(1, FOX_W)
    kg = jnp.tile(fox_kn, FOX_HEADS).reshape(1, FOX_W)
    w_small = jnp.zeros((w_in.shape[0], LANES), F32)
    w_small = w_small.at[:, 0:8].set(col(3)).at[:, 8:12].set(col(5)).at[:, 12:16].set(col(6))
    fq, fk, fv, small, gqkv, gz = proj_multi(
        h, [(col(0), BF16, ep, (bd, qg)), (col(1), BF16, ep, (bd, kg)), (col(2), BF16, None, ()),
            (w_small, F32, None, ()), (col(4), F32, None, ()), (col(7), F32, None, ())], name="proj_even")
    feat = fox_decay(small, fox_fb)
    o_fox = fox_attention(fq, fk, fv, feat)
    o_gdn = gated_delta_net(gqkv, small, gz, gdn_conv, gdn_a_log, gdn_dt_bias, gdn_on)
    return o_fox, o_gdn


def _t5_bucket_np(dist):
    n = np.maximum(dist, 0)
    exact = REL_BUCKETS // 2
    nf = np.maximum(n, 1).astype(np.float32)
    large = exact + (np.log(nf / np.float32(exact)) / np.float32(math.log(REL_MAX_DIST / exact))
                     * np.float32(REL_BUCKETS - exact)).astype(np.int32)
    large = np.minimum(large, REL_BUCKETS - 1)
    return np.where(n < exact, n, large)


def _bias_kernel(tbl_ref, bucket_ref, o_ref):
    h = pl.program_id(0)
    bucket = bucket_ref[...]
    acc = jnp.full(bucket.shape, NEG, F32)
    for b in range(REL_BUCKETS):
        acc = jnp.where(bucket == b, tbl_ref[b, h], acc)
    o_ref[0] = acc


def _bias_table(rel_bias, dist, valid):
    shifted = (rel_bias - rel_bias[REL_BUCKETS - 1:REL_BUCKETS]) * LOG2E
    bucket = np.where(valid, _t5_bucket_np(dist), -1).astype(np.int32)
    rows, cols = int(np.prod(bucket.shape[:-1])), bucket.shape[-1]
    nh = rel_bias.shape[1]
    tb = pl.pallas_call(
        _bias_kernel,
        grid=(nh,),
        in_specs=[pl.BlockSpec(memory_space=pltpu.SMEM), _full((rows, cols))],
        out_specs=pl.BlockSpec((1, rows, cols), lambda h: (h, 0, 0)),
        out_shape=jax.ShapeDtypeStruct((nh, rows, cols), F32),
        compiler_params=_cparams(("arbitrary",)),
        name="t5_bias",
    )(shifted, jnp.asarray(bucket.reshape(rows, cols)))
    return tb.reshape((nh,) + bucket.shape)


def _cmp_kernel(r_ref, pos_ref, w1_ref, w2_ref, kn_ref, o_ref):
    m = r_ref.shape[3]
    half = r_ref.shape[4]
    r = r_ref[0, 0, 0].astype(BF16)
    a = _dot(r, w1_ref[0, :half, :])
    bm = _dot(r, w1_ref[0, half:, :])
    c = _dot(pos_ref[0].astype(BF16), w1_ref[0])
    hid = a + pltpu.roll(bm, m - 1, 0) + c[0:1, :]
    out = _dot(_silu(hid).astype(BF16), w2_ref[0])
    normed = out * lax.rsqrt(jnp.mean(out * out, axis=-1, keepdims=True) + EPS) * kn_ref[...]
    o_ref[0, 0, 0] = jnp.where(pl.program_id(0) == 0, normed, out).astype(o_ref.dtype)


def nsa_compress(kcvc, pos, w1, w2, kn):
    b, t, _ = kcvc.shape
    m = t // CMP_STRIDE
    half = CMP_STRIDE * NSA_DH
    r = kcvc.reshape(b, m, CMP_STRIDE, 2, NSA_KV_HEADS, NSA_DH).transpose(3, 0, 4, 1, 2, 5).reshape(2, b, 2, m, half)
    posf = jnp.zeros((2, 8, 2 * half), F32).at[:, 0].set(pos.reshape(2, 2 * half))
    w2d = jnp.concatenate([w2, w2], axis=-1).astype(BF16)
    knd = jnp.tile(kn, 2).reshape(1, LANES)
    return pl.pallas_call(
        _cmp_kernel,
        grid=(2, b, NSA_KV_HEADS),
        in_specs=[pl.BlockSpec((1, 1, 1, m, half), lambda s, i, k: (s, i, k, 0, 0)),
                  pl.BlockSpec((1, 8, 2 * half), lambda s, i, k: (s, 0, 0)),
                  pl.BlockSpec((1, 2 * half, CMP_HIDDEN), lambda s, i, k: (s, 0, 0)),
                  pl.BlockSpec((1, CMP_HIDDEN, LANES), lambda s, i, k: (s, 0, 0)),
                  _full((1, LANES))],
        out_specs=pl.BlockSpec((1, 1, 1, m, LANES), lambda s, i, k: (s, i, k, 0, 0)),
        out_shape=jax.ShapeDtypeStruct((2, b, NSA_KV_HEADS, m, LANES), BF16),
        compiler_params=_cparams(("arbitrary", "arbitrary", "arbitrary")),
        name="nsa_compress",
    )(r, posf, w1.astype(BF16), w2d, knd)


def _dot_split(a, b):
    hi = a.astype(BF16)
    lo = (a - hi.astype(F32)).astype(BF16)
    return _dot(hi, b) + _dot(lo, b)


def _head_q(q_ref, hh, lo):
    blk = q_ref[0, :, (hh // 2) * LANES:(hh // 2 + 1) * LANES]
    keep = lo if hh % 2 == 0 else jnp.logical_not(lo)
    return jnp.where(keep, blk, jnp.zeros_like(blk))


def _pair_heads(o, lo):
    return jnp.concatenate([jnp.where(lo, o[0], o[1]), jnp.where(lo, o[2], o[3])], axis=1)


def _nsa_sel_kernel(q_ref, kc_ref, vc_ref, ov_ref, bt_ref, o_ref, sel_ref, *, tq, nband, n_slc):
    i = pl.program_id(2)
    ncp = kc_ref.shape[3]
    nsp = sel_ref.shape[3]
    per = tq // CMP_STRIDE
    var = jnp.minimum(i, 1)
    bs = pl.multiple_of(per * jnp.maximum(i - 1, 0), per)
    lo = lax.broadcasted_iota(jnp.int32, (1, LANES), 1) < NSA_DH
    kc = kc_ref[0, 0, 0]
    vc = vc_ref[0, 0, 0]
    kcb = kc_ref[0, 0, 0, pl.ds(bs, nband), :]
    vcb = vc_ref[0, 0, 0, pl.ds(bs, nband), :]
    far_ok = lax.broadcasted_iota(jnp.int32, (1, ncp), 1) < per * (i - 1)
    hs = range(NSA_GROUP)
    qh = [_head_q(q_ref, hh, lo) for hh in hs]
    s_far = [jnp.where(far_ok, _dot_nt(q, kc), NEG) for q in qh]
    s_band = [_dot_nt(qh[hh], kcb) + bt_ref[var, hh] for hh in hs]
    m = [jnp.maximum(jnp.max(a, axis=1, keepdims=True), jnp.max(b, axis=1, keepdims=True))
         for a, b in zip(s_far, s_band)]
    m = [jnp.where(x < 0.5 * NEG, 0.0, x) for x in m]
    p_far = [jnp.exp2(a - x) for a, x in zip(s_far, m)]
    p_band = [jnp.exp2(b - x) for b, x in zip(s_band, m)]
    l = [jnp.sum(a, axis=1, keepdims=True) + jnp.sum(b, axis=1, keepdims=True) for a, b in zip(p_far, p_band)]
    inv = [1.0 / jnp.where(x == 0.0, 1.0, x) for x in l]
    outs = [(_dot(a.astype(BF16), vc) + _dot(b.astype(BF16), vcb)) * x for a, b, x in zip(p_far, p_band, inv)]
    ps_far = p_far[0] * inv[0]
    ps_band = p_band[0] * inv[0]
    for hh in range(1, NSA_GROUP):
        ps_far = ps_far + p_far[hh] * inv[hh]
        ps_band = ps_band + p_band[hh] * inv[hh]
    o_ref[0] = _pair_heads(outs, lo).astype(o_ref.dtype)

    imp = _dot_split(ps_far, ov_ref[...]) + _dot_split(ps_band, ov_ref[pl.ds(bs, nband), :])
    blk = lax.broadcasted_iota(jnp.int32, (1, nsp), 1)
    blk_f = blk.astype(F32)
    qpos = i * tq + lax.broadcasted_iota(jnp.int32, (tq, 1), 0)
    cur = lax.shift_right_logical(qpos, int(math.log2(SLC_LEN)))
    forced = (blk == 0) | (blk == cur) | (blk == cur - 1)
    work = jnp.where(forced, FORCE_SCORE, jnp.where(blk <= cur, imp, NEG))
    work = jnp.where(blk < n_slc, work, -jnp.inf)
    ngrp = 4
    rg = tq // ngrp
    works = [work[r * rg:(r + 1) * rg] for r in range(ngrp)]
    sels = [jnp.zeros((rg, nsp), F32) for _ in range(ngrp)]
    for _ in range(min(SLC_TOPK, n_slc)):
        ms = [jnp.max(w, axis=1, keepdims=True) for w in works]
        firsts = [jnp.min(jnp.where(w == m, blk_f, float(nsp)), axis=1, keepdims=True) for w, m in zip(works, ms)]
        picks = [blk_f == f for f in firsts]
        sels = [jnp.where(p, 1.0, s) for p, s in zip(picks, sels)]
        works = [jnp.where(p, -jnp.inf, w) for p, w in zip(picks, works)]
    sel_ref[0, 0] = jnp.concatenate(sels, axis=0).astype(sel_ref.dtype)


def nsa_select(q, cmp_kv, rel_bias, tq=512):
    b, t, _ = q.shape
    ncp = t // CMP_STRIDE
    n_cmp = ncp - 1
    n_slc = t // SLC_LEN
    nsp = max(LANES, n_slc)
    per = tq // CMP_STRIDE
    nband = 2 * per
    n = np.arange(ncp)[:, None]
    s = np.arange(nsp)[None, :]
    ov = ((CMP_STRIDE * n < SLC_LEN * s + SLC_LEN) & (CMP_STRIDE * n + CMP_LEN > SLC_LEN * s)
          & (n < n_cmp) & (s < n_slc)).astype(np.float32)
    qi = np.arange(tq)[:, None]
    nj = np.arange(nband)[None, :]
    end = CMP_STRIDE * nj + CMP_LEN - 1
    dist = np.stack([qi - end, tq + qi - end])
    bt = _bias_table(rel_bias, dist, dist >= 0)
    bt = bt.reshape(NSA_KV_HEADS, NSA_GROUP, 2, tq, nband).transpose(0, 2, 1, 3, 4)
    bt = bt.reshape(NSA_KV_HEADS * 2, NSA_GROUP, tq, nband)
    gw = NSA_GROUP * NSA_DH
    return pl.pallas_call(
        functools.partial(_nsa_sel_kernel, tq=tq, nband=nband, n_slc=n_slc),
        grid=(b, NSA_KV_HEADS, t // tq),
        in_specs=[pl.BlockSpec((1, tq, gw), lambda bi, k, i: (bi, i, k)),
                  pl.BlockSpec((1, 1, 1, ncp, LANES), lambda bi, k, i: (0, bi, k, 0, 0)),
                  pl.BlockSpec((1, 1, 1, ncp, LANES), lambda bi, k, i: (1, bi, k, 0, 0)),
                  _full((ncp, nsp)),
                  pl.BlockSpec((2, NSA_GROUP, tq, nband), lambda bi, k, i: (k, 0, 0, 0))],
        out_specs=[pl.BlockSpec((1, tq, gw), lambda bi, k, i: (bi, i, k)),
                   pl.BlockSpec((1, 1, tq, nsp), lambda bi, k, i: (bi, k, i, 0))],
        out_shape=[jax.ShapeDtypeStruct((b, t, NSA_W), BF16),
                   jax.ShapeDtypeStruct((b, NSA_KV_HEADS, t, nsp), BF16)],
        compiler_params=_cparams(("arbitrary", "arbitrary", "arbitrary")),
        name="nsa_select",
    )(q, cmp_kv, cmp_kv, jnp.asarray(ov, dtype=BF16), bt)


def _nsa_main_kernel(q_ref, ks_ref, vs_ref, kw0_ref, kw1_ref, kw2_ref, vw0_ref, vw1_ref, vw2_ref, sel_ref,
                     ocmp_ref, gate_ref, tb_ref, wm_ref, eg_ref, o_ref, *, tq):
    i = pl.program_id(2)
    nsp = sel_ref.shape[3]
    g = NSA_GROUP
    lo = lax.broadcasted_iota(jnp.int32, (1, LANES), 1) < NSA_DH
    qst = jnp.concatenate([_head_q(q_ref, hh, lo) for hh in range(g)], axis=0)
    sel = sel_ref[0, 0]
    blk_row = lax.broadcasted_iota(jnp.int32, (nsp, 1), 0)
    causal = (lax.broadcasted_iota(jnp.int32, (tq, tq), 1) <= lax.broadcasted_iota(jnp.int32, (tq, tq), 0))
    one = jnp.ones((1, LANES), BF16)

    def sel_step(jt, carry, near, tw):
        m, acc = carry
        start = pl.multiple_of(jnp.maximum(jt, 0) * tw, tw)
        kt = ks_ref[0, pl.ds(start, tw), :]
        vt = jnp.where(lo, vs_ref[0, pl.ds(start, tw), :], one)
        col_blk = lax.shift_right_logical(lax.broadcasted_iota(jnp.int32, (1, tw), 1), int(math.log2(SLC_LEN)))
        expand = jnp.where(blk_row - (tw // SLC_LEN) * jt == col_blk, 1.0, 0.0).astype(BF16)
        keep = _dot(sel, expand)
        madd = (keep - 1.0) * (-NEG)
        if near == 2:
            madd = jnp.where(causal, madd, NEG)
        s = _dot_nt(qst, kt).reshape(g, tq, tw) + madd[None]
        if near is not None:
            s = s + tb_ref[:, :, near * tq:(near + 1) * tq]
        s = s.reshape(g * tq, tw)
        m_new = jnp.maximum(m, jnp.max(s, axis=1, keepdims=True))
        p = jnp.exp2(s - m_new)
        acc = jnp.exp2(m - m_new) * acc + _dot(p.astype(BF16), vt)
        return m_new, acc

    n_far = jnp.maximum(i - 2, 0)
    carry = (jnp.full((g * tq, 1), NEG, F32), jnp.zeros((g * tq, LANES), F32))
    carry = lax.fori_loop(0, lax.shift_right_logical(n_far, 1), lambda j, c: sel_step(j, c, None, 2 * tq), carry)
    carry = lax.cond((n_far & 1) == 1, lambda c: sel_step(i - 3, c, None, tq), lambda c: c, carry)
    for near in range(3):
        carry = sel_step(i - 2 + near, carry, near, tq)
    o_slc = (carry[1] / pltpu.roll(carry[1], NSA_DH, 1)).reshape(g, tq, LANES)

    var = jnp.minimum(i, 2)
    kws = (kw0_ref, kw1_ref, kw2_ref)
    vws = (vw0_ref, vw1_ref, vw2_ref)
    sw = []
    for near in range(3):
        s = _dot_nt(qst, kws[near][0]).reshape(g, tq, tq)
        s = s + tb_ref[:, :, near * tq:(near + 1) * tq] + wm_ref[var, :, near * tq:(near + 1) * tq][None]
        sw.append(s.reshape(g * tq, tq))
    m = jnp.maximum(jnp.maximum(jnp.max(sw[0], axis=1, keepdims=True), jnp.max(sw[1], axis=1, keepdims=True)),
                    jnp.max(sw[2], axis=1, keepdims=True))
    acc = jnp.zeros((g * tq, LANES), F32)
    for near in range(3):
        p = jnp.exp2(sw[near] - m)
        acc = acc + _dot(p.astype(BF16), jnp.where(lo, vws[near][0], one))
    o_win = (acc / pltpu.roll(acc, NSA_DH, 1)).reshape(g, tq, LANES)

    pair = lambda o: jnp.concatenate([jnp.where(lo, o[0], pltpu.roll(o[1], NSA_DH, 1)),
                                      jnp.where(lo, o[2], pltpu.roll(o[3], NSA_DH, 1))], axis=1)
    gates = _dot_hi(_sigmoid(gate_ref[0]), eg_ref[0])
    gw = g * NSA_DH
    out = (gates[:, 0:gw] * ocmp_ref[0].astype(F32)
           + gates[:, gw:2 * gw] * pair(o_slc) + gates[:, 2 * gw:3 * gw] * pair(o_win))
    o_ref[0] = out.astype(o_ref.dtype)


def nsa_main(q, ksw, vsw, sel, o_cmp, small, rel_bias, tq=256):
    b, t, _ = q.shape
    nsp = sel.shape[-1]
    g = NSA_GROUP
    gw = g * NSA_DH
    qi = np.arange(tq)[:, None]
    c = np.arange(3 * tq)[None, :]
    dist = qi + 2 * tq - c
    tb = _bias_table(rel_bias, dist, np.ones_like(dist, bool))
    wm = np.zeros((3, tq, 3 * tq), np.float32)
    for var in range(3):
        exists = c >= tq * (2 - var)
        wm[var] = np.where((dist >= 0) & (dist < WINDOW) & exists, 0.0, NEG)
    eg = np.zeros((NSA_KV_HEADS, LANES, 3 * gw), np.float32)
    for k in range(NSA_KV_HEADS):
        for hh in range(g):
            for br in range(3):
                eg[k, (k * g + hh) * 3 + br, br * gw + hh * NSA_DH:br * gw + (hh + 1) * NSA_DH] = 1.0
    near = lambda off, col: pl.BlockSpec(
        (1, tq, LANES), lambda bi, k, i: (bi, jnp.maximum(i - off, 0), col + k))
    return pl.pallas_call(
        functools.partial(_nsa_main_kernel, tq=tq),
        grid=(b, NSA_KV_HEADS, t // tq),
        in_specs=[pl.BlockSpec((1, tq, gw), lambda bi, k, i: (bi, i, k)),
                  pl.BlockSpec((1, t, LANES), lambda bi, k, i: (bi, 0, k)),
                  pl.BlockSpec((1, t, LANES), lambda bi, k, i: (bi, 0, k)),
                  near(2, 2), near(1, 2), near(0, 2), near(2, 2), near(1, 2), near(0, 2),
                  pl.BlockSpec((1, 1, tq, nsp), lambda bi, k, i: (bi, k, i, 0)),
                  pl.BlockSpec((1, tq, gw), lambda bi, k, i: (bi, i, k)),
                  pl.BlockSpec((1, tq, LANES), lambda bi, k, i: (bi, i, 0)),
                  pl.BlockSpec((g, tq, 3 * tq), lambda bi, k, i: (k, 0, 0)),
                  _full((3, tq, 3 * tq)),
                  pl.BlockSpec((1, LANES, 3 * gw), lambda bi, k, i: (k, 0, 0))],
        out_specs=pl.BlockSpec((1, tq, gw), lambda bi, k, i: (bi, i, k)),
        out_shape=jax.ShapeDtypeStruct((b, t, NSA_W), BF16),
        compiler_params=_cparams(("arbitrary", "arbitrary", "arbitrary")),
        name="nsa_main",
    )(q, ksw, vsw, ksw, ksw, ksw, vsw, vsw, vsw, sel, o_cmp, small, tb, jnp.asarray(wm), jnp.asarray(eg))


def _gla_kernel(qk_ref, v_ref, r_ref, sm_ref, wg_ref, bg_ref, on_ref, tril_ref, o_ref, s_scr, *, tc):
    c = GLA_CHUNK

    @pl.when(pl.program_id(1) == 0)
    def _():
        s_scr[...] = jnp.zeros_like(s_scr)

    kw = GLA_KW
    log_a = _log_sigmoid(_dot_hi(sm_ref[0], wg_ref[...]) + bg_ref[...]) * (1.0 / GLA_TAU)
    gcum = _dot_hi(tril_ref[...], log_a)
    q = qk_ref[0, :, 0:kw] * (GLA_DK ** -0.5)
    k = qk_ref[0, :, kw:2 * kw]
    q_dec = (q * jnp.exp(gcum)).astype(BF16)
    k_inv = (k * jnp.exp(-gcum)).astype(BF16)
    ri = lax.broadcasted_iota(jnp.int32, (c, c), 0)
    ci = lax.broadcasted_iota(jnp.int32, (c, c), 1)
    causal = ci <= ri
    lo = lax.broadcasted_iota(jnp.int32, (1, LANES), 1) < GLA_DK
    nchunk = tc // c
    heads = range(GLA_HEADS)
    zero = jnp.zeros((c, LANES), BF16)

    qm, vb, kd, egl = {}, {}, {}, {}
    for n in range(nchunk):
        sl = slice(n * c, (n + 1) * c)
        gl = gcum[n * c + c - 1:n * c + c, :]
        kdn = (k[sl] * jnp.exp(gl - gcum[sl])).astype(BF16)
        for h in heads:
            pr = slice((h // 2) * LANES, (h // 2 + 1) * LANES)
            keep = lo if h % 2 == 0 else jnp.logical_not(lo)
            qm[n, h] = jnp.where(keep, q_dec[sl, pr], zero)
            vb[n, h] = v_ref[0, sl, h * GLA_DV:(h + 1) * GLA_DV].astype(BF16)
            kd[n, h] = kdn[:, pr]
            egl[n, h] = jnp.exp(gl[:, pr])
    idx = [(n, h) for n in range(nchunk) for h in heads]
    attn = {i: jnp.where(causal, _dot_nt(qm[i], k_inv[i[0] * c:(i[0] + 1) * c, (i[1] // 2) * LANES:
                                                       (i[1] // 2 + 1) * LANES]), 0.0).astype(BF16) for i in idx}
    o_intra = {i: _dot(attn[i], vb[i]) for i in idx}
    kv = {i: _dot_tn(vb[i], kd[i]) for i in idx}

    st = [s_scr[h] for h in heads]
    for n in range(nchunk):
        sl = slice(n * c, (n + 1) * c)
        o = [o_intra[n, h] + _dot_nt(qm[n, h], st[h].astype(BF16)) for h in heads]
        st = [st[h] * egl[n, h] + kv[n, h] for h in heads]
        for h in heads:
            on = o[h] * lax.rsqrt(jnp.mean(o[h] * o[h], axis=-1, keepdims=True) + EPS) * on_ref[...]
            o_ref[0, sl, h * GLA_DV:(h + 1) * GLA_DV] = (
                on * _silu(r_ref[0, sl, h * GLA_DV:(h + 1) * GLA_DV])).astype(o_ref.dtype)
    for h in heads:
        s_scr[h] = st[h]


def gated_linear_attention(qkvr, small, wg_up, bg, on_gain, tc=512):
    b, t, _ = qkvr.shape
    wg = jnp.zeros((LANES, GLA_KW), F32).at[3 * NSA_HEADS:3 * NSA_HEADS + GLA_GATE_RANK].set(wg_up)
    idx = np.arange(tc)
    tril = ((idx[:, None] >= idx[None, :]) & (idx[:, None] // GLA_CHUNK == idx[None, :] // GLA_CHUNK))
    return pl.pallas_call(
        functools.partial(_gla_kernel, tc=tc),
        grid=(b, t // tc),
        in_specs=[pl.BlockSpec((1, tc, 2 * GLA_KW), lambda i, j: (i, j, 0)),
                  pl.BlockSpec((1, tc, GLA_W), lambda i, j: (i, j, 1)),
                  pl.BlockSpec((1, tc, GLA_W), lambda i, j: (i, j, 2)),
                  pl.BlockSpec((1, tc, LANES), lambda i, j: (i, j, 0)),
                  _full((LANES, GLA_KW)), _full((1, GLA_KW)), _full((1, GLA_DV)), _full((tc, tc))],
        out_specs=pl.BlockSpec((1, tc, GLA_W), lambda i, j: (i, j, 0)),
        out_shape=jax.ShapeDtypeStruct((b, t, GLA_W), BF16),
        scratch_shapes=[pltpu.VMEM((GLA_HEADS, GLA_DV, LANES), F32)],
        compiler_params=_cparams(("arbitrary", "arbitrary")),
        name="gla",
    )(qkvr, qkvr, qkvr, small, wg, bg.reshape(1, GLA_KW), on_gain.reshape(1, GLA_DV),
      jnp.asarray(tril.astype(np.float32)))


def odd_mixer(h, w_in, nsa_qn, nsa_kn, nsa_pos, nsa_cmp_w1, nsa_cmp_w2, gla_wg_up, gla_bg, gla_on, rel_bias):
    cuts = np.cumsum((0,) + OD_SIZES)
    col = lambda i: w_in[:, cuts[i]:cuts[i + 1]]
    wb = lambda a: a.astype(BF16)
    dup = lambda a: jnp.concatenate([a[:, :NSA_DH], a[:, :NSA_DH], a[:, NSA_DH:], a[:, NSA_DH:]], axis=1)
    ep = _head_norm_epilogue(NSA_DH)
    bd = _block_diag_ones(NSA_W, NSA_DH)
    qg = (jnp.tile(nsa_qn, NSA_HEADS) * (NSA_DH ** -0.5 * LOG2E)).reshape(1, NSA_W)
    kg = jnp.tile(nsa_kn, NSA_HEADS).reshape(1, NSA_W)
    w_small = jnp.zeros((w_in.shape[0], LANES), F32)
    w_small = w_small.at[:, 0:24].set(col(7)).at[:, 24:40].set(col(11))
    nq, kcvc, ksw, vsw, small, qkvr = proj_multi(
        h, [(col(0), BF16, ep, (bd, qg)),
            (jnp.concatenate([col(1), col(2)], axis=1), F32, None, ()),
            (jnp.concatenate([dup(col(3)), dup(col(5))], axis=1), BF16, ep, (bd, kg)),
            (jnp.concatenate([dup(col(4)), dup(col(6))], axis=1), BF16, None, ()),
            (w_small, F32, None, ()),
            (jnp.concatenate([col(8), col(9), col(10), col(12)], axis=1), F32, None, ())], name="proj_odd")
    cmp_kv = nsa_compress(kcvc, nsa_pos, nsa_cmp_w1, nsa_cmp_w2, nsa_kn)
    o_cmp, sel = nsa_select(nq, cmp_kv, rel_bias)
    o_nsa = nsa_main(nq, ksw, vsw, sel, o_cmp, small, rel_bias)
    o_gla = gated_linear_attention(qkvr, small, gla_wg_up, gla_bg, gla_on)
    return o_nsa, o_gla


MOE_TM = 256
MOE_ROWS = 512


def _first_index(mask_val, idx, big, axis):
    return jnp.min(jnp.where(mask_val, idx, big), axis=axis, keepdims=True)


def _route_kernel(h_ref, rt_ref, b_ref, up_ref, eid_ref, rank_ref, w_ref, cnt_ref, run):
    tm = h_ref.shape[0]
    ne = N_EXPERTS
    gsz = ne // N_GROUPS

    @pl.when(pl.program_id(0) == 0)
    def _():
        run[...] = jnp.zeros_like(run)

    scores = _sigmoid(_dot_nt(rt_ref[...], h_ref[...], HI))
    biased = scores + b_ref[...]
    b3 = biased.reshape(N_GROUPS, gsz, tm)
    i3 = lax.broadcasted_iota(jnp.int32, (1, gsz, 1), 1).astype(F32)
    m1 = jnp.max(b3, axis=1, keepdims=True)
    f1 = _first_index(b3 == m1, i3, float(gsz), 1)
    m2 = jnp.max(jnp.where(i3 == f1, -jnp.inf, b3), axis=1, keepdims=True)
    gs = (m1 + m2).reshape(N_GROUPS, tm)
    gidx = lax.broadcasted_iota(jnp.int32, (N_GROUPS, 1), 0).astype(F32)
    gmask = jnp.zeros((N_GROUPS, tm), F32)
    for _ in range(TOPK_GROUPS):
        m = jnp.max(gs, axis=0, keepdims=True)
        pick = gidx == _first_index(gs == m, gidx, float(N_GROUPS), 0)
        gmask = jnp.where(pick, 1.0, gmask)
        gs = jnp.where(pick, -jnp.inf, gs)
    emask = jnp.broadcast_to(gmask.reshape(N_GROUPS, 1, tm), (N_GROUPS, gsz, tm)).reshape(ne, tm)
    work = jnp.where(emask > 0.5, biased, -jnp.inf)
    eidx = lax.broadcasted_iota(jnp.int32, (ne, 1), 0).astype(F32)
    picks, eids, ws = [], [], []
    for _ in range(TOP_K):
        m = jnp.max(work, axis=0, keepdims=True)
        first = _first_index(work == m, eidx, float(ne), 0)
        pick = eidx == first
        picks.append(pick)
        eids.append(first)
        ws.append(jnp.sum(jnp.where(pick, scores, 0.0), axis=0, keepdims=True))
        work = jnp.where(pick, -jnp.inf, work)
    wsum = ws[0]
    for k in range(1, TOP_K):
        wsum = wsum + ws[k]
    chosen = jnp.zeros((ne, tm), F32)
    for pick in picks:
        chosen = jnp.where(pick, 1.0, chosen)
    pos = run[...] + _dot(chosen.astype(BF16), up_ref[...])
    run[...] = run[...] + jnp.sum(chosen, axis=1, keepdims=True)
    cnt_ref[...] = run[...]
    row = lax.broadcasted_iota(jnp.int32, (8, 1), 0)
    eid_o = jnp.zeros((8, tm), F32)
    rank_o = jnp.zeros((8, tm), F32)
    w_o = jnp.zeros((LANES, tm), F32)
    rowl = lax.broadcasted_iota(jnp.int32, (LANES, 1), 0)
    for k in range(TOP_K):
        rk = jnp.sum(jnp.where(picks[k], pos, 0.0), axis=0, keepdims=True)
        eid_o = jnp.where(row == k, eids[k], eid_o)
        rank_o = jnp.where(row == k, rk, rank_o)
        w_o = jnp.where(rowl == k, ws[k] / wsum * ROUTE_SCALE, w_o)
    eid_ref[0] = eid_o.astype(jnp.int32)
    rank_ref[0] = rank_o.astype(jnp.int32)
    w_ref[...] = w_o.T


def moe_route(h2, router, e_bias, tm=MOE_TM):
    nt, d = h2.shape
    ne = N_EXPERTS
    up = jnp.asarray(np.triu(np.ones((tm, tm), np.float32), 1), dtype=BF16)
    nb = nt // tm
    return pl.pallas_call(
        _route_kernel,
        grid=(nb,),
        in_specs=[pl.BlockSpec((tm, d), lambda i: (i, 0)), _full((ne, d)), _full((ne, 1)), _full((tm, tm))],
        out_specs=[pl.BlockSpec((1, 8, tm), lambda i: (i, 0, 0)),
                   pl.BlockSpec((1, 8, tm), lambda i: (i, 0, 0)),
                   pl.BlockSpec((tm, LANES), lambda i: (i, 0)),
                   _full((ne, 1))],
        out_shape=[jax.ShapeDtypeStruct((nb, 8, tm), jnp.int32), jax.ShapeDtypeStruct((nb, 8, tm), jnp.int32),
                   jax.ShapeDtypeStruct((nt, LANES), F32), jax.ShapeDtypeStruct((ne, 1), F32)],
        scratch_shapes=[pltpu.VMEM((ne, 1), F32)],
        compiler_params=_cparams(("arbitrary",)),
        name="moe_route",
    )(h2, router.T, e_bias.reshape(ne, 1), up)


def _dispatch_kernel(dest_ref, h_ref, xs_ref, sem):
    tm = h_ref.shape[0]

    def copy(t, row):
        return pltpu.make_async_copy(h_ref.at[pl.ds(t, 1), :], xs_ref.at[pl.ds(row, 1), :], sem)

    def issue(t, _):
        for k in range(TOP_K):
            copy(t, dest_ref[0, k, t]).start()
        return 0

    def drain(t, _):
        for k in range(TOP_K):
            copy(0, 0).wait()
        return 0

    lax.fori_loop(0, tm, issue, 0, unroll=4)
    lax.fori_loop(0, tm, drain, 0, unroll=4)


def moe_dispatch(h2, dest, tm=MOE_TM):
    nt, d = h2.shape
    return pl.pallas_call(
        _dispatch_kernel,
        grid=(nt // tm,),
        in_specs=[pl.BlockSpec((1, 8, tm), lambda i: (i, 0, 0), memory_space=pltpu.SMEM),
                  pl.BlockSpec((tm, d), lambda i: (i, 0))],
        out_specs=pl.BlockSpec(memory_space=pl.ANY),
        scratch_shapes=[pltpu.SemaphoreType.DMA(())],
        out_shape=jax.ShapeDtypeStruct((nt * TOP_K, d), F32),
        compiler_params=_cparams(("arbitrary",)),
        name="moe_dispatch",
    )(dest, h2)


def _ffn_kernel(blk_ref, exp_ref, lo_ref, hi_ref, first_ref, valid_ref, x_ref, wg_ref, wu_ref, wd_ref, o_ref,
                wg_b, wu_b, wd_b):
    i = pl.program_id(0)
    rows = x_ref.shape[0]

    @pl.when((i == 0) | (exp_ref[i] != exp_ref[jnp.maximum(i - 1, 0)]))
    def _():
        wg_b[...] = wg_ref[0].astype(BF16)
        wu_b[...] = wu_ref[0].astype(BF16)
        wd_b[...] = wd_ref[0].astype(BF16)

    @pl.when(valid_ref[i] == 1)
    def _():
        x = x_ref[...].astype(BF16)
        a = _dot(x, wg_b[...])
        u = _dot(x, wu_b[...])
        y = _dot((_silu(a) * u).astype(BF16), wd_b[...])
        r = blk_ref[i] * rows + lax.broadcasted_iota(jnp.int32, (rows, 1), 0)
        y = jnp.where((r >= lo_ref[i]) & (r < hi_ref[i]), y, 0.0)

        @pl.when(first_ref[i] == 1)
        def _():
            o_ref[...] = y

        @pl.when(first_ref[i] == 0)
        def _():
            o_ref[...] = o_ref[...] + y


def _items_kernel(cnt_ref, starts_ref, blk_ref, exp_ref, lo_ref, hi_ref, first_ref, valid_ref, *, rows, n_items):
    shift = int(math.log2(rows))

    def expert(e, carry):
        start, n = carry
        c = cnt_ref[e]
        starts_ref[e] = start
        end = start + c
        first_blk = lax.shift_right_logical(start, shift)
        n_blk = jnp.where(c > 0, lax.shift_right_logical(jnp.maximum(end - 1, 0), shift) - first_blk + 1, 0)

        def item(k, n):
            b = first_blk + k
            lo = jnp.maximum(start, b * rows)
            blk_ref[n] = b
            exp_ref[n] = e
            lo_ref[n] = lo
            hi_ref[n] = jnp.minimum(end, (b + 1) * rows)
            first_ref[n] = (lo == b * rows).astype(jnp.int32)
            valid_ref[n] = 1
            return n + 1

        return end, lax.fori_loop(0, n_blk, item, n)

    _, total = lax.fori_loop(0, N_EXPERTS, expert, (jnp.int32(0), jnp.int32(0)))
    last = jnp.maximum(total - 1, 0)

    def fill(k, _):
        blk_ref[k] = blk_ref[last]
        exp_ref[k] = exp_ref[last]
        lo_ref[k] = 0
        hi_ref[k] = 0
        first_ref[k] = 0
        valid_ref[k] = 0
        return 0

    lax.fori_loop(total, n_items, fill, 0)


def _ffn_items(counts, n_rows, rows):
    n_items = n_rows // rows + N_EXPERTS - 1
    smem = pl.BlockSpec(memory_space=pltpu.SMEM)
    out = pl.pallas_call(
        functools.partial(_items_kernel, rows=rows, n_items=n_items),
        in_specs=[smem],
        out_specs=[smem] * 7,
        out_shape=[jax.ShapeDtypeStruct((N_EXPERTS,), jnp.int32)]
                  + [jax.ShapeDtypeStruct((n_items,), jnp.int32)] * 6,
        name="moe_items",
    )(counts)
    return out[0], tuple(out[1:])


def moe_ffn_sorted(xs, items, wg, wu, wd, layer, rows=MOE_ROWS):
    n_rows, d = xs.shape
    n_items = items[0].shape[0]
    de = wg.shape[-1]
    return pl.pallas_call(
        _ffn_kernel,
        grid_spec=pltpu.PrefetchScalarGridSpec(
            num_scalar_prefetch=6,
            grid=(n_items,),
            in_specs=[pl.BlockSpec((rows, d), lambda i, blk, e, *_: (blk[i], 0)),
                      pl.BlockSpec((None, 1, d, de), lambda i, blk, e, *_: (layer, e[i], 0, 0)),
                      pl.BlockSpec((None, 1, d, de), lambda i, blk, e, *_: (layer, e[i], 0, 0)),
                      pl.BlockSpec((None, 1, de, d), lambda i, blk, e, *_: (layer, e[i], 0, 0))],
            out_specs=pl.BlockSpec((rows, d), lambda i, blk, e, *_: (blk[i], 0)),
            scratch_shapes=[pltpu.VMEM((d, de), BF16), pltpu.VMEM((d, de), BF16), pltpu.VMEM((de, d), BF16)]),
        out_shape=jax.ShapeDtypeStruct((n_rows, d), F32),
        compiler_params=_cparams(("arbitrary",)),
        name="moe_ffn",
    )(*items, xs, wg, wu, wd)


def _combine_kernel(dest_ref, ys_ref, w_ref, h_ref, x_ref, g_ref, sg_ref, su_ref, sd_ref, o_ref, buf, sem):
    tm = h_ref.shape[0]

    def copy(t, k, row):
        return pltpu.make_async_copy(ys_ref.at[pl.ds(row, 1), :], buf.at[k, pl.ds(t, 1), :], sem)

    def issue(t, _):
        for k in range(TOP_K):
            copy(t, k, dest_ref[0, k, t]).start()
        return 0

    def drain(t, _):
        for k in range(TOP_K):
            copy(0, 0, 0).wait()
        return 0

    lax.fori_loop(0, tm, issue, 0, unroll=4)
    hb = h_ref[...].astype(BF16)
    y = _dot((_silu(_dot(hb, sg_ref[...])) * _dot(hb, su_ref[...])).astype(BF16), sd_ref[...])
    lax.fori_loop(0, tm, drain, 0, unroll=4)
    w = w_ref[...]
    for k in range(TOP_K):
        y = y + w[:, k:k + 1] * buf[k]
    o_ref[...] = x_ref[...] + g_ref[0] * y


def moe_combine(ys, dest, w, h2, x2, gate, sg, su, sd, seq, tm=MOE_TM):
    nt, d = h2.shape
    ds_ = sg.shape[-1]
    per_b = seq // tm
    tile = lambda: pl.BlockSpec((tm, d), lambda i: (i, 0))
    return pl.pallas_call(
        _combine_kernel,
        grid=(nt // tm,),
        in_specs=[pl.BlockSpec((1, 8, tm), lambda i: (i, 0, 0), memory_space=pltpu.SMEM),
                  pl.BlockSpec(memory_space=pl.ANY),
                  pl.BlockSpec((tm, LANES), lambda i: (i, 0)), tile(), tile(),
                  pl.BlockSpec((1, 1, d), lambda i: (i // per_b, 0, 0)),
                  _full((d, ds_)), _full((d, ds_)), _full((ds_, d))],
        out_specs=tile(),
        scratch_shapes=[pltpu.VMEM((TOP_K, tm, d), F32), pltpu.SemaphoreType.DMA(())],
        out_shape=jax.ShapeDtypeStruct((nt, d), F32),
        compiler_params=_cparams(("arbitrary",)),
        name="moe_combine",
    )(dest, ys, w, h2, x2, gate, sg.astype(BF16), su.astype(BF16), sd.astype(BF16))


def moe_layer(x, g_norm, sc, sh, gate, router, e_bias, wg, wu, wd, layer, sg, su, sd):
    b, t, d = x.shape
    nt = b * t
    h = ln_mod(x, g_norm, sc, sh, F32)
    h2 = h.reshape(nt, d)
    eid, rank, w, counts = moe_route(h2, router, e_bias)
    starts, items = _ffn_items(counts.reshape(-1).astype(jnp.int32), nt * TOP_K, MOE_ROWS)
    hit = eid[..., None] == jnp.arange(N_EXPERTS, dtype=jnp.int32)
    dest = jnp.sum(jnp.where(hit, starts.astype(jnp.int32), 0), axis=-1) + rank
    xs = moe_dispatch(h2, dest)
    ys = moe_ffn_sorted(xs, items, wg, wu, wd, layer)
    out = moe_combine(ys, dest, w, h2, x.reshape(nt, d), gate.reshape(b, 1, d), sg, su, sd, t)
    return out.reshape(b, t, d)


def kernel(x, c, ada_w, ada_b, norm_mix, norm_ffn, rel_bias, ev_w_in, ev_w_out, fox_fb, fox_qn, fox_kn, gdn_conv, gdn_a_log, gdn_dt_bias, gdn_on, od_w_in, od_w_out, nsa_qn, nsa_kn, nsa_pos, nsa_cmp_w1, nsa_cmp_w2, gla_wg_up, gla_bg, gla_on, moe_router, moe_bias, moe_wg, moe_wu, moe_wd, sh_wg, sh_wu, sh_wd):
    d = x.shape[-1]
    depth = ada_w.shape[0]
    mod = adaln(c, ada_w, ada_b)
    for layer in range(depth):
        sh1, sc1, g1, sh2, sc2, g2 = [mod[layer, :, i * d:(i + 1) * d] for i in range(6)]
        h = ln_mod(x, norm_mix[layer], sc1, sh1, BF16)
        j = layer // 2
        if layer % 2 == 0:
            y1, y2 = even_mixer(h, ev_w_in[j], fox_fb[j], fox_qn[j], fox_kn[j], gdn_conv[j], gdn_a_log[j],
                                gdn_dt_bias[j], gdn_on[j])
            w_out = ev_w_out[j]
        else:
            y1, y2 = odd_mixer(h, od_w_in[j], nsa_qn[j], nsa_kn[j], nsa_pos[j], nsa_cmp_w1[j], nsa_cmp_w2[j],
                               gla_wg_up[j], gla_bg[j], gla_on[j], rel_bias)
            w_out = od_w_out[j]
        x = out_proj(y1, y2, w_out, x, g1)
        x = moe_layer(x, norm_ffn[layer], sc2, sh2, g2, moe_router[layer], moe_bias[layer], moe_wg, moe_wu, moe_wd,
                      layer, sh_wg[layer], sh_wu[layer], sh_wd[layer])
    return x
```

```python
import functools
import math

import numpy as np
import jax
import jax.numpy as jnp
from jax import lax
from jax.experimental import pallas as pl
from jax.experimental.pallas import tpu as pltpu

F32 = jnp.float32
BF16 = jnp.bfloat16
HI = lax.Precision.HIGHEST

EPS = 1e-6
LOG2E = math.log2(math.e)
NEG = -1e30
FORCE_SCORE = 1e9

FOX_HEADS, FOX_DH = 8, 64
GDN_HEADS, GDN_DH, GDN_CONV, GDN_CHUNK = 4, 128, 4, 64
NSA_HEADS, NSA_KV_HEADS, NSA_DH = 8, 2, 64
NSA_GROUP = NSA_HEADS // NSA_KV_HEADS
CMP_LEN, CMP_STRIDE, CMP_HIDDEN = 32, 16, 256
SLC_LEN, SLC_TOPK, WINDOW = 64, 16, 512
GLA_HEADS, GLA_DK, GLA_DV, GLA_GATE_RANK, GLA_TAU, GLA_CHUNK = 4, 64, 128, 16, 16.0, 64
REL_BUCKETS, REL_MAX_DIST = 32, 128
N_EXPERTS, TOP_K, D_EXPERT, D_SHARED = 64, 6, 256, 256
N_GROUPS, TOPK_GROUPS, ROUTE_SCALE = 8, 4, 2.5

FOX_W = FOX_HEADS * FOX_DH
GDN_W = GDN_HEADS * GDN_DH
NSA_W = NSA_HEADS * NSA_DH
NSA_KV_W = NSA_KV_HEADS * NSA_DH
GLA_KW = GLA_HEADS * GLA_DK
GLA_W = GLA_HEADS * GLA_DV
EV_SIZES = (FOX_W, FOX_W, FOX_W, FOX_HEADS, 3 * GDN_W, GDN_HEADS, GDN_HEADS, GDN_W)
OD_SIZES = (NSA_W,) + (NSA_KV_W,) * 6 + (3 * NSA_HEADS, GLA_KW, GLA_KW, GLA_W, GLA_GATE_RANK, GLA_W)

LANES = 128
ROW_GROUP = 64
VMEM_LIMIT = 56 * 1024 * 1024


def _cparams(sem, flags=None):
    return pltpu.CompilerParams(dimension_semantics=sem, vmem_limit_bytes=VMEM_LIMIT, flags=flags)


def _full(shape):
    n = len(shape)
    return pl.BlockSpec(shape, lambda *_: (0,) * n)


def _dot(a, b):
    return jnp.dot(a, b, preferred_element_type=F32)


def _dot_hi(a, b):
    return jnp.dot(a, b, precision=HI, preferred_element_type=F32)


def _dot_nt(a, b, precision=None):
    return lax.dot_general(a, b, (((1,), (1,)), ((), ())), precision=precision, preferred_element_type=F32)


def _dot_tn(a, b, precision=None):
    return lax.dot_general(a, b, (((0,), (0,)), ((), ())), precision=precision, preferred_element_type=F32)


def _sigmoid(x):
    return 1.0 / (1.0 + jnp.exp(-x))


def _silu(x):
    return x * _sigmoid(x)


def _softplus(x):
    return jnp.maximum(x, 0.0) + jnp.log(1.0 + jnp.exp(-jnp.abs(x)))


def _log_sigmoid(x):
    return -_softplus(-x)


def _adaln_kernel(c_ref, w_ref, b_ref, o_ref):
    c = c_ref[...]
    o_ref[0] = _dot_hi(_silu(c), w_ref[0]) + b_ref[0]


def adaln(c, ada_w, ada_b):
    depth, d, n = ada_w.shape
    b = c.shape[0]
    cp = jnp.zeros((8, d), F32).at[:b].set(c)
    tn = 1536
    out = pl.pallas_call(
        _adaln_kernel,
        grid=(depth, n // tn),
        in_specs=[_full((8, d)),
                  pl.BlockSpec((1, d, tn), lambda l, j: (l, 0, j)),
                  pl.BlockSpec((1, 1, tn), lambda l, j: (l, 0, j))],
        out_specs=pl.BlockSpec((1, 8, tn), lambda l, j: (l, 0, j)),
        out_shape=jax.ShapeDtypeStruct((depth, 8, n), F32),
        compiler_params=_cparams(("arbitrary", "arbitrary")),
        name="adaln",
    )(cp, ada_w, ada_b.reshape(depth, 1, n))
    return out[:, :b]


def _ln_kernel(x_ref, g_ref, sc_ref, sh_ref, o_ref):
    x = x_ref[0]
    y = x * lax.rsqrt(jnp.mean(x * x, axis=-1, keepdims=True) + EPS) * g_ref[...]
    o_ref[0] = (y * (1.0 + sc_ref[0]) + sh_ref[0]).astype(o_ref.dtype)


def ln_mod(x, g, sc, sh, out_dtype, tm=512):
    b, t, d = x.shape
    return pl.pallas_call(
        _ln_kernel,
        grid=(b, t // tm),
        in_specs=[pl.BlockSpec((1, tm, d), lambda i, j: (i, j, 0)),
                  _full((1, d)),
                  pl.BlockSpec((1, 1, d), lambda i, j: (i, 0, 0)),
                  pl.BlockSpec((1, 1, d), lambda i, j: (i, 0, 0))],
        out_specs=pl.BlockSpec((1, tm, d), lambda i, j: (i, j, 0)),
        out_shape=jax.ShapeDtypeStruct((b, t, d), out_dtype),
        compiler_params=_cparams(("arbitrary", "arbitrary")),
        name="ln_mod",
    )(x, g.reshape(1, d), sc.reshape(b, 1, d), sh.reshape(b, 1, d))


def proj(h, w, out_dtype, epilogue=None, extras=(), tm=512, name="proj"):
    b, t, d = h.shape
    n = w.shape[1]

    def kern(h_ref, w_ref, *rest):
        o_ref = rest[-1]
        y = _dot(h_ref[0], w_ref[...])
        if epilogue is not None:
            y = epilogue(y, *[e[...] for e in rest[:-1]])
        o_ref[0] = y.astype(out_dtype)

    return pl.pallas_call(
        kern,
        grid=(b, t // tm),
        in_specs=[pl.BlockSpec((1, tm, d), lambda i, j: (i, j, 0)), _full((d, n))]
                 + [_full(e.shape) for e in extras],
        out_specs=pl.BlockSpec((1, tm, n), lambda i, j: (i, j, 0)),
        out_shape=jax.ShapeDtypeStruct((b, t, n), out_dtype),
        compiler_params=_cparams(("arbitrary", "arbitrary")),
        name=name,
    )(h, w, *extras)


def proj_multi(h, groups, tm=512, name="proj"):
    b, t, d = h.shape
    widths = [g[0].shape[1] for g in groups]
    starts = np.cumsum([0] + widths)
    w_cat = jnp.concatenate([g[0] for g in groups], axis=1).astype(BF16)
    extras = [e for g in groups for e in g[3]]
    n_ex = [len(g[3]) for g in groups]
    n_out = len(groups)

    def kern(h_ref, w_ref, *rest):
        ex_refs = rest[:len(extras)]
        o_refs = rest[len(extras):]
        y = _dot(h_ref[0], w_ref[...])
        pos = 0
        for gi, (_, out_dtype, epilogue, _) in enumerate(groups):
            yg = y[:, starts[gi]:starts[gi + 1]]
            if epilogue is not None:
                yg = epilogue(yg, *[e[...] for e in ex_refs[pos:pos + n_ex[gi]]])
            pos += n_ex[gi]
            o_refs[gi][0] = yg.astype(out_dtype)

    return pl.pallas_call(
        kern,
        grid=(b, t // tm),
        in_specs=[pl.BlockSpec((1, tm, d), lambda i, j: (i, j, 0)), _full((d, int(starts[-1])))]
                 + [_full(e.shape) for e in extras],
        out_specs=[pl.BlockSpec((1, tm, n), lambda i, j: (i, j, 0)) for n in widths],
        out_shape=[jax.ShapeDtypeStruct((b, t, n), g[1]) for n, g in zip(widths, groups)],
        compiler_params=_cparams(("arbitrary", "arbitrary")),
        name=name,
    )(h, w_cat, *extras)


def _head_norm_epilogue(dh):
    inv = 1.0 / dh

    def ep(y, bd, gain):
        ssq = _dot((y * y).astype(BF16), bd)
        return y * lax.rsqrt(ssq * inv + EPS) * gain

    return ep


def _block_diag_ones(n, dh):
    i = np.arange(n) // dh
    return jnp.asarray((i[:, None] == i[None, :]).astype(np.float32), dtype=BF16)


def _outproj_kernel(y1_ref, y2_ref, wa_ref, wb_ref, x_ref, g_ref, o_ref):
    y = _dot(y1_ref[0], wa_ref[...]) + _dot(y2_ref[0], wb_ref[...])
    o_ref[0] = x_ref[0] + g_ref[0] * y


def out_proj(y1, y2, w_out, x, gate, tm=512):
    b, t, d = x.shape
    n1, n2 = y1.shape[-1], y2.shape[-1]
    wa = w_out[:n1].astype(BF16)
    wb = w_out[n1:].astype(BF16)
    return pl.pallas_call(
        _outproj_kernel,
        grid=(b, t // tm),
        in_specs=[pl.BlockSpec((1, tm, n1), lambda i, j: (i, j, 0)),
                  pl.BlockSpec((1, tm, n2), lambda i, j: (i, j, 0)),
                  _full((n1, d)), _full((n2, d)),
                  pl.BlockSpec((1, tm, d), lambda i, j: (i, j, 0)),
                  pl.BlockSpec((1, 1, d), lambda i, j: (i, 0, 0))],
        out_specs=pl.BlockSpec((1, tm, d), lambda i, j: (i, j, 0)),
        out_shape=jax.ShapeDtypeStruct((b, t, d), F32),
        compiler_params=_cparams(("arbitrary", "arbitrary")),
        name="out_proj",
    )(y1, y2, wa, wb, x, gate.reshape(b, 1, d))


def _decay_kernel(s_ref, fb_ref, tril_ref, place_ref, o_ref, carry):
    @pl.when(pl.program_id(1) == 0)
    def _():
        carry[...] = jnp.zeros_like(carry)

    tm = s_ref.shape[1]
    lf = _log_sigmoid(s_ref[0] + fb_ref[...])
    cum = _dot_hi(tril_ref[...], lf) + carry[...]
    carry[...] = cum[tm - 1:tm, :]
    x = cum * LOG2E
    hi = x.astype(BF16)
    r1 = x - hi.astype(F32)
    mid = r1.astype(BF16)
    low = (r1 - mid.astype(F32)).astype(BF16)
    o_ref[0] = _dot(jnp.concatenate([hi, mid, low], axis=1), place_ref[...]).astype(o_ref.dtype)


def fox_decay(small, fox_fb, tm=512):
    b, t, _ = small.shape
    fb = jnp.zeros((1, LANES), F32).at[0, :FOX_HEADS].set(fox_fb)
    tril = jnp.asarray(np.tril(np.ones((tm, tm), np.float32)))
    place = np.zeros((3 * LANES, FOX_W), np.float32)
    for h in range(FOX_HEADS):
        for j in range(3):
            place[j * LANES + h, (h // 2) * LANES + (FOX_DH if h % 2 == 0 else 0) + j] = 1.0
    return pl.pallas_call(
        _decay_kernel,
        grid=(b, t // tm),
        in_specs=[pl.BlockSpec((1, tm, LANES), lambda i, j: (i, j, 0)), _full((1, LANES)), _full((tm, tm)),
                  _full((3 * LANES, FOX_W))],
        out_specs=pl.BlockSpec((1, tm, FOX_W), lambda i, j: (i, j, 0)),
        out_shape=jax.ShapeDtypeStruct((b, t, FOX_W), BF16),
        scratch_shapes=[pltpu.VMEM((1, LANES), F32)],
        compiler_params=_cparams(("arbitrary", "arbitrary")),
        name="fox_decay",
    )(small, fb, tril, jnp.asarray(place, dtype=BF16))


def _fox_kernel(q_ref, k_ref, v_ref, f_ref, o_ref, *, tq, tk):
    i = pl.program_id(2)
    lane = lax.broadcasted_iota(jnp.int32, (1, LANES), 1)
    lo = lane < FOX_DH
    coef = jnp.where((lane & (FOX_DH - 1)) < 3, -1.0, 0.0).astype(BF16)
    q = q_ref[0]
    qs = (jnp.where(lo, q, coef), jnp.where(lo, coef, q))
    n_full = (i * tq) // tk
    causal = (lax.broadcasted_iota(jnp.int32, (tq, tk), 1)
              <= lax.broadcasted_iota(jnp.int32, (tq, tk), 0) + (i * tq - n_full * tk))

    def scores(j):
        start = pl.multiple_of(j * tk, tk)
        kt = k_ref[0, pl.ds(start, tk), :]
        ft = f_ref[0, pl.ds(start, tk), :]
        return _dot_nt(qs[0], jnp.where(lo, kt, ft)), _dot_nt(qs[1], jnp.where(lo, ft, kt))

    one = jnp.ones((1, LANES), BF16)

    def update(j, s_pair, carry, diag):
        vt = v_ref[0, pl.ds(pl.multiple_of(j * tk, tk), tk), :]
        vs = (jnp.where(lo, vt, one), jnp.where(lo, one, vt))
        new = []
        for hh in range(2):
            m, acc = carry[hh]
            s = s_pair[hh]
            if diag:
                s = jnp.where(causal, s, NEG)
            m_new = jnp.maximum(m, jnp.max(s, axis=1, keepdims=True))
            p = jnp.exp2(s - m_new)
            acc = jnp.exp2(m - m_new) * acc + _dot(p.astype(BF16), vs[hh])
            new.append((m_new, acc))
        return tuple(new)

    init = tuple((jnp.full((tq, 1), NEG, F32), jnp.zeros((tq, LANES), F32)) for _ in range(2))
    carry = lax.fori_loop(0, n_full, lambda j, c: update(j, scores(j), c, False), init)
    carry = update(n_full, scores(n_full), carry, True)
    acc = jnp.where(lo, carry[0][1], carry[1][1])
    den = jnp.where(lo, carry[1][1], carry[0][1])
    o_ref[0] = (acc / pltpu.roll(den, FOX_DH, 1)).astype(o_ref.dtype)


def fox_attention(q, k, v, feat, tq=512, tk=1024):
    b, t, w = q.shape
    npair = w // LANES
    nt = t // tq
    tk = min(tk, t)
    whole = lambda: pl.BlockSpec((1, t, LANES), lambda bi, p, i: (bi, 0, p))
    return pl.pallas_call(
        functools.partial(_fox_kernel, tq=tq, tk=tk),
        grid=(b, npair, nt),
        in_specs=[pl.BlockSpec((1, tq, LANES), lambda bi, p, i: (bi, i, p)), whole(), whole(), whole()],
        out_specs=pl.BlockSpec((1, tq, LANES), lambda bi, p, i: (bi, i, p)),
        out_shape=jax.ShapeDtypeStruct((b, t, w), BF16),
        compiler_params=_cparams(("arbitrary", "arbitrary", "arbitrary")),
        name="fox_attn",
    )(q, k, v, feat)


def _mm(a, b):
    return _dot(a.astype(BF16), b.astype(BF16))


def _mm3(a, b):
    ah = a.astype(BF16)
    bh = b.astype(BF16)
    al = (a - ah.astype(F32)).astype(BF16)
    bl = (b - bh.astype(F32)).astype(BF16)
    return _dot(jnp.concatenate([ah, ah, al], axis=1), jnp.concatenate([bh, bl, bh], axis=0))


def _tril_solve(a, rhs, ri, ci):
    n = a[0].shape[0]
    both = lambda f, x, y: [f(p, q) for p, q in zip(x, y)]
    eye = (ri == ci).astype(F32)
    same = lambda b: (lax.shift_right_logical(ri, int(math.log2(b)))
                      == lax.shift_right_logical(ci, int(math.log2(b))))
    base = 16
    d = [jnp.where(same(base), p, 0.0) for p in a]
    d2 = both(_mm3, d, d)
    d4 = both(_mm3, d2, d2)
    r1 = [eye - p + p2 - t for p, p2, t in zip(d, d2, both(_mm3, d, d2))]
    d8 = both(_mm3, d4, d4)
    r2 = [eye + p4 + p8 + t for p4, p8, t in zip(d4, d8, both(_mm3, d4, d8))]
    t = both(_mm3, r1, r2)
    b = base
    while b < n:
        join = same(2 * b) & jnp.logical_not(same(b))
        low = [jnp.where(join, p, 0.0) for p in a]
        t = [p - q for p, q in zip(t, both(_mm3, both(_mm3, t, low), t))]
        b *= 2
    return both(_mm3, t, rhs)


GDN_BLOCK = 128


def _gdn_kernel(x_ref, sm_ref, z_ref, cw_ref, ega_ref, egb_ref, alog_ref, dtb_ref, on_ref, tril_ref,
                o_ref, s_scr, prev_scr, *, tc):
    c = GDN_BLOCK
    w = GDN_W

    @pl.when(pl.program_id(1) == 0)
    def _():
        s_scr[...] = jnp.zeros_like(s_scr)
        prev_scr[...] = jnp.zeros_like(prev_scr)

    x = x_ref[0]
    prev = prev_scr[...]
    row8 = lax.broadcasted_iota(jnp.int32, (8, 1), 0)
    acc = x * cw_ref[GDN_CONV - 1:GDN_CONV, :]
    for s in range(1, GDN_CONV):
        rolled = pltpu.roll(x, s, 0)
        head = jnp.where(row8 < s, pltpu.roll(prev, s, 0), rolled[0:8])
        shifted = jnp.concatenate([head, rolled[8:]], axis=0)
        acc = acc + shifted * cw_ref[GDN_CONV - 1 - s:GDN_CONV - s, :]
    prev_scr[...] = x[tc - 8:tc]
    xc = _silu(acc)

    sm = sm_ref[0]
    g_raw = _dot_hi(sm, ega_ref[...])
    b_raw = _dot_hi(sm, egb_ref[...])
    g = -jnp.exp(alog_ref[...]) * _softplus(g_raw + dtb_ref[...])
    beta_all = _sigmoid(b_raw)
    gc_all = _dot_hi(tril_ref[...], g)

    ri = lax.broadcasted_iota(jnp.int32, (c, c), 0)
    ci = lax.broadcasted_iota(jnp.int32, (c, c), 1)
    causal = ci <= ri
    strict = ci < ri

    nblk = tc // c
    a_l, attn_l, rhs_l, qd_l, kd_l, egl_l = [], [], [], [], [], []
    for h in range(GDN_HEADS):
        ln = slice(h * GDN_DH, (h + 1) * GDN_DH)
        qh = xc[:, h * GDN_DH:(h + 1) * GDN_DH]
        kh = xc[:, w + h * GDN_DH:w + (h + 1) * GDN_DH]
        qh = qh * lax.rsqrt(jnp.sum(qh * qh, axis=-1, keepdims=True) + EPS) * (GDN_DH ** -0.5)
        kh = kh * lax.rsqrt(jnp.sum(kh * kh, axis=-1, keepdims=True) + EPS)
        vh = xc[:, 2 * w + h * GDN_DH:2 * w + (h + 1) * GDN_DH]
        gch = gc_all[:, ln]
        gct = gch.T
        egc = jnp.exp(gch)
        bh = beta_all[:, ln]
        for n in range(nblk):
            sl = slice(n * c, (n + 1) * c)
            q, k, v, gc, be = qh[sl], kh[sl], vh[sl], gch[sl], bh[sl]
            decay = jnp.exp(jnp.where(causal, gc - gct[:, sl], NEG))
            kb = k * be
            kk = _dot_nt(jnp.concatenate([kb, q], axis=0).astype(BF16), k.astype(BF16))
            a_l.append(jnp.where(strict, kk[:c] * decay, 0.0))
            attn_l.append(jnp.where(causal, kk[c:] * decay, 0.0))
            rhs_l.append(jnp.concatenate([v * be, kb * egc[sl]], axis=1))
            gl = gc[c - 1:c, :]
            qd_l.append(q * egc[sl])
            kd_l.append(k * jnp.exp(gl - gc))
            egl_l.append(jnp.exp(gl))
    uw_l = _tril_solve(a_l, rhs_l, ri, ci)

    states = [s_scr[h] for h in range(GDN_HEADS)]
    for n in range(nblk):
        sl = slice(n * c, (n + 1) * c)
        idx = [h * nblk + n for h in range(GDN_HEADS)]
        ws = [_mm(jnp.concatenate([uw_l[i][:, GDN_DH:], qd_l[i]], axis=0), states[h])
              for h, i in enumerate(idx)]
        v_new = [uw_l[i][:, :GDN_DH] - ws[h][:c] for h, i in enumerate(idx)]
        o = [ws[h][c:] + _mm(attn_l[i], v_new[h]) for h, i in enumerate(idx)]
        states = [states[h] * egl_l[i] + _dot_tn(kd_l[i].astype(BF16), v_new[h].astype(BF16))
                  for h, i in enumerate(idx)]
        for h in range(GDN_HEADS):
            ln = slice(h * GDN_DH, (h + 1) * GDN_DH)
            on = o[h] * lax.rsqrt(jnp.mean(o[h] * o[h], axis=-1, keepdims=True) + EPS) * on_ref[...]
            o_ref[0, sl, ln] = (on * _silu(z_ref[0, sl, ln])).astype(o_ref.dtype)
    for h in range(GDN_HEADS):
        s_scr[h] = states[h]


def gated_delta_net(x, small, z, conv_w, a_log, dt_bias, on_gain, tc=512):
    b, t, _ = x.shape
    w = GDN_W
    ega = np.zeros((LANES, w), np.float32)
    egb = np.zeros((LANES, w), np.float32)
    for h in range(GDN_HEADS):
        ega[FOX_HEADS + h, h * GDN_DH:(h + 1) * GDN_DH] = 1.0
        egb[FOX_HEADS + GDN_HEADS + h, h * GDN_DH:(h + 1) * GDN_DH] = 1.0
    alog = jnp.repeat(a_log, GDN_DH).reshape(1, w)
    dtb = jnp.repeat(dt_bias, GDN_DH).reshape(1, w)
    idx = np.arange(tc)
    tril = ((idx[:, None] >= idx[None, :]) & (idx[:, None] // GDN_BLOCK == idx[None, :] // GDN_BLOCK))
    row = lambda n: pl.BlockSpec((1, tc, n), lambda i, j: (i, j, 0))
    return pl.pallas_call(
        functools.partial(_gdn_kernel, tc=tc),
        grid=(b, t // tc),
        in_specs=[row(3 * w), row(LANES), row(w), _full((GDN_CONV, 3 * w)), _full((LANES, w)), _full((LANES, w)),
                  _full((1, w)), _full((1, w)), _full((1, GDN_DH)), _full((tc, tc))],
        out_specs=row(w),
        out_shape=jax.ShapeDtypeStruct((b, t, w), BF16),
        scratch_shapes=[pltpu.VMEM((GDN_HEADS, GDN_DH, GDN_DH), F32), pltpu.VMEM((8, 3 * w), F32)],
        compiler_params=_cparams(("arbitrary", "arbitrary")),
        name="gdn",
    )(x, small, z, conv_w, jnp.asarray(ega), jnp.asarray(egb), alog, dtb, on_gain.reshape(1, GDN_DH),
      jnp.asarray(tril.astype(np.float32)))


def even_mixer(h, w_in, fox_fb, fox_qn, fox_kn, gdn_conv, gdn_a_log, gdn_dt_bias, gdn_on):
    cuts = np.cumsum((0,) + EV_SIZES)
    col = lambda i: w_in[:, cuts[i]:cuts[i + 1]]
    wb = lambda a: a.astype(BF16)
    bd = _block_diag_ones(FOX_W, FOX_DH)
    ep = _head_norm_epilogue(FOX_DH)
    qg = (jnp.tile(fox_qn, FOX_HEADS) * (FOX_DH ** -0.5 * LOG2E)).reshape(1, FOX_W)
    kg = jnp.tile(fox_kn, FOX_HEADS).reshape(1, FOX_W)
    w_small = jnp.zeros((w_in.shape[0], LANES), F32)
    w_small = w_small.at[:, 0:8].set(col(3)).at[:, 8:12].set(col(5)).at[:, 12:16].set(col(6))
    fq, fk, fv, small, gqkv, gz = proj_multi(
        h, [(col(0), BF16, ep, (bd, qg)), (col(1), BF16, ep, (bd, kg)), (col(2), BF16, None, ()),
            (w_small, F32, None, ()), (col(4), F32, None, ()), (col(7), F32, None, ())], name="proj_even")
    feat = fox_decay(small, fox_fb)
    o_fox = fox_attention(fq, fk, fv, feat)
    o_gdn = gated_delta_net(gqkv, small, gz, gdn_conv, gdn_a_log, gdn_dt_bias, gdn_on)
    return o_fox, o_gdn


def _t5_bucket_np(dist):
    n = np.maximum(dist, 0)
    exact = REL_BUCKETS // 2
    nf = np.maximum(n, 1).astype(np.float32)
    large = exact + (np.log(nf / np.float32(exact)) / np.float32(math.log(REL_MAX_DIST / exact))
                     * np.float32(REL_BUCKETS - exact)).astype(np.int32)
    large = np.minimum(large, REL_BUCKETS - 1)
    return np.where(n < exact, n, large)


def _bias_kernel(tbl_ref, bucket_ref, o_ref):
    h = pl.program_id(0)
    bucket = bucket_ref[...]
    acc = jnp.full(bucket.shape, NEG, F32)
    for b in range(REL_BUCKETS):
        acc = jnp.where(bucket == b, tbl_ref[b, h], acc)
    o_ref[0] = acc


def _bias_table(rel_bias, dist, valid):
    shifted = (rel_bias - rel_bias[REL_BUCKETS - 1:REL_BUCKETS]) * LOG2E
    bucket = np.where(valid, _t5_bucket_np(dist), -1).astype(np.int32)
    rows, cols = int(np.prod(bucket.shape[:-1])), bucket.shape[-1]
    nh = rel_bias.shape[1]
    tb = pl.pallas_call(
        _bias_kernel,
        grid=(nh,),
        in_specs=[pl.BlockSpec(memory_space=pltpu.SMEM), _full((rows, cols))],
        out_specs=pl.BlockSpec((1, rows, cols), lambda h: (h, 0, 0)),
        out_shape=jax.ShapeDtypeStruct((nh, rows, cols), F32),
        compiler_params=_cparams(("arbitrary",)),
        name="t5_bias",
    )(shifted, jnp.asarray(bucket.reshape(rows, cols)))
    return tb.reshape((nh,) + bucket.shape)


def _cmp_kernel(r_ref, pos_ref, w1_ref, w2_ref, kn_ref, o_ref):
    m = r_ref.shape[3]
    half = r_ref.shape[4]
    r = r_ref[0, 0, 0].astype(BF16)
    a = _dot(r, w1_ref[0, :half, :])
    bm = _dot(r, w1_ref[0, half:, :])
    c = _dot(pos_ref[0].astype(BF16), w1_ref[0])
    hid = a + pltpu.roll(bm, m - 1, 0) + c[0:1, :]
    out = _dot(_silu(hid).astype(BF16), w2_ref[0])
    normed = out * lax.rsqrt(jnp.mean(out * out, axis=-1, keepdims=True) + EPS) * kn_ref[...]
    o_ref[0, 0, 0] = jnp.where(pl.program_id(0) == 0, normed, out).astype(o_ref.dtype)


def nsa_compress(kcvc, pos, w1, w2, kn):
    b, t, _ = kcvc.shape
    m = t // CMP_STRIDE
    half = CMP_STRIDE * NSA_DH
    r = kcvc.reshape(b, m, CMP_STRIDE, 2, NSA_KV_HEADS, NSA_DH).transpose(3, 0, 4, 1, 2, 5).reshape(2, b, 2, m, half)
    posf = jnp.zeros((2, 8, 2 * half), F32).at[:, 0].set(pos.reshape(2, 2 * half))
    w2d = jnp.concatenate([w2, w2], axis=-1).astype(BF16)
    knd = jnp.tile(kn, 2).reshape(1, LANES)
    return pl.pallas_call(
        _cmp_kernel,
        grid=(2, b, NSA_KV_HEADS),
        in_specs=[pl.BlockSpec((1, 1, 1, m, half), lambda s, i, k: (s, i, k, 0, 0)),
                  pl.BlockSpec((1, 8, 2 * half), lambda s, i, k: (s, 0, 0)),
                  pl.BlockSpec((1, 2 * half, CMP_HIDDEN), lambda s, i, k: (s, 0, 0)),
                  pl.BlockSpec((1, CMP_HIDDEN, LANES), lambda s, i, k: (s, 0, 0)),
                  _full((1, LANES))],
        out_specs=pl.BlockSpec((1, 1, 1, m, LANES), lambda s, i, k: (s, i, k, 0, 0)),
        out_shape=jax.ShapeDtypeStruct((2, b, NSA_KV_HEADS, m, LANES), BF16),
        compiler_params=_cparams(("arbitrary", "arbitrary", "arbitrary")),
        name="nsa_compress",
    )(r, posf, w1.astype(BF16), w2d, knd)


def _dot_split(a, b):
    hi = a.astype(BF16)
    lo = (a - hi.astype(F32)).astype(BF16)
    return _dot(hi, b) + _dot(lo, b)


def _head_q(q_ref, hh, lo):
    blk = q_ref[0, :, (hh // 2) * LANES:(hh // 2 + 1) * LANES]
    keep = lo if hh % 2 == 0 else jnp.logical_not(lo)
    return jnp.where(keep, blk, jnp.zeros_like(blk))


def _pair_heads(o, lo):
    return jnp.concatenate([jnp.where(lo, o[0], o[1]), jnp.where(lo, o[2], o[3])], axis=1)


def _nsa_sel_kernel(q_ref, kc_ref, vc_ref, ov_ref, bt_ref, o_ref, sel_ref, *, tq, nband, n_slc):
    i = pl.program_id(2)
    ncp = kc_ref.shape[3]
    nsp = sel_ref.shape[3]
    per = tq // CMP_STRIDE
    var = jnp.minimum(i, 1)
    bs = pl.multiple_of(per * jnp.maximum(i - 1, 0), per)
    lo = lax.broadcasted_iota(jnp.int32, (1, LANES), 1) < NSA_DH
    kc = kc_ref[0, 0, 0]
    vc = vc_ref[0, 0, 0]
    kcb = kc_ref[0, 0, 0, pl.ds(bs, nband), :]
    vcb = vc_ref[0, 0, 0, pl.ds(bs, nband), :]
    far_ok = lax.broadcasted_iota(jnp.int32, (1, ncp), 1) < per * (i - 1)
    hs = range(NSA_GROUP)
    qh = [_head_q(q_ref, hh, lo) for hh in hs]
    s_far = [jnp.where(far_ok, _dot_nt(q, kc), NEG) for q in qh]
    s_band = [_dot_nt(qh[hh], kcb) + bt_ref[var, hh] for hh in hs]
    m = [jnp.maximum(jnp.max(a, axis=1, keepdims=True), jnp.max(b, axis=1, keepdims=True))
         for a, b in zip(s_far, s_band)]
    m = [jnp.where(x < 0.5 * NEG, 0.0, x) for x in m]
    p_far = [jnp.exp2(a - x) for a, x in zip(s_far, m)]
    p_band = [jnp.exp2(b - x) for b, x in zip(s_band, m)]
    l = [jnp.sum(a, axis=1, keepdims=True) + jnp.sum(b, axis=1, keepdims=True) for a, b in zip(p_far, p_band)]
    inv = [1.0 / jnp.where(x == 0.0, 1.0, x) for x in l]
    outs = [(_dot(a.astype(BF16), vc) + _dot(b.astype(BF16), vcb)) * x for a, b, x in zip(p_far, p_band, inv)]
    ps_far = p_far[0] * inv[0]
    ps_band = p_band[0] * inv[0]
    for hh in range(1, NSA_GROUP):
        ps_far = ps_far + p_far[hh] * inv[hh]
        ps_band = ps_band + p_band[hh] * inv[hh]
    o_ref[0] = _pair_heads(outs, lo).astype(o_ref.dtype)

    imp = _dot_split(ps_far, ov_ref[...]) + _dot_split(ps_band, ov_ref[pl.ds(bs, nband), :])
    blk = lax.broadcasted_iota(jnp.int32, (1, nsp), 1)
    blk_f = blk.astype(F32)
    qpos = i * tq + lax.broadcasted_iota(jnp.int32, (tq, 1), 0)
    cur = lax.shift_right_logical(qpos, int(math.log2(SLC_LEN)))
    forced = (blk == 0) | (blk == cur) | (blk == cur - 1)
    work = jnp.where(forced, FORCE_SCORE, jnp.where(blk <= cur, imp, NEG))
    work = jnp.where(blk < n_slc, work, -jnp.inf)
    ngrp = 4
    rg = tq // ngrp
    works = [work[r * rg:(r + 1) * rg] for r in range(ngrp)]
    sels = [jnp.zeros((rg, nsp), F32) for _ in range(ngrp)]
    for _ in range(min(SLC_TOPK, n_slc)):
        ms = [jnp.max(w, axis=1, keepdims=True) for w in works]
        firsts = [jnp.min(jnp.where(w == m, blk_f, float(nsp)), axis=1, keepdims=True) for w, m in zip(works, ms)]
        picks = [blk_f == f for f in firsts]
        sels = [jnp.where(p, 1.0, s) for p, s in zip(picks, sels)]
        works = [jnp.where(p, -jnp.inf, w) for p, w in zip(picks, works)]
    sel_ref[0, 0] = jnp.concatenate(sels, axis=0).astype(sel_ref.dtype)


def nsa_select(q, cmp_kv, rel_bias, tq=512):
    b, t, _ = q.shape
    ncp = t // CMP_STRIDE
    n_cmp = ncp - 1
    n_slc = t // SLC_LEN
    nsp = max(LANES, n_slc)
    per = tq // CMP_STRIDE
    nband = 2 * per
    n = np.arange(ncp)[:, None]
    s = np.arange(nsp)[None, :]
    ov = ((CMP_STRIDE * n < SLC_LEN * s + SLC_LEN) & (CMP_STRIDE * n + CMP_LEN > SLC_LEN * s)
          & (n < n_cmp) & (s < n_slc)).astype(np.float32)
    qi = np.arange(tq)[:, None]
    nj = np.arange(nband)[None, :]
    end = CMP_STRIDE * nj + CMP_LEN - 1
    dist = np.stack([qi - end, tq + qi - end])
    bt = _bias_table(rel_bias, dist, dist >= 0)
    bt = bt.reshape(NSA_KV_HEADS, NSA_GROUP, 2, tq, nband).transpose(0, 2, 1, 3, 4)
    bt = bt.reshape(NSA_KV_HEADS * 2, NSA_GROUP, tq, nband)
    gw = NSA_GROUP * NSA_DH
    return pl.pallas_call(
        functools.partial(_nsa_sel_kernel, tq=tq, nband=nband, n_slc=n_slc),
        grid=(b, NSA_KV_HEADS, t // tq),
        in_specs=[pl.BlockSpec((1, tq, gw), lambda bi, k, i: (bi, i, k)),
                  pl.BlockSpec((1, 1, 1, ncp, LANES), lambda bi, k, i: (0, bi, k, 0, 0)),
                  pl.BlockSpec((1, 1, 1, ncp, LANES), lambda bi, k, i: (1, bi, k, 0, 0)),
                  _full((ncp, nsp)),
                  pl.BlockSpec((2, NSA_GROUP, tq, nband), lambda bi, k, i: (k, 0, 0, 0))],
        out_specs=[pl.BlockSpec((1, tq, gw), lambda bi, k, i: (bi, i, k)),
                   pl.BlockSpec((1, 1, tq, nsp), lambda bi, k, i: (bi, k, i, 0))],
        out_shape=[jax.ShapeDtypeStruct((b, t, NSA_W), BF16),
                   jax.ShapeDtypeStruct((b, NSA_KV_HEADS, t, nsp), BF16)],
        compiler_params=_cparams(("arbitrary", "arbitrary", "arbitrary")),
        name="nsa_select",
    )(q, cmp_kv, cmp_kv, jnp.asarray(ov, dtype=BF16), bt)


def _nsa_main_kernel(q_ref, ks_ref, vs_ref, kw0_ref, kw1_ref, kw2_ref, vw0_ref, vw1_ref, vw2_ref, sel_ref,
                     ocmp_ref, gate_ref, tb_ref, wm_ref, eg_ref, o_ref, *, tq):
    i = pl.program_id(2)
    nsp = sel_ref.shape[3]
    g = NSA_GROUP
    lo = lax.broadcasted_iota(jnp.int32, (1, LANES), 1) < NSA_DH
    qst = jnp.concatenate([_head_q(q_ref, hh, lo) for hh in range(g)], axis=0)
    sel = sel_ref[0, 0]
    blk_row = lax.broadcasted_iota(jnp.int32, (nsp, 1), 0)
    causal = (lax.broadcasted_iota(jnp.int32, (tq, tq), 1) <= lax.broadcasted_iota(jnp.int32, (tq, tq), 0))
    one = jnp.ones((1, LANES), BF16)

    def sel_step(jt, carry, near, tw):
        m, acc = carry
        start = pl.multiple_of(jnp.maximum(jt, 0) * tw, tw)
        kt = ks_ref[0, pl.ds(start, tw), :]
        vt = jnp.where(lo, vs_ref[0, pl.ds(start, tw), :], one)
        col_blk = lax.shift_right_logical(lax.broadcasted_iota(jnp.int32, (1, tw), 1), int(math.log2(SLC_LEN)))
        expand = jnp.where(blk_row - (tw // SLC_LEN) * jt == col_blk, 1.0, 0.0).astype(BF16)
        keep = _dot(sel, expand)
        madd = (keep - 1.0) * (-NEG)
        if near == 2:
            madd = jnp.where(causal, madd, NEG)
        s = _dot_nt(qst, kt).reshape(g, tq, tw) + madd[None]
        if near is not None:
            s = s + tb_ref[:, :, near * tq:(near + 1) * tq]
        s = s.reshape(g * tq, tw)
        m_new = jnp.maximum(m, jnp.max(s, axis=1, keepdims=True))
        p = jnp.exp2(s - m_new)
        acc = jnp.exp2(m - m_new) * acc + _dot(p.astype(BF16), vt)
        return m_new, acc

    n_far = jnp.maximum(i - 2, 0)
    n4 = lax.shift_right_logical(n_far, 2)
    carry = (jnp.full((g * tq, 1), NEG, F32), jnp.zeros((g * tq, LANES), F32))
    carry = lax.fori_loop(0, n4, lambda j, c: sel_step(j, c, None, 4 * tq), carry)
    carry = lax.cond((n_far & 2) == 2, lambda c: sel_step(2 * n4, c, None, 2 * tq), lambda c: c, carry)
    carry = lax.cond((n_far & 1) == 1, lambda c: sel_step(i - 3, c, None, tq), lambda c: c, carry)
    for near in range(3):
        carry = sel_step(i - 2 + near, carry, near, tq)
    o_slc = (carry[1] / pltpu.roll(carry[1], NSA_DH, 1)).reshape(g, tq, LANES)

    var = jnp.minimum(i, 2)
    kws = (kw0_ref, kw1_ref, kw2_ref)
    vws = (vw0_ref, vw1_ref, vw2_ref)
    sw = []
    for near in range(3):
        s = _dot_nt(qst, kws[near][0]).reshape(g, tq, tq)
        s = s + tb_ref[:, :, near * tq:(near + 1) * tq] + wm_ref[var, :, near * tq:(near + 1) * tq][None]
        sw.append(s.reshape(g * tq, tq))
    m = jnp.maximum(jnp.maximum(jnp.max(sw[0], axis=1, keepdims=True), jnp.max(sw[1], axis=1, keepdims=True)),
                    jnp.max(sw[2], axis=1, keepdims=True))
    acc = jnp.zeros((g * tq, LANES), F32)
    for near in range(3):
        p = jnp.exp2(sw[near] - m)
        acc = acc + _dot(p.astype(BF16), jnp.where(lo, vws[near][0], one))
    o_win = (acc / pltpu.roll(acc, NSA_DH, 1)).reshape(g, tq, LANES)

    pair = lambda o: jnp.concatenate([jnp.where(lo, o[0], pltpu.roll(o[1], NSA_DH, 1)),
                                      jnp.where(lo, o[2], pltpu.roll(o[3], NSA_DH, 1))], axis=1)
    gates = _dot_hi(_sigmoid(gate_ref[0]), eg_ref[0])
    gw = g * NSA_DH
    out = (gates[:, 0:gw] * ocmp_ref[0].astype(F32)
           + gates[:, gw:2 * gw] * pair(o_slc) + gates[:, 2 * gw:3 * gw] * pair(o_win))
    o_ref[0] = out.astype(o_ref.dtype)


def nsa_main(q, ksw, vsw, sel, o_cmp, small, rel_bias, tq=256):
    b, t, _ = q.shape
    nsp = sel.shape[-1]
    g = NSA_GROUP
    gw = g * NSA_DH
    qi = np.arange(tq)[:, None]
    c = np.arange(3 * tq)[None, :]
    dist = qi + 2 * tq - c
    tb = _bias_table(rel_bias, dist, np.ones_like(dist, bool))
    wm = np.zeros((3, tq, 3 * tq), np.float32)
    for var in range(3):
        exists = c >= tq * (2 - var)
        wm[var] = np.where((dist >= 0) & (dist < WINDOW) & exists, 0.0, NEG)
    eg = np.zeros((NSA_KV_HEADS, LANES, 3 * gw), np.float32)
    for k in range(NSA_KV_HEADS):
        for hh in range(g):
            for br in range(3):
                eg[k, (k * g + hh) * 3 + br, br * gw + hh * NSA_DH:br * gw + (hh + 1) * NSA_DH] = 1.0
    near = lambda off, col: pl.BlockSpec(
        (1, tq, LANES), lambda bi, k, i: (bi, jnp.maximum(i - off, 0), col + k))
    return pl.pallas_call(
        functools.partial(_nsa_main_kernel, tq=tq),
        grid=(b, NSA_KV_HEADS, t // tq),
        in_specs=[pl.BlockSpec((1, tq, gw), lambda bi, k, i: (bi, i, k)),
                  pl.BlockSpec((1, t, LANES), lambda bi, k, i: (bi, 0, k)),
                  pl.BlockSpec((1, t, LANES), lambda bi, k, i: (bi, 0, k)),
                  near(2, 2), near(1, 2), near(0, 2), near(2, 2), near(1, 2), near(0, 2),
                  pl.BlockSpec((1, 1, tq, nsp), lambda bi, k, i: (bi, k, i, 0)),
                  pl.BlockSpec((1, tq, gw), lambda bi, k, i: (bi, i, k)),
                  pl.BlockSpec((1, tq, LANES), lambda bi, k, i: (bi, i, 0)),
                  pl.BlockSpec((g, tq, 3 * tq), lambda bi, k, i: (k, 0, 0)),
                  _full((3, tq, 3 * tq)),
                  pl.BlockSpec((1, LANES, 3 * gw), lambda bi, k, i: (k, 0, 0))],
        out_specs=pl.BlockSpec((1, tq, gw), lambda bi, k, i: (bi, i, k)),
        out_shape=jax.ShapeDtypeStruct((b, t, NSA_W), BF16),
        compiler_params=_cparams(("arbitrary", "arbitrary", "arbitrary")),
        name="nsa_main",
    )(q, ksw, vsw, ksw, ksw, ksw, vsw, vsw, vsw, sel, o_cmp, small, tb, jnp.asarray(wm), jnp.asarray(eg))


def _gla_kernel(qk_ref, v_ref, r_ref, sm_ref, wg_ref, bg_ref, on_ref, tril_ref, o_ref, s_scr, *, tc):
    c = GLA_CHUNK

    @pl.when(pl.program_id(1) == 0)
    def _():
        s_scr[...] = jnp.zeros_like(s_scr)

    kw = GLA_KW
    log_a = _log_sigmoid(_dot_hi(sm_ref[0], wg_ref[...]) + bg_ref[...]) * (1.0 / GLA_TAU)
    gcum = _dot_hi(tril_ref[...], log_a)
    q = qk_ref[0, :, 0:kw] * (GLA_DK ** -0.5)
    k = qk_ref[0, :, kw:2 * kw]
    q_dec = (q * jnp.exp(gcum)).astype(BF16)
    k_inv = (k * jnp.exp(-gcum)).astype(BF16)
    ri = lax.broadcasted_iota(jnp.int32, (c, c), 0)
    ci = lax.broadcasted_iota(jnp.int32, (c, c), 1)
    causal = ci <= ri
    lo = lax.broadcasted_iota(jnp.int32, (1, LANES), 1) < GLA_DK
    nchunk = tc // c
    heads = range(GLA_HEADS)
    zero = jnp.zeros((c, LANES), BF16)

    qm, vb, kd, egl = {}, {}, {}, {}
    for n in range(nchunk):
        sl = slice(n * c, (n + 1) * c)
        gl = gcum[n * c + c - 1:n * c + c, :]
        kdn = (k[sl] * jnp.exp(gl - gcum[sl])).astype(BF16)
        for h in heads:
            pr = slice((h // 2) * LANES, (h // 2 + 1) * LANES)
            keep = lo if h % 2 == 0 else jnp.logical_not(lo)
            qm[n, h] = jnp.where(keep, q_dec[sl, pr], zero)
            vb[n, h] = v_ref[0, sl, h * GLA_DV:(h + 1) * GLA_DV].astype(BF16)
            kd[n, h] = kdn[:, pr]
            egl[n, h] = jnp.exp(gl[:, pr])
    idx = [(n, h) for n in range(nchunk) for h in heads]
    attn = {i: jnp.where(causal, _dot_nt(qm[i], k_inv[i[0] * c:(i[0] + 1) * c, (i[1] // 2) * LANES:
                                                       (i[1] // 2 + 1) * LANES]), 0.0).astype(BF16) for i in idx}
    o_intra = {i: _dot(attn[i], vb[i]) for i in idx}
    kv = {i: _dot_tn(vb[i], kd[i]) for i in idx}

    st = [s_scr[h] for h in heads]
    for n in range(nchunk):
        sl = slice(n * c, (n + 1) * c)
        o = [o_intra[n, h] + _dot_nt(qm[n, h], st[h].astype(BF16)) for h in heads]
        st = [st[h] * egl[n, h] + kv[n, h] for h in heads]
        for h in heads:
            on = o[h] * lax.rsqrt(jnp.mean(o[h] * o[h], axis=-1, keepdims=True) + EPS) * on_ref[...]
            o_ref[0, sl, h * GLA_DV:(h + 1) * GLA_DV] = (
                on * _silu(r_ref[0, sl, h * GLA_DV:(h + 1) * GLA_DV])).astype(o_ref.dtype)
    for h in heads:
        s_scr[h] = st[h]


def gated_linear_attention(qkvr, small, wg_up, bg, on_gain, tc=512):
    b, t, _ = qkvr.shape
    wg = jnp.zeros((LANES, GLA_KW), F32).at[3 * NSA_HEADS:3 * NSA_HEADS + GLA_GATE_RANK].set(wg_up)
    idx = np.arange(tc)
    tril = ((idx[:, None] >= idx[None, :]) & (idx[:, None] // GLA_CHUNK == idx[None, :] // GLA_CHUNK))
    return pl.pallas_call(
        functools.partial(_gla_kernel, tc=tc),
        grid=(b, t // tc),
        in_specs=[pl.BlockSpec((1, tc, 2 * GLA_KW), lambda i, j: (i, j, 0)),
                  pl.BlockSpec((1, tc, GLA_W), lambda i, j: (i, j, 1)),
                  pl.BlockSpec((1, tc, GLA_W), lambda i, j: (i, j, 2)),
                  pl.BlockSpec((1, tc, LANES), lambda i, j: (i, j, 0)),
                  _full((LANES, GLA_KW)), _full((1, GLA_KW)), _full((1, GLA_DV)), _full((tc, tc))],
        out_specs=pl.BlockSpec((1, tc, GLA_W), lambda i, j: (i, j, 0)),
        out_shape=jax.ShapeDtypeStruct((b, t, GLA_W), BF16),
        scratch_shapes=[pltpu.VMEM((GLA_HEADS, GLA_DV, LANES), F32)],
        compiler_params=_cparams(("arbitrary", "arbitrary")),
        name="gla",
    )(qkvr, qkvr, qkvr, small, wg, bg.reshape(1, GLA_KW), on_gain.reshape(1, GLA_DV),
      jnp.asarray(tril.astype(np.float32)))


def odd_mixer(h, w_in, nsa_qn, nsa_kn, nsa_pos, nsa_cmp_w1, nsa_cmp_w2, gla_wg_up, gla_bg, gla_on, rel_bias):
    cuts = np.cumsum((0,) + OD_SIZES)
    col = lambda i: w_in[:, cuts[i]:cuts[i + 1]]
    wb = lambda a: a.astype(BF16)
    dup = lambda a: jnp.concatenate([a[:, :NSA_DH], a[:, :NSA_DH], a[:, NSA_DH:], a[:, NSA_DH:]], axis=1)
    ep = _head_norm_epilogue(NSA_DH)
    bd = _block_diag_ones(NSA_W, NSA_DH)
    qg = (jnp.tile(nsa_qn, NSA_HEADS) * (NSA_DH ** -0.5 * LOG2E)).reshape(1, NSA_W)
    kg = jnp.tile(nsa_kn, NSA_HEADS).reshape(1, NSA_W)
    w_small = jnp.zeros((w_in.shape[0], LANES), F32)
    w_small = w_small.at[:, 0:24].set(col(7)).at[:, 24:40].set(col(11))
    nq, kcvc, ksw, vsw, small, qkvr = proj_multi(
        h, [(col(0), BF16, ep, (bd, qg)),
            (jnp.concatenate([col(1), col(2)], axis=1), F32, None, ()),
            (jnp.concatenate([dup(col(3)), dup(col(5))], axis=1), BF16, ep, (bd, kg)),
            (jnp.concatenate([dup(col(4)), dup(col(6))], axis=1), BF16, None, ()),
            (w_small, F32, None, ()),
            (jnp.concatenate([col(8), col(9), col(10), col(12)], axis=1), F32, None, ())], name="proj_odd")
    cmp_kv = nsa_compress(kcvc, nsa_pos, nsa_cmp_w1, nsa_cmp_w2, nsa_kn)
    o_cmp, sel = nsa_select(nq, cmp_kv, rel_bias)
    o_nsa = nsa_main(nq, ksw, vsw, sel, o_cmp, small, rel_bias)
    o_gla = gated_linear_attention(qkvr, small, gla_wg_up, gla_bg, gla_on)
    return o_nsa, o_gla


MOE_TM = 256
MOE_ROWS = 512


def _first_index(mask_val, idx, big, axis):
    return jnp.min(jnp.where(mask_val, idx, big), axis=axis, keepdims=True)


def _route_kernel(h_ref, rt_ref, b_ref, up_ref, eid_ref, rank_ref, w_ref, cnt_ref, run):
    tm = h_ref.shape[0]
    ne = N_EXPERTS
    gsz = ne // N_GROUPS

    @pl.when(pl.program_id(0) == 0)
    def _():
        run[...] = jnp.zeros_like(run)

    scores = _sigmoid(_dot_nt(rt_ref[...], h_ref[...], HI))
    biased = scores + b_ref[...]
    b3 = biased.reshape(N_GROUPS, gsz, tm)
    i3 = lax.broadcasted_iota(jnp.int32, (1, gsz, 1), 1).astype(F32)
    m1 = jnp.max(b3, axis=1, keepdims=True)
    f1 = _first_index(b3 == m1, i3, float(gsz), 1)
    m2 = jnp.max(jnp.where(i3 == f1, -jnp.inf, b3), axis=1, keepdims=True)
    gs = (m1 + m2).reshape(N_GROUPS, tm)
    gidx = lax.broadcasted_iota(jnp.int32, (N_GROUPS, 1), 0).astype(F32)
    gmask = jnp.zeros((N_GROUPS, tm), F32)
    for _ in range(TOPK_GROUPS):
        m = jnp.max(gs, axis=0, keepdims=True)
        pick = gidx == _first_index(gs == m, gidx, float(N_GROUPS), 0)
        gmask = jnp.where(pick, 1.0, gmask)
        gs = jnp.where(pick, -jnp.inf, gs)
    emask = jnp.broadcast_to(gmask.reshape(N_GROUPS, 1, tm), (N_GROUPS, gsz, tm)).reshape(ne, tm)
    work = jnp.where(emask > 0.5, biased, -jnp.inf)
    eidx = lax.broadcasted_iota(jnp.int32, (ne, 1), 0).astype(F32)
    picks, eids, ws = [], [], []
    for _ in range(TOP_K):
        m = jnp.max(work, axis=0, keepdims=True)
        first = _first_index(work == m, eidx, float(ne), 0)
        pick = eidx == first
        picks.append(pick)
        eids.append(first)
        ws.append(jnp.sum(jnp.where(pick, scores, 0.0), axis=0, keepdims=True))
        work = jnp.where(pick, -jnp.inf, work)
    wsum = ws[0]
    for k in range(1, TOP_K):
        wsum = wsum + ws[k]
    chosen = jnp.zeros((ne, tm), F32)
    for pick in picks:
        chosen = jnp.where(pick, 1.0, chosen)
    pos = run[...] + _dot(chosen.astype(BF16), up_ref[...])
    run[...] = run[...] + jnp.sum(chosen, axis=1, keepdims=True)
    cnt_ref[...] = run[...]
    row = lax.broadcasted_iota(jnp.int32, (8, 1), 0)
    eid_o = jnp.zeros((8, tm), F32)
    rank_o = jnp.zeros((8, tm), F32)
    w_o = jnp.zeros((LANES, tm), F32)
    rowl = lax.broadcasted_iota(jnp.int32, (LANES, 1), 0)
    for k in range(TOP_K):
        rk = jnp.sum(jnp.where(picks[k], pos, 0.0), axis=0, keepdims=True)
        eid_o = jnp.where(row == k, eids[k], eid_o)
        rank_o = jnp.where(row == k, rk, rank_o)
        w_o = jnp.where(rowl == k, ws[k] / wsum * ROUTE_SCALE, w_o)
    eid_ref[0] = eid_o.astype(jnp.int32)
    rank_ref[0] = rank_o.astype(jnp.int32)
    w_ref[...] = w_o.T


def moe_route(h2, router, e_bias, tm=MOE_TM):
    nt, d = h2.shape
    ne = N_EXPERTS
    up = jnp.asarray(np.triu(np.ones((tm, tm), np.float32), 1), dtype=BF16)
    nb = nt // tm
    return pl.pallas_call(
        _route_kernel,
        grid=(nb,),
        in_specs=[pl.BlockSpec((tm, d), lambda i: (i, 0)), _full((ne, d)), _full((ne, 1)), _full((tm, tm))],
        out_specs=[pl.BlockSpec((1, 8, tm), lambda i: (i, 0, 0)),
                   pl.BlockSpec((1, 8, tm), lambda i: (i, 0, 0)),
                   pl.BlockSpec((tm, LANES), lambda i: (i, 0)),
                   _full((ne, 1))],
        out_shape=[jax.ShapeDtypeStruct((nb, 8, tm), jnp.int32), jax.ShapeDtypeStruct((nb, 8, tm), jnp.int32),
                   jax.ShapeDtypeStruct((nt, LANES), F32), jax.ShapeDtypeStruct((ne, 1), F32)],
        scratch_shapes=[pltpu.VMEM((ne, 1), F32)],
        compiler_params=_cparams(("arbitrary",)),
        name="moe_route",
    )(h2, router.T, e_bias.reshape(ne, 1), up)


def _dispatch_kernel(dest_ref, h_ref, xs_ref, sem):
    tm = h_ref.shape[0]

    def copy(t, row):
        return pltpu.make_async_copy(h_ref.at[pl.ds(t, 1), :], xs_ref.at[pl.ds(row, 1), :], sem)

    def issue(t, _):
        for k in range(TOP_K):
            copy(t, dest_ref[0, k, t]).start()
        return 0

    def drain(t, _):
        for k in range(TOP_K):
            copy(0, 0).wait()
        return 0

    lax.fori_loop(0, tm, issue, 0, unroll=4)
    lax.fori_loop(0, tm, drain, 0, unroll=4)


def moe_dispatch(h2, dest, tm=MOE_TM):
    nt, d = h2.shape
    return pl.pallas_call(
        _dispatch_kernel,
        grid=(nt // tm,),
        in_specs=[pl.BlockSpec((1, 8, tm), lambda i: (i, 0, 0), memory_space=pltpu.SMEM),
                  pl.BlockSpec((tm, d), lambda i: (i, 0))],
        out_specs=pl.BlockSpec(memory_space=pl.ANY),
        scratch_shapes=[pltpu.SemaphoreType.DMA(())],
        out_shape=jax.ShapeDtypeStruct((nt * TOP_K, d), F32),
        compiler_params=_cparams(("arbitrary",)),
        name="moe_dispatch",
    )(dest, h2)


def _ffn_kernel(blk_ref, exp_ref, lo_ref, hi_ref, first_ref, valid_ref, x_ref, wg_ref, wu_ref, wd_ref, o_ref,
                wg_b, wu_b, wd_b):
    i = pl.program_id(0)
    rows = x_ref.shape[0]

    @pl.when((i == 0) | (exp_ref[i] != exp_ref[jnp.maximum(i - 1, 0)]))
    def _():
        wg_b[...] = wg_ref[0].astype(BF16)
        wu_b[...] = wu_ref[0].astype(BF16)
        wd_b[...] = wd_ref[0].astype(BF16)

    @pl.when(valid_ref[i] == 1)
    def _():
        x = x_ref[...].astype(BF16)
        a = _dot(x, wg_b[...])
        u = _dot(x, wu_b[...])
        y = _dot((_silu(a) * u).astype(BF16), wd_b[...])
        r = blk_ref[i] * rows + lax.broadcasted_iota(jnp.int32, (rows, 1), 0)
        y = jnp.where((r >= lo_ref[i]) & (r < hi_ref[i]), y, 0.0)

        @pl.when(first_ref[i] == 1)
        def _():
            o_ref[...] = y

        @pl.when(first_ref[i] == 0)
        def _():
            o_ref[...] = o_ref[...] + y


def _items_kernel(cnt_ref, starts_ref, blk_ref, exp_ref, lo_ref, hi_ref, first_ref, valid_ref, *, rows, n_items):
    shift = int(math.log2(rows))

    def expert(e, carry):
        start, n = carry
        c = cnt_ref[e]
        starts_ref[e] = start
        end = start + c
        first_blk = lax.shift_right_logical(start, shift)
        n_blk = jnp.where(c > 0, lax.shift_right_logical(jnp.maximum(end - 1, 0), shift) - first_blk + 1, 0)

        def item(k, n):
            b = first_blk + k
            lo = jnp.maximum(start, b * rows)
            blk_ref[n] = b
            exp_ref[n] = e
            lo_ref[n] = lo
            hi_ref[n] = jnp.minimum(end, (b + 1) * rows)
            first_ref[n] = (lo == b * rows).astype(jnp.int32)
            valid_ref[n] = 1
            return n + 1

        return end, lax.fori_loop(0, n_blk, item, n)

    _, total = lax.fori_loop(0, N_EXPERTS, expert, (jnp.int32(0), jnp.int32(0)))
    last = jnp.maximum(total - 1, 0)

    def fill(k, _):
        blk_ref[k] = blk_ref[last]
        exp_ref[k] = exp_ref[last]
        lo_ref[k] = 0
        hi_ref[k] = 0
        first_ref[k] = 0
        valid_ref[k] = 0
        return 0

    lax.fori_loop(total, n_items, fill, 0)


def _ffn_items(counts, n_rows, rows):
    n_items = n_rows // rows + N_EXPERTS - 1
    smem = pl.BlockSpec(memory_space=pltpu.SMEM)
    out = pl.pallas_call(
        functools.partial(_items_kernel, rows=rows, n_items=n_items),
        in_specs=[smem],
        out_specs=[smem] * 7,
        out_shape=[jax.ShapeDtypeStruct((N_EXPERTS,), jnp.int32)]
                  + [jax.ShapeDtypeStruct((n_items,), jnp.int32)] * 6,
        name="moe_items",
    )(counts)
    return out[0], tuple(out[1:])


def moe_ffn_sorted(xs, items, wg, wu, wd, layer, rows=MOE_ROWS):
    n_rows, d = xs.shape
    n_items = items[0].shape[0]
    de = wg.shape[-1]
    return pl.pallas_call(
        _ffn_kernel,
        grid_spec=pltpu.PrefetchScalarGridSpec(
            num_scalar_prefetch=6,
            grid=(n_items,),
            in_specs=[pl.BlockSpec((rows, d), lambda i, blk, e, *_: (blk[i], 0)),
                      pl.BlockSpec((None, 1, d, de), lambda i, blk, e, *_: (layer, e[i], 0, 0)),
                      pl.BlockSpec((None, 1, d, de), lambda i, blk, e, *_: (layer, e[i], 0, 0)),
                      pl.BlockSpec((None, 1, de, d), lambda i, blk, e, *_: (layer, e[i], 0, 0))],
            out_specs=pl.BlockSpec((rows, d), lambda i, blk, e, *_: (blk[i], 0)),
            scratch_shapes=[pltpu.VMEM((d, de), BF16), pltpu.VMEM((d, de), BF16), pltpu.VMEM((de, d), BF16)]),
        out_shape=jax.ShapeDtypeStruct((n_rows, d), F32),
        compiler_params=_cparams(("arbitrary",)),
        name="moe_ffn",
    )(*items, xs, wg, wu, wd)


def _combine_kernel(dest_ref, ys_ref, w_ref, h_ref, x_ref, g_ref, sg_ref, su_ref, sd_ref, o_ref, buf, sem):
    tm = h_ref.shape[0]

    def copy(t, k, row):
        return pltpu.make_async_copy(ys_ref.at[pl.ds(row, 1), :], buf.at[k, pl.ds(t, 1), :], sem)

    def issue(t, _):
        for k in range(TOP_K):
            copy(t, k, dest_ref[0, k, t]).start()
        return 0

    def drain(t, _):
        for k in range(TOP_K):
            copy(0, 0, 0).wait()
        return 0

    lax.fori_loop(0, tm, issue, 0, unroll=4)
    hb = h_ref[...].astype(BF16)
    y = _dot((_silu(_dot(hb, sg_ref[...])) * _dot(hb, su_ref[...])).astype(BF16), sd_ref[...])
    lax.fori_loop(0, tm, drain, 0, unroll=4)
    w = w_ref[...]
    for k in range(TOP_K):
        y = y + w[:, k:k + 1] * buf[k]
    o_ref[...] = x_ref[...] + g_ref[0] * y


def moe_combine(ys, dest, w, h2, x2, gate, sg, su, sd, seq, tm=MOE_TM):
    nt, d = h2.shape
    ds_ = sg.shape[-1]
    per_b = seq // tm
    tile = lambda: pl.BlockSpec((tm, d), lambda i: (i, 0))
    return pl.pallas_call(
        _combine_kernel,
        grid=(nt // tm,),
        in_specs=[pl.BlockSpec((1, 8, tm), lambda i: (i, 0, 0), memory_space=pltpu.SMEM),
                  pl.BlockSpec(memory_space=pl.ANY),
                  pl.BlockSpec((tm, LANES), lambda i: (i, 0)), tile(), tile(),
                  pl.BlockSpec((1, 1, d), lambda i: (i // per_b, 0, 0)),
                  _full((d, ds_)), _full((d, ds_)), _full((ds_, d))],
        out_specs=tile(),
        scratch_shapes=[pltpu.VMEM((TOP_K, tm, d), F32), pltpu.SemaphoreType.DMA(())],
        out_shape=jax.ShapeDtypeStruct((nt, d), F32),
        compiler_params=_cparams(("arbitrary",)),
        name="moe_combine",
    )(dest, ys, w, h2, x2, gate, sg.astype(BF16), su.astype(BF16), sd.astype(BF16))


def moe_layer(x, g_norm, sc, sh, gate, router, e_bias, wg, wu, wd, layer, sg, su, sd):
    b, t, d = x.shape
    nt = b * t
    h = ln_mod(x, g_norm, sc, sh, F32)
    h2 = h.reshape(nt, d)
    eid, rank, w, counts = moe_route(h2, router, e_bias)
    starts, items = _ffn_items(counts.reshape(-1).astype(jnp.int32), nt * TOP_K, MOE_ROWS)
    hit = eid[..., None] == jnp.arange(N_EXPERTS, dtype=jnp.int32)
    dest = jnp.sum(jnp.where(hit, starts.astype(jnp.int32), 0), axis=-1) + rank
    xs = moe_dispatch(h2, dest)
    ys = moe_ffn_sorted(xs, items, wg, wu, wd, layer)
    out = moe_combine(ys, dest, w, h2, x.reshape(nt, d), gate.reshape(b, 1, d), sg, su, sd, t)
    return out.reshape(b, t, d)


def kernel(x, c, ada_w, ada_b, norm_mix, norm_ffn, rel_bias, ev_w_in, ev_w_out, fox_fb, fox_qn, fox_kn, gdn_conv, gdn_a_log, gdn_dt_bias, gdn_on, od_w_in, od_w_out, nsa_qn, nsa_kn, nsa_pos, nsa_cmp_w1, nsa_cmp_w2, gla_wg_up, gla_bg, gla_on, moe_router, moe_bias, moe_wg, moe_wu, moe_wd, sh_wg, sh_wu, sh_wd):
    d = x.shape[-1]
    depth = ada_w.shape[0]
    mod = adaln(c, ada_w, ada_b)
    for layer in range(depth):
        sh1, sc1, g1, sh2, sc2, g2 = [mod[layer, :, i * d:(i + 1) * d] for i in range(6)]
        h = ln_mod(x, norm_mix[layer], sc1, sh1, BF16)
        j = layer // 2
        if layer % 2 == 0:
            y1, y2 = even_mixer(h, ev_w_in[j], fox_fb[j], fox_qn[j], fox_kn[j], gdn_conv[j], gdn_a_log[j],
                                gdn_dt_bias[j], gdn_on[j])
            w_out = ev_w_out[j]
        else:
            y1, y2 = odd_mixer(h, od_w_in[j], nsa_qn[j], nsa_kn[j], nsa_pos[j], nsa_cmp_w1[j], nsa_cmp_w2[j],
                               gla_wg_up[j], gla_bg[j], gla_on[j], rel_bias)
            w_out = od_w_out[j]
        x = out_proj(y1, y2, w_out, x, g1)
        x = moe_layer(x, norm_ffn[layer], sc2, sh2, g2, moe_router[layer], moe_bias[layer], moe_wg, moe_wu, moe_wd,
                      layer, sh_wg[layer], sh_wu[layer], sh_wd[layer])
    return x
```

```python
import functools
import math

import numpy as np
import jax
import jax.numpy as jnp
from jax import lax
from jax.experimental import pallas as pl
from jax.experimental.pallas import tpu as pltpu

F32 = jnp.float32
BF16 = jnp.bfloat16
HI = lax.Precision.HIGHEST

EPS = 1e-6
LOG2E = math.log2(math.e)
NEG = -1e30
FORCE_SCORE = 1e9

FOX_HEADS, FOX_DH = 8, 64
GDN_HEADS, GDN_DH, GDN_CONV, GDN_CHUNK = 4, 128, 4, 64
NSA_HEADS, NSA_KV_HEADS, NSA_DH = 8, 2, 64
NSA_GROUP = NSA_HEADS // NSA_KV_HEADS
CMP_LEN, CMP_STRIDE, CMP_HIDDEN = 32, 16, 256
SLC_LEN, SLC_TOPK, WINDOW = 64, 16, 512
GLA_HEADS, GLA_DK, GLA_DV, GLA_GATE_RANK, GLA_TAU, GLA_CHUNK = 4, 64, 128, 16, 16.0, 64
REL_BUCKETS, REL_MAX_DIST = 32, 128
N_EXPERTS, TOP_K, D_EXPERT, D_SHARED = 64, 6, 256, 256
N_GROUPS, TOPK_GROUPS, ROUTE_SCALE = 8, 4, 2.5

FOX_W = FOX_HEADS * FOX_DH
GDN_W = GDN_HEADS * GDN_DH
NSA_W = NSA_HEADS * NSA_DH
NSA_KV_W = NSA_KV_HEADS * NSA_DH
GLA_KW = GLA_HEADS * GLA_DK
GLA_W = GLA_HEADS * GLA_DV
EV_SIZES = (FOX_W, FOX_W, FOX_W, FOX_HEADS, 3 * GDN_W, GDN_HEADS, GDN_HEADS, GDN_W)
OD_SIZES = (NSA_W,) + (NSA_KV_W,) * 6 + (3 * NSA_HEADS, GLA_KW, GLA_KW, GLA_W, GLA_GATE_RANK, GLA_W)

LANES = 128
ROW_GROUP = 64
VMEM_LIMIT = 56 * 1024 * 1024


def _cparams(sem, flags=None):
    return pltpu.CompilerParams(dimension_semantics=sem, vmem_limit_bytes=VMEM_LIMIT, flags=flags)


def _full(shape):
    n = len(shape)
    return pl.BlockSpec(shape, lambda *_: (0,) * n)


def _dot(a, b):
    return jnp.dot(a, b, preferred_element_type=F32)


def _dot_hi(a, b):
    return jnp.dot(a, b, precision=HI, preferred_element_type=F32)


def _dot_nt(a, b, precision=None):
    return lax.dot_general(a, b, (((1,), (1,)), ((), ())), precision=precision, preferred_element_type=F32)


def _dot_tn(a, b, precision=None):
    return lax.dot_general(a, b, (((0,), (0,)), ((), ())), precision=precision, preferred_element_type=F32)


def _sigmoid(x):
    return 1.0 / (1.0 + jnp.exp(-x))


def _silu(x):
    return x * _sigmoid(x)


def _softplus(x):
    return jnp.maximum(x, 0.0) + jnp.log(1.0 + jnp.exp(-jnp.abs(x)))


def _log_sigmoid(x):
    return -_softplus(-x)


def _adaln_kernel(c_ref, w_ref, b_ref, o_ref):
    c = c_ref[...]
    o_ref[0] = _dot_hi(_silu(c), w_ref[0]) + b_ref[0]


def adaln(c, ada_w, ada_b):
    depth, d, n = ada_w.shape
    b = c.shape[0]
    cp = jnp.zeros((8, d), F32).at[:b].set(c)
    tn = 1536
    out = pl.pallas_call(
        _adaln_kernel,
        grid=(depth, n // tn),
        in_specs=[_full((8, d)),
                  pl.BlockSpec((1, d, tn), lambda l, j: (l, 0, j)),
                  pl.BlockSpec((1, 1, tn), lambda l, j: (l, 0, j))],
        out_specs=pl.BlockSpec((1, 8, tn), lambda l, j: (l, 0, j)),
        out_shape=jax.ShapeDtypeStruct((depth, 8, n), F32),
        compiler_params=_cparams(("arbitrary", "arbitrary")),
        name="adaln",
    )(cp, ada_w, ada_b.reshape(depth, 1, n))
    return out[:, :b]


def _ln_kernel(x_ref, g_ref, sc_ref, sh_ref, o_ref):
    x = x_ref[0]
    y = x * lax.rsqrt(jnp.mean(x * x, axis=-1, keepdims=True) + EPS) * g_ref[...]
    o_ref[0] = (y * (1.0 + sc_ref[0]) + sh_ref[0]).astype(o_ref.dtype)


def ln_mod(x, g, sc, sh, out_dtype, tm=512):
    b, t, d = x.shape
    return pl.pallas_call(
        _ln_kernel,
        grid=(b, t // tm),
        in_specs=[pl.BlockSpec((1, tm, d), lambda i, j: (i, j, 0)),
                  _full((1, d)),
                  pl.BlockSpec((1, 1, d), lambda i, j: (i, 0, 0)),
                  pl.BlockSpec((1, 1, d), lambda i, j: (i, 0, 0))],
        out_specs=pl.BlockSpec((1, tm, d), lambda i, j: (i, j, 0)),
        out_shape=jax.ShapeDtypeStruct((b, t, d), out_dtype),
        compiler_params=_cparams(("arbitrary", "arbitrary")),
        name="ln_mod",
    )(x, g.reshape(1, d), sc.reshape(b, 1, d), sh.reshape(b, 1, d))


def proj(h, w, out_dtype, epilogue=None, extras=(), tm=512, name="proj"):
    b, t, d = h.shape
    n = w.shape[1]

    def kern(h_ref, w_ref, *rest):
        o_ref = rest[-1]
        y = _dot(h_ref[0], w_ref[...])
        if epilogue is not None:
            y = epilogue(y, *[e[...] for e in rest[:-1]])
        o_ref[0] = y.astype(out_dtype)

    return pl.pallas_call(
        kern,
        grid=(b, t // tm),
        in_specs=[pl.BlockSpec((1, tm, d), lambda i, j: (i, j, 0)), _full((d, n))]
                 + [_full(e.shape) for e in extras],
        out_specs=pl.BlockSpec((1, tm, n), lambda i, j: (i, j, 0)),
        out_shape=jax.ShapeDtypeStruct((b, t, n), out_dtype),
        compiler_params=_cparams(("arbitrary", "arbitrary")),
        name=name,
    )(h, w, *extras)


def proj_multi(h, groups, tm=512, name="proj"):
    b, t, d = h.shape
    widths = [g[0].shape[1] for g in groups]
    starts = np.cumsum([0] + widths)
    w_cat = jnp.concatenate([g[0] for g in groups], axis=1).astype(BF16)
    extras = [e for g in groups for e in g[3]]
    n_ex = [len(g[3]) for g in groups]
    n_out = len(groups)

    def kern(h_ref, w_ref, *rest):
        ex_refs = rest[:len(extras)]
        o_refs = rest[len(extras):]
        y = _dot(h_ref[0], w_ref[...])
        pos = 0
        for gi, (_, out_dtype, epilogue, _) in enumerate(groups):
            yg = y[:, starts[gi]:starts[gi + 1]]
            if epilogue is not None:
                yg = epilogue(yg, *[e[...] for e in ex_refs[pos:pos + n_ex[gi]]])
            pos += n_ex[gi]
            o_refs[gi][0] = yg.astype(out_dtype)

    return pl.pallas_call(
        kern,
        grid=(b, t // tm),
        in_specs=[pl.BlockSpec((1, tm, d), lambda i, j: (i, j, 0)), _full((d, int(starts[-1])))]
                 + [_full(e.shape) for e in extras],
        out_specs=[pl.BlockSpec((1, tm, n), lambda i, j: (i, j, 0)) for n in widths],
        out_shape=[jax.ShapeDtypeStruct((b, t, n), g[1]) for n, g in zip(widths, groups)],
        compiler_params=_cparams(("arbitrary", "arbitrary")),
        name=name,
    )(h, w_cat, *extras)


def _head_norm_epilogue(dh):
    inv = 1.0 / dh

    def ep(y, bd, gain):
        ssq = _dot((y * y).astype(BF16), bd)
        return y * lax.rsqrt(ssq * inv + EPS) * gain

    return ep


def _block_diag_ones(n, dh):
    i = np.arange(n) // dh
    return jnp.asarray((i[:, None] == i[None, :]).astype(np.float32), dtype=BF16)


def _outproj_kernel(y1_ref, y2_ref, wa_ref, wb_ref, x_ref, g_ref, o_ref):
    y = _dot(y1_ref[0], wa_ref[...]) + _dot(y2_ref[0], wb_ref[...])
    o_ref[0] = x_ref[0] + g_ref[0] * y


def out_proj(y1, y2, w_out, x, gate, tm=512):
    b, t, d = x.shape
    n1, n2 = y1.shape[-1], y2.shape[-1]
    wa = w_out[:n1].astype(BF16)
    wb = w_out[n1:].astype(BF16)
    return pl.pallas_call(
        _outproj_kernel,
        grid=(b, t // tm),
        in_specs=[pl.BlockSpec((1, tm, n1), lambda i, j: (i, j, 0)),
                  pl.BlockSpec((1, tm, n2), lambda i, j: (i, j, 0)),
                  _full((n1, d)), _full((n2, d)),
                  pl.BlockSpec((1, tm, d), lambda i, j: (i, j, 0)),
                  pl.BlockSpec((1, 1, d), lambda i, j: (i, 0, 0))],
        out_specs=pl.BlockSpec((1, tm, d), lambda i, j: (i, j, 0)),
        out_shape=jax.ShapeDtypeStruct((b, t, d), F32),
        compiler_params=_cparams(("arbitrary", "arbitrary")),
        name="out_proj",
    )(y1, y2, wa, wb, x, gate.reshape(b, 1, d))


def _decay_kernel(s_ref, fb_ref, tril_ref, place_ref, o_ref, carry):
    @pl.when(pl.program_id(1) == 0)
    def _():
        carry[...] = jnp.zeros_like(carry)

    tm = s_ref.shape[1]
    lf = _log_sigmoid(s_ref[0] + fb_ref[...])
    cum = _dot_hi(tril_ref[...], lf) + carry[...]
    carry[...] = cum[tm - 1:tm, :]
    x = cum * LOG2E
    hi = x.astype(BF16)
    r1 = x - hi.astype(F32)
    mid = r1.astype(BF16)
    low = (r1 - mid.astype(F32)).astype(BF16)
    o_ref[0] = _dot(jnp.concatenate([hi, mid, low], axis=1), place_ref[...]).astype(o_ref.dtype)


def fox_decay(small, fox_fb, tm=512):
    b, t, _ = small.shape
    fb = jnp.zeros((1, LANES), F32).at[0, :FOX_HEADS].set(fox_fb)
    tril = jnp.asarray(np.tril(np.ones((tm, tm), np.float32)))
    place = np.zeros((3 * LANES, FOX_W), np.float32)
    for h in range(FOX_HEADS):
        for j in range(3):
            place[j * LANES + h, (h // 2) * LANES + (FOX_DH if h % 2 == 0 else 0) + j] = 1.0
    return pl.pallas_call(
        _decay_kernel,
        grid=(b, t // tm),
        in_specs=[pl.BlockSpec((1, tm, LANES), lambda i, j: (i, j, 0)), _full((1, LANES)), _full((tm, tm)),
                  _full((3 * LANES, FOX_W))],
        out_specs=pl.BlockSpec((1, tm, FOX_W), lambda i, j: (i, j, 0)),
        out_shape=jax.ShapeDtypeStruct((b, t, FOX_W), BF16),
        scratch_shapes=[pltpu.VMEM((1, LANES), F32)],
        compiler_params=_cparams(("arbitrary", "arbitrary")),
        name="fox_decay",
    )(small, fb, tril, jnp.asarray(place, dtype=BF16))


def _fox_kernel(q_ref, k_ref, v_ref, f_ref, o_ref, *, tq, wide):
    i = pl.program_id(2)
    lane = lax.broadcasted_iota(jnp.int32, (1, LANES), 1)
    lo = lane < FOX_DH
    coef = jnp.where((lane & (FOX_DH - 1)) < 3, -1.0, 0.0).astype(BF16)
    q = q_ref[0]
    qs = (jnp.where(lo, q, coef), jnp.where(lo, coef, q))
    causal = (lax.broadcasted_iota(jnp.int32, (tq, tq), 1) <= lax.broadcasted_iota(jnp.int32, (tq, tq), 0))
    one = jnp.ones((1, LANES), BF16)

    def step(j, carry, tw, diag=False):
        start = pl.multiple_of(j * tw, tw)
        kt = k_ref[0, pl.ds(start, tw), :]
        ft = f_ref[0, pl.ds(start, tw), :]
        vt = v_ref[0, pl.ds(start, tw), :]
        s_pair = (_dot_nt(qs[0], jnp.where(lo, kt, ft)), _dot_nt(qs[1], jnp.where(lo, ft, kt)))
        vs = (jnp.where(lo, vt, one), jnp.where(lo, one, vt))
        new = []
        for hh in range(2):
            m, acc = carry[hh]
            s = s_pair[hh]
            if diag:
                s = jnp.where(causal, s, NEG)
            m_new = jnp.maximum(m, jnp.max(s, axis=1, keepdims=True))
            p = jnp.exp2(s - m_new)
            acc = jnp.exp2(m - m_new) * acc + _dot(p.astype(BF16), vs[hh])
            new.append((m_new, acc))
        return tuple(new)

    carry = tuple((jnp.full((tq, 1), NEG, F32), jnp.zeros((tq, LANES), F32)) for _ in range(2))
    n_wide = i // wide
    carry = lax.fori_loop(0, n_wide, lambda j, c: step(j, c, wide * tq), carry)
    done = n_wide * wide
    part = wide // 2
    while part >= 1:
        carry = lax.cond((i & part) != 0, lambda c, d=done, w=part: step(d // w, c, w * tq), lambda c: c, carry)
        done = done + (i & part)
        part //= 2
    carry = step(i, carry, tq, diag=True)
    acc = jnp.where(lo, carry[0][1], carry[1][1])
    den = jnp.where(lo, carry[1][1], carry[0][1])
    o_ref[0] = (acc / pltpu.roll(den, FOX_DH, 1)).astype(o_ref.dtype)


def fox_attention(q, k, v, feat, tq=512, wide=4):
    b, t, w = q.shape
    npair = w // LANES
    nt = t // tq
    whole = lambda: pl.BlockSpec((1, t, LANES), lambda bi, p, i: (bi, 0, p))
    return pl.pallas_call(
        functools.partial(_fox_kernel, tq=tq, wide=wide),
        grid=(b, npair, nt),
        in_specs=[pl.BlockSpec((1, tq, LANES), lambda bi, p, i: (bi, i, p)), whole(), whole(), whole()],
        out_specs=pl.BlockSpec((1, tq, LANES), lambda bi, p, i: (bi, i, p)),
        out_shape=jax.ShapeDtypeStruct((b, t, w), BF16),
        compiler_params=_cparams(("arbitrary", "arbitrary", "arbitrary")),
        name="fox_attn",
    )(q, k, v, feat)


def _mm(a, b):
    return _dot(a.astype(BF16), b.astype(BF16))


def _mm3(a, b):
    ah = a.astype(BF16)
    bh = b.astype(BF16)
    al = (a - ah.astype(F32)).astype(BF16)
    bl = (b - bh.astype(F32)).astype(BF16)
    return _dot(jnp.concatenate([ah, ah, al], axis=1), jnp.concatenate([bh, bl, bh], axis=0))


def _tril_solve(a, rhs, ri, ci):
    n = a[0].shape[0]
    both = lambda f, x, y: [f(p, q) for p, q in zip(x, y)]
    eye = (ri == ci).astype(F32)
    same = lambda b: (lax.shift_right_logical(ri, int(math.log2(b)))
                      == lax.shift_right_logical(ci, int(math.log2(b))))
    base = 16
    d = [jnp.where(same(base), p, 0.0) for p in a]
    d2 = both(_mm3, d, d)
    d4 = both(_mm3, d2, d2)
    r1 = [eye - p + p2 - t for p, p2, t in zip(d, d2, both(_mm3, d, d2))]
    d8 = both(_mm3, d4, d4)
    r2 = [eye + p4 + p8 + t for p4, p8, t in zip(d4, d8, both(_mm3, d4, d8))]
    t = both(_mm3, r1, r2)
    b = base
    while b < n:
        join = same(2 * b) & jnp.logical_not(same(b))
        low = [jnp.where(join, p, 0.0) for p in a]
        t = [p - q for p, q in zip(t, both(_mm3, both(_mm3, t, low), t))]
        b *= 2
    return both(_mm3, t, rhs)


GDN_BLOCK = 128


def _gdn_kernel(x_ref, sm_ref, z_ref, cw_ref, ega_ref, egb_ref, alog_ref, dtb_ref, on_ref, tril_ref,
                o_ref, s_scr, prev_scr, *, tc):
    c = GDN_BLOCK
    w = GDN_W

    @pl.when(pl.program_id(1) == 0)
    def _():
        s_scr[...] = jnp.zeros_like(s_scr)
        prev_scr[...] = jnp.zeros_like(prev_scr)

    x = x_ref[0]
    prev = prev_scr[...]
    row8 = lax.broadcasted_iota(jnp.int32, (8, 1), 0)
    acc = x * cw_ref[GDN_CONV - 1:GDN_CONV, :]
    for s in range(1, GDN_CONV):
        rolled = pltpu.roll(x, s, 0)
        head = jnp.where(row8 < s, pltpu.roll(prev, s, 0), rolled[0:8])
        shifted = jnp.concatenate([head, rolled[8:]], axis=0)
        acc = acc + shifted * cw_ref[GDN_CONV - 1 - s:GDN_CONV - s, :]
    prev_scr[...] = x[tc - 8:tc]
    xc = _silu(acc)

    sm = sm_ref[0]
    g_raw = _dot_hi(sm, ega_ref[...])
    b_raw = _dot_hi(sm, egb_ref[...])
    g = -jnp.exp(alog_ref[...]) * _softplus(g_raw + dtb_ref[...])
    beta_all = _sigmoid(b_raw)
    gc_all = _dot_hi(tril_ref[...], g)

    ri = lax.broadcasted_iota(jnp.int32, (c, c), 0)
    ci = lax.broadcasted_iota(jnp.int32, (c, c), 1)
    causal = ci <= ri
    strict = ci < ri

    nblk = tc // c
    a_l, attn_l, rhs_l, qd_l, kd_l, egl_l = [], [], [], [], [], []
    for h in range(GDN_HEADS):
        ln = slice(h * GDN_DH, (h + 1) * GDN_DH)
        qh = xc[:, h * GDN_DH:(h + 1) * GDN_DH]
        kh = xc[:, w + h * GDN_DH:w + (h + 1) * GDN_DH]
        qh = qh * lax.rsqrt(jnp.sum(qh * qh, axis=-1, keepdims=True) + EPS) * (GDN_DH ** -0.5)
        kh = kh * lax.rsqrt(jnp.sum(kh * kh, axis=-1, keepdims=True) + EPS)
        vh = xc[:, 2 * w + h * GDN_DH:2 * w + (h + 1) * GDN_DH]
        gch = gc_all[:, ln]
        gct = gch.T
        egc = jnp.exp(gch)
        bh = beta_all[:, ln]
        for n in range(nblk):
            sl = slice(n * c, (n + 1) * c)
            q, k, v, gc, be = qh[sl], kh[sl], vh[sl], gch[sl], bh[sl]
            decay = jnp.exp(jnp.where(causal, gc - gct[:, sl], NEG))
            kb = k * be
            kk = _dot_nt(jnp.concatenate([kb, q], axis=0).astype(BF16), k.astype(BF16))
            a_l.append(jnp.where(strict, kk[:c] * decay, 0.0))
            attn_l.append(jnp.where(causal, kk[c:] * decay, 0.0))
            rhs_l.append(jnp.concatenate([v * be, kb * egc[sl]], axis=1))
            gl = gc[c - 1:c, :]
            qd_l.append(q * egc[sl])
            kd_l.append(k * jnp.exp(gl - gc))
            egl_l.append(jnp.exp(gl))
    uw_l = _tril_solve(a_l, rhs_l, ri, ci)

    states = [s_scr[h] for h in range(GDN_HEADS)]
    for n in range(nblk):
        sl = slice(n * c, (n + 1) * c)
        idx = [h * nblk + n for h in range(GDN_HEADS)]
        ws = [_mm(jnp.concatenate([uw_l[i][:, GDN_DH:], qd_l[i]], axis=0), states[h])
              for h, i in enumerate(idx)]
        v_new = [uw_l[i][:, :GDN_DH] - ws[h][:c] for h, i in enumerate(idx)]
        o = [ws[h][c:] + _mm(attn_l[i], v_new[h]) for h, i in enumerate(idx)]
        states = [states[h] * egl_l[i] + _dot_tn(kd_l[i].astype(BF16), v_new[h].astype(BF16))
                  for h, i in enumerate(idx)]
        for h in range(GDN_HEADS):
            ln = slice(h * GDN_DH, (h + 1) * GDN_DH)
            on = o[h] * lax.rsqrt(jnp.mean(o[h] * o[h], axis=-1, keepdims=True) + EPS) * on_ref[...]
            o_ref[0, sl, ln] = (on * _silu(z_ref[0, sl, ln])).astype(o_ref.dtype)
    for h in range(GDN_HEADS):
        s_scr[h] = states[h]


def gated_delta_net(x, small, z, conv_w, a_log, dt_bias, on_gain, tc=512):
    b, t, _ = x.shape
    w = GDN_W
    ega = np.zeros((LANES, w), np.float32)
    egb = np.zeros((LANES, w), np.float32)
    for h in range(GDN_HEADS):
        ega[FOX_HEADS + h, h * GDN_DH:(h + 1) * GDN_DH] = 1.0
        egb[FOX_HEADS + GDN_HEADS + h, h * GDN_DH:(h + 1) * GDN_DH] = 1.0
    alog = jnp.repeat(a_log, GDN_DH).reshape(1, w)
    dtb = jnp.repeat(dt_bias, GDN_DH).reshape(1, w)
    idx = np.arange(tc)
    tril = ((idx[:, None] >= idx[None, :]) & (idx[:, None] // GDN_BLOCK == idx[None, :] // GDN_BLOCK))
    row = lambda n: pl.BlockSpec((1, tc, n), lambda i, j: (i, j, 0))
    return pl.pallas_call(
        functools.partial(_gdn_kernel, tc=tc),
        grid=(b, t // tc),
        in_specs=[row(3 * w), row(LANES), row(w), _full((GDN_CONV, 3 * w)), _full((LANES, w)), _full((LANES, w)),
                  _full((1, w)), _full((1, w)), _full((1, GDN_DH)), _full((tc, tc))],
        out_specs=row(w),
        out_shape=jax.ShapeDtypeStruct((b, t, w), BF16),
        scratch_shapes=[pltpu.VMEM((GDN_HEADS, GDN_DH, GDN_DH), F32), pltpu.VMEM((8, 3 * w), F32)],
        compiler_params=_cparams(("arbitrary", "arbitrary")),
        name="gdn",
    )(x, small, z, conv_w, jnp.asarray(ega), jnp.asarray(egb), alog, dtb, on_gain.reshape(1, GDN_DH),
      jnp.asarray(tril.astype(np.float32)))


def even_mixer(h, w_in, fox_fb, fox_qn, fox_kn, gdn_conv, gdn_a_log, gdn_dt_bias, gdn_on):
    cuts = np.cumsum((0,) + EV_SIZES)
    col = lambda i: w_in[:, cuts[i]:cuts[i + 1]]
    wb = lambda a: a.astype(BF16)
    bd = _block_diag_ones(FOX_W, FOX_DH)
    ep = _head_norm_epilogue(FOX_DH)
    qg = (jnp.tile(fox_qn, FOX_HEADS) * (FOX_DH ** -0.5 * LOG2E)).reshape(1, FOX_W)
    kg = jnp.tile(fox_kn, FOX_HEADS).reshape(1, FOX_W)
    w_small = jnp.zeros((w_in.shape[0], LANES), F32)
    w_small = w_small.at[:, 0:8].set(col(3)).at[:, 8:12].set(col(5)).at[:, 12:16].set(col(6))
    fq, fk, fv, small, gqkv, gz = proj_multi(
        h, [(col(0), BF16, ep, (bd, qg)), (col(1), BF16, ep, (bd, kg)), (col(2), BF16, None, ()),
            (w_small, F32, None, ()), (col(4), F32, None, ()), (col(7), F32, None, ())], name="proj_even")
    feat = fox_decay(small, fox_fb)
    o_fox = fox_attention(fq, fk, fv, feat)
    o_gdn = gated_delta_net(gqkv, small, gz, gdn_conv, gdn_a_log, gdn_dt_bias, gdn_on)
    return o_fox, o_gdn


def _t5_bucket_np(dist):
    n = np.maximum(dist, 0)
    exact = REL_BUCKETS // 2
    nf = np.maximum(n, 1).astype(np.float32)
    large = exact + (np.log(nf / np.float32(exact)) / np.float32(math.log(REL_MAX_DIST / exact))
                     * np.float32(REL_BUCKETS - exact)).astype(np.int32)
    large = np.minimum(large, REL_BUCKETS - 1)
    return np.where(n < exact, n, large)


def _bias_kernel(tbl_ref, bucket_ref, o_ref):
    h = pl.program_id(0)
    bucket = bucket_ref[...]
    acc = jnp.full(bucket.shape, NEG, F32)
    for b in range(REL_BUCKETS):
        acc = jnp.where(bucket == b, tbl_ref[b, h], acc)
    o_ref[0] = acc


def _bias_table(rel_bias, dist, valid):
    shifted = (rel_bias - rel_bias[REL_BUCKETS - 1:REL_BUCKETS]) * LOG2E
    bucket = np.where(valid, _t5_bucket_np(dist), -1).astype(np.int32)
    rows, cols = int(np.prod(bucket.shape[:-1])), bucket.shape[-1]
    nh = rel_bias.shape[1]
    tb = pl.pallas_call(
        _bias_kernel,
        grid=(nh,),
        in_specs=[pl.BlockSpec(memory_space=pltpu.SMEM), _full((rows, cols))],
        out_specs=pl.BlockSpec((1, rows, cols), lambda h: (h, 0, 0)),
        out_shape=jax.ShapeDtypeStruct((nh, rows, cols), F32),
        compiler_params=_cparams(("arbitrary",)),
        name="t5_bias",
    )(shifted, jnp.asarray(bucket.reshape(rows, cols)))
    return tb.reshape((nh,) + bucket.shape)


def _cmp_kernel(r_ref, pos_ref, w1_ref, w2_ref, kn_ref, o_ref):
    m = r_ref.shape[3]
    half = r_ref.shape[4]
    r = r_ref[0, 0, 0].astype(BF16)
    a = _dot(r, w1_ref[0, :half, :])
    bm = _dot(r, w1_ref[0, half:, :])
    c = _dot(pos_ref[0].astype(BF16), w1_ref[0])
    hid = a + pltpu.roll(bm, m - 1, 0) + c[0:1, :]
    out = _dot(_silu(hid).astype(BF16), w2_ref[0])
    normed = out * lax.rsqrt(jnp.mean(out * out, axis=-1, keepdims=True) + EPS) * kn_ref[...]
    o_ref[0, 0, 0] = jnp.where(pl.program_id(0) == 0, normed, out).astype(o_ref.dtype)


def nsa_compress(kcvc, pos, w1, w2, kn):
    b, t, _ = kcvc.shape
    m = t // CMP_STRIDE
    half = CMP_STRIDE * NSA_DH
    r = kcvc.reshape(b, m, CMP_STRIDE, 2, NSA_KV_HEADS, NSA_DH).transpose(3, 0, 4, 1, 2, 5).reshape(2, b, 2, m, half)
    posf = jnp.zeros((2, 8, 2 * half), F32).at[:, 0].set(pos.reshape(2, 2 * half))
    w2d = jnp.concatenate([w2, w2], axis=-1).astype(BF16)
    knd = jnp.tile(kn, 2).reshape(1, LANES)
    return pl.pallas_call(
        _cmp_kernel,
        grid=(2, b, NSA_KV_HEADS),
        in_specs=[pl.BlockSpec((1, 1, 1, m, half), lambda s, i, k: (s, i, k, 0, 0)),
                  pl.BlockSpec((1, 8, 2 * half), lambda s, i, k: (s, 0, 0)),
                  pl.BlockSpec((1, 2 * half, CMP_HIDDEN), lambda s, i, k: (s, 0, 0)),
                  pl.BlockSpec((1, CMP_HIDDEN, LANES), lambda s, i, k: (s, 0, 0)),
                  _full((1, LANES))],
        out_specs=pl.BlockSpec((1, 1, 1, m, LANES), lambda s, i, k: (s, i, k, 0, 0)),
        out_shape=jax.ShapeDtypeStruct((2, b, NSA_KV_HEADS, m, LANES), BF16),
        compiler_params=_cparams(("arbitrary", "arbitrary", "arbitrary")),
        name="nsa_compress",
    )(r, posf, w1.astype(BF16), w2d, knd)


def _dot_split(a, b):
    hi = a.astype(BF16)
    lo = (a - hi.astype(F32)).astype(BF16)
    return _dot(hi, b) + _dot(lo, b)


def _head_q(q_ref, hh, lo):
    blk = q_ref[0, :, (hh // 2) * LANES:(hh // 2 + 1) * LANES]
    keep = lo if hh % 2 == 0 else jnp.logical_not(lo)
    return jnp.where(keep, blk, jnp.zeros_like(blk))


def _pair_heads(o, lo):
    return jnp.concatenate([jnp.where(lo, o[0], o[1]), jnp.where(lo, o[2], o[3])], axis=1)


def _nsa_sel_kernel(q_ref, kc_ref, vc_ref, ov_ref, bt_ref, o_ref, sel_ref, *, tq, nband, n_slc):
    i = pl.program_id(2)
    ncp = kc_ref.shape[3]
    nsp = sel_ref.shape[3]
    per = tq // CMP_STRIDE
    var = jnp.minimum(i, 1)
    bs = pl.multiple_of(per * jnp.maximum(i - 1, 0), per)
    lo = lax.broadcasted_iota(jnp.int32, (1, LANES), 1) < NSA_DH
    kc = kc_ref[0, 0, 0]
    vc = vc_ref[0, 0, 0]
    kcb = kc_ref[0, 0, 0, pl.ds(bs, nband), :]
    vcb = vc_ref[0, 0, 0, pl.ds(bs, nband), :]
    far_ok = lax.broadcasted_iota(jnp.int32, (1, ncp), 1) < per * (i - 1)
    hs = range(NSA_GROUP)
    qh = [_head_q(q_ref, hh, lo) for hh in hs]
    s_far = [jnp.where(far_ok, _dot_nt(q, kc), NEG) for q in qh]
    s_band = [_dot_nt(qh[hh], kcb) + bt_ref[var, hh] for hh in hs]
    m = [jnp.maximum(jnp.max(a, axis=1, keepdims=True), jnp.max(b, axis=1, keepdims=True))
         for a, b in zip(s_far, s_band)]
    m = [jnp.where(x < 0.5 * NEG, 0.0, x) for x in m]
    p_far = [jnp.exp2(a - x) for a, x in zip(s_far, m)]
    p_band = [jnp.exp2(b - x) for b, x in zip(s_band, m)]
    l = [jnp.sum(a, axis=1, keepdims=True) + jnp.sum(b, axis=1, keepdims=True) for a, b in zip(p_far, p_band)]
    inv = [1.0 / jnp.where(x == 0.0, 1.0, x) for x in l]
    outs = [(_dot(a.astype(BF16), vc) + _dot(b.astype(BF16), vcb)) * x for a, b, x in zip(p_far, p_band, inv)]
    ps_far = p_far[0] * inv[0]
    ps_band = p_band[0] * inv[0]
    for hh in range(1, NSA_GROUP):
        ps_far = ps_far + p_far[hh] * inv[hh]
        ps_band = ps_band + p_band[hh] * inv[hh]
    o_ref[0] = _pair_heads(outs, lo).astype(o_ref.dtype)

    imp = _dot_split(ps_far, ov_ref[...]) + _dot_split(ps_band, ov_ref[pl.ds(bs, nband), :])
    blk = lax.broadcasted_iota(jnp.int32, (1, nsp), 1)
    blk_f = blk.astype(F32)
    qpos = i * tq + lax.broadcasted_iota(jnp.int32, (tq, 1), 0)
    cur = lax.shift_right_logical(qpos, int(math.log2(SLC_LEN)))
    forced = (blk == 0) | (blk == cur) | (blk == cur - 1)
    work = jnp.where(forced, FORCE_SCORE, jnp.where(blk <= cur, imp, NEG))
    work = jnp.where(blk < n_slc, work, -jnp.inf)
    ngrp = 4
    rg = tq // ngrp
    works = [work[r * rg:(r + 1) * rg] for r in range(ngrp)]
    sels = [jnp.zeros((rg, nsp), F32) for _ in range(ngrp)]
    for _ in range(min(SLC_TOPK, n_slc)):
        ms = [jnp.max(w, axis=1, keepdims=True) for w in works]
        firsts = [jnp.min(jnp.where(w == m, blk_f, float(nsp)), axis=1, keepdims=True) for w, m in zip(works, ms)]
        picks = [blk_f == f for f in firsts]
        sels = [jnp.where(p, 1.0, s) for p, s in zip(picks, sels)]
        works = [jnp.where(p, -jnp.inf, w) for p, w in zip(picks, works)]
    sel_ref[0, 0] = jnp.concatenate(sels, axis=0).astype(sel_ref.dtype)


def nsa_select(q, cmp_kv, rel_bias, tq=512):
    b, t, _ = q.shape
    ncp = t // CMP_STRIDE
    n_cmp = ncp - 1
    n_slc = t // SLC_LEN
    nsp = max(LANES, n_slc)
    per = tq // CMP_STRIDE
    nband = 2 * per
    n = np.arange(ncp)[:, None]
    s = np.arange(nsp)[None, :]
    ov = ((CMP_STRIDE * n < SLC_LEN * s + SLC_LEN) & (CMP_STRIDE * n + CMP_LEN > SLC_LEN * s)
          & (n < n_cmp) & (s < n_slc)).astype(np.float32)
    qi = np.arange(tq)[:, None]
    nj = np.arange(nband)[None, :]
    end = CMP_STRIDE * nj + CMP_LEN - 1
    dist = np.stack([qi - end, tq + qi - end])
    bt = _bias_table(rel_bias, dist, dist >= 0)
    bt = bt.reshape(NSA_KV_HEADS, NSA_GROUP, 2, tq, nband).transpose(0, 2, 1, 3, 4)
    bt = bt.reshape(NSA_KV_HEADS * 2, NSA_GROUP, tq, nband)
    gw = NSA_GROUP * NSA_DH
    return pl.pallas_call(
        functools.partial(_nsa_sel_kernel, tq=tq, nband=nband, n_slc=n_slc),
        grid=(b, NSA_KV_HEADS, t // tq),
        in_specs=[pl.BlockSpec((1, tq, gw), lambda bi, k, i: (bi, i, k)),
                  pl.BlockSpec((1, 1, 1, ncp, LANES), lambda bi, k, i: (0, bi, k, 0, 0)),
                  pl.BlockSpec((1, 1, 1, ncp, LANES), lambda bi, k, i: (1, bi, k, 0, 0)),
                  _full((ncp, nsp)),
                  pl.BlockSpec((2, NSA_GROUP, tq, nband), lambda bi, k, i: (k, 0, 0, 0))],
        out_specs=[pl.BlockSpec((1, tq, gw), lambda bi, k, i: (bi, i, k)),
                   pl.BlockSpec((1, 1, tq, nsp), lambda bi, k, i: (bi, k, i, 0))],
        out_shape=[jax.ShapeDtypeStruct((b, t, NSA_W), BF16),
                   jax.ShapeDtypeStruct((b, NSA_KV_HEADS, t, nsp), BF16)],
        compiler_params=_cparams(("arbitrary", "arbitrary", "arbitrary")),
        name="nsa_select",
    )(q, cmp_kv, cmp_kv, jnp.asarray(ov, dtype=BF16), bt)


def _nsa_main_kernel(q_ref, ks_ref, vs_ref, kw0_ref, kw1_ref, kw2_ref, vw0_ref, vw1_ref, vw2_ref, sel_ref,
                     ocmp_ref, gate_ref, tb_ref, wm_ref, eg_ref, o_ref, *, tq):
    i = pl.program_id(2)
    nsp = sel_ref.shape[3]
    g = NSA_GROUP
    lo = lax.broadcasted_iota(jnp.int32, (1, LANES), 1) < NSA_DH
    qst = jnp.concatenate([_head_q(q_ref, hh, lo) for hh in range(g)], axis=0)
    sel = sel_ref[0, 0]
    blk_row = lax.broadcasted_iota(jnp.int32, (nsp, 1), 0)
    causal = (lax.broadcasted_iota(jnp.int32, (tq, tq), 1) <= lax.broadcasted_iota(jnp.int32, (tq, tq), 0))
    one = jnp.ones((1, LANES), BF16)

    def sel_step(jt, carry, near, tw):
        m, acc = carry
        start = pl.multiple_of(jnp.maximum(jt, 0) * tw, tw)
        kt = ks_ref[0, pl.ds(start, tw), :]
        vt = jnp.where(lo, vs_ref[0, pl.ds(start, tw), :], one)
        col_blk = lax.shift_right_logical(lax.broadcasted_iota(jnp.int32, (1, tw), 1), int(math.log2(SLC_LEN)))
        expand = jnp.where(blk_row - (tw // SLC_LEN) * jt == col_blk, 1.0, 0.0).astype(BF16)
        keep = _dot(sel, expand)
        madd = (keep - 1.0) * (-NEG)
        if near == 2:
            madd = jnp.where(causal, madd, NEG)
        s = _dot_nt(qst, kt).reshape(g, tq, tw) + madd[None]
        if near is not None:
            s = s + tb_ref[:, :, near * tq:(near + 1) * tq]
        s = s.reshape(g * tq, tw)
        m_new = jnp.maximum(m, jnp.max(s, axis=1, keepdims=True))
        p = jnp.exp2(s - m_new)
        acc = jnp.exp2(m - m_new) * acc + _dot(p.astype(BF16), vt)
        return m_new, acc

    n_far = jnp.maximum(i - 2, 0)
    n4 = lax.shift_right_logical(n_far, 2)
    carry = (jnp.full((g * tq, 1), NEG, F32), jnp.zeros((g * tq, LANES), F32))
    carry = lax.fori_loop(0, n4, lambda j, c: sel_step(j, c, None, 4 * tq), carry)
    carry = lax.cond((n_far & 2) == 2, lambda c: sel_step(2 * n4, c, None, 2 * tq), lambda c: c, carry)
    carry = lax.cond((n_far & 1) == 1, lambda c: sel_step(i - 3, c, None, tq), lambda c: c, carry)
    for near in range(3):
        carry = sel_step(i - 2 + near, carry, near, tq)
    o_slc = (carry[1] / pltpu.roll(carry[1], NSA_DH, 1)).reshape(g, tq, LANES)

    var = jnp.minimum(i, 2)
    kws = (kw0_ref, kw1_ref, kw2_ref)
    vws = (vw0_ref, vw1_ref, vw2_ref)
    sw = []
    for near in range(3):
        s = _dot_nt(qst, kws[near][0]).reshape(g, tq, tq)
        s = s + tb_ref[:, :, near * tq:(near + 1) * tq] + wm_ref[var, :, near * tq:(near + 1) * tq][None]
        sw.append(s.reshape(g * tq, tq))
    m = jnp.maximum(jnp.maximum(jnp.max(sw[0], axis=1, keepdims=True), jnp.max(sw[1], axis=1, keepdims=True)),
                    jnp.max(sw[2], axis=1, keepdims=True))
    acc = jnp.zeros((g * tq, LANES), F32)
    for near in range(3):
        p = jnp.exp2(sw[near] - m)
        acc = acc + _dot(p.astype(BF16), jnp.where(lo, vws[near][0], one))
    o_win = (acc / pltpu.roll(acc, NSA_DH, 1)).reshape(g, tq, LANES)

    pair = lambda o: jnp.concatenate([jnp.where(lo, o[0], pltpu.roll(o[1], NSA_DH, 1)),
                                      jnp.where(lo, o[2], pltpu.roll(o[3], NSA_DH, 1))], axis=1)
    gates = _dot_hi(_sigmoid(gate_ref[0]), eg_ref[0])
    gw = g * NSA_DH
    out = (gates[:, 0:gw] * ocmp_ref[0].astype(F32)
           + gates[:, gw:2 * gw] * pair(o_slc) + gates[:, 2 * gw:3 * gw] * pair(o_win))
    o_ref[0] = out.astype(o_ref.dtype)


def nsa_main(q, ksw, vsw, sel, o_cmp, small, rel_bias, tq=256):
    b, t, _ = q.shape
    nsp = sel.shape[-1]
    g = NSA_GROUP
    gw = g * NSA_DH
    qi = np.arange(tq)[:, None]
    c = np.arange(3 * tq)[None, :]
    dist = qi + 2 * tq - c
    tb = _bias_table(rel_bias, dist, np.ones_like(dist, bool))
    wm = np.zeros((3, tq, 3 * tq), np.float32)
    for var in range(3):
        exists = c >= tq * (2 - var)
        wm[var] = np.where((dist >= 0) & (dist < WINDOW) & exists, 0.0, NEG)
    eg = np.zeros((NSA_KV_HEADS, LANES, 3 * gw), np.float32)
    for k in range(NSA_KV_HEADS):
        for hh in range(g):
            for br in range(3):
                eg[k, (k * g + hh) * 3 + br, br * gw + hh * NSA_DH:br * gw + (hh + 1) * NSA_DH] = 1.0
    near = lambda off, col: pl.BlockSpec(
        (1, tq, LANES), lambda bi, k, i: (bi, jnp.maximum(i - off, 0), col + k))
    return pl.pallas_call(
        functools.partial(_nsa_main_kernel, tq=tq),
        grid=(b, NSA_KV_HEADS, t // tq),
        in_specs=[pl.BlockSpec((1, tq, gw), lambda bi, k, i: (bi, i, k)),
                  pl.BlockSpec((1, t, LANES), lambda bi, k, i: (bi, 0, k)),
                  pl.BlockSpec((1, t, LANES), lambda bi, k, i: (bi, 0, k)),
                  near(2, 2), near(1, 2), near(0, 2), near(2, 2), near(1, 2), near(0, 2),
                  pl.BlockSpec((1, 1, tq, nsp), lambda bi, k, i: (bi, k, i, 0)),
                  pl.BlockSpec((1, tq, gw), lambda bi, k, i: (bi, i, k)),
                  pl.BlockSpec((1, tq, LANES), lambda bi, k, i: (bi, i, 0)),
                  pl.BlockSpec((g, tq, 3 * tq), lambda bi, k, i: (k, 0, 0)),
                  _full((3, tq, 3 * tq)),
                  pl.BlockSpec((1, LANES, 3 * gw), lambda bi, k, i: (k, 0, 0))],
        out_specs=pl.BlockSpec((1, tq, gw), lambda bi, k, i: (bi, i, k)),
        out_shape=jax.ShapeDtypeStruct((b, t, NSA_W), BF16),
        compiler_params=_cparams(("arbitrary", "arbitrary", "arbitrary")),
        name="nsa_main",
    )(q, ksw, vsw, ksw, ksw, ksw, vsw, vsw, vsw, sel, o_cmp, small, tb, jnp.asarray(wm), jnp.asarray(eg))


def _gla_kernel(qk_ref, v_ref, r_ref, sm_ref, wg_ref, bg_ref, on_ref, tril_ref, o_ref, s_scr, *, tc):
    c = GLA_CHUNK

    @pl.when(pl.program_id(1) == 0)
    def _():
        s_scr[...] = jnp.zeros_like(s_scr)

    kw = GLA_KW
    log_a = _log_sigmoid(_dot_hi(sm_ref[0], wg_ref[...]) + bg_ref[...]) * (1.0 / GLA_TAU)
    gcum = _dot_hi(tril_ref[...], log_a)
    q = qk_ref[0, :, 0:kw] * (GLA_DK ** -0.5)
    k = qk_ref[0, :, kw:2 * kw]
    q_dec = (q * jnp.exp(gcum)).astype(BF16)
    k_inv = (k * jnp.exp(-gcum)).astype(BF16)
    ri = lax.broadcasted_iota(jnp.int32, (c, c), 0)
    ci = lax.broadcasted_iota(jnp.int32, (c, c), 1)
    causal = ci <= ri
    lo = lax.broadcasted_iota(jnp.int32, (1, LANES), 1) < GLA_DK
    nchunk = tc // c
    heads = range(GLA_HEADS)
    zero = jnp.zeros((c, LANES), BF16)

    qm, vb, kd, egl = {}, {}, {}, {}
    for n in range(nchunk):
        sl = slice(n * c, (n + 1) * c)
        gl = gcum[n * c + c - 1:n * c + c, :]
        kdn = (k[sl] * jnp.exp(gl - gcum[sl])).astype(BF16)
        for h in heads:
            pr = slice((h // 2) * LANES, (h // 2 + 1) * LANES)
            keep = lo if h % 2 == 0 else jnp.logical_not(lo)
            qm[n, h] = jnp.where(keep, q_dec[sl, pr], zero)
            vb[n, h] = v_ref[0, sl, h * GLA_DV:(h + 1) * GLA_DV].astype(BF16)
            kd[n, h] = kdn[:, pr]
            egl[n, h] = jnp.exp(gl[:, pr])
    idx = [(n, h) for n in range(nchunk) for h in heads]
    attn = {i: jnp.where(causal, _dot_nt(qm[i], k_inv[i[0] * c:(i[0] + 1) * c, (i[1] // 2) * LANES:
                                                       (i[1] // 2 + 1) * LANES]), 0.0).astype(BF16) for i in idx}
    o_intra = {i: _dot(attn[i], vb[i]) for i in idx}
    kv = {i: _dot_tn(vb[i], kd[i]) for i in idx}

    st = [s_scr[h] for h in heads]
    for n in range(nchunk):
        sl = slice(n * c, (n + 1) * c)
        o = [o_intra[n, h] + _dot_nt(qm[n, h], st[h].astype(BF16)) for h in heads]
        st = [st[h] * egl[n, h] + kv[n, h] for h in heads]
        for h in heads:
            on = o[h] * lax.rsqrt(jnp.mean(o[h] * o[h], axis=-1, keepdims=True) + EPS) * on_ref[...]
            o_ref[0, sl, h * GLA_DV:(h + 1) * GLA_DV] = (
                on * _silu(r_ref[0, sl, h * GLA_DV:(h + 1) * GLA_DV])).astype(o_ref.dtype)
    for h in heads:
        s_scr[h] = st[h]


def gated_linear_attention(qkvr, small, wg_up, bg, on_gain, tc=512):
    b, t, _ = qkvr.shape
    wg = jnp.zeros((LANES, GLA_KW), F32).at[3 * NSA_HEADS:3 * NSA_HEADS + GLA_GATE_RANK].set(wg_up)
    idx = np.arange(tc)
    tril = ((idx[:, None] >= idx[None, :]) & (idx[:, None] // GLA_CHUNK == idx[None, :] // GLA_CHUNK))
    return pl.pallas_call(
        functools.partial(_gla_kernel, tc=tc),
        grid=(b, t // tc),
        in_specs=[pl.BlockSpec((1, tc, 2 * GLA_KW), lambda i, j: (i, j, 0)),
                  pl.BlockSpec((1, tc, GLA_W), lambda i, j: (i, j, 1)),
                  pl.BlockSpec((1, tc, GLA_W), lambda i, j: (i, j, 2)),
                  pl.BlockSpec((1, tc, LANES), lambda i, j: (i, j, 0)),
                  _full((LANES, GLA_KW)), _full((1, GLA_KW)), _full((1, GLA_DV)), _full((tc, tc))],
        out_specs=pl.BlockSpec((1, tc, GLA_W), lambda i, j: (i, j, 0)),
        out_shape=jax.ShapeDtypeStruct((b, t, GLA_W), BF16),
        scratch_shapes=[pltpu.VMEM((GLA_HEADS, GLA_DV, LANES), F32)],
        compiler_params=_cparams(("arbitrary", "arbitrary")),
        name="gla",
    )(qkvr, qkvr, qkvr, small, wg, bg.reshape(1, GLA_KW), on_gain.reshape(1, GLA_DV),
      jnp.asarray(tril.astype(np.float32)))


def odd_mixer(h, w_in, nsa_qn, nsa_kn, nsa_pos, nsa_cmp_w1, nsa_cmp_w2, gla_wg_up, gla_bg, gla_on, rel_bias):
    cuts = np.cumsum((0,) + OD_SIZES)
    col = lambda i: w_in[:, cuts[i]:cuts[i + 1]]
    wb = lambda a: a.astype(BF16)
    dup = lambda a: jnp.concatenate([a[:, :NSA_DH], a[:, :NSA_DH], a[:, NSA_DH:], a[:, NSA_DH:]], axis=1)
    ep = _head_norm_epilogue(NSA_DH)
    bd = _block_diag_ones(NSA_W, NSA_DH)
    qg = (jnp.tile(nsa_qn, NSA_HEADS) * (NSA_DH ** -0.5 * LOG2E)).reshape(1, NSA_W)
    kg = jnp.tile(nsa_kn, NSA_HEADS).reshape(1, NSA_W)
    w_small = jnp.zeros((w_in.shape[0], LANES), F32)
    w_small = w_small.at[:, 0:24].set(col(7)).at[:, 24:40].set(col(11))
    nq, kcvc, ksw, vsw, small, qkvr = proj_multi(
        h, [(col(0), BF16, ep, (bd, qg)),
            (jnp.concatenate([col(1), col(2)], axis=1), F32, None, ()),
            (jnp.concatenate([dup(col(3)), dup(col(5))], axis=1), BF16, ep, (bd, kg)),
            (jnp.concatenate([dup(col(4)), dup(col(6))], axis=1), BF16, None, ()),
            (w_small, F32, None, ()),
            (jnp.concatenate([col(8), col(9), col(10), col(12)], axis=1), F32, None, ())], name="proj_odd")
    cmp_kv = nsa_compress(kcvc, nsa_pos, nsa_cmp_w1, nsa_cmp_w2, nsa_kn)
    o_cmp, sel = nsa_select(nq, cmp_kv, rel_bias)
    o_nsa = nsa_main(nq, ksw, vsw, sel, o_cmp, small, rel_bias)
    o_gla = gated_linear_attention(qkvr, small, gla_wg_up, gla_bg, gla_on)
    return o_nsa, o_gla


MOE_TM = 256
MOE_ROWS = 512


def _first_index(mask_val, idx, big, axis):
    return jnp.min(jnp.where(mask_val, idx, big), axis=axis, keepdims=True)


def _route_kernel(h_ref, rt_ref, b_ref, up_ref, eid_ref, rank_ref, w_ref, cnt_ref, run):
    tm = h_ref.shape[0]
    ne = N_EXPERTS
    gsz = ne // N_GROUPS

    @pl.when(pl.program_id(0) == 0)
    def _():
        run[...] = jnp.zeros_like(run)

    scores = _sigmoid(_dot_nt(rt_ref[...], h_ref[...], HI))
    biased = scores + b_ref[...]
    b3 = biased.reshape(N_GROUPS, gsz, tm)
    i3 = lax.broadcasted_iota(jnp.int32, (1, gsz, 1), 1).astype(F32)
    m1 = jnp.max(b3, axis=1, keepdims=True)
    f1 = _first_index(b3 == m1, i3, float(gsz), 1)
    m2 = jnp.max(jnp.where(i3 == f1, -jnp.inf, b3), axis=1, keepdims=True)
    gs = (m1 + m2).reshape(N_GROUPS, tm)
    gidx = lax.broadcasted_iota(jnp.int32, (N_GROUPS, 1), 0).astype(F32)
    gmask = jnp.zeros((N_GROUPS, tm), F32)
    for _ in range(TOPK_GROUPS):
        m = jnp.max(gs, axis=0, keepdims=True)
        pick = gidx == _first_index(gs == m, gidx, float(N_GROUPS), 0)
        gmask = jnp.where(pick, 1.0, gmask)
        gs = jnp.where(pick, -jnp.inf, gs)
    emask = jnp.broadcast_to(gmask.reshape(N_GROUPS, 1, tm), (N_GROUPS, gsz, tm)).reshape(ne, tm)
    work = jnp.where(emask > 0.5, biased, -jnp.inf)
    eidx = lax.broadcasted_iota(jnp.int32, (ne, 1), 0).astype(F32)
    picks, eids, ws = [], [], []
    for _ in range(TOP_K):
        m = jnp.max(work, axis=0, keepdims=True)
        first = _first_index(work == m, eidx, float(ne), 0)
        pick = eidx == first
        picks.append(pick)
        eids.append(first)
        ws.append(jnp.sum(jnp.where(pick, scores, 0.0), axis=0, keepdims=True))
        work = jnp.where(pick, -jnp.inf, work)
    wsum = ws[0]
    for k in range(1, TOP_K):
        wsum = wsum + ws[k]
    chosen = jnp.zeros((ne, tm), F32)
    for pick in picks:
        chosen = jnp.where(pick, 1.0, chosen)
    pos = run[...] + _dot(chosen.astype(BF16), up_ref[...])
    run[...] = run[...] + jnp.sum(chosen, axis=1, keepdims=True)
    cnt_ref[...] = run[...]
    row = lax.broadcasted_iota(jnp.int32, (8, 1), 0)
    eid_o = jnp.zeros((8, tm), F32)
    rank_o = jnp.zeros((8, tm), F32)
    w_o = jnp.zeros((LANES, tm), F32)
    rowl = lax.broadcasted_iota(jnp.int32, (LANES, 1), 0)
    for k in range(TOP_K):
        rk = jnp.sum(jnp.where(picks[k], pos, 0.0), axis=0, keepdims=True)
        eid_o = jnp.where(row == k, eids[k], eid_o)
        rank_o = jnp.where(row == k, rk, rank_o)
        w_o = jnp.where(rowl == k, ws[k] / wsum * ROUTE_SCALE, w_o)
    eid_ref[0] = eid_o.astype(jnp.int32)
    rank_ref[0] = rank_o.astype(jnp.int32)
    w_ref[...] = w_o.T


def moe_route(h2, router, e_bias, tm=MOE_TM):
    nt, d = h2.shape
    ne = N_EXPERTS
    up = jnp.asarray(np.triu(np.ones((tm, tm), np.float32), 1), dtype=BF16)
    nb = nt // tm
    return pl.pallas_call(
        _route_kernel,
        grid=(nb,),
        in_specs=[pl.BlockSpec((tm, d), lambda i: (i, 0)), _full((ne, d)), _full((ne, 1)), _full((tm, tm))],
        out_specs=[pl.BlockSpec((1, 8, tm), lambda i: (i, 0, 0)),
                   pl.BlockSpec((1, 8, tm), lambda i: (i, 0, 0)),
                   pl.BlockSpec((tm, LANES), lambda i: (i, 0)),
                   _full((ne, 1))],
        out_shape=[jax.ShapeDtypeStruct((nb, 8, tm), jnp.int32), jax.ShapeDtypeStruct((nb, 8, tm), jnp.int32),
                   jax.ShapeDtypeStruct((nt, LANES), F32), jax.ShapeDtypeStruct((ne, 1), F32)],
        scratch_shapes=[pltpu.VMEM((ne, 1), F32)],
        compiler_params=_cparams(("arbitrary",)),
        name="moe_route",
    )(h2, router.T, e_bias.reshape(ne, 1), up)


def _dispatch_kernel(dest_ref, h_ref, xs_ref, sem):
    tm = h_ref.shape[0]

    def copy(t, row):
        return pltpu.make_async_copy(h_ref.at[pl.ds(t, 1), :], xs_ref.at[pl.ds(row, 1), :], sem)

    def issue(t, _):
        for k in range(TOP_K):
            copy(t, dest_ref[0, k, t]).start()
        return 0

    def drain(t, _):
        for k in range(TOP_K):
            copy(0, 0).wait()
        return 0

    lax.fori_loop(0, tm, issue, 0, unroll=4)
    lax.fori_loop(0, tm, drain, 0, unroll=4)


def moe_dispatch(h2, dest, tm=MOE_TM):
    nt, d = h2.shape
    return pl.pallas_call(
        _dispatch_kernel,
        grid=(nt // tm,),
        in_specs=[pl.BlockSpec((1, 8, tm), lambda i: (i, 0, 0), memory_space=pltpu.SMEM),
                  pl.BlockSpec((tm, d), lambda i: (i, 0))],
        out_specs=pl.BlockSpec(memory_space=pl.ANY),
        scratch_shapes=[pltpu.SemaphoreType.DMA(())],
        out_shape=jax.ShapeDtypeStruct((nt * TOP_K, d), F32),
        compiler_params=_cparams(("arbitrary",)),
        name="moe_dispatch",
    )(dest, h2)


def _ffn_kernel(blk_ref, exp_ref, lo_ref, hi_ref, first_ref, valid_ref, x_ref, wg_ref, wu_ref, wd_ref, o_ref,
                wg_b, wu_b, wd_b):
    i = pl.program_id(0)
    rows = x_ref.shape[0]

    @pl.when((i == 0) | (exp_ref[i] != exp_ref[jnp.maximum(i - 1, 0)]))
    def _():
        wg_b[...] = wg_ref[0].astype(BF16)
        wu_b[...] = wu_ref[0].astype(BF16)
        wd_b[...] = wd_ref[0].astype(BF16)

    @pl.when(valid_ref[i] == 1)
    def _():
        x = x_ref[...].astype(BF16)
        a = _dot(x, wg_b[...])
        u = _dot(x, wu_b[...])
        y = _dot((_silu(a) * u).astype(BF16), wd_b[...])
        r = blk_ref[i] * rows + lax.broadcasted_iota(jnp.int32, (rows, 1), 0)
        y = jnp.where((r >= lo_ref[i]) & (r < hi_ref[i]), y, 0.0)

        @pl.when(first_ref[i] == 1)
        def _():
            o_ref[...] = y

        @pl.when(first_ref[i] == 0)
        def _():
            o_ref[...] = o_ref[...] + y


def _items_kernel(cnt_ref, starts_ref, blk_ref, exp_ref, lo_ref, hi_ref, first_ref, valid_ref, *, rows, n_items):
    shift = int(math.log2(rows))

    def expert(e, carry):
        start, n = carry
        c = cnt_ref[e]
        starts_ref[e] = start
        end = start + c
        first_blk = lax.shift_right_logical(start, shift)
        n_blk = jnp.where(c > 0, lax.shift_right_logical(jnp.maximum(end - 1, 0), shift) - first_blk + 1, 0)

        def item(k, n):
            b = first_blk + k
            lo = jnp.maximum(start, b * rows)
            blk_ref[n] = b
            exp_ref[n] = e
            lo_ref[n] = lo
            hi_ref[n] = jnp.minimum(end, (b + 1) * rows)
            first_ref[n] = (lo == b * rows).astype(jnp.int32)
            valid_ref[n] = 1
            return n + 1

        return end, lax.fori_loop(0, n_blk, item, n)

    _, total = lax.fori_loop(0, N_EXPERTS, expert, (jnp.int32(0), jnp.int32(0)))
    last = jnp.maximum(total - 1, 0)

    def fill(k, _):
        blk_ref[k] = blk_ref[last]
        exp_ref[k] = exp_ref[last]
        lo_ref[k] = 0
        hi_ref[k] = 0
        first_ref[k] = 0
        valid_ref[k] = 0
        return 0

    lax.fori_loop(total, n_items, fill, 0)


def _ffn_items(counts, n_rows, rows):
    n_items = n_rows // rows + N_EXPERTS - 1
    smem = pl.BlockSpec(memory_space=pltpu.SMEM)
    out = pl.pallas_call(
        functools.partial(_items_kernel, rows=rows, n_items=n_items),
        in_specs=[smem],
        out_specs=[smem] * 7,
        out_shape=[jax.ShapeDtypeStruct((N_EXPERTS,), jnp.int32)]
                  + [jax.ShapeDtypeStruct((n_items,), jnp.int32)] * 6,
        name="moe_items",
    )(counts)
    return out[0], tuple(out[1:])


def moe_ffn_sorted(xs, items, wg, wu, wd, layer, rows=MOE_ROWS):
    n_rows, d = xs.shape
    n_items = items[0].shape[0]
    de = wg.shape[-1]
    return pl.pallas_call(
        _ffn_kernel,
        grid_spec=pltpu.PrefetchScalarGridSpec(
            num_scalar_prefetch=6,
            grid=(n_items,),
            in_specs=[pl.BlockSpec((rows, d), lambda i, blk, e, *_: (blk[i], 0)),
                      pl.BlockSpec((None, 1, d, de), lambda i, blk, e, *_: (layer, e[i], 0, 0)),
                      pl.BlockSpec((None, 1, d, de), lambda i, blk, e, *_: (layer, e[i], 0, 0)),
                      pl.BlockSpec((None, 1, de, d), lambda i, blk, e, *_: (layer, e[i], 0, 0))],
            out_specs=pl.BlockSpec((rows, d), lambda i, blk, e, *_: (blk[i], 0)),
            scratch_shapes=[pltpu.VMEM((d, de), BF16), pltpu.VMEM((d, de), BF16), pltpu.VMEM((de, d), BF16)]),
        out_shape=jax.ShapeDtypeStruct((n_rows, d), F32),
        compiler_params=_cparams(("arbitrary",)),
        name="moe_ffn",
    )(*items, xs, wg, wu, wd)


def _combine_kernel(dest_ref, ys_ref, w_ref, h_ref, x_ref, g_ref, sg_ref, su_ref, sd_ref, o_ref, buf, sem):
    tm = h_ref.shape[0]

    def copy(t, k, row):
        return pltpu.make_async_copy(ys_ref.at[pl.ds(row, 1), :], buf.at[k, pl.ds(t, 1), :], sem)

    def issue(t, _):
        for k in range(TOP_K):
            copy(t, k, dest_ref[0, k, t]).start()
        return 0

    def drain(t, _):
        for k in range(TOP_K):
            copy(0, 0, 0).wait()
        return 0

    lax.fori_loop(0, tm, issue, 0, unroll=4)
    hb = h_ref[...].astype(BF16)
    y = _dot((_silu(_dot(hb, sg_ref[...])) * _dot(hb, su_ref[...])).astype(BF16), sd_ref[...])
    lax.fori_loop(0, tm, drain, 0, unroll=4)
    w = w_ref[...]
    for k in range(TOP_K):
        y = y + w[:, k:k + 1] * buf[k]
    o_ref[...] = x_ref[...] + g_ref[0] * y


def moe_combine(ys, dest, w, h2, x2, gate, sg, su, sd, seq, tm=MOE_TM):
    nt, d = h2.shape
    ds_ = sg.shape[-1]
    per_b = seq // tm
    tile = lambda: pl.BlockSpec((tm, d), lambda i: (i, 0))
    return pl.pallas_call(
        _combine_kernel,
        grid=(nt // tm,),
        in_specs=[pl.BlockSpec((1, 8, tm), lambda i: (i, 0, 0), memory_space=pltpu.SMEM),
                  pl.BlockSpec(memory_space=pl.ANY),
                  pl.BlockSpec((tm, LANES), lambda i: (i, 0)), tile(), tile(),
                  pl.BlockSpec((1, 1, d), lambda i: (i // per_b, 0, 0)),
                  _full((d, ds_)), _full((d, ds_)), _full((ds_, d))],
        out_specs=tile(),
        scratch_shapes=[pltpu.VMEM((TOP_K, tm, d), F32), pltpu.SemaphoreType.DMA(())],
        out_shape=jax.ShapeDtypeStruct((nt, d), F32),
        compiler_params=_cparams(("arbitrary",)),
        name="moe_combine",
    )(dest, ys, w, h2, x2, gate, sg.astype(BF16), su.astype(BF16), sd.astype(BF16))


def moe_layer(x, g_norm, sc, sh, gate, router, e_bias, wg, wu, wd, layer, sg, su, sd):
    b, t, d = x.shape
    nt = b * t
    h = ln_mod(x, g_norm, sc, sh, F32)
    h2 = h.reshape(nt, d)
    eid, rank, w, counts = moe_route(h2, router, e_bias)
    starts, items = _ffn_items(counts.reshape(-1).astype(jnp.int32), nt * TOP_K, MOE_ROWS)
    hit = eid[..., None] == jnp.arange(N_EXPERTS, dtype=jnp.int32)
    dest = jnp.sum(jnp.where(hit, starts.astype(jnp.int32), 0), axis=-1) + rank
    xs = moe_dispatch(h2, dest)
    ys = moe_ffn_sorted(xs, items, wg, wu, wd, layer)
    out = moe_combine(ys, dest, w, h2, x.reshape(nt, d), gate.reshape(b, 1, d), sg, su, sd, t)
    return out.reshape(b, t, d)


def kernel(x, c, ada_w, ada_b, norm_mix, norm_ffn, rel_bias, ev_w_in, ev_w_out, fox_fb, fox_qn, fox_kn, gdn_conv, gdn_a_log, gdn_dt_bias, gdn_on, od_w_in, od_w_out, nsa_qn, nsa_kn, nsa_pos, nsa_cmp_w1, nsa_cmp_w2, gla_wg_up, gla_bg, gla_on, moe_router, moe_bias, moe_wg, moe_wu, moe_wd, sh_wg, sh_wu, sh_wd):
    d = x.shape[-1]
    depth = ada_w.shape[0]
    mod = adaln(c, ada_w, ada_b)
    for layer in range(depth):
        sh1, sc1, g1, sh2, sc2, g2 = [mod[layer, :, i * d:(i + 1) * d] for i in range(6)]
        h = ln_mod(x, norm_mix[layer], sc1, sh1, BF16)
        j = layer // 2
        if layer % 2 == 0:
            y1, y2 = even_mixer(h, ev_w_in[j], fox_fb[j], fox_qn[j], fox_kn[j], gdn_conv[j], gdn_a_log[j],
                                gdn_dt_bias[j], gdn_on[j])
            w_out = ev_w_out[j]
        else:
            y1, y2 = odd_mixer(h, od_w_in[j], nsa_qn[j], nsa_kn[j], nsa_pos[j], nsa_cmp_w1[j], nsa_cmp_w2[j],
                               gla_wg_up[j], gla_bg[j], gla_on[j], rel_bias)
            w_out = od_w_out[j]
        x = out_proj(y1, y2, w_out, x, g1)
        x = moe_layer(x, norm_ffn[layer], sc2, sh2, g2, moe_router[layer], moe_bias[layer], moe_wg, moe_wu, moe_wd,
                      layer, sh_wg[layer], sh_wu[layer], sh_wd[layer])
    return x
```

```python
import functools
import math

import numpy as np
import jax
import jax.numpy as jnp
from jax import lax
from jax.experimental import pallas as pl
from jax.experimental.pallas import tpu as pltpu

F32 = jnp.float32
BF16 = jnp.bfloat16
HI = lax.Precision.HIGHEST

EPS = 1e-6
LOG2E = math.log2(math.e)
NEG = -1e30
FORCE_SCORE = 1e9

FOX_HEADS, FOX_DH = 8, 64
GDN_HEADS, GDN_DH, GDN_CONV, GDN_CHUNK = 4, 128, 4, 64
NSA_HEADS, NSA_KV_HEADS, NSA_DH = 8, 2, 64
NSA_GROUP = NSA_HEADS // NSA_KV_HEADS
CMP_LEN, CMP_STRIDE, CMP_HIDDEN = 32, 16, 256
SLC_LEN, SLC_TOPK, WINDOW = 64, 16, 512
GLA_HEADS, GLA_DK, GLA_DV, GLA_GATE_RANK, GLA_TAU, GLA_CHUNK = 4, 64, 128, 16, 16.0, 64
REL_BUCKETS, REL_MAX_DIST = 32, 128
N_EXPERTS, TOP_K, D_EXPERT, D_SHARED = 64, 6, 256, 256
N_GROUPS, TOPK_GROUPS, ROUTE_SCALE = 8, 4, 2.5

FOX_W = FOX_HEADS * FOX_DH
GDN_W = GDN_HEADS * GDN_DH
NSA_W = NSA_HEADS * NSA_DH
NSA_KV_W = NSA_KV_HEADS * NSA_DH
GLA_KW = GLA_HEADS * GLA_DK
GLA_W = GLA_HEADS * GLA_DV
EV_SIZES = (FOX_W, FOX_W, FOX_W, FOX_HEADS, 3 * GDN_W, GDN_HEADS, GDN_HEADS, GDN_W)
OD_SIZES = (NSA_W,) + (NSA_KV_W,) * 6 + (3 * NSA_HEADS, GLA_KW, GLA_KW, GLA_W, GLA_GATE_RANK, GLA_W)

LANES = 128
SUP_BLOCKS = 64
VMEM_LIMIT = 56 * 1024 * 1024


def _cparams(sem, flags=None):
    return pltpu.CompilerParams(dimension_semantics=sem, vmem_limit_bytes=VMEM_LIMIT, flags=flags)


def _full(shape):
    n = len(shape)
    return pl.BlockSpec(shape, lambda *_: (0,) * n)


def _dot(a, b):
    return jnp.dot(a, b, preferred_element_type=F32)


def _dot_hi(a, b):
    return jnp.dot(a, b, precision=HI, preferred_element_type=F32)


def _dot_nt(a, b, precision=None):
    return lax.dot_general(a, b, (((1,), (1,)), ((), ())), precision=precision, preferred_element_type=F32)


def _dot_tn(a, b, precision=None):
    return lax.dot_general(a, b, (((0,), (0,)), ((), ())), precision=precision, preferred_element_type=F32)


def _sigmoid(x):
    return 1.0 / (1.0 + jnp.exp(-x))


def _silu(x):
    return x * _sigmoid(x)


def _softplus(x):
    return jnp.maximum(x, 0.0) + jnp.log(1.0 + jnp.exp(-jnp.abs(x)))


def _log_sigmoid(x):
    return -_softplus(-x)


def _adaln_kernel(c_ref, w_ref, b_ref, o_ref):
    c = c_ref[...]
    o_ref[0] = _dot_hi(_silu(c), w_ref[0]) + b_ref[0]


def adaln(c, ada_w, ada_b):
    depth, d, n = ada_w.shape
    b = c.shape[0]
    cp = jnp.zeros((8, d), F32).at[:b].set(c)
    tn = 1536
    out = pl.pallas_call(
        _adaln_kernel,
        grid=(depth, n // tn),
        in_specs=[_full((8, d)),
                  pl.BlockSpec((1, d, tn), lambda l, j: (l, 0, j)),
                  pl.BlockSpec((1, 1, tn), lambda l, j: (l, 0, j))],
        out_specs=pl.BlockSpec((1, 8, tn), lambda l, j: (l, 0, j)),
        out_shape=jax.ShapeDtypeStruct((depth, 8, n), F32),
        compiler_params=_cparams(("arbitrary", "arbitrary")),
        name="adaln",
    )(cp, ada_w, ada_b.reshape(depth, 1, n))
    return out[:, :b]


def _ln_kernel(x_ref, g_ref, sc_ref, sh_ref, o_ref):
    x = x_ref[0]
    y = x * lax.rsqrt(jnp.mean(x * x, axis=-1, keepdims=True) + EPS) * g_ref[...]
    o_ref[0] = (y * (1.0 + sc_ref[0]) + sh_ref[0]).astype(o_ref.dtype)


def ln_mod(x, g, sc, sh, out_dtype, tm=512):
    b, t, d = x.shape
    return pl.pallas_call(
        _ln_kernel,
        grid=(b, t // tm),
        in_specs=[pl.BlockSpec((1, tm, d), lambda i, j: (i, j, 0)),
                  _full((1, d)),
                  pl.BlockSpec((1, 1, d), lambda i, j: (i, 0, 0)),
                  pl.BlockSpec((1, 1, d), lambda i, j: (i, 0, 0))],
        out_specs=pl.BlockSpec((1, tm, d), lambda i, j: (i, j, 0)),
        out_shape=jax.ShapeDtypeStruct((b, t, d), out_dtype),
        compiler_params=_cparams(("arbitrary", "arbitrary")),
        name="ln_mod",
    )(x, g.reshape(1, d), sc.reshape(b, 1, d), sh.reshape(b, 1, d))


def proj(h, w, out_dtype, epilogue=None, extras=(), tm=512, name="proj"):
    b, t, d = h.shape
    n = w.shape[1]

    def kern(h_ref, w_ref, *rest):
        o_ref = rest[-1]
        y = _dot(h_ref[0], w_ref[...])
        if epilogue is not None:
            y = epilogue(y, *[e[...] for e in rest[:-1]])
        o_ref[0] = y.astype(out_dtype)

    return pl.pallas_call(
        kern,
        grid=(b, t // tm),
        in_specs=[pl.BlockSpec((1, tm, d), lambda i, j: (i, j, 0)), _full((d, n))]
                 + [_full(e.shape) for e in extras],
        out_specs=pl.BlockSpec((1, tm, n), lambda i, j: (i, j, 0)),
        out_shape=jax.ShapeDtypeStruct((b, t, n), out_dtype),
        compiler_params=_cparams(("arbitrary", "arbitrary")),
        name=name,
    )(h, w, *extras)


def proj_multi(h, groups, tm=512, name="proj"):
    b, t, d = h.shape
    widths = [g[0].shape[1] for g in groups]
    starts = np.cumsum([0] + widths)
    w_cat = jnp.concatenate([g[0] for g in groups], axis=1).astype(BF16)
    extras = [e for g in groups for e in g[3]]
    n_ex = [len(g[3]) for g in groups]
    n_out = len(groups)

    def kern(h_ref, w_ref, *rest):
        ex_refs = rest[:len(extras)]
        o_refs = rest[len(extras):]
        y = _dot(h_ref[0], w_ref[...])
        pos = 0
        for gi, (_, out_dtype, epilogue, _) in enumerate(groups):
            yg = y[:, starts[gi]:starts[gi + 1]]
            if epilogue is not None:
                yg = epilogue(yg, *[e[...] for e in ex_refs[pos:pos + n_ex[gi]]])
            pos += n_ex[gi]
            o_refs[gi][0] = yg.astype(out_dtype)

    return pl.pallas_call(
        kern,
        grid=(b, t // tm),
        in_specs=[pl.BlockSpec((1, tm, d), lambda i, j: (i, j, 0)), _full((d, int(starts[-1])))]
                 + [_full(e.shape) for e in extras],
        out_specs=[pl.BlockSpec((1, tm, n), lambda i, j: (i, j, 0)) for n in widths],
        out_shape=[jax.ShapeDtypeStruct((b, t, n), g[1]) for n, g in zip(widths, groups)],
        compiler_params=_cparams(("arbitrary", "arbitrary")),
        name=name,
    )(h, w_cat, *extras)


def _head_norm_epilogue(dh):
    inv = 1.0 / dh

    def ep(y, bd, gain):
        ssq = _dot((y * y).astype(BF16), bd)
        return y * lax.rsqrt(ssq * inv + EPS) * gain

    return ep


def _block_diag_ones(n, dh):
    i = np.arange(n) // dh
    return jnp.asarray((i[:, None] == i[None, :]).astype(np.float32), dtype=BF16)


def _outproj_kernel(y1_ref, y2_ref, wa_ref, wb_ref, x_ref, g_ref, o_ref):
    y = _dot(y1_ref[0], wa_ref[...]) + _dot(y2_ref[0], wb_ref[...])
    o_ref[0] = x_ref[0] + g_ref[0] * y


def out_proj(y1, y2, w_out, x, gate, tm=512):
    b, t, d = x.shape
    n1, n2 = y1.shape[-1], y2.shape[-1]
    wa = w_out[:n1].astype(BF16)
    wb = w_out[n1:].astype(BF16)
    return pl.pallas_call(
        _outproj_kernel,
        grid=(b, t // tm),
        in_specs=[pl.BlockSpec((1, tm, n1), lambda i, j: (i, j, 0)),
                  pl.BlockSpec((1, tm, n2), lambda i, j: (i, j, 0)),
                  _full((n1, d)), _full((n2, d)),
                  pl.BlockSpec((1, tm, d), lambda i, j: (i, j, 0)),
                  pl.BlockSpec((1, 1, d), lambda i, j: (i, 0, 0))],
        out_specs=pl.BlockSpec((1, tm, d), lambda i, j: (i, j, 0)),
        out_shape=jax.ShapeDtypeStruct((b, t, d), F32),
        compiler_params=_cparams(("arbitrary", "arbitrary")),
        name="out_proj",
    )(y1, y2, wa, wb, x, gate.reshape(b, 1, d))


def _decay_kernel(s_ref, fb_ref, tril_ref, place_ref, o_ref, carry):
    @pl.when(pl.program_id(1) == 0)
    def _():
        carry[...] = jnp.zeros_like(carry)

    tm = s_ref.shape[1]
    lf = _log_sigmoid(s_ref[0] + fb_ref[...])
    cum = _dot_hi(tril_ref[...], lf) + carry[...]
    carry[...] = cum[tm - 1:tm, :]
    x = cum * LOG2E
    hi = x.astype(BF16)
    r1 = x - hi.astype(F32)
    mid = r1.astype(BF16)
    low = (r1 - mid.astype(F32)).astype(BF16)
    o_ref[0] = _dot(jnp.concatenate([hi, mid, low], axis=1), place_ref[...]).astype(o_ref.dtype)


def fox_decay(small, fox_fb, tm=512):
    b, t, _ = small.shape
    fb = jnp.zeros((1, LANES), F32).at[0, :FOX_HEADS].set(fox_fb)
    tril = jnp.asarray(np.tril(np.ones((tm, tm), np.float32)))
    place = np.zeros((3 * LANES, FOX_W), np.float32)
    for h in range(FOX_HEADS):
        for j in range(3):
            place[j * LANES + h, (h // 2) * LANES + (FOX_DH if h % 2 == 0 else 0) + j] = 1.0
    return pl.pallas_call(
        _decay_kernel,
        grid=(b, t // tm),
        in_specs=[pl.BlockSpec((1, tm, LANES), lambda i, j: (i, j, 0)), _full((1, LANES)), _full((tm, tm)),
                  _full((3 * LANES, FOX_W))],
        out_specs=pl.BlockSpec((1, tm, FOX_W), lambda i, j: (i, j, 0)),
        out_shape=jax.ShapeDtypeStruct((b, t, FOX_W), BF16),
        scratch_shapes=[pltpu.VMEM((1, LANES), F32)],
        compiler_params=_cparams(("arbitrary", "arbitrary")),
        name="fox_decay",
    )(small, fb, tril, jnp.asarray(place, dtype=BF16))


def _fox_kernel(q_ref, k_ref, v_ref, f_ref, o_ref, *, tq, wide):
    i = pl.program_id(2)
    lane = lax.broadcasted_iota(jnp.int32, (1, LANES), 1)
    lo = lane < FOX_DH
    coef = jnp.where((lane & (FOX_DH - 1)) < 3, -1.0, 0.0).astype(BF16)
    q = q_ref[0]
    qs = (jnp.where(lo, q, coef), jnp.where(lo, coef, q))
    causal = (lax.broadcasted_iota(jnp.int32, (tq, tq), 1) <= lax.broadcasted_iota(jnp.int32, (tq, tq), 0))
    one = jnp.ones((1, LANES), BF16)

    def step(j, carry, tw, diag=False):
        start = pl.multiple_of(j * tw, tw)
        kt = k_ref[0, pl.ds(start, tw), :]
        ft = f_ref[0, pl.ds(start, tw), :]
        vt = v_ref[0, pl.ds(start, tw), :]
        s_pair = (_dot_nt(qs[0], jnp.where(lo, kt, ft)), _dot_nt(qs[1], jnp.where(lo, ft, kt)))
        vs = (jnp.where(lo, vt, one), jnp.where(lo, one, vt))
        new = []
        for hh in range(2):
            m, acc = carry[hh]
            s = s_pair[hh]
            if diag:
                s = jnp.where(causal, s, NEG)
            m_new = jnp.maximum(m, jnp.max(s, axis=1, keepdims=True))
            p = jnp.exp2(s - m_new)
            acc = jnp.exp2(m - m_new) * acc + _dot(p.astype(BF16), vs[hh])
            new.append((m_new, acc))
        return tuple(new)

    carry = tuple((jnp.full((tq, 1), NEG, F32), jnp.zeros((tq, LANES), F32)) for _ in range(2))
    n_wide = i // wide
    carry = lax.fori_loop(0, n_wide, lambda j, c: step(j, c, wide * tq), carry)
    done = n_wide * wide
    part = wide // 2
    while part >= 1:
        carry = lax.cond((i & part) != 0, lambda c, d=done, w=part: step(d // w, c, w * tq), lambda c: c, carry)
        done = done + (i & part)
        part //= 2
    carry = step(i, carry, tq, diag=True)
    acc = jnp.where(lo, carry[0][1], carry[1][1])
    den = jnp.where(lo, carry[1][1], carry[0][1])
    o_ref[0] = (acc / pltpu.roll(den, FOX_DH, 1)).astype(o_ref.dtype)


def fox_attention(q, k, v, feat, tq=512, wide=4):
    b, t, w = q.shape
    npair = w // LANES
    nt = t // tq
    whole = lambda: pl.BlockSpec((1, t, LANES), lambda bi, p, i: (bi, 0, p))
    return pl.pallas_call(
        functools.partial(_fox_kernel, tq=tq, wide=wide),
        grid=(b, npair, nt),
        in_specs=[pl.BlockSpec((1, tq, LANES), lambda bi, p, i: (bi, i, p)), whole(), whole(), whole()],
        out_specs=pl.BlockSpec((1, tq, LANES), lambda bi, p, i: (bi, i, p)),
        out_shape=jax.ShapeDtypeStruct((b, t, w), BF16),
        compiler_params=_cparams(("arbitrary", "arbitrary", "arbitrary")),
        name="fox_attn",
    )(q, k, v, feat)


def _mm(a, b):
    return _dot(a.astype(BF16), b.astype(BF16))


def _mm3(a, b):
    ah = a.astype(BF16)
    bh = b.astype(BF16)
    al = (a - ah.astype(F32)).astype(BF16)
    bl = (b - bh.astype(F32)).astype(BF16)
    return _dot(jnp.concatenate([ah, ah, al], axis=1), jnp.concatenate([bh, bl, bh], axis=0))


def _tril_solve(a, rhs, ri, ci):
    n = a[0].shape[0]
    both = lambda f, x, y: [f(p, q) for p, q in zip(x, y)]
    eye = (ri == ci).astype(F32)
    same = lambda b: (lax.shift_right_logical(ri, int(math.log2(b)))
                      == lax.shift_right_logical(ci, int(math.log2(b))))
    base = 16
    d = [jnp.where(same(base), p, 0.0) for p in a]
    d2 = both(_mm, d, d)
    d4 = both(_mm, d2, d2)
    r1 = [eye - p + p2 - t for p, p2, t in zip(d, d2, both(_mm, d, d2))]
    d8 = both(_mm, d4, d4)
    r2 = [eye + p4 + p8 + t for p4, p8, t in zip(d4, d8, both(_mm, d4, d8))]
    t = both(_mm, r1, r2)
    b = base
    while b < n:
        join = same(2 * b) & jnp.logical_not(same(b))
        low = [jnp.where(join, p, 0.0) for p in a]
        t = [p - q for p, q in zip(t, both(_mm, both(_mm, t, low), t))]
        b *= 2
    return both(_mm3, t, rhs)


GDN_BLOCK = 128


def _gdn_kernel(x_ref, sm_ref, z_ref, cw_ref, ega_ref, egb_ref, alog_ref, dtb_ref, on_ref, tril_ref,
                o_ref, s_scr, prev_scr, *, tc):
    c = GDN_BLOCK
    w = GDN_W

    @pl.when(pl.program_id(1) == 0)
    def _():
        s_scr[...] = jnp.zeros_like(s_scr)
        prev_scr[...] = jnp.zeros_like(prev_scr)

    x = x_ref[0]
    prev = prev_scr[...]
    row8 = lax.broadcasted_iota(jnp.int32, (8, 1), 0)
    acc = x * cw_ref[GDN_CONV - 1:GDN_CONV, :]
    for s in range(1, GDN_CONV):
        rolled = pltpu.roll(x, s, 0)
        head = jnp.where(row8 < s, pltpu.roll(prev, s, 0), rolled[0:8])
        shifted = jnp.concatenate([head, rolled[8:]], axis=0)
        acc = acc + shifted * cw_ref[GDN_CONV - 1 - s:GDN_CONV - s, :]
    prev_scr[...] = x[tc - 8:tc]
    xc = _silu(acc)

    sm = sm_ref[0]
    g_raw = _dot_hi(sm, ega_ref[...])
    b_raw = _dot_hi(sm, egb_ref[...])
    g = -jnp.exp(alog_ref[...]) * _softplus(g_raw + dtb_ref[...])
    beta_all = _sigmoid(b_raw)
    gc_all = _dot_hi(tril_ref[...], g)

    ri = lax.broadcasted_iota(jnp.int32, (c, c), 0)
    ci = lax.broadcasted_iota(jnp.int32, (c, c), 1)
    causal = ci <= ri
    strict = ci < ri

    nblk = tc // c
    a_l, attn_l, rhs_l, qd_l, kd_l, egl_l = [], [], [], [], [], []
    for h in range(GDN_HEADS):
        ln = slice(h * GDN_DH, (h + 1) * GDN_DH)
        qh = xc[:, h * GDN_DH:(h + 1) * GDN_DH]
        kh = xc[:, w + h * GDN_DH:w + (h + 1) * GDN_DH]
        qh = qh * lax.rsqrt(jnp.sum(qh * qh, axis=-1, keepdims=True) + EPS) * (GDN_DH ** -0.5)
        kh = kh * lax.rsqrt(jnp.sum(kh * kh, axis=-1, keepdims=True) + EPS)
        vh = xc[:, 2 * w + h * GDN_DH:2 * w + (h + 1) * GDN_DH]
        gch = gc_all[:, ln]
        gct = gch.T
        egc = jnp.exp(gch)
        bh = beta_all[:, ln]
        for n in range(nblk):
            sl = slice(n * c, (n + 1) * c)
            q, k, v, gc, be = qh[sl], kh[sl], vh[sl], gch[sl], bh[sl]
            decay = jnp.exp(jnp.where(causal, gc - gct[:, sl], NEG))
            kb = k * be
            kk = _dot_nt(jnp.concatenate([kb, q], axis=0).astype(BF16), k.astype(BF16))
            a_l.append(jnp.where(strict, kk[:c] * decay, 0.0))
            attn_l.append(jnp.where(causal, kk[c:] * decay, 0.0))
            rhs_l.append(jnp.concatenate([v * be, kb * egc[sl]], axis=1))
            gl = gc[c - 1:c, :]
            qd_l.append(q * egc[sl])
            kd_l.append(k * jnp.exp(gl - gc))
            egl_l.append(jnp.exp(gl))
    uw_l = _tril_solve(a_l, rhs_l, ri, ci)

    states = [s_scr[h] for h in range(GDN_HEADS)]
    for n in range(nblk):
        sl = slice(n * c, (n + 1) * c)
        idx = [h * nblk + n for h in range(GDN_HEADS)]
        ws = [_mm(jnp.concatenate([uw_l[i][:, GDN_DH:], qd_l[i]], axis=0), states[h])
              for h, i in enumerate(idx)]
        v_new = [uw_l[i][:, :GDN_DH] - ws[h][:c] for h, i in enumerate(idx)]
        o = [ws[h][c:] + _mm(attn_l[i], v_new[h]) for h, i in enumerate(idx)]
        states = [states[h] * egl_l[i] + _dot_tn(kd_l[i].astype(BF16), v_new[h].astype(BF16))
                  for h, i in enumerate(idx)]
        for h in range(GDN_HEADS):
            ln = slice(h * GDN_DH, (h + 1) * GDN_DH)
            on = o[h] * lax.rsqrt(jnp.mean(o[h] * o[h], axis=-1, keepdims=True) + EPS) * on_ref[...]
            o_ref[0, sl, ln] = (on * _silu(z_ref[0, sl, ln])).astype(o_ref.dtype)
    for h in range(GDN_HEADS):
        s_scr[h] = states[h]


def gated_delta_net(x, small, z, conv_w, a_log, dt_bias, on_gain, tc=512):
    b, t, _ = x.shape
    w = GDN_W
    ega = np.zeros((LANES, w), np.float32)
    egb = np.zeros((LANES, w), np.float32)
    for h in range(GDN_HEADS):
        ega[FOX_HEADS + h, h * GDN_DH:(h + 1) * GDN_DH] = 1.0
        egb[FOX_HEADS + GDN_HEADS + h, h * GDN_DH:(h + 1) * GDN_DH] = 1.0
    alog = jnp.repeat(a_log, GDN_DH).reshape(1, w)
    dtb = jnp.repeat(dt_bias, GDN_DH).reshape(1, w)
    idx = np.arange(tc)
    tril = ((idx[:, None] >= idx[None, :]) & (idx[:, None] // GDN_BLOCK == idx[None, :] // GDN_BLOCK))
    row = lambda n: pl.BlockSpec((1, tc, n), lambda i, j: (i, j, 0))
    return pl.pallas_call(
        functools.partial(_gdn_kernel, tc=tc),
        grid=(b, t // tc),
        in_specs=[row(3 * w), row(LANES), row(w), _full((GDN_CONV, 3 * w)), _full((LANES, w)), _full((LANES, w)),
                  _full((1, w)), _full((1, w)), _full((1, GDN_DH)), _full((tc, tc))],
        out_specs=row(w),
        out_shape=jax.ShapeDtypeStruct((b, t, w), BF16),
        scratch_shapes=[pltpu.VMEM((GDN_HEADS, GDN_DH, GDN_DH), F32), pltpu.VMEM((8, 3 * w), F32)],
        compiler_params=_cparams(("arbitrary", "arbitrary")),
        name="gdn",
    )(x, small, z, conv_w, jnp.asarray(ega), jnp.asarray(egb), alog, dtb, on_gain.reshape(1, GDN_DH),
      jnp.asarray(tril.astype(np.float32)))


def even_mixer(h, w_in, fox_fb, fox_qn, fox_kn, gdn_conv, gdn_a_log, gdn_dt_bias, gdn_on):
    cuts = np.cumsum((0,) + EV_SIZES)
    col = lambda i: w_in[:, cuts[i]:cuts[i + 1]]
    wb = lambda a: a.astype(BF16)
    bd = _block_diag_ones(FOX_W, FOX_DH)
    ep = _head_norm_epilogue(FOX_DH)
    qg = (jnp.tile(fox_qn, FOX_HEADS) * (FOX_DH ** -0.5 * LOG2E)).reshape(1, FOX_W)
    kg = jnp.tile(fox_kn, FOX_HEADS).reshape(1, FOX_W)
    w_small = jnp.zeros((w_in.shape[0], LANES), F32)
    w_small = w_small.at[:, 0:8].set(col(3)).at[:, 8:12].set(col(5)).at[:, 12:16].set(col(6))
    fq, fk, fv, small, gqkv, gz = proj_multi(
        h, [(col(0), BF16, ep, (bd, qg)), (col(1), BF16, ep, (bd, kg)), (col(2), BF16, None, ()),
            (w_small, F32, None, ()), (col(4), F32, None, ()), (col(7), F32, None, ())], name="proj_even")
    feat = fox_decay(small, fox_fb)
    o_fox = fox_attention(fq, fk, fv, feat)
    o_gdn = gated_delta_net(gqkv, small, gz, gdn_conv, gdn_a_log, gdn_dt_bias, gdn_on)
    return o_fox, o_gdn


def _t5_bucket_np(dist):
    n = np.maximum(dist, 0)
    exact = REL_BUCKETS // 2
    nf = np.maximum(n, 1).astype(np.float32)
    large = exact + (np.log(nf / np.float32(exact)) / np.float32(math.log(REL_MAX_DIST / exact))
                     * np.float32(REL_BUCKETS - exact)).astype(np.int32)
    large = np.minimum(large, REL_BUCKETS - 1)
    return np.where(n < exact, n, large)


def _bias_kernel(tbl_ref, bucket_ref, o_ref):
    h = pl.program_id(0)
    bucket = bucket_ref[...]
    acc = jnp.full(bucket.shape, NEG, F32)
    for b in range(REL_BUCKETS):
        acc = jnp.where(bucket == b, tbl_ref[b, h], acc)
    o_ref[0] = acc


def _bias_table(rel_bias, dist, valid):
    shifted = (rel_bias - rel_bias[REL_BUCKETS - 1:REL_BUCKETS]) * LOG2E
    bucket = np.where(valid, _t5_bucket_np(dist), -1).astype(np.int32)
    rows, cols = int(np.prod(bucket.shape[:-1])), bucket.shape[-1]
    nh = rel_bias.shape[1]
    tb = pl.pallas_call(
        _bias_kernel,
        grid=(nh,),
        in_specs=[pl.BlockSpec(memory_space=pltpu.SMEM), _full((rows, cols))],
        out_specs=pl.BlockSpec((1, rows, cols), lambda h: (h, 0, 0)),
        out_shape=jax.ShapeDtypeStruct((nh, rows, cols), F32),
        compiler_params=_cparams(("arbitrary",)),
        name="t5_bias",
    )(shifted, jnp.asarray(bucket.reshape(rows, cols)))
    return tb.reshape((nh,) + bucket.shape)


def _cmp_kernel(r_ref, pos_ref, w1_ref, w2_ref, kn_ref, o_ref):
    m = r_ref.shape[3]
    half = r_ref.shape[4]
    r = r_ref[0, 0, 0].astype(BF16)
    a = _dot(r, w1_ref[0, :half, :])
    bm = _dot(r, w1_ref[0, half:, :])
    c = _dot(pos_ref[0].astype(BF16), w1_ref[0])
    hid = a + pltpu.roll(bm, m - 1, 0) + c[0:1, :]
    out = _dot(_silu(hid).astype(BF16), w2_ref[0])
    normed = out * lax.rsqrt(jnp.mean(out * out, axis=-1, keepdims=True) + EPS) * kn_ref[...]
    o_ref[0, 0, 0] = jnp.where(pl.program_id(0) == 0, normed, out).astype(o_ref.dtype)


def nsa_compress(kcvc, pos, w1, w2, kn):
    b, t, _ = kcvc.shape
    m = t // CMP_STRIDE
    half = CMP_STRIDE * NSA_DH
    r = kcvc.reshape(b, m, CMP_STRIDE, 2, NSA_KV_HEADS, NSA_DH).transpose(3, 0, 4, 1, 2, 5).reshape(2, b, 2, m, half)
    posf = jnp.zeros((2, 8, 2 * half), F32).at[:, 0].set(pos.reshape(2, 2 * half))
    w2d = jnp.concatenate([w2, w2], axis=-1).astype(BF16)
    knd = jnp.tile(kn, 2).reshape(1, LANES)
    return pl.pallas_call(
        _cmp_kernel,
        grid=(2, b, NSA_KV_HEADS),
        in_specs=[pl.BlockSpec((1, 1, 1, m, half), lambda s, i, k: (s, i, k, 0, 0)),
                  pl.BlockSpec((1, 8, 2 * half), lambda s, i, k: (s, 0, 0)),
                  pl.BlockSpec((1, 2 * half, CMP_HIDDEN), lambda s, i, k: (s, 0, 0)),
                  pl.BlockSpec((1, CMP_HIDDEN, LANES), lambda s, i, k: (s, 0, 0)),
                  _full((1, LANES))],
        out_specs=pl.BlockSpec((1, 1, 1, m, LANES), lambda s, i, k: (s, i, k, 0, 0)),
        out_shape=jax.ShapeDtypeStruct((2, b, NSA_KV_HEADS, m, LANES), BF16),
        compiler_params=_cparams(("arbitrary", "arbitrary", "arbitrary")),
        name="nsa_compress",
    )(r, posf, w1.astype(BF16), w2d, knd)


def _dot_split(a, b):
    hi = a.astype(BF16)
    lo = (a - hi.astype(F32)).astype(BF16)
    return _dot(hi, b) + _dot(lo, b)


def _head_q(q_ref, hh, lo):
    blk = q_ref[0, :, (hh // 2) * LANES:(hh // 2 + 1) * LANES]
    keep = lo if hh % 2 == 0 else jnp.logical_not(lo)
    return jnp.where(keep, blk, jnp.zeros_like(blk))


def _pair_heads(o, lo):
    return jnp.concatenate([jnp.where(lo, o[0], o[1]), jnp.where(lo, o[2], o[3])], axis=1)


def _nsa_sel_kernel(q_ref, kc_ref, vc_ref, ov_ref, bt_ref, o_ref, sel_ref, *, tq, nband, n_slc):
    i = pl.program_id(2)
    ncp = kc_ref.shape[3]
    nsp = ov_ref.shape[1]
    per = tq // CMP_STRIDE
    var = jnp.minimum(i, 1)
    bs = pl.multiple_of(per * jnp.maximum(i - 1, 0), per)
    lo = lax.broadcasted_iota(jnp.int32, (1, LANES), 1) < NSA_DH
    kc = kc_ref[0, 0, 0]
    vc = vc_ref[0, 0, 0]
    kcb = kc_ref[0, 0, 0, pl.ds(bs, nband), :]
    vcb = vc_ref[0, 0, 0, pl.ds(bs, nband), :]
    far_ok = lax.broadcasted_iota(jnp.int32, (1, ncp), 1) < per * (i - 1)
    hs = range(NSA_GROUP)
    qh = [_head_q(q_ref, hh, lo) for hh in hs]
    s_far = [jnp.where(far_ok, _dot_nt(q, kc), NEG) for q in qh]
    s_band = [_dot_nt(qh[hh], kcb) + bt_ref[var, hh] for hh in hs]
    m = [jnp.maximum(jnp.max(a, axis=1, keepdims=True), jnp.max(b, axis=1, keepdims=True))
         for a, b in zip(s_far, s_band)]
    m = [jnp.where(x < 0.5 * NEG, 0.0, x) for x in m]
    p_far = [jnp.exp2(a - x) for a, x in zip(s_far, m)]
    p_band = [jnp.exp2(b - x) for b, x in zip(s_band, m)]
    l = [jnp.sum(a, axis=1, keepdims=True) + jnp.sum(b, axis=1, keepdims=True) for a, b in zip(p_far, p_band)]
    inv = [1.0 / jnp.where(x == 0.0, 1.0, x) for x in l]
    outs = [(_dot(a.astype(BF16), vc) + _dot(b.astype(BF16), vcb)) * x for a, b, x in zip(p_far, p_band, inv)]
    ps_far = p_far[0] * inv[0]
    ps_band = p_band[0] * inv[0]
    for hh in range(1, NSA_GROUP):
        ps_far = ps_far + p_far[hh] * inv[hh]
        ps_band = ps_band + p_band[hh] * inv[hh]
    o_ref[0] = _pair_heads(outs, lo).astype(o_ref.dtype)

    imp = _dot_split(ps_far, ov_ref[...]) + _dot_split(ps_band, ov_ref[pl.ds(bs, nband), :])
    blk = lax.broadcasted_iota(jnp.int32, (1, nsp), 1)
    blk_f = blk.astype(F32)
    qpos = i * tq + lax.broadcasted_iota(jnp.int32, (tq, 1), 0)
    cur = lax.shift_right_logical(qpos, int(math.log2(SLC_LEN)))
    forced = (blk == 0) | (blk == cur) | (blk == cur - 1)
    work = jnp.where(forced, FORCE_SCORE, jnp.where(blk <= cur, imp, NEG))
    work = jnp.where(blk < n_slc, work, -jnp.inf)
    ngrp = 4
    rg = tq // ngrp
    works = [work[r * rg:(r + 1) * rg] for r in range(ngrp)]
    sels = [jnp.zeros((rg, nsp), F32) for _ in range(ngrp)]
    for _ in range(min(SLC_TOPK, n_slc)):
        ms = [jnp.max(w, axis=1, keepdims=True) for w in works]
        firsts = [jnp.min(jnp.where(w == m, blk_f, float(nsp)), axis=1, keepdims=True) for w, m in zip(works, ms)]
        picks = [blk_f == f for f in firsts]
        sels = [jnp.where(p, 1.0, s) for p, s in zip(picks, sels)]
        works = [jnp.where(p, -jnp.inf, w) for p, w in zip(picks, works)]
    sel = jnp.concatenate(sels, axis=0)
    for sup in range(sel_ref.shape[2]):
        col = sel[:, (sup // 2) * LANES:(sup // 2 + 1) * LANES]
        if sup % 2 == 0:
            col = pltpu.roll(col, SUP_BLOCKS, 1)
        sel_ref[0, 0, sup] = jnp.where(lo, 0.0, jnp.where(col > 0.5, 0.0, NEG)).astype(sel_ref.dtype)


def nsa_select(q, cmp_kv, rel_bias, tq=512):
    b, t, _ = q.shape
    ncp = t // CMP_STRIDE
    n_cmp = ncp - 1
    n_slc = t // SLC_LEN
    nsp = max(LANES, n_slc)
    nsup = max(1, n_slc // SUP_BLOCKS)
    per = tq // CMP_STRIDE
    nband = 2 * per
    n = np.arange(ncp)[:, None]
    s = np.arange(nsp)[None, :]
    ov = ((CMP_STRIDE * n < SLC_LEN * s + SLC_LEN) & (CMP_STRIDE * n + CMP_LEN > SLC_LEN * s)
          & (n < n_cmp) & (s < n_slc)).astype(np.float32)
    qi = np.arange(tq)[:, None]
    nj = np.arange(nband)[None, :]
    end = CMP_STRIDE * nj + CMP_LEN - 1
    dist = np.stack([qi - end, tq + qi - end])
    bt = _bias_table(rel_bias, dist, dist >= 0)
    bt = bt.reshape(NSA_KV_HEADS, NSA_GROUP, 2, tq, nband).transpose(0, 2, 1, 3, 4)
    bt = bt.reshape(NSA_KV_HEADS * 2, NSA_GROUP, tq, nband)
    gw = NSA_GROUP * NSA_DH
    return pl.pallas_call(
        functools.partial(_nsa_sel_kernel, tq=tq, nband=nband, n_slc=n_slc),
        grid=(b, NSA_KV_HEADS, t // tq),
        in_specs=[pl.BlockSpec((1, tq, gw), lambda bi, k, i: (bi, i, k)),
                  pl.BlockSpec((1, 1, 1, ncp, LANES), lambda bi, k, i: (0, bi, k, 0, 0)),
                  pl.BlockSpec((1, 1, 1, ncp, LANES), lambda bi, k, i: (1, bi, k, 0, 0)),
                  _full((ncp, nsp)),
                  pl.BlockSpec((2, NSA_GROUP, tq, nband), lambda bi, k, i: (k, 0, 0, 0))],
        out_specs=[pl.BlockSpec((1, tq, gw), lambda bi, k, i: (bi, i, k)),
                   pl.BlockSpec((1, 1, nsup, tq, LANES), lambda bi, k, i: (bi, k, 0, i, 0))],
        out_shape=[jax.ShapeDtypeStruct((b, t, NSA_W), BF16),
                   jax.ShapeDtypeStruct((b, NSA_KV_HEADS, nsup, t, LANES), BF16)],
        compiler_params=_cparams(("arbitrary", "arbitrary", "arbitrary")),
        name="nsa_select",
    )(q, cmp_kv, cmp_kv, jnp.asarray(ov, dtype=BF16), bt)


def _nsa_main_kernel(q_ref, ks_ref, vs_ref, kw0_ref, kw1_ref, kw2_ref, vw0_ref, vw1_ref, vw2_ref, sel_ref, oh_ref,
                     ocmp_ref, gate_ref, tb_ref, wm_ref, eg_ref, o_ref, *, tq):
    i = pl.program_id(2)
    g = NSA_GROUP
    lo = lax.broadcasted_iota(jnp.int32, (1, LANES), 1) < NSA_DH

    def head_low(hh):
        blk = q_ref[0, :, (hh // 2) * LANES:(hh // 2 + 1) * LANES]
        if hh % 2 == 1:
            blk = pltpu.roll(blk.astype(F32), NSA_DH, 1).astype(BF16)
        return jnp.where(lo, blk, jnp.zeros_like(blk))

    qst = jnp.concatenate([head_low(hh) for hh in range(g)], axis=0)
    causal = (lax.broadcasted_iota(jnp.int32, (tq, tq), 1) <= lax.broadcasted_iota(jnp.int32, (tq, tq), 0))
    one = jnp.ones((1, LANES), BF16)
    sup_keys = SUP_BLOCKS * SLC_LEN
    all_masked = jnp.where(lo, 0.0, NEG).astype(BF16)

    def sel_step(jt, carry, near, tw):
        m, acc = carry
        key0 = jnp.maximum(jt, 0) * tw
        start = pl.multiple_of(key0, tw)
        within = pl.multiple_of(key0 & (sup_keys - 1), tw)
        mq = sel_ref[0, 0, lax.shift_right_logical(key0, int(math.log2(sup_keys)))]
        mq = jnp.where(jt >= 0, mq, all_masked)
        q_ext = qst + jnp.concatenate([mq] * g, axis=0)
        kt = jnp.where(lo, ks_ref[0, pl.ds(start, tw), :], oh_ref[pl.ds(within, tw), :])
        vt = jnp.where(lo, vs_ref[0, pl.ds(start, tw), :], one)
        s = _dot_nt(q_ext, kt)
        if near is not None:
            s = s.reshape(g, tq, tw) + tb_ref[:, :, near * tq:(near + 1) * tq]
            if near == 2:
                s = jnp.where(causal[None], s, NEG)
            s = s.reshape(g * tq, tw)
        m_new = jnp.maximum(m, jnp.max(s, axis=1, keepdims=True))
        p = jnp.exp2(s - m_new)
        acc = jnp.exp2(m - m_new) * acc + _dot(p.astype(BF16), vt)
        return m_new, acc

    n_far = jnp.maximum(i - 2, 0)
    n4 = lax.shift_right_logical(n_far, 2)
    carry = (jnp.full((g * tq, 1), NEG, F32), jnp.zeros((g * tq, LANES), F32))
    carry = lax.fori_loop(0, n4, lambda j, c: sel_step(j, c, None, 4 * tq), carry)
    carry = lax.cond((n_far & 2) == 2, lambda c: sel_step(2 * n4, c, None, 2 * tq), lambda c: c, carry)
    carry = lax.cond((n_far & 1) == 1, lambda c: sel_step(i - 3, c, None, tq), lambda c: c, carry)
    for near in range(3):
        carry = sel_step(i - 2 + near, carry, near, tq)
    o_slc = (carry[1] / pltpu.roll(carry[1], NSA_DH, 1)).reshape(g, tq, LANES)

    var = jnp.minimum(i, 2)
    kws = (kw0_ref, kw1_ref, kw2_ref)
    vws = (vw0_ref, vw1_ref, vw2_ref)
    sw = []
    for near in range(3):
        s = _dot_nt(qst, kws[near][0]).reshape(g, tq, tq)
        s = s + tb_ref[:, :, near * tq:(near + 1) * tq] + wm_ref[var, :, near * tq:(near + 1) * tq][None]
        sw.append(s.reshape(g * tq, tq))
    m = jnp.maximum(jnp.maximum(jnp.max(sw[0], axis=1, keepdims=True), jnp.max(sw[1], axis=1, keepdims=True)),
                    jnp.max(sw[2], axis=1, keepdims=True))
    acc = jnp.zeros((g * tq, LANES), F32)
    for near in range(3):
        p = jnp.exp2(sw[near] - m)
        acc = acc + _dot(p.astype(BF16), jnp.where(lo, vws[near][0], one))
    o_win = (acc / pltpu.roll(acc, NSA_DH, 1)).reshape(g, tq, LANES)

    pair = lambda o: jnp.concatenate([jnp.where(lo, o[0], pltpu.roll(o[1], NSA_DH, 1)),
                                      jnp.where(lo, o[2], pltpu.roll(o[3], NSA_DH, 1))], axis=1)
    gates = _dot_hi(_sigmoid(gate_ref[0]), eg_ref[0])
    gw = g * NSA_DH
    out = (gates[:, 0:gw] * ocmp_ref[0].astype(F32)
           + gates[:, gw:2 * gw] * pair(o_slc) + gates[:, 2 * gw:3 * gw] * pair(o_win))
    o_ref[0] = out.astype(o_ref.dtype)


def nsa_main(q, ksw, vsw, sel, o_cmp, small, rel_bias, tq=256):
    b, t, _ = q.shape
    nsup = sel.shape[2]
    g = NSA_GROUP
    gw = g * NSA_DH
    sup_keys = SUP_BLOCKS * SLC_LEN
    oh = np.zeros((sup_keys, LANES), np.float32)
    oh[np.arange(sup_keys), NSA_DH + np.arange(sup_keys) // SLC_LEN] = 1.0
    qi = np.arange(tq)[:, None]
    c = np.arange(3 * tq)[None, :]
    dist = qi + 2 * tq - c
    tb = _bias_table(rel_bias, dist, np.ones_like(dist, bool))
    wm = np.zeros((3, tq, 3 * tq), np.float32)
    for var in range(3):
        exists = c >= tq * (2 - var)
        wm[var] = np.where((dist >= 0) & (dist < WINDOW) & exists, 0.0, NEG)
    eg = np.zeros((NSA_KV_HEADS, LANES, 3 * gw), np.float32)
    for k in range(NSA_KV_HEADS):
        for hh in range(g):
            for br in range(3):
                eg[k, (k * g + hh) * 3 + br, br * gw + hh * NSA_DH:br * gw + (hh + 1) * NSA_DH] = 1.0
    near = lambda off, col: pl.BlockSpec(
        (1, tq, LANES), lambda bi, k, i: (bi, jnp.maximum(i - off, 0), col + k))
    return pl.pallas_call(
        functools.partial(_nsa_main_kernel, tq=tq),
        grid=(b, NSA_KV_HEADS, t // tq),
        in_specs=[pl.BlockSpec((1, tq, gw), lambda bi, k, i: (bi, i, k)),
                  pl.BlockSpec((1, t, LANES), lambda bi, k, i: (bi, 0, k)),
                  pl.BlockSpec((1, t, LANES), lambda bi, k, i: (bi, 0, k)),
                  near(2, 2), near(1, 2), near(0, 2), near(2, 2), near(1, 2), near(0, 2),
                  pl.BlockSpec((1, 1, nsup, tq, LANES), lambda bi, k, i: (bi, k, 0, i, 0)),
                  _full((sup_keys, LANES)),
                  pl.BlockSpec((1, tq, gw), lambda bi, k, i: (bi, i, k)),
                  pl.BlockSpec((1, tq, LANES), lambda bi, k, i: (bi, i, 0)),
                  pl.BlockSpec((g, tq, 3 * tq), lambda bi, k, i: (k, 0, 0)),
                  _full((3, tq, 3 * tq)),
                  pl.BlockSpec((1, LANES, 3 * gw), lambda bi, k, i: (k, 0, 0))],
        out_specs=pl.BlockSpec((1, tq, gw), lambda bi, k, i: (bi, i, k)),
        out_shape=jax.ShapeDtypeStruct((b, t, NSA_W), BF16),
        compiler_params=_cparams(("arbitrary", "arbitrary", "arbitrary")),
        name="nsa_main",
    )(q, ksw, vsw, ksw, ksw, ksw, vsw, vsw, vsw, sel, jnp.asarray(oh, dtype=BF16), o_cmp, small, tb,
      jnp.asarray(wm), jnp.asarray(eg))


def _gla_kernel(qk_ref, v_ref, r_ref, sm_ref, wg_ref, bg_ref, on_ref, tril_ref, o_ref, s_scr, *, tc):
    c = GLA_CHUNK

    @pl.when(pl.program_id(1) == 0)
    def _():
        s_scr[...] = jnp.zeros_like(s_scr)

    kw = GLA_KW
    log_a = _log_sigmoid(_dot_hi(sm_ref[0], wg_ref[...]) + bg_ref[...]) * (1.0 / GLA_TAU)
    gcum = _dot_hi(tril_ref[...], log_a)
    q = qk_ref[0, :, 0:kw] * (GLA_DK ** -0.5)
    k = qk_ref[0, :, kw:2 * kw]
    q_dec = (q * jnp.exp(gcum)).astype(BF16)
    k_inv = (k * jnp.exp(-gcum)).astype(BF16)
    ri = lax.broadcasted_iota(jnp.int32, (c, c), 0)
    ci = lax.broadcasted_iota(jnp.int32, (c, c), 1)
    causal = ci <= ri
    lo = lax.broadcasted_iota(jnp.int32, (1, LANES), 1) < GLA_DK
    nchunk = tc // c
    heads = range(GLA_HEADS)
    zero = jnp.zeros((c, LANES), BF16)

    qm, vb, kd, egl = {}, {}, {}, {}
    for n in range(nchunk):
        sl = slice(n * c, (n + 1) * c)
        gl = gcum[n * c + c - 1:n * c + c, :]
        kdn = (k[sl] * jnp.exp(gl - gcum[sl])).astype(BF16)
        for h in heads:
            pr = slice((h // 2) * LANES, (h // 2 + 1) * LANES)
            keep = lo if h % 2 == 0 else jnp.logical_not(lo)
            qm[n, h] = jnp.where(keep, q_dec[sl, pr], zero)
            vb[n, h] = v_ref[0, sl, h * GLA_DV:(h + 1) * GLA_DV].astype(BF16)
            kd[n, h] = kdn[:, pr]
            egl[n, h] = jnp.exp(gl[:, pr])
    idx = [(n, h) for n in range(nchunk) for h in heads]
    attn = {i: jnp.where(causal, _dot_nt(qm[i], k_inv[i[0] * c:(i[0] + 1) * c, (i[1] // 2) * LANES:
                                                       (i[1] // 2 + 1) * LANES]), 0.0).astype(BF16) for i in idx}
    o_intra = {i: _dot(attn[i], vb[i]) for i in idx}
    kv = {i: _dot_tn(vb[i], kd[i]) for i in idx}

    st = [s_scr[h] for h in heads]
    for n in range(nchunk):
        sl = slice(n * c, (n + 1) * c)
        o = [o_intra[n, h] + _dot_nt(qm[n, h], st[h].astype(BF16)) for h in heads]
        st = [st[h] * egl[n, h] + kv[n, h] for h in heads]
        for h in heads:
            on = o[h] * lax.rsqrt(jnp.mean(o[h] * o[h], axis=-1, keepdims=True) + EPS) * on_ref[...]
            o_ref[0, sl, h * GLA_DV:(h + 1) * GLA_DV] = (
                on * _silu(r_ref[0, sl, h * GLA_DV:(h + 1) * GLA_DV])).astype(o_ref.dtype)
    for h in heads:
        s_scr[h] = st[h]


def gated_linear_attention(qkvr, small, wg_up, bg, on_gain, tc=512):
    b, t, _ = qkvr.shape
    wg = jnp.zeros((LANES, GLA_KW), F32).at[3 * NSA_HEADS:3 * NSA_HEADS + GLA_GATE_RANK].set(wg_up)
    idx = np.arange(tc)
    tril = ((idx[:, None] >= idx[None, :]) & (idx[:, None] // GLA_CHUNK == idx[None, :] // GLA_CHUNK))
    return pl.pallas_call(
        functools.partial(_gla_kernel, tc=tc),
        grid=(b, t // tc),
        in_specs=[pl.BlockSpec((1, tc, 2 * GLA_KW), lambda i, j: (i, j, 0)),
                  pl.BlockSpec((1, tc, GLA_W), lambda i, j: (i, j, 1)),
                  pl.BlockSpec((1, tc, GLA_W), lambda i, j: (i, j, 2)),
                  pl.BlockSpec((1, tc, LANES), lambda i, j: (i, j, 0)),
                  _full((LANES, GLA_KW)), _full((1, GLA_KW)), _full((1, GLA_DV)), _full((tc, tc))],
        out_specs=pl.BlockSpec((1, tc, GLA_W), lambda i, j: (i, j, 0)),
        out_shape=jax.ShapeDtypeStruct((b, t, GLA_W), BF16),
        scratch_shapes=[pltpu.VMEM((GLA_HEADS, GLA_DV, LANES), F32)],
        compiler_params=_cparams(("arbitrary", "arbitrary")),
        name="gla",
    )(qkvr, qkvr, qkvr, small, wg, bg.reshape(1, GLA_KW), on_gain.reshape(1, GLA_DV),
      jnp.asarray(tril.astype(np.float32)))


def odd_mixer(h, w_in, nsa_qn, nsa_kn, nsa_pos, nsa_cmp_w1, nsa_cmp_w2, gla_wg_up, gla_bg, gla_on, rel_bias):
    cuts = np.cumsum((0,) + OD_SIZES)
    col = lambda i: w_in[:, cuts[i]:cuts[i + 1]]
    wb = lambda a: a.astype(BF16)
    dup = lambda a: jnp.concatenate([a[:, :NSA_DH], a[:, :NSA_DH], a[:, NSA_DH:], a[:, NSA_DH:]], axis=1)
    ep = _head_norm_epilogue(NSA_DH)
    bd = _block_diag_ones(NSA_W, NSA_DH)
    qg = (jnp.tile(nsa_qn, NSA_HEADS) * (NSA_DH ** -0.5 * LOG2E)).reshape(1, NSA_W)
    kg = jnp.tile(nsa_kn, NSA_HEADS).reshape(1, NSA_W)
    w_small = jnp.zeros((w_in.shape[0], LANES), F32)
    w_small = w_small.at[:, 0:24].set(col(7)).at[:, 24:40].set(col(11))
    nq, kcvc, ksw, vsw, small, qkvr = proj_multi(
        h, [(col(0), BF16, ep, (bd, qg)),
            (jnp.concatenate([col(1), col(2)], axis=1), F32, None, ()),
            (jnp.concatenate([dup(col(3)), dup(col(5))], axis=1), BF16, ep, (bd, kg)),
            (jnp.concatenate([dup(col(4)), dup(col(6))], axis=1), BF16, None, ()),
            (w_small, F32, None, ()),
            (jnp.concatenate([col(8), col(9), col(10), col(12)], axis=1), F32, None, ())], name="proj_odd")
    cmp_kv = nsa_compress(kcvc, nsa_pos, nsa_cmp_w1, nsa_cmp_w2, nsa_kn)
    o_cmp, sel = nsa_select(nq, cmp_kv, rel_bias)
    o_nsa = nsa_main(nq, ksw, vsw, sel, o_cmp, small, rel_bias)
    o_gla = gated_linear_attention(qkvr, small, gla_wg_up, gla_bg, gla_on)
    return o_nsa, o_gla


MOE_TM = 256
MOE_ROWS = 512


def _first_index(mask_val, idx, big, axis):
    return jnp.min(jnp.where(mask_val, idx, big), axis=axis, keepdims=True)


def _route_kernel(h_ref, rt_ref, b_ref, up_ref, eid_ref, rank_ref, w_ref, cnt_ref, run):
    tm = h_ref.shape[0]
    ne = N_EXPERTS
    gsz = ne // N_GROUPS

    @pl.when(pl.program_id(0) == 0)
    def _():
        run[...] = jnp.zeros_like(run)

    scores = _sigmoid(_dot_nt(rt_ref[...], h_ref[...], HI))
    biased = scores + b_ref[...]
    b3 = biased.reshape(N_GROUPS, gsz, tm)
    i3 = lax.broadcasted_iota(jnp.int32, (1, gsz, 1), 1).astype(F32)
    m1 = jnp.max(b3, axis=1, keepdims=True)
    f1 = _first_index(b3 == m1, i3, float(gsz), 1)
    m2 = jnp.max(jnp.where(i3 == f1, -jnp.inf, b3), axis=1, keepdims=True)
    gs = (m1 + m2).reshape(N_GROUPS, tm)
    gidx = lax.broadcasted_iota(jnp.int32, (N_GROUPS, 1), 0).astype(F32)
    gmask = jnp.zeros((N_GROUPS, tm), F32)
    for _ in range(TOPK_GROUPS):
        m = jnp.max(gs, axis=0, keepdims=True)
        pick = gidx == _first_index(gs == m, gidx, float(N_GROUPS), 0)
        gmask = jnp.where(pick, 1.0, gmask)
        gs = jnp.where(pick, -jnp.inf, gs)
    emask = jnp.broadcast_to(gmask.reshape(N_GROUPS, 1, tm), (N_GROUPS, gsz, tm)).reshape(ne, tm)
    work = jnp.where(emask > 0.5, biased, -jnp.inf)
    eidx = lax.broadcasted_iota(jnp.int32, (ne, 1), 0).astype(F32)
    picks, eids, ws = [], [], []
    for _ in range(TOP_K):
        m = jnp.max(work, axis=0, keepdims=True)
        first = _first_index(work == m, eidx, float(ne), 0)
        pick = eidx == first
        picks.append(pick)
        eids.append(first)
        ws.append(jnp.sum(jnp.where(pick, scores, 0.0), axis=0, keepdims=True))
        work = jnp.where(pick, -jnp.inf, work)
    wsum = ws[0]
    for k in range(1, TOP_K):
        wsum = wsum + ws[k]
    chosen = jnp.zeros((ne, tm), F32)
    for pick in picks:
        chosen = jnp.where(pick, 1.0, chosen)
    pos = run[...] + _dot(chosen.astype(BF16), up_ref[...])
    run[...] = run[...] + jnp.sum(chosen, axis=1, keepdims=True)
    cnt_ref[...] = run[...]
    row = lax.broadcasted_iota(jnp.int32, (8, 1), 0)
    eid_o = jnp.zeros((8, tm), F32)
    rank_o = jnp.zeros((8, tm), F32)
    w_o = jnp.zeros((LANES, tm), F32)
    rowl = lax.broadcasted_iota(jnp.int32, (LANES, 1), 0)
    for k in range(TOP_K):
        rk = jnp.sum(jnp.where(picks[k], pos, 0.0), axis=0, keepdims=True)
        eid_o = jnp.where(row == k, eids[k], eid_o)
        rank_o = jnp.where(row == k, rk, rank_o)
        w_o = jnp.where(rowl == k, ws[k] / wsum * ROUTE_SCALE, w_o)
    eid_ref[0] = eid_o.astype(jnp.int32)
    rank_ref[0] = rank_o.astype(jnp.int32)
    w_ref[...] = w_o.T


def moe_route(h2, router, e_bias, tm=MOE_TM):
    nt, d = h2.shape
    ne = N_EXPERTS
    up = jnp.asarray(np.triu(np.ones((tm, tm), np.float32), 1), dtype=BF16)
    nb = nt // tm
    return pl.pallas_call(
        _route_kernel,
        grid=(nb,),
        in_specs=[pl.BlockSpec((tm, d), lambda i: (i, 0)), _full((ne, d)), _full((ne, 1)), _full((tm, tm))],
        out_specs=[pl.BlockSpec((1, 8, tm), lambda i: (i, 0, 0)),
                   pl.BlockSpec((1, 8, tm), lambda i: (i, 0, 0)),
                   pl.BlockSpec((tm, LANES), lambda i: (i, 0)),
                   _full((ne, 1))],
        out_shape=[jax.ShapeDtypeStruct((nb, 8, tm), jnp.int32), jax.ShapeDtypeStruct((nb, 8, tm), jnp.int32),
                   jax.ShapeDtypeStruct((nt, LANES), F32), jax.ShapeDtypeStruct((ne, 1), F32)],
        scratch_shapes=[pltpu.VMEM((ne, 1), F32)],
        compiler_params=_cparams(("arbitrary",)),
        name="moe_route",
    )(h2, router.T, e_bias.reshape(ne, 1), up)


def _dispatch_kernel(dest_ref, h_ref, xs_ref, sem):
    tm = h_ref.shape[0]

    def copy(t, row):
        return pltpu.make_async_copy(h_ref.at[pl.ds(t, 1), :], xs_ref.at[pl.ds(row, 1), :], sem)

    def issue(t, _):
        for k in range(TOP_K):
            copy(t, dest_ref[0, k, t]).start()
        return 0

    def drain(t, _):
        for k in range(TOP_K):
            copy(0, 0).wait()
        return 0

    lax.fori_loop(0, tm, issue, 0, unroll=4)
    lax.fori_loop(0, tm, drain, 0, unroll=4)


def moe_dispatch(h2, dest, tm=MOE_TM):
    nt, d = h2.shape
    return pl.pallas_call(
        _dispatch_kernel,
        grid=(nt // tm,),
        in_specs=[pl.BlockSpec((1, 8, tm), lambda i: (i, 0, 0), memory_space=pltpu.SMEM),
                  pl.BlockSpec((tm, d), lambda i: (i, 0))],
        out_specs=pl.BlockSpec(memory_space=pl.ANY),
        scratch_shapes=[pltpu.SemaphoreType.DMA(())],
        out_shape=jax.ShapeDtypeStruct((nt * TOP_K, d), F32),
        compiler_params=_cparams(("arbitrary",)),
        name="moe_dispatch",
    )(dest, h2)


def _ffn_kernel(blk_ref, exp_ref, lo_ref, hi_ref, first_ref, valid_ref, x_ref, wg_ref, wu_ref, wd_ref, o_ref,
                wg_b, wu_b, wd_b):
    i = pl.program_id(0)
    rows = x_ref.shape[0]

    @pl.when((i == 0) | (exp_ref[i] != exp_ref[jnp.maximum(i - 1, 0)]))
    def _():
        wg_b[...] = wg_ref[0].astype(BF16)
        wu_b[...] = wu_ref[0].astype(BF16)
        wd_b[...] = wd_ref[0].astype(BF16)

    @pl.when(valid_ref[i] == 1)
    def _():
        x = x_ref[...].astype(BF16)
        a = _dot(x, wg_b[...])
        u = _dot(x, wu_b[...])
        y = _dot((_silu(a) * u).astype(BF16), wd_b[...])
        r = blk_ref[i] * rows + lax.broadcasted_iota(jnp.int32, (rows, 1), 0)
        y = jnp.where((r >= lo_ref[i]) & (r < hi_ref[i]), y, 0.0)

        @pl.when(first_ref[i] == 1)
        def _():
            o_ref[...] = y

        @pl.when(first_ref[i] == 0)
        def _():
            o_ref[...] = o_ref[...] + y


def _items_kernel(cnt_ref, starts_ref, blk_ref, exp_ref, lo_ref, hi_ref, first_ref, valid_ref, *, rows, n_items):
    shift = int(math.log2(rows))

    def expert(e, carry):
        start, n = carry
        c = cnt_ref[e]
        starts_ref[e] = start
        end = start + c
        first_blk = lax.shift_right_logical(start, shift)
        n_blk = jnp.where(c > 0, lax.shift_right_logical(jnp.maximum(end - 1, 0), shift) - first_blk + 1, 0)

        def item(k, n):
            b = first_blk + k
            lo = jnp.maximum(start, b * rows)
            blk_ref[n] = b
            exp_ref[n] = e
            lo_ref[n] = lo
            hi_ref[n] = jnp.minimum(end, (b + 1) * rows)
            first_ref[n] = (lo == b * rows).astype(jnp.int32)
            valid_ref[n] = 1
            return n + 1

        return end, lax.fori_loop(0, n_blk, item, n)

    _, total = lax.fori_loop(0, N_EXPERTS, expert, (jnp.int32(0), jnp.int32(0)))
    last = jnp.maximum(total - 1, 0)

    def fill(k, _):
        blk_ref[k] = blk_ref[last]
        exp_ref[k] = exp_ref[last]
        lo_ref[k] = 0
        hi_ref[k] = 0
        first_ref[k] = 0
        valid_ref[k] = 0
        return 0

    lax.fori_loop(total, n_items, fill, 0)


def _ffn_items(counts, n_rows, rows):
    n_items = n_rows // rows + N_EXPERTS - 1
    smem = pl.BlockSpec(memory_space=pltpu.SMEM)
    out = pl.pallas_call(
        functools.partial(_items_kernel, rows=rows, n_items=n_items),
        in_specs=[smem],
        out_specs=[smem] * 7,
        out_shape=[jax.ShapeDtypeStruct((N_EXPERTS,), jnp.int32)]
                  + [jax.ShapeDtypeStruct((n_items,), jnp.int32)] * 6,
        name="moe_items",
    )(counts)
    return out[0], tuple(out[1:])


def moe_ffn_sorted(xs, items, wg, wu, wd, layer, rows=MOE_ROWS):
    n_rows, d = xs.shape
    n_items = items[0].shape[0]
    de = wg.shape[-1]
    return pl.pallas_call(
        _ffn_kernel,
        grid_spec=pltpu.PrefetchScalarGridSpec(
            num_scalar_prefetch=6,
            grid=(n_items,),
            in_specs=[pl.BlockSpec((rows, d), lambda i, blk, e, *_: (blk[i], 0)),
                      pl.BlockSpec((None, 1, d, de), lambda i, blk, e, *_: (layer, e[i], 0, 0)),
                      pl.BlockSpec((None, 1, d, de), lambda i, blk, e, *_: (layer, e[i], 0, 0)),
                      pl.BlockSpec((None, 1, de, d), lambda i, blk, e, *_: (layer, e[i], 0, 0))],
            out_specs=pl.BlockSpec((rows, d), lambda i, blk, e, *_: (blk[i], 0)),
            scratch_shapes=[pltpu.VMEM((d, de), BF16), pltpu.VMEM((d, de), BF16), pltpu.VMEM((de, d), BF16)]),
        out_shape=jax.ShapeDtypeStruct((n_rows, d), F32),
        compiler_params=_cparams(("arbitrary",)),
        name="moe_ffn",
    )(*items, xs, wg, wu, wd)


def _combine_kernel(dest_ref, ys_ref, w_ref, h_ref, x_ref, g_ref, sg_ref, su_ref, sd_ref, o_ref, buf, sem):
    tm = h_ref.shape[0]

    def copy(t, k, row):
        return pltpu.make_async_copy(ys_ref.at[pl.ds(row, 1), :], buf.at[k, pl.ds(t, 1), :], sem)

    def issue(t, _):
        for k in range(TOP_K):
            copy(t, k, dest_ref[0, k, t]).start()
        return 0

    def drain(t, _):
        for k in range(TOP_K):
            copy(0, 0, 0).wait()
        return 0

    lax.fori_loop(0, tm, issue, 0, unroll=4)
    hb = h_ref[...].astype(BF16)
    y = _dot((_silu(_dot(hb, sg_ref[...])) * _dot(hb, su_ref[...])).astype(BF16), sd_ref[...])
    lax.fori_loop(0, tm, drain, 0, unroll=4)
    w = w_ref[...]
    for k in range(TOP_K):
        y = y + w[:, k:k + 1] * buf[k]
    o_ref[...] = x_ref[...] + g_ref[0] * y


def moe_combine(ys, dest, w, h2, x2, gate, sg, su, sd, seq, tm=MOE_TM):
    nt, d = h2.shape
    ds_ = sg.shape[-1]
    per_b = seq // tm
    tile = lambda: pl.BlockSpec((tm, d), lambda i: (i, 0))
    return pl.pallas_call(
        _combine_kernel,
        grid=(nt // tm,),
        in_specs=[pl.BlockSpec((1, 8, tm), lambda i: (i, 0, 0), memory_space=pltpu.SMEM),
                  pl.BlockSpec(memory_space=pl.ANY),
                  pl.BlockSpec((tm, LANES), lambda i: (i, 0)), tile(), tile(),
                  pl.BlockSpec((1, 1, d), lambda i: (i // per_b, 0, 0)),
                  _full((d, ds_)), _full((d, ds_)), _full((ds_, d))],
        out_specs=tile(),
        scratch_shapes=[pltpu.VMEM((TOP_K, tm, d), F32), pltpu.SemaphoreType.DMA(())],
        out_shape=jax.ShapeDtypeStruct((nt, d), F32),
        compiler_params=_cparams(("arbitrary",)),
        name="moe_combine",
    )(dest, ys, w, h2, x2, gate, sg.astype(BF16), su.astype(BF16), sd.astype(BF16))


def moe_layer(x, g_norm, sc, sh, gate, router, e_bias, wg, wu, wd, layer, sg, su, sd):
    b, t, d = x.shape
    nt = b * t
    h = ln_mod(x, g_norm, sc, sh, F32)
    h2 = h.reshape(nt, d)
    eid, rank, w, counts = moe_route(h2, router, e_bias)
    starts, items = _ffn_items(counts.reshape(-1).astype(jnp.int32), nt * TOP_K, MOE_ROWS)
    hit = eid[..., None] == jnp.arange(N_EXPERTS, dtype=jnp.int32)
    dest = jnp.sum(jnp.where(hit, starts.astype(jnp.int32), 0), axis=-1) + rank
    xs = moe_dispatch(h2, dest)
    ys = moe_ffn_sorted(xs, items, wg, wu, wd, layer)
    out = moe_combine(ys, dest, w, h2, x.reshape(nt, d), gate.reshape(b, 1, d), sg, su, sd, t)
    return out.reshape(b, t, d)


def kernel(x, c, ada_w, ada_b, norm_mix, norm_ffn, rel_bias, ev_w_in, ev_w_out, fox_fb, fox_qn, fox_kn, gdn_conv, gdn_a_log, gdn_dt_bias, gdn_on, od_w_in, od_w_out, nsa_qn, nsa_kn, nsa_pos, nsa_cmp_w1, nsa_cmp_w2, gla_wg_up, gla_bg, gla_on, moe_router, moe_bias, moe_wg, moe_wu, moe_wd, sh_wg, sh_wu, sh_wd):
    d = x.shape[-1]
    depth = ada_w.shape[0]
    mod = adaln(c, ada_w, ada_b)
    for layer in range(depth):
        sh1, sc1, g1, sh2, sc2, g2 = [mod[layer, :, i * d:(i + 1) * d] for i in range(6)]
        h = ln_mod(x, norm_mix[layer], sc1, sh1, BF16)
        j = layer // 2
        if layer % 2 == 0:
            y1, y2 = even_mixer(h, ev_w_in[j], fox_fb[j], fox_qn[j], fox_kn[j], gdn_conv[j], gdn_a_log[j],
                                gdn_dt_bias[j], gdn_on[j])
            w_out = ev_w_out[j]
        else:
            y1, y2 = odd_mixer(h, od_w_in[j], nsa_qn[j], nsa_kn[j], nsa_pos[j], nsa_cmp_w1[j], nsa_cmp_w2[j],
                               gla_wg_up[j], gla_bg[j], gla_on[j], rel_bias)
            w_out = od_w_out[j]
        x = out_proj(y1, y2, w_out, x, g1)
        x = moe_layer(x, norm_ffn[layer], sc2, sh2, g2, moe_router[layer], moe_bias[layer], moe_wg, moe_wu, moe_wd,
                      layer, sh_wg[layer], sh_wu[layer], sh_wd[layer])
    return x
```

```python
import functools
import math

import numpy as np
import jax
import jax.numpy as jnp
from jax import lax
from jax.experimental import pallas as pl
from jax.experimental.pallas import tpu as pltpu

F32 = jnp.float32
BF16 = jnp.bfloat16
HI = lax.Precision.HIGHEST

EPS = 1e-6
LOG2E = math.log2(math.e)
NEG = -1e30

FOX_HEADS, FOX_DH = 8, 64
GDN_HEADS, GDN_DH, GDN_CONV = 4, 128, 4
NSA_HEADS, NSA_KV_HEADS, NSA_DH = 8, 2, 64
NSA_GROUP = NSA_HEADS // NSA_KV_HEADS
CMP_LEN, CMP_STRIDE, CMP_HIDDEN = 32, 16, 256
SLC_LEN, SLC_TOPK, WINDOW = 64, 16, 512
GLA_HEADS, GLA_DK, GLA_DV, GLA_GATE_RANK, GLA_TAU, GLA_CHUNK = 4, 64, 128, 16, 16.0, 64
REL_BUCKETS, REL_MAX_DIST = 32, 128
N_EXPERTS, TOP_K, D_EXPERT, D_SHARED = 64, 6, 256, 256
N_GROUPS, TOPK_GROUPS, ROUTE_SCALE = 8, 4, 2.5

FOX_W = FOX_HEADS * FOX_DH
GDN_W = GDN_HEADS * GDN_DH
NSA_W = NSA_HEADS * NSA_DH
NSA_KV_W = NSA_KV_HEADS * NSA_DH
GLA_KW = GLA_HEADS * GLA_DK
GLA_W = GLA_HEADS * GLA_DV
EV_SIZES = (FOX_W, FOX_W, FOX_W, FOX_HEADS, 3 * GDN_W, GDN_HEADS, GDN_HEADS, GDN_W)
OD_SIZES = (NSA_W,) + (NSA_KV_W,) * 6 + (3 * NSA_HEADS, GLA_KW, GLA_KW, GLA_W, GLA_GATE_RANK, GLA_W)

LANES = 128
SUP_BLOCKS = 64
VMEM_LIMIT = 56 * 1024 * 1024


def _cparams(sem, flags=None):
    return pltpu.CompilerParams(dimension_semantics=sem, vmem_limit_bytes=VMEM_LIMIT, flags=flags)


def _full(shape):
    n = len(shape)
    return pl.BlockSpec(shape, lambda *_: (0,) * n)


def _dot(a, b):
    return jnp.dot(a, b, preferred_element_type=F32)


def _dot_hi(a, b):
    return jnp.dot(a, b, precision=HI, preferred_element_type=F32)


def _dot_nt(a, b, precision=None):
    return lax.dot_general(a, b, (((1,), (1,)), ((), ())), precision=precision, preferred_element_type=F32)


def _dot_tn(a, b, precision=None):
    return lax.dot_general(a, b, (((0,), (0,)), ((), ())), precision=precision, preferred_element_type=F32)


def _sigmoid(x):
    return 1.0 / (1.0 + jnp.exp(-x))


def _silu(x):
    return x * _sigmoid(x)


def _softplus(x):
    return jnp.maximum(x, 0.0) + jnp.log(1.0 + jnp.exp(-jnp.abs(x)))


def _log_sigmoid(x):
    return -_softplus(-x)


def _adaln_kernel(c_ref, w_ref, b_ref, o_ref):
    c = c_ref[...]
    o_ref[0] = _dot_hi(_silu(c), w_ref[0]) + b_ref[0]


def adaln(c, ada_w, ada_b):
    depth, d, n = ada_w.shape
    b = c.shape[0]
    cp = jnp.zeros((8, d), F32).at[:b].set(c)
    tn = 1536
    out = pl.pallas_call(
        _adaln_kernel,
        grid=(depth, n // tn),
        in_specs=[_full((8, d)),
                  pl.BlockSpec((1, d, tn), lambda l, j: (l, 0, j)),
                  pl.BlockSpec((1, 1, tn), lambda l, j: (l, 0, j))],
        out_specs=pl.BlockSpec((1, 8, tn), lambda l, j: (l, 0, j)),
        out_shape=jax.ShapeDtypeStruct((depth, 8, n), F32),
        compiler_params=_cparams(("arbitrary", "arbitrary")),
        name="adaln",
    )(cp, ada_w, ada_b.reshape(depth, 1, n))
    return out[:, :b]


def _ln_kernel(x_ref, g_ref, sc_ref, sh_ref, o_ref):
    x = x_ref[0]
    y = x * lax.rsqrt(jnp.mean(x * x, axis=-1, keepdims=True) + EPS) * g_ref[...]
    o_ref[0] = (y * (1.0 + sc_ref[0]) + sh_ref[0]).astype(o_ref.dtype)


def ln_mod(x, g, sc, sh, out_dtype, tm=512):
    b, t, d = x.shape
    return pl.pallas_call(
        _ln_kernel,
        grid=(b, t // tm),
        in_specs=[pl.BlockSpec((1, tm, d), lambda i, j: (i, j, 0)),
                  _full((1, d)),
                  pl.BlockSpec((1, 1, d), lambda i, j: (i, 0, 0)),
                  pl.BlockSpec((1, 1, d), lambda i, j: (i, 0, 0))],
        out_specs=pl.BlockSpec((1, tm, d), lambda i, j: (i, j, 0)),
        out_shape=jax.ShapeDtypeStruct((b, t, d), out_dtype),
        compiler_params=_cparams(("arbitrary", "arbitrary")),
        name="ln_mod",
    )(x, g.reshape(1, d), sc.reshape(b, 1, d), sh.reshape(b, 1, d))


def proj_multi(h, groups, tm=512, name="proj"):
    b, t, d = h.shape
    widths = [g[0].shape[1] for g in groups]
    starts = np.cumsum([0] + widths)
    w_cat = jnp.concatenate([g[0] for g in groups], axis=1).astype(BF16)
    extras = [e for g in groups for e in g[3]]
    n_ex = [len(g[3]) for g in groups]
    n_out = len(groups)

    def kern(h_ref, w_ref, *rest):
        ex_refs = rest[:len(extras)]
        o_refs = rest[len(extras):]
        y = _dot(h_ref[0], w_ref[...])
        pos = 0
        for gi, (_, out_dtype, epilogue, _) in enumerate(groups):
            yg = y[:, starts[gi]:starts[gi + 1]]
            if epilogue is not None:
                yg = epilogue(yg, *[e[...] for e in ex_refs[pos:pos + n_ex[gi]]])
            pos += n_ex[gi]
            o_refs[gi][0] = yg.astype(out_dtype)

    return pl.pallas_call(
        kern,
        grid=(b, t // tm),
        in_specs=[pl.BlockSpec((1, tm, d), lambda i, j: (i, j, 0)), _full((d, int(starts[-1])))]
                 + [_full(e.shape) for e in extras],
        out_specs=[pl.BlockSpec((1, tm, n), lambda i, j: (i, j, 0)) for n in widths],
        out_shape=[jax.ShapeDtypeStruct((b, t, n), g[1]) for n, g in zip(widths, groups)],
        compiler_params=_cparams(("arbitrary", "arbitrary")),
        name=name,
    )(h, w_cat, *extras)


def _head_norm_epilogue(dh):
    inv = 1.0 / dh

    def ep(y, bd, gain):
        ssq = _dot((y * y).astype(BF16), bd)
        return y * lax.rsqrt(ssq * inv + EPS) * gain

    return ep


def _block_diag_ones(n, dh):
    i = np.arange(n) // dh
    return jnp.asarray((i[:, None] == i[None, :]).astype(np.float32), dtype=BF16)


def _outproj_kernel(y1_ref, y2_ref, wa_ref, wb_ref, x_ref, g_ref, o_ref):
    y = _dot(y1_ref[0], wa_ref[...]) + _dot(y2_ref[0], wb_ref[...])
    o_ref[0] = x_ref[0] + g_ref[0] * y


def out_proj(y1, y2, w_out, x, gate, tm=512):
    b, t, d = x.shape
    n1, n2 = y1.shape[-1], y2.shape[-1]
    wa = w_out[:n1].astype(BF16)
    wb = w_out[n1:].astype(BF16)
    return pl.pallas_call(
        _outproj_kernel,
        grid=(b, t // tm),
        in_specs=[pl.BlockSpec((1, tm, n1), lambda i, j: (i, j, 0)),
                  pl.BlockSpec((1, tm, n2), lambda i, j: (i, j, 0)),
                  _full((n1, d)), _full((n2, d)),
                  pl.BlockSpec((1, tm, d), lambda i, j: (i, j, 0)),
                  pl.BlockSpec((1, 1, d), lambda i, j: (i, 0, 0))],
        out_specs=pl.BlockSpec((1, tm, d), lambda i, j: (i, j, 0)),
        out_shape=jax.ShapeDtypeStruct((b, t, d), F32),
        compiler_params=_cparams(("arbitrary", "arbitrary")),
        name="out_proj",
    )(y1, y2, wa, wb, x, gate.reshape(b, 1, d))


def _decay_kernel(s_ref, fb_ref, tril_ref, place_ref, o_ref, carry):
    @pl.when(pl.program_id(1) == 0)
    def _():
        carry[...] = jnp.zeros_like(carry)

    tm = s_ref.shape[1]
    lf = _log_sigmoid(s_ref[0] + fb_ref[...])
    cum = _dot_hi(tril_ref[...], lf) + carry[...]
    carry[...] = cum[tm - 1:tm, :]
    x = cum * LOG2E
    hi = x.astype(BF16)
    r1 = x - hi.astype(F32)
    mid = r1.astype(BF16)
    low = (r1 - mid.astype(F32)).astype(BF16)
    o_ref[0] = _dot(jnp.concatenate([hi, mid, low], axis=1), place_ref[...]).astype(o_ref.dtype)


def fox_decay(small, fox_fb, tm=512):
    b, t, _ = small.shape
    fb = jnp.zeros((1, LANES), F32).at[0, :FOX_HEADS].set(fox_fb)
    tril = jnp.asarray(np.tril(np.ones((tm, tm), np.float32)))
    place = np.zeros((3 * LANES, FOX_W), np.float32)
    for h in range(FOX_HEADS):
        for j in range(3):
            place[j * LANES + h, (h // 2) * LANES + (FOX_DH if h % 2 == 0 else 0) + j] = 1.0
    return pl.pallas_call(
        _decay_kernel,
        grid=(b, t // tm),
        in_specs=[pl.BlockSpec((1, tm, LANES), lambda i, j: (i, j, 0)), _full((1, LANES)), _full((tm, tm)),
                  _full((3 * LANES, FOX_W))],
        out_specs=pl.BlockSpec((1, tm, FOX_W), lambda i, j: (i, j, 0)),
        out_shape=jax.ShapeDtypeStruct((b, t, FOX_W), BF16),
        scratch_shapes=[pltpu.VMEM((1, LANES), F32)],
        compiler_params=_cparams(("arbitrary", "arbitrary")),
        name="fox_decay",
    )(small, fb, tril, jnp.asarray(place, dtype=BF16))


def _fox_kernel(q_ref, k_ref, v_ref, f_ref, o_ref, *, tq, wide):
    i = pl.program_id(2)
    lane = lax.broadcasted_iota(jnp.int32, (1, LANES), 1)
    lo = lane < FOX_DH
    coef = jnp.where((lane & (FOX_DH - 1)) < 3, -1.0, 0.0).astype(BF16)
    q = q_ref[0]
    qs = (jnp.where(lo, q, coef), jnp.where(lo, coef, q))
    causal = (lax.broadcasted_iota(jnp.int32, (tq, tq), 1) <= lax.broadcasted_iota(jnp.int32, (tq, tq), 0))
    one = jnp.ones((1, LANES), BF16)

    def step(j, carry, tw, diag=False):
        start = pl.multiple_of(j * tw, tw)
        kt = k_ref[0, pl.ds(start, tw), :]
        ft = f_ref[0, pl.ds(start, tw), :]
        vt = v_ref[0, pl.ds(start, tw), :]
        s_pair = (_dot_nt(qs[0], jnp.where(lo, kt, ft)), _dot_nt(qs[1], jnp.where(lo, ft, kt)))
        vs = (jnp.where(lo, vt, one), jnp.where(lo, one, vt))
        new = []
        for hh in range(2):
            m, acc = carry[hh]
            s = s_pair[hh]
            if diag:
                s = jnp.where(causal, s, NEG)
            m_new = jnp.maximum(m, jnp.max(s, axis=1, keepdims=True))
            p = jnp.exp2(s - m_new)
            acc = jnp.exp2(m - m_new) * acc + _dot(p.astype(BF16), vs[hh])
            new.append((m_new, acc))
        return tuple(new)

    carry = tuple((jnp.full((tq, 1), NEG, F32), jnp.zeros((tq, LANES), F32)) for _ in range(2))
    n_wide = i // wide
    carry = lax.fori_loop(0, n_wide, lambda j, c: step(j, c, wide * tq), carry)
    done = n_wide * wide
    part = wide // 2
    while part >= 1:
        carry = lax.cond((i & part) != 0, lambda c, d=done, w=part: step(d // w, c, w * tq), lambda c: c, carry)
        done = done + (i & part)
        part //= 2
    carry = step(i, carry, tq, diag=True)
    acc = jnp.where(lo, carry[0][1], carry[1][1])
    den = jnp.where(lo, carry[1][1], carry[0][1])
    o_ref[0] = (acc / pltpu.roll(den, FOX_DH, 1)).astype(o_ref.dtype)


def fox_attention(q, k, v, feat, tq=512, wide=4):
    b, t, w = q.shape
    npair = w // LANES
    nt = t // tq
    whole = lambda: pl.BlockSpec((1, t, LANES), lambda bi, p, i: (bi, 0, p))
    return pl.pallas_call(
        functools.partial(_fox_kernel, tq=tq, wide=wide),
        grid=(b, npair, nt),
        in_specs=[pl.BlockSpec((1, tq, LANES), lambda bi, p, i: (bi, i, p)), whole(), whole(), whole()],
        out_specs=pl.BlockSpec((1, tq, LANES), lambda bi, p, i: (bi, i, p)),
        out_shape=jax.ShapeDtypeStruct((b, t, w), BF16),
        compiler_params=_cparams(("arbitrary", "arbitrary", "arbitrary")),
        name="fox_attn",
    )(q, k, v, feat)


def _mm(a, b):
    return _dot(a.astype(BF16), b.astype(BF16))


def _mm3(a, b):
    ah = a.astype(BF16)
    bh = b.astype(BF16)
    al = (a - ah.astype(F32)).astype(BF16)
    bl = (b - bh.astype(F32)).astype(BF16)
    return _dot(jnp.concatenate([ah, ah, al], axis=1), jnp.concatenate([bh, bl, bh], axis=0))


def _tril_solve(a, rhs, ri, ci):
    n = a[0].shape[0]
    both = lambda f, x, y: [f(p, q) for p, q in zip(x, y)]
    eye = (ri == ci).astype(F32)
    same = lambda b: (lax.shift_right_logical(ri, int(math.log2(b)))
                      == lax.shift_right_logical(ci, int(math.log2(b))))
    base = 16
    d = [jnp.where(same(base), p, 0.0) for p in a]
    d2 = both(_mm, d, d)
    d4 = both(_mm, d2, d2)
    r1 = [eye - p + p2 - t for p, p2, t in zip(d, d2, both(_mm, d, d2))]
    d8 = both(_mm, d4, d4)
    r2 = [eye + p4 + p8 + t for p4, p8, t in zip(d4, d8, both(_mm, d4, d8))]
    t = both(_mm, r1, r2)
    b = base
    while b < n:
        join = same(2 * b) & jnp.logical_not(same(b))
        low = [jnp.where(join, p, 0.0) for p in a]
        t = [p - q for p, q in zip(t, both(_mm, both(_mm, t, low), t))]
        b *= 2
    return both(_mm3, t, rhs)


GDN_BLOCK = 128


def _gdn_kernel(x_ref, sm_ref, z_ref, cw_ref, ega_ref, egb_ref, alog_ref, dtb_ref, on_ref, tril_ref,
                o_ref, s_scr, prev_scr, *, tc):
    c = GDN_BLOCK
    w = GDN_W

    @pl.when(pl.program_id(1) == 0)
    def _():
        s_scr[...] = jnp.zeros_like(s_scr)
        prev_scr[...] = jnp.zeros_like(prev_scr)

    x = x_ref[0]
    prev = prev_scr[...]
    row8 = lax.broadcasted_iota(jnp.int32, (8, 1), 0)
    acc = x * cw_ref[GDN_CONV - 1:GDN_CONV, :]
    for s in range(1, GDN_CONV):
        rolled = pltpu.roll(x, s, 0)
        head = jnp.where(row8 < s, pltpu.roll(prev, s, 0), rolled[0:8])
        shifted = jnp.concatenate([head, rolled[8:]], axis=0)
        acc = acc + shifted * cw_ref[GDN_CONV - 1 - s:GDN_CONV - s, :]
    prev_scr[...] = x[tc - 8:tc]
    xc = _silu(acc)

    sm = sm_ref[0]
    g_raw = _dot_hi(sm, ega_ref[...])
    b_raw = _dot_hi(sm, egb_ref[...])
    g = -jnp.exp(alog_ref[...]) * _softplus(g_raw + dtb_ref[...])
    beta_all = _sigmoid(b_raw)
    gc_all = _dot_hi(tril_ref[...], g)

    ri = lax.broadcasted_iota(jnp.int32, (c, c), 0)
    ci = lax.broadcasted_iota(jnp.int32, (c, c), 1)
    causal = ci <= ri
    strict = ci < ri

    nblk = tc // c
    a_l, attn_l, rhs_l, qd_l, kd_l, egl_l = [], [], [], [], [], []
    for h in range(GDN_HEADS):
        ln = slice(h * GDN_DH, (h + 1) * GDN_DH)
        qh = xc[:, h * GDN_DH:(h + 1) * GDN_DH]
        kh = xc[:, w + h * GDN_DH:w + (h + 1) * GDN_DH]
        qh = qh * lax.rsqrt(jnp.sum(qh * qh, axis=-1, keepdims=True) + EPS) * (GDN_DH ** -0.5)
        kh = kh * lax.rsqrt(jnp.sum(kh * kh, axis=-1, keepdims=True) + EPS)
        vh = xc[:, 2 * w + h * GDN_DH:2 * w + (h + 1) * GDN_DH]
        gch = gc_all[:, ln]
        gct = gch.T
        egc = jnp.exp(gch)
        bh = beta_all[:, ln]
        for n in range(nblk):
            sl = slice(n * c, (n + 1) * c)
            q, k, v, gc, be = qh[sl], kh[sl], vh[sl], gch[sl], bh[sl]
            decay = jnp.exp(jnp.where(causal, gc - gct[:, sl], NEG))
            kb = k * be
            kk = _dot_nt(jnp.concatenate([kb, q], axis=0).astype(BF16), k.astype(BF16))
            a_l.append(jnp.where(strict, kk[:c] * decay, 0.0))
            attn_l.append(jnp.where(causal, kk[c:] * decay, 0.0))
            rhs_l.append(jnp.concatenate([v * be, kb * egc[sl]], axis=1))
            gl = gc[c - 1:c, :]
            qd_l.append(q * egc[sl])
            kd_l.append(k * jnp.exp(gl - gc))
            egl_l.append(jnp.exp(gl))
    uw_l = _tril_solve(a_l, rhs_l, ri, ci)

    states = [s_scr[h] for h in range(GDN_HEADS)]
    for n in range(nblk):
        sl = slice(n * c, (n + 1) * c)
        idx = [h * nblk + n for h in range(GDN_HEADS)]
        ws = [_mm(jnp.concatenate([uw_l[i][:, GDN_DH:], qd_l[i]], axis=0), states[h])
              for h, i in enumerate(idx)]
        v_new = [uw_l[i][:, :GDN_DH] - ws[h][:c] for h, i in enumerate(idx)]
        o = [ws[h][c:] + _mm(attn_l[i], v_new[h]) for h, i in enumerate(idx)]
        states = [states[h] * egl_l[i] + _dot_tn(kd_l[i].astype(BF16), v_new[h].astype(BF16))
                  for h, i in enumerate(idx)]
        for h in range(GDN_HEADS):
            ln = slice(h * GDN_DH, (h + 1) * GDN_DH)
            on = o[h] * lax.rsqrt(jnp.mean(o[h] * o[h], axis=-1, keepdims=True) + EPS) * on_ref[...]
            o_ref[0, sl, ln] = (on * _silu(z_ref[0, sl, ln])).astype(o_ref.dtype)
    for h in range(GDN_HEADS):
        s_scr[h] = states[h]


def gated_delta_net(x, small, z, conv_w, a_log, dt_bias, on_gain, tc=512):
    b, t, _ = x.shape
    w = GDN_W
    ega = np.zeros((LANES, w), np.float32)
    egb = np.zeros((LANES, w), np.float32)
    for h in range(GDN_HEADS):
        ega[FOX_HEADS + h, h * GDN_DH:(h + 1) * GDN_DH] = 1.0
        egb[FOX_HEADS + GDN_HEADS + h, h * GDN_DH:(h + 1) * GDN_DH] = 1.0
    alog = jnp.repeat(a_log, GDN_DH).reshape(1, w)
    dtb = jnp.repeat(dt_bias, GDN_DH).reshape(1, w)
    idx = np.arange(tc)
    tril = ((idx[:, None] >= idx[None, :]) & (idx[:, None] // GDN_BLOCK == idx[None, :] // GDN_BLOCK))
    row = lambda n: pl.BlockSpec((1, tc, n), lambda i, j: (i, j, 0))
    return pl.pallas_call(
        functools.partial(_gdn_kernel, tc=tc),
        grid=(b, t // tc),
        in_specs=[row(3 * w), row(LANES), row(w), _full((GDN_CONV, 3 * w)), _full((LANES, w)), _full((LANES, w)),
                  _full((1, w)), _full((1, w)), _full((1, GDN_DH)), _full((tc, tc))],
        out_specs=row(w),
        out_shape=jax.ShapeDtypeStruct((b, t, w), BF16),
        scratch_shapes=[pltpu.VMEM((GDN_HEADS, GDN_DH, GDN_DH), F32), pltpu.VMEM((8, 3 * w), F32)],
        compiler_params=_cparams(("arbitrary", "arbitrary")),
        name="gdn",
    )(x, small, z, conv_w, jnp.asarray(ega), jnp.asarray(egb), alog, dtb, on_gain.reshape(1, GDN_DH),
      jnp.asarray(tril.astype(np.float32)))


def even_mixer(h, w_in, fox_fb, fox_qn, fox_kn, gdn_conv, gdn_a_log, gdn_dt_bias, gdn_on):
    cuts = np.cumsum((0,) + EV_SIZES)
    col = lambda i: w_in[:, cuts[i]:cuts[i + 1]]
    bd = _block_diag_ones(FOX_W, FOX_DH)
    ep = _head_norm_epilogue(FOX_DH)
    qg = (jnp.tile(fox_qn, FOX_HEADS) * (FOX_DH ** -0.5 * LOG2E)).reshape(1, FOX_W)
    kg = jnp.tile(fox_kn, FOX_HEADS).reshape(1, FOX_W)
    w_small = jnp.zeros((w_in.shape[0], LANES), F32)
    w_small = w_small.at[:, 0:8].set(col(3)).at[:, 8:12].set(col(5)).at[:, 12:16].set(col(6))
    fq, fk, fv, small, gqkv, gz = proj_multi(
        h, [(col(0), BF16, ep, (bd, qg)), (col(1), BF16, ep, (bd, kg)), (col(2), BF16, None, ()),
            (w_small, F32, None, ()), (col(4), F32, None, ()), (col(7), F32, None, ())], name="proj_even")
    feat = fox_decay(small, fox_fb)
    o_fox = fox_attention(fq, fk, fv, feat)
    o_gdn = gated_delta_net(gqkv, small, gz, gdn_conv, gdn_a_log, gdn_dt_bias, gdn_on)
    return o_fox, o_gdn


def _t5_bucket_np(dist):
    n = np.maximum(dist, 0)
    exact = REL_BUCKETS // 2
    nf = np.maximum(n, 1).astype(np.float32)
    large = exact + (np.log(nf / np.float32(exact)) / np.float32(math.log(REL_MAX_DIST / exact))
                     * np.float32(REL_BUCKETS - exact)).astype(np.int32)
    large = np.minimum(large, REL_BUCKETS - 1)
    return np.where(n < exact, n, large)


def _bias_kernel(tbl_ref, bucket_ref, o_ref):
    h = pl.program_id(0)
    bucket = bucket_ref[...]
    acc = jnp.full(bucket.shape, NEG, F32)
    for b in range(REL_BUCKETS):
        acc = jnp.where(bucket == b, tbl_ref[b, h], acc)
    o_ref[0] = acc


def _bias_table(rel_bias, dist, valid):
    shifted = (rel_bias - rel_bias[REL_BUCKETS - 1:REL_BUCKETS]) * LOG2E
    bucket = np.where(valid, _t5_bucket_np(dist), -1).astype(np.int32)
    rows, cols = int(np.prod(bucket.shape[:-1])), bucket.shape[-1]
    nh = rel_bias.shape[1]
    tb = pl.pallas_call(
        _bias_kernel,
        grid=(nh,),
        in_specs=[pl.BlockSpec(memory_space=pltpu.SMEM), _full((rows, cols))],
        out_specs=pl.BlockSpec((1, rows, cols), lambda h: (h, 0, 0)),
        out_shape=jax.ShapeDtypeStruct((nh, rows, cols), F32),
        compiler_params=_cparams(("arbitrary",)),
        name="t5_bias",
    )(shifted, jnp.asarray(bucket.reshape(rows, cols)))
    return tb.reshape((nh,) + bucket.shape)


def _cmp_kernel(r_ref, pos_ref, w1_ref, w2_ref, kn_ref, o_ref):
    m = r_ref.shape[3]
    half = r_ref.shape[4]
    r = r_ref[0, 0, 0].astype(BF16)
    a = _dot(r, w1_ref[0, :half, :])
    bm = _dot(r, w1_ref[0, half:, :])
    c = _dot(pos_ref[0].astype(BF16), w1_ref[0])
    hid = a + pltpu.roll(bm, m - 1, 0) + c[0:1, :]
    out = _dot(_silu(hid).astype(BF16), w2_ref[0])
    normed = out * lax.rsqrt(jnp.mean(out * out, axis=-1, keepdims=True) + EPS) * kn_ref[...]
    o_ref[0, 0, 0] = jnp.where(pl.program_id(0) == 0, normed, out).astype(o_ref.dtype)


def nsa_compress(kcvc, pos, w1, w2, kn):
    b, t, _ = kcvc.shape
    m = t // CMP_STRIDE
    half = CMP_STRIDE * NSA_DH
    r = kcvc.reshape(b, m, CMP_STRIDE, 2, NSA_KV_HEADS, NSA_DH).transpose(3, 0, 4, 1, 2, 5).reshape(2, b, 2, m, half)
    posf = jnp.zeros((2, 8, 2 * half), F32).at[:, 0].set(pos.reshape(2, 2 * half))
    w2d = jnp.concatenate([w2, w2], axis=-1).astype(BF16)
    knd = jnp.tile(kn, 2).reshape(1, LANES)
    return pl.pallas_call(
        _cmp_kernel,
        grid=(2, b, NSA_KV_HEADS),
        in_specs=[pl.BlockSpec((1, 1, 1, m, half), lambda s, i, k: (s, i, k, 0, 0)),
                  pl.BlockSpec((1, 8, 2 * half), lambda s, i, k: (s, 0, 0)),
                  pl.BlockSpec((1, 2 * half, CMP_HIDDEN), lambda s, i, k: (s, 0, 0)),
                  pl.BlockSpec((1, CMP_HIDDEN, LANES), lambda s, i, k: (s, 0, 0)),
                  _full((1, LANES))],
        out_specs=pl.BlockSpec((1, 1, 1, m, LANES), lambda s, i, k: (s, i, k, 0, 0)),
        out_shape=jax.ShapeDtypeStruct((2, b, NSA_KV_HEADS, m, LANES), BF16),
        compiler_params=_cparams(("arbitrary", "arbitrary", "arbitrary")),
        name="nsa_compress",
    )(r, posf, w1.astype(BF16), w2d, knd)


def _dot_split(a, b):
    hi = a.astype(BF16)
    lo = (a - hi.astype(F32)).astype(BF16)
    return _dot(hi, b) + _dot(lo, b)


def _head_q(q_ref, hh, lo):
    blk = q_ref[0, :, (hh // 2) * LANES:(hh // 2 + 1) * LANES]
    keep = lo if hh % 2 == 0 else jnp.logical_not(lo)
    return jnp.where(keep, blk, jnp.zeros_like(blk))


def _pair_heads(o, lo):
    return jnp.concatenate([jnp.where(lo, o[0], o[1]), jnp.where(lo, o[2], o[3])], axis=1)


def _nsa_sel_kernel(q_ref, kc_ref, vc_ref, ov_ref, bt_ref, o_ref, sel_ref, *, tq, nband, n_slc):
    i = pl.program_id(2)
    ncp = kc_ref.shape[3]
    nsp = ov_ref.shape[1]
    per = tq // CMP_STRIDE
    var = jnp.minimum(i, 1)
    bs = pl.multiple_of(per * jnp.maximum(i - 1, 0), per)
    lo = lax.broadcasted_iota(jnp.int32, (1, LANES), 1) < NSA_DH
    kc = kc_ref[0, 0, 0]
    vc = vc_ref[0, 0, 0]
    kcb = kc_ref[0, 0, 0, pl.ds(bs, nband), :]
    vcb = vc_ref[0, 0, 0, pl.ds(bs, nband), :]
    far_ok = lax.broadcasted_iota(jnp.int32, (1, ncp), 1) < per * (i - 1)
    hs = range(NSA_GROUP)
    qh = [_head_q(q_ref, hh, lo) for hh in hs]
    s_far = [jnp.where(far_ok, _dot_nt(q, kc), NEG) for q in qh]
    s_band = [_dot_nt(qh[hh], kcb) + bt_ref[var, hh] for hh in hs]
    m = [jnp.maximum(jnp.max(a, axis=1, keepdims=True), jnp.max(b, axis=1, keepdims=True))
         for a, b in zip(s_far, s_band)]
    m = [jnp.where(x < 0.5 * NEG, 0.0, x) for x in m]
    p_far = [jnp.exp2(a - x) for a, x in zip(s_far, m)]
    p_band = [jnp.exp2(b - x) for b, x in zip(s_band, m)]
    l = [jnp.sum(a, axis=1, keepdims=True) + jnp.sum(b, axis=1, keepdims=True) for a, b in zip(p_far, p_band)]
    inv = [1.0 / jnp.where(x == 0.0, 1.0, x) for x in l]
    outs = [(_dot(a.astype(BF16), vc) + _dot(b.astype(BF16), vcb)) * x for a, b, x in zip(p_far, p_band, inv)]
    ps_far = p_far[0] * inv[0]
    ps_band = p_band[0] * inv[0]
    for hh in range(1, NSA_GROUP):
        ps_far = ps_far + p_far[hh] * inv[hh]
        ps_band = ps_band + p_band[hh] * inv[hh]
    o_ref[0] = _pair_heads(outs, lo).astype(o_ref.dtype)

    imp = _dot_split(ps_far, ov_ref[...]) + _dot_split(ps_band, ov_ref[pl.ds(bs, nband), :])
    blk = lax.broadcasted_iota(jnp.int32, (1, nsp), 1)
    blk_f = blk.astype(F32)
    qpos = i * tq + lax.broadcasted_iota(jnp.int32, (tq, 1), 0)
    cur = lax.shift_right_logical(qpos, int(math.log2(SLC_LEN)))
    forced = (blk == 0) | (blk == cur) | (blk == cur - 1)
    work = jnp.where(forced, -jnp.inf, jnp.where(blk <= cur, imp, NEG))
    work = jnp.where(blk < n_slc, work, -jnp.inf)
    ngrp = 4
    rg = tq // ngrp
    works = [work[r * rg:(r + 1) * rg] for r in range(ngrp)]
    sels = [jnp.where(forced[r * rg:(r + 1) * rg], 1.0, jnp.zeros((rg, nsp), F32)) for r in range(ngrp)]
    for _ in range(max(min(SLC_TOPK, n_slc) - 3, 0)):
        ms = [jnp.max(w, axis=1, keepdims=True) for w in works]
        firsts = [jnp.min(jnp.where(w == m, blk_f, float(nsp)), axis=1, keepdims=True) for w, m in zip(works, ms)]
        picks = [blk_f == f for f in firsts]
        sels = [jnp.where(p, 1.0, s) for p, s in zip(picks, sels)]
        works = [jnp.where(p, -jnp.inf, w) for p, w in zip(picks, works)]
    sel = jnp.concatenate(sels, axis=0)
    for sup in range(sel_ref.shape[2]):
        col = sel[:, (sup // 2) * LANES:(sup // 2 + 1) * LANES]
        if sup % 2 == 0:
            col = pltpu.roll(col, SUP_BLOCKS, 1)
        sel_ref[0, 0, sup] = jnp.where(lo, 0.0, jnp.where(col > 0.5, 0.0, NEG)).astype(sel_ref.dtype)


def nsa_select(q, cmp_kv, rel_bias, tq=512):
    b, t, _ = q.shape
    ncp = t // CMP_STRIDE
    n_cmp = ncp - 1
    n_slc = t // SLC_LEN
    nsp = max(LANES, n_slc)
    nsup = max(1, n_slc // SUP_BLOCKS)
    per = tq // CMP_STRIDE
    nband = 2 * per
    n = np.arange(ncp)[:, None]
    s = np.arange(nsp)[None, :]
    ov = ((CMP_STRIDE * n < SLC_LEN * s + SLC_LEN) & (CMP_STRIDE * n + CMP_LEN > SLC_LEN * s)
          & (n < n_cmp) & (s < n_slc)).astype(np.float32)
    qi = np.arange(tq)[:, None]
    nj = np.arange(nband)[None, :]
    end = CMP_STRIDE * nj + CMP_LEN - 1
    dist = np.stack([qi - end, tq + qi - end])
    bt = _bias_table(rel_bias, dist, dist >= 0)
    bt = bt.reshape(NSA_KV_HEADS, NSA_GROUP, 2, tq, nband).transpose(0, 2, 1, 3, 4)
    bt = bt.reshape(NSA_KV_HEADS * 2, NSA_GROUP, tq, nband)
    gw = NSA_GROUP * NSA_DH
    return pl.pallas_call(
        functools.partial(_nsa_sel_kernel, tq=tq, nband=nband, n_slc=n_slc),
        grid=(b, NSA_KV_HEADS, t // tq),
        in_specs=[pl.BlockSpec((1, tq, gw), lambda bi, k, i: (bi, i, k)),
                  pl.BlockSpec((1, 1, 1, ncp, LANES), lambda bi, k, i: (0, bi, k, 0, 0)),
                  pl.BlockSpec((1, 1, 1, ncp, LANES), lambda bi, k, i: (1, bi, k, 0, 0)),
                  _full((ncp, nsp)),
                  pl.BlockSpec((2, NSA_GROUP, tq, nband), lambda bi, k, i: (k, 0, 0, 0))],
        out_specs=[pl.BlockSpec((1, tq, gw), lambda bi, k, i: (bi, i, k)),
                   pl.BlockSpec((1, 1, nsup, tq, LANES), lambda bi, k, i: (bi, k, 0, i, 0))],
        out_shape=[jax.ShapeDtypeStruct((b, t, NSA_W), BF16),
                   jax.ShapeDtypeStruct((b, NSA_KV_HEADS, nsup, t, LANES), BF16)],
        compiler_params=_cparams(("arbitrary", "arbitrary", "arbitrary")),
        name="nsa_select",
    )(q, cmp_kv, cmp_kv, jnp.asarray(ov, dtype=BF16), bt)


def _nsa_main_kernel(q_ref, ks_ref, vs_ref, kw0_ref, kw1_ref, kw2_ref, vw0_ref, vw1_ref, vw2_ref, sel_ref, oh_ref,
                     ocmp_ref, gate_ref, tb_ref, wm_ref, eg_ref, o_ref, *, tq):
    i = pl.program_id(2)
    g = NSA_GROUP
    lo = lax.broadcasted_iota(jnp.int32, (1, LANES), 1) < NSA_DH

    def head_low(hh):
        blk = q_ref[0, :, (hh // 2) * LANES:(hh // 2 + 1) * LANES]
        if hh % 2 == 1:
            blk = pltpu.roll(blk.astype(F32), NSA_DH, 1).astype(BF16)
        return jnp.where(lo, blk, jnp.zeros_like(blk))

    qst = jnp.concatenate([head_low(hh) for hh in range(g)], axis=0)
    causal = (lax.broadcasted_iota(jnp.int32, (tq, tq), 1) <= lax.broadcasted_iota(jnp.int32, (tq, tq), 0))
    one = jnp.ones((1, LANES), BF16)
    sup_keys = SUP_BLOCKS * SLC_LEN
    all_masked = jnp.where(lo, 0.0, NEG).astype(BF16)

    def sel_step(jt, carry, near, tw):
        m, acc = carry
        key0 = jnp.maximum(jt, 0) * tw
        start = pl.multiple_of(key0, tw)
        within = pl.multiple_of(key0 & (sup_keys - 1), tw)
        mq = sel_ref[0, 0, lax.shift_right_logical(key0, int(math.log2(sup_keys)))]
        mq = jnp.where(jt >= 0, mq, all_masked)
        q_ext = qst + jnp.concatenate([mq] * g, axis=0)
        kt = jnp.where(lo, ks_ref[0, pl.ds(start, tw), :], oh_ref[pl.ds(within, tw), :])
        vt = jnp.where(lo, vs_ref[0, pl.ds(start, tw), :], one)
        s = _dot_nt(q_ext, kt)
        if near is not None:
            s = s.reshape(g, tq, tw) + tb_ref[:, :, near * tq:(near + 1) * tq]
            if near == 2:
                s = jnp.where(causal[None], s, NEG)
            s = s.reshape(g * tq, tw)
        m_new = jnp.maximum(m, jnp.max(s, axis=1, keepdims=True))
        p = jnp.exp2(s - m_new)
        acc = jnp.exp2(m - m_new) * acc + _dot(p.astype(BF16), vt)
        return m_new, acc

    n_far = jnp.maximum(i - 2, 0)
    n4 = lax.shift_right_logical(n_far, 2)
    carry = (jnp.full((g * tq, 1), NEG, F32), jnp.zeros((g * tq, LANES), F32))
    carry = lax.fori_loop(0, n4, lambda j, c: sel_step(j, c, None, 4 * tq), carry)
    carry = lax.cond((n_far & 2) == 2, lambda c: sel_step(2 * n4, c, None, 2 * tq), lambda c: c, carry)
    carry = lax.cond((n_far & 1) == 1, lambda c: sel_step(i - 3, c, None, tq), lambda c: c, carry)
    for near in range(3):
        carry = sel_step(i - 2 + near, carry, near, tq)
    o_slc = (carry[1] / pltpu.roll(carry[1], NSA_DH, 1)).reshape(g, tq, LANES)

    var = jnp.minimum(i, 2)
    kws = (kw0_ref, kw1_ref, kw2_ref)
    vws = (vw0_ref, vw1_ref, vw2_ref)
    sw = []
    for near in range(3):
        s = _dot_nt(qst, kws[near][0]).reshape(g, tq, tq)
        s = s + tb_ref[:, :, near * tq:(near + 1) * tq] + wm_ref[var, :, near * tq:(near + 1) * tq][None]
        sw.append(s.reshape(g * tq, tq))
    m = jnp.maximum(jnp.maximum(jnp.max(sw[0], axis=1, keepdims=True), jnp.max(sw[1], axis=1, keepdims=True)),
                    jnp.max(sw[2], axis=1, keepdims=True))
    acc = jnp.zeros((g * tq, LANES), F32)
    for near in range(3):
        p = jnp.exp2(sw[near] - m)
        acc = acc + _dot(p.astype(BF16), jnp.where(lo, vws[near][0], one))
    o_win = (acc / pltpu.roll(acc, NSA_DH, 1)).reshape(g, tq, LANES)

    pair = lambda o: jnp.concatenate([jnp.where(lo, o[0], pltpu.roll(o[1], NSA_DH, 1)),
                                      jnp.where(lo, o[2], pltpu.roll(o[3], NSA_DH, 1))], axis=1)
    gates = _dot_hi(_sigmoid(gate_ref[0]), eg_ref[0])
    gw = g * NSA_DH
    out = (gates[:, 0:gw] * ocmp_ref[0].astype(F32)
           + gates[:, gw:2 * gw] * pair(o_slc) + gates[:, 2 * gw:3 * gw] * pair(o_win))
    o_ref[0] = out.astype(o_ref.dtype)


def nsa_main(q, ksw, vsw, sel, o_cmp, small, rel_bias, tq=256):
    b, t, _ = q.shape
    nsup = sel.shape[2]
    g = NSA_GROUP
    gw = g * NSA_DH
    sup_keys = SUP_BLOCKS * SLC_LEN
    oh = np.zeros((sup_keys, LANES), np.float32)
    oh[np.arange(sup_keys), NSA_DH + np.arange(sup_keys) // SLC_LEN] = 1.0
    qi = np.arange(tq)[:, None]
    c = np.arange(3 * tq)[None, :]
    dist = qi + 2 * tq - c
    tb = _bias_table(rel_bias, dist, np.ones_like(dist, bool))
    wm = np.zeros((3, tq, 3 * tq), np.float32)
    for var in range(3):
        exists = c >= tq * (2 - var)
        wm[var] = np.where((dist >= 0) & (dist < WINDOW) & exists, 0.0, NEG)
    eg = np.zeros((NSA_KV_HEADS, LANES, 3 * gw), np.float32)
    for k in range(NSA_KV_HEADS):
        for hh in range(g):
            for br in range(3):
                eg[k, (k * g + hh) * 3 + br, br * gw + hh * NSA_DH:br * gw + (hh + 1) * NSA_DH] = 1.0
    near = lambda off, col: pl.BlockSpec(
        (1, tq, LANES), lambda bi, k, i: (bi, jnp.maximum(i - off, 0), col + k))
    return pl.pallas_call(
        functools.partial(_nsa_main_kernel, tq=tq),
        grid=(b, NSA_KV_HEADS, t // tq),
        in_specs=[pl.BlockSpec((1, tq, gw), lambda bi, k, i: (bi, i, k)),
                  pl.BlockSpec((1, t, LANES), lambda bi, k, i: (bi, 0, k)),
                  pl.BlockSpec((1, t, LANES), lambda bi, k, i: (bi, 0, k)),
                  near(2, 2), near(1, 2), near(0, 2), near(2, 2), near(1, 2), near(0, 2),
                  pl.BlockSpec((1, 1, nsup, tq, LANES), lambda bi, k, i: (bi, k, 0, i, 0)),
                  _full((sup_keys, LANES)),
                  pl.BlockSpec((1, tq, gw), lambda bi, k, i: (bi, i, k)),
                  pl.BlockSpec((1, tq, LANES), lambda bi, k, i: (bi, i, 0)),
                  pl.BlockSpec((g, tq, 3 * tq), lambda bi, k, i: (k, 0, 0)),
                  _full((3, tq, 3 * tq)),
                  pl.BlockSpec((1, LANES, 3 * gw), lambda bi, k, i: (k, 0, 0))],
        out_specs=pl.BlockSpec((1, tq, gw), lambda bi, k, i: (bi, i, k)),
        out_shape=jax.ShapeDtypeStruct((b, t, NSA_W), BF16),
        compiler_params=_cparams(("arbitrary", "arbitrary", "arbitrary")),
        name="nsa_main",
    )(q, ksw, vsw, ksw, ksw, ksw, vsw, vsw, vsw, sel, jnp.asarray(oh, dtype=BF16), o_cmp, small, tb,
      jnp.asarray(wm), jnp.asarray(eg))


def _gla_kernel(qk_ref, v_ref, r_ref, sm_ref, wg_ref, bg_ref, on_ref, tril_ref, o_ref, s_scr, *, tc):
    c = GLA_CHUNK

    @pl.when(pl.program_id(1) == 0)
    def _():
        s_scr[...] = jnp.zeros_like(s_scr)

    kw = GLA_KW
    log_a = _log_sigmoid(_dot_hi(sm_ref[0], wg_ref[...]) + bg_ref[...]) * (1.0 / GLA_TAU)
    gcum = _dot_hi(tril_ref[...], log_a)
    q = qk_ref[0, :, 0:kw] * (GLA_DK ** -0.5)
    k = qk_ref[0, :, kw:2 * kw]
    q_dec = (q * jnp.exp(gcum)).astype(BF16)
    k_inv = (k * jnp.exp(-gcum)).astype(BF16)
    ri = lax.broadcasted_iota(jnp.int32, (c, c), 0)
    ci = lax.broadcasted_iota(jnp.int32, (c, c), 1)
    causal = ci <= ri
    lo = lax.broadcasted_iota(jnp.int32, (1, LANES), 1) < GLA_DK
    nchunk = tc // c
    heads = range(GLA_HEADS)
    zero = jnp.zeros((c, LANES), BF16)

    qm, vb, kd, egl = {}, {}, {}, {}
    for n in range(nchunk):
        sl = slice(n * c, (n + 1) * c)
        gl = gcum[n * c + c - 1:n * c + c, :]
        kdn = (k[sl] * jnp.exp(gl - gcum[sl])).astype(BF16)
        for h in heads:
            pr = slice((h // 2) * LANES, (h // 2 + 1) * LANES)
            keep = lo if h % 2 == 0 else jnp.logical_not(lo)
            qm[n, h] = jnp.where(keep, q_dec[sl, pr], zero)
            vb[n, h] = v_ref[0, sl, h * GLA_DV:(h + 1) * GLA_DV].astype(BF16)
            kd[n, h] = kdn[:, pr]
            egl[n, h] = jnp.exp(gl[:, pr])
    idx = [(n, h) for n in range(nchunk) for h in heads]
    attn = {i: jnp.where(causal, _dot_nt(qm[i], k_inv[i[0] * c:(i[0] + 1) * c, (i[1] // 2) * LANES:
                                                       (i[1] // 2 + 1) * LANES]), 0.0).astype(BF16) for i in idx}
    o_intra = {i: _dot(attn[i], vb[i]) for i in idx}
    kv = {i: _dot_tn(vb[i], kd[i]) for i in idx}

    st = [s_scr[h] for h in heads]
    for n in range(nchunk):
        sl = slice(n * c, (n + 1) * c)
        o = [o_intra[n, h] + _dot_nt(qm[n, h], st[h].astype(BF16)) for h in heads]
        st = [st[h] * egl[n, h] + kv[n, h] for h in heads]
        for h in heads:
            on = o[h] * lax.rsqrt(jnp.mean(o[h] * o[h], axis=-1, keepdims=True) + EPS) * on_ref[...]
            o_ref[0, sl, h * GLA_DV:(h + 1) * GLA_DV] = (
                on * _silu(r_ref[0, sl, h * GLA_DV:(h + 1) * GLA_DV])).astype(o_ref.dtype)
    for h in heads:
        s_scr[h] = st[h]


def gated_linear_attention(qkvr, small, wg_up, bg, on_gain, tc=512):
    b, t, _ = qkvr.shape
    wg = jnp.zeros((LANES, GLA_KW), F32).at[3 * NSA_HEADS:3 * NSA_HEADS + GLA_GATE_RANK].set(wg_up)
    idx = np.arange(tc)
    tril = ((idx[:, None] >= idx[None, :]) & (idx[:, None] // GLA_CHUNK == idx[None, :] // GLA_CHUNK))
    return pl.pallas_call(
        functools.partial(_gla_kernel, tc=tc),
        grid=(b, t // tc),
        in_specs=[pl.BlockSpec((1, tc, 2 * GLA_KW), lambda i, j: (i, j, 0)),
                  pl.BlockSpec((1, tc, GLA_W), lambda i, j: (i, j, 1)),
                  pl.BlockSpec((1, tc, GLA_W), lambda i, j: (i, j, 2)),
                  pl.BlockSpec((1, tc, LANES), lambda i, j: (i, j, 0)),
                  _full((LANES, GLA_KW)), _full((1, GLA_KW)), _full((1, GLA_DV)), _full((tc, tc))],
        out_specs=pl.BlockSpec((1, tc, GLA_W), lambda i, j: (i, j, 0)),
        out_shape=jax.ShapeDtypeStruct((b, t, GLA_W), BF16),
        scratch_shapes=[pltpu.VMEM((GLA_HEADS, GLA_DV, LANES), F32)],
        compiler_params=_cparams(("arbitrary", "arbitrary")),
        name="gla",
    )(qkvr, qkvr, qkvr, small, wg, bg.reshape(1, GLA_KW), on_gain.reshape(1, GLA_DV),
      jnp.asarray(tril.astype(np.float32)))


def odd_mixer(h, w_in, nsa_qn, nsa_kn, nsa_pos, nsa_cmp_w1, nsa_cmp_w2, gla_wg_up, gla_bg, gla_on, rel_bias):
    cuts = np.cumsum((0,) + OD_SIZES)
    col = lambda i: w_in[:, cuts[i]:cuts[i + 1]]
    dup = lambda a: jnp.concatenate([a[:, :NSA_DH], a[:, :NSA_DH], a[:, NSA_DH:], a[:, NSA_DH:]], axis=1)
    ep = _head_norm_epilogue(NSA_DH)
    bd = _block_diag_ones(NSA_W, NSA_DH)
    qg = (jnp.tile(nsa_qn, NSA_HEADS) * (NSA_DH ** -0.5 * LOG2E)).reshape(1, NSA_W)
    kg = jnp.tile(nsa_kn, NSA_HEADS).reshape(1, NSA_W)
    w_small = jnp.zeros((w_in.shape[0], LANES), F32)
    w_small = w_small.at[:, 0:24].set(col(7)).at[:, 24:40].set(col(11))
    nq, kcvc, ksw, vsw, small, qkvr = proj_multi(
        h, [(col(0), BF16, ep, (bd, qg)),
            (jnp.concatenate([col(1), col(2)], axis=1), F32, None, ()),
            (jnp.concatenate([dup(col(3)), dup(col(5))], axis=1), BF16, ep, (bd, kg)),
            (jnp.concatenate([dup(col(4)), dup(col(6))], axis=1), BF16, None, ()),
            (w_small, F32, None, ()),
            (jnp.concatenate([col(8), col(9), col(10), col(12)], axis=1), F32, None, ())], name="proj_odd")
    cmp_kv = nsa_compress(kcvc, nsa_pos, nsa_cmp_w1, nsa_cmp_w2, nsa_kn)
    o_cmp, sel = nsa_select(nq, cmp_kv, rel_bias)
    o_nsa = nsa_main(nq, ksw, vsw, sel, o_cmp, small, rel_bias)
    o_gla = gated_linear_attention(qkvr, small, gla_wg_up, gla_bg, gla_on)
    return o_nsa, o_gla


MOE_TM = 256
MOE_ROWS = 512


def _first_index(mask_val, idx, big, axis):
    return jnp.min(jnp.where(mask_val, idx, big), axis=axis, keepdims=True)


def _route_kernel(h_ref, rt_ref, b_ref, up_ref, eid_ref, rank_ref, w_ref, cnt_ref, run):
    tm = h_ref.shape[0]
    ne = N_EXPERTS
    gsz = ne // N_GROUPS

    @pl.when(pl.program_id(0) == 0)
    def _():
        run[...] = jnp.zeros_like(run)

    scores = _sigmoid(_dot_nt(rt_ref[...], h_ref[...], HI))
    biased = scores + b_ref[...]
    b3 = biased.reshape(N_GROUPS, gsz, tm)
    i3 = lax.broadcasted_iota(jnp.int32, (1, gsz, 1), 1).astype(F32)
    m1 = jnp.max(b3, axis=1, keepdims=True)
    f1 = _first_index(b3 == m1, i3, float(gsz), 1)
    m2 = jnp.max(jnp.where(i3 == f1, -jnp.inf, b3), axis=1, keepdims=True)
    gs = (m1 + m2).reshape(N_GROUPS, tm)
    gidx = lax.broadcasted_iota(jnp.int32, (N_GROUPS, 1), 0).astype(F32)
    gmask = jnp.zeros((N_GROUPS, tm), F32)
    for _ in range(TOPK_GROUPS):
        m = jnp.max(gs, axis=0, keepdims=True)
        pick = gidx == _first_index(gs == m, gidx, float(N_GROUPS), 0)
        gmask = jnp.where(pick, 1.0, gmask)
        gs = jnp.where(pick, -jnp.inf, gs)
    emask = jnp.broadcast_to(gmask.reshape(N_GROUPS, 1, tm), (N_GROUPS, gsz, tm)).reshape(ne, tm)
    work = jnp.where(emask > 0.5, biased, -jnp.inf)
    eidx = lax.broadcasted_iota(jnp.int32, (ne, 1), 0).astype(F32)
    picks, eids, ws = [], [], []
    for _ in range(TOP_K):
        m = jnp.max(work, axis=0, keepdims=True)
        first = _first_index(work == m, eidx, float(ne), 0)
        pick = eidx == first
        picks.append(pick)
        eids.append(first)
        ws.append(jnp.sum(jnp.where(pick, scores, 0.0), axis=0, keepdims=True))
        work = jnp.where(pick, -jnp.inf, work)
    wsum = ws[0]
    for k in range(1, TOP_K):
        wsum = wsum + ws[k]
    chosen = jnp.zeros((ne, tm), F32)
    for pick in picks:
        chosen = jnp.where(pick, 1.0, chosen)
    pos = run[...] + _dot(chosen.astype(BF16), up_ref[...])
    run[...] = run[...] + jnp.sum(chosen, axis=1, keepdims=True)
    cnt_ref[...] = run[...]
    row = lax.broadcasted_iota(jnp.int32, (8, 1), 0)
    eid_o = jnp.zeros((8, tm), F32)
    rank_o = jnp.zeros((8, tm), F32)
    w_o = jnp.zeros((LANES, tm), F32)
    rowl = lax.broadcasted_iota(jnp.int32, (LANES, 1), 0)
    for k in range(TOP_K):
        rk = jnp.sum(jnp.where(picks[k], pos, 0.0), axis=0, keepdims=True)
        eid_o = jnp.where(row == k, eids[k], eid_o)
        rank_o = jnp.where(row == k, rk, rank_o)
        w_o = jnp.where(rowl == k, ws[k] / wsum * ROUTE_SCALE, w_o)
    eid_ref[0] = eid_o.astype(jnp.int32)
    rank_ref[0] = rank_o.astype(jnp.int32)
    w_ref[...] = w_o.T


def moe_route(h2, router, e_bias, tm=MOE_TM):
    nt, d = h2.shape
    ne = N_EXPERTS
    up = jnp.asarray(np.triu(np.ones((tm, tm), np.float32), 1), dtype=BF16)
    nb = nt // tm
    return pl.pallas_call(
        _route_kernel,
        grid=(nb,),
        in_specs=[pl.BlockSpec((tm, d), lambda i: (i, 0)), _full((ne, d)), _full((ne, 1)), _full((tm, tm))],
        out_specs=[pl.BlockSpec((1, 8, tm), lambda i: (i, 0, 0)),
                   pl.BlockSpec((1, 8, tm), lambda i: (i, 0, 0)),
                   pl.BlockSpec((tm, LANES), lambda i: (i, 0)),
                   _full((ne, 1))],
        out_shape=[jax.ShapeDtypeStruct((nb, 8, tm), jnp.int32), jax.ShapeDtypeStruct((nb, 8, tm), jnp.int32),
                   jax.ShapeDtypeStruct((nt, LANES), F32), jax.ShapeDtypeStruct((ne, 1), F32)],
        scratch_shapes=[pltpu.VMEM((ne, 1), F32)],
        compiler_params=_cparams(("arbitrary",)),
        name="moe_route",
    )(h2, router.T, e_bias.reshape(ne, 1), up)


def _dispatch_kernel(dest_ref, h_ref, xs_ref, sem):
    tm = h_ref.shape[0]

    def copy(t, row):
        return pltpu.make_async_copy(h_ref.at[pl.ds(t, 1), :], xs_ref.at[pl.ds(row, 1), :], sem)

    def issue(t, _):
        for k in range(TOP_K):
            copy(t, dest_ref[0, k, t]).start()
        return 0

    def drain(t, _):
        for k in range(TOP_K):
            copy(0, 0).wait()
        return 0

    lax.fori_loop(0, tm, issue, 0, unroll=4)
    lax.fori_loop(0, tm, drain, 0, unroll=4)


def moe_dispatch(h2, dest, tm=MOE_TM):
    nt, d = h2.shape
    return pl.pallas_call(
        _dispatch_kernel,
        grid=(nt // tm,),
        in_specs=[pl.BlockSpec((1, 8, tm), lambda i: (i, 0, 0), memory_space=pltpu.SMEM),
                  pl.BlockSpec((tm, d), lambda i: (i, 0))],
        out_specs=pl.BlockSpec(memory_space=pl.ANY),
        scratch_shapes=[pltpu.SemaphoreType.DMA(())],
        out_shape=jax.ShapeDtypeStruct((nt * TOP_K, d), F32),
        compiler_params=_cparams(("arbitrary",)),
        name="moe_dispatch",
    )(dest, h2)


def _ffn_kernel(blk_ref, exp_ref, lo_ref, hi_ref, first_ref, valid_ref, x_ref, wg_ref, wu_ref, wd_ref, o_ref,
                wg_b, wu_b, wd_b):
    i = pl.program_id(0)
    rows = x_ref.shape[0]

    @pl.when((i == 0) | (exp_ref[i] != exp_ref[jnp.maximum(i - 1, 0)]))
    def _():
        wg_b[...] = wg_ref[0].astype(BF16)
        wu_b[...] = wu_ref[0].astype(BF16)
        wd_b[...] = wd_ref[0].astype(BF16)

    @pl.when(valid_ref[i] == 1)
    def _():
        x = x_ref[...].astype(BF16)
        a = _dot(x, wg_b[...])
        u = _dot(x, wu_b[...])
        y = _dot((_silu(a) * u).astype(BF16), wd_b[...])
        r = blk_ref[i] * rows + lax.broadcasted_iota(jnp.int32, (rows, 1), 0)
        y = jnp.where((r >= lo_ref[i]) & (r < hi_ref[i]), y, 0.0)

        @pl.when(first_ref[i] == 1)
        def _():
            o_ref[...] = y

        @pl.when(first_ref[i] == 0)
        def _():
            o_ref[...] = o_ref[...] + y


def _items_kernel(cnt_ref, starts_ref, blk_ref, exp_ref, lo_ref, hi_ref, first_ref, valid_ref, *, rows, n_items):
    shift = int(math.log2(rows))

    def expert(e, carry):
        start, n = carry
        c = cnt_ref[e]
        starts_ref[e] = start
        end = start + c
        first_blk = lax.shift_right_logical(start, shift)
        n_blk = jnp.where(c > 0, lax.shift_right_logical(jnp.maximum(end - 1, 0), shift) - first_blk + 1, 0)

        def item(k, n):
            b = first_blk + k
            lo = jnp.maximum(start, b * rows)
            blk_ref[n] = b
            exp_ref[n] = e
            lo_ref[n] = lo
            hi_ref[n] = jnp.minimum(end, (b + 1) * rows)
            first_ref[n] = (lo == b * rows).astype(jnp.int32)
            valid_ref[n] = 1
            return n + 1

        return end, lax.fori_loop(0, n_blk, item, n)

    _, total = lax.fori_loop(0, N_EXPERTS, expert, (jnp.int32(0), jnp.int32(0)))
    last = jnp.maximum(total - 1, 0)

    def fill(k, _):
        blk_ref[k] = blk_ref[last]
        exp_ref[k] = exp_ref[last]
        lo_ref[k] = 0
        hi_ref[k] = 0
        first_ref[k] = 0
        valid_ref[k] = 0
        return 0

    lax.fori_loop(total, n_items, fill, 0)


def _ffn_items(counts, n_rows, rows):
    n_items = n_rows // rows + N_EXPERTS - 1
    smem = pl.BlockSpec(memory_space=pltpu.SMEM)
    out = pl.pallas_call(
        functools.partial(_items_kernel, rows=rows, n_items=n_items),
        in_specs=[smem],
        out_specs=[smem] * 7,
        out_shape=[jax.ShapeDtypeStruct((N_EXPERTS,), jnp.int32)]
                  + [jax.ShapeDtypeStruct((n_items,), jnp.int32)] * 6,
        name="moe_items",
    )(counts)
    return out[0], tuple(out[1:])


def moe_ffn_sorted(xs, items, wg, wu, wd, layer, rows=MOE_ROWS):
    n_rows, d = xs.shape
    n_items = items[0].shape[0]
    de = wg.shape[-1]
    return pl.pallas_call(
        _ffn_kernel,
        grid_spec=pltpu.PrefetchScalarGridSpec(
            num_scalar_prefetch=6,
            grid=(n_items,),
            in_specs=[pl.BlockSpec((rows, d), lambda i, blk, e, *_: (blk[i], 0)),
                      pl.BlockSpec((None, 1, d, de), lambda i, blk, e, *_: (layer, e[i], 0, 0)),
                      pl.BlockSpec((None, 1, d, de), lambda i, blk, e, *_: (layer, e[i], 0, 0)),
                      pl.BlockSpec((None, 1, de, d), lambda i, blk, e, *_: (layer, e[i], 0, 0))],
            out_specs=pl.BlockSpec((rows, d), lambda i, blk, e, *_: (blk[i], 0)),
            scratch_shapes=[pltpu.VMEM((d, de), BF16), pltpu.VMEM((d, de), BF16), pltpu.VMEM((de, d), BF16)]),
        out_shape=jax.ShapeDtypeStruct((n_rows, d), F32),
        compiler_params=_cparams(("arbitrary",)),
        name="moe_ffn",
    )(*items, xs, wg, wu, wd)


def _combine_kernel(dest_ref, ys_ref, w_ref, h_ref, x_ref, g_ref, sg_ref, su_ref, sd_ref, o_ref, buf, sem):
    tm = h_ref.shape[0]

    def copy(t, k, row):
        return pltpu.make_async_copy(ys_ref.at[pl.ds(row, 1), :], buf.at[k, pl.ds(t, 1), :], sem)

    def issue(t, _):
        for k in range(TOP_K):
            copy(t, k, dest_ref[0, k, t]).start()
        return 0

    def drain(t, _):
        for k in range(TOP_K):
            copy(0, 0, 0).wait()
        return 0

    lax.fori_loop(0, tm, issue, 0, unroll=4)
    hb = h_ref[...].astype(BF16)
    y = _dot((_silu(_dot(hb, sg_ref[...])) * _dot(hb, su_ref[...])).astype(BF16), sd_ref[...])
    lax.fori_loop(0, tm, drain, 0, unroll=4)
    w = w_ref[...]
    for k in range(TOP_K):
        y = y + w[:, k:k + 1] * buf[k]
    o_ref[...] = x_ref[...] + g_ref[0] * y


def moe_combine(ys, dest, w, h2, x2, gate, sg, su, sd, seq, tm=MOE_TM):
    nt, d = h2.shape
    ds_ = sg.shape[-1]
    per_b = seq // tm
    tile = lambda: pl.BlockSpec((tm, d), lambda i: (i, 0))
    return pl.pallas_call(
        _combine_kernel,
        grid=(nt // tm,),
        in_specs=[pl.BlockSpec((1, 8, tm), lambda i: (i, 0, 0), memory_space=pltpu.SMEM),
                  pl.BlockSpec(memory_space=pl.ANY),
                  pl.BlockSpec((tm, LANES), lambda i: (i, 0)), tile(), tile(),
                  pl.BlockSpec((1, 1, d), lambda i: (i // per_b, 0, 0)),
                  _full((d, ds_)), _full((d, ds_)), _full((ds_, d))],
        out_specs=tile(),
        scratch_shapes=[pltpu.VMEM((TOP_K, tm, d), F32), pltpu.SemaphoreType.DMA(())],
        out_shape=jax.ShapeDtypeStruct((nt, d), F32),
        compiler_params=_cparams(("arbitrary",)),
        name="moe_combine",
    )(dest, ys, w, h2, x2, gate, sg.astype(BF16), su.astype(BF16), sd.astype(BF16))


def moe_layer(x, g_norm, sc, sh, gate, router, e_bias, wg, wu, wd, layer, sg, su, sd):
    b, t, d = x.shape
    nt = b * t
    h = ln_mod(x, g_norm, sc, sh, F32)
    h2 = h.reshape(nt, d)
    eid, rank, w, counts = moe_route(h2, router, e_bias)
    starts, items = _ffn_items(counts.reshape(-1).astype(jnp.int32), nt * TOP_K, MOE_ROWS)
    hit = eid[..., None] == jnp.arange(N_EXPERTS, dtype=jnp.int32)
    dest = jnp.sum(jnp.where(hit, starts.astype(jnp.int32), 0), axis=-1) + rank
    xs = moe_dispatch(h2, dest)
    ys = moe_ffn_sorted(xs, items, wg, wu, wd, layer)
    out = moe_combine(ys, dest, w, h2, x.reshape(nt, d), gate.reshape(b, 1, d), sg, su, sd, t)
    return out.reshape(b, t, d)


def kernel(x, c, ada_w, ada_b, norm_mix, norm_ffn, rel_bias, ev_w_in, ev_w_out, fox_fb, fox_qn, fox_kn, gdn_conv, gdn_a_log, gdn_dt_bias, gdn_on, od_w_in, od_w_out, nsa_qn, nsa_kn, nsa_pos, nsa_cmp_w1, nsa_cmp_w2, gla_wg_up, gla_bg, gla_on, moe_router, moe_bias, moe_wg, moe_wu, moe_wd, sh_wg, sh_wu, sh_wd):
    d = x.shape[-1]
    depth = ada_w.shape[0]
    mod = adaln(c, ada_w, ada_b)
    for layer in range(depth):
        sh1, sc1, g1, sh2, sc2, g2 = [mod[layer, :, i * d:(i + 1) * d] for i in range(6)]
        h = ln_mod(x, norm_mix[layer], sc1, sh1, BF16)
        j = layer // 2
        if layer % 2 == 0:
            y1, y2 = even_mixer(h, ev_w_in[j], fox_fb[j], fox_qn[j], fox_kn[j], gdn_conv[j], gdn_a_log[j],
                                gdn_dt_bias[j], gdn_on[j])
            w_out = ev_w_out[j]
        else:
            y1, y2 = odd_mixer(h, od_w_in[j], nsa_qn[j], nsa_kn[j], nsa_pos[j], nsa_cmp_w1[j], nsa_cmp_w2[j],
                               gla_wg_up[j], gla_bg[j], gla_on[j], rel_bias)
            w_out = od_w_out[j]
        x = out_proj(y1, y2, w_out, x, g1)
        x = moe_layer(x, norm_ffn[layer], sc2, sh2, g2, moe_router[layer], moe_bias[layer], moe_wg, moe_wu, moe_wd,
                      layer, sh_wg[layer], sh_wu[layer], sh_wd[layer])
    return x
```

```python
import functools
import math

import numpy as np
import jax
import jax.numpy as jnp
from jax import lax
from jax.experimental import pallas as pl
from jax.experimental.pallas import tpu as pltpu

F32 = jnp.float32
BF16 = jnp.bfloat16
HI = lax.Precision.HIGHEST

EPS = 1e-6
LOG2E = math.log2(math.e)
NEG = -1e30

FOX_HEADS, FOX_DH = 8, 64
GDN_HEADS, GDN_DH, GDN_CONV = 4, 128, 4
NSA_HEADS, NSA_KV_HEADS, NSA_DH = 8, 2, 64
NSA_GROUP = NSA_HEADS // NSA_KV_HEADS
CMP_LEN, CMP_STRIDE, CMP_HIDDEN = 32, 16, 256
SLC_LEN, SLC_TOPK, WINDOW = 64, 16, 512
GLA_HEADS, GLA_DK, GLA_DV, GLA_GATE_RANK, GLA_TAU, GLA_CHUNK = 4, 64, 128, 16, 16.0, 64
REL_BUCKETS, REL_MAX_DIST = 32, 128
N_EXPERTS, TOP_K, D_EXPERT, D_SHARED = 64, 6, 256, 256
N_GROUPS, TOPK_GROUPS, ROUTE_SCALE = 8, 4, 2.5

FOX_W = FOX_HEADS * FOX_DH
GDN_W = GDN_HEADS * GDN_DH
NSA_W = NSA_HEADS * NSA_DH
NSA_KV_W = NSA_KV_HEADS * NSA_DH
GLA_KW = GLA_HEADS * GLA_DK
GLA_W = GLA_HEADS * GLA_DV
EV_SIZES = (FOX_W, FOX_W, FOX_W, FOX_HEADS, 3 * GDN_W, GDN_HEADS, GDN_HEADS, GDN_W)
OD_SIZES = (NSA_W,) + (NSA_KV_W,) * 6 + (3 * NSA_HEADS, GLA_KW, GLA_KW, GLA_W, GLA_GATE_RANK, GLA_W)

LANES = 128
SUP_BLOCKS = 64
VMEM_LIMIT = 56 * 1024 * 1024


def _cparams(sem, flags=None):
    return pltpu.CompilerParams(dimension_semantics=sem, vmem_limit_bytes=VMEM_LIMIT, flags=flags)


def _full(shape):
    n = len(shape)
    return pl.BlockSpec(shape, lambda *_: (0,) * n)


def _dot(a, b):
    return jnp.dot(a, b, preferred_element_type=F32)


def _dot_hi(a, b):
    return jnp.dot(a, b, precision=HI, preferred_element_type=F32)


def _dot_nt(a, b, precision=None):
    return lax.dot_general(a, b, (((1,), (1,)), ((), ())), precision=precision, preferred_element_type=F32)


def _dot_tn(a, b, precision=None):
    return lax.dot_general(a, b, (((0,), (0,)), ((), ())), precision=precision, preferred_element_type=F32)


def _sigmoid(x):
    return 1.0 / (1.0 + jnp.exp(-x))


def _silu(x):
    return x * _sigmoid(x)


def _softplus(x):
    return jnp.maximum(x, 0.0) + jnp.log(1.0 + jnp.exp(-jnp.abs(x)))


def _log_sigmoid(x):
    return -_softplus(-x)


def _adaln_kernel(c_ref, w_ref, b_ref, o_ref):
    c = c_ref[...]
    o_ref[0] = _dot_hi(_silu(c), w_ref[0]) + b_ref[0]


def adaln(c, ada_w, ada_b):
    depth, d, n = ada_w.shape
    b = c.shape[0]
    cp = jnp.zeros((8, d), F32).at[:b].set(c)
    tn = 1536
    out = pl.pallas_call(
        _adaln_kernel,
        grid=(depth, n // tn),
        in_specs=[_full((8, d)),
                  pl.BlockSpec((1, d, tn), lambda l, j: (l, 0, j)),
                  pl.BlockSpec((1, 1, tn), lambda l, j: (l, 0, j))],
        out_specs=pl.BlockSpec((1, 8, tn), lambda l, j: (l, 0, j)),
        out_shape=jax.ShapeDtypeStruct((depth, 8, n), F32),
        compiler_params=_cparams(("arbitrary", "arbitrary")),
        name="adaln",
    )(cp, ada_w, ada_b.reshape(depth, 1, n))
    return out[:, :b]


def _ln_kernel(x_ref, g_ref, sc_ref, sh_ref, o_ref):
    x = x_ref[0]
    y = x * lax.rsqrt(jnp.mean(x * x, axis=-1, keepdims=True) + EPS) * g_ref[...]
    o_ref[0] = (y * (1.0 + sc_ref[0]) + sh_ref[0]).astype(o_ref.dtype)


def ln_mod(x, g, sc, sh, out_dtype, tm=512):
    b, t, d = x.shape
    return pl.pallas_call(
        _ln_kernel,
        grid=(b, t // tm),
        in_specs=[pl.BlockSpec((1, tm, d), lambda i, j: (i, j, 0)),
                  _full((1, d)),
                  pl.BlockSpec((1, 1, d), lambda i, j: (i, 0, 0)),
                  pl.BlockSpec((1, 1, d), lambda i, j: (i, 0, 0))],
        out_specs=pl.BlockSpec((1, tm, d), lambda i, j: (i, j, 0)),
        out_shape=jax.ShapeDtypeStruct((b, t, d), out_dtype),
        compiler_params=_cparams(("arbitrary", "arbitrary")),
        name="ln_mod",
    )(x, g.reshape(1, d), sc.reshape(b, 1, d), sh.reshape(b, 1, d))


def proj_multi(h, groups, tm=512, name="proj"):
    b, t, d = h.shape
    widths = [g[0].shape[1] for g in groups]
    starts = np.cumsum([0] + widths)
    w_cat = jnp.concatenate([g[0] for g in groups], axis=1).astype(BF16)
    extras = [e for g in groups for e in g[3]]
    n_ex = [len(g[3]) for g in groups]
    n_out = len(groups)

    def kern(h_ref, w_ref, *rest):
        ex_refs = rest[:len(extras)]
        o_refs = rest[len(extras):]
        y = _dot(h_ref[0], w_ref[...])
        pos = 0
        for gi, (_, out_dtype, epilogue, _) in enumerate(groups):
            yg = y[:, starts[gi]:starts[gi + 1]]
            if epilogue is not None:
                yg = epilogue(yg, *[e[...] for e in ex_refs[pos:pos + n_ex[gi]]])
            pos += n_ex[gi]
            o_refs[gi][0] = yg.astype(out_dtype)

    return pl.pallas_call(
        kern,
        grid=(b, t // tm),
        in_specs=[pl.BlockSpec((1, tm, d), lambda i, j: (i, j, 0)), _full((d, int(starts[-1])))]
                 + [_full(e.shape) for e in extras],
        out_specs=[pl.BlockSpec((1, tm, n), lambda i, j: (i, j, 0)) for n in widths],
        out_shape=[jax.ShapeDtypeStruct((b, t, n), g[1]) for n, g in zip(widths, groups)],
        compiler_params=_cparams(("arbitrary", "arbitrary")),
        name=name,
    )(h, w_cat, *extras)


def _head_norm_epilogue(dh):
    inv = 1.0 / dh

    def ep(y, bd, gain):
        ssq = _dot((y * y).astype(BF16), bd)
        return y * lax.rsqrt(ssq * inv + EPS) * gain

    return ep


def _block_diag_ones(n, dh):
    i = np.arange(n) // dh
    return jnp.asarray((i[:, None] == i[None, :]).astype(np.float32), dtype=BF16)


def _outproj_kernel(y1_ref, y2_ref, wa_ref, wb_ref, x_ref, g_ref, o_ref):
    y = _dot(y1_ref[0], wa_ref[...]) + _dot(y2_ref[0], wb_ref[...])
    o_ref[0] = x_ref[0] + g_ref[0] * y


def out_proj(y1, y2, w_out, x, gate, tm=512):
    b, t, d = x.shape
    n1, n2 = y1.shape[-1], y2.shape[-1]
    wa = w_out[:n1].astype(BF16)
    wb = w_out[n1:].astype(BF16)
    return pl.pallas_call(
        _outproj_kernel,
        grid=(b, t // tm),
        in_specs=[pl.BlockSpec((1, tm, n1), lambda i, j: (i, j, 0)),
                  pl.BlockSpec((1, tm, n2), lambda i, j: (i, j, 0)),
                  _full((n1, d)), _full((n2, d)),
                  pl.BlockSpec((1, tm, d), lambda i, j: (i, j, 0)),
                  pl.BlockSpec((1, 1, d), lambda i, j: (i, 0, 0))],
        out_specs=pl.BlockSpec((1, tm, d), lambda i, j: (i, j, 0)),
        out_shape=jax.ShapeDtypeStruct((b, t, d), F32),
        compiler_params=_cparams(("arbitrary", "arbitrary")),
        name="out_proj",
    )(y1, y2, wa, wb, x, gate.reshape(b, 1, d))


def _decay_kernel(s_ref, fb_ref, tril_ref, place_ref, o_ref, carry):
    @pl.when(pl.program_id(1) == 0)
    def _():
        carry[...] = jnp.zeros_like(carry)

    tm = s_ref.shape[1]
    lf = _log_sigmoid(s_ref[0] + fb_ref[...])
    cum = _dot_hi(tril_ref[...], lf) + carry[...]
    carry[...] = cum[tm - 1:tm, :]
    x = cum * LOG2E
    hi = x.astype(BF16)
    r1 = x - hi.astype(F32)
    mid = r1.astype(BF16)
    low = (r1 - mid.astype(F32)).astype(BF16)
    o_ref[0] = _dot(jnp.concatenate([hi, mid, low], axis=1), place_ref[...]).astype(o_ref.dtype)


def fox_decay(small, fox_fb, tm=512):
    b, t, _ = small.shape
    fb = jnp.zeros((1, LANES), F32).at[0, :FOX_HEADS].set(fox_fb)
    tril = jnp.asarray(np.tril(np.ones((tm, tm), np.float32)))
    place = np.zeros((3 * LANES, FOX_W), np.float32)
    for h in range(FOX_HEADS):
        for j in range(3):
            place[j * LANES + h, (h // 2) * LANES + (FOX_DH if h % 2 == 0 else 0) + j] = 1.0
    return pl.pallas_call(
        _decay_kernel,
        grid=(b, t // tm),
        in_specs=[pl.BlockSpec((1, tm, LANES), lambda i, j: (i, j, 0)), _full((1, LANES)), _full((tm, tm)),
                  _full((3 * LANES, FOX_W))],
        out_specs=pl.BlockSpec((1, tm, FOX_W), lambda i, j: (i, j, 0)),
        out_shape=jax.ShapeDtypeStruct((b, t, FOX_W), BF16),
        scratch_shapes=[pltpu.VMEM((1, LANES), F32)],
        compiler_params=_cparams(("arbitrary", "arbitrary")),
        name="fox_decay",
    )(small, fb, tril, jnp.asarray(place, dtype=BF16))


def _fox_kernel(q_ref, k_ref, v_ref, f_ref, o_ref, *, tq, wide):
    i = pl.program_id(2)
    lane = lax.broadcasted_iota(jnp.int32, (1, LANES), 1)
    lo = lane < FOX_DH
    coef = jnp.where((lane & (FOX_DH - 1)) < 3, -1.0, 0.0).astype(BF16)
    q = q_ref[0]
    qs = (jnp.where(lo, q, coef), jnp.where(lo, coef, q))
    causal = (lax.broadcasted_iota(jnp.int32, (tq, tq), 1) <= lax.broadcasted_iota(jnp.int32, (tq, tq), 0))
    one = jnp.ones((1, LANES), BF16)

    def step(j, carry, tw, diag=False):
        start = pl.multiple_of(j * tw, tw)
        kt = k_ref[0, pl.ds(start, tw), :]
        ft = f_ref[0, pl.ds(start, tw), :]
        vt = v_ref[0, pl.ds(start, tw), :]
        s_pair = (_dot_nt(qs[0], jnp.where(lo, kt, ft)), _dot_nt(qs[1], jnp.where(lo, ft, kt)))
        vs = (jnp.where(lo, vt, one), jnp.where(lo, one, vt))
        new = []
        for hh in range(2):
            m, acc = carry[hh]
            s = s_pair[hh]
            if diag:
                s = jnp.where(causal, s, NEG)
            m_new = jnp.maximum(m, jnp.max(s, axis=1, keepdims=True))
            p = jnp.exp2(s - m_new)
            acc = jnp.exp2(m - m_new) * acc + _dot(p.astype(BF16), vs[hh])
            new.append((m_new, acc))
        return tuple(new)

    carry = tuple((jnp.full((tq, 1), NEG, F32), jnp.zeros((tq, LANES), F32)) for _ in range(2))
    n_wide = i // wide
    carry = lax.fori_loop(0, n_wide, lambda j, c: step(j, c, wide * tq), carry)
    done = n_wide * wide
    part = wide // 2
    while part >= 1:
        carry = lax.cond((i & part) != 0, lambda c, d=done, w=part: step(d // w, c, w * tq), lambda c: c, carry)
        done = done + (i & part)
        part //= 2
    carry = step(i, carry, tq, diag=True)
    acc = jnp.where(lo, carry[0][1], carry[1][1])
    den = jnp.where(lo, carry[1][1], carry[0][1])
    o_ref[0] = (acc / pltpu.roll(den, FOX_DH, 1)).astype(o_ref.dtype)


def fox_attention(q, k, v, feat, tq=512, wide=4):
    b, t, w = q.shape
    npair = w // LANES
    nt = t // tq
    whole = lambda: pl.BlockSpec((1, t, LANES), lambda bi, p, i: (bi, 0, p))
    return pl.pallas_call(
        functools.partial(_fox_kernel, tq=tq, wide=wide),
        grid=(b, npair, nt),
        in_specs=[pl.BlockSpec((1, tq, LANES), lambda bi, p, i: (bi, i, p)), whole(), whole(), whole()],
        out_specs=pl.BlockSpec((1, tq, LANES), lambda bi, p, i: (bi, i, p)),
        out_shape=jax.ShapeDtypeStruct((b, t, w), BF16),
        compiler_params=_cparams(("arbitrary", "arbitrary", "arbitrary")),
        name="fox_attn",
    )(q, k, v, feat)


def _mm(a, b):
    return _dot(a.astype(BF16), b.astype(BF16))


def _mm3(a, b):
    ah = a.astype(BF16)
    bh = b.astype(BF16)
    al = (a - ah.astype(F32)).astype(BF16)
    bl = (b - bh.astype(F32)).astype(BF16)
    return _dot(jnp.concatenate([ah, ah, al], axis=1), jnp.concatenate([bh, bl, bh], axis=0))


def _tril_solve(a, rhs, ri, ci):
    n = a[0].shape[0]
    both = lambda f, x, y: [f(p, q) for p, q in zip(x, y)]
    eye = (ri == ci).astype(F32)
    same = lambda b: (lax.shift_right_logical(ri, int(math.log2(b)))
                      == lax.shift_right_logical(ci, int(math.log2(b))))
    base = 16
    d = [jnp.where(same(base), p, 0.0) for p in a]
    d2 = both(_mm, d, d)
    d4 = both(_mm, d2, d2)
    r1 = [eye - p + p2 - t for p, p2, t in zip(d, d2, both(_mm, d, d2))]
    d8 = both(_mm, d4, d4)
    r2 = [eye + p4 + p8 + t for p4, p8, t in zip(d4, d8, both(_mm, d4, d8))]
    t = both(_mm, r1, r2)
    b = base
    while b < n:
        join = same(2 * b) & jnp.logical_not(same(b))
        low = [jnp.where(join, p, 0.0) for p in a]
        t = [p - q for p, q in zip(t, both(_mm, both(_mm, t, low), t))]
        b *= 2
    return both(_mm3, t, rhs)


GDN_BLOCK = 128


def _gdn_kernel(x_ref, sm_ref, z_ref, cw_ref, ega_ref, egb_ref, alog_ref, dtb_ref, on_ref, tril_ref,
                o_ref, s_scr, prev_scr, *, tc):
    c = GDN_BLOCK
    w = GDN_W

    @pl.when(pl.program_id(1) == 0)
    def _():
        s_scr[...] = jnp.zeros_like(s_scr)
        prev_scr[...] = jnp.zeros_like(prev_scr)

    x = x_ref[0]
    prev = prev_scr[...]
    row8 = lax.broadcasted_iota(jnp.int32, (8, 1), 0)
    acc = x * cw_ref[GDN_CONV - 1:GDN_CONV, :]
    for s in range(1, GDN_CONV):
        rolled = pltpu.roll(x, s, 0)
        head = jnp.where(row8 < s, pltpu.roll(prev, s, 0), rolled[0:8])
        shifted = jnp.concatenate([head, rolled[8:]], axis=0)
        acc = acc + shifted * cw_ref[GDN_CONV - 1 - s:GDN_CONV - s, :]
    prev_scr[...] = x[tc - 8:tc]
    xc = _silu(acc)

    sm = sm_ref[0]
    g_raw = _dot_hi(sm, ega_ref[...])
    b_raw = _dot_hi(sm, egb_ref[...])
    g = -jnp.exp(alog_ref[...]) * _softplus(g_raw + dtb_ref[...])
    beta_all = _sigmoid(b_raw)
    gc_all = _dot_hi(tril_ref[...], g)

    ri = lax.broadcasted_iota(jnp.int32, (c, c), 0)
    ci = lax.broadcasted_iota(jnp.int32, (c, c), 1)
    causal = ci <= ri
    strict = ci < ri

    nblk = tc // c
    a_l, attn_l, rhs_l, qd_l, kd_l, egl_l = [], [], [], [], [], []
    for h in range(GDN_HEADS):
        ln = slice(h * GDN_DH, (h + 1) * GDN_DH)
        qh = xc[:, h * GDN_DH:(h + 1) * GDN_DH]
        kh = xc[:, w + h * GDN_DH:w + (h + 1) * GDN_DH]
        qh = qh * lax.rsqrt(jnp.sum(qh * qh, axis=-1, keepdims=True) + EPS) * (GDN_DH ** -0.5)
        kh = kh * lax.rsqrt(jnp.sum(kh * kh, axis=-1, keepdims=True) + EPS)
        vh = xc[:, 2 * w + h * GDN_DH:2 * w + (h + 1) * GDN_DH]
        gch = gc_all[:, ln]
        gct = gch.T
        egc = jnp.exp(gch)
        bh = beta_all[:, ln]
        for n in range(nblk):
            sl = slice(n * c, (n + 1) * c)
            q, k, v, gc, be = qh[sl], kh[sl], vh[sl], gch[sl], bh[sl]
            decay = jnp.exp(jnp.where(causal, gc - gct[:, sl], NEG))
            kb = k * be
            kk = _dot_nt(jnp.concatenate([kb, q], axis=0).astype(BF16), k.astype(BF16))
            a_l.append(jnp.where(strict, kk[:c] * decay, 0.0))
            attn_l.append(jnp.where(causal, kk[c:] * decay, 0.0))
            rhs_l.append(jnp.concatenate([v * be, kb * egc[sl]], axis=1))
            gl = gc[c - 1:c, :]
            qd_l.append(q * egc[sl])
            kd_l.append(k * jnp.exp(gl - gc))
            egl_l.append(jnp.exp(gl))
    uw_l = _tril_solve(a_l, rhs_l, ri, ci)

    states = [s_scr[h] for h in range(GDN_HEADS)]
    for n in range(nblk):
        sl = slice(n * c, (n + 1) * c)
        idx = [h * nblk + n for h in range(GDN_HEADS)]
        ws = [_mm(jnp.concatenate([uw_l[i][:, GDN_DH:], qd_l[i]], axis=0), states[h])
              for h, i in enumerate(idx)]
        v_new = [uw_l[i][:, :GDN_DH] - ws[h][:c] for h, i in enumerate(idx)]
        o = [ws[h][c:] + _mm(attn_l[i], v_new[h]) for h, i in enumerate(idx)]
        states = [states[h] * egl_l[i] + _dot_tn(kd_l[i].astype(BF16), v_new[h].astype(BF16))
                  for h, i in enumerate(idx)]
        for h in range(GDN_HEADS):
            ln = slice(h * GDN_DH, (h + 1) * GDN_DH)
            on = o[h] * lax.rsqrt(jnp.mean(o[h] * o[h], axis=-1, keepdims=True) + EPS) * on_ref[...]
            o_ref[0, sl, ln] = (on * _silu(z_ref[0, sl, ln])).astype(o_ref.dtype)
    for h in range(GDN_HEADS):
        s_scr[h] = states[h]


def gated_delta_net(x, small, z, conv_w, a_log, dt_bias, on_gain, tc=512):
    b, t, _ = x.shape
    w = GDN_W
    ega = np.zeros((LANES, w), np.float32)
    egb = np.zeros((LANES, w), np.float32)
    for h in range(GDN_HEADS):
        ega[FOX_HEADS + h, h * GDN_DH:(h + 1) * GDN_DH] = 1.0
        egb[FOX_HEADS + GDN_HEADS + h, h * GDN_DH:(h + 1) * GDN_DH] = 1.0
    alog = jnp.repeat(a_log, GDN_DH).reshape(1, w)
    dtb = jnp.repeat(dt_bias, GDN_DH).reshape(1, w)
    idx = np.arange(tc)
    tril = ((idx[:, None] >= idx[None, :]) & (idx[:, None] // GDN_BLOCK == idx[None, :] // GDN_BLOCK))
    row = lambda n: pl.BlockSpec((1, tc, n), lambda i, j: (i, j, 0))
    return pl.pallas_call(
        functools.partial(_gdn_kernel, tc=tc),
        grid=(b, t // tc),
        in_specs=[row(3 * w), row(LANES), row(w), _full((GDN_CONV, 3 * w)), _full((LANES, w)), _full((LANES, w)),
                  _full((1, w)), _full((1, w)), _full((1, GDN_DH)), _full((tc, tc))],
        out_specs=row(w),
        out_shape=jax.ShapeDtypeStruct((b, t, w), BF16),
        scratch_shapes=[pltpu.VMEM((GDN_HEADS, GDN_DH, GDN_DH), F32), pltpu.VMEM((8, 3 * w), F32)],
        compiler_params=_cparams(("arbitrary", "arbitrary")),
        name="gdn",
    )(x, small, z, conv_w, jnp.asarray(ega), jnp.asarray(egb), alog, dtb, on_gain.reshape(1, GDN_DH),
      jnp.asarray(tril.astype(np.float32)))


def even_mixer(h, w_in, fox_fb, fox_qn, fox_kn, gdn_conv, gdn_a_log, gdn_dt_bias, gdn_on):
    cuts = np.cumsum((0,) + EV_SIZES)
    col = lambda i: w_in[:, cuts[i]:cuts[i + 1]]
    bd = _block_diag_ones(FOX_W, FOX_DH)
    ep = _head_norm_epilogue(FOX_DH)
    qg = (jnp.tile(fox_qn, FOX_HEADS) * (FOX_DH ** -0.5 * LOG2E)).reshape(1, FOX_W)
    kg = jnp.tile(fox_kn, FOX_HEADS).reshape(1, FOX_W)
    w_small = jnp.zeros((w_in.shape[0], LANES), F32)
    w_small = w_small.at[:, 0:8].set(col(3)).at[:, 8:12].set(col(5)).at[:, 12:16].set(col(6))
    fq, fk, fv, small, gqkv, gz = proj_multi(
        h, [(col(0), BF16, ep, (bd, qg)), (col(1), BF16, ep, (bd, kg)), (col(2), BF16, None, ()),
            (w_small, F32, None, ()), (col(4), F32, None, ()), (col(7), F32, None, ())], name="proj_even")
    feat = fox_decay(small, fox_fb)
    o_fox = fox_attention(fq, fk, fv, feat)
    o_gdn = gated_delta_net(gqkv, small, gz, gdn_conv, gdn_a_log, gdn_dt_bias, gdn_on)
    return o_fox, o_gdn


def _t5_bucket_np(dist):
    n = np.maximum(dist, 0)
    exact = REL_BUCKETS // 2
    nf = np.maximum(n, 1).astype(np.float32)
    large = exact + (np.log(nf / np.float32(exact)) / np.float32(math.log(REL_MAX_DIST / exact))
                     * np.float32(REL_BUCKETS - exact)).astype(np.int32)
    large = np.minimum(large, REL_BUCKETS - 1)
    return np.where(n < exact, n, large)


def _bias_kernel(tbl_ref, bucket_ref, o_ref):
    h = pl.program_id(0)
    bucket = bucket_ref[...]
    acc = jnp.full(bucket.shape, NEG, F32)
    for b in range(REL_BUCKETS):
        acc = jnp.where(bucket == b, tbl_ref[b, h], acc)
    o_ref[0] = acc


def _bias_table(rel_bias, dist, valid):
    shifted = (rel_bias - rel_bias[REL_BUCKETS - 1:REL_BUCKETS]) * LOG2E
    bucket = np.where(valid, _t5_bucket_np(dist), -1).astype(np.int32)
    rows, cols = int(np.prod(bucket.shape[:-1])), bucket.shape[-1]
    nh = rel_bias.shape[1]
    tb = pl.pallas_call(
        _bias_kernel,
        grid=(nh,),
        in_specs=[pl.BlockSpec(memory_space=pltpu.SMEM), _full((rows, cols))],
        out_specs=pl.BlockSpec((1, rows, cols), lambda h: (h, 0, 0)),
        out_shape=jax.ShapeDtypeStruct((nh, rows, cols), F32),
        compiler_params=_cparams(("arbitrary",)),
        name="t5_bias",
    )(shifted, jnp.asarray(bucket.reshape(rows, cols)))
    return tb.reshape((nh,) + bucket.shape)


def _cmp_kernel(r_ref, pos_ref, w1_ref, w2_ref, kn_ref, o_ref):
    m = r_ref.shape[3]
    half = r_ref.shape[4]
    r = r_ref[0, 0, 0].astype(BF16)
    a = _dot(r, w1_ref[0, :half, :])
    bm = _dot(r, w1_ref[0, half:, :])
    c = _dot(pos_ref[0].astype(BF16), w1_ref[0])
    hid = a + pltpu.roll(bm, m - 1, 0) + c[0:1, :]
    out = _dot(_silu(hid).astype(BF16), w2_ref[0])
    normed = out * lax.rsqrt(jnp.mean(out * out, axis=-1, keepdims=True) + EPS) * kn_ref[...]
    o_ref[0, 0, 0] = jnp.where(pl.program_id(0) == 0, normed, out).astype(o_ref.dtype)


def nsa_compress(kcvc, pos, w1, w2, kn):
    b, t, _ = kcvc.shape
    m = t // CMP_STRIDE
    half = CMP_STRIDE * NSA_DH
    r = kcvc.reshape(b, m, CMP_STRIDE, 2, NSA_KV_HEADS, NSA_DH).transpose(3, 0, 4, 1, 2, 5).reshape(2, b, 2, m, half)
    posf = jnp.zeros((2, 8, 2 * half), F32).at[:, 0].set(pos.reshape(2, 2 * half))
    w2d = jnp.concatenate([w2, w2], axis=-1).astype(BF16)
    knd = jnp.tile(kn, 2).reshape(1, LANES)
    return pl.pallas_call(
        _cmp_kernel,
        grid=(2, b, NSA_KV_HEADS),
        in_specs=[pl.BlockSpec((1, 1, 1, m, half), lambda s, i, k: (s, i, k, 0, 0)),
                  pl.BlockSpec((1, 8, 2 * half), lambda s, i, k: (s, 0, 0)),
                  pl.BlockSpec((1, 2 * half, CMP_HIDDEN), lambda s, i, k: (s, 0, 0)),
                  pl.BlockSpec((1, CMP_HIDDEN, LANES), lambda s, i, k: (s, 0, 0)),
                  _full((1, LANES))],
        out_specs=pl.BlockSpec((1, 1, 1, m, LANES), lambda s, i, k: (s, i, k, 0, 0)),
        out_shape=jax.ShapeDtypeStruct((2, b, NSA_KV_HEADS, m, LANES), BF16),
        compiler_params=_cparams(("arbitrary", "arbitrary", "arbitrary")),
        name="nsa_compress",
    )(r, posf, w1.astype(BF16), w2d, knd)


def _dot_split(a, b):
    hi = a.astype(BF16)
    lo = (a - hi.astype(F32)).astype(BF16)
    return _dot(hi, b) + _dot(lo, b)


def _head_q(q_ref, hh, lo):
    blk = q_ref[0, :, (hh // 2) * LANES:(hh // 2 + 1) * LANES]
    keep = lo if hh % 2 == 0 else jnp.logical_not(lo)
    return jnp.where(keep, blk, jnp.zeros_like(blk))


def _pair_heads(o, lo):
    return jnp.concatenate([jnp.where(lo, o[0], o[1]), jnp.where(lo, o[2], o[3])], axis=1)


def _nsa_sel_kernel(q_ref, kc_ref, vc_ref, ov_ref, bt_ref, o_ref, sel_ref, *, tq, nband, n_slc):
    i = pl.program_id(2)
    ncp = kc_ref.shape[3]
    nsp = ov_ref.shape[1]
    per = tq // CMP_STRIDE
    var = jnp.minimum(i, 1)
    bs = pl.multiple_of(per * jnp.maximum(i - 1, 0), per)
    lo = lax.broadcasted_iota(jnp.int32, (1, LANES), 1) < NSA_DH
    kc = kc_ref[0, 0, 0]
    vc = vc_ref[0, 0, 0]
    kcb = kc_ref[0, 0, 0, pl.ds(bs, nband), :]
    vcb = vc_ref[0, 0, 0, pl.ds(bs, nband), :]
    far_ok = lax.broadcasted_iota(jnp.int32, (1, ncp), 1) < per * (i - 1)
    hs = range(NSA_GROUP)
    qh = [_head_q(q_ref, hh, lo) for hh in hs]
    s_far = [jnp.where(far_ok, _dot_nt(q, kc), NEG) for q in qh]
    s_band = [_dot_nt(qh[hh], kcb) + bt_ref[var, hh] for hh in hs]
    m = [jnp.maximum(jnp.max(a, axis=1, keepdims=True), jnp.max(b, axis=1, keepdims=True))
         for a, b in zip(s_far, s_band)]
    m = [jnp.where(x < 0.5 * NEG, 0.0, x) for x in m]
    p_far = [jnp.exp2(a - x) for a, x in zip(s_far, m)]
    p_band = [jnp.exp2(b - x) for b, x in zip(s_band, m)]
    l = [jnp.sum(a, axis=1, keepdims=True) + jnp.sum(b, axis=1, keepdims=True) for a, b in zip(p_far, p_band)]
    inv = [1.0 / jnp.where(x == 0.0, 1.0, x) for x in l]
    outs = [(_dot(a.astype(BF16), vc) + _dot(b.astype(BF16), vcb)) * x for a, b, x in zip(p_far, p_band, inv)]
    ps_far = p_far[0] * inv[0]
    ps_band = p_band[0] * inv[0]
    for hh in range(1, NSA_GROUP):
        ps_far = ps_far + p_far[hh] * inv[hh]
        ps_band = ps_band + p_band[hh] * inv[hh]
    o_ref[0] = _pair_heads(outs, lo).astype(o_ref.dtype)

    imp = _dot_split(ps_far, ov_ref[...]) + _dot_split(ps_band, ov_ref[pl.ds(bs, nband), :])
    blk = lax.broadcasted_iota(jnp.int32, (1, nsp), 1)
    blk_f = blk.astype(F32)
    qpos = i * tq + lax.broadcasted_iota(jnp.int32, (tq, 1), 0)
    cur = lax.shift_right_logical(qpos, int(math.log2(SLC_LEN)))
    forced = (blk == 0) | (blk == cur) | (blk == cur - 1)
    work = jnp.where(forced, -jnp.inf, jnp.where(blk <= cur, imp, NEG))
    work = jnp.where(blk < n_slc, work, -jnp.inf)
    ngrp = 4
    rg = tq // ngrp
    works = [work[r * rg:(r + 1) * rg] for r in range(ngrp)]
    sels = [jnp.where(forced[r * rg:(r + 1) * rg], 1.0, jnp.zeros((rg, nsp), F32)) for r in range(ngrp)]
    for _ in range(max(min(SLC_TOPK, n_slc) - 3, 0)):
        ms = [jnp.max(w, axis=1, keepdims=True) for w in works]
        firsts = [jnp.min(jnp.where(w == m, blk_f, float(nsp)), axis=1, keepdims=True) for w, m in zip(works, ms)]
        picks = [blk_f == f for f in firsts]
        sels = [jnp.where(p, 1.0, s) for p, s in zip(picks, sels)]
        works = [jnp.where(p, -jnp.inf, w) for p, w in zip(picks, works)]
    sel = jnp.concatenate(sels, axis=0)
    for sup in range(sel_ref.shape[2]):
        col = sel[:, (sup // 2) * LANES:(sup // 2 + 1) * LANES]
        if sup % 2 == 0:
            col = pltpu.roll(col, SUP_BLOCKS, 1)
        sel_ref[0, 0, sup] = jnp.where(lo, 0.0, jnp.where(col > 0.5, 0.0, NEG)).astype(sel_ref.dtype)


def nsa_select(q, cmp_kv, rel_bias, tq=512):
    b, t, _ = q.shape
    ncp = t // CMP_STRIDE
    n_cmp = ncp - 1
    n_slc = t // SLC_LEN
    nsp = max(LANES, n_slc)
    nsup = max(1, n_slc // SUP_BLOCKS)
    per = tq // CMP_STRIDE
    nband = 2 * per
    n = np.arange(ncp)[:, None]
    s = np.arange(nsp)[None, :]
    ov = ((CMP_STRIDE * n < SLC_LEN * s + SLC_LEN) & (CMP_STRIDE * n + CMP_LEN > SLC_LEN * s)
          & (n < n_cmp) & (s < n_slc)).astype(np.float32)
    qi = np.arange(tq)[:, None]
    nj = np.arange(nband)[None, :]
    end = CMP_STRIDE * nj + CMP_LEN - 1
    dist = np.stack([qi - end, tq + qi - end])
    bt = _bias_table(rel_bias, dist, dist >= 0)
    bt = bt.reshape(NSA_KV_HEADS, NSA_GROUP, 2, tq, nband).transpose(0, 2, 1, 3, 4)
    bt = bt.reshape(NSA_KV_HEADS * 2, NSA_GROUP, tq, nband)
    gw = NSA_GROUP * NSA_DH
    return pl.pallas_call(
        functools.partial(_nsa_sel_kernel, tq=tq, nband=nband, n_slc=n_slc),
        grid=(b, NSA_KV_HEADS, t // tq),
        in_specs=[pl.BlockSpec((1, tq, gw), lambda bi, k, i: (bi, i, k)),
                  pl.BlockSpec((1, 1, 1, ncp, LANES), lambda bi, k, i: (0, bi, k, 0, 0)),
                  pl.BlockSpec((1, 1, 1, ncp, LANES), lambda bi, k, i: (1, bi, k, 0, 0)),
                  _full((ncp, nsp)),
                  pl.BlockSpec((2, NSA_GROUP, tq, nband), lambda bi, k, i: (k, 0, 0, 0))],
        out_specs=[pl.BlockSpec((1, tq, gw), lambda bi, k, i: (bi, i, k)),
                   pl.BlockSpec((1, 1, nsup, tq, LANES), lambda bi, k, i: (bi, k, 0, i, 0))],
        out_shape=[jax.ShapeDtypeStruct((b, t, NSA_W), BF16),
                   jax.ShapeDtypeStruct((b, NSA_KV_HEADS, nsup, t, LANES), BF16)],
        compiler_params=_cparams(("arbitrary", "arbitrary", "arbitrary")),
        name="nsa_select",
    )(q, cmp_kv, cmp_kv, jnp.asarray(ov, dtype=BF16), bt)


def _nsa_main_kernel(q_ref, ks_ref, vs_ref, kw0_ref, kw1_ref, kw2_ref, vw0_ref, vw1_ref, vw2_ref, sel_ref, oh_ref,
                     ocmp_ref, gate_ref, tb_ref, wm_ref, eg_ref, o_ref, *, tq, wide):
    i = pl.program_id(2)
    g = NSA_GROUP
    lo = lax.broadcasted_iota(jnp.int32, (1, LANES), 1) < NSA_DH

    def head_low(hh):
        blk = q_ref[0, :, (hh // 2) * LANES:(hh // 2 + 1) * LANES]
        if hh % 2 == 1:
            blk = pltpu.roll(blk.astype(F32), NSA_DH, 1).astype(BF16)
        return jnp.where(lo, blk, jnp.zeros_like(blk))

    qst = jnp.concatenate([head_low(hh) for hh in range(g)], axis=0)
    causal = (lax.broadcasted_iota(jnp.int32, (tq, tq), 1) <= lax.broadcasted_iota(jnp.int32, (tq, tq), 0))
    one = jnp.ones((1, LANES), BF16)
    sup_keys = SUP_BLOCKS * SLC_LEN
    all_masked = jnp.where(lo, 0.0, NEG).astype(BF16)

    def sel_step(jt, carry, near, tw):
        m, acc = carry
        key0 = jnp.maximum(jt, 0) * tw
        start = pl.multiple_of(key0, tw)
        within = pl.multiple_of(key0 & (sup_keys - 1), tw)
        mq = sel_ref[0, 0, lax.shift_right_logical(key0, int(math.log2(sup_keys)))]
        mq = jnp.where(jt >= 0, mq, all_masked)
        q_ext = qst + jnp.concatenate([mq] * g, axis=0)
        kt = jnp.where(lo, ks_ref[0, pl.ds(start, tw), :], oh_ref[pl.ds(within, tw), :])
        vt = jnp.where(lo, vs_ref[0, pl.ds(start, tw), :], one)
        s = _dot_nt(q_ext, kt)
        if near is not None:
            s = s.reshape(g, tq, tw) + tb_ref[:, :, near * tq:(near + 1) * tq]
            if near == 2:
                s = jnp.where(causal[None], s, NEG)
            s = s.reshape(g * tq, tw)
        m_new = jnp.maximum(m, jnp.max(s, axis=1, keepdims=True))
        p = jnp.exp2(s - m_new)
        acc = jnp.exp2(m - m_new) * acc + _dot(p.astype(BF16), vt)
        return m_new, acc

    n_far = jnp.maximum(i - 2, 0)
    n_wide = n_far // wide
    carry = (jnp.full((g * tq, 1), NEG, F32), jnp.zeros((g * tq, LANES), F32))
    carry = lax.fori_loop(0, n_wide, lambda j, c: sel_step(j, c, None, wide * tq), carry)
    done = n_wide * wide
    part = wide // 2
    while part >= 1:
        carry = lax.cond((n_far & part) != 0, lambda c, d=done, w=part: sel_step(d // w, c, None, w * tq),
                         lambda c: c, carry)
        done = done + (n_far & part)
        part //= 2
    for near in range(3):
        carry = sel_step(i - 2 + near, carry, near, tq)
    o_slc = (carry[1] / pltpu.roll(carry[1], NSA_DH, 1)).reshape(g, tq, LANES)

    var = jnp.minimum(i, 2)
    kws = (kw0_ref, kw1_ref, kw2_ref)
    vws = (vw0_ref, vw1_ref, vw2_ref)
    sw = []
    for near in range(3):
        s = _dot_nt(qst, kws[near][0]).reshape(g, tq, tq)
        s = s + tb_ref[:, :, near * tq:(near + 1) * tq] + wm_ref[var, :, near * tq:(near + 1) * tq][None]
        sw.append(s.reshape(g * tq, tq))
    m = jnp.maximum(jnp.maximum(jnp.max(sw[0], axis=1, keepdims=True), jnp.max(sw[1], axis=1, keepdims=True)),
                    jnp.max(sw[2], axis=1, keepdims=True))
    acc = jnp.zeros((g * tq, LANES), F32)
    for near in range(3):
        p = jnp.exp2(sw[near] - m)
        acc = acc + _dot(p.astype(BF16), jnp.where(lo, vws[near][0], one))
    o_win = (acc / pltpu.roll(acc, NSA_DH, 1)).reshape(g, tq, LANES)

    pair = lambda o: jnp.concatenate([jnp.where(lo, o[0], pltpu.roll(o[1], NSA_DH, 1)),
                                      jnp.where(lo, o[2], pltpu.roll(o[3], NSA_DH, 1))], axis=1)
    gates = _dot_hi(_sigmoid(gate_ref[0]), eg_ref[0])
    gw = g * NSA_DH
    out = (gates[:, 0:gw] * ocmp_ref[0].astype(F32)
           + gates[:, gw:2 * gw] * pair(o_slc) + gates[:, 2 * gw:3 * gw] * pair(o_win))
    o_ref[0] = out.astype(o_ref.dtype)


def nsa_main(q, ksw, vsw, sel, o_cmp, small, rel_bias, tq=256, wide=8):
    b, t, _ = q.shape
    nsup = sel.shape[2]
    g = NSA_GROUP
    gw = g * NSA_DH
    sup_keys = SUP_BLOCKS * SLC_LEN
    oh = np.zeros((sup_keys, LANES), np.float32)
    oh[np.arange(sup_keys), NSA_DH + np.arange(sup_keys) // SLC_LEN] = 1.0
    qi = np.arange(tq)[:, None]
    c = np.arange(3 * tq)[None, :]
    dist = qi + 2 * tq - c
    tb = _bias_table(rel_bias, dist, np.ones_like(dist, bool))
    wm = np.zeros((3, tq, 3 * tq), np.float32)
    for var in range(3):
        exists = c >= tq * (2 - var)
        wm[var] = np.where((dist >= 0) & (dist < WINDOW) & exists, 0.0, NEG)
    eg = np.zeros((NSA_KV_HEADS, LANES, 3 * gw), np.float32)
    for k in range(NSA_KV_HEADS):
        for hh in range(g):
            for br in range(3):
                eg[k, (k * g + hh) * 3 + br, br * gw + hh * NSA_DH:br * gw + (hh + 1) * NSA_DH] = 1.0
    near = lambda off, col: pl.BlockSpec(
        (1, tq, LANES), lambda bi, k, i: (bi, jnp.maximum(i - off, 0), col + k))
    return pl.pallas_call(
        functools.partial(_nsa_main_kernel, tq=tq, wide=wide),
        grid=(b, NSA_KV_HEADS, t // tq),
        in_specs=[pl.BlockSpec((1, tq, gw), lambda bi, k, i: (bi, i, k)),
                  pl.BlockSpec((1, t, LANES), lambda bi, k, i: (bi, 0, k)),
                  pl.BlockSpec((1, t, LANES), lambda bi, k, i: (bi, 0, k)),
                  near(2, 2), near(1, 2), near(0, 2), near(2, 2), near(1, 2), near(0, 2),
                  pl.BlockSpec((1, 1, nsup, tq, LANES), lambda bi, k, i: (bi, k, 0, i, 0)),
                  _full((sup_keys, LANES)),
                  pl.BlockSpec((1, tq, gw), lambda bi, k, i: (bi, i, k)),
                  pl.BlockSpec((1, tq, LANES), lambda bi, k, i: (bi, i, 0)),
                  pl.BlockSpec((g, tq, 3 * tq), lambda bi, k, i: (k, 0, 0)),
                  _full((3, tq, 3 * tq)),
                  pl.BlockSpec((1, LANES, 3 * gw), lambda bi, k, i: (k, 0, 0))],
        out_specs=pl.BlockSpec((1, tq, gw), lambda bi, k, i: (bi, i, k)),
        out_shape=jax.ShapeDtypeStruct((b, t, NSA_W), BF16),
        compiler_params=_cparams(("arbitrary", "arbitrary", "arbitrary")),
        name="nsa_main",
    )(q, ksw, vsw, ksw, ksw, ksw, vsw, vsw, vsw, sel, jnp.asarray(oh, dtype=BF16), o_cmp, small, tb,
      jnp.asarray(wm), jnp.asarray(eg))


def _gla_kernel(qk_ref, v_ref, r_ref, sm_ref, wg_ref, bg_ref, on_ref, tril_ref, o_ref, s_scr, *, tc):
    c = GLA_CHUNK

    @pl.when(pl.program_id(1) == 0)
    def _():
        s_scr[...] = jnp.zeros_like(s_scr)

    kw = GLA_KW
    log_a = _log_sigmoid(_dot_hi(sm_ref[0], wg_ref[...]) + bg_ref[...]) * (1.0 / GLA_TAU)
    gcum = _dot_hi(tril_ref[...], log_a)
    q = qk_ref[0, :, 0:kw] * (GLA_DK ** -0.5)
    k = qk_ref[0, :, kw:2 * kw]
    q_dec = (q * jnp.exp(gcum)).astype(BF16)
    k_inv = (k * jnp.exp(-gcum)).astype(BF16)
    ri = lax.broadcasted_iota(jnp.int32, (c, c), 0)
    ci = lax.broadcasted_iota(jnp.int32, (c, c), 1)
    causal = ci <= ri
    lo = lax.broadcasted_iota(jnp.int32, (1, LANES), 1) < GLA_DK
    nchunk = tc // c
    heads = range(GLA_HEADS)
    zero = jnp.zeros((c, LANES), BF16)

    qm, vb, kd, egl = {}, {}, {}, {}
    for n in range(nchunk):
        sl = slice(n * c, (n + 1) * c)
        gl = gcum[n * c + c - 1:n * c + c, :]
        kdn = (k[sl] * jnp.exp(gl - gcum[sl])).astype(BF16)
        for h in heads:
            pr = slice((h // 2) * LANES, (h // 2 + 1) * LANES)
            keep = lo if h % 2 == 0 else jnp.logical_not(lo)
            qm[n, h] = jnp.where(keep, q_dec[sl, pr], zero)
            vb[n, h] = v_ref[0, sl, h * GLA_DV:(h + 1) * GLA_DV].astype(BF16)
            kd[n, h] = kdn[:, pr]
            egl[n, h] = jnp.exp(gl[:, pr])
    idx = [(n, h) for n in range(nchunk) for h in heads]
    attn = {i: jnp.where(causal, _dot_nt(qm[i], k_inv[i[0] * c:(i[0] + 1) * c, (i[1] // 2) * LANES:
                                                       (i[1] // 2 + 1) * LANES]), 0.0).astype(BF16) for i in idx}
    o_intra = {i: _dot(attn[i], vb[i]) for i in idx}
    kv = {i: _dot_tn(vb[i], kd[i]) for i in idx}

    st = [s_scr[h] for h in heads]
    for n in range(nchunk):
        sl = slice(n * c, (n + 1) * c)
        o = [o_intra[n, h] + _dot_nt(qm[n, h], st[h].astype(BF16)) for h in heads]
        st = [st[h] * egl[n, h] + kv[n, h] for h in heads]
        for h in heads:
            on = o[h] * lax.rsqrt(jnp.mean(o[h] * o[h], axis=-1, keepdims=True) + EPS) * on_ref[...]
            o_ref[0, sl, h * GLA_DV:(h + 1) * GLA_DV] = (
                on * _silu(r_ref[0, sl, h * GLA_DV:(h + 1) * GLA_DV])).astype(o_ref.dtype)
    for h in heads:
        s_scr[h] = st[h]


def gated_linear_attention(qkvr, small, wg_up, bg, on_gain, tc=512):
    b, t, _ = qkvr.shape
    wg = jnp.zeros((LANES, GLA_KW), F32).at[3 * NSA_HEADS:3 * NSA_HEADS + GLA_GATE_RANK].set(wg_up)
    idx = np.arange(tc)
    tril = ((idx[:, None] >= idx[None, :]) & (idx[:, None] // GLA_CHUNK == idx[None, :] // GLA_CHUNK))
    return pl.pallas_call(
        functools.partial(_gla_kernel, tc=tc),
        grid=(b, t // tc),
        in_specs=[pl.BlockSpec((1, tc, 2 * GLA_KW), lambda i, j: (i, j, 0)),
                  pl.BlockSpec((1, tc, GLA_W), lambda i, j: (i, j, 1)),
                  pl.BlockSpec((1, tc, GLA_W), lambda i, j: (i, j, 2)),
                  pl.BlockSpec((1, tc, LANES), lambda i, j: (i, j, 0)),
                  _full((LANES, GLA_KW)), _full((1, GLA_KW)), _full((1, GLA_DV)), _full((tc, tc))],
        out_specs=pl.BlockSpec((1, tc, GLA_W), lambda i, j: (i, j, 0)),
        out_shape=jax.ShapeDtypeStruct((b, t, GLA_W), BF16),
        scratch_shapes=[pltpu.VMEM((GLA_HEADS, GLA_DV, LANES), F32)],
        compiler_params=_cparams(("arbitrary", "arbitrary")),
        name="gla",
    )(qkvr, qkvr, qkvr, small, wg, bg.reshape(1, GLA_KW), on_gain.reshape(1, GLA_DV),
      jnp.asarray(tril.astype(np.float32)))


def odd_mixer(h, w_in, nsa_qn, nsa_kn, nsa_pos, nsa_cmp_w1, nsa_cmp_w2, gla_wg_up, gla_bg, gla_on, rel_bias):
    cuts = np.cumsum((0,) + OD_SIZES)
    col = lambda i: w_in[:, cuts[i]:cuts[i + 1]]
    dup = lambda a: jnp.concatenate([a[:, :NSA_DH], a[:, :NSA_DH], a[:, NSA_DH:], a[:, NSA_DH:]], axis=1)
    ep = _head_norm_epilogue(NSA_DH)
    bd = _block_diag_ones(NSA_W, NSA_DH)
    qg = (jnp.tile(nsa_qn, NSA_HEADS) * (NSA_DH ** -0.5 * LOG2E)).reshape(1, NSA_W)
    kg = jnp.tile(nsa_kn, NSA_HEADS).reshape(1, NSA_W)
    w_small = jnp.zeros((w_in.shape[0], LANES), F32)
    w_small = w_small.at[:, 0:24].set(col(7)).at[:, 24:40].set(col(11))
    nq, kcvc, ksw, vsw, small, qkvr = proj_multi(
        h, [(col(0), BF16, ep, (bd, qg)),
            (jnp.concatenate([col(1), col(2)], axis=1), F32, None, ()),
            (jnp.concatenate([dup(col(3)), dup(col(5))], axis=1), BF16, ep, (bd, kg)),
            (jnp.concatenate([dup(col(4)), dup(col(6))], axis=1), BF16, None, ()),
            (w_small, F32, None, ()),
            (jnp.concatenate([col(8), col(9), col(10), col(12)], axis=1), F32, None, ())], name="proj_odd")
    cmp_kv = nsa_compress(kcvc, nsa_pos, nsa_cmp_w1, nsa_cmp_w2, nsa_kn)
    o_cmp, sel = nsa_select(nq, cmp_kv, rel_bias)
    o_nsa = nsa_main(nq, ksw, vsw, sel, o_cmp, small, rel_bias)
    o_gla = gated_linear_attention(qkvr, small, gla_wg_up, gla_bg, gla_on)
    return o_nsa, o_gla


MOE_TM = 256
MOE_ROWS = 512


def _first_index(mask_val, idx, big, axis):
    return jnp.min(jnp.where(mask_val, idx, big), axis=axis, keepdims=True)


def _route_kernel(h_ref, rt_ref, b_ref, up_ref, eid_ref, rank_ref, w_ref, cnt_ref, run):
    tm = h_ref.shape[0]
    ne = N_EXPERTS
    gsz = ne // N_GROUPS

    @pl.when(pl.program_id(0) == 0)
    def _():
        run[...] = jnp.zeros_like(run)

    scores = _sigmoid(_dot_nt(rt_ref[...], h_ref[...], HI))
    biased = scores + b_ref[...]
    b3 = biased.reshape(N_GROUPS, gsz, tm)
    i3 = lax.broadcasted_iota(jnp.int32, (1, gsz, 1), 1).astype(F32)
    m1 = jnp.max(b3, axis=1, keepdims=True)
    f1 = _first_index(b3 == m1, i3, float(gsz), 1)
    m2 = jnp.max(jnp.where(i3 == f1, -jnp.inf, b3), axis=1, keepdims=True)
    gs = (m1 + m2).reshape(N_GROUPS, tm)
    gidx = lax.broadcasted_iota(jnp.int32, (N_GROUPS, 1), 0).astype(F32)
    gmask = jnp.zeros((N_GROUPS, tm), F32)
    for _ in range(TOPK_GROUPS):
        m = jnp.max(gs, axis=0, keepdims=True)
        pick = gidx == _first_index(gs == m, gidx, float(N_GROUPS), 0)
        gmask = jnp.where(pick, 1.0, gmask)
        gs = jnp.where(pick, -jnp.inf, gs)
    emask = jnp.broadcast_to(gmask.reshape(N_GROUPS, 1, tm), (N_GROUPS, gsz, tm)).reshape(ne, tm)
    work = jnp.where(emask > 0.5, biased, -jnp.inf)
    eidx = lax.broadcasted_iota(jnp.int32, (ne, 1), 0).astype(F32)
    picks, eids, ws = [], [], []
    for _ in range(TOP_K):
        m = jnp.max(work, axis=0, keepdims=True)
        first = _first_index(work == m, eidx, float(ne), 0)
        pick = eidx == first
        picks.append(pick)
        eids.append(first)
        ws.append(jnp.sum(jnp.where(pick, scores, 0.0), axis=0, keepdims=True))
        work = jnp.where(pick, -jnp.inf, work)
    wsum = ws[0]
    for k in range(1, TOP_K):
        wsum = wsum + ws[k]
    chosen = jnp.zeros((ne, tm), F32)
    for pick in picks:
        chosen = jnp.where(pick, 1.0, chosen)
    pos = run[...] + _dot(chosen.astype(BF16), up_ref[...])
    run[...] = run[...] + jnp.sum(chosen, axis=1, keepdims=True)
    cnt_ref[...] = run[...]
    row = lax.broadcasted_iota(jnp.int32, (8, 1), 0)
    eid_o = jnp.zeros((8, tm), F32)
    rank_o = jnp.zeros((8, tm), F32)
    w_o = jnp.zeros((LANES, tm), F32)
    rowl = lax.broadcasted_iota(jnp.int32, (LANES, 1), 0)
    for k in range(TOP_K):
        rk = jnp.sum(jnp.where(picks[k], pos, 0.0), axis=0, keepdims=True)
        eid_o = jnp.where(row == k, eids[k], eid_o)
        rank_o = jnp.where(row == k, rk, rank_o)
        w_o = jnp.where(rowl == k, ws[k] / wsum * ROUTE_SCALE, w_o)
    eid_ref[0] = eid_o.astype(jnp.int32)
    rank_ref[0] = rank_o.astype(jnp.int32)
    w_ref[...] = w_o.T


def moe_route(h2, router, e_bias, tm=MOE_TM):
    nt, d = h2.shape
    ne = N_EXPERTS
    up = jnp.asarray(np.triu(np.ones((tm, tm), np.float32), 1), dtype=BF16)
    nb = nt // tm
    return pl.pallas_call(
        _route_kernel,
        grid=(nb,),
        in_specs=[pl.BlockSpec((tm, d), lambda i: (i, 0)), _full((ne, d)), _full((ne, 1)), _full((tm, tm))],
        out_specs=[pl.BlockSpec((1, 8, tm), lambda i: (i, 0, 0)),
                   pl.BlockSpec((1, 8, tm), lambda i: (i, 0, 0)),
                   pl.BlockSpec((tm, LANES), lambda i: (i, 0)),
                   _full((ne, 1))],
        out_shape=[jax.ShapeDtypeStruct((nb, 8, tm), jnp.int32), jax.ShapeDtypeStruct((nb, 8, tm), jnp.int32),
                   jax.ShapeDtypeStruct((nt, LANES), F32), jax.ShapeDtypeStruct((ne, 1), F32)],
        scratch_shapes=[pltpu.VMEM((ne, 1), F32)],
        compiler_params=_cparams(("arbitrary",)),
        name="moe_route",
    )(h2, router.T, e_bias.reshape(ne, 1), up)


def _dispatch_kernel(dest_ref, h_ref, xs_ref, sem):
    tm = h_ref.shape[0]

    def copy(t, row):
        return pltpu.make_async_copy(h_ref.at[pl.ds(t, 1), :], xs_ref.at[pl.ds(row, 1), :], sem)

    def issue(t, _):
        for k in range(TOP_K):
            copy(t, dest_ref[0, k, t]).start()
        return 0

    def drain(t, _):
        for k in range(TOP_K):
            copy(0, 0).wait()
        return 0

    lax.fori_loop(0, tm, issue, 0, unroll=4)
    lax.fori_loop(0, tm, drain, 0, unroll=4)


def moe_dispatch(h2, dest, tm=MOE_TM):
    nt, d = h2.shape
    return pl.pallas_call(
        _dispatch_kernel,
        grid=(nt // tm,),
        in_specs=[pl.BlockSpec((1, 8, tm), lambda i: (i, 0, 0), memory_space=pltpu.SMEM),
                  pl.BlockSpec((tm, d), lambda i: (i, 0))],
        out_specs=pl.BlockSpec(memory_space=pl.ANY),
        scratch_shapes=[pltpu.SemaphoreType.DMA(())],
        out_shape=jax.ShapeDtypeStruct((nt * TOP_K, d), F32),
        compiler_params=_cparams(("arbitrary",)),
        name="moe_dispatch",
    )(dest, h2)


def _ffn_kernel(blk_ref, exp_ref, lo_ref, hi_ref, first_ref, valid_ref, x_ref, wg_ref, wu_ref, wd_ref, o_ref,
                wg_b, wu_b, wd_b):
    i = pl.program_id(0)
    rows = x_ref.shape[0]

    @pl.when((i == 0) | (exp_ref[i] != exp_ref[jnp.maximum(i - 1, 0)]))
    def _():
        wg_b[...] = wg_ref[0].astype(BF16)
        wu_b[...] = wu_ref[0].astype(BF16)
        wd_b[...] = wd_ref[0].astype(BF16)

    @pl.when(valid_ref[i] == 1)
    def _():
        x = x_ref[...].astype(BF16)
        a = _dot(x, wg_b[...])
        u = _dot(x, wu_b[...])
        y = _dot((_silu(a) * u).astype(BF16), wd_b[...])
        r = blk_ref[i] * rows + lax.broadcasted_iota(jnp.int32, (rows, 1), 0)
        y = jnp.where((r >= lo_ref[i]) & (r < hi_ref[i]), y, 0.0)

        @pl.when(first_ref[i] == 1)
        def _():
            o_ref[...] = y

        @pl.when(first_ref[i] == 0)
        def _():
            o_ref[...] = o_ref[...] + y


def _items_kernel(cnt_ref, starts_ref, blk_ref, exp_ref, lo_ref, hi_ref, first_ref, valid_ref, *, rows, n_items):
    shift = int(math.log2(rows))

    def expert(e, carry):
        start, n = carry
        c = cnt_ref[e]
        starts_ref[e] = start
        end = start + c
        first_blk = lax.shift_right_logical(start, shift)
        n_blk = jnp.where(c > 0, lax.shift_right_logical(jnp.maximum(end - 1, 0), shift) - first_blk + 1, 0)

        def item(k, n):
            b = first_blk + k
            lo = jnp.maximum(start, b * rows)
            blk_ref[n] = b
            exp_ref[n] = e
            lo_ref[n] = lo
            hi_ref[n] = jnp.minimum(end, (b + 1) * rows)
            first_ref[n] = (lo == b * rows).astype(jnp.int32)
            valid_ref[n] = 1
            return n + 1

        return end, lax.fori_loop(0, n_blk, item, n)

    _, total = lax.fori_loop(0, N_EXPERTS, expert, (jnp.int32(0), jnp.int32(0)))
    last = jnp.maximum(total - 1, 0)

    def fill(k, _):
        blk_ref[k] = blk_ref[last]
        exp_ref[k] = exp_ref[last]
        lo_ref[k] = 0
        hi_ref[k] = 0
        first_ref[k] = 0
        valid_ref[k] = 0
        return 0

    lax.fori_loop(total, n_items, fill, 0)


def _ffn_items(counts, n_rows, rows):
    n_items = n_rows // rows + N_EXPERTS - 1
    smem = pl.BlockSpec(memory_space=pltpu.SMEM)
    out = pl.pallas_call(
        functools.partial(_items_kernel, rows=rows, n_items=n_items),
        in_specs=[smem],
        out_specs=[smem] * 7,
        out_shape=[jax.ShapeDtypeStruct((N_EXPERTS,), jnp.int32)]
                  + [jax.ShapeDtypeStruct((n_items,), jnp.int32)] * 6,
        name="moe_items",
    )(counts)
    return out[0], tuple(out[1:])


def moe_ffn_sorted(xs, items, wg, wu, wd, layer, rows=MOE_ROWS):
    n_rows, d = xs.shape
    n_items = items[0].shape[0]
    de = wg.shape[-1]
    return pl.pallas_call(
        _ffn_kernel,
        grid_spec=pltpu.PrefetchScalarGridSpec(
            num_scalar_prefetch=6,
            grid=(n_items,),
            in_specs=[pl.BlockSpec((rows, d), lambda i, blk, e, *_: (blk[i], 0)),
                      pl.BlockSpec((None, 1, d, de), lambda i, blk, e, *_: (layer, e[i], 0, 0)),
                      pl.BlockSpec((None, 1, d, de), lambda i, blk, e, *_: (layer, e[i], 0, 0)),
                      pl.BlockSpec((None, 1, de, d), lambda i, blk, e, *_: (layer, e[i], 0, 0))],
            out_specs=pl.BlockSpec((rows, d), lambda i, blk, e, *_: (blk[i], 0)),
            scratch_shapes=[pltpu.VMEM((d, de), BF16), pltpu.VMEM((d, de), BF16), pltpu.VMEM((de, d), BF16)]),
        out_shape=jax.ShapeDtypeStruct((n_rows, d), F32),
        compiler_params=_cparams(("arbitrary",)),
        name="moe_ffn",
    )(*items, xs, wg, wu, wd)


def _combine_kernel(dest_ref, ys_ref, w_ref, h_ref, x_ref, g_ref, sg_ref, su_ref, sd_ref, o_ref, buf, sem):
    tm = h_ref.shape[0]

    def copy(t, k, row):
        return pltpu.make_async_copy(ys_ref.at[pl.ds(row, 1), :], buf.at[k, pl.ds(t, 1), :], sem)

    def issue(t, _):
        for k in range(TOP_K):
            copy(t, k, dest_ref[0, k, t]).start()
        return 0

    def drain(t, _):
        for k in range(TOP_K):
            copy(0, 0, 0).wait()
        return 0

    lax.fori_loop(0, tm, issue, 0, unroll=4)
    hb = h_ref[...].astype(BF16)
    y = _dot((_silu(_dot(hb, sg_ref[...])) * _dot(hb, su_ref[...])).astype(BF16), sd_ref[...])
    lax.fori_loop(0, tm, drain, 0, unroll=4)
    w = w_ref[...]
    for k in range(TOP_K):
        y = y + w[:, k:k + 1] * buf[k]
    o_ref[...] = x_ref[...] + g_ref[0] * y


def moe_combine(ys, dest, w, h2, x2, gate, sg, su, sd, seq, tm=MOE_TM):
    nt, d = h2.shape
    ds_ = sg.shape[-1]
    per_b = seq // tm
    tile = lambda: pl.BlockSpec((tm, d), lambda i: (i, 0))
    return pl.pallas_call(
        _combine_kernel,
        grid=(nt // tm,),
        in_specs=[pl.BlockSpec((1, 8, tm), lambda i: (i, 0, 0), memory_space=pltpu.SMEM),
                  pl.BlockSpec(memory_space=pl.ANY),
                  pl.BlockSpec((tm, LANES), lambda i: (i, 0)), tile(), tile(),
                  pl.BlockSpec((1, 1, d), lambda i: (i // per_b, 0, 0)),
                  _full((d, ds_)), _full((d, ds_)), _full((ds_, d))],
        out_specs=tile(),
        scratch_shapes=[pltpu.VMEM((TOP_K, tm, d), F32), pltpu.SemaphoreType.DMA(())],
        out_shape=jax.ShapeDtypeStruct((nt, d), F32),
        compiler_params=_cparams(("arbitrary",)),
        name="moe_combine",
    )(dest, ys, w, h2, x2, gate, sg.astype(BF16), su.astype(BF16), sd.astype(BF16))


def moe_layer(x, g_norm, sc, sh, gate, router, e_bias, wg, wu, wd, layer, sg, su, sd):
    b, t, d = x.shape
    nt = b * t
    h = ln_mod(x, g_norm, sc, sh, F32)
    h2 = h.reshape(nt, d)
    eid, rank, w, counts = moe_route(h2, router, e_bias)
    starts, items = _ffn_items(counts.reshape(-1).astype(jnp.int32), nt * TOP_K, MOE_ROWS)
    hit = eid[..., None] == jnp.arange(N_EXPERTS, dtype=jnp.int32)
    dest = jnp.sum(jnp.where(hit, starts.astype(jnp.int32), 0), axis=-1) + rank
    xs = moe_dispatch(h2, dest)
    ys = moe_ffn_sorted(xs, items, wg, wu, wd, layer)
    out = moe_combine(ys, dest, w, h2, x.reshape(nt, d), gate.reshape(b, 1, d), sg, su, sd, t)
    return out.reshape(b, t, d)


def kernel(x, c, ada_w, ada_b, norm_mix, norm_ffn, rel_bias, ev_w_in, ev_w_out, fox_fb, fox_qn, fox_kn, gdn_conv, gdn_a_log, gdn_dt_bias, gdn_on, od_w_in, od_w_out, nsa_qn, nsa_kn, nsa_pos, nsa_cmp_w1, nsa_cmp_w2, gla_wg_up, gla_bg, gla_on, moe_router, moe_bias, moe_wg, moe_wu, moe_wd, sh_wg, sh_wu, sh_wd):
    d = x.shape[-1]
    depth = ada_w.shape[0]
    mod = adaln(c, ada_w, ada_b)
    for layer in range(depth):
        sh1, sc1, g1, sh2, sc2, g2 = [mod[layer, :, i * d:(i + 1) * d] for i in range(6)]
        h = ln_mod(x, norm_mix[layer], sc1, sh1, BF16)
        j = layer // 2
        if layer % 2 == 0:
            y1, y2 = even_mixer(h, ev_w_in[j], fox_fb[j], fox_qn[j], fox_kn[j], gdn_conv[j], gdn_a_log[j],
                                gdn_dt_bias[j], gdn_on[j])
            w_out = ev_w_out[j]
        else:
            y1, y2 = odd_mixer(h, od_w_in[j], nsa_qn[j], nsa_kn[j], nsa_pos[j], nsa_cmp_w1[j], nsa_cmp_w2[j],
                               gla_wg_up[j], gla_bg[j], gla_on[j], rel_bias)
            w_out = od_w_out[j]
        x = out_proj(y1, y2, w_out, x, g1)
        x = moe_layer(x, norm_ffn[layer], sc2, sh2, g2, moe_router[layer], moe_bias[layer], moe_wg, moe_wu, moe_wd,
                      layer, sh_wg[layer], sh_wu[layer], sh_wd[layer])
    return x
```

```python
import functools
import math

import numpy as np
import jax
import jax.numpy as jnp
from jax import lax
from jax.experimental import pallas as pl
from jax.experimental.pallas import tpu as pltpu

F32 = jnp.float32
BF16 = jnp.bfloat16
HI = lax.Precision.HIGHEST

EPS = 1e-6
LOG2E = math.log2(math.e)
NEG = -1e30

FOX_HEADS, FOX_DH = 8, 64
GDN_HEADS, GDN_DH, GDN_CONV = 4, 128, 4
NSA_HEADS, NSA_KV_HEADS, NSA_DH = 8, 2, 64
NSA_GROUP = NSA_HEADS // NSA_KV_HEADS
CMP_LEN, CMP_STRIDE, CMP_HIDDEN = 32, 16, 256
SLC_LEN, SLC_TOPK, WINDOW = 64, 16, 512
GLA_HEADS, GLA_DK, GLA_DV, GLA_GATE_RANK, GLA_TAU, GLA_CHUNK = 4, 64, 128, 16, 16.0, 64
REL_BUCKETS, REL_MAX_DIST = 32, 128
N_EXPERTS, TOP_K, D_EXPERT, D_SHARED = 64, 6, 256, 256
N_GROUPS, TOPK_GROUPS, ROUTE_SCALE = 8, 4, 2.5

FOX_W = FOX_HEADS * FOX_DH
GDN_W = GDN_HEADS * GDN_DH
NSA_W = NSA_HEADS * NSA_DH
NSA_KV_W = NSA_KV_HEADS * NSA_DH
GLA_KW = GLA_HEADS * GLA_DK
GLA_W = GLA_HEADS * GLA_DV
EV_SIZES = (FOX_W, FOX_W, FOX_W, FOX_HEADS, 3 * GDN_W, GDN_HEADS, GDN_HEADS, GDN_W)
OD_SIZES = (NSA_W,) + (NSA_KV_W,) * 6 + (3 * NSA_HEADS, GLA_KW, GLA_KW, GLA_W, GLA_GATE_RANK, GLA_W)

LANES = 128
SUP_BLOCKS = 64
VMEM_LIMIT = 56 * 1024 * 1024


def _cparams(sem, flags=None):
    return pltpu.CompilerParams(dimension_semantics=sem, vmem_limit_bytes=VMEM_LIMIT, flags=flags)


def _full(shape):
    n = len(shape)
    return pl.BlockSpec(shape, lambda *_: (0,) * n)


def _dot(a, b):
    return jnp.dot(a, b, preferred_element_type=F32)


def _dot_hi(a, b):
    return jnp.dot(a, b, precision=HI, preferred_element_type=F32)


def _dot_nt(a, b, precision=None):
    return lax.dot_general(a, b, (((1,), (1,)), ((), ())), precision=precision, preferred_element_type=F32)


def _dot_tn(a, b, precision=None):
    return lax.dot_general(a, b, (((0,), (0,)), ((), ())), precision=precision, preferred_element_type=F32)


def _sigmoid(x):
    return 1.0 / (1.0 + jnp.exp(-x))


def _silu(x):
    return x * _sigmoid(x)


def _softplus(x):
    return jnp.maximum(x, 0.0) + jnp.log(1.0 + jnp.exp(-jnp.abs(x)))


def _log_sigmoid(x):
    return -_softplus(-x)


def _adaln_kernel(c_ref, w_ref, b_ref, o_ref):
    c = c_ref[...]
    o_ref[0] = _dot_hi(_silu(c), w_ref[0]) + b_ref[0]


def adaln(c, ada_w, ada_b):
    depth, d, n = ada_w.shape
    b = c.shape[0]
    cp = jnp.zeros((8, d), F32).at[:b].set(c)
    tn = 1536
    out = pl.pallas_call(
        _adaln_kernel,
        grid=(depth, n // tn),
        in_specs=[_full((8, d)),
                  pl.BlockSpec((1, d, tn), lambda l, j: (l, 0, j)),
                  pl.BlockSpec((1, 1, tn), lambda l, j: (l, 0, j))],
        out_specs=pl.BlockSpec((1, 8, tn), lambda l, j: (l, 0, j)),
        out_shape=jax.ShapeDtypeStruct((depth, 8, n), F32),
        compiler_params=_cparams(("arbitrary", "arbitrary")),
        name="adaln",
    )(cp, ada_w, ada_b.reshape(depth, 1, n))
    return out[:, :b]


def _ln_kernel(x_ref, g_ref, sc_ref, sh_ref, o_ref):
    x = x_ref[0]
    y = x * lax.rsqrt(jnp.mean(x * x, axis=-1, keepdims=True) + EPS) * g_ref[...]
    o_ref[0] = (y * (1.0 + sc_ref[0]) + sh_ref[0]).astype(o_ref.dtype)


def ln_mod(x, g, sc, sh, out_dtype, tm=512):
    b, t, d = x.shape
    return pl.pallas_call(
        _ln_kernel,
        grid=(b, t // tm),
        in_specs=[pl.BlockSpec((1, tm, d), lambda i, j: (i, j, 0)),
                  _full((1, d)),
                  pl.BlockSpec((1, 1, d), lambda i, j: (i, 0, 0)),
                  pl.BlockSpec((1, 1, d), lambda i, j: (i, 0, 0))],
        out_specs=pl.BlockSpec((1, tm, d), lambda i, j: (i, j, 0)),
        out_shape=jax.ShapeDtypeStruct((b, t, d), out_dtype),
        compiler_params=_cparams(("arbitrary", "arbitrary")),
        name="ln_mod",
    )(x, g.reshape(1, d), sc.reshape(b, 1, d), sh.reshape(b, 1, d))


def proj_multi(h, groups, tm=512, name="proj"):
    b, t, d = h.shape
    widths = [g[0].shape[1] for g in groups]
    starts = np.cumsum([0] + widths)
    w_cat = jnp.concatenate([g[0] for g in groups], axis=1).astype(BF16)
    extras = [e for g in groups for e in g[3]]
    n_ex = [len(g[3]) for g in groups]
    n_out = len(groups)

    def kern(h_ref, w_ref, *rest):
        ex_refs = rest[:len(extras)]
        o_refs = rest[len(extras):]
        y = _dot(h_ref[0], w_ref[...])
        pos = 0
        for gi, (_, out_dtype, epilogue, _) in enumerate(groups):
            yg = y[:, starts[gi]:starts[gi + 1]]
            if epilogue is not None:
                yg = epilogue(yg, *[e[...] for e in ex_refs[pos:pos + n_ex[gi]]])
            pos += n_ex[gi]
            o_refs[gi][0] = yg.astype(out_dtype)

    return pl.pallas_call(
        kern,
        grid=(b, t // tm),
        in_specs=[pl.BlockSpec((1, tm, d), lambda i, j: (i, j, 0)), _full((d, int(starts[-1])))]
                 + [_full(e.shape) for e in extras],
        out_specs=[pl.BlockSpec((1, tm, n), lambda i, j: (i, j, 0)) for n in widths],
        out_shape=[jax.ShapeDtypeStruct((b, t, n), g[1]) for n, g in zip(widths, groups)],
        compiler_params=_cparams(("arbitrary", "arbitrary")),
        name=name,
    )(h, w_cat, *extras)


def _head_norm_epilogue(dh):
    inv = 1.0 / dh

    def ep(y, bd, gain):
        ssq = _dot((y * y).astype(BF16), bd)
        return y * lax.rsqrt(ssq * inv + EPS) * gain

    return ep


def _block_diag_ones(n, dh):
    i = np.arange(n) // dh
    return jnp.asarray((i[:, None] == i[None, :]).astype(np.float32), dtype=BF16)


def _outproj_kernel(y1_ref, y2_ref, wa_ref, wb_ref, x_ref, g_ref, o_ref):
    y = _dot(y1_ref[0], wa_ref[...]) + _dot(y2_ref[0], wb_ref[...])
    o_ref[0] = x_ref[0] + g_ref[0] * y


def out_proj(y1, y2, w_out, x, gate, tm=512):
    b, t, d = x.shape
    n1, n2 = y1.shape[-1], y2.shape[-1]
    wa = w_out[:n1].astype(BF16)
    wb = w_out[n1:].astype(BF16)
    return pl.pallas_call(
        _outproj_kernel,
        grid=(b, t // tm),
        in_specs=[pl.BlockSpec((1, tm, n1), lambda i, j: (i, j, 0)),
                  pl.BlockSpec((1, tm, n2), lambda i, j: (i, j, 0)),
                  _full((n1, d)), _full((n2, d)),
                  pl.BlockSpec((1, tm, d), lambda i, j: (i, j, 0)),
                  pl.BlockSpec((1, 1, d), lambda i, j: (i, 0, 0))],
        out_specs=pl.BlockSpec((1, tm, d), lambda i, j: (i, j, 0)),
        out_shape=jax.ShapeDtypeStruct((b, t, d), F32),
        compiler_params=_cparams(("arbitrary", "arbitrary")),
        name="out_proj",
    )(y1, y2, wa, wb, x, gate.reshape(b, 1, d))


def _decay_kernel(s_ref, fb_ref, tril_ref, place_ref, o_ref, carry):
    @pl.when(pl.program_id(1) == 0)
    def _():
        carry[...] = jnp.zeros_like(carry)

    tm = s_ref.shape[1]
    lf = _log_sigmoid(s_ref[0] + fb_ref[...])
    cum = _dot_hi(tril_ref[...], lf) + carry[...]
    carry[...] = cum[tm - 1:tm, :]
    x = cum * LOG2E
    hi = x.astype(BF16)
    r1 = x - hi.astype(F32)
    mid = r1.astype(BF16)
    low = (r1 - mid.astype(F32)).astype(BF16)
    o_ref[0] = _dot(jnp.concatenate([hi, mid, low], axis=1), place_ref[...]).astype(o_ref.dtype)


def fox_decay(small, fox_fb, tm=512):
    b, t, _ = small.shape
    fb = jnp.zeros((1, LANES), F32).at[0, :FOX_HEADS].set(fox_fb)
    tril = jnp.asarray(np.tril(np.ones((tm, tm), np.float32)))
    place = np.zeros((3 * LANES, FOX_W), np.float32)
    for h in range(FOX_HEADS):
        for j in range(3):
            place[j * LANES + h, (h // 2) * LANES + (FOX_DH if h % 2 == 0 else 0) + j] = 1.0
    return pl.pallas_call(
        _decay_kernel,
        grid=(b, t // tm),
        in_specs=[pl.BlockSpec((1, tm, LANES), lambda i, j: (i, j, 0)), _full((1, LANES)), _full((tm, tm)),
                  _full((3 * LANES, FOX_W))],
        out_specs=pl.BlockSpec((1, tm, FOX_W), lambda i, j: (i, j, 0)),
        out_shape=jax.ShapeDtypeStruct((b, t, FOX_W), BF16),
        scratch_shapes=[pltpu.VMEM((1, LANES), F32)],
        compiler_params=_cparams(("arbitrary", "arbitrary")),
        name="fox_decay",
    )(small, fb, tril, jnp.asarray(place, dtype=BF16))


def _fox_kernel(q_ref, k_ref, v_ref, f_ref, o_ref, *, tq, wide):
    i = pl.program_id(2)
    lane = lax.broadcasted_iota(jnp.int32, (1, LANES), 1)
    lo = lane < FOX_DH
    coef = jnp.where((lane & (FOX_DH - 1)) < 3, -1.0, 0.0).astype(BF16)
    q = q_ref[0]
    qs = (jnp.where(lo, q, coef), jnp.where(lo, coef, q))
    causal = (lax.broadcasted_iota(jnp.int32, (tq, tq), 1) <= lax.broadcasted_iota(jnp.int32, (tq, tq), 0))
    one = jnp.ones((1, LANES), BF16)

    def step(j, carry, tw, diag=False):
        start = pl.multiple_of(j * tw, tw)
        kt = k_ref[0, pl.ds(start, tw), :]
        ft = f_ref[0, pl.ds(start, tw), :]
        vt = v_ref[0, pl.ds(start, tw), :]
        s_pair = (_dot_nt(qs[0], jnp.where(lo, kt, ft)), _dot_nt(qs[1], jnp.where(lo, ft, kt)))
        vs = (jnp.where(lo, vt, one), jnp.where(lo, one, vt))
        new = []
        for hh in range(2):
            m, acc = carry[hh]
            s = s_pair[hh]
            if diag:
                s = jnp.where(causal, s, NEG)
            m_new = jnp.maximum(m, jnp.max(s, axis=1, keepdims=True))
            p = jnp.exp2(s - m_new)
            acc = jnp.exp2(m - m_new) * acc + _dot(p.astype(BF16), vs[hh])
            new.append((m_new, acc))
        return tuple(new)

    carry = tuple((jnp.full((tq, 1), NEG, F32), jnp.zeros((tq, LANES), F32)) for _ in range(2))
    n_wide = i // wide
    carry = lax.fori_loop(0, n_wide, lambda j, c: step(j, c, wide * tq), carry)
    done = n_wide * wide
    part = wide // 2
    while part >= 1:
        carry = lax.cond((i & part) != 0, lambda c, d=done, w=part: step(d // w, c, w * tq), lambda c: c, carry)
        done = done + (i & part)
        part //= 2
    carry = step(i, carry, tq, diag=True)
    acc = jnp.where(lo, carry[0][1], carry[1][1])
    den = jnp.where(lo, carry[1][1], carry[0][1])
    o_ref[0] = (acc / pltpu.roll(den, FOX_DH, 1)).astype(o_ref.dtype)


def fox_attention(q, k, v, feat, tq=512, wide=4):
    b, t, w = q.shape
    npair = w // LANES
    nt = t // tq
    whole = lambda: pl.BlockSpec((1, t, LANES), lambda bi, p, i: (bi, 0, p))
    return pl.pallas_call(
        functools.partial(_fox_kernel, tq=tq, wide=wide),
        grid=(b, npair, nt),
        in_specs=[pl.BlockSpec((1, tq, LANES), lambda bi, p, i: (bi, i, p)), whole(), whole(), whole()],
        out_specs=pl.BlockSpec((1, tq, LANES), lambda bi, p, i: (bi, i, p)),
        out_shape=jax.ShapeDtypeStruct((b, t, w), BF16),
        compiler_params=_cparams(("arbitrary", "arbitrary", "arbitrary")),
        name="fox_attn",
    )(q, k, v, feat)


def _mm(a, b):
    return _dot(a.astype(BF16), b.astype(BF16))


def _mm3(a, b):
    ah = a.astype(BF16)
    bh = b.astype(BF16)
    al = (a - ah.astype(F32)).astype(BF16)
    bl = (b - bh.astype(F32)).astype(BF16)
    return _dot(jnp.concatenate([ah, ah, al], axis=1), jnp.concatenate([bh, bl, bh], axis=0))


def _tril_solve(a, rhs, ri, ci):
    n = a[0].shape[0]
    both = lambda f, x, y: [f(p, q) for p, q in zip(x, y)]
    eye = (ri == ci).astype(F32)
    same = lambda b: (lax.shift_right_logical(ri, int(math.log2(b)))
                      == lax.shift_right_logical(ci, int(math.log2(b))))
    base = 16
    d = [jnp.where(same(base), p, 0.0) for p in a]
    d2 = both(_mm, d, d)
    d4 = both(_mm, d2, d2)
    r1 = [eye - p + p2 - t for p, p2, t in zip(d, d2, both(_mm, d, d2))]
    d8 = both(_mm, d4, d4)
    r2 = [eye + p4 + p8 + t for p4, p8, t in zip(d4, d8, both(_mm, d4, d8))]
    t = both(_mm, r1, r2)
    b = base
    while b < n:
        join = same(2 * b) & jnp.logical_not(same(b))
        low = [jnp.where(join, p, 0.0) for p in a]
        t = [p - q for p, q in zip(t, both(_mm, both(_mm, t, low), t))]
        b *= 2
    return both(_mm3, t, rhs)


GDN_BLOCK = 128


def _gdn_kernel(x_ref, sm_ref, z_ref, cw_ref, ega_ref, egb_ref, alog_ref, dtb_ref, on_ref, tril_ref,
                o_ref, s_scr, prev_scr, *, tc):
    c = GDN_BLOCK
    w = GDN_W

    @pl.when(pl.program_id(1) == 0)
    def _():
        s_scr[...] = jnp.zeros_like(s_scr)
        prev_scr[...] = jnp.zeros_like(prev_scr)

    x = x_ref[0]
    prev = prev_scr[...]
    row8 = lax.broadcasted_iota(jnp.int32, (8, 1), 0)
    acc = x * cw_ref[GDN_CONV - 1:GDN_CONV, :]
    for s in range(1, GDN_CONV):
        rolled = pltpu.roll(x, s, 0)
        head = jnp.where(row8 < s, pltpu.roll(prev, s, 0), rolled[0:8])
        shifted = jnp.concatenate([head, rolled[8:]], axis=0)
        acc = acc + shifted * cw_ref[GDN_CONV - 1 - s:GDN_CONV - s, :]
    prev_scr[...] = x[tc - 8:tc]
    xc = _silu(acc)

    sm = sm_ref[0]
    g_raw = _dot_hi(sm, ega_ref[...])
    b_raw = _dot_hi(sm, egb_ref[...])
    g = -jnp.exp(alog_ref[...]) * _softplus(g_raw + dtb_ref[...])
    beta_all = _sigmoid(b_raw)
    gc_all = _dot_hi(tril_ref[...], g)

    ri = lax.broadcasted_iota(jnp.int32, (c, c), 0)
    ci = lax.broadcasted_iota(jnp.int32, (c, c), 1)
    causal = ci <= ri
    strict = ci < ri

    nblk = tc // c
    a_l, attn_l, rhs_l, qd_l, kd_l, egl_l = [], [], [], [], [], []
    for h in range(GDN_HEADS):
        ln = slice(h * GDN_DH, (h + 1) * GDN_DH)
        qh = xc[:, h * GDN_DH:(h + 1) * GDN_DH]
        kh = xc[:, w + h * GDN_DH:w + (h + 1) * GDN_DH]
        qh = qh * lax.rsqrt(jnp.sum(qh * qh, axis=-1, keepdims=True) + EPS) * (GDN_DH ** -0.5)
        kh = kh * lax.rsqrt(jnp.sum(kh * kh, axis=-1, keepdims=True) + EPS)
        vh = xc[:, 2 * w + h * GDN_DH:2 * w + (h + 1) * GDN_DH]
        gch = gc_all[:, ln]
        gct = gch.T
        egc = jnp.exp(gch)
        bh = beta_all[:, ln]
        for n in range(nblk):
            sl = slice(n * c, (n + 1) * c)
            q, k, v, gc, be = qh[sl], kh[sl], vh[sl], gch[sl], bh[sl]
            decay = jnp.exp(jnp.where(causal, gc - gct[:, sl], NEG))
            kb = k * be
            kk = _dot_nt(jnp.concatenate([kb, q], axis=0).astype(BF16), k.astype(BF16))
            a_l.append(jnp.where(strict, kk[:c] * decay, 0.0))
            attn_l.append(jnp.where(causal, kk[c:] * decay, 0.0))
            rhs_l.append(jnp.concatenate([v * be, kb * egc[sl]], axis=1))
            gl = gc[c - 1:c, :]
            qd_l.append(q * egc[sl])
            kd_l.append(k * jnp.exp(gl - gc))
            egl_l.append(jnp.exp(gl))
    uw_l = _tril_solve(a_l, rhs_l, ri, ci)

    states = [s_scr[h] for h in range(GDN_HEADS)]
    for n in range(nblk):
        sl = slice(n * c, (n + 1) * c)
        idx = [h * nblk + n for h in range(GDN_HEADS)]
        ws = [_mm(jnp.concatenate([uw_l[i][:, GDN_DH:], qd_l[i]], axis=0), states[h])
              for h, i in enumerate(idx)]
        v_new = [uw_l[i][:, :GDN_DH] - ws[h][:c] for h, i in enumerate(idx)]
        o = [ws[h][c:] + _mm(attn_l[i], v_new[h]) for h, i in enumerate(idx)]
        states = [states[h] * egl_l[i] + _dot_tn(kd_l[i].astype(BF16), v_new[h].astype(BF16))
                  for h, i in enumerate(idx)]
        for h in range(GDN_HEADS):
            ln = slice(h * GDN_DH, (h + 1) * GDN_DH)
            on = o[h] * lax.rsqrt(jnp.mean(o[h] * o[h], axis=-1, keepdims=True) + EPS) * on_ref[...]
            o_ref[0, sl, ln] = (on * _silu(z_ref[0, sl, ln])).astype(o_ref.dtype)
    for h in range(GDN_HEADS):
        s_scr[h] = states[h]


def gated_delta_net(x, small, z, conv_w, a_log, dt_bias, on_gain, tc=512):
    b, t, _ = x.shape
    w = GDN_W
    ega = np.zeros((LANES, w), np.float32)
    egb = np.zeros((LANES, w), np.float32)
    for h in range(GDN_HEADS):
        ega[FOX_HEADS + h, h * GDN_DH:(h + 1) * GDN_DH] = 1.0
        egb[FOX_HEADS + GDN_HEADS + h, h * GDN_DH:(h + 1) * GDN_DH] = 1.0
    alog = jnp.repeat(a_log, GDN_DH).reshape(1, w)
    dtb = jnp.repeat(dt_bias, GDN_DH).reshape(1, w)
    idx = np.arange(tc)
    tril = ((idx[:, None] >= idx[None, :]) & (idx[:, None] // GDN_BLOCK == idx[None, :] // GDN_BLOCK))
    row = lambda n: pl.BlockSpec((1, tc, n), lambda i, j: (i, j, 0))
    return pl.pallas_call(
        functools.partial(_gdn_kernel, tc=tc),
        grid=(b, t // tc),
        in_specs=[row(3 * w), row(LANES), row(w), _full((GDN_CONV, 3 * w)), _full((LANES, w)), _full((LANES, w)),
                  _full((1, w)), _full((1, w)), _full((1, GDN_DH)), _full((tc, tc))],
        out_specs=row(w),
        out_shape=jax.ShapeDtypeStruct((b, t, w), BF16),
        scratch_shapes=[pltpu.VMEM((GDN_HEADS, GDN_DH, GDN_DH), F32), pltpu.VMEM((8, 3 * w), F32)],
        compiler_params=_cparams(("arbitrary", "arbitrary")),
        name="gdn",
    )(x, small, z, conv_w, jnp.asarray(ega), jnp.asarray(egb), alog, dtb, on_gain.reshape(1, GDN_DH),
      jnp.asarray(tril.astype(np.float32)))


def even_mixer(h, w_in, fox_fb, fox_qn, fox_kn, gdn_conv, gdn_a_log, gdn_dt_bias, gdn_on):
    cuts = np.cumsum((0,) + EV_SIZES)
    col = lambda i: w_in[:, cuts[i]:cuts[i + 1]]
    bd = _block_diag_ones(FOX_W, FOX_DH)
    ep = _head_norm_epilogue(FOX_DH)
    qg = (jnp.tile(fox_qn, FOX_HEADS) * (FOX_DH ** -0.5 * LOG2E)).reshape(1, FOX_W)
    kg = jnp.tile(fox_kn, FOX_HEADS).reshape(1, FOX_W)
    w_small = jnp.zeros((w_in.shape[0], LANES), F32)
    w_small = w_small.at[:, 0:8].set(col(3)).at[:, 8:12].set(col(5)).at[:, 12:16].set(col(6))
    fq, fk, fv, small, gqkv, gz = proj_multi(
        h, [(col(0), BF16, ep, (bd, qg)), (col(1), BF16, ep, (bd, kg)), (col(2), BF16, None, ()),
            (w_small, F32, None, ()), (col(4), F32, None, ()), (col(7), F32, None, ())], name="proj_even")
    feat = fox_decay(small, fox_fb)
    o_fox = fox_attention(fq, fk, fv, feat)
    o_gdn = gated_delta_net(gqkv, small, gz, gdn_conv, gdn_a_log, gdn_dt_bias, gdn_on)
    return o_fox, o_gdn


def _t5_bucket_np(dist):
    n = np.maximum(dist, 0)
    exact = REL_BUCKETS // 2
    nf = np.maximum(n, 1).astype(np.float32)
    large = exact + (np.log(nf / np.float32(exact)) / np.float32(math.log(REL_MAX_DIST / exact))
                     * np.float32(REL_BUCKETS - exact)).astype(np.int32)
    large = np.minimum(large, REL_BUCKETS - 1)
    return np.where(n < exact, n, large)


def _bias_kernel(tbl_ref, bucket_ref, o_ref):
    h = pl.program_id(0)
    bucket = bucket_ref[...]
    acc = jnp.full(bucket.shape, NEG, F32)
    for b in range(REL_BUCKETS):
        acc = jnp.where(bucket == b, tbl_ref[b, h], acc)
    o_ref[0] = acc


def _bias_table(rel_bias, dist, valid):
    shifted = (rel_bias - rel_bias[REL_BUCKETS - 1:REL_BUCKETS]) * LOG2E
    bucket = np.where(valid, _t5_bucket_np(dist), -1).astype(np.int32)
    rows, cols = int(np.prod(bucket.shape[:-1])), bucket.shape[-1]
    nh = rel_bias.shape[1]
    tb = pl.pallas_call(
        _bias_kernel,
        grid=(nh,),
        in_specs=[pl.BlockSpec(memory_space=pltpu.SMEM), _full((rows, cols))],
        out_specs=pl.BlockSpec((1, rows, cols), lambda h: (h, 0, 0)),
        out_shape=jax.ShapeDtypeStruct((nh, rows, cols), F32),
        compiler_params=_cparams(("arbitrary",)),
        name="t5_bias",
    )(shifted, jnp.asarray(bucket.reshape(rows, cols)))
    return tb.reshape((nh,) + bucket.shape)


def _cmp_kernel(r_ref, pos_ref, w1_ref, w2_ref, kn_ref, o_ref):
    m = r_ref.shape[3]
    half = r_ref.shape[4]
    r = r_ref[0, 0, 0].astype(BF16)
    a = _dot(r, w1_ref[0, :half, :])
    bm = _dot(r, w1_ref[0, half:, :])
    c = _dot(pos_ref[0].astype(BF16), w1_ref[0])
    hid = a + pltpu.roll(bm, m - 1, 0) + c[0:1, :]
    out = _dot(_silu(hid).astype(BF16), w2_ref[0])
    normed = out * lax.rsqrt(jnp.mean(out * out, axis=-1, keepdims=True) + EPS) * kn_ref[...]
    o_ref[0, 0, 0] = jnp.where(pl.program_id(0) == 0, normed, out).astype(o_ref.dtype)


def nsa_compress(kcvc, pos, w1, w2, kn):
    b, t, _ = kcvc.shape
    m = t // CMP_STRIDE
    half = CMP_STRIDE * NSA_DH
    r = kcvc.reshape(b, m, CMP_STRIDE, 2, NSA_KV_HEADS, NSA_DH).transpose(3, 0, 4, 1, 2, 5).reshape(2, b, 2, m, half)
    posf = jnp.zeros((2, 8, 2 * half), F32).at[:, 0].set(pos.reshape(2, 2 * half))
    w2d = jnp.concatenate([w2, w2], axis=-1).astype(BF16)
    knd = jnp.tile(kn, 2).reshape(1, LANES)
    return pl.pallas_call(
        _cmp_kernel,
        grid=(2, b, NSA_KV_HEADS),
        in_specs=[pl.BlockSpec((1, 1, 1, m, half), lambda s, i, k: (s, i, k, 0, 0)),
                  pl.BlockSpec((1, 8, 2 * half), lambda s, i, k: (s, 0, 0)),
                  pl.BlockSpec((1, 2 * half, CMP_HIDDEN), lambda s, i, k: (s, 0, 0)),
                  pl.BlockSpec((1, CMP_HIDDEN, LANES), lambda s, i, k: (s, 0, 0)),
                  _full((1, LANES))],
        out_specs=pl.BlockSpec((1, 1, 1, m, LANES), lambda s, i, k: (s, i, k, 0, 0)),
        out_shape=jax.ShapeDtypeStruct((2, b, NSA_KV_HEADS, m, LANES), BF16),
        compiler_params=_cparams(("arbitrary", "arbitrary", "arbitrary")),
        name="nsa_compress",
    )(r, posf, w1.astype(BF16), w2d, knd)


def _dot_split(a, b):
    hi = a.astype(BF16)
    lo = (a - hi.astype(F32)).astype(BF16)
    return _dot(hi, b) + _dot(lo, b)


def _head_q(q_ref, hh, lo):
    blk = q_ref[0, :, (hh // 2) * LANES:(hh // 2 + 1) * LANES]
    keep = lo if hh % 2 == 0 else jnp.logical_not(lo)
    return jnp.where(keep, blk, jnp.zeros_like(blk))


def _pair_heads(o, lo):
    return jnp.concatenate([jnp.where(lo, o[0], o[1]), jnp.where(lo, o[2], o[3])], axis=1)


def _nsa_sel_kernel(q_ref, kc_ref, vc_ref, ov_ref, bt_ref, o_ref, sel_ref, *, tq, nband, n_slc):
    i = pl.program_id(2)
    ncp = kc_ref.shape[3]
    nsp = ov_ref.shape[1]
    per = tq // CMP_STRIDE
    var = jnp.minimum(i, 1)
    bs = pl.multiple_of(per * jnp.maximum(i - 1, 0), per)
    lo = lax.broadcasted_iota(jnp.int32, (1, LANES), 1) < NSA_DH
    kc = kc_ref[0, 0, 0]
    vc = vc_ref[0, 0, 0]
    kcb = kc_ref[0, 0, 0, pl.ds(bs, nband), :]
    vcb = vc_ref[0, 0, 0, pl.ds(bs, nband), :]
    far_ok = lax.broadcasted_iota(jnp.int32, (1, ncp), 1) < per * (i - 1)
    hs = range(NSA_GROUP)
    qh = [_head_q(q_ref, hh, lo) for hh in hs]
    s_far = [jnp.where(far_ok, _dot_nt(q, kc), NEG) for q in qh]
    s_band = [_dot_nt(qh[hh], kcb) + bt_ref[var, hh] for hh in hs]
    m = [jnp.maximum(jnp.max(a, axis=1, keepdims=True), jnp.max(b, axis=1, keepdims=True))
         for a, b in zip(s_far, s_band)]
    m = [jnp.where(x < 0.5 * NEG, 0.0, x) for x in m]
    p_far = [jnp.exp2(a - x) for a, x in zip(s_far, m)]
    p_band = [jnp.exp2(b - x) for b, x in zip(s_band, m)]
    l = [jnp.sum(a, axis=1, keepdims=True) + jnp.sum(b, axis=1, keepdims=True) for a, b in zip(p_far, p_band)]
    inv = [1.0 / jnp.where(x == 0.0, 1.0, x) for x in l]
    outs = [(_dot(a.astype(BF16), vc) + _dot(b.astype(BF16), vcb)) * x for a, b, x in zip(p_far, p_band, inv)]
    ps_far = p_far[0] * inv[0]
    ps_band = p_band[0] * inv[0]
    for hh in range(1, NSA_GROUP):
        ps_far = ps_far + p_far[hh] * inv[hh]
        ps_band = ps_band + p_band[hh] * inv[hh]
    o_ref[0] = _pair_heads(outs, lo).astype(o_ref.dtype)

    imp = _dot_split(ps_far, ov_ref[...]) + _dot_split(ps_band, ov_ref[pl.ds(bs, nband), :])
    blk = lax.broadcasted_iota(jnp.int32, (1, nsp), 1)
    blk_f = blk.astype(F32)
    qpos = i * tq + lax.broadcasted_iota(jnp.int32, (tq, 1), 0)
    cur = lax.shift_right_logical(qpos, int(math.log2(SLC_LEN)))
    forced = (blk == 0) | (blk == cur) | (blk == cur - 1)
    work = jnp.where(forced, -jnp.inf, jnp.where(blk <= cur, imp, NEG))
    work = jnp.where(blk < n_slc, work, -jnp.inf)
    ngrp = 4
    rg = tq // ngrp
    works = [work[r * rg:(r + 1) * rg] for r in range(ngrp)]
    sels = [jnp.where(forced[r * rg:(r + 1) * rg], 1.0, jnp.zeros((rg, nsp), F32)) for r in range(ngrp)]
    for _ in range(max(min(SLC_TOPK, n_slc) - 3, 0)):
        ms = [jnp.max(w, axis=1, keepdims=True) for w in works]
        firsts = [jnp.min(jnp.where(w == m, blk_f, float(nsp)), axis=1, keepdims=True) for w, m in zip(works, ms)]
        picks = [blk_f == f for f in firsts]
        sels = [jnp.where(p, 1.0, s) for p, s in zip(picks, sels)]
        works = [jnp.where(p, -jnp.inf, w) for p, w in zip(picks, works)]
    sel = jnp.concatenate(sels, axis=0)
    for sup in range(sel_ref.shape[2]):
        col = sel[:, (sup // 2) * LANES:(sup // 2 + 1) * LANES]
        if sup % 2 == 0:
            col = pltpu.roll(col, SUP_BLOCKS, 1)
        sel_ref[0, 0, sup] = jnp.where(lo, 0.0, jnp.where(col > 0.5, 0.0, NEG)).astype(sel_ref.dtype)


def nsa_select(q, cmp_kv, rel_bias, tq=512):
    b, t, _ = q.shape
    ncp = t // CMP_STRIDE
    n_cmp = ncp - 1
    n_slc = t // SLC_LEN
    nsp = max(LANES, n_slc)
    nsup = max(1, n_slc // SUP_BLOCKS)
    per = tq // CMP_STRIDE
    nband = 2 * per
    n = np.arange(ncp)[:, None]
    s = np.arange(nsp)[None, :]
    ov = ((CMP_STRIDE * n < SLC_LEN * s + SLC_LEN) & (CMP_STRIDE * n + CMP_LEN > SLC_LEN * s)
          & (n < n_cmp) & (s < n_slc)).astype(np.float32)
    qi = np.arange(tq)[:, None]
    nj = np.arange(nband)[None, :]
    end = CMP_STRIDE * nj + CMP_LEN - 1
    dist = np.stack([qi - end, tq + qi - end])
    bt = _bias_table(rel_bias, dist, dist >= 0)
    bt = bt.reshape(NSA_KV_HEADS, NSA_GROUP, 2, tq, nband).transpose(0, 2, 1, 3, 4)
    bt = bt.reshape(NSA_KV_HEADS * 2, NSA_GROUP, tq, nband)
    gw = NSA_GROUP * NSA_DH
    return pl.pallas_call(
        functools.partial(_nsa_sel_kernel, tq=tq, nband=nband, n_slc=n_slc),
        grid=(b, NSA_KV_HEADS, t // tq),
        in_specs=[pl.BlockSpec((1, tq, gw), lambda bi, k, i: (bi, i, k)),
                  pl.BlockSpec((1, 1, 1, ncp, LANES), lambda bi, k, i: (0, bi, k, 0, 0)),
                  pl.BlockSpec((1, 1, 1, ncp, LANES), lambda bi, k, i: (1, bi, k, 0, 0)),
                  _full((ncp, nsp)),
                  pl.BlockSpec((2, NSA_GROUP, tq, nband), lambda bi, k, i: (k, 0, 0, 0))],
        out_specs=[pl.BlockSpec((1, tq, gw), lambda bi, k, i: (bi, i, k)),
                   pl.BlockSpec((1, 1, nsup, tq, LANES), lambda bi, k, i: (bi, k, 0, i, 0))],
        out_shape=[jax.ShapeDtypeStruct((b, t, NSA_W), BF16),
                   jax.ShapeDtypeStruct((b, NSA_KV_HEADS, nsup, t, LANES), BF16)],
        compiler_params=_cparams(("arbitrary", "arbitrary", "arbitrary")),
        name="nsa_select",
    )(q, cmp_kv, cmp_kv, jnp.asarray(ov, dtype=BF16), bt)


def _nsa_main_kernel(q_ref, ks_ref, vs_ref, kw0_ref, kw1_ref, kw2_ref, vw0_ref, vw1_ref, vw2_ref, sel_ref, oh_ref,
                     ocmp_ref, gate_ref, tb_ref, wm_ref, eg_ref, o_ref, *, tq, wide):
    i = pl.program_id(2)
    g = NSA_GROUP
    lo = lax.broadcasted_iota(jnp.int32, (1, LANES), 1) < NSA_DH

    def head_low(hh):
        blk = q_ref[0, :, (hh // 2) * LANES:(hh // 2 + 1) * LANES]
        if hh % 2 == 1:
            blk = pltpu.roll(blk.astype(F32), NSA_DH, 1).astype(BF16)
        return jnp.where(lo, blk, jnp.zeros_like(blk))

    qst = jnp.concatenate([head_low(hh) for hh in range(g)], axis=0)
    causal = (lax.broadcasted_iota(jnp.int32, (tq, tq), 1) <= lax.broadcasted_iota(jnp.int32, (tq, tq), 0))
    one = jnp.ones((1, LANES), BF16)
    sup_keys = SUP_BLOCKS * SLC_LEN
    all_masked = jnp.where(lo, 0.0, NEG).astype(BF16)

    def sel_step(jt, carry, near, tw):
        m, acc = carry
        key0 = jnp.maximum(jt, 0) * tw
        start = pl.multiple_of(key0, tw)
        within = pl.multiple_of(key0 & (sup_keys - 1), tw)
        mq = sel_ref[0, 0, lax.shift_right_logical(key0, int(math.log2(sup_keys)))]
        mq = jnp.where(jt >= 0, mq, all_masked)
        q_ext = qst + jnp.concatenate([mq] * g, axis=0)
        kt = jnp.where(lo, ks_ref[0, pl.ds(start, tw), :], oh_ref[pl.ds(within, tw), :])
        vt = jnp.where(lo, vs_ref[0, pl.ds(start, tw), :], one)
        s = _dot_nt(q_ext, kt)
        if near is not None:
            s = s.reshape(g, tq, tw) + tb_ref[:, :, near * tq:(near + 1) * tq]
            if near == 2:
                s = jnp.where(causal[None], s, NEG)
            s = s.reshape(g * tq, tw)
        m_new = jnp.maximum(m, jnp.max(s, axis=1, keepdims=True))
        p = jnp.exp2(s - m_new)
        acc = jnp.exp2(m - m_new) * acc + _dot(p.astype(BF16), vt)
        return m_new, acc

    n_far = jnp.maximum(i - 2, 0)
    n_wide = n_far // wide
    carry = (jnp.full((g * tq, 1), NEG, F32), jnp.zeros((g * tq, LANES), F32))
    carry = lax.fori_loop(0, n_wide, lambda j, c: sel_step(j, c, None, wide * tq), carry)
    done = n_wide * wide
    part = wide // 2
    while part >= 1:
        carry = lax.cond((n_far & part) != 0, lambda c, d=done, w=part: sel_step(d // w, c, None, w * tq),
                         lambda c: c, carry)
        done = done + (n_far & part)
        part //= 2
    for near in range(3):
        carry = sel_step(i - 2 + near, carry, near, tq)
    o_slc = (carry[1] / pltpu.roll(carry[1], NSA_DH, 1)).reshape(g, tq, LANES)

    var = jnp.minimum(i, 2)
    kws = (kw0_ref, kw1_ref, kw2_ref)
    vws = (vw0_ref, vw1_ref, vw2_ref)
    sw = []
    for near in range(3):
        s = _dot_nt(qst, kws[near][0]).reshape(g, tq, tq)
        s = s + tb_ref[:, :, near * tq:(near + 1) * tq] + wm_ref[var, :, near * tq:(near + 1) * tq][None]
        sw.append(s.reshape(g * tq, tq))
    m = jnp.maximum(jnp.maximum(jnp.max(sw[0], axis=1, keepdims=True), jnp.max(sw[1], axis=1, keepdims=True)),
                    jnp.max(sw[2], axis=1, keepdims=True))
    acc = jnp.zeros((g * tq, LANES), F32)
    for near in range(3):
        p = jnp.exp2(sw[near] - m)
        acc = acc + _dot(p.astype(BF16), jnp.where(lo, vws[near][0], one))
    o_win = (acc / pltpu.roll(acc, NSA_DH, 1)).reshape(g, tq, LANES)

    pair = lambda o: jnp.concatenate([jnp.where(lo, o[0], pltpu.roll(o[1], NSA_DH, 1)),
                                      jnp.where(lo, o[2], pltpu.roll(o[3], NSA_DH, 1))], axis=1)
    gates = _dot_hi(_sigmoid(gate_ref[0]), eg_ref[0])
    gw = g * NSA_DH
    out = (gates[:, 0:gw] * ocmp_ref[0].astype(F32)
           + gates[:, gw:2 * gw] * pair(o_slc) + gates[:, 2 * gw:3 * gw] * pair(o_win))
    o_ref[0] = out.astype(o_ref.dtype)


def nsa_main(q, ksw, vsw, sel, o_cmp, small, rel_bias, tq=256, wide=8):
    b, t, _ = q.shape
    nsup = sel.shape[2]
    g = NSA_GROUP
    gw = g * NSA_DH
    sup_keys = SUP_BLOCKS * SLC_LEN
    oh = np.zeros((sup_keys, LANES), np.float32)
    oh[np.arange(sup_keys), NSA_DH + np.arange(sup_keys) // SLC_LEN] = 1.0
    qi = np.arange(tq)[:, None]
    c = np.arange(3 * tq)[None, :]
    dist = qi + 2 * tq - c
    tb = _bias_table(rel_bias, dist, np.ones_like(dist, bool))
    wm = np.zeros((3, tq, 3 * tq), np.float32)
    for var in range(3):
        exists = c >= tq * (2 - var)
        wm[var] = np.where((dist >= 0) & (dist < WINDOW) & exists, 0.0, NEG)
    eg = np.zeros((NSA_KV_HEADS, LANES, 3 * gw), np.float32)
    for k in range(NSA_KV_HEADS):
        for hh in range(g):
            for br in range(3):
                eg[k, (k * g + hh) * 3 + br, br * gw + hh * NSA_DH:br * gw + (hh + 1) * NSA_DH] = 1.0
    near = lambda off, col: pl.BlockSpec(
        (1, tq, LANES), lambda bi, k, i: (bi, jnp.maximum(i - off, 0), col + k))
    return pl.pallas_call(
        functools.partial(_nsa_main_kernel, tq=tq, wide=wide),
        grid=(b, NSA_KV_HEADS, t // tq),
        in_specs=[pl.BlockSpec((1, tq, gw), lambda bi, k, i: (bi, i, k)),
                  pl.BlockSpec((1, t, LANES), lambda bi, k, i: (bi, 0, k)),
                  pl.BlockSpec((1, t, LANES), lambda bi, k, i: (bi, 0, k)),
                  near(2, 2), near(1, 2), near(0, 2), near(2, 2), near(1, 2), near(0, 2),
                  pl.BlockSpec((1, 1, nsup, tq, LANES), lambda bi, k, i: (bi, k, 0, i, 0)),
                  _full((sup_keys, LANES)),
                  pl.BlockSpec((1, tq, gw), lambda bi, k, i: (bi, i, k)),
                  pl.BlockSpec((1, tq, LANES), lambda bi, k, i: (bi, i, 0)),
                  pl.BlockSpec((g, tq, 3 * tq), lambda bi, k, i: (k, 0, 0)),
                  _full((3, tq, 3 * tq)),
                  pl.BlockSpec((1, LANES, 3 * gw), lambda bi, k, i: (k, 0, 0))],
        out_specs=pl.BlockSpec((1, tq, gw), lambda bi, k, i: (bi, i, k)),
        out_shape=jax.ShapeDtypeStruct((b, t, NSA_W), BF16),
        compiler_params=_cparams(("arbitrary", "arbitrary", "arbitrary")),
        name="nsa_main",
    )(q, ksw, vsw, ksw, ksw, ksw, vsw, vsw, vsw, sel, jnp.asarray(oh, dtype=BF16), o_cmp, small, tb,
      jnp.asarray(wm), jnp.asarray(eg))


def _gla_kernel(qk_ref, v_ref, r_ref, sm_ref, wg_ref, bg_ref, on_ref, tril_ref, o_ref, s_scr, *, tc):
    c = GLA_CHUNK

    @pl.when(pl.program_id(1) == 0)
    def _():
        s_scr[...] = jnp.zeros_like(s_scr)

    kw = GLA_KW
    log_a = _log_sigmoid(_dot_hi(sm_ref[0], wg_ref[...]) + bg_ref[...]) * (1.0 / GLA_TAU)
    gcum = _dot_hi(tril_ref[...], log_a)
    q = qk_ref[0, :, 0:kw] * (GLA_DK ** -0.5)
    k = qk_ref[0, :, kw:2 * kw]
    q_dec = (q * jnp.exp(gcum)).astype(BF16)
    k_inv = (k * jnp.exp(-gcum)).astype(BF16)
    ri = lax.broadcasted_iota(jnp.int32, (c, c), 0)
    ci = lax.broadcasted_iota(jnp.int32, (c, c), 1)
    causal = ci <= ri
    lo = lax.broadcasted_iota(jnp.int32, (1, LANES), 1) < GLA_DK
    nchunk = tc // c
    heads = range(GLA_HEADS)
    zero = jnp.zeros((c, LANES), BF16)

    qm, vb, kd, egl = {}, {}, {}, {}
    for n in range(nchunk):
        sl = slice(n * c, (n + 1) * c)
        gl = gcum[n * c + c - 1:n * c + c, :]
        kdn = (k[sl] * jnp.exp(gl - gcum[sl])).astype(BF16)
        for h in heads:
            pr = slice((h // 2) * LANES, (h // 2 + 1) * LANES)
            keep = lo if h % 2 == 0 else jnp.logical_not(lo)
            qm[n, h] = jnp.where(keep, q_dec[sl, pr], zero)
            vb[n, h] = v_ref[0, sl, h * GLA_DV:(h + 1) * GLA_DV].astype(BF16)
            kd[n, h] = kdn[:, pr]
            egl[n, h] = jnp.exp(gl[:, pr])
    idx = [(n, h) for n in range(nchunk) for h in heads]
    attn = {i: jnp.where(causal, _dot_nt(qm[i], k_inv[i[0] * c:(i[0] + 1) * c, (i[1] // 2) * LANES:
                                                       (i[1] // 2 + 1) * LANES]), 0.0).astype(BF16) for i in idx}
    o_intra = {i: _dot(attn[i], vb[i]) for i in idx}
    kv = {i: _dot_tn(vb[i], kd[i]) for i in idx}

    st = [s_scr[h] for h in heads]
    for n in range(nchunk):
        sl = slice(n * c, (n + 1) * c)
        o = [o_intra[n, h] + _dot_nt(qm[n, h], st[h].astype(BF16)) for h in heads]
        st = [st[h] * egl[n, h] + kv[n, h] for h in heads]
        for h in heads:
            on = o[h] * lax.rsqrt(jnp.mean(o[h] * o[h], axis=-1, keepdims=True) + EPS) * on_ref[...]
            o_ref[0, sl, h * GLA_DV:(h + 1) * GLA_DV] = (
                on * _silu(r_ref[0, sl, h * GLA_DV:(h + 1) * GLA_DV])).astype(o_ref.dtype)
    for h in heads:
        s_scr[h] = st[h]


def gated_linear_attention(qkvr, small, wg_up, bg, on_gain, tc=512):
    b, t, _ = qkvr.shape
    wg = jnp.zeros((LANES, GLA_KW), F32).at[3 * NSA_HEADS:3 * NSA_HEADS + GLA_GATE_RANK].set(wg_up)
    idx = np.arange(tc)
    tril = ((idx[:, None] >= idx[None, :]) & (idx[:, None] // GLA_CHUNK == idx[None, :] // GLA_CHUNK))
    return pl.pallas_call(
        functools.partial(_gla_kernel, tc=tc),
        grid=(b, t // tc),
        in_specs=[pl.BlockSpec((1, tc, 2 * GLA_KW), lambda i, j: (i, j, 0)),
                  pl.BlockSpec((1, tc, GLA_W), lambda i, j: (i, j, 1)),
                  pl.BlockSpec((1, tc, GLA_W), lambda i, j: (i, j, 2)),
                  pl.BlockSpec((1, tc, LANES), lambda i, j: (i, j, 0)),
                  _full((LANES, GLA_KW)), _full((1, GLA_KW)), _full((1, GLA_DV)), _full((tc, tc))],
        out_specs=pl.BlockSpec((1, tc, GLA_W), lambda i, j: (i, j, 0)),
        out_shape=jax.ShapeDtypeStruct((b, t, GLA_W), BF16),
        scratch_shapes=[pltpu.VMEM((GLA_HEADS, GLA_DV, LANES), F32)],
        compiler_params=_cparams(("arbitrary", "arbitrary")),
        name="gla",
    )(qkvr, qkvr, qkvr, small, wg, bg.reshape(1, GLA_KW), on_gain.reshape(1, GLA_DV),
      jnp.asarray(tril.astype(np.float32)))


def odd_mixer(h, w_in, nsa_qn, nsa_kn, nsa_pos, nsa_cmp_w1, nsa_cmp_w2, gla_wg_up, gla_bg, gla_on, rel_bias):
    cuts = np.cumsum((0,) + OD_SIZES)
    col = lambda i: w_in[:, cuts[i]:cuts[i + 1]]
    dup = lambda a: jnp.concatenate([a[:, :NSA_DH], a[:, :NSA_DH], a[:, NSA_DH:], a[:, NSA_DH:]], axis=1)
    ep = _head_norm_epilogue(NSA_DH)
    bd = _block_diag_ones(NSA_W, NSA_DH)
    qg = (jnp.tile(nsa_qn, NSA_HEADS) * (NSA_DH ** -0.5 * LOG2E)).reshape(1, NSA_W)
    kg = jnp.tile(nsa_kn, NSA_HEADS).reshape(1, NSA_W)
    w_small = jnp.zeros((w_in.shape[0], LANES), F32)
    w_small = w_small.at[:, 0:24].set(col(7)).at[:, 24:40].set(col(11))
    nq, kcvc, ksw, vsw, small, qkvr = proj_multi(
        h, [(col(0), BF16, ep, (bd, qg)),
            (jnp.concatenate([col(1), col(2)], axis=1), F32, None, ()),
            (jnp.concatenate([dup(col(3)), dup(col(5))], axis=1), BF16, ep, (bd, kg)),
            (jnp.concatenate([dup(col(4)), dup(col(6))], axis=1), BF16, None, ()),
            (w_small, F32, None, ()),
            (jnp.concatenate([col(8), col(9), col(10), col(12)], axis=1), F32, None, ())], name="proj_odd")
    cmp_kv = nsa_compress(kcvc, nsa_pos, nsa_cmp_w1, nsa_cmp_w2, nsa_kn)
    o_cmp, sel = nsa_select(nq, cmp_kv, rel_bias)
    o_nsa = nsa_main(nq, ksw, vsw, sel, o_cmp, small, rel_bias)
    o_gla = gated_linear_attention(qkvr, small, gla_wg_up, gla_bg, gla_on)
    return o_nsa, o_gla


MOE_TM = 256
MOE_ROWS = 512


def _first_index(mask_val, idx, big, axis):
    return jnp.min(jnp.where(mask_val, idx, big), axis=axis, keepdims=True)


def _route_kernel(h_ref, rt_ref, b_ref, up_ref, eid_ref, rank_ref, w_ref, cnt_ref, run):
    tm = h_ref.shape[0]
    ne = N_EXPERTS
    gsz = ne // N_GROUPS

    @pl.when(pl.program_id(0) == 0)
    def _():
        run[...] = jnp.zeros_like(run)

    scores = _sigmoid(_dot_nt(rt_ref[...], h_ref[...], HI))
    biased = scores + b_ref[...]
    b3 = biased.reshape(N_GROUPS, gsz, tm)
    i3 = lax.broadcasted_iota(jnp.int32, (1, gsz, 1), 1).astype(F32)
    m1 = jnp.max(b3, axis=1, keepdims=True)
    f1 = _first_index(b3 == m1, i3, float(gsz), 1)
    m2 = jnp.max(jnp.where(i3 == f1, -jnp.inf, b3), axis=1, keepdims=True)
    gs = (m1 + m2).reshape(N_GROUPS, tm)
    gidx = lax.broadcasted_iota(jnp.int32, (N_GROUPS, 1), 0).astype(F32)
    gmask = jnp.zeros((N_GROUPS, tm), F32)
    for _ in range(TOPK_GROUPS):
        m = jnp.max(gs, axis=0, keepdims=True)
        pick = gidx == _first_index(gs == m, gidx, float(N_GROUPS), 0)
        gmask = jnp.where(pick, 1.0, gmask)
        gs = jnp.where(pick, -jnp.inf, gs)
    emask = jnp.broadcast_to(gmask.reshape(N_GROUPS, 1, tm), (N_GROUPS, gsz, tm)).reshape(ne, tm)
    work = jnp.where(emask > 0.5, biased, -jnp.inf)
    eidx = lax.broadcasted_iota(jnp.int32, (ne, 1), 0).astype(F32)
    picks, eids, ws = [], [], []
    for _ in range(TOP_K):
        m = jnp.max(work, axis=0, keepdims=True)
        first = _first_index(work == m, eidx, float(ne), 0)
        pick = eidx == first
        picks.append(pick)
        eids.append(first)
        ws.append(jnp.sum(jnp.where(pick, scores, 0.0), axis=0, keepdims=True))
        work = jnp.where(pick, -jnp.inf, work)
    wsum = ws[0]
    for k in range(1, TOP_K):
        wsum = wsum + ws[k]
    chosen = jnp.zeros((ne, tm), F32)
    for pick in picks:
        chosen = jnp.where(pick, 1.0, chosen)
    pos = run[...] + _dot(chosen.astype(BF16), up_ref[...])
    run[...] = run[...] + jnp.sum(chosen, axis=1, keepdims=True)
    cnt_ref[...] = run[...]
    row = lax.broadcasted_iota(jnp.int32, (8, 1), 0)
    eid_o = jnp.zeros((8, tm), F32)
    rank_o = jnp.zeros((8, tm), F32)
    w_o = jnp.zeros((LANES, tm), F32)
    rowl = lax.broadcasted_iota(jnp.int32, (LANES, 1), 0)
    for k in range(TOP_K):
        rk = jnp.sum(jnp.where(picks[k], pos, 0.0), axis=0, keepdims=True)
        eid_o = jnp.where(row == k, eids[k], eid_o)
        rank_o = jnp.where(row == k, rk, rank_o)
        w_o = jnp.where(rowl == k, ws[k] / wsum * ROUTE_SCALE, w_o)
    eid_ref[0] = eid_o.astype(jnp.int32)
    rank_ref[0] = rank_o.astype(jnp.int32)
    w_ref[...] = w_o.T


def moe_route(h2, router, e_bias, tm=MOE_TM):
    nt, d = h2.shape
    ne = N_EXPERTS
    up = jnp.asarray(np.triu(np.ones((tm, tm), np.float32), 1), dtype=BF16)
    nb = nt // tm
    return pl.pallas_call(
        _route_kernel,
        grid=(nb,),
        in_specs=[pl.BlockSpec((tm, d), lambda i: (i, 0)), _full((ne, d)), _full((ne, 1)), _full((tm, tm))],
        out_specs=[pl.BlockSpec((1, 8, tm), lambda i: (i, 0, 0)),
                   pl.BlockSpec((1, 8, tm), lambda i: (i, 0, 0)),
                   pl.BlockSpec((tm, LANES), lambda i: (i, 0)),
                   _full((ne, 1))],
        out_shape=[jax.ShapeDtypeStruct((nb, 8, tm), jnp.int32), jax.ShapeDtypeStruct((nb, 8, tm), jnp.int32),
                   jax.ShapeDtypeStruct((nt, LANES), F32), jax.ShapeDtypeStruct((ne, 1), F32)],
        scratch_shapes=[pltpu.VMEM((ne, 1), F32)],
        compiler_params=_cparams(("arbitrary",)),
        name="moe_route",
    )(h2, router.T, e_bias.reshape(ne, 1), up)


def _dispatch_kernel(dest_ref, h_ref, xs_ref, sem):
    tm = h_ref.shape[0]

    def copy(t, row):
        return pltpu.make_async_copy(h_ref.at[pl.ds(t, 1), :], xs_ref.at[pl.ds(row, 1), :], sem)

    def issue(t, _):
        for k in range(TOP_K):
            copy(t, dest_ref[0, t, k]).start()
        return 0

    def drain(t, _):
        for k in range(TOP_K):
            copy(0, 0).wait()
        return 0

    lax.fori_loop(0, tm, issue, 0, unroll=4)
    lax.fori_loop(0, tm, drain, 0, unroll=4)


def moe_dispatch(h2, dest, tm=MOE_TM):
    nt, d = h2.shape
    return pl.pallas_call(
        _dispatch_kernel,
        grid=(nt // tm,),
        in_specs=[pl.BlockSpec((1, tm, 8), lambda i: (i, 0, 0), memory_space=pltpu.SMEM),
                  pl.BlockSpec((tm, d), lambda i: (i, 0))],
        out_specs=pl.BlockSpec(memory_space=pl.ANY),
        scratch_shapes=[pltpu.SemaphoreType.DMA(())],
        out_shape=jax.ShapeDtypeStruct((nt * TOP_K, d), F32),
        compiler_params=_cparams(("arbitrary",)),
        name="moe_dispatch",
    )(dest, h2)


def _ffn_kernel(blk_ref, exp_ref, lo_ref, hi_ref, first_ref, valid_ref, x_ref, wg_ref, wu_ref, wd_ref, o_ref,
                wg_b, wu_b, wd_b):
    i = pl.program_id(0)
    rows = x_ref.shape[0]

    @pl.when((i == 0) | (exp_ref[i] != exp_ref[jnp.maximum(i - 1, 0)]))
    def _():
        wg_b[...] = wg_ref[0].astype(BF16)
        wu_b[...] = wu_ref[0].astype(BF16)
        wd_b[...] = wd_ref[0].astype(BF16)

    @pl.when(valid_ref[i] == 1)
    def _():
        x = x_ref[...].astype(BF16)
        a = _dot(x, wg_b[...])
        u = _dot(x, wu_b[...])
        y = _dot((_silu(a) * u).astype(BF16), wd_b[...])
        r = blk_ref[i] * rows + lax.broadcasted_iota(jnp.int32, (rows, 1), 0)
        y = jnp.where((r >= lo_ref[i]) & (r < hi_ref[i]), y, 0.0)

        @pl.when(first_ref[i] == 1)
        def _():
            o_ref[...] = y

        @pl.when(first_ref[i] == 0)
        def _():
            o_ref[...] = o_ref[...] + y


def _items_kernel(cnt_ref, starts_ref, blk_ref, exp_ref, lo_ref, hi_ref, first_ref, valid_ref, *, rows, n_items):
    shift = int(math.log2(rows))

    def expert(e, carry):
        start, n = carry
        c = cnt_ref[e]
        starts_ref[e] = start
        end = start + c
        first_blk = lax.shift_right_logical(start, shift)
        n_blk = jnp.where(c > 0, lax.shift_right_logical(jnp.maximum(end - 1, 0), shift) - first_blk + 1, 0)

        def item(k, n):
            b = first_blk + k
            lo = jnp.maximum(start, b * rows)
            blk_ref[n] = b
            exp_ref[n] = e
            lo_ref[n] = lo
            hi_ref[n] = jnp.minimum(end, (b + 1) * rows)
            first_ref[n] = (lo == b * rows).astype(jnp.int32)
            valid_ref[n] = 1
            return n + 1

        return end, lax.fori_loop(0, n_blk, item, n)

    _, total = lax.fori_loop(0, N_EXPERTS, expert, (jnp.int32(0), jnp.int32(0)))
    last = jnp.maximum(total - 1, 0)

    def fill(k, _):
        blk_ref[k] = blk_ref[last]
        exp_ref[k] = exp_ref[last]
        lo_ref[k] = 0
        hi_ref[k] = 0
        first_ref[k] = 0
        valid_ref[k] = 0
        return 0

    lax.fori_loop(total, n_items, fill, 0)


def _ffn_items(counts, n_rows, rows):
    n_items = n_rows // rows + N_EXPERTS - 1
    smem = pl.BlockSpec(memory_space=pltpu.SMEM)
    out = pl.pallas_call(
        functools.partial(_items_kernel, rows=rows, n_items=n_items),
        in_specs=[smem],
        out_specs=[smem] * 7,
        out_shape=[jax.ShapeDtypeStruct((N_EXPERTS,), jnp.int32)]
                  + [jax.ShapeDtypeStruct((n_items,), jnp.int32)] * 6,
        name="moe_items",
    )(counts)
    return out[0], tuple(out[1:])


def moe_ffn_sorted(xs, items, wg, wu, wd, layer, rows=MOE_ROWS):
    n_rows, d = xs.shape
    n_items = items[0].shape[0]
    de = wg.shape[-1]
    return pl.pallas_call(
        _ffn_kernel,
        grid_spec=pltpu.PrefetchScalarGridSpec(
            num_scalar_prefetch=6,
            grid=(n_items,),
            in_specs=[pl.BlockSpec((rows, d), lambda i, blk, e, *_: (blk[i], 0)),
                      pl.BlockSpec((None, 1, d, de), lambda i, blk, e, *_: (layer, e[i], 0, 0)),
                      pl.BlockSpec((None, 1, d, de), lambda i, blk, e, *_: (layer, e[i], 0, 0)),
                      pl.BlockSpec((None, 1, de, d), lambda i, blk, e, *_: (layer, e[i], 0, 0))],
            out_specs=pl.BlockSpec((rows, d), lambda i, blk, e, *_: (blk[i], 0)),
            scratch_shapes=[pltpu.VMEM((d, de), BF16), pltpu.VMEM((d, de), BF16), pltpu.VMEM((de, d), BF16)]),
        out_shape=jax.ShapeDtypeStruct((n_rows, d), F32),
        compiler_params=_cparams(("arbitrary",)),
        name="moe_ffn",
    )(*items, xs, wg, wu, wd)


def _combine_kernel(dest_ref, ys_ref, w_ref, h_ref, x_ref, g_ref, sg_ref, su_ref, sd_ref, o_ref, buf, sem):
    tm = h_ref.shape[0]

    def copy(t, k, row):
        return pltpu.make_async_copy(ys_ref.at[pl.ds(row, 1), :], buf.at[k, pl.ds(t, 1), :], sem)

    def issue(t, _):
        for k in range(TOP_K):
            copy(t, k, dest_ref[0, t, k]).start()
        return 0

    def drain(t, _):
        for k in range(TOP_K):
            copy(0, 0, 0).wait()
        return 0

    lax.fori_loop(0, tm, issue, 0, unroll=4)
    hb = h_ref[...].astype(BF16)
    y = _dot((_silu(_dot(hb, sg_ref[...])) * _dot(hb, su_ref[...])).astype(BF16), sd_ref[...])
    lax.fori_loop(0, tm, drain, 0, unroll=4)
    w = w_ref[...]
    for k in range(TOP_K):
        y = y + w[:, k:k + 1] * buf[k]
    o_ref[...] = x_ref[...] + g_ref[0] * y


def moe_combine(ys, dest, w, h2, x2, gate, sg, su, sd, seq, tm=MOE_TM):
    nt, d = h2.shape
    ds_ = sg.shape[-1]
    per_b = seq // tm
    tile = lambda: pl.BlockSpec((tm, d), lambda i: (i, 0))
    return pl.pallas_call(
        _combine_kernel,
        grid=(nt // tm,),
        in_specs=[pl.BlockSpec((1, tm, 8), lambda i: (i, 0, 0), memory_space=pltpu.SMEM),
                  pl.BlockSpec(memory_space=pl.ANY),
                  pl.BlockSpec((tm, LANES), lambda i: (i, 0)), tile(), tile(),
                  pl.BlockSpec((1, 1, d), lambda i: (i // per_b, 0, 0)),
                  _full((d, ds_)), _full((d, ds_)), _full((ds_, d))],
        out_specs=tile(),
        scratch_shapes=[pltpu.VMEM((TOP_K, tm, d), F32), pltpu.SemaphoreType.DMA(())],
        out_shape=jax.ShapeDtypeStruct((nt, d), F32),
        compiler_params=_cparams(("arbitrary",)),
        name="moe_combine",
    )(dest, ys, w, h2, x2, gate, sg.astype(BF16), su.astype(BF16), sd.astype(BF16))


def moe_layer(x, g_norm, sc, sh, gate, router, e_bias, wg, wu, wd, layer, sg, su, sd):
    b, t, d = x.shape
    nt = b * t
    h = ln_mod(x, g_norm, sc, sh, F32)
    h2 = h.reshape(nt, d)
    eid, rank, w, counts = moe_route(h2, router, e_bias)
    starts, items = _ffn_items(counts.reshape(-1).astype(jnp.int32), nt * TOP_K, MOE_ROWS)
    hit = eid[..., None] == jnp.arange(N_EXPERTS, dtype=jnp.int32)
    dest = jnp.sum(jnp.where(hit, starts.astype(jnp.int32), 0), axis=-1) + rank
    dest = jnp.swapaxes(dest, 1, 2)
    xs = moe_dispatch(h2, dest)
    ys = moe_ffn_sorted(xs, items, wg, wu, wd, layer)
    out = moe_combine(ys, dest, w, h2, x.reshape(nt, d), gate.reshape(b, 1, d), sg, su, sd, t)
    return out.reshape(b, t, d)


def kernel(x, c, ada_w, ada_b, norm_mix, norm_ffn, rel_bias, ev_w_in, ev_w_out, fox_fb, fox_qn, fox_kn, gdn_conv, gdn_a_log, gdn_dt_bias, gdn_on, od_w_in, od_w_out, nsa_qn, nsa_kn, nsa_pos, nsa_cmp_w1, nsa_cmp_w2, gla_wg_up, gla_bg, gla_on, moe_router, moe_bias, moe_wg, moe_wu, moe_wd, sh_wg, sh_wu, sh_wd):
    d = x.shape[-1]
    depth = ada_w.shape[0]
    mod = adaln(c, ada_w, ada_b)
    for layer in range(depth):
        sh1, sc1, g1, sh2, sc2, g2 = [mod[layer, :, i * d:(i + 1) * d] for i in range(6)]
        h = ln_mod(x, norm_mix[layer], sc1, sh1, BF16)
        j = layer // 2
        if layer % 2 == 0:
            y1, y2 = even_mixer(h, ev_w_in[j], fox_fb[j], fox_qn[j], fox_kn[j], gdn_conv[j], gdn_a_log[j],
                                gdn_dt_bias[j], gdn_on[j])
            w_out = ev_w_out[j]
        else:
            y1, y2 = odd_mixer(h, od_w_in[j], nsa_qn[j], nsa_kn[j], nsa_pos[j], nsa_cmp_w1[j], nsa_cmp_w2[j],
                               gla_wg_up[j], gla_bg[j], gla_on[j], rel_bias)
            w_out = od_w_out[j]
        x = out_proj(y1, y2, w_out, x, g1)
        x = moe_layer(x, norm_ffn[layer], sc2, sh2, g2, moe_router[layer], moe_bias[layer], moe_wg, moe_wu, moe_wd,
                      layer, sh_wg[layer], sh_wu[layer], sh_wd[layer])
    return x
```

```python
import functools
import math

import numpy as np
import jax
import jax.numpy as jnp
from jax import lax
from jax.experimental import pallas as pl
from jax.experimental.pallas import tpu as pltpu

F32 = jnp.float32
BF16 = jnp.bfloat16
HI = lax.Precision.HIGHEST

EPS = 1e-6
LOG2E = math.log2(math.e)
NEG = -1e30

FOX_HEADS, FOX_DH = 8, 64
GDN_HEADS, GDN_DH, GDN_CONV = 4, 128, 4
NSA_HEADS, NSA_KV_HEADS, NSA_DH = 8, 2, 64
NSA_GROUP = NSA_HEADS // NSA_KV_HEADS
CMP_LEN, CMP_STRIDE, CMP_HIDDEN = 32, 16, 256
SLC_LEN, SLC_TOPK, WINDOW = 64, 16, 512
GLA_HEADS, GLA_DK, GLA_DV, GLA_GATE_RANK, GLA_TAU, GLA_CHUNK = 4, 64, 128, 16, 16.0, 64
REL_BUCKETS, REL_MAX_DIST = 32, 128
N_EXPERTS, TOP_K, D_EXPERT, D_SHARED = 64, 6, 256, 256
N_GROUPS, TOPK_GROUPS, ROUTE_SCALE = 8, 4, 2.5

FOX_W = FOX_HEADS * FOX_DH
GDN_W = GDN_HEADS * GDN_DH
NSA_W = NSA_HEADS * NSA_DH
NSA_KV_W = NSA_KV_HEADS * NSA_DH
GLA_KW = GLA_HEADS * GLA_DK
GLA_W = GLA_HEADS * GLA_DV
EV_SIZES = (FOX_W, FOX_W, FOX_W, FOX_HEADS, 3 * GDN_W, GDN_HEADS, GDN_HEADS, GDN_W)
OD_SIZES = (NSA_W,) + (NSA_KV_W,) * 6 + (3 * NSA_HEADS, GLA_KW, GLA_KW, GLA_W, GLA_GATE_RANK, GLA_W)

LANES = 128
SUP_BLOCKS = 64
VMEM_LIMIT = 56 * 1024 * 1024


def _cparams(sem, flags=None):
    return pltpu.CompilerParams(dimension_semantics=sem, vmem_limit_bytes=VMEM_LIMIT, flags=flags)


def _full(shape):
    n = len(shape)
    return pl.BlockSpec(shape, lambda *_: (0,) * n)


def _dot(a, b):
    return jnp.dot(a, b, preferred_element_type=F32)


def _dot_hi(a, b):
    return jnp.dot(a, b, precision=HI, preferred_element_type=F32)


def _dot_nt(a, b, precision=None):
    return lax.dot_general(a, b, (((1,), (1,)), ((), ())), precision=precision, preferred_element_type=F32)


def _dot_tn(a, b, precision=None):
    return lax.dot_general(a, b, (((0,), (0,)), ((), ())), precision=precision, preferred_element_type=F32)


def _sigmoid(x):
    return 1.0 / (1.0 + jnp.exp(-x))


def _silu(x):
    return x * _sigmoid(x)


def _softplus(x):
    return jnp.maximum(x, 0.0) + jnp.log(1.0 + jnp.exp(-jnp.abs(x)))


def _log_sigmoid(x):
    return -_softplus(-x)


def _adaln_kernel(c_ref, w_ref, b_ref, o_ref):
    c = c_ref[...]
    o_ref[0] = _dot_hi(_silu(c), w_ref[0]) + b_ref[0]


def adaln(c, ada_w, ada_b):
    depth, d, n = ada_w.shape
    b = c.shape[0]
    cp = jnp.zeros((8, d), F32).at[:b].set(c)
    tn = 1536
    out = pl.pallas_call(
        _adaln_kernel,
        grid=(depth, n // tn),
        in_specs=[_full((8, d)),
                  pl.BlockSpec((1, d, tn), lambda l, j: (l, 0, j)),
                  pl.BlockSpec((1, 1, tn), lambda l, j: (l, 0, j))],
        out_specs=pl.BlockSpec((1, 8, tn), lambda l, j: (l, 0, j)),
        out_shape=jax.ShapeDtypeStruct((depth, 8, n), F32),
        compiler_params=_cparams(("arbitrary", "arbitrary")),
        name="adaln",
    )(cp, ada_w, ada_b.reshape(depth, 1, n))
    return out[:, :b]


def _ln_kernel(x_ref, g_ref, sc_ref, sh_ref, o_ref):
    x = x_ref[0]
    y = x * lax.rsqrt(jnp.mean(x * x, axis=-1, keepdims=True) + EPS) * g_ref[...]
    o_ref[0] = (y * (1.0 + sc_ref[0]) + sh_ref[0]).astype(o_ref.dtype)


def ln_mod(x, g, sc, sh, out_dtype, tm=512):
    b, t, d = x.shape
    return pl.pallas_call(
        _ln_kernel,
        grid=(b, t // tm),
        in_specs=[pl.BlockSpec((1, tm, d), lambda i, j: (i, j, 0)),
                  _full((1, d)),
                  pl.BlockSpec((1, 1, d), lambda i, j: (i, 0, 0)),
                  pl.BlockSpec((1, 1, d), lambda i, j: (i, 0, 0))],
        out_specs=pl.BlockSpec((1, tm, d), lambda i, j: (i, j, 0)),
        out_shape=jax.ShapeDtypeStruct((b, t, d), out_dtype),
        compiler_params=_cparams(("arbitrary", "arbitrary")),
        name="ln_mod",
    )(x, g.reshape(1, d), sc.reshape(b, 1, d), sh.reshape(b, 1, d))


def proj_multi(h, groups, tm=512, name="proj"):
    b, t, d = h.shape
    widths = [g[0].shape[1] for g in groups]
    starts = np.cumsum([0] + widths)
    w_cat = jnp.concatenate([g[0] for g in groups], axis=1).astype(BF16)
    extras = [e for g in groups for e in g[3]]
    n_ex = [len(g[3]) for g in groups]
    n_out = len(groups)

    def kern(h_ref, w_ref, *rest):
        ex_refs = rest[:len(extras)]
        o_refs = rest[len(extras):]
        y = _dot(h_ref[0], w_ref[...])
        pos = 0
        for gi, (_, out_dtype, epilogue, _) in enumerate(groups):
            yg = y[:, starts[gi]:starts[gi + 1]]
            if epilogue is not None:
                yg = epilogue(yg, *[e[...] for e in ex_refs[pos:pos + n_ex[gi]]])
            pos += n_ex[gi]
            o_refs[gi][0] = yg.astype(out_dtype)

    return pl.pallas_call(
        kern,
        grid=(b, t // tm),
        in_specs=[pl.BlockSpec((1, tm, d), lambda i, j: (i, j, 0)), _full((d, int(starts[-1])))]
                 + [_full(e.shape) for e in extras],
        out_specs=[pl.BlockSpec((1, tm, n), lambda i, j: (i, j, 0)) for n in widths],
        out_shape=[jax.ShapeDtypeStruct((b, t, n), g[1]) for n, g in zip(widths, groups)],
        compiler_params=_cparams(("arbitrary", "arbitrary")),
        name=name,
    )(h, w_cat, *extras)


def _head_norm_epilogue(dh):
    inv = 1.0 / dh

    def ep(y, bd, gain):
        ssq = _dot((y * y).astype(BF16), bd)
        return y * lax.rsqrt(ssq * inv + EPS) * gain

    return ep


def _block_diag_ones(n, dh):
    i = np.arange(n) // dh
    return jnp.asarray((i[:, None] == i[None, :]).astype(np.float32), dtype=BF16)


def _outproj_kernel(y1_ref, y2_ref, wa_ref, wb_ref, x_ref, g_ref, o_ref):
    y = _dot(y1_ref[0], wa_ref[...]) + _dot(y2_ref[0], wb_ref[...])
    o_ref[0] = x_ref[0] + g_ref[0] * y


def out_proj(y1, y2, w_out, x, gate, tm=512):
    b, t, d = x.shape
    n1, n2 = y1.shape[-1], y2.shape[-1]
    wa = w_out[:n1].astype(BF16)
    wb = w_out[n1:].astype(BF16)
    return pl.pallas_call(
        _outproj_kernel,
        grid=(b, t // tm),
        in_specs=[pl.BlockSpec((1, tm, n1), lambda i, j: (i, j, 0)),
                  pl.BlockSpec((1, tm, n2), lambda i, j: (i, j, 0)),
                  _full((n1, d)), _full((n2, d)),
                  pl.BlockSpec((1, tm, d), lambda i, j: (i, j, 0)),
                  pl.BlockSpec((1, 1, d), lambda i, j: (i, 0, 0))],
        out_specs=pl.BlockSpec((1, tm, d), lambda i, j: (i, j, 0)),
        out_shape=jax.ShapeDtypeStruct((b, t, d), F32),
        compiler_params=_cparams(("arbitrary", "arbitrary")),
        name="out_proj",
    )(y1, y2, wa, wb, x, gate.reshape(b, 1, d))


def _decay_kernel(s_ref, fb_ref, tril_ref, place_ref, o_ref, carry):
    @pl.when(pl.program_id(1) == 0)
    def _():
        carry[...] = jnp.zeros_like(carry)

    tm = s_ref.shape[1]
    lf = _log_sigmoid(s_ref[0] + fb_ref[...])
    cum = _dot_hi(tril_ref[...], lf) + carry[...]
    carry[...] = cum[tm - 1:tm, :]
    x = cum * LOG2E
    hi = x.astype(BF16)
    r1 = x - hi.astype(F32)
    mid = r1.astype(BF16)
    low = (r1 - mid.astype(F32)).astype(BF16)
    o_ref[0] = _dot(jnp.concatenate([hi, mid, low], axis=1), place_ref[...]).astype(o_ref.dtype)


def fox_decay(small, fox_fb, tm=512):
    b, t, _ = small.shape
    fb = jnp.zeros((1, LANES), F32).at[0, :FOX_HEADS].set(fox_fb)
    tril = jnp.asarray(np.tril(np.ones((tm, tm), np.float32)))
    place = np.zeros((3 * LANES, FOX_W), np.float32)
    for h in range(FOX_HEADS):
        for j in range(3):
            place[j * LANES + h, (h // 2) * LANES + (FOX_DH if h % 2 == 0 else 0) + j] = 1.0
    return pl.pallas_call(
        _decay_kernel,
        grid=(b, t // tm),
        in_specs=[pl.BlockSpec((1, tm, LANES), lambda i, j: (i, j, 0)), _full((1, LANES)), _full((tm, tm)),
                  _full((3 * LANES, FOX_W))],
        out_specs=pl.BlockSpec((1, tm, FOX_W), lambda i, j: (i, j, 0)),
        out_shape=jax.ShapeDtypeStruct((b, t, FOX_W), BF16),
        scratch_shapes=[pltpu.VMEM((1, LANES), F32)],
        compiler_params=_cparams(("arbitrary", "arbitrary")),
        name="fox_decay",
    )(small, fb, tril, jnp.asarray(place, dtype=BF16))


def _fox_kernel(q_ref, k_ref, v_ref, f_ref, o_ref, *, tq, wide):
    i = pl.program_id(2)
    lane = lax.broadcasted_iota(jnp.int32, (1, LANES), 1)
    lo = lane < FOX_DH
    coef = jnp.where((lane & (FOX_DH - 1)) < 3, -1.0, 0.0).astype(BF16)
    q = q_ref[0]
    qs = (jnp.where(lo, q, coef), jnp.where(lo, coef, q))
    causal = (lax.broadcasted_iota(jnp.int32, (tq, tq), 1) <= lax.broadcasted_iota(jnp.int32, (tq, tq), 0))
    one = jnp.ones((1, LANES), BF16)

    def step(j, carry, tw, diag=False):
        start = pl.multiple_of(j * tw, tw)
        kt = k_ref[0, pl.ds(start, tw), :]
        ft = f_ref[0, pl.ds(start, tw), :]
        vt = v_ref[0, pl.ds(start, tw), :]
        s_pair = (_dot_nt(qs[0], jnp.where(lo, kt, ft)), _dot_nt(qs[1], jnp.where(lo, ft, kt)))
        vs = (jnp.where(lo, vt, one), jnp.where(lo, one, vt))
        new = []
        for hh in range(2):
            m, acc = carry[hh]
            s = s_pair[hh]
            if diag:
                s = jnp.where(causal, s, NEG)
            m_new = jnp.maximum(m, jnp.max(s, axis=1, keepdims=True))
            p = jnp.exp2(s - m_new)
            acc = jnp.exp2(m - m_new) * acc + _dot(p.astype(BF16), vs[hh])
            new.append((m_new, acc))
        return tuple(new)

    carry = tuple((jnp.full((tq, 1), NEG, F32), jnp.zeros((tq, LANES), F32)) for _ in range(2))
    n_wide = i // wide
    carry = lax.fori_loop(0, n_wide, lambda j, c: step(j, c, wide * tq), carry)
    done = n_wide * wide
    part = wide // 2
    while part >= 1:
        carry = lax.cond((i & part) != 0, lambda c, d=done, w=part: step(d // w, c, w * tq), lambda c: c, carry)
        done = done + (i & part)
        part //= 2
    carry = step(i, carry, tq, diag=True)
    acc = jnp.where(lo, carry[0][1], carry[1][1])
    den = jnp.where(lo, carry[1][1], carry[0][1])
    o_ref[0] = (acc / pltpu.roll(den, FOX_DH, 1)).astype(o_ref.dtype)


def fox_attention(q, k, v, feat, tq=512, wide=4):
    b, t, w = q.shape
    npair = w // LANES
    nt = t // tq
    whole = lambda: pl.BlockSpec((1, t, LANES), lambda bi, p, i: (bi, 0, p))
    return pl.pallas_call(
        functools.partial(_fox_kernel, tq=tq, wide=wide),
        grid=(b, npair, nt),
        in_specs=[pl.BlockSpec((1, tq, LANES), lambda bi, p, i: (bi, i, p)), whole(), whole(), whole()],
        out_specs=pl.BlockSpec((1, tq, LANES), lambda bi, p, i: (bi, i, p)),
        out_shape=jax.ShapeDtypeStruct((b, t, w), BF16),
        compiler_params=_cparams(("arbitrary", "arbitrary", "arbitrary")),
        name="fox_attn",
    )(q, k, v, feat)


def _mm(a, b):
    return _dot(a.astype(BF16), b.astype(BF16))


def _mm3(a, b):
    ah = a.astype(BF16)
    bh = b.astype(BF16)
    al = (a - ah.astype(F32)).astype(BF16)
    bl = (b - bh.astype(F32)).astype(BF16)
    return _dot(jnp.concatenate([ah, ah, al], axis=1), jnp.concatenate([bh, bl, bh], axis=0))


def _tril_solve(a, rhs, ri, ci):
    n = a[0].shape[0]
    both = lambda f, x, y: [f(p, q) for p, q in zip(x, y)]
    eye = (ri == ci).astype(F32)
    same = lambda b: (lax.shift_right_logical(ri, int(math.log2(b)))
                      == lax.shift_right_logical(ci, int(math.log2(b))))
    base = 16
    d = [jnp.where(same(base), p, 0.0) for p in a]
    d2 = both(_mm, d, d)
    d4 = both(_mm, d2, d2)
    r1 = [eye - p + p2 - t for p, p2, t in zip(d, d2, both(_mm, d, d2))]
    d8 = both(_mm, d4, d4)
    r2 = [eye + p4 + p8 + t for p4, p8, t in zip(d4, d8, both(_mm, d4, d8))]
    t = both(_mm, r1, r2)
    b = base
    while b < n:
        join = same(2 * b) & jnp.logical_not(same(b))
        low = [jnp.where(join, p, 0.0) for p in a]
        t = [p - q for p, q in zip(t, both(_mm, both(_mm, t, low), t))]
        b *= 2
    return both(_mm3, t, rhs)


GDN_BLOCK = 128


def _gdn_kernel(x_ref, sm_ref, z_ref, cw_ref, ega_ref, egb_ref, alog_ref, dtb_ref, on_ref, tril_ref,
                o_ref, s_scr, prev_scr, *, tc):
    c = GDN_BLOCK
    w = GDN_W

    @pl.when(pl.program_id(1) == 0)
    def _():
        s_scr[...] = jnp.zeros_like(s_scr)
        prev_scr[...] = jnp.zeros_like(prev_scr)

    x = x_ref[0]
    prev = prev_scr[...]
    row8 = lax.broadcasted_iota(jnp.int32, (8, 1), 0)
    acc = x * cw_ref[GDN_CONV - 1:GDN_CONV, :]
    for s in range(1, GDN_CONV):
        rolled = pltpu.roll(x, s, 0)
        head = jnp.where(row8 < s, pltpu.roll(prev, s, 0), rolled[0:8])
        shifted = jnp.concatenate([head, rolled[8:]], axis=0)
        acc = acc + shifted * cw_ref[GDN_CONV - 1 - s:GDN_CONV - s, :]
    prev_scr[...] = x[tc - 8:tc]
    xc = _silu(acc)

    sm = sm_ref[0]
    g_raw = _dot_hi(sm, ega_ref[...])
    b_raw = _dot_hi(sm, egb_ref[...])
    g = -jnp.exp(alog_ref[...]) * _softplus(g_raw + dtb_ref[...])
    beta_all = _sigmoid(b_raw)
    gc_all = _dot_hi(tril_ref[...], g)

    ri = lax.broadcasted_iota(jnp.int32, (c, c), 0)
    ci = lax.broadcasted_iota(jnp.int32, (c, c), 1)
    causal = ci <= ri
    strict = ci < ri

    nblk = tc // c
    a_l, attn_l, rhs_l, qd_l, kd_l, egl_l = [], [], [], [], [], []
    for h in range(GDN_HEADS):
        ln = slice(h * GDN_DH, (h + 1) * GDN_DH)
        qh = xc[:, h * GDN_DH:(h + 1) * GDN_DH]
        kh = xc[:, w + h * GDN_DH:w + (h + 1) * GDN_DH]
        qh = qh * lax.rsqrt(jnp.sum(qh * qh, axis=-1, keepdims=True) + EPS) * (GDN_DH ** -0.5)
        kh = kh * lax.rsqrt(jnp.sum(kh * kh, axis=-1, keepdims=True) + EPS)
        vh = xc[:, 2 * w + h * GDN_DH:2 * w + (h + 1) * GDN_DH]
        gch = gc_all[:, ln]
        gct = gch.T
        egc = jnp.exp(gch)
        bh = beta_all[:, ln]
        for n in range(nblk):
            sl = slice(n * c, (n + 1) * c)
            q, k, v, gc, be = qh[sl], kh[sl], vh[sl], gch[sl], bh[sl]
            decay = jnp.exp(jnp.where(causal, gc - gct[:, sl], NEG))
            kb = k * be
            kk = _dot_nt(jnp.concatenate([kb, q], axis=0).astype(BF16), k.astype(BF16))
            a_l.append(jnp.where(strict, kk[:c] * decay, 0.0))
            attn_l.append(jnp.where(causal, kk[c:] * decay, 0.0))
            rhs_l.append(jnp.concatenate([v * be, kb * egc[sl]], axis=1))
            gl = gc[c - 1:c, :]
            qd_l.append(q * egc[sl])
            kd_l.append(k * jnp.exp(gl - gc))
            egl_l.append(jnp.exp(gl))
    uw_l = _tril_solve(a_l, rhs_l, ri, ci)

    states = [s_scr[h] for h in range(GDN_HEADS)]
    for n in range(nblk):
        sl = slice(n * c, (n + 1) * c)
        idx = [h * nblk + n for h in range(GDN_HEADS)]
        ws = [_mm(jnp.concatenate([uw_l[i][:, GDN_DH:], qd_l[i]], axis=0), states[h])
              for h, i in enumerate(idx)]
        v_new = [uw_l[i][:, :GDN_DH] - ws[h][:c] for h, i in enumerate(idx)]
        o = [ws[h][c:] + _mm(attn_l[i], v_new[h]) for h, i in enumerate(idx)]
        states = [states[h] * egl_l[i] + _dot_tn(kd_l[i].astype(BF16), v_new[h].astype(BF16))
                  for h, i in enumerate(idx)]
        for h in range(GDN_HEADS):
            ln = slice(h * GDN_DH, (h + 1) * GDN_DH)
            on = o[h] * lax.rsqrt(jnp.mean(o[h] * o[h], axis=-1, keepdims=True) + EPS) * on_ref[...]
            o_ref[0, sl, ln] = (on * _silu(z_ref[0, sl, ln])).astype(o_ref.dtype)
    for h in range(GDN_HEADS):
        s_scr[h] = states[h]


def gated_delta_net(x, small, z, conv_w, a_log, dt_bias, on_gain, tc=512):
    b, t, _ = x.shape
    w = GDN_W
    ega = np.zeros((LANES, w), np.float32)
    egb = np.zeros((LANES, w), np.float32)
    for h in range(GDN_HEADS):
        ega[FOX_HEADS + h, h * GDN_DH:(h + 1) * GDN_DH] = 1.0
        egb[FOX_HEADS + GDN_HEADS + h, h * GDN_DH:(h + 1) * GDN_DH] = 1.0
    alog = jnp.repeat(a_log, GDN_DH).reshape(1, w)
    dtb = jnp.repeat(dt_bias, GDN_DH).reshape(1, w)
    idx = np.arange(tc)
    tril = ((idx[:, None] >= idx[None, :]) & (idx[:, None] // GDN_BLOCK == idx[None, :] // GDN_BLOCK))
    row = lambda n: pl.BlockSpec((1, tc, n), lambda i, j: (i, j, 0))
    return pl.pallas_call(
        functools.partial(_gdn_kernel, tc=tc),
        grid=(b, t // tc),
        in_specs=[row(3 * w), row(LANES), row(w), _full((GDN_CONV, 3 * w)), _full((LANES, w)), _full((LANES, w)),
                  _full((1, w)), _full((1, w)), _full((1, GDN_DH)), _full((tc, tc))],
        out_specs=row(w),
        out_shape=jax.ShapeDtypeStruct((b, t, w), BF16),
        scratch_shapes=[pltpu.VMEM((GDN_HEADS, GDN_DH, GDN_DH), F32), pltpu.VMEM((8, 3 * w), F32)],
        compiler_params=_cparams(("arbitrary", "arbitrary")),
        name="gdn",
    )(x, small, z, conv_w, jnp.asarray(ega), jnp.asarray(egb), alog, dtb, on_gain.reshape(1, GDN_DH),
      jnp.asarray(tril.astype(np.float32)))


def even_mixer(h, w_in, fox_fb, fox_qn, fox_kn, gdn_conv, gdn_a_log, gdn_dt_bias, gdn_on):
    cuts = np.cumsum((0,) + EV_SIZES)
    col = lambda i: w_in[:, cuts[i]:cuts[i + 1]]
    bd = _block_diag_ones(FOX_W, FOX_DH)
    ep = _head_norm_epilogue(FOX_DH)
    qg = (jnp.tile(fox_qn, FOX_HEADS) * (FOX_DH ** -0.5 * LOG2E)).reshape(1, FOX_W)
    kg = jnp.tile(fox_kn, FOX_HEADS).reshape(1, FOX_W)
    w_small = jnp.zeros((w_in.shape[0], LANES), F32)
    w_small = w_small.at[:, 0:8].set(col(3)).at[:, 8:12].set(col(5)).at[:, 12:16].set(col(6))
    fq, fk, fv, small, gqkv, gz = proj_multi(
        h, [(col(0), BF16, ep, (bd, qg)), (col(1), BF16, ep, (bd, kg)), (col(2), BF16, None, ()),
            (w_small, F32, None, ()), (col(4), F32, None, ()), (col(7), F32, None, ())], name="proj_even")
    feat = fox_decay(small, fox_fb)
    o_fox = fox_attention(fq, fk, fv, feat)
    o_gdn = gated_delta_net(gqkv, small, gz, gdn_conv, gdn_a_log, gdn_dt_bias, gdn_on)
    return o_fox, o_gdn


def _t5_bucket_np(dist):
    n = np.maximum(dist, 0)
    exact = REL_BUCKETS // 2
    nf = np.maximum(n, 1).astype(np.float32)
    large = exact + (np.log(nf / np.float32(exact)) / np.float32(math.log(REL_MAX_DIST / exact))
                     * np.float32(REL_BUCKETS - exact)).astype(np.int32)
    large = np.minimum(large, REL_BUCKETS - 1)
    return np.where(n < exact, n, large)


def _bias_kernel(tbl_ref, bucket_ref, o_ref):
    h = pl.program_id(0)
    bucket = bucket_ref[...]
    acc = jnp.full(bucket.shape, NEG, F32)
    for b in range(REL_BUCKETS):
        acc = jnp.where(bucket == b, tbl_ref[b, h], acc)
    o_ref[0] = acc


def _bias_table(rel_bias, dist, valid):
    shifted = (rel_bias - rel_bias[REL_BUCKETS - 1:REL_BUCKETS]) * LOG2E
    bucket = np.where(valid, _t5_bucket_np(dist), -1).astype(np.int32)
    rows, cols = int(np.prod(bucket.shape[:-1])), bucket.shape[-1]
    nh = rel_bias.shape[1]
    tb = pl.pallas_call(
        _bias_kernel,
        grid=(nh,),
        in_specs=[pl.BlockSpec(memory_space=pltpu.SMEM), _full((rows, cols))],
        out_specs=pl.BlockSpec((1, rows, cols), lambda h: (h, 0, 0)),
        out_shape=jax.ShapeDtypeStruct((nh, rows, cols), F32),
        compiler_params=_cparams(("arbitrary",)),
        name="t5_bias",
    )(shifted, jnp.asarray(bucket.reshape(rows, cols)))
    return tb.reshape((nh,) + bucket.shape)


def _cmp_kernel(r_ref, pos_ref, w1_ref, w2_ref, kn_ref, o_ref):
    m = r_ref.shape[3]
    half = r_ref.shape[4]
    r = r_ref[0, 0, 0].astype(BF16)
    a = _dot(r, w1_ref[0, :half, :])
    bm = _dot(r, w1_ref[0, half:, :])
    c = _dot(pos_ref[0].astype(BF16), w1_ref[0])
    hid = a + pltpu.roll(bm, m - 1, 0) + c[0:1, :]
    out = _dot(_silu(hid).astype(BF16), w2_ref[0])
    normed = out * lax.rsqrt(jnp.mean(out * out, axis=-1, keepdims=True) + EPS) * kn_ref[...]
    o_ref[0, 0, 0] = jnp.where(pl.program_id(0) == 0, normed, out).astype(o_ref.dtype)


def nsa_compress(kcvc, pos, w1, w2, kn):
    b, t, _ = kcvc.shape
    m = t // CMP_STRIDE
    half = CMP_STRIDE * NSA_DH
    r = kcvc.reshape(b, m, CMP_STRIDE, 2, NSA_KV_HEADS, NSA_DH).transpose(3, 0, 4, 1, 2, 5).reshape(2, b, 2, m, half)
    posf = jnp.zeros((2, 8, 2 * half), F32).at[:, 0].set(pos.reshape(2, 2 * half))
    w2d = jnp.concatenate([w2, w2], axis=-1).astype(BF16)
    knd = jnp.tile(kn, 2).reshape(1, LANES)
    return pl.pallas_call(
        _cmp_kernel,
        grid=(2, b, NSA_KV_HEADS),
        in_specs=[pl.BlockSpec((1, 1, 1, m, half), lambda s, i, k: (s, i, k, 0, 0)),
                  pl.BlockSpec((1, 8, 2 * half), lambda s, i, k: (s, 0, 0)),
                  pl.BlockSpec((1, 2 * half, CMP_HIDDEN), lambda s, i, k: (s, 0, 0)),
                  pl.BlockSpec((1, CMP_HIDDEN, LANES), lambda s, i, k: (s, 0, 0)),
                  _full((1, LANES))],
        out_specs=pl.BlockSpec((1, 1, 1, m, LANES), lambda s, i, k: (s, i, k, 0, 0)),
        out_shape=jax.ShapeDtypeStruct((2, b, NSA_KV_HEADS, m, LANES), BF16),
        compiler_params=_cparams(("arbitrary", "arbitrary", "arbitrary")),
        name="nsa_compress",
    )(r, posf, w1.astype(BF16), w2d, knd)


def _dot_split(a, b):
    hi = a.astype(BF16)
    lo = (a - hi.astype(F32)).astype(BF16)
    return _dot(hi, b) + _dot(lo, b)


def _head_q(q_ref, hh, lo):
    blk = q_ref[0, :, (hh // 2) * LANES:(hh // 2 + 1) * LANES]
    keep = lo if hh % 2 == 0 else jnp.logical_not(lo)
    return jnp.where(keep, blk, jnp.zeros_like(blk))


def _pair_heads(o, lo):
    return jnp.concatenate([jnp.where(lo, o[0], o[1]), jnp.where(lo, o[2], o[3])], axis=1)


def _nsa_sel_kernel(q_ref, kc_ref, vc_ref, ov_ref, bt_ref, o_ref, sel_ref, *, tq, nband, n_slc):
    i = pl.program_id(2)
    ncp = kc_ref.shape[3]
    nsp = ov_ref.shape[1]
    per = tq // CMP_STRIDE
    var = jnp.minimum(i, 1)
    bs = pl.multiple_of(per * jnp.maximum(i - 1, 0), per)
    lo = lax.broadcasted_iota(jnp.int32, (1, LANES), 1) < NSA_DH
    kc = kc_ref[0, 0, 0]
    vc = vc_ref[0, 0, 0]
    kcb = kc_ref[0, 0, 0, pl.ds(bs, nband), :]
    vcb = vc_ref[0, 0, 0, pl.ds(bs, nband), :]
    far_ok = lax.broadcasted_iota(jnp.int32, (1, ncp), 1) < per * (i - 1)
    hs = range(NSA_GROUP)
    qh = [_head_q(q_ref, hh, lo) for hh in hs]
    s_far = [jnp.where(far_ok, _dot_nt(q, kc), NEG) for q in qh]
    s_band = [_dot_nt(qh[hh], kcb) + bt_ref[var, hh] for hh in hs]
    m = [jnp.maximum(jnp.max(a, axis=1, keepdims=True), jnp.max(b, axis=1, keepdims=True))
         for a, b in zip(s_far, s_band)]
    m = [jnp.where(x < 0.5 * NEG, 0.0, x) for x in m]
    p_far = [jnp.exp2(a - x) for a, x in zip(s_far, m)]
    p_band = [jnp.exp2(b - x) for b, x in zip(s_band, m)]
    l = [jnp.sum(a, axis=1, keepdims=True) + jnp.sum(b, axis=1, keepdims=True) for a, b in zip(p_far, p_band)]
    inv = [1.0 / jnp.where(x == 0.0, 1.0, x) for x in l]
    outs = [(_dot(a.astype(BF16), vc) + _dot(b.astype(BF16), vcb)) * x for a, b, x in zip(p_far, p_band, inv)]
    ps_far = p_far[0] * inv[0]
    ps_band = p_band[0] * inv[0]
    for hh in range(1, NSA_GROUP):
        ps_far = ps_far + p_far[hh] * inv[hh]
        ps_band = ps_band + p_band[hh] * inv[hh]
    o_ref[0] = _pair_heads(outs, lo).astype(o_ref.dtype)

    imp = _dot_split(ps_far, ov_ref[...]) + _dot_split(ps_band, ov_ref[pl.ds(bs, nband), :])
    blk = lax.broadcasted_iota(jnp.int32, (1, nsp), 1)
    blk_f = blk.astype(F32)
    qpos = i * tq + lax.broadcasted_iota(jnp.int32, (tq, 1), 0)
    cur = lax.shift_right_logical(qpos, int(math.log2(SLC_LEN)))
    forced = (blk == 0) | (blk == cur) | (blk == cur - 1)
    work = jnp.where(forced, -jnp.inf, jnp.where(blk <= cur, imp, NEG))
    work = jnp.where(blk < n_slc, work, -jnp.inf)
    ngrp = 4
    rg = tq // ngrp
    works = [work[r * rg:(r + 1) * rg] for r in range(ngrp)]
    sels = [jnp.where(forced[r * rg:(r + 1) * rg], 1.0, jnp.zeros((rg, nsp), F32)) for r in range(ngrp)]
    for _ in range(max(min(SLC_TOPK, n_slc) - 3, 0)):
        ms = [jnp.max(w, axis=1, keepdims=True) for w in works]
        firsts = [jnp.min(jnp.where(w == m, blk_f, float(nsp)), axis=1, keepdims=True) for w, m in zip(works, ms)]
        picks = [blk_f == f for f in firsts]
        sels = [jnp.where(p, 1.0, s) for p, s in zip(picks, sels)]
        works = [jnp.where(p, -jnp.inf, w) for p, w in zip(picks, works)]
    sel = jnp.concatenate(sels, axis=0)
    for sup in range(sel_ref.shape[2]):
        col = sel[:, (sup // 2) * LANES:(sup // 2 + 1) * LANES]
        if sup % 2 == 0:
            col = pltpu.roll(col, SUP_BLOCKS, 1)
        sel_ref[0, 0, sup] = jnp.where(lo, 0.0, jnp.where(col > 0.5, 0.0, NEG)).astype(sel_ref.dtype)


def nsa_select(q, cmp_kv, rel_bias, tq=512):
    b, t, _ = q.shape
    ncp = t // CMP_STRIDE
    n_cmp = ncp - 1
    n_slc = t // SLC_LEN
    nsp = max(LANES, n_slc)
    nsup = max(1, n_slc // SUP_BLOCKS)
    per = tq // CMP_STRIDE
    nband = 2 * per
    n = np.arange(ncp)[:, None]
    s = np.arange(nsp)[None, :]
    ov = ((CMP_STRIDE * n < SLC_LEN * s + SLC_LEN) & (CMP_STRIDE * n + CMP_LEN > SLC_LEN * s)
          & (n < n_cmp) & (s < n_slc)).astype(np.float32)
    qi = np.arange(tq)[:, None]
    nj = np.arange(nband)[None, :]
    end = CMP_STRIDE * nj + CMP_LEN - 1
    dist = np.stack([qi - end, tq + qi - end])
    bt = _bias_table(rel_bias, dist, dist >= 0)
    bt = bt.reshape(NSA_KV_HEADS, NSA_GROUP, 2, tq, nband).transpose(0, 2, 1, 3, 4)
    bt = bt.reshape(NSA_KV_HEADS * 2, NSA_GROUP, tq, nband)
    gw = NSA_GROUP * NSA_DH
    return pl.pallas_call(
        functools.partial(_nsa_sel_kernel, tq=tq, nband=nband, n_slc=n_slc),
        grid=(b, NSA_KV_HEADS, t // tq),
        in_specs=[pl.BlockSpec((1, tq, gw), lambda bi, k, i: (bi, i, k)),
                  pl.BlockSpec((1, 1, 1, ncp, LANES), lambda bi, k, i: (0, bi, k, 0, 0)),
                  pl.BlockSpec((1, 1, 1, ncp, LANES), lambda bi, k, i: (1, bi, k, 0, 0)),
                  _full((ncp, nsp)),
                  pl.BlockSpec((2, NSA_GROUP, tq, nband), lambda bi, k, i: (k, 0, 0, 0))],
        out_specs=[pl.BlockSpec((1, tq, gw), lambda bi, k, i: (bi, i, k)),
                   pl.BlockSpec((1, 1, nsup, tq, LANES), lambda bi, k, i: (bi, k, 0, i, 0))],
        out_shape=[jax.ShapeDtypeStruct((b, t, NSA_W), BF16),
                   jax.ShapeDtypeStruct((b, NSA_KV_HEADS, nsup, t, LANES), BF16)],
        compiler_params=_cparams(("arbitrary", "arbitrary", "arbitrary")),
        name="nsa_select",
    )(q, cmp_kv, cmp_kv, jnp.asarray(ov, dtype=BF16), bt)


def _nsa_main_kernel(q_ref, ks_ref, vs_ref, kw0_ref, kw1_ref, kw2_ref, vw0_ref, vw1_ref, vw2_ref, sel_ref, oh_ref,
                     ocmp_ref, gate_ref, tb_ref, wm_ref, eg_ref, o_ref, *, tq, wide):
    i = pl.program_id(2)
    g = NSA_GROUP
    lo = lax.broadcasted_iota(jnp.int32, (1, LANES), 1) < NSA_DH

    def head_low(hh):
        blk = q_ref[0, :, (hh // 2) * LANES:(hh // 2 + 1) * LANES]
        if hh % 2 == 1:
            blk = pltpu.roll(blk.astype(F32), NSA_DH, 1).astype(BF16)
        return jnp.where(lo, blk, jnp.zeros_like(blk))

    qst = jnp.concatenate([head_low(hh) for hh in range(g)], axis=0)
    causal = (lax.broadcasted_iota(jnp.int32, (tq, tq), 1) <= lax.broadcasted_iota(jnp.int32, (tq, tq), 0))
    one = jnp.ones((1, LANES), BF16)
    sup_keys = SUP_BLOCKS * SLC_LEN
    all_masked = jnp.where(lo, 0.0, NEG).astype(BF16)

    def sel_step(jt, carry, near, tw):
        m, acc = carry
        key0 = jnp.maximum(jt, 0) * tw
        start = pl.multiple_of(key0, tw)
        within = pl.multiple_of(key0 & (sup_keys - 1), tw)
        mq = sel_ref[0, 0, lax.shift_right_logical(key0, int(math.log2(sup_keys)))]
        mq = jnp.where(jt >= 0, mq, all_masked)
        q_ext = qst + jnp.concatenate([mq] * g, axis=0)
        kt = jnp.where(lo, ks_ref[0, pl.ds(start, tw), :], oh_ref[pl.ds(within, tw), :])
        vt = jnp.where(lo, vs_ref[0, pl.ds(start, tw), :], one)
        s = _dot_nt(q_ext, kt)
        if near is not None:
            s = s.reshape(g, tq, tw) + tb_ref[:, :, near * tq:(near + 1) * tq]
            if near == 2:
                s = jnp.where(causal[None], s, NEG)
            s = s.reshape(g * tq, tw)
        m_new = jnp.maximum(m, jnp.max(s, axis=1, keepdims=True))
        p = jnp.exp2(s - m_new)
        acc = jnp.exp2(m - m_new) * acc + _dot(p.astype(BF16), vt)
        return m_new, acc

    n_far = jnp.maximum(i - 2, 0)
    n_wide = n_far // wide
    carry = (jnp.full((g * tq, 1), NEG, F32), jnp.zeros((g * tq, LANES), F32))
    carry = lax.fori_loop(0, n_wide, lambda j, c: sel_step(j, c, None, wide * tq), carry)
    done = n_wide * wide
    part = wide // 2
    while part >= 1:
        carry = lax.cond((n_far & part) != 0, lambda c, d=done, w=part: sel_step(d // w, c, None, w * tq),
                         lambda c: c, carry)
        done = done + (n_far & part)
        part //= 2
    for near in range(3):
        carry = sel_step(i - 2 + near, carry, near, tq)
    o_slc = (carry[1] / pltpu.roll(carry[1], NSA_DH, 1)).reshape(g, tq, LANES)

    var = jnp.minimum(i, 2)
    kws = (kw0_ref, kw1_ref, kw2_ref)
    vws = (vw0_ref, vw1_ref, vw2_ref)
    sw = []
    for near in range(3):
        s = _dot_nt(qst, kws[near][0]).reshape(g, tq, tq)
        s = s + tb_ref[:, :, near * tq:(near + 1) * tq] + wm_ref[var, :, near * tq:(near + 1) * tq][None]
        sw.append(s.reshape(g * tq, tq))
    m = jnp.maximum(jnp.maximum(jnp.max(sw[0], axis=1, keepdims=True), jnp.max(sw[1], axis=1, keepdims=True)),
                    jnp.max(sw[2], axis=1, keepdims=True))
    acc = jnp.zeros((g * tq, LANES), F32)
    for near in range(3):
        p = jnp.exp2(sw[near] - m)
        acc = acc + _dot(p.astype(BF16), jnp.where(lo, vws[near][0], one))
    o_win = (acc / pltpu.roll(acc, NSA_DH, 1)).reshape(g, tq, LANES)

    pair = lambda o: jnp.concatenate([jnp.where(lo, o[0], pltpu.roll(o[1], NSA_DH, 1)),
                                      jnp.where(lo, o[2], pltpu.roll(o[3], NSA_DH, 1))], axis=1)
    gates = _dot_hi(_sigmoid(gate_ref[0]), eg_ref[0])
    gw = g * NSA_DH
    out = (gates[:, 0:gw] * ocmp_ref[0].astype(F32)
           + gates[:, gw:2 * gw] * pair(o_slc) + gates[:, 2 * gw:3 * gw] * pair(o_win))
    o_ref[0] = out.astype(o_ref.dtype)


def nsa_main(q, ksw, vsw, sel, o_cmp, small, rel_bias, tq=256, wide=8):
    b, t, _ = q.shape
    nsup = sel.shape[2]
    g = NSA_GROUP
    gw = g * NSA_DH
    sup_keys = SUP_BLOCKS * SLC_LEN
    oh = np.zeros((sup_keys, LANES), np.float32)
    oh[np.arange(sup_keys), NSA_DH + np.arange(sup_keys) // SLC_LEN] = 1.0
    qi = np.arange(tq)[:, None]
    c = np.arange(3 * tq)[None, :]
    dist = qi + 2 * tq - c
    tb = _bias_table(rel_bias, dist, np.ones_like(dist, bool))
    wm = np.zeros((3, tq, 3 * tq), np.float32)
    for var in range(3):
        exists = c >= tq * (2 - var)
        wm[var] = np.where((dist >= 0) & (dist < WINDOW) & exists, 0.0, NEG)
    eg = np.zeros((NSA_KV_HEADS, LANES, 3 * gw), np.float32)
    for k in range(NSA_KV_HEADS):
        for hh in range(g):
            for br in range(3):
                eg[k, (k * g + hh) * 3 + br, br * gw + hh * NSA_DH:br * gw + (hh + 1) * NSA_DH] = 1.0
    near = lambda off, col: pl.BlockSpec(
        (1, tq, LANES), lambda bi, k, i: (bi, jnp.maximum(i - off, 0), col + k))
    return pl.pallas_call(
        functools.partial(_nsa_main_kernel, tq=tq, wide=wide),
        grid=(b, NSA_KV_HEADS, t // tq),
        in_specs=[pl.BlockSpec((1, tq, gw), lambda bi, k, i: (bi, i, k)),
                  pl.BlockSpec((1, t, LANES), lambda bi, k, i: (bi, 0, k)),
                  pl.BlockSpec((1, t, LANES), lambda bi, k, i: (bi, 0, k)),
                  near(2, 2), near(1, 2), near(0, 2), near(2, 2), near(1, 2), near(0, 2),
                  pl.BlockSpec((1, 1, nsup, tq, LANES), lambda bi, k, i: (bi, k, 0, i, 0)),
                  _full((sup_keys, LANES)),
                  pl.BlockSpec((1, tq, gw), lambda bi, k, i: (bi, i, k)),
                  pl.BlockSpec((1, tq, LANES), lambda bi, k, i: (bi, i, 0)),
                  pl.BlockSpec((g, tq, 3 * tq), lambda bi, k, i: (k, 0, 0)),
                  _full((3, tq, 3 * tq)),
                  pl.BlockSpec((1, LANES, 3 * gw), lambda bi, k, i: (k, 0, 0))],
        out_specs=pl.BlockSpec((1, tq, gw), lambda bi, k, i: (bi, i, k)),
        out_shape=jax.ShapeDtypeStruct((b, t, NSA_W), BF16),
        compiler_params=_cparams(("arbitrary", "arbitrary", "arbitrary")),
        name="nsa_main",
    )(q, ksw, vsw, ksw, ksw, ksw, vsw, vsw, vsw, sel, jnp.asarray(oh, dtype=BF16), o_cmp, small, tb,
      jnp.asarray(wm), jnp.asarray(eg))


def _gla_kernel(qk_ref, v_ref, r_ref, sm_ref, wg_ref, bg_ref, on_ref, tril_ref, o_ref, s_scr, *, tc):
    c = GLA_CHUNK

    @pl.when(pl.program_id(1) == 0)
    def _():
        s_scr[...] = jnp.zeros_like(s_scr)

    kw = GLA_KW
    log_a = _log_sigmoid(_dot_hi(sm_ref[0], wg_ref[...]) + bg_ref[...]) * (1.0 / GLA_TAU)
    gcum = _dot_hi(tril_ref[...], log_a)
    q = qk_ref[0, :, 0:kw] * (GLA_DK ** -0.5)
    k = qk_ref[0, :, kw:2 * kw]
    q_dec = (q * jnp.exp(gcum)).astype(BF16)
    k_inv = (k * jnp.exp(-gcum)).astype(BF16)
    ri = lax.broadcasted_iota(jnp.int32, (c, c), 0)
    ci = lax.broadcasted_iota(jnp.int32, (c, c), 1)
    causal = ci <= ri
    lo = lax.broadcasted_iota(jnp.int32, (1, LANES), 1) < GLA_DK
    nchunk = tc // c
    heads = range(GLA_HEADS)
    zero = jnp.zeros((c, LANES), BF16)

    qm, vb, kd, egl = {}, {}, {}, {}
    for n in range(nchunk):
        sl = slice(n * c, (n + 1) * c)
        gl = gcum[n * c + c - 1:n * c + c, :]
        kdn = (k[sl] * jnp.exp(gl - gcum[sl])).astype(BF16)
        for h in heads:
            pr = slice((h // 2) * LANES, (h // 2 + 1) * LANES)
            keep = lo if h % 2 == 0 else jnp.logical_not(lo)
            qm[n, h] = jnp.where(keep, q_dec[sl, pr], zero)
            vb[n, h] = v_ref[0, sl, h * GLA_DV:(h + 1) * GLA_DV].astype(BF16)
            kd[n, h] = kdn[:, pr]
            egl[n, h] = jnp.exp(gl[:, pr])
    idx = [(n, h) for n in range(nchunk) for h in heads]
    attn = {i: jnp.where(causal, _dot_nt(qm[i], k_inv[i[0] * c:(i[0] + 1) * c, (i[1] // 2) * LANES:
                                                       (i[1] // 2 + 1) * LANES]), 0.0).astype(BF16) for i in idx}
    o_intra = {i: _dot(attn[i], vb[i]) for i in idx}
    kv = {i: _dot_tn(vb[i], kd[i]) for i in idx}

    st = [s_scr[h] for h in heads]
    for n in range(nchunk):
        sl = slice(n * c, (n + 1) * c)
        o = [o_intra[n, h] + _dot_nt(qm[n, h], st[h].astype(BF16)) for h in heads]
        st = [st[h] * egl[n, h] + kv[n, h] for h in heads]
        for h in heads:
            on = o[h] * lax.rsqrt(jnp.mean(o[h] * o[h], axis=-1, keepdims=True) + EPS) * on_ref[...]
            o_ref[0, sl, h * GLA_DV:(h + 1) * GLA_DV] = (
                on * _silu(r_ref[0, sl, h * GLA_DV:(h + 1) * GLA_DV])).astype(o_ref.dtype)
    for h in heads:
        s_scr[h] = st[h]


def gated_linear_attention(qkvr, small, wg_up, bg, on_gain, tc=512):
    b, t, _ = qkvr.shape
    wg = jnp.zeros((LANES, GLA_KW), F32).at[3 * NSA_HEADS:3 * NSA_HEADS + GLA_GATE_RANK].set(wg_up)
    idx = np.arange(tc)
    tril = ((idx[:, None] >= idx[None, :]) & (idx[:, None] // GLA_CHUNK == idx[None, :] // GLA_CHUNK))
    return pl.pallas_call(
        functools.partial(_gla_kernel, tc=tc),
        grid=(b, t // tc),
        in_specs=[pl.BlockSpec((1, tc, 2 * GLA_KW), lambda i, j: (i, j, 0)),
                  pl.BlockSpec((1, tc, GLA_W), lambda i, j: (i, j, 1)),
                  pl.BlockSpec((1, tc, GLA_W), lambda i, j: (i, j, 2)),
                  pl.BlockSpec((1, tc, LANES), lambda i, j: (i, j, 0)),
                  _full((LANES, GLA_KW)), _full((1, GLA_KW)), _full((1, GLA_DV)), _full((tc, tc))],
        out_specs=pl.BlockSpec((1, tc, GLA_W), lambda i, j: (i, j, 0)),
        out_shape=jax.ShapeDtypeStruct((b, t, GLA_W), BF16),
        scratch_shapes=[pltpu.VMEM((GLA_HEADS, GLA_DV, LANES), F32)],
        compiler_params=_cparams(("arbitrary", "arbitrary")),
        name="gla",
    )(qkvr, qkvr, qkvr, small, wg, bg.reshape(1, GLA_KW), on_gain.reshape(1, GLA_DV),
      jnp.asarray(tril.astype(np.float32)))


def odd_mixer(h, w_in, nsa_qn, nsa_kn, nsa_pos, nsa_cmp_w1, nsa_cmp_w2, gla_wg_up, gla_bg, gla_on, rel_bias):
    cuts = np.cumsum((0,) + OD_SIZES)
    col = lambda i: w_in[:, cuts[i]:cuts[i + 1]]
    dup = lambda a: jnp.concatenate([a[:, :NSA_DH], a[:, :NSA_DH], a[:, NSA_DH:], a[:, NSA_DH:]], axis=1)
    ep = _head_norm_epilogue(NSA_DH)
    bd = _block_diag_ones(NSA_W, NSA_DH)
    qg = (jnp.tile(nsa_qn, NSA_HEADS) * (NSA_DH ** -0.5 * LOG2E)).reshape(1, NSA_W)
    kg = jnp.tile(nsa_kn, NSA_HEADS).reshape(1, NSA_W)
    w_small = jnp.zeros((w_in.shape[0], LANES), F32)
    w_small = w_small.at[:, 0:24].set(col(7)).at[:, 24:40].set(col(11))
    nq, kcvc, ksw, vsw, small, qkvr = proj_multi(
        h, [(col(0), BF16, ep, (bd, qg)),
            (jnp.concatenate([col(1), col(2)], axis=1), F32, None, ()),
            (jnp.concatenate([dup(col(3)), dup(col(5))], axis=1), BF16, ep, (bd, kg)),
            (jnp.concatenate([dup(col(4)), dup(col(6))], axis=1), BF16, None, ()),
            (w_small, F32, None, ()),
            (jnp.concatenate([col(8), col(9), col(10), col(12)], axis=1), F32, None, ())], name="proj_odd")
    cmp_kv = nsa_compress(kcvc, nsa_pos, nsa_cmp_w1, nsa_cmp_w2, nsa_kn)
    o_cmp, sel = nsa_select(nq, cmp_kv, rel_bias)
    o_nsa = nsa_main(nq, ksw, vsw, sel, o_cmp, small, rel_bias)
    o_gla = gated_linear_attention(qkvr, small, gla_wg_up, gla_bg, gla_on)
    return o_nsa, o_gla


MOE_TM = 256
MOE_ROWS = 512


def _first_index(mask_val, idx, big, axis):
    return jnp.min(jnp.where(mask_val, idx, big), axis=axis, keepdims=True)


def _route_kernel(h_ref, rt_ref, b_ref, up_ref, eid_ref, rank_ref, w_ref, cnt_ref, run):
    tm = h_ref.shape[0]
    ne = N_EXPERTS
    gsz = ne // N_GROUPS

    @pl.when(pl.program_id(0) == 0)
    def _():
        run[...] = jnp.zeros_like(run)

    scores = _sigmoid(_dot_nt(rt_ref[...], h_ref[...], HI))
    biased = scores + b_ref[...]
    b3 = biased.reshape(N_GROUPS, gsz, tm)
    i3 = lax.broadcasted_iota(jnp.int32, (1, gsz, 1), 1).astype(F32)
    m1 = jnp.max(b3, axis=1, keepdims=True)
    f1 = _first_index(b3 == m1, i3, float(gsz), 1)
    m2 = jnp.max(jnp.where(i3 == f1, -jnp.inf, b3), axis=1, keepdims=True)
    gs = (m1 + m2).reshape(N_GROUPS, tm)
    gidx = lax.broadcasted_iota(jnp.int32, (N_GROUPS, 1), 0).astype(F32)
    gmask = jnp.zeros((N_GROUPS, tm), F32)
    for _ in range(TOPK_GROUPS):
        m = jnp.max(gs, axis=0, keepdims=True)
        pick = gidx == _first_index(gs == m, gidx, float(N_GROUPS), 0)
        gmask = jnp.where(pick, 1.0, gmask)
        gs = jnp.where(pick, -jnp.inf, gs)
    emask = jnp.broadcast_to(gmask.reshape(N_GROUPS, 1, tm), (N_GROUPS, gsz, tm)).reshape(ne, tm)
    work = jnp.where(emask > 0.5, biased, -jnp.inf)
    eidx = lax.broadcasted_iota(jnp.int32, (ne, 1), 0).astype(F32)
    picks, eids, ws = [], [], []
    for _ in range(TOP_K):
        m = jnp.max(work, axis=0, keepdims=True)
        first = _first_index(work == m, eidx, float(ne), 0)
        pick = eidx == first
        picks.append(pick)
        eids.append(first)
        ws.append(jnp.sum(jnp.where(pick, scores, 0.0), axis=0, keepdims=True))
        work = jnp.where(pick, -jnp.inf, work)
    wsum = ws[0]
    for k in range(1, TOP_K):
        wsum = wsum + ws[k]
    chosen = jnp.zeros((ne, tm), F32)
    for pick in picks:
        chosen = jnp.where(pick, 1.0, chosen)
    pos = run[...] + _dot(chosen.astype(BF16), up_ref[...])
    run[...] = run[...] + jnp.sum(chosen, axis=1, keepdims=True)
    cnt_ref[...] = run[...]
    row = lax.broadcasted_iota(jnp.int32, (8, 1), 0)
    eid_o = jnp.zeros((8, tm), F32)
    rank_o = jnp.zeros((8, tm), F32)
    w_o = jnp.zeros((LANES, tm), F32)
    rowl = lax.broadcasted_iota(jnp.int32, (LANES, 1), 0)
    for k in range(TOP_K):
        rk = jnp.sum(jnp.where(picks[k], pos, 0.0), axis=0, keepdims=True)
        eid_o = jnp.where(row == k, eids[k], eid_o)
        rank_o = jnp.where(row == k, rk, rank_o)
        w_o = jnp.where(rowl == k, ws[k] / wsum * ROUTE_SCALE, w_o)
    eid_ref[0] = eid_o.astype(jnp.int32)
    rank_ref[0] = rank_o.astype(jnp.int32)
    w_ref[...] = w_o.T


def moe_route(h2, router, e_bias, tm=MOE_TM):
    nt, d = h2.shape
    ne = N_EXPERTS
    up = jnp.asarray(np.triu(np.ones((tm, tm), np.float32), 1), dtype=BF16)
    nb = nt // tm
    return pl.pallas_call(
        _route_kernel,
        grid=(nb,),
        in_specs=[pl.BlockSpec((tm, d), lambda i: (i, 0)), _full((ne, d)), _full((ne, 1)), _full((tm, tm))],
        out_specs=[pl.BlockSpec((1, 8, tm), lambda i: (i, 0, 0)),
                   pl.BlockSpec((1, 8, tm), lambda i: (i, 0, 0)),
                   pl.BlockSpec((tm, LANES), lambda i: (i, 0)),
                   _full((ne, 1))],
        out_shape=[jax.ShapeDtypeStruct((nb, 8, tm), jnp.int32), jax.ShapeDtypeStruct((nb, 8, tm), jnp.int32),
                   jax.ShapeDtypeStruct((nt, LANES), F32), jax.ShapeDtypeStruct((ne, 1), F32)],
        scratch_shapes=[pltpu.VMEM((ne, 1), F32)],
        compiler_params=_cparams(("arbitrary",)),
        name="moe_route",
    )(h2, router.T, e_bias.reshape(ne, 1), up)


def _dispatch_kernel(dest_ref, h_ref, xs_ref, sem):
    tm = h_ref.shape[0]

    def copy(t, row):
        return pltpu.make_async_copy(h_ref.at[pl.ds(t, 1), :], xs_ref.at[pl.ds(row, 1), :], sem)

    def issue(t, _):
        for k in range(TOP_K):
            copy(t, dest_ref[0, t, k]).start(priority=k % 2)
        return 0

    def drain(t, _):
        for k in range(TOP_K):
            copy(0, 0).wait()
        return 0

    lax.fori_loop(0, tm, issue, 0, unroll=4)
    lax.fori_loop(0, tm, drain, 0, unroll=4)


def moe_dispatch(h2, dest, tm=MOE_TM):
    nt, d = h2.shape
    return pl.pallas_call(
        _dispatch_kernel,
        grid=(nt // tm,),
        in_specs=[pl.BlockSpec((1, tm, 8), lambda i: (i, 0, 0), memory_space=pltpu.SMEM),
                  pl.BlockSpec((tm, d), lambda i: (i, 0))],
        out_specs=pl.BlockSpec(memory_space=pl.ANY),
        scratch_shapes=[pltpu.SemaphoreType.DMA(())],
        out_shape=jax.ShapeDtypeStruct((nt * TOP_K, d), F32),
        compiler_params=_cparams(("arbitrary",)),
        name="moe_dispatch",
    )(dest, h2)


def _ffn_kernel(blk_ref, exp_ref, lo_ref, hi_ref, first_ref, valid_ref, x_ref, wg_ref, wu_ref, wd_ref, o_ref,
                wg_b, wu_b, wd_b):
    i = pl.program_id(0)
    rows = x_ref.shape[0]

    @pl.when((i == 0) | (exp_ref[i] != exp_ref[jnp.maximum(i - 1, 0)]))
    def _():
        wg_b[...] = wg_ref[0].astype(BF16)
        wu_b[...] = wu_ref[0].astype(BF16)
        wd_b[...] = wd_ref[0].astype(BF16)

    @pl.when(valid_ref[i] == 1)
    def _():
        x = x_ref[...].astype(BF16)
        a = _dot(x, wg_b[...])
        u = _dot(x, wu_b[...])
        y = _dot((_silu(a) * u).astype(BF16), wd_b[...])
        r = blk_ref[i] * rows + lax.broadcasted_iota(jnp.int32, (rows, 1), 0)
        y = jnp.where((r >= lo_ref[i]) & (r < hi_ref[i]), y, 0.0)

        @pl.when(first_ref[i] == 1)
        def _():
            o_ref[...] = y

        @pl.when(first_ref[i] == 0)
        def _():
            o_ref[...] = o_ref[...] + y


def _items_kernel(cnt_ref, starts_ref, blk_ref, exp_ref, lo_ref, hi_ref, first_ref, valid_ref, *, rows, n_items):
    shift = int(math.log2(rows))

    def expert(e, carry):
        start, n = carry
        c = cnt_ref[e]
        starts_ref[e] = start
        end = start + c
        first_blk = lax.shift_right_logical(start, shift)
        n_blk = jnp.where(c > 0, lax.shift_right_logical(jnp.maximum(end - 1, 0), shift) - first_blk + 1, 0)

        def item(k, n):
            b = first_blk + k
            lo = jnp.maximum(start, b * rows)
            blk_ref[n] = b
            exp_ref[n] = e
            lo_ref[n] = lo
            hi_ref[n] = jnp.minimum(end, (b + 1) * rows)
            first_ref[n] = (lo == b * rows).astype(jnp.int32)
            valid_ref[n] = 1
            return n + 1

        return end, lax.fori_loop(0, n_blk, item, n)

    _, total = lax.fori_loop(0, N_EXPERTS, expert, (jnp.int32(0), jnp.int32(0)))
    last = jnp.maximum(total - 1, 0)

    def fill(k, _):
        blk_ref[k] = blk_ref[last]
        exp_ref[k] = exp_ref[last]
        lo_ref[k] = 0
        hi_ref[k] = 0
        first_ref[k] = 0
        valid_ref[k] = 0
        return 0

    lax.fori_loop(total, n_items, fill, 0)


def _ffn_items(counts, n_rows, rows):
    n_items = n_rows // rows + N_EXPERTS - 1
    smem = pl.BlockSpec(memory_space=pltpu.SMEM)
    out = pl.pallas_call(
        functools.partial(_items_kernel, rows=rows, n_items=n_items),
        in_specs=[smem],
        out_specs=[smem] * 7,
        out_shape=[jax.ShapeDtypeStruct((N_EXPERTS,), jnp.int32)]
                  + [jax.ShapeDtypeStruct((n_items,), jnp.int32)] * 6,
        name="moe_items",
    )(counts)
    return out[0], tuple(out[1:])


def moe_ffn_sorted(xs, items, wg, wu, wd, layer, rows=MOE_ROWS):
    n_rows, d = xs.shape
    n_items = items[0].shape[0]
    de = wg.shape[-1]
    return pl.pallas_call(
        _ffn_kernel,
        grid_spec=pltpu.PrefetchScalarGridSpec(
            num_scalar_prefetch=6,
            grid=(n_items,),
            in_specs=[pl.BlockSpec((rows, d), lambda i, blk, e, *_: (blk[i], 0)),
                      pl.BlockSpec((None, 1, d, de), lambda i, blk, e, *_: (layer, e[i], 0, 0)),
                      pl.BlockSpec((None, 1, d, de), lambda i, blk, e, *_: (layer, e[i], 0, 0)),
                      pl.BlockSpec((None, 1, de, d), lambda i, blk, e, *_: (layer, e[i], 0, 0))],
            out_specs=pl.BlockSpec((rows, d), lambda i, blk, e, *_: (blk[i], 0)),
            scratch_shapes=[pltpu.VMEM((d, de), BF16), pltpu.VMEM((d, de), BF16), pltpu.VMEM((de, d), BF16)]),
        out_shape=jax.ShapeDtypeStruct((n_rows, d), F32),
        compiler_params=_cparams(("arbitrary",)),
        name="moe_ffn",
    )(*items, xs, wg, wu, wd)


def _combine_kernel(dest_ref, ys_ref, w_ref, h_ref, x_ref, g_ref, sg_ref, su_ref, sd_ref, o_ref, buf, sem):
    tm = h_ref.shape[0]

    def copy(t, k, row):
        return pltpu.make_async_copy(ys_ref.at[pl.ds(row, 1), :], buf.at[k, pl.ds(t, 1), :], sem)

    def issue(t, _):
        for k in range(TOP_K):
            copy(t, k, dest_ref[0, t, k]).start(priority=k % 2)
        return 0

    def drain(t, _):
        for k in range(TOP_K):
            copy(0, 0, 0).wait()
        return 0

    lax.fori_loop(0, tm, issue, 0, unroll=4)
    hb = h_ref[...].astype(BF16)
    y = _dot((_silu(_dot(hb, sg_ref[...])) * _dot(hb, su_ref[...])).astype(BF16), sd_ref[...])
    lax.fori_loop(0, tm, drain, 0, unroll=4)
    w = w_ref[...]
    for k in range(TOP_K):
        y = y + w[:, k:k + 1] * buf[k]
    o_ref[...] = x_ref[...] + g_ref[0] * y


def moe_combine(ys, dest, w, h2, x2, gate, sg, su, sd, seq, tm=MOE_TM):
    nt, d = h2.shape
    ds_ = sg.shape[-1]
    per_b = seq // tm
    tile = lambda: pl.BlockSpec((tm, d), lambda i: (i, 0))
    return pl.pallas_call(
        _combine_kernel,
        grid=(nt // tm,),
        in_specs=[pl.BlockSpec((1, tm, 8), lambda i: (i, 0, 0), memory_space=pltpu.SMEM),
                  pl.BlockSpec(memory_space=pl.ANY),
                  pl.BlockSpec((tm, LANES), lambda i: (i, 0)), tile(), tile(),
                  pl.BlockSpec((1, 1, d), lambda i: (i // per_b, 0, 0)),
                  _full((d, ds_)), _full((d, ds_)), _full((ds_, d))],
        out_specs=tile(),
        scratch_shapes=[pltpu.VMEM((TOP_K, tm, d), F32), pltpu.SemaphoreType.DMA(())],
        out_shape=jax.ShapeDtypeStruct((nt, d), F32),
        compiler_params=_cparams(("arbitrary",)),
        name="moe_combine",
    )(dest, ys, w, h2, x2, gate, sg.astype(BF16), su.astype(BF16), sd.astype(BF16))


def moe_layer(x, g_norm, sc, sh, gate, router, e_bias, wg, wu, wd, layer, sg, su, sd):
    b, t, d = x.shape
    nt = b * t
    h = ln_mod(x, g_norm, sc, sh, F32)
    h2 = h.reshape(nt, d)
    eid, rank, w, counts = moe_route(h2, router, e_bias)
    starts, items = _ffn_items(counts.reshape(-1).astype(jnp.int32), nt * TOP_K, MOE_ROWS)
    hit = eid[..., None] == jnp.arange(N_EXPERTS, dtype=jnp.int32)
    dest = jnp.sum(jnp.where(hit, starts.astype(jnp.int32), 0), axis=-1) + rank
    dest = jnp.swapaxes(dest, 1, 2)
    xs = moe_dispatch(h2, dest)
    ys = moe_ffn_sorted(xs, items, wg, wu, wd, layer)
    out = moe_combine(ys, dest, w, h2, x.reshape(nt, d), gate.reshape(b, 1, d), sg, su, sd, t)
    return out.reshape(b, t, d)


def kernel(x, c, ada_w, ada_b, norm_mix, norm_ffn, rel_bias, ev_w_in, ev_w_out, fox_fb, fox_qn, fox_kn, gdn_conv, gdn_a_log, gdn_dt_bias, gdn_on, od_w_in, od_w_out, nsa_qn, nsa_kn, nsa_pos, nsa_cmp_w1, nsa_cmp_w2, gla_wg_up, gla_bg, gla_on, moe_router, moe_bias, moe_wg, moe_wu, moe_wd, sh_wg, sh_wu, sh_wd):
    d = x.shape[-1]
    depth = ada_w.shape[0]
    mod = adaln(c, ada_w, ada_b)
    for layer in range(depth):
        sh1, sc1, g1, sh2, sc2, g2 = [mod[layer, :, i * d:(i + 1) * d] for i in range(6)]
        h = ln_mod(x, norm_mix[layer], sc1, sh1, BF16)
        j = layer // 2
        if layer % 2 == 0:
            y1, y2 = even_mixer(h, ev_w_in[j], fox_fb[j], fox_qn[j], fox_kn[j], gdn_conv[j], gdn_a_log[j],
                                gdn_dt_bias[j], gdn_on[j])
            w_out = ev_w_out[j]
        else:
            y1, y2 = odd_mixer(h, od_w_in[j], nsa_qn[j], nsa_kn[j], nsa_pos[j], nsa_cmp_w1[j], nsa_cmp_w2[j],
                               gla_wg_up[j], gla_bg[j], gla_on[j], rel_bias)
            w_out = od_w_out[j]
        x = out_proj(y1, y2, w_out, x, g1)
        x = moe_layer(x, norm_ffn[layer], sc2, sh2, g2, moe_router[layer], moe_bias[layer], moe_wg, moe_wu, moe_wd,
                      layer, sh_wg[layer], sh_wu[layer], sh_wd[layer])
    return x
```

```python
import functools
import math

import numpy as np
import jax
import jax.numpy as jnp
from jax import lax
from jax.experimental import pallas as pl
from jax.experimental.pallas import tpu as pltpu

F32 = jnp.float32
BF16 = jnp.bfloat16
HI = lax.Precision.HIGHEST

EPS = 1e-6
LOG2E = math.log2(math.e)
NEG = -1e30

FOX_HEADS, FOX_DH = 8, 64
GDN_HEADS, GDN_DH, GDN_CONV = 4, 128, 4
NSA_HEADS, NSA_KV_HEADS, NSA_DH = 8, 2, 64
NSA_GROUP = NSA_HEADS // NSA_KV_HEADS
CMP_LEN, CMP_STRIDE, CMP_HIDDEN = 32, 16, 256
SLC_LEN, SLC_TOPK, WINDOW = 64, 16, 512
GLA_HEADS, GLA_DK, GLA_DV, GLA_GATE_RANK, GLA_TAU, GLA_CHUNK = 4, 64, 128, 16, 16.0, 64
REL_BUCKETS, REL_MAX_DIST = 32, 128
N_EXPERTS, TOP_K, D_EXPERT, D_SHARED = 64, 6, 256, 256
N_GROUPS, TOPK_GROUPS, ROUTE_SCALE = 8, 4, 2.5

FOX_W = FOX_HEADS * FOX_DH
GDN_W = GDN_HEADS * GDN_DH
NSA_W = NSA_HEADS * NSA_DH
NSA_KV_W = NSA_KV_HEADS * NSA_DH
GLA_KW = GLA_HEADS * GLA_DK
GLA_W = GLA_HEADS * GLA_DV
EV_SIZES = (FOX_W, FOX_W, FOX_W, FOX_HEADS, 3 * GDN_W, GDN_HEADS, GDN_HEADS, GDN_W)
OD_SIZES = (NSA_W,) + (NSA_KV_W,) * 6 + (3 * NSA_HEADS, GLA_KW, GLA_KW, GLA_W, GLA_GATE_RANK, GLA_W)

LANES = 128
SUP_BLOCKS = 64
VMEM_LIMIT = 56 * 1024 * 1024


def _cparams(sem, flags=None):
    return pltpu.CompilerParams(dimension_semantics=sem, vmem_limit_bytes=VMEM_LIMIT, flags=flags)


def _full(shape):
    n = len(shape)
    return pl.BlockSpec(shape, lambda *_: (0,) * n)


def _dot(a, b):
    return jnp.dot(a, b, preferred_element_type=F32)


def _dot_hi(a, b):
    return jnp.dot(a, b, precision=HI, preferred_element_type=F32)


def _dot_nt(a, b, precision=None):
    return lax.dot_general(a, b, (((1,), (1,)), ((), ())), precision=precision, preferred_element_type=F32)


def _dot_tn(a, b, precision=None):
    return lax.dot_general(a, b, (((0,), (0,)), ((), ())), precision=precision, preferred_element_type=F32)


def _sigmoid(x):
    return 1.0 / (1.0 + jnp.exp(-x))


def _silu(x):
    return x * _sigmoid(x)


def _softplus(x):
    return jnp.maximum(x, 0.0) + jnp.log(1.0 + jnp.exp(-jnp.abs(x)))


def _log_sigmoid(x):
    return -_softplus(-x)


def _adaln_kernel(c_ref, w_ref, b_ref, o_ref):
    c = c_ref[...]
    o_ref[0] = _dot_hi(_silu(c), w_ref[0]) + b_ref[0]


def adaln(c, ada_w, ada_b):
    depth, d, n = ada_w.shape
    b = c.shape[0]
    cp = jnp.zeros((8, d), F32).at[:b].set(c)
    tn = 1536
    out = pl.pallas_call(
        _adaln_kernel,
        grid=(depth, n // tn),
        in_specs=[_full((8, d)),
                  pl.BlockSpec((1, d, tn), lambda l, j: (l, 0, j)),
                  pl.BlockSpec((1, 1, tn), lambda l, j: (l, 0, j))],
        out_specs=pl.BlockSpec((1, 8, tn), lambda l, j: (l, 0, j)),
        out_shape=jax.ShapeDtypeStruct((depth, 8, n), F32),
        compiler_params=_cparams(("arbitrary", "arbitrary")),
        name="adaln",
    )(cp, ada_w, ada_b.reshape(depth, 1, n))
    return out[:, :b]


def _ln_kernel(x_ref, g_ref, sc_ref, sh_ref, o_ref):
    x = x_ref[0]
    y = x * lax.rsqrt(jnp.mean(x * x, axis=-1, keepdims=True) + EPS) * g_ref[...]
    o_ref[0] = (y * (1.0 + sc_ref[0]) + sh_ref[0]).astype(o_ref.dtype)


def ln_mod(x, g, sc, sh, out_dtype, tm=512):
    b, t, d = x.shape
    return pl.pallas_call(
        _ln_kernel,
        grid=(b, t // tm),
        in_specs=[pl.BlockSpec((1, tm, d), lambda i, j: (i, j, 0)),
                  _full((1, d)),
                  pl.BlockSpec((1, 1, d), lambda i, j: (i, 0, 0)),
                  pl.BlockSpec((1, 1, d), lambda i, j: (i, 0, 0))],
        out_specs=pl.BlockSpec((1, tm, d), lambda i, j: (i, j, 0)),
        out_shape=jax.ShapeDtypeStruct((b, t, d), out_dtype),
        compiler_params=_cparams(("arbitrary", "arbitrary")),
        name="ln_mod",
    )(x, g.reshape(1, d), sc.reshape(b, 1, d), sh.reshape(b, 1, d))


def proj_multi(h, groups, tm=512, name="proj"):
    b, t, d = h.shape
    widths = [g[0].shape[1] for g in groups]
    starts = np.cumsum([0] + widths)
    w_cat = jnp.concatenate([g[0] for g in groups], axis=1).astype(BF16)
    extras = [e for g in groups for e in g[3]]
    n_ex = [len(g[3]) for g in groups]
    n_out = len(groups)

    def kern(h_ref, w_ref, *rest):
        ex_refs = rest[:len(extras)]
        o_refs = rest[len(extras):]
        y = _dot(h_ref[0], w_ref[...])
        pos = 0
        for gi, (_, out_dtype, epilogue, _) in enumerate(groups):
            yg = y[:, starts[gi]:starts[gi + 1]]
            if epilogue is not None:
                yg = epilogue(yg, *[e[...] for e in ex_refs[pos:pos + n_ex[gi]]])
            pos += n_ex[gi]
            o_refs[gi][0] = yg.astype(out_dtype)

    return pl.pallas_call(
        kern,
        grid=(b, t // tm),
        in_specs=[pl.BlockSpec((1, tm, d), lambda i, j: (i, j, 0)), _full((d, int(starts[-1])))]
                 + [_full(e.shape) for e in extras],
        out_specs=[pl.BlockSpec((1, tm, n), lambda i, j: (i, j, 0)) for n in widths],
        out_shape=[jax.ShapeDtypeStruct((b, t, n), g[1]) for n, g in zip(widths, groups)],
        compiler_params=_cparams(("arbitrary", "arbitrary")),
        name=name,
    )(h, w_cat, *extras)


def _head_norm_epilogue(dh):
    inv = 1.0 / dh

    def ep(y, bd, gain):
        ssq = _dot((y * y).astype(BF16), bd)
        return y * lax.rsqrt(ssq * inv + EPS) * gain

    return ep


def _block_diag_ones(n, dh):
    i = np.arange(n) // dh
    return jnp.asarray((i[:, None] == i[None, :]).astype(np.float32), dtype=BF16)


def _outproj_kernel(y1_ref, y2_ref, wa_ref, wb_ref, x_ref, g_ref, o_ref):
    y = _dot(y1_ref[0], wa_ref[...]) + _dot(y2_ref[0], wb_ref[...])
    o_ref[0] = x_ref[0] + g_ref[0] * y


def out_proj(y1, y2, w_out, x, gate, tm=512):
    b, t, d = x.shape
    n1, n2 = y1.shape[-1], y2.shape[-1]
    wa = w_out[:n1].astype(BF16)
    wb = w_out[n1:].astype(BF16)
    return pl.pallas_call(
        _outproj_kernel,
        grid=(b, t // tm),
        in_specs=[pl.BlockSpec((1, tm, n1), lambda i, j: (i, j, 0)),
                  pl.BlockSpec((1, tm, n2), lambda i, j: (i, j, 0)),
                  _full((n1, d)), _full((n2, d)),
                  pl.BlockSpec((1, tm, d), lambda i, j: (i, j, 0)),
                  pl.BlockSpec((1, 1, d), lambda i, j: (i, 0, 0))],
        out_specs=pl.BlockSpec((1, tm, d), lambda i, j: (i, j, 0)),
        out_shape=jax.ShapeDtypeStruct((b, t, d), F32),
        compiler_params=_cparams(("arbitrary", "arbitrary")),
        name="out_proj",
    )(y1, y2, wa, wb, x, gate.reshape(b, 1, d))


def _decay_kernel(s_ref, fb_ref, tril_ref, place_ref, o_ref, carry):
    @pl.when(pl.program_id(1) == 0)
    def _():
        carry[...] = jnp.zeros_like(carry)

    tm = s_ref.shape[1]
    lf = _log_sigmoid(s_ref[0] + fb_ref[...])
    cum = _dot_hi(tril_ref[...], lf) + carry[...]
    carry[...] = cum[tm - 1:tm, :]
    x = cum * LOG2E
    hi = x.astype(BF16)
    r1 = x - hi.astype(F32)
    mid = r1.astype(BF16)
    low = (r1 - mid.astype(F32)).astype(BF16)
    o_ref[0] = _dot(jnp.concatenate([hi, mid, low], axis=1), place_ref[...]).astype(o_ref.dtype)


def fox_decay(small, fox_fb, tm=512):
    b, t, _ = small.shape
    fb = jnp.zeros((1, LANES), F32).at[0, :FOX_HEADS].set(fox_fb)
    tril = jnp.asarray(np.tril(np.ones((tm, tm), np.float32)))
    place = np.zeros((3 * LANES, FOX_W), np.float32)
    for h in range(FOX_HEADS):
        for j in range(3):
            place[j * LANES + h, (h // 2) * LANES + (FOX_DH if h % 2 == 0 else 0) + j] = 1.0
    return pl.pallas_call(
        _decay_kernel,
        grid=(b, t // tm),
        in_specs=[pl.BlockSpec((1, tm, LANES), lambda i, j: (i, j, 0)), _full((1, LANES)), _full((tm, tm)),
                  _full((3 * LANES, FOX_W))],
        out_specs=pl.BlockSpec((1, tm, FOX_W), lambda i, j: (i, j, 0)),
        out_shape=jax.ShapeDtypeStruct((b, t, FOX_W), BF16),
        scratch_shapes=[pltpu.VMEM((1, LANES), F32)],
        compiler_params=_cparams(("arbitrary", "arbitrary")),
        name="fox_decay",
    )(small, fb, tril, jnp.asarray(place, dtype=BF16))


def _fox_kernel(q_ref, k_ref, v_ref, f_ref, o_ref, *, tq, wide):
    i = pl.program_id(2)
    lane = lax.broadcasted_iota(jnp.int32, (1, LANES), 1)
    lo = lane < FOX_DH
    coef = jnp.where((lane & (FOX_DH - 1)) < 3, -1.0, 0.0).astype(BF16)
    q = q_ref[0]
    qs = (jnp.where(lo, q, coef), jnp.where(lo, coef, q))
    causal = (lax.broadcasted_iota(jnp.int32, (tq, tq), 1) <= lax.broadcasted_iota(jnp.int32, (tq, tq), 0))
    one = jnp.ones((1, LANES), BF16)

    def step(j, carry, tw, diag=False):
        start = pl.multiple_of(j * tw, tw)
        kt = k_ref[0, pl.ds(start, tw), :]
        ft = f_ref[0, pl.ds(start, tw), :]
        vt = v_ref[0, pl.ds(start, tw), :]
        s_pair = (_dot_nt(qs[0], jnp.where(lo, kt, ft)), _dot_nt(qs[1], jnp.where(lo, ft, kt)))
        vs = (jnp.where(lo, vt, one), jnp.where(lo, one, vt))
        new = []
        for hh in range(2):
            m, acc = carry[hh]
            s = s_pair[hh]
            if diag:
                s = jnp.where(causal, s, NEG)
            m_new = jnp.maximum(m, jnp.max(s, axis=1, keepdims=True))
            p = jnp.exp2(s - m_new)
            acc = jnp.exp2(m - m_new) * acc + _dot(p.astype(BF16), vs[hh])
            new.append((m_new, acc))
        return tuple(new)

    carry = tuple((jnp.full((tq, 1), NEG, F32), jnp.zeros((tq, LANES), F32)) for _ in range(2))
    n_wide = i // wide
    carry = lax.fori_loop(0, n_wide, lambda j, c: step(j, c, wide * tq), carry)
    done = n_wide * wide
    part = wide // 2
    while part >= 1:
        carry = lax.cond((i & part) != 0, lambda c, d=done, w=part: step(d // w, c, w * tq), lambda c: c, carry)
        done = done + (i & part)
        part //= 2
    carry = step(i, carry, tq, diag=True)
    acc = jnp.where(lo, carry[0][1], carry[1][1])
    den = jnp.where(lo, carry[1][1], carry[0][1])
    o_ref[0] = (acc / pltpu.roll(den, FOX_DH, 1)).astype(o_ref.dtype)


def fox_attention(q, k, v, feat, tq=512, wide=4):
    b, t, w = q.shape
    npair = w // LANES
    nt = t // tq
    whole = lambda: pl.BlockSpec((1, t, LANES), lambda bi, p, i: (bi, 0, p))
    return pl.pallas_call(
        functools.partial(_fox_kernel, tq=tq, wide=wide),
        grid=(b, npair, nt),
        in_specs=[pl.BlockSpec((1, tq, LANES), lambda bi, p, i: (bi, i, p)), whole(), whole(), whole()],
        out_specs=pl.BlockSpec((1, tq, LANES), lambda bi, p, i: (bi, i, p)),
        out_shape=jax.ShapeDtypeStruct((b, t, w), BF16),
        compiler_params=_cparams(("arbitrary", "arbitrary", "arbitrary")),
        name="fox_attn",
    )(q, k, v, feat)


def _mm(a, b):
    return _dot(a.astype(BF16), b.astype(BF16))


def _mm3(a, b):
    ah = a.astype(BF16)
    bh = b.astype(BF16)
    al = (a - ah.astype(F32)).astype(BF16)
    bl = (b - bh.astype(F32)).astype(BF16)
    return _dot(jnp.concatenate([ah, ah, al], axis=1), jnp.concatenate([bh, bl, bh], axis=0))


def _tril_solve(a, rhs, ri, ci):
    n = a[0].shape[0]
    both = lambda f, x, y: [f(p, q) for p, q in zip(x, y)]
    eye = (ri == ci).astype(F32)
    same = lambda b: (lax.shift_right_logical(ri, int(math.log2(b)))
                      == lax.shift_right_logical(ci, int(math.log2(b))))
    base = 16
    d = [jnp.where(same(base), p, 0.0) for p in a]
    d2 = both(_mm, d, d)
    d4 = both(_mm, d2, d2)
    r1 = [eye - p + p2 - t for p, p2, t in zip(d, d2, both(_mm, d, d2))]
    d8 = both(_mm, d4, d4)
    r2 = [eye + p4 + p8 + t for p4, p8, t in zip(d4, d8, both(_mm, d4, d8))]
    t = both(_mm, r1, r2)
    b = base
    while b < n:
        join = same(2 * b) & jnp.logical_not(same(b))
        low = [jnp.where(join, p, 0.0) for p in a]
        t = [p - q for p, q in zip(t, both(_mm, both(_mm, t, low), t))]
        b *= 2
    return both(_mm3, t, rhs)


GDN_BLOCK = 128


def _gdn_kernel(x_ref, sm_ref, z_ref, cw_ref, ega_ref, egb_ref, alog_ref, dtb_ref, on_ref, tril_ref,
                o_ref, s_scr, prev_scr, *, tc):
    c = GDN_BLOCK
    w = GDN_W

    @pl.when(pl.program_id(1) == 0)
    def _():
        s_scr[...] = jnp.zeros_like(s_scr)
        prev_scr[...] = jnp.zeros_like(prev_scr)

    x = x_ref[0]
    prev = prev_scr[...]
    row8 = lax.broadcasted_iota(jnp.int32, (8, 1), 0)
    acc = x * cw_ref[GDN_CONV - 1:GDN_CONV, :]
    for s in range(1, GDN_CONV):
        rolled = pltpu.roll(x, s, 0)
        head = jnp.where(row8 < s, pltpu.roll(prev, s, 0), rolled[0:8])
        shifted = jnp.concatenate([head, rolled[8:]], axis=0)
        acc = acc + shifted * cw_ref[GDN_CONV - 1 - s:GDN_CONV - s, :]
    prev_scr[...] = x[tc - 8:tc]
    xc = _silu(acc)

    sm = sm_ref[0]
    g_raw = _dot_hi(sm, ega_ref[...])
    b_raw = _dot_hi(sm, egb_ref[...])
    g = -jnp.exp(alog_ref[...]) * _softplus(g_raw + dtb_ref[...])
    beta_all = _sigmoid(b_raw)
    gc_all = _dot_hi(tril_ref[...], g)

    ri = lax.broadcasted_iota(jnp.int32, (c, c), 0)
    ci = lax.broadcasted_iota(jnp.int32, (c, c), 1)
    causal = ci <= ri
    strict = ci < ri

    nblk = tc // c
    a_l, attn_l, rhs_l, qd_l, kd_l, egl_l = [], [], [], [], [], []
    for h in range(GDN_HEADS):
        ln = slice(h * GDN_DH, (h + 1) * GDN_DH)
        qh = xc[:, h * GDN_DH:(h + 1) * GDN_DH]
        kh = xc[:, w + h * GDN_DH:w + (h + 1) * GDN_DH]
        qh = qh * lax.rsqrt(jnp.sum(qh * qh, axis=-1, keepdims=True) + EPS) * (GDN_DH ** -0.5)
        kh = kh * lax.rsqrt(jnp.sum(kh * kh, axis=-1, keepdims=True) + EPS)
        vh = xc[:, 2 * w + h * GDN_DH:2 * w + (h + 1) * GDN_DH]
        gch = gc_all[:, ln]
        gct = gch.T
        egc = jnp.exp(gch)
        bh = beta_all[:, ln]
        for n in range(nblk):
            sl = slice(n * c, (n + 1) * c)
            q, k, v, gc, be = qh[sl], kh[sl], vh[sl], gch[sl], bh[sl]
            decay = jnp.exp(jnp.where(causal, gc - gct[:, sl], NEG))
            kb = k * be
            kk = _dot_nt(jnp.concatenate([kb, q], axis=0).astype(BF16), k.astype(BF16))
            a_l.append(jnp.where(strict, kk[:c] * decay, 0.0))
            attn_l.append(jnp.where(causal, kk[c:] * decay, 0.0))
            rhs_l.append(jnp.concatenate([v * be, kb * egc[sl]], axis=1))
            gl = gc[c - 1:c, :]
            qd_l.append(q * egc[sl])
            kd_l.append(k * jnp.exp(gl - gc))
            egl_l.append(jnp.exp(gl))
    uw_l = _tril_solve(a_l, rhs_l, ri, ci)

    states = [s_scr[h] for h in range(GDN_HEADS)]
    for n in range(nblk):
        sl = slice(n * c, (n + 1) * c)
        idx = [h * nblk + n for h in range(GDN_HEADS)]
        ws = [_mm(jnp.concatenate([uw_l[i][:, GDN_DH:], qd_l[i]], axis=0), states[h])
              for h, i in enumerate(idx)]
        v_new = [uw_l[i][:, :GDN_DH] - ws[h][:c] for h, i in enumerate(idx)]
        o = [ws[h][c:] + _mm(attn_l[i], v_new[h]) for h, i in enumerate(idx)]
        states = [states[h] * egl_l[i] + _dot_tn(kd_l[i].astype(BF16), v_new[h].astype(BF16))
                  for h, i in enumerate(idx)]
        for h in range(GDN_HEADS):
            ln = slice(h * GDN_DH, (h + 1) * GDN_DH)
            on = o[h] * lax.rsqrt(jnp.mean(o[h] * o[h], axis=-1, keepdims=True) + EPS) * on_ref[...]
            o_ref[0, sl, ln] = (on * _silu(z_ref[0, sl, ln])).astype(o_ref.dtype)
    for h in range(GDN_HEADS):
        s_scr[h] = states[h]


def gated_delta_net(x, small, z, conv_w, a_log, dt_bias, on_gain, tc=512):
    b, t, _ = x.shape
    w = GDN_W
    ega = np.zeros((LANES, w), np.float32)
    egb = np.zeros((LANES, w), np.float32)
    for h in range(GDN_HEADS):
        ega[FOX_HEADS + h, h * GDN_DH:(h + 1) * GDN_DH] = 1.0
        egb[FOX_HEADS + GDN_HEADS + h, h * GDN_DH:(h + 1) * GDN_DH] = 1.0
    alog = jnp.repeat(a_log, GDN_DH).reshape(1, w)
    dtb = jnp.repeat(dt_bias, GDN_DH).reshape(1, w)
    idx = np.arange(tc)
    tril = ((idx[:, None] >= idx[None, :]) & (idx[:, None] // GDN_BLOCK == idx[None, :] // GDN_BLOCK))
    row = lambda n: pl.BlockSpec((1, tc, n), lambda i, j: (i, j, 0))
    return pl.pallas_call(
        functools.partial(_gdn_kernel, tc=tc),
        grid=(b, t // tc),
        in_specs=[row(3 * w), row(LANES), row(w), _full((GDN_CONV, 3 * w)), _full((LANES, w)), _full((LANES, w)),
                  _full((1, w)), _full((1, w)), _full((1, GDN_DH)), _full((tc, tc))],
        out_specs=row(w),
        out_shape=jax.ShapeDtypeStruct((b, t, w), BF16),
        scratch_shapes=[pltpu.VMEM((GDN_HEADS, GDN_DH, GDN_DH), F32), pltpu.VMEM((8, 3 * w), F32)],
        compiler_params=_cparams(("arbitrary", "arbitrary")),
        name="gdn",
    )(x, small, z, conv_w, jnp.asarray(ega), jnp.asarray(egb), alog, dtb, on_gain.reshape(1, GDN_DH),
      jnp.asarray(tril.astype(np.float32)))


def even_mixer(h, w_in, fox_fb, fox_qn, fox_kn, gdn_conv, gdn_a_log, gdn_dt_bias, gdn_on):
    cuts = np.cumsum((0,) + EV_SIZES)
    col = lambda i: w_in[:, cuts[i]:cuts[i + 1]]
    bd = _block_diag_ones(FOX_W, FOX_DH)
    ep = _head_norm_epilogue(FOX_DH)
    qg = (jnp.tile(fox_qn, FOX_HEADS) * (FOX_DH ** -0.5 * LOG2E)).reshape(1, FOX_W)
    kg = jnp.tile(fox_kn, FOX_HEADS).reshape(1, FOX_W)
    w_small = jnp.zeros((w_in.shape[0], LANES), F32)
    w_small = w_small.at[:, 0:8].set(col(3)).at[:, 8:12].set(col(5)).at[:, 12:16].set(col(6))
    fq, fk, fv, small, gqkv, gz = proj_multi(
        h, [(col(0), BF16, ep, (bd, qg)), (col(1), BF16, ep, (bd, kg)), (col(2), BF16, None, ()),
            (w_small, F32, None, ()), (col(4), F32, None, ()), (col(7), F32, None, ())], name="proj_even")
    feat = fox_decay(small, fox_fb)
    o_fox = fox_attention(fq, fk, fv, feat)
    o_gdn = gated_delta_net(gqkv, small, gz, gdn_conv, gdn_a_log, gdn_dt_bias, gdn_on)
    return o_fox, o_gdn


def _t5_bucket_np(dist):
    n = np.maximum(dist, 0)
    exact = REL_BUCKETS // 2
    nf = np.maximum(n, 1).astype(np.float32)
    large = exact + (np.log(nf / np.float32(exact)) / np.float32(math.log(REL_MAX_DIST / exact))
                     * np.float32(REL_BUCKETS - exact)).astype(np.int32)
    large = np.minimum(large, REL_BUCKETS - 1)
    return np.where(n < exact, n, large)


def _bias_kernel(tbl_ref, bucket_ref, o_ref):
    h = pl.program_id(0)
    bucket = bucket_ref[...]
    acc = jnp.full(bucket.shape, NEG, F32)
    for b in range(REL_BUCKETS):
        acc = jnp.where(bucket == b, tbl_ref[b, h], acc)
    o_ref[0] = acc


def _bias_table(rel_bias, dist, valid):
    shifted = (rel_bias - rel_bias[REL_BUCKETS - 1:REL_BUCKETS]) * LOG2E
    bucket = np.where(valid, _t5_bucket_np(dist), -1).astype(np.int32)
    rows, cols = int(np.prod(bucket.shape[:-1])), bucket.shape[-1]
    nh = rel_bias.shape[1]
    tb = pl.pallas_call(
        _bias_kernel,
        grid=(nh,),
        in_specs=[pl.BlockSpec(memory_space=pltpu.SMEM), _full((rows, cols))],
        out_specs=pl.BlockSpec((1, rows, cols), lambda h: (h, 0, 0)),
        out_shape=jax.ShapeDtypeStruct((nh, rows, cols), F32),
        compiler_params=_cparams(("arbitrary",)),
        name="t5_bias",
    )(shifted, jnp.asarray(bucket.reshape(rows, cols)))
    return tb.reshape((nh,) + bucket.shape)


def _cmp_kernel(r_ref, pos_ref, w1_ref, w2_ref, kn_ref, o_ref):
    m = r_ref.shape[3]
    half = r_ref.shape[4]
    r = r_ref[0, 0, 0].astype(BF16)
    a = _dot(r, w1_ref[0, :half, :])
    bm = _dot(r, w1_ref[0, half:, :])
    c = _dot(pos_ref[0].astype(BF16), w1_ref[0])
    hid = a + pltpu.roll(bm, m - 1, 0) + c[0:1, :]
    out = _dot(_silu(hid).astype(BF16), w2_ref[0])
    normed = out * lax.rsqrt(jnp.mean(out * out, axis=-1, keepdims=True) + EPS) * kn_ref[...]
    o_ref[0, 0, 0] = jnp.where(pl.program_id(0) == 0, normed, out).astype(o_ref.dtype)


def nsa_compress(kcvc, pos, w1, w2, kn):
    b, t, _ = kcvc.shape
    m = t // CMP_STRIDE
    half = CMP_STRIDE * NSA_DH
    r = kcvc.reshape(b, m, CMP_STRIDE, 2, NSA_KV_HEADS, NSA_DH).transpose(3, 0, 4, 1, 2, 5).reshape(2, b, 2, m, half)
    posf = jnp.zeros((2, 8, 2 * half), F32).at[:, 0].set(pos.reshape(2, 2 * half))
    w2d = jnp.concatenate([w2, w2], axis=-1).astype(BF16)
    knd = jnp.tile(kn, 2).reshape(1, LANES)
    return pl.pallas_call(
        _cmp_kernel,
        grid=(2, b, NSA_KV_HEADS),
        in_specs=[pl.BlockSpec((1, 1, 1, m, half), lambda s, i, k: (s, i, k, 0, 0)),
                  pl.BlockSpec((1, 8, 2 * half), lambda s, i, k: (s, 0, 0)),
                  pl.BlockSpec((1, 2 * half, CMP_HIDDEN), lambda s, i, k: (s, 0, 0)),
                  pl.BlockSpec((1, CMP_HIDDEN, LANES), lambda s, i, k: (s, 0, 0)),
                  _full((1, LANES))],
        out_specs=pl.BlockSpec((1, 1, 1, m, LANES), lambda s, i, k: (s, i, k, 0, 0)),
        out_shape=jax.ShapeDtypeStruct((2, b, NSA_KV_HEADS, m, LANES), BF16),
        compiler_params=_cparams(("arbitrary", "arbitrary", "arbitrary")),
        name="nsa_compress",
    )(r, posf, w1.astype(BF16), w2d, knd)


def _dot_split(a, b):
    hi = a.astype(BF16)
    lo = (a - hi.astype(F32)).astype(BF16)
    return _dot(hi, b) + _dot(lo, b)


def _head_q(q_ref, hh, lo):
    blk = q_ref[0, :, (hh // 2) * LANES:(hh // 2 + 1) * LANES]
    keep = lo if hh % 2 == 0 else jnp.logical_not(lo)
    return jnp.where(keep, blk, jnp.zeros_like(blk))


def _pair_heads(o, lo):
    return jnp.concatenate([jnp.where(lo, o[0], o[1]), jnp.where(lo, o[2], o[3])], axis=1)


def _nsa_sel_kernel(q_ref, kc_ref, vc_ref, ov_ref, bt_ref, o_ref, sel_ref, *, tq, nband, n_slc):
    i = pl.program_id(2)
    ncp = kc_ref.shape[3]
    nsp = ov_ref.shape[1]
    per = tq // CMP_STRIDE
    var = jnp.minimum(i, 1)
    bs = pl.multiple_of(per * jnp.maximum(i - 1, 0), per)
    lo = lax.broadcasted_iota(jnp.int32, (1, LANES), 1) < NSA_DH
    kc = kc_ref[0, 0, 0]
    vc = vc_ref[0, 0, 0]
    kcb = kc_ref[0, 0, 0, pl.ds(bs, nband), :]
    vcb = vc_ref[0, 0, 0, pl.ds(bs, nband), :]
    far_ok = lax.broadcasted_iota(jnp.int32, (1, ncp), 1) < per * (i - 1)
    hs = range(NSA_GROUP)
    qh = [_head_q(q_ref, hh, lo) for hh in hs]
    s_far = [jnp.where(far_ok, _dot_nt(q, kc), NEG) for q in qh]
    s_band = [_dot_nt(qh[hh], kcb) + bt_ref[var, hh] for hh in hs]
    m = [jnp.maximum(jnp.max(a, axis=1, keepdims=True), jnp.max(b, axis=1, keepdims=True))
         for a, b in zip(s_far, s_band)]
    m = [jnp.where(x < 0.5 * NEG, 0.0, x) for x in m]
    p_far = [jnp.exp2(a - x) for a, x in zip(s_far, m)]
    p_band = [jnp.exp2(b - x) for b, x in zip(s_band, m)]
    l = [jnp.sum(a, axis=1, keepdims=True) + jnp.sum(b, axis=1, keepdims=True) for a, b in zip(p_far, p_band)]
    inv = [1.0 / jnp.where(x == 0.0, 1.0, x) for x in l]
    outs = [(_dot(a.astype(BF16), vc) + _dot(b.astype(BF16), vcb)) * x for a, b, x in zip(p_far, p_band, inv)]
    ps_far = p_far[0] * inv[0]
    ps_band = p_band[0] * inv[0]
    for hh in range(1, NSA_GROUP):
        ps_far = ps_far + p_far[hh] * inv[hh]
        ps_band = ps_band + p_band[hh] * inv[hh]
    o_ref[0] = _pair_heads(outs, lo).astype(o_ref.dtype)

    imp = _dot_split(ps_far, ov_ref[...]) + _dot_split(ps_band, ov_ref[pl.ds(bs, nband), :])
    blk = lax.broadcasted_iota(jnp.int32, (1, nsp), 1)
    blk_f = blk.astype(F32)
    qpos = i * tq + lax.broadcasted_iota(jnp.int32, (tq, 1), 0)
    cur = lax.shift_right_logical(qpos, int(math.log2(SLC_LEN)))
    forced = (blk == 0) | (blk == cur) | (blk == cur - 1)
    work = jnp.where(forced, -jnp.inf, jnp.where(blk <= cur, imp, NEG))
    work = jnp.where(blk < n_slc, work, -jnp.inf)
    ngrp = 4
    rg = tq // ngrp
    works = [work[r * rg:(r + 1) * rg] for r in range(ngrp)]
    sels = [jnp.where(forced[r * rg:(r + 1) * rg], 1.0, jnp.zeros((rg, nsp), F32)) for r in range(ngrp)]
    for _ in range(max(min(SLC_TOPK, n_slc) - 3, 0)):
        ms = [jnp.max(w, axis=1, keepdims=True) for w in works]
        firsts = [jnp.min(jnp.where(w == m, blk_f, float(nsp)), axis=1, keepdims=True) for w, m in zip(works, ms)]
        picks = [blk_f == f for f in firsts]
        sels = [jnp.where(p, 1.0, s) for p, s in zip(picks, sels)]
        works = [jnp.where(p, -jnp.inf, w) for p, w in zip(picks, works)]
    sel = jnp.concatenate(sels, axis=0)
    for sup in range(sel_ref.shape[2]):
        col = sel[:, (sup // 2) * LANES:(sup // 2 + 1) * LANES]
        if sup % 2 == 0:
            col = pltpu.roll(col, SUP_BLOCKS, 1)
        sel_ref[0, 0, sup] = jnp.where(lo, 0.0, jnp.where(col > 0.5, 0.0, NEG)).astype(sel_ref.dtype)


def nsa_select(q, cmp_kv, rel_bias, tq=512):
    b, t, _ = q.shape
    ncp = t // CMP_STRIDE
    n_cmp = ncp - 1
    n_slc = t // SLC_LEN
    nsp = max(LANES, n_slc)
    nsup = max(1, n_slc // SUP_BLOCKS)
    per = tq // CMP_STRIDE
    nband = 2 * per
    n = np.arange(ncp)[:, None]
    s = np.arange(nsp)[None, :]
    ov = ((CMP_STRIDE * n < SLC_LEN * s + SLC_LEN) & (CMP_STRIDE * n + CMP_LEN > SLC_LEN * s)
          & (n < n_cmp) & (s < n_slc)).astype(np.float32)
    qi = np.arange(tq)[:, None]
    nj = np.arange(nband)[None, :]
    end = CMP_STRIDE * nj + CMP_LEN - 1
    dist = np.stack([qi - end, tq + qi - end])
    bt = _bias_table(rel_bias, dist, dist >= 0)
    bt = bt.reshape(NSA_KV_HEADS, NSA_GROUP, 2, tq, nband).transpose(0, 2, 1, 3, 4)
    bt = bt.reshape(NSA_KV_HEADS * 2, NSA_GROUP, tq, nband)
    gw = NSA_GROUP * NSA_DH
    return pl.pallas_call(
        functools.partial(_nsa_sel_kernel, tq=tq, nband=nband, n_slc=n_slc),
        grid=(b, NSA_KV_HEADS, t // tq),
        in_specs=[pl.BlockSpec((1, tq, gw), lambda bi, k, i: (bi, i, k)),
                  pl.BlockSpec((1, 1, 1, ncp, LANES), lambda bi, k, i: (0, bi, k, 0, 0)),
                  pl.BlockSpec((1, 1, 1, ncp, LANES), lambda bi, k, i: (1, bi, k, 0, 0)),
                  _full((ncp, nsp)),
                  pl.BlockSpec((2, NSA_GROUP, tq, nband), lambda bi, k, i: (k, 0, 0, 0))],
        out_specs=[pl.BlockSpec((1, tq, gw), lambda bi, k, i: (bi, i, k)),
                   pl.BlockSpec((1, 1, nsup, tq, LANES), lambda bi, k, i: (bi, k, 0, i, 0))],
        out_shape=[jax.ShapeDtypeStruct((b, t, NSA_W), BF16),
                   jax.ShapeDtypeStruct((b, NSA_KV_HEADS, nsup, t, LANES), BF16)],
        compiler_params=_cparams(("arbitrary", "arbitrary", "arbitrary")),
        name="nsa_select",
    )(q, cmp_kv, cmp_kv, jnp.asarray(ov, dtype=BF16), bt)


def _nsa_main_kernel(q_ref, ks_ref, vs_ref, kw0_ref, kw1_ref, kw2_ref, vw0_ref, vw1_ref, vw2_ref, sel_ref, oh_ref,
                     ocmp_ref, gate_ref, tb_ref, wm_ref, eg_ref, o_ref, *, tq, wide):
    i = pl.program_id(2)
    g = NSA_GROUP
    lo = lax.broadcasted_iota(jnp.int32, (1, LANES), 1) < NSA_DH

    def head_low(hh):
        blk = q_ref[0, :, (hh // 2) * LANES:(hh // 2 + 1) * LANES]
        if hh % 2 == 1:
            blk = pltpu.roll(blk.astype(F32), NSA_DH, 1).astype(BF16)
        return jnp.where(lo, blk, jnp.zeros_like(blk))

    qst = jnp.concatenate([head_low(hh) for hh in range(g)], axis=0)
    causal = (lax.broadcasted_iota(jnp.int32, (tq, tq), 1) <= lax.broadcasted_iota(jnp.int32, (tq, tq), 0))
    one = jnp.ones((1, LANES), BF16)
    sup_keys = SUP_BLOCKS * SLC_LEN
    all_masked = jnp.where(lo, 0.0, NEG).astype(BF16)

    def sel_step(jt, carry, near, tw):
        m, acc = carry
        key0 = jnp.maximum(jt, 0) * tw
        start = pl.multiple_of(key0, tw)
        within = pl.multiple_of(key0 & (sup_keys - 1), tw)
        mq = sel_ref[0, 0, lax.shift_right_logical(key0, int(math.log2(sup_keys)))]
        mq = jnp.where(jt >= 0, mq, all_masked)
        q_ext = qst + jnp.concatenate([mq] * g, axis=0)
        kt = jnp.where(lo, ks_ref[0, pl.ds(start, tw), :], oh_ref[pl.ds(within, tw), :])
        vt = jnp.where(lo, vs_ref[0, pl.ds(start, tw), :], one)
        s = _dot_nt(q_ext, kt)
        if near is not None:
            s = s.reshape(g, tq, tw) + tb_ref[:, :, near * tq:(near + 1) * tq]
            if near == 2:
                s = jnp.where(causal[None], s, NEG)
            s = s.reshape(g * tq, tw)
        m_new = jnp.maximum(m, jnp.max(s, axis=1, keepdims=True))
        p = jnp.exp2(s - m_new)
        acc = jnp.exp2(m - m_new) * acc + _dot(p.astype(BF16), vt)
        return m_new, acc

    n_far = jnp.maximum(i - 2, 0)
    n_wide = n_far // wide
    carry = (jnp.full((g * tq, 1), NEG, F32), jnp.zeros((g * tq, LANES), F32))
    carry = lax.fori_loop(0, n_wide, lambda j, c: sel_step(j, c, None, wide * tq), carry)
    done = n_wide * wide
    part = wide // 2
    while part >= 1:
        carry = lax.cond((n_far & part) != 0, lambda c, d=done, w=part: sel_step(d // w, c, None, w * tq),
                         lambda c: c, carry)
        done = done + (n_far & part)
        part //= 2
    for near in range(3):
        carry = sel_step(i - 2 + near, carry, near, tq)
    o_slc = (carry[1] / pltpu.roll(carry[1], NSA_DH, 1)).reshape(g, tq, LANES)

    var = jnp.minimum(i, 2)
    kws = (kw0_ref, kw1_ref, kw2_ref)
    vws = (vw0_ref, vw1_ref, vw2_ref)
    sw = []
    for near in range(3):
        s = _dot_nt(qst, kws[near][0]).reshape(g, tq, tq)
        s = s + tb_ref[:, :, near * tq:(near + 1) * tq] + wm_ref[var, :, near * tq:(near + 1) * tq][None]
        sw.append(s.reshape(g * tq, tq))
    m = jnp.maximum(jnp.maximum(jnp.max(sw[0], axis=1, keepdims=True), jnp.max(sw[1], axis=1, keepdims=True)),
                    jnp.max(sw[2], axis=1, keepdims=True))
    acc = jnp.zeros((g * tq, LANES), F32)
    for near in range(3):
        p = jnp.exp2(sw[near] - m)
        acc = acc + _dot(p.astype(BF16), jnp.where(lo, vws[near][0], one))
    o_win = (acc / pltpu.roll(acc, NSA_DH, 1)).reshape(g, tq, LANES)

    pair = lambda o: jnp.concatenate([jnp.where(lo, o[0], pltpu.roll(o[1], NSA_DH, 1)),
                                      jnp.where(lo, o[2], pltpu.roll(o[3], NSA_DH, 1))], axis=1)
    gates = _dot_hi(_sigmoid(gate_ref[0]), eg_ref[0])
    gw = g * NSA_DH
    out = (gates[:, 0:gw] * ocmp_ref[0].astype(F32)
           + gates[:, gw:2 * gw] * pair(o_slc) + gates[:, 2 * gw:3 * gw] * pair(o_win))
    o_ref[0] = out.astype(o_ref.dtype)


def nsa_main(q, ksw, vsw, sel, o_cmp, small, rel_bias, tq=256, wide=8):
    b, t, _ = q.shape
    nsup = sel.shape[2]
    g = NSA_GROUP
    gw = g * NSA_DH
    sup_keys = SUP_BLOCKS * SLC_LEN
    oh = np.zeros((sup_keys, LANES), np.float32)
    oh[np.arange(sup_keys), NSA_DH + np.arange(sup_keys) // SLC_LEN] = 1.0
    qi = np.arange(tq)[:, None]
    c = np.arange(3 * tq)[None, :]
    dist = qi + 2 * tq - c
    tb = _bias_table(rel_bias, dist, np.ones_like(dist, bool))
    wm = np.zeros((3, tq, 3 * tq), np.float32)
    for var in range(3):
        exists = c >= tq * (2 - var)
        wm[var] = np.where((dist >= 0) & (dist < WINDOW) & exists, 0.0, NEG)
    eg = np.zeros((NSA_KV_HEADS, LANES, 3 * gw), np.float32)
    for k in range(NSA_KV_HEADS):
        for hh in range(g):
            for br in range(3):
                eg[k, (k * g + hh) * 3 + br, br * gw + hh * NSA_DH:br * gw + (hh + 1) * NSA_DH] = 1.0
    near = lambda off, col: pl.BlockSpec(
        (1, tq, LANES), lambda bi, k, i: (bi, jnp.maximum(i - off, 0), col + k))
    return pl.pallas_call(
        functools.partial(_nsa_main_kernel, tq=tq, wide=wide),
        grid=(b, NSA_KV_HEADS, t // tq),
        in_specs=[pl.BlockSpec((1, tq, gw), lambda bi, k, i: (bi, i, k)),
                  pl.BlockSpec((1, t, LANES), lambda bi, k, i: (bi, 0, k)),
                  pl.BlockSpec((1, t, LANES), lambda bi, k, i: (bi, 0, k)),
                  near(2, 2), near(1, 2), near(0, 2), near(2, 2), near(1, 2), near(0, 2),
                  pl.BlockSpec((1, 1, nsup, tq, LANES), lambda bi, k, i: (bi, k, 0, i, 0)),
                  _full((sup_keys, LANES)),
                  pl.BlockSpec((1, tq, gw), lambda bi, k, i: (bi, i, k)),
                  pl.BlockSpec((1, tq, LANES), lambda bi, k, i: (bi, i, 0)),
                  pl.BlockSpec((g, tq, 3 * tq), lambda bi, k, i: (k, 0, 0)),
                  _full((3, tq, 3 * tq)),
                  pl.BlockSpec((1, LANES, 3 * gw), lambda bi, k, i: (k, 0, 0))],
        out_specs=pl.BlockSpec((1, tq, gw), lambda bi, k, i: (bi, i, k)),
        out_shape=jax.ShapeDtypeStruct((b, t, NSA_W), BF16),
        compiler_params=_cparams(("arbitrary", "arbitrary", "arbitrary")),
        name="nsa_main",
    )(q, ksw, vsw, ksw, ksw, ksw, vsw, vsw, vsw, sel, jnp.asarray(oh, dtype=BF16), o_cmp, small, tb,
      jnp.asarray(wm), jnp.asarray(eg))


def _gla_kernel(qk_ref, v_ref, r_ref, sm_ref, wg_ref, bg_ref, on_ref, tril_ref, o_ref, s_scr, *, tc):
    c = GLA_CHUNK

    @pl.when(pl.program_id(1) == 0)
    def _():
        s_scr[...] = jnp.zeros_like(s_scr)

    kw = GLA_KW
    log_a = _log_sigmoid(_dot_hi(sm_ref[0], wg_ref[...]) + bg_ref[...]) * (1.0 / GLA_TAU)
    gcum = _dot_hi(tril_ref[...], log_a)
    q = qk_ref[0, :, 0:kw] * (GLA_DK ** -0.5)
    k = qk_ref[0, :, kw:2 * kw]
    q_dec = (q * jnp.exp(gcum)).astype(BF16)
    k_inv = (k * jnp.exp(-gcum)).astype(BF16)
    ri = lax.broadcasted_iota(jnp.int32, (c, c), 0)
    ci = lax.broadcasted_iota(jnp.int32, (c, c), 1)
    causal = ci <= ri
    lo = lax.broadcasted_iota(jnp.int32, (1, LANES), 1) < GLA_DK
    nchunk = tc // c
    heads = range(GLA_HEADS)
    zero = jnp.zeros((c, LANES), BF16)

    qm, vb, kd, egl = {}, {}, {}, {}
    for n in range(nchunk):
        sl = slice(n * c, (n + 1) * c)
        gl = gcum[n * c + c - 1:n * c + c, :]
        kdn = (k[sl] * jnp.exp(gl - gcum[sl])).astype(BF16)
        for h in heads:
            pr = slice((h // 2) * LANES, (h // 2 + 1) * LANES)
            keep = lo if h % 2 == 0 else jnp.logical_not(lo)
            qm[n, h] = jnp.where(keep, q_dec[sl, pr], zero)
            vb[n, h] = v_ref[0, sl, h * GLA_DV:(h + 1) * GLA_DV].astype(BF16)
            kd[n, h] = kdn[:, pr]
            egl[n, h] = jnp.exp(gl[:, pr])
    idx = [(n, h) for n in range(nchunk) for h in heads]
    attn = {i: jnp.where(causal, _dot_nt(qm[i], k_inv[i[0] * c:(i[0] + 1) * c, (i[1] // 2) * LANES:
                                                       (i[1] // 2 + 1) * LANES]), 0.0).astype(BF16) for i in idx}
    o_intra = {i: _dot(attn[i], vb[i]) for i in idx}
    kv = {i: _dot_tn(vb[i], kd[i]) for i in idx}

    st = [s_scr[h] for h in heads]
    for n in range(nchunk):
        sl = slice(n * c, (n + 1) * c)
        o = [o_intra[n, h] + _dot_nt(qm[n, h], st[h].astype(BF16)) for h in heads]
        st = [st[h] * egl[n, h] + kv[n, h] for h in heads]
        for h in heads:
            on = o[h] * lax.rsqrt(jnp.mean(o[h] * o[h], axis=-1, keepdims=True) + EPS) * on_ref[...]
            o_ref[0, sl, h * GLA_DV:(h + 1) * GLA_DV] = (
                on * _silu(r_ref[0, sl, h * GLA_DV:(h + 1) * GLA_DV])).astype(o_ref.dtype)
    for h in heads:
        s_scr[h] = st[h]


def gated_linear_attention(qkvr, small, wg_up, bg, on_gain, tc=512):
    b, t, _ = qkvr.shape
    wg = jnp.zeros((LANES, GLA_KW), F32).at[3 * NSA_HEADS:3 * NSA_HEADS + GLA_GATE_RANK].set(wg_up)
    idx = np.arange(tc)
    tril = ((idx[:, None] >= idx[None, :]) & (idx[:, None] // GLA_CHUNK == idx[None, :] // GLA_CHUNK))
    return pl.pallas_call(
        functools.partial(_gla_kernel, tc=tc),
        grid=(b, t // tc),
        in_specs=[pl.BlockSpec((1, tc, 2 * GLA_KW), lambda i, j: (i, j, 0)),
                  pl.BlockSpec((1, tc, GLA_W), lambda i, j: (i, j, 1)),
                  pl.BlockSpec((1, tc, GLA_W), lambda i, j: (i, j, 2)),
                  pl.BlockSpec((1, tc, LANES), lambda i, j: (i, j, 0)),
                  _full((LANES, GLA_KW)), _full((1, GLA_KW)), _full((1, GLA_DV)), _full((tc, tc))],
        out_specs=pl.BlockSpec((1, tc, GLA_W), lambda i, j: (i, j, 0)),
        out_shape=jax.ShapeDtypeStruct((b, t, GLA_W), BF16),
        scratch_shapes=[pltpu.VMEM((GLA_HEADS, GLA_DV, LANES), F32)],
        compiler_params=_cparams(("arbitrary", "arbitrary")),
        name="gla",
    )(qkvr, qkvr, qkvr, small, wg, bg.reshape(1, GLA_KW), on_gain.reshape(1, GLA_DV),
      jnp.asarray(tril.astype(np.float32)))


def odd_mixer(h, w_in, nsa_qn, nsa_kn, nsa_pos, nsa_cmp_w1, nsa_cmp_w2, gla_wg_up, gla_bg, gla_on, rel_bias):
    cuts = np.cumsum((0,) + OD_SIZES)
    col = lambda i: w_in[:, cuts[i]:cuts[i + 1]]
    dup = lambda a: jnp.concatenate([a[:, :NSA_DH], a[:, :NSA_DH], a[:, NSA_DH:], a[:, NSA_DH:]], axis=1)
    ep = _head_norm_epilogue(NSA_DH)
    bd = _block_diag_ones(NSA_W, NSA_DH)
    qg = (jnp.tile(nsa_qn, NSA_HEADS) * (NSA_DH ** -0.5 * LOG2E)).reshape(1, NSA_W)
    kg = jnp.tile(nsa_kn, NSA_HEADS).reshape(1, NSA_W)
    w_small = jnp.zeros((w_in.shape[0], LANES), F32)
    w_small = w_small.at[:, 0:24].set(col(7)).at[:, 24:40].set(col(11))
    nq, kcvc, ksw, vsw, small, qkvr = proj_multi(
        h, [(col(0), BF16, ep, (bd, qg)),
            (jnp.concatenate([col(1), col(2)], axis=1), F32, None, ()),
            (jnp.concatenate([dup(col(3)), dup(col(5))], axis=1), BF16, ep, (bd, kg)),
            (jnp.concatenate([dup(col(4)), dup(col(6))], axis=1), BF16, None, ()),
            (w_small, F32, None, ()),
            (jnp.concatenate([col(8), col(9), col(10), col(12)], axis=1), F32, None, ())], name="proj_odd")
    cmp_kv = nsa_compress(kcvc, nsa_pos, nsa_cmp_w1, nsa_cmp_w2, nsa_kn)
    o_cmp, sel = nsa_select(nq, cmp_kv, rel_bias)
    o_nsa = nsa_main(nq, ksw, vsw, sel, o_cmp, small, rel_bias)
    o_gla = gated_linear_attention(qkvr, small, gla_wg_up, gla_bg, gla_on)
    return o_nsa, o_gla


MOE_TM = 256
MOE_ROWS = 512


def _first_index(mask_val, idx, big, axis):
    return jnp.min(jnp.where(mask_val, idx, big), axis=axis, keepdims=True)


def _route_kernel(h_ref, rt_ref, b_ref, up_ref, eid_ref, rank_ref, w_ref, cnt_ref, run):
    tm = h_ref.shape[0]
    ne = N_EXPERTS
    gsz = ne // N_GROUPS

    @pl.when(pl.program_id(0) == 0)
    def _():
        run[...] = jnp.zeros_like(run)

    scores = _sigmoid(_dot_nt(rt_ref[...], h_ref[...], HI))
    biased = scores + b_ref[...]
    b3 = biased.reshape(N_GROUPS, gsz, tm)
    i3 = lax.broadcasted_iota(jnp.int32, (1, gsz, 1), 1).astype(F32)
    m1 = jnp.max(b3, axis=1, keepdims=True)
    f1 = _first_index(b3 == m1, i3, float(gsz), 1)
    m2 = jnp.max(jnp.where(i3 == f1, -jnp.inf, b3), axis=1, keepdims=True)
    gs = (m1 + m2).reshape(N_GROUPS, tm)
    gidx = lax.broadcasted_iota(jnp.int32, (N_GROUPS, 1), 0).astype(F32)
    gmask = jnp.zeros((N_GROUPS, tm), F32)
    for _ in range(TOPK_GROUPS):
        m = jnp.max(gs, axis=0, keepdims=True)
        pick = gidx == _first_index(gs == m, gidx, float(N_GROUPS), 0)
        gmask = jnp.where(pick, 1.0, gmask)
        gs = jnp.where(pick, -jnp.inf, gs)
    emask = jnp.broadcast_to(gmask.reshape(N_GROUPS, 1, tm), (N_GROUPS, gsz, tm)).reshape(ne, tm)
    work = jnp.where(emask > 0.5, biased, -jnp.inf)
    eidx = lax.broadcasted_iota(jnp.int32, (ne, 1), 0).astype(F32)
    picks, eids, ws = [], [], []
    for _ in range(TOP_K):
        m = jnp.max(work, axis=0, keepdims=True)
        first = _first_index(work == m, eidx, float(ne), 0)
        pick = eidx == first
        picks.append(pick)
        eids.append(first)
        ws.append(jnp.sum(jnp.where(pick, scores, 0.0), axis=0, keepdims=True))
        work = jnp.where(pick, -jnp.inf, work)
    wsum = ws[0]
    for k in range(1, TOP_K):
        wsum = wsum + ws[k]
    chosen = jnp.zeros((ne, tm), F32)
    for pick in picks:
        chosen = jnp.where(pick, 1.0, chosen)
    pos = run[...] + _dot(chosen.astype(BF16), up_ref[...])
    run[...] = run[...] + jnp.sum(chosen, axis=1, keepdims=True)
    cnt_ref[...] = run[...]
    row = lax.broadcasted_iota(jnp.int32, (8, 1), 0)
    eid_o = jnp.zeros((8, tm), F32)
    rank_o = jnp.zeros((8, tm), F32)
    w_o = jnp.zeros((LANES, tm), F32)
    rowl = lax.broadcasted_iota(jnp.int32, (LANES, 1), 0)
    for k in range(TOP_K):
        rk = jnp.sum(jnp.where(picks[k], pos, 0.0), axis=0, keepdims=True)
        eid_o = jnp.where(row == k, eids[k], eid_o)
        rank_o = jnp.where(row == k, rk, rank_o)
        w_o = jnp.where(rowl == k, ws[k] / wsum * ROUTE_SCALE, w_o)
    eid_ref[0] = eid_o.astype(jnp.int32)
    rank_ref[0] = rank_o.astype(jnp.int32)
    w_ref[...] = w_o.T


def moe_route(h2, router, e_bias, tm=MOE_TM):
    nt, d = h2.shape
    ne = N_EXPERTS
    up = jnp.asarray(np.triu(np.ones((tm, tm), np.float32), 1), dtype=BF16)
    nb = nt // tm
    return pl.pallas_call(
        _route_kernel,
        grid=(nb,),
        in_specs=[pl.BlockSpec((tm, d), lambda i: (i, 0)), _full((ne, d)), _full((ne, 1)), _full((tm, tm))],
        out_specs=[pl.BlockSpec((1, 8, tm), lambda i: (i, 0, 0)),
                   pl.BlockSpec((1, 8, tm), lambda i: (i, 0, 0)),
                   pl.BlockSpec((tm, LANES), lambda i: (i, 0)),
                   _full((ne, 1))],
        out_shape=[jax.ShapeDtypeStruct((nb, 8, tm), jnp.int32), jax.ShapeDtypeStruct((nb, 8, tm), jnp.int32),
                   jax.ShapeDtypeStruct((nt, LANES), F32), jax.ShapeDtypeStruct((ne, 1), F32)],
        scratch_shapes=[pltpu.VMEM((ne, 1), F32)],
        compiler_params=_cparams(("arbitrary",)),
        name="moe_route",
    )(h2, router.T, e_bias.reshape(ne, 1), up)


def _pack_bf16_pairs(x):
    n = x.shape[1] // 2
    bits = pltpu.bitcast(x, jnp.uint32)
    rounded = bits + jnp.uint32(0x7FFF) + (lax.shift_right_logical(bits, jnp.uint32(16)) & jnp.uint32(1))
    return lax.shift_right_logical(rounded[:, :n], jnp.uint32(16)) | (rounded[:, n:] & jnp.uint32(0xFFFF0000))


def _unpack_bf16_pairs(p):
    lo = pltpu.bitcast(lax.shift_left(p, jnp.uint32(16)), F32)
    hi = pltpu.bitcast(p & jnp.uint32(0xFFFF0000), F32)
    return lo, hi


def _dispatch_kernel(dest_ref, h_ref, xs_ref, hp, sem):
    tm = h_ref.shape[0]
    hp[...] = _pack_bf16_pairs(h_ref[...])

    def copy(t, row):
        return pltpu.make_async_copy(hp.at[pl.ds(t, 1), :], xs_ref.at[pl.ds(row, 1), :], sem)

    def issue(t, _):
        for k in range(TOP_K):
            copy(t, dest_ref[0, t, k]).start(priority=k % 2)
        return 0

    def drain(t, _):
        for k in range(TOP_K):
            copy(0, 0).wait()
        return 0

    lax.fori_loop(0, tm, issue, 0, unroll=4)
    lax.fori_loop(0, tm, drain, 0, unroll=4)


def moe_dispatch(h2, dest, tm=MOE_TM):
    nt, d = h2.shape
    return pl.pallas_call(
        _dispatch_kernel,
        grid=(nt // tm,),
        in_specs=[pl.BlockSpec((1, tm, 8), lambda i: (i, 0, 0), memory_space=pltpu.SMEM),
                  pl.BlockSpec((tm, d), lambda i: (i, 0))],
        out_specs=pl.BlockSpec(memory_space=pl.ANY),
        scratch_shapes=[pltpu.VMEM((tm, d // 2), jnp.uint32), pltpu.SemaphoreType.DMA(())],
        out_shape=jax.ShapeDtypeStruct((nt * TOP_K, d // 2), jnp.uint32),
        compiler_params=_cparams(("arbitrary",)),
        name="moe_dispatch",
    )(dest, h2)


def _ffn_kernel(blk_ref, exp_ref, lo_ref, hi_ref, first_ref, valid_ref, x_ref, wg_ref, wu_ref, wd_ref, o_ref,
                wg_b, wu_b, wd_b):
    i = pl.program_id(0)
    rows = x_ref.shape[0]

    @pl.when((i == 0) | (exp_ref[i] != exp_ref[jnp.maximum(i - 1, 0)]))
    def _():
        wg_b[...] = wg_ref[0].astype(BF16)
        wu_b[...] = wu_ref[0].astype(BF16)
        wd_b[...] = wd_ref[0].astype(BF16)

    @pl.when(valid_ref[i] == 1)
    def _():
        x_lo, x_hi = _unpack_bf16_pairs(x_ref[...])
        x = jnp.concatenate([x_lo, x_hi], axis=1).astype(BF16)
        a = _dot(x, wg_b[...])
        u = _dot(x, wu_b[...])
        y = _dot((_silu(a) * u).astype(BF16), wd_b[...])
        r = blk_ref[i] * rows + lax.broadcasted_iota(jnp.int32, (rows, 1), 0)
        y = jnp.where((r >= lo_ref[i]) & (r < hi_ref[i]), y, 0.0)

        @pl.when(first_ref[i] == 1)
        def _():
            o_ref[...] = _pack_bf16_pairs(y)

        @pl.when(first_ref[i] == 0)
        def _():
            o_lo, o_hi = _unpack_bf16_pairs(o_ref[...])
            o_ref[...] = _pack_bf16_pairs(jnp.concatenate([o_lo, o_hi], axis=1) + y)


def _items_kernel(cnt_ref, starts_ref, blk_ref, exp_ref, lo_ref, hi_ref, first_ref, valid_ref, *, rows, n_items):
    shift = int(math.log2(rows))

    def expert(e, carry):
        start, n = carry
        c = cnt_ref[e]
        starts_ref[e] = start
        end = start + c
        first_blk = lax.shift_right_logical(start, shift)
        n_blk = jnp.where(c > 0, lax.shift_right_logical(jnp.maximum(end - 1, 0), shift) - first_blk + 1, 0)

        def item(k, n):
            b = first_blk + k
            lo = jnp.maximum(start, b * rows)
            blk_ref[n] = b
            exp_ref[n] = e
            lo_ref[n] = lo
            hi_ref[n] = jnp.minimum(end, (b + 1) * rows)
            first_ref[n] = (lo == b * rows).astype(jnp.int32)
            valid_ref[n] = 1
            return n + 1

        return end, lax.fori_loop(0, n_blk, item, n)

    _, total = lax.fori_loop(0, N_EXPERTS, expert, (jnp.int32(0), jnp.int32(0)))
    last = jnp.maximum(total - 1, 0)

    def fill(k, _):
        blk_ref[k] = blk_ref[last]
        exp_ref[k] = exp_ref[last]
        lo_ref[k] = 0
        hi_ref[k] = 0
        first_ref[k] = 0
        valid_ref[k] = 0
        return 0

    lax.fori_loop(total, n_items, fill, 0)


def _ffn_items(counts, n_rows, rows):
    n_items = n_rows // rows + N_EXPERTS - 1
    smem = pl.BlockSpec(memory_space=pltpu.SMEM)
    out = pl.pallas_call(
        functools.partial(_items_kernel, rows=rows, n_items=n_items),
        in_specs=[smem],
        out_specs=[smem] * 7,
        out_shape=[jax.ShapeDtypeStruct((N_EXPERTS,), jnp.int32)]
                  + [jax.ShapeDtypeStruct((n_items,), jnp.int32)] * 6,
        name="moe_items",
    )(counts)
    return out[0], tuple(out[1:])


def moe_ffn_sorted(xs, items, wg, wu, wd, layer, rows=MOE_ROWS):
    n_rows, dp = xs.shape
    d = 2 * dp
    n_items = items[0].shape[0]
    de = wg.shape[-1]
    return pl.pallas_call(
        _ffn_kernel,
        grid_spec=pltpu.PrefetchScalarGridSpec(
            num_scalar_prefetch=6,
            grid=(n_items,),
            in_specs=[pl.BlockSpec((rows, dp), lambda i, blk, e, *_: (blk[i], 0)),
                      pl.BlockSpec((None, 1, d, de), lambda i, blk, e, *_: (layer, e[i], 0, 0)),
                      pl.BlockSpec((None, 1, d, de), lambda i, blk, e, *_: (layer, e[i], 0, 0)),
                      pl.BlockSpec((None, 1, de, d), lambda i, blk, e, *_: (layer, e[i], 0, 0))],
            out_specs=pl.BlockSpec((rows, dp), lambda i, blk, e, *_: (blk[i], 0)),
            scratch_shapes=[pltpu.VMEM((d, de), BF16), pltpu.VMEM((d, de), BF16), pltpu.VMEM((de, d), BF16)]),
        out_shape=jax.ShapeDtypeStruct((n_rows, dp), jnp.uint32),
        compiler_params=_cparams(("arbitrary",)),
        name="moe_ffn",
    )(*items, xs, wg, wu, wd)


def _combine_kernel(dest_ref, ys_ref, w_ref, h_ref, x_ref, g_ref, sg_ref, su_ref, sd_ref, o_ref, buf, sem):
    tm = h_ref.shape[0]

    def copy(t, k, row):
        return pltpu.make_async_copy(ys_ref.at[pl.ds(row, 1), :], buf.at[k, pl.ds(t, 1), :], sem)

    def issue(t, _):
        for k in range(TOP_K):
            copy(t, k, dest_ref[0, t, k]).start(priority=k % 2)
        return 0

    def drain(t, _):
        for k in range(TOP_K):
            copy(0, 0, 0).wait()
        return 0

    lax.fori_loop(0, tm, issue, 0, unroll=4)
    hb = h_ref[...].astype(BF16)
    y = _dot((_silu(_dot(hb, sg_ref[...])) * _dot(hb, su_ref[...])).astype(BF16), sd_ref[...])
    lax.fori_loop(0, tm, drain, 0, unroll=4)
    w = w_ref[...]
    half = y.shape[1] // 2
    y_lo, y_hi = y[:, :half], y[:, half:]
    for k in range(TOP_K):
        e_lo, e_hi = _unpack_bf16_pairs(buf[k])
        y_lo = y_lo + w[:, k:k + 1] * e_lo
        y_hi = y_hi + w[:, k:k + 1] * e_hi
    o_ref[...] = x_ref[...] + g_ref[0] * jnp.concatenate([y_lo, y_hi], axis=1)


def moe_combine(ys, dest, w, h2, x2, gate, sg, su, sd, seq, tm=MOE_TM):
    nt, d = h2.shape
    ds_ = sg.shape[-1]
    per_b = seq // tm
    tile = lambda: pl.BlockSpec((tm, d), lambda i: (i, 0))
    return pl.pallas_call(
        _combine_kernel,
        grid=(nt // tm,),
        in_specs=[pl.BlockSpec((1, tm, 8), lambda i: (i, 0, 0), memory_space=pltpu.SMEM),
                  pl.BlockSpec(memory_space=pl.ANY),
                  pl.BlockSpec((tm, LANES), lambda i: (i, 0)), tile(), tile(),
                  pl.BlockSpec((1, 1, d), lambda i: (i // per_b, 0, 0)),
                  _full((d, ds_)), _full((d, ds_)), _full((ds_, d))],
        out_specs=tile(),
        scratch_shapes=[pltpu.VMEM((TOP_K, tm, d // 2), jnp.uint32), pltpu.SemaphoreType.DMA(())],
        out_shape=jax.ShapeDtypeStruct((nt, d), F32),
        compiler_params=_cparams(("arbitrary",)),
        name="moe_combine",
    )(dest, ys, w, h2, x2, gate, sg.astype(BF16), su.astype(BF16), sd.astype(BF16))


def moe_layer(x, g_norm, sc, sh, gate, router, e_bias, wg, wu, wd, layer, sg, su, sd):
    b, t, d = x.shape
    nt = b * t
    h = ln_mod(x, g_norm, sc, sh, F32)
    h2 = h.reshape(nt, d)
    eid, rank, w, counts = moe_route(h2, router, e_bias)
    starts, items = _ffn_items(counts.reshape(-1).astype(jnp.int32), nt * TOP_K, MOE_ROWS)
    hit = eid[..., None] == jnp.arange(N_EXPERTS, dtype=jnp.int32)
    dest = jnp.sum(jnp.where(hit, starts.astype(jnp.int32), 0), axis=-1) + rank
    dest = jnp.swapaxes(dest, 1, 2)
    xs = moe_dispatch(h2, dest)
    ys = moe_ffn_sorted(xs, items, wg, wu, wd, layer)
    out = moe_combine(ys, dest, w, h2, x.reshape(nt, d), gate.reshape(b, 1, d), sg, su, sd, t)
    return out.reshape(b, t, d)


def kernel(x, c, ada_w, ada_b, norm_mix, norm_ffn, rel_bias, ev_w_in, ev_w_out, fox_fb, fox_qn, fox_kn, gdn_conv, gdn_a_log, gdn_dt_bias, gdn_on, od_w_in, od_w_out, nsa_qn, nsa_kn, nsa_pos, nsa_cmp_w1, nsa_cmp_w2, gla_wg_up, gla_bg, gla_on, moe_router, moe_bias, moe_wg, moe_wu, moe_wd, sh_wg, sh_wu, sh_wd):
    d = x.shape[-1]
    depth = ada_w.shape[0]
    mod = adaln(c, ada_w, ada_b)
    for layer in range(depth):
        sh1, sc1, g1, sh2, sc2, g2 = [mod[layer, :, i * d:(i + 1) * d] for i in range(6)]
        h = ln_mod(x, norm_mix[layer], sc1, sh1, BF16)
        j = layer // 2
        if layer % 2 == 0:
            y1, y2 = even_mixer(h, ev_w_in[j], fox_fb[j], fox_qn[j], fox_kn[j], gdn_conv[j], gdn_a_log[j],
                                gdn_dt_bias[j], gdn_on[j])
            w_out = ev_w_out[j]
        else:
            y1, y2 = odd_mixer(h, od_w_in[j], nsa_qn[j], nsa_kn[j], nsa_pos[j], nsa_cmp_w1[j], nsa_cmp_w2[j],
                               gla_wg_up[j], gla_bg[j], gla_on[j], rel_bias)
            w_out = od_w_out[j]
        x = out_proj(y1, y2, w_out, x, g1)
        x = moe_layer(x, norm_ffn[layer], sc2, sh2, g2, moe_router[layer], moe_bias[layer], moe_wg, moe_wu, moe_wd,
                      layer, sh_wg[layer], sh_wu[layer], sh_wd[layer])
    return x
```

```python
import functools
import math

import numpy as np
import jax
import jax.numpy as jnp
from jax import lax
from jax.experimental import pallas as pl
from jax.experimental.pallas import tpu as pltpu

F32 = jnp.float32
BF16 = jnp.bfloat16
HI = lax.Precision.HIGHEST

EPS = 1e-6
LOG2E = math.log2(math.e)
NEG = -1e30

FOX_HEADS, FOX_DH = 8, 64
GDN_HEADS, GDN_DH, GDN_CONV = 4, 128, 4
NSA_HEADS, NSA_KV_HEADS, NSA_DH = 8, 2, 64
NSA_GROUP = NSA_HEADS // NSA_KV_HEADS
CMP_LEN, CMP_STRIDE, CMP_HIDDEN = 32, 16, 256
SLC_LEN, SLC_TOPK, WINDOW = 64, 16, 512
GLA_HEADS, GLA_DK, GLA_DV, GLA_GATE_RANK, GLA_TAU, GLA_CHUNK = 4, 64, 128, 16, 16.0, 64
REL_BUCKETS, REL_MAX_DIST = 32, 128
N_EXPERTS, TOP_K, D_EXPERT, D_SHARED = 64, 6, 256, 256
N_GROUPS, TOPK_GROUPS, ROUTE_SCALE = 8, 4, 2.5

FOX_W = FOX_HEADS * FOX_DH
GDN_W = GDN_HEADS * GDN_DH
NSA_W = NSA_HEADS * NSA_DH
NSA_KV_W = NSA_KV_HEADS * NSA_DH
GLA_KW = GLA_HEADS * GLA_DK
GLA_W = GLA_HEADS * GLA_DV
EV_SIZES = (FOX_W, FOX_W, FOX_W, FOX_HEADS, 3 * GDN_W, GDN_HEADS, GDN_HEADS, GDN_W)
OD_SIZES = (NSA_W,) + (NSA_KV_W,) * 6 + (3 * NSA_HEADS, GLA_KW, GLA_KW, GLA_W, GLA_GATE_RANK, GLA_W)

LANES = 128
SUP_BLOCKS = 64
VMEM_LIMIT = 56 * 1024 * 1024


def _cparams(sem, flags=None):
    return pltpu.CompilerParams(dimension_semantics=sem, vmem_limit_bytes=VMEM_LIMIT, flags=flags)


def _full(shape):
    n = len(shape)
    return pl.BlockSpec(shape, lambda *_: (0,) * n)


def _dot(a, b):
    return jnp.dot(a, b, preferred_element_type=F32)


def _dot_hi(a, b):
    return jnp.dot(a, b, precision=HI, preferred_element_type=F32)


def _dot_nt(a, b, precision=None):
    return lax.dot_general(a, b, (((1,), (1,)), ((), ())), precision=precision, preferred_element_type=F32)


def _dot_tn(a, b, precision=None):
    return lax.dot_general(a, b, (((0,), (0,)), ((), ())), precision=precision, preferred_element_type=F32)


def _sigmoid(x):
    return 1.0 / (1.0 + jnp.exp(-x))


def _silu(x):
    return x * _sigmoid(x)


def _softplus(x):
    return jnp.maximum(x, 0.0) + jnp.log(1.0 + jnp.exp(-jnp.abs(x)))


def _log_sigmoid(x):
    return -_softplus(-x)


def _adaln_kernel(c_ref, w_ref, b_ref, o_ref):
    c = c_ref[...]
    o_ref[0] = _dot_hi(_silu(c), w_ref[0]) + b_ref[0]


def adaln(c, ada_w, ada_b):
    depth, d, n = ada_w.shape
    b = c.shape[0]
    cp = jnp.zeros((8, d), F32).at[:b].set(c)
    tn = 1536
    out = pl.pallas_call(
        _adaln_kernel,
        grid=(depth, n // tn),
        in_specs=[_full((8, d)),
                  pl.BlockSpec((1, d, tn), lambda l, j: (l, 0, j)),
                  pl.BlockSpec((1, 1, tn), lambda l, j: (l, 0, j))],
        out_specs=pl.BlockSpec((1, 8, tn), lambda l, j: (l, 0, j)),
        out_shape=jax.ShapeDtypeStruct((depth, 8, n), F32),
        compiler_params=_cparams(("arbitrary", "arbitrary")),
        name="adaln",
    )(cp, ada_w, ada_b.reshape(depth, 1, n))
    return out[:, :b]


def _ln_kernel(x_ref, g_ref, sc_ref, sh_ref, o_ref):
    x = x_ref[0]
    y = x * lax.rsqrt(jnp.mean(x * x, axis=-1, keepdims=True) + EPS) * g_ref[...]
    o_ref[0] = (y * (1.0 + sc_ref[0]) + sh_ref[0]).astype(o_ref.dtype)


def ln_mod(x, g, sc, sh, out_dtype, tm=512):
    b, t, d = x.shape
    return pl.pallas_call(
        _ln_kernel,
        grid=(b, t // tm),
        in_specs=[pl.BlockSpec((1, tm, d), lambda i, j: (i, j, 0)),
                  _full((1, d)),
                  pl.BlockSpec((1, 1, d), lambda i, j: (i, 0, 0)),
                  pl.BlockSpec((1, 1, d), lambda i, j: (i, 0, 0))],
        out_specs=pl.BlockSpec((1, tm, d), lambda i, j: (i, j, 0)),
        out_shape=jax.ShapeDtypeStruct((b, t, d), out_dtype),
        compiler_params=_cparams(("arbitrary", "arbitrary")),
        name="ln_mod",
    )(x, g.reshape(1, d), sc.reshape(b, 1, d), sh.reshape(b, 1, d))


def proj_multi(h, groups, tm=512, name="proj"):
    b, t, d = h.shape
    widths = [g[0].shape[1] for g in groups]
    starts = np.cumsum([0] + widths)
    w_cat = jnp.concatenate([g[0] for g in groups], axis=1).astype(BF16)
    extras = [e for g in groups for e in g[3]]
    n_ex = [len(g[3]) for g in groups]
    n_out = len(groups)

    def kern(h_ref, w_ref, *rest):
        ex_refs = rest[:len(extras)]
        o_refs = rest[len(extras):]
        y = _dot(h_ref[0], w_ref[...])
        pos = 0
        for gi, (_, out_dtype, epilogue, _) in enumerate(groups):
            yg = y[:, starts[gi]:starts[gi + 1]]
            if epilogue is not None:
                yg = epilogue(yg, *[e[...] for e in ex_refs[pos:pos + n_ex[gi]]])
            pos += n_ex[gi]
            o_refs[gi][0] = yg.astype(out_dtype)

    return pl.pallas_call(
        kern,
        grid=(b, t // tm),
        in_specs=[pl.BlockSpec((1, tm, d), lambda i, j: (i, j, 0)), _full((d, int(starts[-1])))]
                 + [_full(e.shape) for e in extras],
        out_specs=[pl.BlockSpec((1, tm, n), lambda i, j: (i, j, 0)) for n in widths],
        out_shape=[jax.ShapeDtypeStruct((b, t, n), g[1]) for n, g in zip(widths, groups)],
        compiler_params=_cparams(("arbitrary", "arbitrary")),
        name=name,
    )(h, w_cat, *extras)


def _head_norm_epilogue(dh):
    inv = 1.0 / dh

    def ep(y, bd, gain):
        ssq = _dot((y * y).astype(BF16), bd)
        return y * lax.rsqrt(ssq * inv + EPS) * gain

    return ep


def _block_diag_ones(n, dh):
    i = np.arange(n) // dh
    return jnp.asarray((i[:, None] == i[None, :]).astype(np.float32), dtype=BF16)


def _norm_mod(x, gain, sc, sh):
    y = x * lax.rsqrt(jnp.mean(x * x, axis=-1, keepdims=True) + EPS) * gain
    return y * (1.0 + sc) + sh


def _outproj_kernel(y1_ref, y2_ref, wa_ref, wb_ref, x_ref, g_ref, n_ref, sc_ref, sh_ref, o_ref, h_ref):
    y = _dot(y1_ref[0], wa_ref[...]) + _dot(y2_ref[0], wb_ref[...])
    x_new = x_ref[0] + g_ref[0] * y
    o_ref[0] = x_new
    h_ref[0] = _norm_mod(x_new, n_ref[...], sc_ref[0], sh_ref[0]).astype(h_ref.dtype)


def out_proj(y1, y2, w_out, x, gate, g_norm, sc, sh, tm=512):
    b, t, d = x.shape
    n1, n2 = y1.shape[-1], y2.shape[-1]
    wa = w_out[:n1].astype(BF16)
    wb = w_out[n1:].astype(BF16)
    row = lambda n: pl.BlockSpec((1, tm, n), lambda i, j: (i, j, 0))
    per_b = lambda: pl.BlockSpec((1, 1, d), lambda i, j: (i, 0, 0))
    return pl.pallas_call(
        _outproj_kernel,
        grid=(b, t // tm),
        in_specs=[row(n1), row(n2), _full((n1, d)), _full((n2, d)), row(d), per_b(), _full((1, d)), per_b(), per_b()],
        out_specs=[row(d), row(d)],
        out_shape=[jax.ShapeDtypeStruct((b, t, d), F32), jax.ShapeDtypeStruct((b, t, d), F32)],
        compiler_params=_cparams(("arbitrary", "arbitrary")),
        name="out_proj",
    )(y1, y2, wa, wb, x, gate.reshape(b, 1, d), g_norm.reshape(1, d), sc.reshape(b, 1, d), sh.reshape(b, 1, d))


def _decay_kernel(s_ref, fb_ref, tril_ref, place_ref, o_ref, carry):
    @pl.when(pl.program_id(1) == 0)
    def _():
        carry[...] = jnp.zeros_like(carry)

    tm = s_ref.shape[1]
    lf = _log_sigmoid(s_ref[0] + fb_ref[...])
    cum = _dot_hi(tril_ref[...], lf) + carry[...]
    carry[...] = cum[tm - 1:tm, :]
    x = cum * LOG2E
    hi = x.astype(BF16)
    r1 = x - hi.astype(F32)
    mid = r1.astype(BF16)
    low = (r1 - mid.astype(F32)).astype(BF16)
    o_ref[0] = _dot(jnp.concatenate([hi, mid, low], axis=1), place_ref[...]).astype(o_ref.dtype)


def fox_decay(small, fox_fb, tm=512):
    b, t, _ = small.shape
    fb = jnp.zeros((1, LANES), F32).at[0, :FOX_HEADS].set(fox_fb)
    tril = jnp.asarray(np.tril(np.ones((tm, tm), np.float32)))
    place = np.zeros((3 * LANES, FOX_W), np.float32)
    for h in range(FOX_HEADS):
        for j in range(3):
            place[j * LANES + h, (h // 2) * LANES + (FOX_DH if h % 2 == 0 else 0) + j] = 1.0
    return pl.pallas_call(
        _decay_kernel,
        grid=(b, t // tm),
        in_specs=[pl.BlockSpec((1, tm, LANES), lambda i, j: (i, j, 0)), _full((1, LANES)), _full((tm, tm)),
                  _full((3 * LANES, FOX_W))],
        out_specs=pl.BlockSpec((1, tm, FOX_W), lambda i, j: (i, j, 0)),
        out_shape=jax.ShapeDtypeStruct((b, t, FOX_W), BF16),
        scratch_shapes=[pltpu.VMEM((1, LANES), F32)],
        compiler_params=_cparams(("arbitrary", "arbitrary")),
        name="fox_decay",
    )(small, fb, tril, jnp.asarray(place, dtype=BF16))


def _fox_kernel(q_ref, k_ref, v_ref, f_ref, o_ref, *, tq, wide):
    i = pl.program_id(2)
    lane = lax.broadcasted_iota(jnp.int32, (1, LANES), 1)
    lo = lane < FOX_DH
    coef = jnp.where((lane & (FOX_DH - 1)) < 3, -1.0, 0.0).astype(BF16)
    q = q_ref[0]
    qs = (jnp.where(lo, q, coef), jnp.where(lo, coef, q))
    causal = (lax.broadcasted_iota(jnp.int32, (tq, tq), 1) <= lax.broadcasted_iota(jnp.int32, (tq, tq), 0))
    one = jnp.ones((1, LANES), BF16)

    def step(j, carry, tw, diag=False):
        start = pl.multiple_of(j * tw, tw)
        kt = k_ref[0, pl.ds(start, tw), :]
        ft = f_ref[0, pl.ds(start, tw), :]
        vt = v_ref[0, pl.ds(start, tw), :]
        s_pair = (_dot_nt(qs[0], jnp.where(lo, kt, ft)), _dot_nt(qs[1], jnp.where(lo, ft, kt)))
        vs = (jnp.where(lo, vt, one), jnp.where(lo, one, vt))
        new = []
        for hh in range(2):
            m, acc = carry[hh]
            s = s_pair[hh]
            if diag:
                s = jnp.where(causal, s, NEG)
            m_new = jnp.maximum(m, jnp.max(s, axis=1, keepdims=True))
            p = jnp.exp2(s - m_new)
            acc = jnp.exp2(m - m_new) * acc + _dot(p.astype(BF16), vs[hh])
            new.append((m_new, acc))
        return tuple(new)

    carry = tuple((jnp.full((tq, 1), NEG, F32), jnp.zeros((tq, LANES), F32)) for _ in range(2))
    n_wide = i // wide
    carry = lax.fori_loop(0, n_wide, lambda j, c: step(j, c, wide * tq), carry)
    done = n_wide * wide
    part = wide // 2
    while part >= 1:
        carry = lax.cond((i & part) != 0, lambda c, d=done, w=part: step(d // w, c, w * tq), lambda c: c, carry)
        done = done + (i & part)
        part //= 2
    carry = step(i, carry, tq, diag=True)
    acc = jnp.where(lo, carry[0][1], carry[1][1])
    den = jnp.where(lo, carry[1][1], carry[0][1])
    o_ref[0] = (acc / pltpu.roll(den, FOX_DH, 1)).astype(o_ref.dtype)


def fox_attention(q, k, v, feat, tq=512, wide=4):
    b, t, w = q.shape
    npair = w // LANES
    nt = t // tq
    whole = lambda: pl.BlockSpec((1, t, LANES), lambda bi, p, i: (bi, 0, p))
    return pl.pallas_call(
        functools.partial(_fox_kernel, tq=tq, wide=wide),
        grid=(b, npair, nt),
        in_specs=[pl.BlockSpec((1, tq, LANES), lambda bi, p, i: (bi, i, p)), whole(), whole(), whole()],
        out_specs=pl.BlockSpec((1, tq, LANES), lambda bi, p, i: (bi, i, p)),
        out_shape=jax.ShapeDtypeStruct((b, t, w), BF16),
        compiler_params=_cparams(("arbitrary", "arbitrary", "arbitrary")),
        name="fox_attn",
    )(q, k, v, feat)


def _mm(a, b):
    return _dot(a.astype(BF16), b.astype(BF16))


def _mm3(a, b):
    ah = a.astype(BF16)
    bh = b.astype(BF16)
    al = (a - ah.astype(F32)).astype(BF16)
    bl = (b - bh.astype(F32)).astype(BF16)
    return _dot(jnp.concatenate([ah, ah, al], axis=1), jnp.concatenate([bh, bl, bh], axis=0))


def _tril_solve(a, rhs, ri, ci):
    n = a[0].shape[0]
    both = lambda f, x, y: [f(p, q) for p, q in zip(x, y)]
    eye = (ri == ci).astype(F32)
    same = lambda b: (lax.shift_right_logical(ri, int(math.log2(b)))
                      == lax.shift_right_logical(ci, int(math.log2(b))))
    base = 16
    d = [jnp.where(same(base), p, 0.0) for p in a]
    d2 = both(_mm, d, d)
    d4 = both(_mm, d2, d2)
    r1 = [eye - p + p2 - t for p, p2, t in zip(d, d2, both(_mm, d, d2))]
    d8 = both(_mm, d4, d4)
    r2 = [eye + p4 + p8 + t for p4, p8, t in zip(d4, d8, both(_mm, d4, d8))]
    t = both(_mm, r1, r2)
    b = base
    while b < n:
        join = same(2 * b) & jnp.logical_not(same(b))
        low = [jnp.where(join, p, 0.0) for p in a]
        t = [p - q for p, q in zip(t, both(_mm, both(_mm, t, low), t))]
        b *= 2
    return both(_mm3, t, rhs)


GDN_BLOCK = 128


def _gdn_kernel(x_ref, sm_ref, z_ref, cw_ref, ega_ref, egb_ref, alog_ref, dtb_ref, on_ref, tril_ref,
                o_ref, s_scr, prev_scr, *, tc):
    c = GDN_BLOCK
    w = GDN_W

    @pl.when(pl.program_id(1) == 0)
    def _():
        s_scr[...] = jnp.zeros_like(s_scr)
        prev_scr[...] = jnp.zeros_like(prev_scr)

    x = x_ref[0]
    prev = prev_scr[...]
    row8 = lax.broadcasted_iota(jnp.int32, (8, 1), 0)
    acc = x * cw_ref[GDN_CONV - 1:GDN_CONV, :]
    for s in range(1, GDN_CONV):
        rolled = pltpu.roll(x, s, 0)
        head = jnp.where(row8 < s, pltpu.roll(prev, s, 0), rolled[0:8])
        shifted = jnp.concatenate([head, rolled[8:]], axis=0)
        acc = acc + shifted * cw_ref[GDN_CONV - 1 - s:GDN_CONV - s, :]
    prev_scr[...] = x[tc - 8:tc]
    xc = _silu(acc)

    sm = sm_ref[0]
    g_raw = _dot_hi(sm, ega_ref[...])
    b_raw = _dot_hi(sm, egb_ref[...])
    g = -jnp.exp(alog_ref[...]) * _softplus(g_raw + dtb_ref[...])
    beta_all = _sigmoid(b_raw)
    gc_all = _dot_hi(tril_ref[...], g)

    ri = lax.broadcasted_iota(jnp.int32, (c, c), 0)
    ci = lax.broadcasted_iota(jnp.int32, (c, c), 1)
    causal = ci <= ri
    strict = ci < ri

    nblk = tc // c
    a_l, attn_l, rhs_l, qd_l, kd_l, egl_l = [], [], [], [], [], []
    for h in range(GDN_HEADS):
        ln = slice(h * GDN_DH, (h + 1) * GDN_DH)
        qh = xc[:, h * GDN_DH:(h + 1) * GDN_DH]
        kh = xc[:, w + h * GDN_DH:w + (h + 1) * GDN_DH]
        qh = qh * lax.rsqrt(jnp.sum(qh * qh, axis=-1, keepdims=True) + EPS) * (GDN_DH ** -0.5)
        kh = kh * lax.rsqrt(jnp.sum(kh * kh, axis=-1, keepdims=True) + EPS)
        vh = xc[:, 2 * w + h * GDN_DH:2 * w + (h + 1) * GDN_DH]
        gch = gc_all[:, ln]
        gct = gch.T
        egc = jnp.exp(gch)
        bh = beta_all[:, ln]
        for n in range(nblk):
            sl = slice(n * c, (n + 1) * c)
            q, k, v, gc, be = qh[sl], kh[sl], vh[sl], gch[sl], bh[sl]
            decay = jnp.exp(jnp.where(causal, gc - gct[:, sl], NEG))
            kb = k * be
            kk = _dot_nt(jnp.concatenate([kb, q], axis=0).astype(BF16), k.astype(BF16))
            a_l.append(jnp.where(strict, kk[:c] * decay, 0.0))
            attn_l.append(jnp.where(causal, kk[c:] * decay, 0.0))
            rhs_l.append(jnp.concatenate([v * be, kb * egc[sl]], axis=1))
            gl = gc[c - 1:c, :]
            qd_l.append(q * egc[sl])
            kd_l.append(k * jnp.exp(gl - gc))
            egl_l.append(jnp.exp(gl))
    uw_l = _tril_solve(a_l, rhs_l, ri, ci)

    states = [s_scr[h] for h in range(GDN_HEADS)]
    for n in range(nblk):
        sl = slice(n * c, (n + 1) * c)
        idx = [h * nblk + n for h in range(GDN_HEADS)]
        ws = [_mm(jnp.concatenate([uw_l[i][:, GDN_DH:], qd_l[i]], axis=0), states[h])
              for h, i in enumerate(idx)]
        v_new = [uw_l[i][:, :GDN_DH] - ws[h][:c] for h, i in enumerate(idx)]
        o = [ws[h][c:] + _mm(attn_l[i], v_new[h]) for h, i in enumerate(idx)]
        states = [states[h] * egl_l[i] + _dot_tn(kd_l[i].astype(BF16), v_new[h].astype(BF16))
                  for h, i in enumerate(idx)]
        for h in range(GDN_HEADS):
            ln = slice(h * GDN_DH, (h + 1) * GDN_DH)
            on = o[h] * lax.rsqrt(jnp.mean(o[h] * o[h], axis=-1, keepdims=True) + EPS) * on_ref[...]
            o_ref[0, sl, ln] = (on * _silu(z_ref[0, sl, ln])).astype(o_ref.dtype)
    for h in range(GDN_HEADS):
        s_scr[h] = states[h]


def gated_delta_net(x, small, z, conv_w, a_log, dt_bias, on_gain, tc=512):
    b, t, _ = x.shape
    w = GDN_W
    ega = np.zeros((LANES, w), np.float32)
    egb = np.zeros((LANES, w), np.float32)
    for h in range(GDN_HEADS):
        ega[FOX_HEADS + h, h * GDN_DH:(h + 1) * GDN_DH] = 1.0
        egb[FOX_HEADS + GDN_HEADS + h, h * GDN_DH:(h + 1) * GDN_DH] = 1.0
    alog = jnp.repeat(a_log, GDN_DH).reshape(1, w)
    dtb = jnp.repeat(dt_bias, GDN_DH).reshape(1, w)
    idx = np.arange(tc)
    tril = ((idx[:, None] >= idx[None, :]) & (idx[:, None] // GDN_BLOCK == idx[None, :] // GDN_BLOCK))
    row = lambda n: pl.BlockSpec((1, tc, n), lambda i, j: (i, j, 0))
    return pl.pallas_call(
        functools.partial(_gdn_kernel, tc=tc),
        grid=(b, t // tc),
        in_specs=[row(3 * w), row(LANES), row(w), _full((GDN_CONV, 3 * w)), _full((LANES, w)), _full((LANES, w)),
                  _full((1, w)), _full((1, w)), _full((1, GDN_DH)), _full((tc, tc))],
        out_specs=row(w),
        out_shape=jax.ShapeDtypeStruct((b, t, w), BF16),
        scratch_shapes=[pltpu.VMEM((GDN_HEADS, GDN_DH, GDN_DH), F32), pltpu.VMEM((8, 3 * w), F32)],
        compiler_params=_cparams(("arbitrary", "arbitrary")),
        name="gdn",
    )(x, small, z, conv_w, jnp.asarray(ega), jnp.asarray(egb), alog, dtb, on_gain.reshape(1, GDN_DH),
      jnp.asarray(tril.astype(np.float32)))


def even_mixer(h, w_in, fox_fb, fox_qn, fox_kn, gdn_conv, gdn_a_log, gdn_dt_bias, gdn_on):
    cuts = np.cumsum((0,) + EV_SIZES)
    col = lambda i: w_in[:, cuts[i]:cuts[i + 1]]
    bd = _block_diag_ones(FOX_W, FOX_DH)
    ep = _head_norm_epilogue(FOX_DH)
    qg = (jnp.tile(fox_qn, FOX_HEADS) * (FOX_DH ** -0.5 * LOG2E)).reshape(1, FOX_W)
    kg = jnp.tile(fox_kn, FOX_HEADS).reshape(1, FOX_W)
    w_small = jnp.zeros((w_in.shape[0], LANES), F32)
    w_small = w_small.at[:, 0:8].set(col(3)).at[:, 8:12].set(col(5)).at[:, 12:16].set(col(6))
    fq, fk, fv, small, gqkv, gz = proj_multi(
        h, [(col(0), BF16, ep, (bd, qg)), (col(1), BF16, ep, (bd, kg)), (col(2), BF16, None, ()),
            (w_small, F32, None, ()), (col(4), F32, None, ()), (col(7), F32, None, ())], name="proj_even")
    feat = fox_decay(small, fox_fb)
    o_fox = fox_attention(fq, fk, fv, feat)
    o_gdn = gated_delta_net(gqkv, small, gz, gdn_conv, gdn_a_log, gdn_dt_bias, gdn_on)
    return o_fox, o_gdn


def _t5_bucket_np(dist):
    n = np.maximum(dist, 0)
    exact = REL_BUCKETS // 2
    nf = np.maximum(n, 1).astype(np.float32)
    large = exact + (np.log(nf / np.float32(exact)) / np.float32(math.log(REL_MAX_DIST / exact))
                     * np.float32(REL_BUCKETS - exact)).astype(np.int32)
    large = np.minimum(large, REL_BUCKETS - 1)
    return np.where(n < exact, n, large)


def _bias_kernel(tbl_ref, bucket_ref, o_ref):
    h = pl.program_id(0)
    bucket = bucket_ref[...]
    acc = jnp.full(bucket.shape, NEG, F32)
    for b in range(REL_BUCKETS):
        acc = jnp.where(bucket == b, tbl_ref[b, h], acc)
    o_ref[0] = acc


def _bias_table(rel_bias, dist, valid):
    shifted = (rel_bias - rel_bias[REL_BUCKETS - 1:REL_BUCKETS]) * LOG2E
    bucket = np.where(valid, _t5_bucket_np(dist), -1).astype(np.int32)
    rows, cols = int(np.prod(bucket.shape[:-1])), bucket.shape[-1]
    nh = rel_bias.shape[1]
    tb = pl.pallas_call(
        _bias_kernel,
        grid=(nh,),
        in_specs=[pl.BlockSpec(memory_space=pltpu.SMEM), _full((rows, cols))],
        out_specs=pl.BlockSpec((1, rows, cols), lambda h: (h, 0, 0)),
        out_shape=jax.ShapeDtypeStruct((nh, rows, cols), F32),
        compiler_params=_cparams(("arbitrary",)),
        name="t5_bias",
    )(shifted, jnp.asarray(bucket.reshape(rows, cols)))
    return tb.reshape((nh,) + bucket.shape)


def _cmp_kernel(r_ref, pos_ref, w1_ref, w2_ref, kn_ref, o_ref):
    m = r_ref.shape[3]
    half = r_ref.shape[4]
    r = r_ref[0, 0, 0].astype(BF16)
    a = _dot(r, w1_ref[0, :half, :])
    bm = _dot(r, w1_ref[0, half:, :])
    c = _dot(pos_ref[0].astype(BF16), w1_ref[0])
    hid = a + pltpu.roll(bm, m - 1, 0) + c[0:1, :]
    out = _dot(_silu(hid).astype(BF16), w2_ref[0])
    normed = out * lax.rsqrt(jnp.mean(out * out, axis=-1, keepdims=True) + EPS) * kn_ref[...]
    o_ref[0, 0, 0] = jnp.where(pl.program_id(0) == 0, normed, out).astype(o_ref.dtype)


def nsa_compress(kcvc, pos, w1, w2, kn):
    b, t, _ = kcvc.shape
    m = t // CMP_STRIDE
    half = CMP_STRIDE * NSA_DH
    r = kcvc.reshape(b, m, CMP_STRIDE, 2, NSA_KV_HEADS, NSA_DH).transpose(3, 0, 4, 1, 2, 5).reshape(2, b, 2, m, half)
    posf = jnp.zeros((2, 8, 2 * half), F32).at[:, 0].set(pos.reshape(2, 2 * half))
    w2d = jnp.concatenate([w2, w2], axis=-1).astype(BF16)
    knd = jnp.tile(kn, 2).reshape(1, LANES)
    return pl.pallas_call(
        _cmp_kernel,
        grid=(2, b, NSA_KV_HEADS),
        in_specs=[pl.BlockSpec((1, 1, 1, m, half), lambda s, i, k: (s, i, k, 0, 0)),
                  pl.BlockSpec((1, 8, 2 * half), lambda s, i, k: (s, 0, 0)),
                  pl.BlockSpec((1, 2 * half, CMP_HIDDEN), lambda s, i, k: (s, 0, 0)),
                  pl.BlockSpec((1, CMP_HIDDEN, LANES), lambda s, i, k: (s, 0, 0)),
                  _full((1, LANES))],
        out_specs=pl.BlockSpec((1, 1, 1, m, LANES), lambda s, i, k: (s, i, k, 0, 0)),
        out_shape=jax.ShapeDtypeStruct((2, b, NSA_KV_HEADS, m, LANES), BF16),
        compiler_params=_cparams(("arbitrary", "arbitrary", "arbitrary")),
        name="nsa_compress",
    )(r, posf, w1.astype(BF16), w2d, knd)


def _dot_split(a, b):
    hi = a.astype(BF16)
    lo = (a - hi.astype(F32)).astype(BF16)
    return _dot(hi, b) + _dot(lo, b)


def _head_q(q_ref, hh, lo):
    blk = q_ref[0, :, (hh // 2) * LANES:(hh // 2 + 1) * LANES]
    keep = lo if hh % 2 == 0 else jnp.logical_not(lo)
    return jnp.where(keep, blk, jnp.zeros_like(blk))


def _pair_heads(o, lo):
    return jnp.concatenate([jnp.where(lo, o[0], o[1]), jnp.where(lo, o[2], o[3])], axis=1)


def _nsa_sel_kernel(q_ref, kc_ref, vc_ref, ov_ref, bt_ref, o_ref, sel_ref, *, tq, nband, n_slc):
    i = pl.program_id(2)
    ncp = kc_ref.shape[3]
    nsp = ov_ref.shape[1]
    per = tq // CMP_STRIDE
    var = jnp.minimum(i, 1)
    bs = pl.multiple_of(per * jnp.maximum(i - 1, 0), per)
    lo = lax.broadcasted_iota(jnp.int32, (1, LANES), 1) < NSA_DH
    kc = kc_ref[0, 0, 0]
    vc = vc_ref[0, 0, 0]
    kcb = kc_ref[0, 0, 0, pl.ds(bs, nband), :]
    vcb = vc_ref[0, 0, 0, pl.ds(bs, nband), :]
    far_ok = lax.broadcasted_iota(jnp.int32, (1, ncp), 1) < per * (i - 1)
    hs = range(NSA_GROUP)
    qh = [_head_q(q_ref, hh, lo) for hh in hs]
    s_far = [jnp.where(far_ok, _dot_nt(q, kc), NEG) for q in qh]
    s_band = [_dot_nt(qh[hh], kcb) + bt_ref[var, hh] for hh in hs]
    m = [jnp.maximum(jnp.max(a, axis=1, keepdims=True), jnp.max(b, axis=1, keepdims=True))
         for a, b in zip(s_far, s_band)]
    m = [jnp.where(x < 0.5 * NEG, 0.0, x) for x in m]
    p_far = [jnp.exp2(a - x) for a, x in zip(s_far, m)]
    p_band = [jnp.exp2(b - x) for b, x in zip(s_band, m)]
    l = [jnp.sum(a, axis=1, keepdims=True) + jnp.sum(b, axis=1, keepdims=True) for a, b in zip(p_far, p_band)]
    inv = [1.0 / jnp.where(x == 0.0, 1.0, x) for x in l]
    outs = [(_dot(a.astype(BF16), vc) + _dot(b.astype(BF16), vcb)) * x for a, b, x in zip(p_far, p_band, inv)]
    ps_far = p_far[0] * inv[0]
    ps_band = p_band[0] * inv[0]
    for hh in range(1, NSA_GROUP):
        ps_far = ps_far + p_far[hh] * inv[hh]
        ps_band = ps_band + p_band[hh] * inv[hh]
    o_ref[0] = _pair_heads(outs, lo).astype(o_ref.dtype)

    imp = _dot_split(ps_far, ov_ref[...]) + _dot_split(ps_band, ov_ref[pl.ds(bs, nband), :])
    blk = lax.broadcasted_iota(jnp.int32, (1, nsp), 1)
    blk_f = blk.astype(F32)
    qpos = i * tq + lax.broadcasted_iota(jnp.int32, (tq, 1), 0)
    cur = lax.shift_right_logical(qpos, int(math.log2(SLC_LEN)))
    forced = (blk == 0) | (blk == cur) | (blk == cur - 1)
    work = jnp.where(forced, -jnp.inf, jnp.where(blk <= cur, imp, NEG))
    work = jnp.where(blk < n_slc, work, -jnp.inf)
    ngrp = 4
    rg = tq // ngrp
    works = [work[r * rg:(r + 1) * rg] for r in range(ngrp)]
    sels = [jnp.where(forced[r * rg:(r + 1) * rg], 1.0, jnp.zeros((rg, nsp), F32)) for r in range(ngrp)]
    for _ in range(max(min(SLC_TOPK, n_slc) - 3, 0)):
        ms = [jnp.max(w, axis=1, keepdims=True) for w in works]
        firsts = [jnp.min(jnp.where(w == m, blk_f, float(nsp)), axis=1, keepdims=True) for w, m in zip(works, ms)]
        picks = [blk_f == f for f in firsts]
        sels = [jnp.where(p, 1.0, s) for p, s in zip(picks, sels)]
        works = [jnp.where(p, -jnp.inf, w) for p, w in zip(picks, works)]
    sel = jnp.concatenate(sels, axis=0)
    for sup in range(sel_ref.shape[2]):
        col = sel[:, (sup // 2) * LANES:(sup // 2 + 1) * LANES]
        if sup % 2 == 0:
            col = pltpu.roll(col, SUP_BLOCKS, 1)
        sel_ref[0, 0, sup] = jnp.where(lo, 0.0, jnp.where(col > 0.5, 0.0, NEG)).astype(sel_ref.dtype)


def nsa_select(q, cmp_kv, rel_bias, tq=512):
    b, t, _ = q.shape
    ncp = t // CMP_STRIDE
    n_cmp = ncp - 1
    n_slc = t // SLC_LEN
    nsp = max(LANES, n_slc)
    nsup = max(1, n_slc // SUP_BLOCKS)
    per = tq // CMP_STRIDE
    nband = 2 * per
    n = np.arange(ncp)[:, None]
    s = np.arange(nsp)[None, :]
    ov = ((CMP_STRIDE * n < SLC_LEN * s + SLC_LEN) & (CMP_STRIDE * n + CMP_LEN > SLC_LEN * s)
          & (n < n_cmp) & (s < n_slc)).astype(np.float32)
    qi = np.arange(tq)[:, None]
    nj = np.arange(nband)[None, :]
    end = CMP_STRIDE * nj + CMP_LEN - 1
    dist = np.stack([qi - end, tq + qi - end])
    bt = _bias_table(rel_bias, dist, dist >= 0)
    bt = bt.reshape(NSA_KV_HEADS, NSA_GROUP, 2, tq, nband).transpose(0, 2, 1, 3, 4)
    bt = bt.reshape(NSA_KV_HEADS * 2, NSA_GROUP, tq, nband)
    gw = NSA_GROUP * NSA_DH
    return pl.pallas_call(
        functools.partial(_nsa_sel_kernel, tq=tq, nband=nband, n_slc=n_slc),
        grid=(b, NSA_KV_HEADS, t // tq),
        in_specs=[pl.BlockSpec((1, tq, gw), lambda bi, k, i: (bi, i, k)),
                  pl.BlockSpec((1, 1, 1, ncp, LANES), lambda bi, k, i: (0, bi, k, 0, 0)),
                  pl.BlockSpec((1, 1, 1, ncp, LANES), lambda bi, k, i: (1, bi, k, 0, 0)),
                  _full((ncp, nsp)),
                  pl.BlockSpec((2, NSA_GROUP, tq, nband), lambda bi, k, i: (k, 0, 0, 0))],
        out_specs=[pl.BlockSpec((1, tq, gw), lambda bi, k, i: (bi, i, k)),
                   pl.BlockSpec((1, 1, nsup, tq, LANES), lambda bi, k, i: (bi, k, 0, i, 0))],
        out_shape=[jax.ShapeDtypeStruct((b, t, NSA_W), BF16),
                   jax.ShapeDtypeStruct((b, NSA_KV_HEADS, nsup, t, LANES), BF16)],
        compiler_params=_cparams(("arbitrary", "arbitrary", "arbitrary")),
        name="nsa_select",
    )(q, cmp_kv, cmp_kv, jnp.asarray(ov, dtype=BF16), bt)


def _nsa_main_kernel(q_ref, ks_ref, vs_ref, kw0_ref, kw1_ref, kw2_ref, vw0_ref, vw1_ref, vw2_ref, sel_ref, oh_ref,
                     ocmp_ref, gate_ref, tb_ref, wm_ref, eg_ref, o_ref, *, tq, wide):
    i = pl.program_id(2)
    g = NSA_GROUP
    lo = lax.broadcasted_iota(jnp.int32, (1, LANES), 1) < NSA_DH

    def head_low(hh):
        blk = q_ref[0, :, (hh // 2) * LANES:(hh // 2 + 1) * LANES]
        if hh % 2 == 1:
            blk = pltpu.roll(blk.astype(F32), NSA_DH, 1).astype(BF16)
        return jnp.where(lo, blk, jnp.zeros_like(blk))

    qst = jnp.concatenate([head_low(hh) for hh in range(g)], axis=0)
    causal = (lax.broadcasted_iota(jnp.int32, (tq, tq), 1) <= lax.broadcasted_iota(jnp.int32, (tq, tq), 0))
    one = jnp.ones((1, LANES), BF16)
    sup_keys = SUP_BLOCKS * SLC_LEN
    all_masked = jnp.where(lo, 0.0, NEG).astype(BF16)

    def sel_step(jt, carry, near, tw):
        m, acc = carry
        key0 = jnp.maximum(jt, 0) * tw
        start = pl.multiple_of(key0, tw)
        within = pl.multiple_of(key0 & (sup_keys - 1), tw)
        mq = sel_ref[0, 0, lax.shift_right_logical(key0, int(math.log2(sup_keys)))]
        mq = jnp.where(jt >= 0, mq, all_masked)
        q_ext = qst + jnp.concatenate([mq] * g, axis=0)
        kt = jnp.where(lo, ks_ref[0, pl.ds(start, tw), :], oh_ref[pl.ds(within, tw), :])
        vt = jnp.where(lo, vs_ref[0, pl.ds(start, tw), :], one)
        s = _dot_nt(q_ext, kt)
        if near is not None:
            s = s.reshape(g, tq, tw) + tb_ref[:, :, near * tq:(near + 1) * tq]
            if near == 2:
                s = jnp.where(causal[None], s, NEG)
            s = s.reshape(g * tq, tw)
        m_new = jnp.maximum(m, jnp.max(s, axis=1, keepdims=True))
        p = jnp.exp2(s - m_new)
        acc = jnp.exp2(m - m_new) * acc + _dot(p.astype(BF16), vt)
        return m_new, acc

    n_far = jnp.maximum(i - 2, 0)
    n_wide = n_far // wide
    carry = (jnp.full((g * tq, 1), NEG, F32), jnp.zeros((g * tq, LANES), F32))
    carry = lax.fori_loop(0, n_wide, lambda j, c: sel_step(j, c, None, wide * tq), carry)
    done = n_wide * wide
    part = wide // 2
    while part >= 1:
        carry = lax.cond((n_far & part) != 0, lambda c, d=done, w=part: sel_step(d // w, c, None, w * tq),
                         lambda c: c, carry)
        done = done + (n_far & part)
        part //= 2
    for near in range(3):
        carry = sel_step(i - 2 + near, carry, near, tq)
    o_slc = (carry[1] / pltpu.roll(carry[1], NSA_DH, 1)).reshape(g, tq, LANES)

    var = jnp.minimum(i, 2)
    kws = (kw0_ref, kw1_ref, kw2_ref)
    vws = (vw0_ref, vw1_ref, vw2_ref)
    sw = []
    for near in range(3):
        s = _dot_nt(qst, kws[near][0]).reshape(g, tq, tq)
        s = s + tb_ref[:, :, near * tq:(near + 1) * tq] + wm_ref[var, :, near * tq:(near + 1) * tq][None]
        sw.append(s.reshape(g * tq, tq))
    m = jnp.maximum(jnp.maximum(jnp.max(sw[0], axis=1, keepdims=True), jnp.max(sw[1], axis=1, keepdims=True)),
                    jnp.max(sw[2], axis=1, keepdims=True))
    acc = jnp.zeros((g * tq, LANES), F32)
    for near in range(3):
        p = jnp.exp2(sw[near] - m)
        acc = acc + _dot(p.astype(BF16), jnp.where(lo, vws[near][0], one))
    o_win = (acc / pltpu.roll(acc, NSA_DH, 1)).reshape(g, tq, LANES)

    pair = lambda o: jnp.concatenate([jnp.where(lo, o[0], pltpu.roll(o[1], NSA_DH, 1)),
                                      jnp.where(lo, o[2], pltpu.roll(o[3], NSA_DH, 1))], axis=1)
    gates = _dot_hi(_sigmoid(gate_ref[0]), eg_ref[0])
    gw = g * NSA_DH
    out = (gates[:, 0:gw] * ocmp_ref[0].astype(F32)
           + gates[:, gw:2 * gw] * pair(o_slc) + gates[:, 2 * gw:3 * gw] * pair(o_win))
    o_ref[0] = out.astype(o_ref.dtype)


def nsa_main(q, ksw, vsw, sel, o_cmp, small, rel_bias, tq=256, wide=8):
    b, t, _ = q.shape
    nsup = sel.shape[2]
    g = NSA_GROUP
    gw = g * NSA_DH
    sup_keys = SUP_BLOCKS * SLC_LEN
    oh = np.zeros((sup_keys, LANES), np.float32)
    oh[np.arange(sup_keys), NSA_DH + np.arange(sup_keys) // SLC_LEN] = 1.0
    qi = np.arange(tq)[:, None]
    c = np.arange(3 * tq)[None, :]
    dist = qi + 2 * tq - c
    tb = _bias_table(rel_bias, dist, np.ones_like(dist, bool))
    wm = np.zeros((3, tq, 3 * tq), np.float32)
    for var in range(3):
        exists = c >= tq * (2 - var)
        wm[var] = np.where((dist >= 0) & (dist < WINDOW) & exists, 0.0, NEG)
    eg = np.zeros((NSA_KV_HEADS, LANES, 3 * gw), np.float32)
    for k in range(NSA_KV_HEADS):
        for hh in range(g):
            for br in range(3):
                eg[k, (k * g + hh) * 3 + br, br * gw + hh * NSA_DH:br * gw + (hh + 1) * NSA_DH] = 1.0
    near = lambda off, col: pl.BlockSpec(
        (1, tq, LANES), lambda bi, k, i: (bi, jnp.maximum(i - off, 0), col + k))
    return pl.pallas_call(
        functools.partial(_nsa_main_kernel, tq=tq, wide=wide),
        grid=(b, NSA_KV_HEADS, t // tq),
        in_specs=[pl.BlockSpec((1, tq, gw), lambda bi, k, i: (bi, i, k)),
                  pl.BlockSpec((1, t, LANES), lambda bi, k, i: (bi, 0, k)),
                  pl.BlockSpec((1, t, LANES), lambda bi, k, i: (bi, 0, k)),
                  near(2, 2), near(1, 2), near(0, 2), near(2, 2), near(1, 2), near(0, 2),
                  pl.BlockSpec((1, 1, nsup, tq, LANES), lambda bi, k, i: (bi, k, 0, i, 0)),
                  _full((sup_keys, LANES)),
                  pl.BlockSpec((1, tq, gw), lambda bi, k, i: (bi, i, k)),
                  pl.BlockSpec((1, tq, LANES), lambda bi, k, i: (bi, i, 0)),
                  pl.BlockSpec((g, tq, 3 * tq), lambda bi, k, i: (k, 0, 0)),
                  _full((3, tq, 3 * tq)),
                  pl.BlockSpec((1, LANES, 3 * gw), lambda bi, k, i: (k, 0, 0))],
        out_specs=pl.BlockSpec((1, tq, gw), lambda bi, k, i: (bi, i, k)),
        out_shape=jax.ShapeDtypeStruct((b, t, NSA_W), BF16),
        compiler_params=_cparams(("arbitrary", "arbitrary", "arbitrary")),
        name="nsa_main",
    )(q, ksw, vsw, ksw, ksw, ksw, vsw, vsw, vsw, sel, jnp.asarray(oh, dtype=BF16), o_cmp, small, tb,
      jnp.asarray(wm), jnp.asarray(eg))


def _gla_kernel(qk_ref, v_ref, r_ref, sm_ref, wg_ref, bg_ref, on_ref, tril_ref, o_ref, s_scr, *, tc):
    c = GLA_CHUNK

    @pl.when(pl.program_id(1) == 0)
    def _():
        s_scr[...] = jnp.zeros_like(s_scr)

    kw = GLA_KW
    log_a = _log_sigmoid(_dot_hi(sm_ref[0], wg_ref[...]) + bg_ref[...]) * (1.0 / GLA_TAU)
    gcum = _dot_hi(tril_ref[...], log_a)
    q = qk_ref[0, :, 0:kw] * (GLA_DK ** -0.5)
    k = qk_ref[0, :, kw:2 * kw]
    q_dec = (q * jnp.exp(gcum)).astype(BF16)
    k_inv = (k * jnp.exp(-gcum)).astype(BF16)
    ri = lax.broadcasted_iota(jnp.int32, (c, c), 0)
    ci = lax.broadcasted_iota(jnp.int32, (c, c), 1)
    causal = ci <= ri
    lo = lax.broadcasted_iota(jnp.int32, (1, LANES), 1) < GLA_DK
    nchunk = tc // c
    heads = range(GLA_HEADS)
    zero = jnp.zeros((c, LANES), BF16)

    qm, vb, kd, egl = {}, {}, {}, {}
    for n in range(nchunk):
        sl = slice(n * c, (n + 1) * c)
        gl = gcum[n * c + c - 1:n * c + c, :]
        kdn = (k[sl] * jnp.exp(gl - gcum[sl])).astype(BF16)
        for h in heads:
            pr = slice((h // 2) * LANES, (h // 2 + 1) * LANES)
            keep = lo if h % 2 == 0 else jnp.logical_not(lo)
            qm[n, h] = jnp.where(keep, q_dec[sl, pr], zero)
            vb[n, h] = v_ref[0, sl, h * GLA_DV:(h + 1) * GLA_DV].astype(BF16)
            kd[n, h] = kdn[:, pr]
            egl[n, h] = jnp.exp(gl[:, pr])
    idx = [(n, h) for n in range(nchunk) for h in heads]
    attn = {i: jnp.where(causal, _dot_nt(qm[i], k_inv[i[0] * c:(i[0] + 1) * c, (i[1] // 2) * LANES:
                                                       (i[1] // 2 + 1) * LANES]), 0.0).astype(BF16) for i in idx}
    o_intra = {i: _dot(attn[i], vb[i]) for i in idx}
    kv = {i: _dot_tn(vb[i], kd[i]) for i in idx}

    st = [s_scr[h] for h in heads]
    for n in range(nchunk):
        sl = slice(n * c, (n + 1) * c)
        o = [o_intra[n, h] + _dot_nt(qm[n, h], st[h].astype(BF16)) for h in heads]
        st = [st[h] * egl[n, h] + kv[n, h] for h in heads]
        for h in heads:
            on = o[h] * lax.rsqrt(jnp.mean(o[h] * o[h], axis=-1, keepdims=True) + EPS) * on_ref[...]
            o_ref[0, sl, h * GLA_DV:(h + 1) * GLA_DV] = (
                on * _silu(r_ref[0, sl, h * GLA_DV:(h + 1) * GLA_DV])).astype(o_ref.dtype)
    for h in heads:
        s_scr[h] = st[h]


def gated_linear_attention(qkvr, small, wg_up, bg, on_gain, tc=512):
    b, t, _ = qkvr.shape
    wg = jnp.zeros((LANES, GLA_KW), F32).at[3 * NSA_HEADS:3 * NSA_HEADS + GLA_GATE_RANK].set(wg_up)
    idx = np.arange(tc)
    tril = ((idx[:, None] >= idx[None, :]) & (idx[:, None] // GLA_CHUNK == idx[None, :] // GLA_CHUNK))
    return pl.pallas_call(
        functools.partial(_gla_kernel, tc=tc),
        grid=(b, t // tc),
        in_specs=[pl.BlockSpec((1, tc, 2 * GLA_KW), lambda i, j: (i, j, 0)),
                  pl.BlockSpec((1, tc, GLA_W), lambda i, j: (i, j, 1)),
                  pl.BlockSpec((1, tc, GLA_W), lambda i, j: (i, j, 2)),
                  pl.BlockSpec((1, tc, LANES), lambda i, j: (i, j, 0)),
                  _full((LANES, GLA_KW)), _full((1, GLA_KW)), _full((1, GLA_DV)), _full((tc, tc))],
        out_specs=pl.BlockSpec((1, tc, GLA_W), lambda i, j: (i, j, 0)),
        out_shape=jax.ShapeDtypeStruct((b, t, GLA_W), BF16),
        scratch_shapes=[pltpu.VMEM((GLA_HEADS, GLA_DV, LANES), F32)],
        compiler_params=_cparams(("arbitrary", "arbitrary")),
        name="gla",
    )(qkvr, qkvr, qkvr, small, wg, bg.reshape(1, GLA_KW), on_gain.reshape(1, GLA_DV),
      jnp.asarray(tril.astype(np.float32)))


def odd_mixer(h, w_in, nsa_qn, nsa_kn, nsa_pos, nsa_cmp_w1, nsa_cmp_w2, gla_wg_up, gla_bg, gla_on, rel_bias):
    cuts = np.cumsum((0,) + OD_SIZES)
    col = lambda i: w_in[:, cuts[i]:cuts[i + 1]]
    dup = lambda a: jnp.concatenate([a[:, :NSA_DH], a[:, :NSA_DH], a[:, NSA_DH:], a[:, NSA_DH:]], axis=1)
    ep = _head_norm_epilogue(NSA_DH)
    bd = _block_diag_ones(NSA_W, NSA_DH)
    qg = (jnp.tile(nsa_qn, NSA_HEADS) * (NSA_DH ** -0.5 * LOG2E)).reshape(1, NSA_W)
    kg = jnp.tile(nsa_kn, NSA_HEADS).reshape(1, NSA_W)
    w_small = jnp.zeros((w_in.shape[0], LANES), F32)
    w_small = w_small.at[:, 0:24].set(col(7)).at[:, 24:40].set(col(11))
    nq, kcvc, ksw, vsw, small, qkvr = proj_multi(
        h, [(col(0), BF16, ep, (bd, qg)),
            (jnp.concatenate([col(1), col(2)], axis=1), F32, None, ()),
            (jnp.concatenate([dup(col(3)), dup(col(5))], axis=1), BF16, ep, (bd, kg)),
            (jnp.concatenate([dup(col(4)), dup(col(6))], axis=1), BF16, None, ()),
            (w_small, F32, None, ()),
            (jnp.concatenate([col(8), col(9), col(10), col(12)], axis=1), F32, None, ())], name="proj_odd")
    cmp_kv = nsa_compress(kcvc, nsa_pos, nsa_cmp_w1, nsa_cmp_w2, nsa_kn)
    o_cmp, sel = nsa_select(nq, cmp_kv, rel_bias)
    o_nsa = nsa_main(nq, ksw, vsw, sel, o_cmp, small, rel_bias)
    o_gla = gated_linear_attention(qkvr, small, gla_wg_up, gla_bg, gla_on)
    return o_nsa, o_gla


MOE_TM = 256
MOE_ROWS = 512


def _first_index(mask_val, idx, big, axis):
    return jnp.min(jnp.where(mask_val, idx, big), axis=axis, keepdims=True)


def _route_kernel(h_ref, rt_ref, b_ref, up_ref, eid_ref, rank_ref, w_ref, cnt_ref, run):
    tm = h_ref.shape[0]
    ne = N_EXPERTS
    gsz = ne // N_GROUPS

    @pl.when(pl.program_id(0) == 0)
    def _():
        run[...] = jnp.zeros_like(run)

    scores = _sigmoid(_dot_nt(rt_ref[...], h_ref[...], HI))
    biased = scores + b_ref[...]
    b3 = biased.reshape(N_GROUPS, gsz, tm)
    i3 = lax.broadcasted_iota(jnp.int32, (1, gsz, 1), 1).astype(F32)
    m1 = jnp.max(b3, axis=1, keepdims=True)
    f1 = _first_index(b3 == m1, i3, float(gsz), 1)
    m2 = jnp.max(jnp.where(i3 == f1, -jnp.inf, b3), axis=1, keepdims=True)
    gs = (m1 + m2).reshape(N_GROUPS, tm)
    gidx = lax.broadcasted_iota(jnp.int32, (N_GROUPS, 1), 0).astype(F32)
    gmask = jnp.zeros((N_GROUPS, tm), F32)
    for _ in range(TOPK_GROUPS):
        m = jnp.max(gs, axis=0, keepdims=True)
        pick = gidx == _first_index(gs == m, gidx, float(N_GROUPS), 0)
        gmask = jnp.where(pick, 1.0, gmask)
        gs = jnp.where(pick, -jnp.inf, gs)
    emask = jnp.broadcast_to(gmask.reshape(N_GROUPS, 1, tm), (N_GROUPS, gsz, tm)).reshape(ne, tm)
    work = jnp.where(emask > 0.5, biased, -jnp.inf)
    eidx = lax.broadcasted_iota(jnp.int32, (ne, 1), 0).astype(F32)
    picks, eids, ws = [], [], []
    for _ in range(TOP_K):
        m = jnp.max(work, axis=0, keepdims=True)
        first = _first_index(work == m, eidx, float(ne), 0)
        pick = eidx == first
        picks.append(pick)
        eids.append(first)
        ws.append(jnp.sum(jnp.where(pick, scores, 0.0), axis=0, keepdims=True))
        work = jnp.where(pick, -jnp.inf, work)
    wsum = ws[0]
    for k in range(1, TOP_K):
        wsum = wsum + ws[k]
    chosen = jnp.zeros((ne, tm), F32)
    for pick in picks:
        chosen = jnp.where(pick, 1.0, chosen)
    pos = run[...] + _dot(chosen.astype(BF16), up_ref[...])
    run[...] = run[...] + jnp.sum(chosen, axis=1, keepdims=True)
    cnt_ref[...] = run[...]
    row = lax.broadcasted_iota(jnp.int32, (8, 1), 0)
    eid_o = jnp.zeros((8, tm), F32)
    rank_o = jnp.zeros((8, tm), F32)
    w_o = jnp.zeros((LANES, tm), F32)
    rowl = lax.broadcasted_iota(jnp.int32, (LANES, 1), 0)
    for k in range(TOP_K):
        rk = jnp.sum(jnp.where(picks[k], pos, 0.0), axis=0, keepdims=True)
        eid_o = jnp.where(row == k, eids[k], eid_o)
        rank_o = jnp.where(row == k, rk, rank_o)
        w_o = jnp.where(rowl == k, ws[k] / wsum * ROUTE_SCALE, w_o)
    eid_ref[0] = eid_o.astype(jnp.int32)
    rank_ref[0] = rank_o.astype(jnp.int32)
    w_ref[...] = w_o.T


def moe_route(h2, router, e_bias, tm=MOE_TM):
    nt, d = h2.shape
    ne = N_EXPERTS
    up = jnp.asarray(np.triu(np.ones((tm, tm), np.float32), 1), dtype=BF16)
    nb = nt // tm
    return pl.pallas_call(
        _route_kernel,
        grid=(nb,),
        in_specs=[pl.BlockSpec((tm, d), lambda i: (i, 0)), _full((ne, d)), _full((ne, 1)), _full((tm, tm))],
        out_specs=[pl.BlockSpec((1, 8, tm), lambda i: (i, 0, 0)),
                   pl.BlockSpec((1, 8, tm), lambda i: (i, 0, 0)),
                   pl.BlockSpec((tm, LANES), lambda i: (i, 0)),
                   _full((ne, 1))],
        out_shape=[jax.ShapeDtypeStruct((nb, 8, tm), jnp.int32), jax.ShapeDtypeStruct((nb, 8, tm), jnp.int32),
                   jax.ShapeDtypeStruct((nt, LANES), F32), jax.ShapeDtypeStruct((ne, 1), F32)],
        scratch_shapes=[pltpu.VMEM((ne, 1), F32)],
        compiler_params=_cparams(("arbitrary",)),
        name="moe_route",
    )(h2, router.T, e_bias.reshape(ne, 1), up)


def _dispatch_kernel(dest_ref, h_ref, xs_ref, sem):
    tm = h_ref.shape[0]

    def copy(t, row):
        return pltpu.make_async_copy(h_ref.at[pl.ds(t, 1), :], xs_ref.at[pl.ds(row, 1), :], sem)

    def issue(t, _):
        for k in range(TOP_K):
            copy(t, dest_ref[0, t, k]).start(priority=k % 2)
        return 0

    def drain(t, _):
        for k in range(TOP_K):
            copy(0, 0).wait()
        return 0

    lax.fori_loop(0, tm, issue, 0, unroll=4)
    lax.fori_loop(0, tm, drain, 0, unroll=4)


def moe_dispatch(h2, dest, tm=MOE_TM):
    nt, d = h2.shape
    return pl.pallas_call(
        _dispatch_kernel,
        grid=(nt // tm,),
        in_specs=[pl.BlockSpec((1, tm, 8), lambda i: (i, 0, 0), memory_space=pltpu.SMEM),
                  pl.BlockSpec((tm, d), lambda i: (i, 0))],
        out_specs=pl.BlockSpec(memory_space=pl.ANY),
        scratch_shapes=[pltpu.SemaphoreType.DMA(())],
        out_shape=jax.ShapeDtypeStruct((nt * TOP_K, d), F32),
        compiler_params=_cparams(("arbitrary",)),
        name="moe_dispatch",
    )(dest, h2)


def _ffn_kernel(blk_ref, exp_ref, lo_ref, hi_ref, first_ref, valid_ref, x_ref, wg_ref, wu_ref, wd_ref, o_ref,
                wg_b, wu_b, wd_b):
    i = pl.program_id(0)
    rows = x_ref.shape[0]

    @pl.when((i == 0) | (exp_ref[i] != exp_ref[jnp.maximum(i - 1, 0)]))
    def _():
        wg_b[...] = wg_ref[0].astype(BF16)
        wu_b[...] = wu_ref[0].astype(BF16)
        wd_b[...] = wd_ref[0].astype(BF16)

    @pl.when(valid_ref[i] == 1)
    def _():
        x = x_ref[...].astype(BF16)
        a = _dot(x, wg_b[...])
        u = _dot(x, wu_b[...])
        y = _dot((_silu(a) * u).astype(BF16), wd_b[...])
        r = blk_ref[i] * rows + lax.broadcasted_iota(jnp.int32, (rows, 1), 0)
        y = jnp.where((r >= lo_ref[i]) & (r < hi_ref[i]), y, 0.0)

        @pl.when(first_ref[i] == 1)
        def _():
            o_ref[...] = y

        @pl.when(first_ref[i] == 0)
        def _():
            o_ref[...] = o_ref[...] + y


def _items_kernel(cnt_ref, starts_ref, blk_ref, exp_ref, lo_ref, hi_ref, first_ref, valid_ref, *, rows, n_items):
    shift = int(math.log2(rows))

    def expert(e, carry):
        start, n = carry
        c = cnt_ref[e]
        starts_ref[e] = start
        end = start + c
        first_blk = lax.shift_right_logical(start, shift)
        n_blk = jnp.where(c > 0, lax.shift_right_logical(jnp.maximum(end - 1, 0), shift) - first_blk + 1, 0)

        def item(k, n):
            b = first_blk + k
            lo = jnp.maximum(start, b * rows)
            blk_ref[n] = b
            exp_ref[n] = e
            lo_ref[n] = lo
            hi_ref[n] = jnp.minimum(end, (b + 1) * rows)
            first_ref[n] = (lo == b * rows).astype(jnp.int32)
            valid_ref[n] = 1
            return n + 1

        return end, lax.fori_loop(0, n_blk, item, n)

    _, total = lax.fori_loop(0, N_EXPERTS, expert, (jnp.int32(0), jnp.int32(0)))
    last = jnp.maximum(total - 1, 0)

    def fill(k, _):
        blk_ref[k] = blk_ref[last]
        exp_ref[k] = exp_ref[last]
        lo_ref[k] = 0
        hi_ref[k] = 0
        first_ref[k] = 0
        valid_ref[k] = 0
        return 0

    lax.fori_loop(total, n_items, fill, 0)


def _ffn_items(counts, n_rows, rows):
    n_items = n_rows // rows + N_EXPERTS - 1
    smem = pl.BlockSpec(memory_space=pltpu.SMEM)
    out = pl.pallas_call(
        functools.partial(_items_kernel, rows=rows, n_items=n_items),
        in_specs=[smem],
        out_specs=[smem] * 7,
        out_shape=[jax.ShapeDtypeStruct((N_EXPERTS,), jnp.int32)]
                  + [jax.ShapeDtypeStruct((n_items,), jnp.int32)] * 6,
        name="moe_items",
    )(counts)
    return out[0], tuple(out[1:])


def moe_ffn_sorted(xs, items, wg, wu, wd, layer, rows=MOE_ROWS):
    n_rows, d = xs.shape
    n_items = items[0].shape[0]
    de = wg.shape[-1]
    return pl.pallas_call(
        _ffn_kernel,
        grid_spec=pltpu.PrefetchScalarGridSpec(
            num_scalar_prefetch=6,
            grid=(n_items,),
            in_specs=[pl.BlockSpec((rows, d), lambda i, blk, e, *_: (blk[i], 0)),
                      pl.BlockSpec((None, 1, d, de), lambda i, blk, e, *_: (layer, e[i], 0, 0)),
                      pl.BlockSpec((None, 1, d, de), lambda i, blk, e, *_: (layer, e[i], 0, 0)),
                      pl.BlockSpec((None, 1, de, d), lambda i, blk, e, *_: (layer, e[i], 0, 0))],
            out_specs=pl.BlockSpec((rows, d), lambda i, blk, e, *_: (blk[i], 0)),
            scratch_shapes=[pltpu.VMEM((d, de), BF16), pltpu.VMEM((d, de), BF16), pltpu.VMEM((de, d), BF16)]),
        out_shape=jax.ShapeDtypeStruct((n_rows, d), F32),
        compiler_params=_cparams(("arbitrary",)),
        name="moe_ffn",
    )(*items, xs, wg, wu, wd)


def _combine_kernel(dest_ref, ys_ref, w_ref, h_ref, x_ref, g_ref, sg_ref, su_ref, sd_ref, *rest):
    if len(rest) == 3:
        nxt, (o_ref, buf, sem) = None, rest
    else:
        nxt, (o_ref, hn_ref, buf, sem) = rest[:3], rest[3:]
    tm = h_ref.shape[0]

    def copy(t, k, row):
        return pltpu.make_async_copy(ys_ref.at[pl.ds(row, 1), :], buf.at[k, pl.ds(t, 1), :], sem)

    def issue(t, _):
        for k in range(TOP_K):
            copy(t, k, dest_ref[0, t, k]).start(priority=k % 2)
        return 0

    def drain(t, _):
        for k in range(TOP_K):
            copy(0, 0, 0).wait()
        return 0

    lax.fori_loop(0, tm, issue, 0, unroll=4)
    hb = h_ref[...].astype(BF16)
    y = _dot((_silu(_dot(hb, sg_ref[...])) * _dot(hb, su_ref[...])).astype(BF16), sd_ref[...])
    lax.fori_loop(0, tm, drain, 0, unroll=4)
    w = w_ref[...]
    for k in range(TOP_K):
        y = y + w[:, k:k + 1] * buf[k]
    x_new = x_ref[...] + g_ref[0] * y
    o_ref[...] = x_new
    if nxt is not None:
        hn_ref[...] = _norm_mod(x_new, nxt[0][...], nxt[1][0], nxt[2][0]).astype(hn_ref.dtype)


def moe_combine(ys, dest, w, h2, x2, gate, sg, su, sd, seq, next_norm=None, tm=MOE_TM):
    nt, d = h2.shape
    ds_ = sg.shape[-1]
    per_b = seq // tm
    tile = lambda: pl.BlockSpec((tm, d), lambda i: (i, 0))
    batch = lambda: pl.BlockSpec((1, 1, d), lambda i: (i // per_b, 0, 0))
    nb = gate.shape[0]
    extra_in, extra_specs = [], []
    out_specs, out_shape = tile(), jax.ShapeDtypeStruct((nt, d), F32)
    if next_norm is not None:
        g_n, sc_n, sh_n = next_norm
        extra_in = [g_n.reshape(1, d), sc_n.reshape(nb, 1, d), sh_n.reshape(nb, 1, d)]
        extra_specs = [_full((1, d)), batch(), batch()]
        out_specs = [tile(), tile()]
        out_shape = [out_shape, jax.ShapeDtypeStruct((nt, d), BF16)]
    return pl.pallas_call(
        _combine_kernel,
        grid=(nt // tm,),
        in_specs=[pl.BlockSpec((1, tm, 8), lambda i: (i, 0, 0), memory_space=pltpu.SMEM),
                  pl.BlockSpec(memory_space=pl.ANY),
                  pl.BlockSpec((tm, LANES), lambda i: (i, 0)), tile(), tile(), batch(),
                  _full((d, ds_)), _full((d, ds_)), _full((ds_, d))] + extra_specs,
        out_specs=out_specs,
        scratch_shapes=[pltpu.VMEM((TOP_K, tm, d), F32), pltpu.SemaphoreType.DMA(())],
        out_shape=out_shape,
        compiler_params=_cparams(("arbitrary",)),
        name="moe_combine",
    )(dest, ys, w, h2, x2, gate, sg.astype(BF16), su.astype(BF16), sd.astype(BF16), *extra_in)


def moe_layer(x, h, gate, router, e_bias, wg, wu, wd, layer, sg, su, sd, next_norm=None):
    b, t, d = x.shape
    nt = b * t
    h2 = h.reshape(nt, d)
    eid, rank, w, counts = moe_route(h2, router, e_bias)
    starts, items = _ffn_items(counts.reshape(-1).astype(jnp.int32), nt * TOP_K, MOE_ROWS)
    hit = eid[..., None] == jnp.arange(N_EXPERTS, dtype=jnp.int32)
    dest = jnp.sum(jnp.where(hit, starts.astype(jnp.int32), 0), axis=-1) + rank
    dest = jnp.swapaxes(dest, 1, 2)
    xs = moe_dispatch(h2, dest)
    ys = moe_ffn_sorted(xs, items, wg, wu, wd, layer)
    out = moe_combine(ys, dest, w, h2, x.reshape(nt, d), gate.reshape(b, 1, d), sg, su, sd, t, next_norm)
    if next_norm is None:
        return out.reshape(b, t, d), None
    return out[0].reshape(b, t, d), out[1].reshape(b, t, d)


def kernel(x, c, ada_w, ada_b, norm_mix, norm_ffn, rel_bias, ev_w_in, ev_w_out, fox_fb, fox_qn, fox_kn, gdn_conv, gdn_a_log, gdn_dt_bias, gdn_on, od_w_in, od_w_out, nsa_qn, nsa_kn, nsa_pos, nsa_cmp_w1, nsa_cmp_w2, gla_wg_up, gla_bg, gla_on, moe_router, moe_bias, moe_wg, moe_wu, moe_wd, sh_wg, sh_wu, sh_wd):
    d = x.shape[-1]
    depth = ada_w.shape[0]
    mod = adaln(c, ada_w, ada_b)
    mods = [[mod[layer, :, i * d:(i + 1) * d] for i in range(6)] for layer in range(depth)]
    h = ln_mod(x, norm_mix[0], mods[0][1], mods[0][0], BF16)
    for layer in range(depth):
        sh1, sc1, g1, sh2, sc2, g2 = mods[layer]
        j = layer // 2
        if layer % 2 == 0:
            y1, y2 = even_mixer(h, ev_w_in[j], fox_fb[j], fox_qn[j], fox_kn[j], gdn_conv[j], gdn_a_log[j],
                                gdn_dt_bias[j], gdn_on[j])
            w_out = ev_w_out[j]
        else:
            y1, y2 = odd_mixer(h, od_w_in[j], nsa_qn[j], nsa_kn[j], nsa_pos[j], nsa_cmp_w1[j], nsa_cmp_w2[j],
                               gla_wg_up[j], gla_bg[j], gla_on[j], rel_bias)
            w_out = od_w_out[j]
        x, h_ffn = out_proj(y1, y2, w_out, x, g1, norm_ffn[layer], sc2, sh2)
        nxt = None if layer + 1 == depth else (norm_mix[layer + 1], mods[layer + 1][1], mods[layer + 1][0])
        x, h = moe_layer(x, h_ffn, g2, moe_router[layer], moe_bias[layer], moe_wg, moe_wu, moe_wd, layer,
                         sh_wg[layer], sh_wu[layer], sh_wd[layer], nxt)
    return x
```

```python
import functools
import math

import numpy as np
import jax
import jax.numpy as jnp
from jax import lax
from jax.experimental import pallas as pl
from jax.experimental.pallas import tpu as pltpu

F32 = jnp.float32
BF16 = jnp.bfloat16
HI = lax.Precision.HIGHEST

EPS = 1e-6
LOG2E = math.log2(math.e)
NEG = -1e30

FOX_HEADS, FOX_DH = 8, 64
GDN_HEADS, GDN_DH, GDN_CONV = 4, 128, 4
NSA_HEADS, NSA_KV_HEADS, NSA_DH = 8, 2, 64
NSA_GROUP = NSA_HEADS // NSA_KV_HEADS
CMP_LEN, CMP_STRIDE, CMP_HIDDEN = 32, 16, 256
SLC_LEN, SLC_TOPK, WINDOW = 64, 16, 512
GLA_HEADS, GLA_DK, GLA_DV, GLA_GATE_RANK, GLA_TAU, GLA_CHUNK = 4, 64, 128, 16, 16.0, 64
REL_BUCKETS, REL_MAX_DIST = 32, 128
N_EXPERTS, TOP_K, D_EXPERT, D_SHARED = 64, 6, 256, 256
N_GROUPS, TOPK_GROUPS, ROUTE_SCALE = 8, 4, 2.5

FOX_W = FOX_HEADS * FOX_DH
GDN_W = GDN_HEADS * GDN_DH
NSA_W = NSA_HEADS * NSA_DH
NSA_KV_W = NSA_KV_HEADS * NSA_DH
GLA_KW = GLA_HEADS * GLA_DK
GLA_W = GLA_HEADS * GLA_DV
EV_SIZES = (FOX_W, FOX_W, FOX_W, FOX_HEADS, 3 * GDN_W, GDN_HEADS, GDN_HEADS, GDN_W)
OD_SIZES = (NSA_W,) + (NSA_KV_W,) * 6 + (3 * NSA_HEADS, GLA_KW, GLA_KW, GLA_W, GLA_GATE_RANK, GLA_W)

LANES = 128
SUP_BLOCKS = 64
VMEM_LIMIT = 56 * 1024 * 1024


def _cparams(sem):
    return pltpu.CompilerParams(dimension_semantics=sem, vmem_limit_bytes=VMEM_LIMIT)


def _full(shape):
    n = len(shape)
    return pl.BlockSpec(shape, lambda *_: (0,) * n)


def _dot(a, b):
    return jnp.dot(a, b, preferred_element_type=F32)


def _dot_hi(a, b):
    return jnp.dot(a, b, precision=HI, preferred_element_type=F32)


def _dot_nt(a, b, precision=None):
    return lax.dot_general(a, b, (((1,), (1,)), ((), ())), precision=precision, preferred_element_type=F32)


def _dot_tn(a, b, precision=None):
    return lax.dot_general(a, b, (((0,), (0,)), ((), ())), precision=precision, preferred_element_type=F32)


def _sigmoid(x):
    return 1.0 / (1.0 + jnp.exp(-x))


def _silu(x):
    return x * _sigmoid(x)


def _softplus(x):
    return jnp.maximum(x, 0.0) + jnp.log(1.0 + jnp.exp(-jnp.abs(x)))


def _log_sigmoid(x):
    return -_softplus(-x)


def _adaln_kernel(c_ref, w_ref, b_ref, o_ref):
    c = c_ref[...]
    o_ref[0] = _dot_hi(_silu(c), w_ref[0]) + b_ref[0]


def adaln(c, ada_w, ada_b):
    depth, d, n = ada_w.shape
    b = c.shape[0]
    cp = jnp.zeros((8, d), F32).at[:b].set(c)
    tn = 1536
    out = pl.pallas_call(
        _adaln_kernel,
        grid=(depth, n // tn),
        in_specs=[_full((8, d)),
                  pl.BlockSpec((1, d, tn), lambda l, j: (l, 0, j)),
                  pl.BlockSpec((1, 1, tn), lambda l, j: (l, 0, j))],
        out_specs=pl.BlockSpec((1, 8, tn), lambda l, j: (l, 0, j)),
        out_shape=jax.ShapeDtypeStruct((depth, 8, n), F32),
        compiler_params=_cparams(("arbitrary", "arbitrary")),
        name="adaln",
    )(cp, ada_w, ada_b.reshape(depth, 1, n))
    return out[:, :b]


def _ln_kernel(x_ref, g_ref, sc_ref, sh_ref, o_ref):
    x = x_ref[0]
    y = x * lax.rsqrt(jnp.mean(x * x, axis=-1, keepdims=True) + EPS) * g_ref[...]
    o_ref[0] = (y * (1.0 + sc_ref[0]) + sh_ref[0]).astype(o_ref.dtype)


def ln_mod(x, g, sc, sh, out_dtype, tm=512):
    b, t, d = x.shape
    return pl.pallas_call(
        _ln_kernel,
        grid=(b, t // tm),
        in_specs=[pl.BlockSpec((1, tm, d), lambda i, j: (i, j, 0)),
                  _full((1, d)),
                  pl.BlockSpec((1, 1, d), lambda i, j: (i, 0, 0)),
                  pl.BlockSpec((1, 1, d), lambda i, j: (i, 0, 0))],
        out_specs=pl.BlockSpec((1, tm, d), lambda i, j: (i, j, 0)),
        out_shape=jax.ShapeDtypeStruct((b, t, d), out_dtype),
        compiler_params=_cparams(("arbitrary", "arbitrary")),
        name="ln_mod",
    )(x, g.reshape(1, d), sc.reshape(b, 1, d), sh.reshape(b, 1, d))


def proj_multi(h, groups, tm=512, name="proj"):
    b, t, d = h.shape
    widths = [g[0].shape[1] for g in groups]
    starts = np.cumsum([0] + widths)
    w_cat = jnp.concatenate([g[0] for g in groups], axis=1).astype(BF16)
    extras = [e for g in groups for e in g[3]]
    n_ex = [len(g[3]) for g in groups]
    n_out = len(groups)

    def kern(h_ref, w_ref, *rest):
        ex_refs = rest[:len(extras)]
        o_refs = rest[len(extras):]
        y = _dot(h_ref[0], w_ref[...])
        pos = 0
        for gi, (_, out_dtype, epilogue, _) in enumerate(groups):
            yg = y[:, starts[gi]:starts[gi + 1]]
            if epilogue is not None:
                yg = epilogue(yg, *[e[...] for e in ex_refs[pos:pos + n_ex[gi]]])
            pos += n_ex[gi]
            o_refs[gi][0] = yg.astype(out_dtype)

    return pl.pallas_call(
        kern,
        grid=(b, t // tm),
        in_specs=[pl.BlockSpec((1, tm, d), lambda i, j: (i, j, 0)), _full((d, int(starts[-1])))]
                 + [_full(e.shape) for e in extras],
        out_specs=[pl.BlockSpec((1, tm, n), lambda i, j: (i, j, 0)) for n in widths],
        out_shape=[jax.ShapeDtypeStruct((b, t, n), g[1]) for n, g in zip(widths, groups)],
        compiler_params=_cparams(("arbitrary", "arbitrary")),
        name=name,
    )(h, w_cat, *extras)


def _head_norm_epilogue(dh):
    inv = 1.0 / dh

    def ep(y, bd, gain):
        ssq = _dot((y * y).astype(BF16), bd)
        return y * lax.rsqrt(ssq * inv + EPS) * gain

    return ep


def _block_diag_ones(n, dh):
    i = np.arange(n) // dh
    return jnp.asarray((i[:, None] == i[None, :]).astype(np.float32), dtype=BF16)


def _norm_mod(x, gain, sc, sh):
    y = x * lax.rsqrt(jnp.mean(x * x, axis=-1, keepdims=True) + EPS) * gain
    return y * (1.0 + sc) + sh


def _outproj_kernel(y1_ref, y2_ref, wa_ref, wb_ref, x_ref, g_ref, n_ref, sc_ref, sh_ref, o_ref, h_ref):
    y = _dot(y1_ref[0], wa_ref[...]) + _dot(y2_ref[0], wb_ref[...])
    x_new = x_ref[0] + g_ref[0] * y
    o_ref[0] = x_new
    h_ref[0] = _norm_mod(x_new, n_ref[...], sc_ref[0], sh_ref[0]).astype(h_ref.dtype)


def out_proj(y1, y2, w_out, x, gate, g_norm, sc, sh, tm=512):
    b, t, d = x.shape
    n1, n2 = y1.shape[-1], y2.shape[-1]
    wa = w_out[:n1].astype(BF16)
    wb = w_out[n1:].astype(BF16)
    row = lambda n: pl.BlockSpec((1, tm, n), lambda i, j: (i, j, 0))
    per_b = lambda: pl.BlockSpec((1, 1, d), lambda i, j: (i, 0, 0))
    return pl.pallas_call(
        _outproj_kernel,
        grid=(b, t // tm),
        in_specs=[row(n1), row(n2), _full((n1, d)), _full((n2, d)), row(d), per_b(), _full((1, d)), per_b(), per_b()],
        out_specs=[row(d), row(d)],
        out_shape=[jax.ShapeDtypeStruct((b, t, d), F32), jax.ShapeDtypeStruct((b, t, d), F32)],
        compiler_params=_cparams(("arbitrary", "arbitrary")),
        name="out_proj",
    )(y1, y2, wa, wb, x, gate.reshape(b, 1, d), g_norm.reshape(1, d), sc.reshape(b, 1, d), sh.reshape(b, 1, d))


def _decay_kernel(s_ref, fb_ref, tril_ref, place_ref, o_ref, carry):
    @pl.when(pl.program_id(1) == 0)
    def _():
        carry[...] = jnp.zeros_like(carry)

    tm = s_ref.shape[1]
    lf = _log_sigmoid(s_ref[0] + fb_ref[...])
    cum = _dot_hi(tril_ref[...], lf) + carry[...]
    carry[...] = cum[tm - 1:tm, :]
    x = cum * LOG2E
    hi = x.astype(BF16)
    r1 = x - hi.astype(F32)
    mid = r1.astype(BF16)
    low = (r1 - mid.astype(F32)).astype(BF16)
    o_ref[0] = _dot(jnp.concatenate([hi, mid, low], axis=1), place_ref[...]).astype(o_ref.dtype)


def fox_decay(small, fox_fb, tm=512):
    b, t, _ = small.shape
    fb = jnp.zeros((1, LANES), F32).at[0, :FOX_HEADS].set(fox_fb)
    tril = jnp.asarray(np.tril(np.ones((tm, tm), np.float32)))
    place = np.zeros((3 * LANES, FOX_W), np.float32)
    for h in range(FOX_HEADS):
        for j in range(3):
            place[j * LANES + h, (h // 2) * LANES + (FOX_DH if h % 2 == 0 else 0) + j] = 1.0
    return pl.pallas_call(
        _decay_kernel,
        grid=(b, t // tm),
        in_specs=[pl.BlockSpec((1, tm, LANES), lambda i, j: (i, j, 0)), _full((1, LANES)), _full((tm, tm)),
                  _full((3 * LANES, FOX_W))],
        out_specs=pl.BlockSpec((1, tm, FOX_W), lambda i, j: (i, j, 0)),
        out_shape=jax.ShapeDtypeStruct((b, t, FOX_W), BF16),
        scratch_shapes=[pltpu.VMEM((1, LANES), F32)],
        compiler_params=_cparams(("arbitrary", "arbitrary")),
        name="fox_decay",
    )(small, fb, tril, jnp.asarray(place, dtype=BF16))


def _fox_kernel(q_ref, k_ref, v_ref, f_ref, o_ref, *, tq, wide):
    i = pl.program_id(2)
    lane = lax.broadcasted_iota(jnp.int32, (1, LANES), 1)
    lo = lane < FOX_DH
    coef = jnp.where((lane & (FOX_DH - 1)) < 3, -1.0, 0.0).astype(BF16)
    q = q_ref[0]
    qs = (jnp.where(lo, q, coef), jnp.where(lo, coef, q))
    causal = (lax.broadcasted_iota(jnp.int32, (tq, tq), 1) <= lax.broadcasted_iota(jnp.int32, (tq, tq), 0))
    one = jnp.ones((1, LANES), BF16)

    def step(j, carry, tw, diag=False):
        start = pl.multiple_of(j * tw, tw)
        kt = k_ref[0, pl.ds(start, tw), :]
        ft = f_ref[0, pl.ds(start, tw), :]
        vt = v_ref[0, pl.ds(start, tw), :]
        s_pair = (_dot_nt(qs[0], jnp.where(lo, kt, ft)), _dot_nt(qs[1], jnp.where(lo, ft, kt)))
        vs = (jnp.where(lo, vt, one), jnp.where(lo, one, vt))
        new = []
        for hh in range(2):
            m, acc = carry[hh]
            s = s_pair[hh]
            if diag:
                s = jnp.where(causal, s, NEG)
            m_new = jnp.maximum(m, jnp.max(s, axis=1, keepdims=True))
            p = jnp.exp2(s - m_new)
            acc = jnp.exp2(m - m_new) * acc + _dot(p.astype(BF16), vs[hh])
            new.append((m_new, acc))
        return tuple(new)

    carry = tuple((jnp.full((tq, 1), NEG, F32), jnp.zeros((tq, LANES), F32)) for _ in range(2))
    n_wide = i // wide
    carry = lax.fori_loop(0, n_wide, lambda j, c: step(j, c, wide * tq), carry)
    done = n_wide * wide
    part = wide // 2
    while part >= 1:
        carry = lax.cond((i & part) != 0, lambda c, d=done, w=part: step(d // w, c, w * tq), lambda c: c, carry)
        done = done + (i & part)
        part //= 2
    carry = step(i, carry, tq, diag=True)
    acc = jnp.where(lo, carry[0][1], carry[1][1])
    den = jnp.where(lo, carry[1][1], carry[0][1])
    o_ref[0] = (acc / pltpu.roll(den, FOX_DH, 1)).astype(o_ref.dtype)


def fox_attention(q, k, v, feat, tq=512, wide=4):
    b, t, w = q.shape
    npair = w // LANES
    nt = t // tq
    whole = lambda: pl.BlockSpec((1, t, LANES), lambda bi, p, i: (bi, 0, p))
    return pl.pallas_call(
        functools.partial(_fox_kernel, tq=tq, wide=wide),
        grid=(b, npair, nt),
        in_specs=[pl.BlockSpec((1, tq, LANES), lambda bi, p, i: (bi, i, p)), whole(), whole(), whole()],
        out_specs=pl.BlockSpec((1, tq, LANES), lambda bi, p, i: (bi, i, p)),
        out_shape=jax.ShapeDtypeStruct((b, t, w), BF16),
        compiler_params=_cparams(("arbitrary", "arbitrary", "arbitrary")),
        name="fox_attn",
    )(q, k, v, feat)


def _mm(a, b):
    return _dot(a.astype(BF16), b.astype(BF16))


def _tril_solve(a, rhs, ri, ci):
    n = a[0].shape[0]
    both = lambda f, x, y: [f(p, q) for p, q in zip(x, y)]
    eye = (ri == ci).astype(F32)
    same = lambda b: (lax.shift_right_logical(ri, int(math.log2(b)))
                      == lax.shift_right_logical(ci, int(math.log2(b))))
    base = 16
    d = [jnp.where(same(base), p, 0.0) for p in a]
    d2 = both(_mm, d, d)
    d4 = both(_mm, d2, d2)
    r1 = [eye - p + p2 - t for p, p2, t in zip(d, d2, both(_mm, d, d2))]
    d8 = both(_mm, d4, d4)
    r2 = [eye + p4 + p8 + t for p4, p8, t in zip(d4, d8, both(_mm, d4, d8))]
    t = both(_mm, r1, r2)
    b = base
    while b < n:
        join = same(2 * b) & jnp.logical_not(same(b))
        low = [jnp.where(join, p, 0.0) for p in a]
        t = [p - q for p, q in zip(t, both(_mm, both(_mm, t, low), t))]
        b *= 2
    return both(_mm, t, rhs)


GDN_BLOCK = 128


def _gdn_kernel(x_ref, sm_ref, z_ref, cw_ref, ega_ref, egb_ref, alog_ref, dtb_ref, on_ref, tril_ref,
                o_ref, s_scr, prev_scr, *, tc):
    c = GDN_BLOCK
    w = GDN_W

    @pl.when(pl.program_id(1) == 0)
    def _():
        s_scr[...] = jnp.zeros_like(s_scr)
        prev_scr[...] = jnp.zeros_like(prev_scr)

    x = x_ref[0]
    prev = prev_scr[...]
    row8 = lax.broadcasted_iota(jnp.int32, (8, 1), 0)
    acc = x * cw_ref[GDN_CONV - 1:GDN_CONV, :]
    for s in range(1, GDN_CONV):
        rolled = pltpu.roll(x, s, 0)
        head = jnp.where(row8 < s, pltpu.roll(prev, s, 0), rolled[0:8])
        shifted = jnp.concatenate([head, rolled[8:]], axis=0)
        acc = acc + shifted * cw_ref[GDN_CONV - 1 - s:GDN_CONV - s, :]
    prev_scr[...] = x[tc - 8:tc]
    xc = _silu(acc)

    sm = sm_ref[0]
    g_raw = _dot_hi(sm, ega_ref[...])
    b_raw = _dot_hi(sm, egb_ref[...])
    g = -jnp.exp(alog_ref[...]) * _softplus(g_raw + dtb_ref[...])
    beta_all = _sigmoid(b_raw)
    gc_all = _dot_hi(tril_ref[...], g)

    ri = lax.broadcasted_iota(jnp.int32, (c, c), 0)
    ci = lax.broadcasted_iota(jnp.int32, (c, c), 1)
    causal = ci <= ri
    strict = ci < ri

    nblk = tc // c
    a_l, attn_l, rhs_l, qd_l, kd_l, egl_l = [], [], [], [], [], []
    for h in range(GDN_HEADS):
        ln = slice(h * GDN_DH, (h + 1) * GDN_DH)
        qh = xc[:, h * GDN_DH:(h + 1) * GDN_DH]
        kh = xc[:, w + h * GDN_DH:w + (h + 1) * GDN_DH]
        qh = qh * lax.rsqrt(jnp.sum(qh * qh, axis=-1, keepdims=True) + EPS) * (GDN_DH ** -0.5)
        kh = kh * lax.rsqrt(jnp.sum(kh * kh, axis=-1, keepdims=True) + EPS)
        vh = xc[:, 2 * w + h * GDN_DH:2 * w + (h + 1) * GDN_DH]
        gch = gc_all[:, ln]
        gct = gch.T
        egc = jnp.exp(gch)
        bh = beta_all[:, ln]
        for n in range(nblk):
            sl = slice(n * c, (n + 1) * c)
            q, k, v, gc, be = qh[sl], kh[sl], vh[sl], gch[sl], bh[sl]
            decay = jnp.exp(jnp.where(causal, gc - gct[:, sl], NEG))
            kb = k * be
            kk = _dot_nt(jnp.concatenate([kb, q], axis=0).astype(BF16), k.astype(BF16))
            a_l.append(jnp.where(strict, kk[:c] * decay, 0.0))
            attn_l.append(jnp.where(causal, kk[c:] * decay, 0.0))
            rhs_l.append(jnp.concatenate([v * be, kb * egc[sl]], axis=1))
            gl = gc[c - 1:c, :]
            qd_l.append(q * egc[sl])
            kd_l.append(k * jnp.exp(gl - gc))
            egl_l.append(jnp.exp(gl))
    uw_l = _tril_solve(a_l, rhs_l, ri, ci)

    states = [s_scr[h] for h in range(GDN_HEADS)]
    for n in range(nblk):
        sl = slice(n * c, (n + 1) * c)
        idx = [h * nblk + n for h in range(GDN_HEADS)]
        ws = [_mm(jnp.concatenate([uw_l[i][:, GDN_DH:], qd_l[i]], axis=0), states[h])
              for h, i in enumerate(idx)]
        v_new = [uw_l[i][:, :GDN_DH] - ws[h][:c] for h, i in enumerate(idx)]
        o = [ws[h][c:] + _mm(attn_l[i], v_new[h]) for h, i in enumerate(idx)]
        states = [states[h] * egl_l[i] + _dot_tn(kd_l[i].astype(BF16), v_new[h].astype(BF16))
                  for h, i in enumerate(idx)]
        for h in range(GDN_HEADS):
            ln = slice(h * GDN_DH, (h + 1) * GDN_DH)
            on = o[h] * lax.rsqrt(jnp.mean(o[h] * o[h], axis=-1, keepdims=True) + EPS) * on_ref[...]
            o_ref[0, sl, ln] = (on * _silu(z_ref[0, sl, ln])).astype(o_ref.dtype)
    for h in range(GDN_HEADS):
        s_scr[h] = states[h]


def gated_delta_net(x, small, z, conv_w, a_log, dt_bias, on_gain, tc=512):
    b, t, _ = x.shape
    w = GDN_W
    ega = np.zeros((LANES, w), np.float32)
    egb = np.zeros((LANES, w), np.float32)
    for h in range(GDN_HEADS):
        ega[FOX_HEADS + h, h * GDN_DH:(h + 1) * GDN_DH] = 1.0
        egb[FOX_HEADS + GDN_HEADS + h, h * GDN_DH:(h + 1) * GDN_DH] = 1.0
    alog = jnp.repeat(a_log, GDN_DH).reshape(1, w)
    dtb = jnp.repeat(dt_bias, GDN_DH).reshape(1, w)
    idx = np.arange(tc)
    tril = ((idx[:, None] >= idx[None, :]) & (idx[:, None] // GDN_BLOCK == idx[None, :] // GDN_BLOCK))
    row = lambda n: pl.BlockSpec((1, tc, n), lambda i, j: (i, j, 0))
    return pl.pallas_call(
        functools.partial(_gdn_kernel, tc=tc),
        grid=(b, t // tc),
        in_specs=[row(3 * w), row(LANES), row(w), _full((GDN_CONV, 3 * w)), _full((LANES, w)), _full((LANES, w)),
                  _full((1, w)), _full((1, w)), _full((1, GDN_DH)), _full((tc, tc))],
        out_specs=row(w),
        out_shape=jax.ShapeDtypeStruct((b, t, w), BF16),
        scratch_shapes=[pltpu.VMEM((GDN_HEADS, GDN_DH, GDN_DH), F32), pltpu.VMEM((8, 3 * w), F32)],
        compiler_params=_cparams(("arbitrary", "arbitrary")),
        name="gdn",
    )(x, small, z, conv_w, jnp.asarray(ega), jnp.asarray(egb), alog, dtb, on_gain.reshape(1, GDN_DH),
      jnp.asarray(tril.astype(np.float32)))


def even_mixer(h, w_in, fox_fb, fox_qn, fox_kn, gdn_conv, gdn_a_log, gdn_dt_bias, gdn_on):
    cuts = np.cumsum((0,) + EV_SIZES)
    col = lambda i: w_in[:, cuts[i]:cuts[i + 1]]
    bd = _block_diag_ones(FOX_W, FOX_DH)
    ep = _head_norm_epilogue(FOX_DH)
    qg = (jnp.tile(fox_qn, FOX_HEADS) * (FOX_DH ** -0.5 * LOG2E)).reshape(1, FOX_W)
    kg = jnp.tile(fox_kn, FOX_HEADS).reshape(1, FOX_W)
    w_small = jnp.zeros((w_in.shape[0], LANES), F32)
    w_small = w_small.at[:, 0:8].set(col(3)).at[:, 8:12].set(col(5)).at[:, 12:16].set(col(6))
    fq, fk, fv, small, gqkv, gz = proj_multi(
        h, [(col(0), BF16, ep, (bd, qg)), (col(1), BF16, ep, (bd, kg)), (col(2), BF16, None, ()),
            (w_small, F32, None, ()), (col(4), F32, None, ()), (col(7), F32, None, ())], name="proj_even")
    feat = fox_decay(small, fox_fb)
    o_fox = fox_attention(fq, fk, fv, feat)
    o_gdn = gated_delta_net(gqkv, small, gz, gdn_conv, gdn_a_log, gdn_dt_bias, gdn_on)
    return o_fox, o_gdn


def _t5_bucket_np(dist):
    n = np.maximum(dist, 0)
    exact = REL_BUCKETS // 2
    nf = np.maximum(n, 1).astype(np.float32)
    large = exact + (np.log(nf / np.float32(exact)) / np.float32(math.log(REL_MAX_DIST / exact))
                     * np.float32(REL_BUCKETS - exact)).astype(np.int32)
    large = np.minimum(large, REL_BUCKETS - 1)
    return np.where(n < exact, n, large)


def _bias_kernel(tbl_ref, bucket_ref, o_ref):
    h = pl.program_id(0)
    bucket = bucket_ref[...]
    acc = jnp.full(bucket.shape, NEG, F32)
    for b in range(REL_BUCKETS):
        acc = jnp.where(bucket == b, tbl_ref[b, h], acc)
    o_ref[0] = acc


def _bias_table(rel_bias, dist, valid):
    shifted = (rel_bias - rel_bias[REL_BUCKETS - 1:REL_BUCKETS]) * LOG2E
    bucket = np.where(valid, _t5_bucket_np(dist), -1).astype(np.int32)
    rows, cols = int(np.prod(bucket.shape[:-1])), bucket.shape[-1]
    nh = rel_bias.shape[1]
    tb = pl.pallas_call(
        _bias_kernel,
        grid=(nh,),
        in_specs=[pl.BlockSpec(memory_space=pltpu.SMEM), _full((rows, cols))],
        out_specs=pl.BlockSpec((1, rows, cols), lambda h: (h, 0, 0)),
        out_shape=jax.ShapeDtypeStruct((nh, rows, cols), F32),
        compiler_params=_cparams(("arbitrary",)),
        name="t5_bias",
    )(shifted, jnp.asarray(bucket.reshape(rows, cols)))
    return tb.reshape((nh,) + bucket.shape)


def _cmp_kernel(r_ref, pos_ref, w1_ref, w2_ref, kn_ref, o_ref):
    m = r_ref.shape[3]
    half = r_ref.shape[4]
    r = r_ref[0, 0, 0].astype(BF16)
    a = _dot(r, w1_ref[0, :half, :])
    bm = _dot(r, w1_ref[0, half:, :])
    c = _dot(pos_ref[0].astype(BF16), w1_ref[0])
    hid = a + pltpu.roll(bm, m - 1, 0) + c[0:1, :]
    out = _dot(_silu(hid).astype(BF16), w2_ref[0])
    normed = out * lax.rsqrt(jnp.mean(out * out, axis=-1, keepdims=True) + EPS) * kn_ref[...]
    o_ref[0, 0, 0] = jnp.where(pl.program_id(0) == 0, normed, out).astype(o_ref.dtype)


def nsa_compress(kcvc, pos, w1, w2, kn):
    b, t, _ = kcvc.shape
    m = t // CMP_STRIDE
    half = CMP_STRIDE * NSA_DH
    r = kcvc.reshape(b, m, CMP_STRIDE, 2, NSA_KV_HEADS, NSA_DH).transpose(3, 0, 4, 1, 2, 5).reshape(2, b, 2, m, half)
    posf = jnp.zeros((2, 8, 2 * half), F32).at[:, 0].set(pos.reshape(2, 2 * half))
    w2d = jnp.concatenate([w2, w2], axis=-1).astype(BF16)
    knd = jnp.tile(kn, 2).reshape(1, LANES)
    return pl.pallas_call(
        _cmp_kernel,
        grid=(2, b, NSA_KV_HEADS),
        in_specs=[pl.BlockSpec((1, 1, 1, m, half), lambda s, i, k: (s, i, k, 0, 0)),
                  pl.BlockSpec((1, 8, 2 * half), lambda s, i, k: (s, 0, 0)),
                  pl.BlockSpec((1, 2 * half, CMP_HIDDEN), lambda s, i, k: (s, 0, 0)),
                  pl.BlockSpec((1, CMP_HIDDEN, LANES), lambda s, i, k: (s, 0, 0)),
                  _full((1, LANES))],
        out_specs=pl.BlockSpec((1, 1, 1, m, LANES), lambda s, i, k: (s, i, k, 0, 0)),
        out_shape=jax.ShapeDtypeStruct((2, b, NSA_KV_HEADS, m, LANES), BF16),
        compiler_params=_cparams(("arbitrary", "arbitrary", "arbitrary")),
        name="nsa_compress",
    )(r, posf, w1.astype(BF16), w2d, knd)


def _dot_split(a, b):
    hi = a.astype(BF16)
    lo = (a - hi.astype(F32)).astype(BF16)
    return _dot(hi, b) + _dot(lo, b)


def _head_q(q_ref, hh, lo):
    blk = q_ref[0, :, (hh // 2) * LANES:(hh // 2 + 1) * LANES]
    keep = lo if hh % 2 == 0 else jnp.logical_not(lo)
    return jnp.where(keep, blk, jnp.zeros_like(blk))


def _pair_heads(o, lo):
    return jnp.concatenate([jnp.where(lo, o[0], o[1]), jnp.where(lo, o[2], o[3])], axis=1)


def _nsa_sel_kernel(q_ref, kc_ref, vc_ref, ov_ref, bt_ref, o_ref, sel_ref, *, tq, nband, n_slc):
    i = pl.program_id(2)
    ncp = kc_ref.shape[3]
    nsp = ov_ref.shape[1]
    per = tq // CMP_STRIDE
    var = jnp.minimum(i, 1)
    bs = pl.multiple_of(per * jnp.maximum(i - 1, 0), per)
    lo = lax.broadcasted_iota(jnp.int32, (1, LANES), 1) < NSA_DH
    kc = kc_ref[0, 0, 0]
    vc = vc_ref[0, 0, 0]
    kcb = kc_ref[0, 0, 0, pl.ds(bs, nband), :]
    vcb = vc_ref[0, 0, 0, pl.ds(bs, nband), :]
    far_ok = lax.broadcasted_iota(jnp.int32, (1, ncp), 1) < per * (i - 1)
    hs = range(NSA_GROUP)
    qh = [_head_q(q_ref, hh, lo) for hh in hs]
    s_far = [jnp.where(far_ok, _dot_nt(q, kc), NEG) for q in qh]
    s_band = [_dot_nt(qh[hh], kcb) + bt_ref[var, hh] for hh in hs]
    m = [jnp.maximum(jnp.max(a, axis=1, keepdims=True), jnp.max(b, axis=1, keepdims=True))
         for a, b in zip(s_far, s_band)]
    m = [jnp.where(x < 0.5 * NEG, 0.0, x) for x in m]
    p_far = [jnp.exp2(a - x) for a, x in zip(s_far, m)]
    p_band = [jnp.exp2(b - x) for b, x in zip(s_band, m)]
    l = [jnp.sum(a, axis=1, keepdims=True) + jnp.sum(b, axis=1, keepdims=True) for a, b in zip(p_far, p_band)]
    inv = [1.0 / jnp.where(x == 0.0, 1.0, x) for x in l]
    outs = [(_dot(a.astype(BF16), vc) + _dot(b.astype(BF16), vcb)) * x for a, b, x in zip(p_far, p_band, inv)]
    ps_far = p_far[0] * inv[0]
    ps_band = p_band[0] * inv[0]
    for hh in range(1, NSA_GROUP):
        ps_far = ps_far + p_far[hh] * inv[hh]
        ps_band = ps_band + p_band[hh] * inv[hh]
    o_ref[0] = _pair_heads(outs, lo).astype(o_ref.dtype)

    imp = _dot_split(ps_far, ov_ref[...]) + _dot_split(ps_band, ov_ref[pl.ds(bs, nband), :])
    blk = lax.broadcasted_iota(jnp.int32, (1, nsp), 1)
    blk_f = blk.astype(F32)
    qpos = i * tq + lax.broadcasted_iota(jnp.int32, (tq, 1), 0)
    cur = lax.shift_right_logical(qpos, int(math.log2(SLC_LEN)))
    forced = (blk == 0) | (blk == cur) | (blk == cur - 1)
    work = jnp.where(forced, -jnp.inf, jnp.where(blk <= cur, imp, NEG))
    work = jnp.where(blk < n_slc, work, -jnp.inf)
    ngrp = 4
    rg = tq // ngrp
    works = [work[r * rg:(r + 1) * rg] for r in range(ngrp)]
    sels = [jnp.where(forced[r * rg:(r + 1) * rg], 1.0, jnp.zeros((rg, nsp), F32)) for r in range(ngrp)]
    for _ in range(max(min(SLC_TOPK, n_slc) - 3, 0)):
        ms = [jnp.max(w, axis=1, keepdims=True) for w in works]
        firsts = [jnp.min(jnp.where(w == m, blk_f, float(nsp)), axis=1, keepdims=True) for w, m in zip(works, ms)]
        picks = [blk_f == f for f in firsts]
        sels = [jnp.where(p, 1.0, s) for p, s in zip(picks, sels)]
        works = [jnp.where(p, -jnp.inf, w) for p, w in zip(picks, works)]
    sel = jnp.concatenate(sels, axis=0)
    for sup in range(sel_ref.shape[2]):
        col = sel[:, (sup // 2) * LANES:(sup // 2 + 1) * LANES]
        if sup % 2 == 0:
            col = pltpu.roll(col, SUP_BLOCKS, 1)
        sel_ref[0, 0, sup] = jnp.where(lo, 0.0, jnp.where(col > 0.5, 0.0, NEG)).astype(sel_ref.dtype)


def nsa_select(q, cmp_kv, rel_bias, tq=512):
    b, t, _ = q.shape
    ncp = t // CMP_STRIDE
    n_cmp = ncp - 1
    n_slc = t // SLC_LEN
    nsp = max(LANES, n_slc)
    nsup = max(1, n_slc // SUP_BLOCKS)
    per = tq // CMP_STRIDE
    nband = 2 * per
    n = np.arange(ncp)[:, None]
    s = np.arange(nsp)[None, :]
    ov = ((CMP_STRIDE * n < SLC_LEN * s + SLC_LEN) & (CMP_STRIDE * n + CMP_LEN > SLC_LEN * s)
          & (n < n_cmp) & (s < n_slc)).astype(np.float32)
    qi = np.arange(tq)[:, None]
    nj = np.arange(nband)[None, :]
    end = CMP_STRIDE * nj + CMP_LEN - 1
    dist = np.stack([qi - end, tq + qi - end])
    bt = _bias_table(rel_bias, dist, dist >= 0)
    bt = bt.reshape(NSA_KV_HEADS, NSA_GROUP, 2, tq, nband).transpose(0, 2, 1, 3, 4)
    bt = bt.reshape(NSA_KV_HEADS * 2, NSA_GROUP, tq, nband)
    gw = NSA_GROUP * NSA_DH
    return pl.pallas_call(
        functools.partial(_nsa_sel_kernel, tq=tq, nband=nband, n_slc=n_slc),
        grid=(b, NSA_KV_HEADS, t // tq),
        in_specs=[pl.BlockSpec((1, tq, gw), lambda bi, k, i: (bi, i, k)),
                  pl.BlockSpec((1, 1, 1, ncp, LANES), lambda bi, k, i: (0, bi, k, 0, 0)),
                  pl.BlockSpec((1, 1, 1, ncp, LANES), lambda bi, k, i: (1, bi, k, 0, 0)),
                  _full((ncp, nsp)),
                  pl.BlockSpec((2, NSA_GROUP, tq, nband), lambda bi, k, i: (k, 0, 0, 0))],
        out_specs=[pl.BlockSpec((1, tq, gw), lambda bi, k, i: (bi, i, k)),
                   pl.BlockSpec((1, 1, nsup, tq, LANES), lambda bi, k, i: (bi, k, 0, i, 0))],
        out_shape=[jax.ShapeDtypeStruct((b, t, NSA_W), BF16),
                   jax.ShapeDtypeStruct((b, NSA_KV_HEADS, nsup, t, LANES), BF16)],
        compiler_params=_cparams(("arbitrary", "arbitrary", "arbitrary")),
        name="nsa_select",
    )(q, cmp_kv, cmp_kv, jnp.asarray(ov, dtype=BF16), bt)


def _nsa_main_kernel(q_ref, ks_ref, vs_ref, kw0_ref, kw1_ref, kw2_ref, vw0_ref, vw1_ref, vw2_ref, sel_ref, oh_ref,
                     ocmp_ref, gate_ref, tb_ref, wm_ref, eg_ref, o_ref, *, tq, wide):
    i = pl.program_id(2)
    g = NSA_GROUP
    lo = lax.broadcasted_iota(jnp.int32, (1, LANES), 1) < NSA_DH

    def head_low(hh):
        blk = q_ref[0, :, (hh // 2) * LANES:(hh // 2 + 1) * LANES]
        if hh % 2 == 1:
            blk = pltpu.roll(blk.astype(F32), NSA_DH, 1).astype(BF16)
        return jnp.where(lo, blk, jnp.zeros_like(blk))

    qst = jnp.concatenate([head_low(hh) for hh in range(g)], axis=0)
    causal = (lax.broadcasted_iota(jnp.int32, (tq, tq), 1) <= lax.broadcasted_iota(jnp.int32, (tq, tq), 0))
    one = jnp.ones((1, LANES), BF16)
    sup_keys = SUP_BLOCKS * SLC_LEN
    all_masked = jnp.where(lo, 0.0, NEG).astype(BF16)

    def sel_step(jt, carry, near, tw):
        m, acc = carry
        key0 = jnp.maximum(jt, 0) * tw
        start = pl.multiple_of(key0, tw)
        within = pl.multiple_of(key0 & (sup_keys - 1), tw)
        mq = sel_ref[0, 0, lax.shift_right_logical(key0, int(math.log2(sup_keys)))]
        mq = jnp.where(jt >= 0, mq, all_masked)
        q_ext = qst + jnp.concatenate([mq] * g, axis=0)
        kt = jnp.where(lo, ks_ref[0, pl.ds(start, tw), :], oh_ref[pl.ds(within, tw), :])
        vt = jnp.where(lo, vs_ref[0, pl.ds(start, tw), :], one)
        s = _dot_nt(q_ext, kt)
        if near is not None:
            s = s.reshape(g, tq, tw) + tb_ref[:, :, near * tq:(near + 1) * tq]
            if near == 2:
                s = jnp.where(causal[None], s, NEG)
            s = s.reshape(g * tq, tw)
        m_new = jnp.maximum(m, jnp.max(s, axis=1, keepdims=True))
        p = jnp.exp2(s - m_new)
        acc = jnp.exp2(m - m_new) * acc + _dot(p.astype(BF16), vt)
        return m_new, acc

    n_far = jnp.maximum(i - 2, 0)
    n_wide = n_far // wide
    carry = (jnp.full((g * tq, 1), NEG, F32), jnp.zeros((g * tq, LANES), F32))
    carry = lax.fori_loop(0, n_wide, lambda j, c: sel_step(j, c, None, wide * tq), carry)
    done = n_wide * wide
    part = wide // 2
    while part >= 1:
        carry = lax.cond((n_far & part) != 0, lambda c, d=done, w=part: sel_step(d // w, c, None, w * tq),
                         lambda c: c, carry)
        done = done + (n_far & part)
        part //= 2
    for near in range(3):
        carry = sel_step(i - 2 + near, carry, near, tq)
    o_slc = (carry[1] / pltpu.roll(carry[1], NSA_DH, 1)).reshape(g, tq, LANES)

    var = jnp.minimum(i, 2)
    kws = (kw0_ref, kw1_ref, kw2_ref)
    vws = (vw0_ref, vw1_ref, vw2_ref)
    sw = []
    for near in range(3):
        s = _dot_nt(qst, kws[near][0]).reshape(g, tq, tq)
        s = s + tb_ref[:, :, near * tq:(near + 1) * tq] + wm_ref[var, :, near * tq:(near + 1) * tq][None]
        sw.append(s.reshape(g * tq, tq))
    m = jnp.maximum(jnp.maximum(jnp.max(sw[0], axis=1, keepdims=True), jnp.max(sw[1], axis=1, keepdims=True)),
                    jnp.max(sw[2], axis=1, keepdims=True))
    acc = jnp.zeros((g * tq, LANES), F32)
    for near in range(3):
        p = jnp.exp2(sw[near] - m)
        acc = acc + _dot(p.astype(BF16), jnp.where(lo, vws[near][0], one))
    o_win = (acc / pltpu.roll(acc, NSA_DH, 1)).reshape(g, tq, LANES)

    pair = lambda o: jnp.concatenate([jnp.where(lo, o[0], pltpu.roll(o[1], NSA_DH, 1)),
                                      jnp.where(lo, o[2], pltpu.roll(o[3], NSA_DH, 1))], axis=1)
    gates = _dot_hi(_sigmoid(gate_ref[0]), eg_ref[0])
    gw = g * NSA_DH
    out = (gates[:, 0:gw] * ocmp_ref[0].astype(F32)
           + gates[:, gw:2 * gw] * pair(o_slc) + gates[:, 2 * gw:3 * gw] * pair(o_win))
    o_ref[0] = out.astype(o_ref.dtype)


def nsa_main(q, ksw, vsw, sel, o_cmp, small, rel_bias, tq=256, wide=8):
    b, t, _ = q.shape
    nsup = sel.shape[2]
    g = NSA_GROUP
    gw = g * NSA_DH
    sup_keys = SUP_BLOCKS * SLC_LEN
    oh = np.zeros((sup_keys, LANES), np.float32)
    oh[np.arange(sup_keys), NSA_DH + np.arange(sup_keys) // SLC_LEN] = 1.0
    qi = np.arange(tq)[:, None]
    c = np.arange(3 * tq)[None, :]
    dist = qi + 2 * tq - c
    tb = _bias_table(rel_bias, dist, np.ones_like(dist, bool))
    wm = np.zeros((3, tq, 3 * tq), np.float32)
    for var in range(3):
        exists = c >= tq * (2 - var)
        wm[var] = np.where((dist >= 0) & (dist < WINDOW) & exists, 0.0, NEG)
    eg = np.zeros((NSA_KV_HEADS, LANES, 3 * gw), np.float32)
    for k in range(NSA_KV_HEADS):
        for hh in range(g):
            for br in range(3):
                eg[k, (k * g + hh) * 3 + br, br * gw + hh * NSA_DH:br * gw + (hh + 1) * NSA_DH] = 1.0
    near = lambda off, col: pl.BlockSpec(
        (1, tq, LANES), lambda bi, k, i: (bi, jnp.maximum(i - off, 0), col + k))
    return pl.pallas_call(
        functools.partial(_nsa_main_kernel, tq=tq, wide=wide),
        grid=(b, NSA_KV_HEADS, t // tq),
        in_specs=[pl.BlockSpec((1, tq, gw), lambda bi, k, i: (bi, i, k)),
                  pl.BlockSpec((1, t, LANES), lambda bi, k, i: (bi, 0, k)),
                  pl.BlockSpec((1, t, LANES), lambda bi, k, i: (bi, 0, k)),
                  near(2, 2), near(1, 2), near(0, 2), near(2, 2), near(1, 2), near(0, 2),
                  pl.BlockSpec((1, 1, nsup, tq, LANES), lambda bi, k, i: (bi, k, 0, i, 0)),
                  _full((sup_keys, LANES)),
                  pl.BlockSpec((1, tq, gw), lambda bi, k, i: (bi, i, k)),
                  pl.BlockSpec((1, tq, LANES), lambda bi, k, i: (bi, i, 0)),
                  pl.BlockSpec((g, tq, 3 * tq), lambda bi, k, i: (k, 0, 0)),
                  _full((3, tq, 3 * tq)),
                  pl.BlockSpec((1, LANES, 3 * gw), lambda bi, k, i: (k, 0, 0))],
        out_specs=pl.BlockSpec((1, tq, gw), lambda bi, k, i: (bi, i, k)),
        out_shape=jax.ShapeDtypeStruct((b, t, NSA_W), BF16),
        compiler_params=_cparams(("arbitrary", "arbitrary", "arbitrary")),
        name="nsa_main",
    )(q, ksw, vsw, ksw, ksw, ksw, vsw, vsw, vsw, sel, jnp.asarray(oh, dtype=BF16), o_cmp, small, tb,
      jnp.asarray(wm), jnp.asarray(eg))


def _gla_kernel(qk_ref, v_ref, r_ref, sm_ref, wg_ref, bg_ref, on_ref, tril_ref, o_ref, s_scr, *, tc):
    c = GLA_CHUNK

    @pl.when(pl.program_id(1) == 0)
    def _():
        s_scr[...] = jnp.zeros_like(s_scr)

    kw = GLA_KW
    log_a = _log_sigmoid(_dot_hi(sm_ref[0], wg_ref[...]) + bg_ref[...]) * (1.0 / GLA_TAU)
    gcum = _dot_hi(tril_ref[...], log_a)
    q = qk_ref[0, :, 0:kw] * (GLA_DK ** -0.5)
    k = qk_ref[0, :, kw:2 * kw]
    q_dec = (q * jnp.exp(gcum)).astype(BF16)
    k_inv = (k * jnp.exp(-gcum)).astype(BF16)
    ri = lax.broadcasted_iota(jnp.int32, (c, c), 0)
    ci = lax.broadcasted_iota(jnp.int32, (c, c), 1)
    causal = ci <= ri
    lo = lax.broadcasted_iota(jnp.int32, (1, LANES), 1) < GLA_DK
    nchunk = tc // c
    heads = range(GLA_HEADS)
    zero = jnp.zeros((c, LANES), BF16)

    qm, vb, kd, egl = {}, {}, {}, {}
    for n in range(nchunk):
        sl = slice(n * c, (n + 1) * c)
        gl = gcum[n * c + c - 1:n * c + c, :]
        kdn = (k[sl] * jnp.exp(gl - gcum[sl])).astype(BF16)
        for h in heads:
            pr = slice((h // 2) * LANES, (h // 2 + 1) * LANES)
            keep = lo if h % 2 == 0 else jnp.logical_not(lo)
            qm[n, h] = jnp.where(keep, q_dec[sl, pr], zero)
            vb[n, h] = v_ref[0, sl, h * GLA_DV:(h + 1) * GLA_DV].astype(BF16)
            kd[n, h] = kdn[:, pr]
            egl[n, h] = jnp.exp(gl[:, pr])
    idx = [(n, h) for n in range(nchunk) for h in heads]
    attn = {i: jnp.where(causal, _dot_nt(qm[i], k_inv[i[0] * c:(i[0] + 1) * c, (i[1] // 2) * LANES:
                                                       (i[1] // 2 + 1) * LANES]), 0.0).astype(BF16) for i in idx}
    o_intra = {i: _dot(attn[i], vb[i]) for i in idx}
    kv = {i: _dot_tn(vb[i], kd[i]) for i in idx}

    st = [s_scr[h] for h in heads]
    for n in range(nchunk):
        sl = slice(n * c, (n + 1) * c)
        o = [o_intra[n, h] + _dot_nt(qm[n, h], st[h].astype(BF16)) for h in heads]
        st = [st[h] * egl[n, h] + kv[n, h] for h in heads]
        for h in heads:
            on = o[h] * lax.rsqrt(jnp.mean(o[h] * o[h], axis=-1, keepdims=True) + EPS) * on_ref[...]
            o_ref[0, sl, h * GLA_DV:(h + 1) * GLA_DV] = (
                on * _silu(r_ref[0, sl, h * GLA_DV:(h + 1) * GLA_DV])).astype(o_ref.dtype)
    for h in heads:
        s_scr[h] = st[h]


def gated_linear_attention(qkvr, small, wg_up, bg, on_gain, tc=512):
    b, t, _ = qkvr.shape
    wg = jnp.zeros((LANES, GLA_KW), F32).at[3 * NSA_HEADS:3 * NSA_HEADS + GLA_GATE_RANK].set(wg_up)
    idx = np.arange(tc)
    tril = ((idx[:, None] >= idx[None, :]) & (idx[:, None] // GLA_CHUNK == idx[None, :] // GLA_CHUNK))
    return pl.pallas_call(
        functools.partial(_gla_kernel, tc=tc),
        grid=(b, t // tc),
        in_specs=[pl.BlockSpec((1, tc, 2 * GLA_KW), lambda i, j: (i, j, 0)),
                  pl.BlockSpec((1, tc, GLA_W), lambda i, j: (i, j, 1)),
                  pl.BlockSpec((1, tc, GLA_W), lambda i, j: (i, j, 2)),
                  pl.BlockSpec((1, tc, LANES), lambda i, j: (i, j, 0)),
                  _full((LANES, GLA_KW)), _full((1, GLA_KW)), _full((1, GLA_DV)), _full((tc, tc))],
        out_specs=pl.BlockSpec((1, tc, GLA_W), lambda i, j: (i, j, 0)),
        out_shape=jax.ShapeDtypeStruct((b, t, GLA_W), BF16),
        scratch_shapes=[pltpu.VMEM((GLA_HEADS, GLA_DV, LANES), F32)],
        compiler_params=_cparams(("arbitrary", "arbitrary")),
        name="gla",
    )(qkvr, qkvr, qkvr, small, wg, bg.reshape(1, GLA_KW), on_gain.reshape(1, GLA_DV),
      jnp.asarray(tril.astype(np.float32)))


def odd_mixer(h, w_in, nsa_qn, nsa_kn, nsa_pos, nsa_cmp_w1, nsa_cmp_w2, gla_wg_up, gla_bg, gla_on, rel_bias):
    cuts = np.cumsum((0,) + OD_SIZES)
    col = lambda i: w_in[:, cuts[i]:cuts[i + 1]]
    dup = lambda a: jnp.concatenate([a[:, :NSA_DH], a[:, :NSA_DH], a[:, NSA_DH:], a[:, NSA_DH:]], axis=1)
    ep = _head_norm_epilogue(NSA_DH)
    bd = _block_diag_ones(NSA_W, NSA_DH)
    qg = (jnp.tile(nsa_qn, NSA_HEADS) * (NSA_DH ** -0.5 * LOG2E)).reshape(1, NSA_W)
    kg = jnp.tile(nsa_kn, NSA_HEADS).reshape(1, NSA_W)
    w_small = jnp.zeros((w_in.shape[0], LANES), F32)
    w_small = w_small.at[:, 0:24].set(col(7)).at[:, 24:40].set(col(11))
    nq, kcvc, ksw, vsw, small, qkvr = proj_multi(
        h, [(col(0), BF16, ep, (bd, qg)),
            (jnp.concatenate([col(1), col(2)], axis=1), F32, None, ()),
            (jnp.concatenate([dup(col(3)), dup(col(5))], axis=1), BF16, ep, (bd, kg)),
            (jnp.concatenate([dup(col(4)), dup(col(6))], axis=1), BF16, None, ()),
            (w_small, F32, None, ()),
            (jnp.concatenate([col(8), col(9), col(10), col(12)], axis=1), F32, None, ())], name="proj_odd")
    cmp_kv = nsa_compress(kcvc, nsa_pos, nsa_cmp_w1, nsa_cmp_w2, nsa_kn)
    o_cmp, sel = nsa_select(nq, cmp_kv, rel_bias)
    o_nsa = nsa_main(nq, ksw, vsw, sel, o_cmp, small, rel_bias)
    o_gla = gated_linear_attention(qkvr, small, gla_wg_up, gla_bg, gla_on)
    return o_nsa, o_gla


MOE_TM = 256
MOE_ROWS = 512


def _first_index(mask_val, idx, big, axis):
    return jnp.min(jnp.where(mask_val, idx, big), axis=axis, keepdims=True)


def _route_kernel(h_ref, rt_ref, b_ref, up_ref, eid_ref, rank_ref, w_ref, cnt_ref, run):
    tm = h_ref.shape[0]
    ne = N_EXPERTS
    gsz = ne // N_GROUPS

    @pl.when(pl.program_id(0) == 0)
    def _():
        run[...] = jnp.zeros_like(run)

    scores = _sigmoid(_dot_nt(rt_ref[...], h_ref[...], HI))
    biased = scores + b_ref[...]
    b3 = biased.reshape(N_GROUPS, gsz, tm)
    i3 = lax.broadcasted_iota(jnp.int32, (1, gsz, 1), 1).astype(F32)
    m1 = jnp.max(b3, axis=1, keepdims=True)
    f1 = _first_index(b3 == m1, i3, float(gsz), 1)
    m2 = jnp.max(jnp.where(i3 == f1, -jnp.inf, b3), axis=1, keepdims=True)
    gs = (m1 + m2).reshape(N_GROUPS, tm)
    gidx = lax.broadcasted_iota(jnp.int32, (N_GROUPS, 1), 0).astype(F32)
    gmask = jnp.zeros((N_GROUPS, tm), F32)
    for _ in range(TOPK_GROUPS):
        m = jnp.max(gs, axis=0, keepdims=True)
        pick = gidx == _first_index(gs == m, gidx, float(N_GROUPS), 0)
        gmask = jnp.where(pick, 1.0, gmask)
        gs = jnp.where(pick, -jnp.inf, gs)
    emask = jnp.broadcast_to(gmask.reshape(N_GROUPS, 1, tm), (N_GROUPS, gsz, tm)).reshape(ne, tm)
    work = jnp.where(emask > 0.5, biased, -jnp.inf)
    eidx = lax.broadcasted_iota(jnp.int32, (ne, 1), 0).astype(F32)
    picks, eids, ws = [], [], []
    for _ in range(TOP_K):
        m = jnp.max(work, axis=0, keepdims=True)
        first = _first_index(work == m, eidx, float(ne), 0)
        pick = eidx == first
        picks.append(pick)
        eids.append(first)
        ws.append(jnp.sum(jnp.where(pick, scores, 0.0), axis=0, keepdims=True))
        work = jnp.where(pick, -jnp.inf, work)
    wsum = ws[0]
    for k in range(1, TOP_K):
        wsum = wsum + ws[k]
    chosen = jnp.zeros((ne, tm), F32)
    for pick in picks:
        chosen = jnp.where(pick, 1.0, chosen)
    pos = run[...] + _dot(chosen.astype(BF16), up_ref[...])
    run[...] = run[...] + jnp.sum(chosen, axis=1, keepdims=True)
    cnt_ref[...] = run[...]
    row = lax.broadcasted_iota(jnp.int32, (8, 1), 0)
    eid_o = jnp.zeros((8, tm), F32)
    rank_o = jnp.zeros((8, tm), F32)
    w_o = jnp.zeros((LANES, tm), F32)
    rowl = lax.broadcasted_iota(jnp.int32, (LANES, 1), 0)
    for k in range(TOP_K):
        rk = jnp.sum(jnp.where(picks[k], pos, 0.0), axis=0, keepdims=True)
        eid_o = jnp.where(row == k, eids[k], eid_o)
        rank_o = jnp.where(row == k, rk, rank_o)
        w_o = jnp.where(rowl == k, ws[k] / wsum * ROUTE_SCALE, w_o)
    eid_ref[0] = eid_o.astype(jnp.int32)
    rank_ref[0] = rank_o.astype(jnp.int32)
    w_ref[...] = w_o.T


def moe_route(h2, router, e_bias, tm=MOE_TM):
    nt, d = h2.shape
    ne = N_EXPERTS
    up = jnp.asarray(np.triu(np.ones((tm, tm), np.float32), 1), dtype=BF16)
    nb = nt // tm
    return pl.pallas_call(
        _route_kernel,
        grid=(nb,),
        in_specs=[pl.BlockSpec((tm, d), lambda i: (i, 0)), _full((ne, d)), _full((ne, 1)), _full((tm, tm))],
        out_specs=[pl.BlockSpec((1, 8, tm), lambda i: (i, 0, 0)),
                   pl.BlockSpec((1, 8, tm), lambda i: (i, 0, 0)),
                   pl.BlockSpec((tm, LANES), lambda i: (i, 0)),
                   _full((ne, 1))],
        out_shape=[jax.ShapeDtypeStruct((nb, 8, tm), jnp.int32), jax.ShapeDtypeStruct((nb, 8, tm), jnp.int32),
                   jax.ShapeDtypeStruct((nt, LANES), F32), jax.ShapeDtypeStruct((ne, 1), F32)],
        scratch_shapes=[pltpu.VMEM((ne, 1), F32)],
        compiler_params=_cparams(("arbitrary",)),
        name="moe_route",
    )(h2, router.T, e_bias.reshape(ne, 1), up)


def _dispatch_kernel(dest_ref, h_ref, xs_ref, sem):
    tm = h_ref.shape[0]

    def copy(t, row):
        return pltpu.make_async_copy(h_ref.at[pl.ds(t, 1), :], xs_ref.at[pl.ds(row, 1), :], sem)

    def issue(t, _):
        for k in range(TOP_K):
            copy(t, dest_ref[0, t, k]).start(priority=k % 2)
        return 0

    def drain(t, _):
        for k in range(TOP_K):
            copy(0, 0).wait()
        return 0

    lax.fori_loop(0, tm, issue, 0, unroll=4)
    lax.fori_loop(0, tm, drain, 0, unroll=4)


def moe_dispatch(h2, dest, tm=MOE_TM):
    nt, d = h2.shape
    return pl.pallas_call(
        _dispatch_kernel,
        grid=(nt // tm,),
        in_specs=[pl.BlockSpec((1, tm, 8), lambda i: (i, 0, 0), memory_space=pltpu.SMEM),
                  pl.BlockSpec((tm, d), lambda i: (i, 0))],
        out_specs=pl.BlockSpec(memory_space=pl.ANY),
        scratch_shapes=[pltpu.SemaphoreType.DMA(())],
        out_shape=jax.ShapeDtypeStruct((nt * TOP_K, d), F32),
        compiler_params=_cparams(("arbitrary",)),
        name="moe_dispatch",
    )(dest, h2)


def _ffn_kernel(blk_ref, exp_ref, lo_ref, hi_ref, first_ref, valid_ref, x_ref, wg_ref, wu_ref, wd_ref, o_ref,
                wg_b, wu_b, wd_b):
    i = pl.program_id(0)
    rows = x_ref.shape[0]

    @pl.when((i == 0) | (exp_ref[i] != exp_ref[jnp.maximum(i - 1, 0)]))
    def _():
        wg_b[...] = wg_ref[0].astype(BF16)
        wu_b[...] = wu_ref[0].astype(BF16)
        wd_b[...] = wd_ref[0].astype(BF16)

    @pl.when(valid_ref[i] == 1)
    def _():
        x = x_ref[...].astype(BF16)
        a = _dot(x, wg_b[...])
        u = _dot(x, wu_b[...])
        y = _dot((_silu(a) * u).astype(BF16), wd_b[...])
        r = blk_ref[i] * rows + lax.broadcasted_iota(jnp.int32, (rows, 1), 0)
        y = jnp.where((r >= lo_ref[i]) & (r < hi_ref[i]), y, 0.0)

        @pl.when(first_ref[i] == 1)
        def _():
            o_ref[...] = y

        @pl.when(first_ref[i] == 0)
        def _():
            o_ref[...] = o_ref[...] + y


def _items_kernel(cnt_ref, starts_ref, blk_ref, exp_ref, lo_ref, hi_ref, first_ref, valid_ref, *, rows, n_items):
    shift = int(math.log2(rows))

    def expert(e, carry):
        start, n = carry
        c = cnt_ref[e]
        starts_ref[e] = start
        end = start + c
        first_blk = lax.shift_right_logical(start, shift)
        n_blk = jnp.where(c > 0, lax.shift_right_logical(jnp.maximum(end - 1, 0), shift) - first_blk + 1, 0)

        def item(k, n):
            b = first_blk + k
            lo = jnp.maximum(start, b * rows)
            blk_ref[n] = b
            exp_ref[n] = e
            lo_ref[n] = lo
            hi_ref[n] = jnp.minimum(end, (b + 1) * rows)
            first_ref[n] = (lo == b * rows).astype(jnp.int32)
            valid_ref[n] = 1
            return n + 1

        return end, lax.fori_loop(0, n_blk, item, n)

    _, total = lax.fori_loop(0, N_EXPERTS, expert, (jnp.int32(0), jnp.int32(0)))
    last = jnp.maximum(total - 1, 0)

    def fill(k, _):
        blk_ref[k] = blk_ref[last]
        exp_ref[k] = exp_ref[last]
        lo_ref[k] = 0
        hi_ref[k] = 0
        first_ref[k] = 0
        valid_ref[k] = 0
        return 0

    lax.fori_loop(total, n_items, fill, 0)


def _ffn_items(counts, n_rows, rows):
    n_items = n_rows // rows + N_EXPERTS - 1
    smem = pl.BlockSpec(memory_space=pltpu.SMEM)
    out = pl.pallas_call(
        functools.partial(_items_kernel, rows=rows, n_items=n_items),
        in_specs=[smem],
        out_specs=[smem] * 7,
        out_shape=[jax.ShapeDtypeStruct((N_EXPERTS,), jnp.int32)]
                  + [jax.ShapeDtypeStruct((n_items,), jnp.int32)] * 6,
        name="moe_items",
    )(counts)
    return out[0], tuple(out[1:])


def moe_ffn_sorted(xs, items, wg, wu, wd, layer, rows=MOE_ROWS):
    n_rows, d = xs.shape
    n_items = items[0].shape[0]
    de = wg.shape[-1]
    return pl.pallas_call(
        _ffn_kernel,
        grid_spec=pltpu.PrefetchScalarGridSpec(
            num_scalar_prefetch=6,
            grid=(n_items,),
            in_specs=[pl.BlockSpec((rows, d), lambda i, blk, e, *_: (blk[i], 0)),
                      pl.BlockSpec((None, 1, d, de), lambda i, blk, e, *_: (layer, e[i], 0, 0)),
                      pl.BlockSpec((None, 1, d, de), lambda i, blk, e, *_: (layer, e[i], 0, 0)),
                      pl.BlockSpec((None, 1, de, d), lambda i, blk, e, *_: (layer, e[i], 0, 0))],
            out_specs=pl.BlockSpec((rows, d), lambda i, blk, e, *_: (blk[i], 0)),
            scratch_shapes=[pltpu.VMEM((d, de), BF16), pltpu.VMEM((d, de), BF16), pltpu.VMEM((de, d), BF16)]),
        out_shape=jax.ShapeDtypeStruct((n_rows, d), F32),
        compiler_params=_cparams(("arbitrary",)),
        name="moe_ffn",
    )(*items, xs, wg, wu, wd)


def _combine_kernel(dest_ref, ys_ref, w_ref, h_ref, x_ref, g_ref, sg_ref, su_ref, sd_ref, *rest):
    if len(rest) == 3:
        nxt, (o_ref, buf, sem) = None, rest
    else:
        nxt, (o_ref, hn_ref, buf, sem) = rest[:3], rest[3:]
    tm = h_ref.shape[0]

    def copy(t, k, row):
        return pltpu.make_async_copy(ys_ref.at[pl.ds(row, 1), :], buf.at[k, pl.ds(t, 1), :], sem)

    def issue(t, _):
        for k in range(TOP_K):
            copy(t, k, dest_ref[0, t, k]).start(priority=k % 2)
        return 0

    def drain(t, _):
        for k in range(TOP_K):
            copy(0, 0, 0).wait()
        return 0

    lax.fori_loop(0, tm, issue, 0, unroll=4)
    hb = h_ref[...].astype(BF16)
    y = _dot((_silu(_dot(hb, sg_ref[...])) * _dot(hb, su_ref[...])).astype(BF16), sd_ref[...])
    lax.fori_loop(0, tm, drain, 0, unroll=4)
    w = w_ref[...]
    for k in range(TOP_K):
        y = y + w[:, k:k + 1] * buf[k]
    x_new = x_ref[...] + g_ref[0] * y
    o_ref[...] = x_new
    if nxt is not None:
        hn_ref[...] = _norm_mod(x_new, nxt[0][...], nxt[1][0], nxt[2][0]).astype(hn_ref.dtype)


def moe_combine(ys, dest, w, h2, x2, gate, sg, su, sd, seq, next_norm=None, tm=MOE_TM):
    nt, d = h2.shape
    ds_ = sg.shape[-1]
    per_b = seq // tm
    tile = lambda: pl.BlockSpec((tm, d), lambda i: (i, 0))
    batch = lambda: pl.BlockSpec((1, 1, d), lambda i: (i // per_b, 0, 0))
    nb = gate.shape[0]
    extra_in, extra_specs = [], []
    out_specs, out_shape = tile(), jax.ShapeDtypeStruct((nt, d), F32)
    if next_norm is not None:
        g_n, sc_n, sh_n = next_norm
        extra_in = [g_n.reshape(1, d), sc_n.reshape(nb, 1, d), sh_n.reshape(nb, 1, d)]
        extra_specs = [_full((1, d)), batch(), batch()]
        out_specs = [tile(), tile()]
        out_shape = [out_shape, jax.ShapeDtypeStruct((nt, d), BF16)]
    return pl.pallas_call(
        _combine_kernel,
        grid=(nt // tm,),
        in_specs=[pl.BlockSpec((1, tm, 8), lambda i: (i, 0, 0), memory_space=pltpu.SMEM),
                  pl.BlockSpec(memory_space=pl.ANY),
                  pl.BlockSpec((tm, LANES), lambda i: (i, 0)), tile(), tile(), batch(),
                  _full((d, ds_)), _full((d, ds_)), _full((ds_, d))] + extra_specs,
        out_specs=out_specs,
        scratch_shapes=[pltpu.VMEM((TOP_K, tm, d), F32), pltpu.SemaphoreType.DMA(())],
        out_shape=out_shape,
        compiler_params=_cparams(("arbitrary",)),
        name="moe_combine",
    )(dest, ys, w, h2, x2, gate, sg.astype(BF16), su.astype(BF16), sd.astype(BF16), *extra_in)


def moe_layer(x, h, gate, router, e_bias, wg, wu, wd, layer, sg, su, sd, next_norm=None):
    b, t, d = x.shape
    nt = b * t
    h2 = h.reshape(nt, d)
    eid, rank, w, counts = moe_route(h2, router, e_bias)
    starts, items = _ffn_items(counts.reshape(-1).astype(jnp.int32), nt * TOP_K, MOE_ROWS)
    hit = eid[..., None] == jnp.arange(N_EXPERTS, dtype=jnp.int32)
    dest = jnp.sum(jnp.where(hit, starts.astype(jnp.int32), 0), axis=-1) + rank
    dest = jnp.swapaxes(dest, 1, 2)
    xs = moe_dispatch(h2, dest)
    ys = moe_ffn_sorted(xs, items, wg, wu, wd, layer)
    out = moe_combine(ys, dest, w, h2, x.reshape(nt, d), gate.reshape(b, 1, d), sg, su, sd, t, next_norm)
    if next_norm is None:
        return out.reshape(b, t, d), None
    return out[0].reshape(b, t, d), out[1].reshape(b, t, d)


def kernel(x, c, ada_w, ada_b, norm_mix, norm_ffn, rel_bias, ev_w_in, ev_w_out, fox_fb, fox_qn, fox_kn, gdn_conv, gdn_a_log, gdn_dt_bias, gdn_on, od_w_in, od_w_out, nsa_qn, nsa_kn, nsa_pos, nsa_cmp_w1, nsa_cmp_w2, gla_wg_up, gla_bg, gla_on, moe_router, moe_bias, moe_wg, moe_wu, moe_wd, sh_wg, sh_wu, sh_wd):
    d = x.shape[-1]
    depth = ada_w.shape[0]
    mod = adaln(c, ada_w, ada_b)
    mods = [[mod[layer, :, i * d:(i + 1) * d] for i in range(6)] for layer in range(depth)]
    h = ln_mod(x, norm_mix[0], mods[0][1], mods[0][0], BF16)
    for layer in range(depth):
        sh1, sc1, g1, sh2, sc2, g2 = mods[layer]
        j = layer // 2
        if layer % 2 == 0:
            y1, y2 = even_mixer(h, ev_w_in[j], fox_fb[j], fox_qn[j], fox_kn[j], gdn_conv[j], gdn_a_log[j],
                                gdn_dt_bias[j], gdn_on[j])
            w_out = ev_w_out[j]
        else:
            y1, y2 = odd_mixer(h, od_w_in[j], nsa_qn[j], nsa_kn[j], nsa_pos[j], nsa_cmp_w1[j], nsa_cmp_w2[j],
                               gla_wg_up[j], gla_bg[j], gla_on[j], rel_bias)
            w_out = od_w_out[j]
        x, h_ffn = out_proj(y1, y2, w_out, x, g1, norm_ffn[layer], sc2, sh2)
        nxt = None if layer + 1 == depth else (norm_mix[layer + 1], mods[layer + 1][1], mods[layer + 1][0])
        x, h = moe_layer(x, h_ffn, g2, moe_router[layer], moe_bias[layer], moe_wg, moe_wu, moe_wd, layer,
                         sh_wg[layer], sh_wu[layer], sh_wd[layer], nxt)
    return x
```

```python
import functools
import math

import numpy as np
import jax
import jax.numpy as jnp
from jax import lax
from jax.experimental import pallas as pl
from jax.experimental.pallas import tpu as pltpu

F32 = jnp.float32
BF16 = jnp.bfloat16
HI = lax.Precision.HIGHEST

EPS = 1e-6
LOG2E = math.log2(math.e)
NEG = -1e30

FOX_HEADS, FOX_DH = 8, 64
GDN_HEADS, GDN_DH, GDN_CONV = 4, 128, 4
NSA_HEADS, NSA_KV_HEADS, NSA_DH = 8, 2, 64
NSA_GROUP = NSA_HEADS // NSA_KV_HEADS
CMP_LEN, CMP_STRIDE, CMP_HIDDEN = 32, 16, 256
SLC_LEN, SLC_TOPK, WINDOW = 64, 16, 512
GLA_HEADS, GLA_DK, GLA_DV, GLA_GATE_RANK, GLA_TAU, GLA_CHUNK = 4, 64, 128, 16, 16.0, 64
REL_BUCKETS, REL_MAX_DIST = 32, 128
N_EXPERTS, TOP_K, D_EXPERT, D_SHARED = 64, 6, 256, 256
N_GROUPS, TOPK_GROUPS, ROUTE_SCALE = 8, 4, 2.5

FOX_W = FOX_HEADS * FOX_DH
GDN_W = GDN_HEADS * GDN_DH
NSA_W = NSA_HEADS * NSA_DH
NSA_KV_W = NSA_KV_HEADS * NSA_DH
GLA_KW = GLA_HEADS * GLA_DK
GLA_W = GLA_HEADS * GLA_DV
EV_SIZES = (FOX_W, FOX_W, FOX_W, FOX_HEADS, 3 * GDN_W, GDN_HEADS, GDN_HEADS, GDN_W)
OD_SIZES = (NSA_W,) + (NSA_KV_W,) * 6 + (3 * NSA_HEADS, GLA_KW, GLA_KW, GLA_W, GLA_GATE_RANK, GLA_W)

LANES = 128
SUP_BLOCKS = 64
VMEM_LIMIT = 56 * 1024 * 1024


def _cparams(sem):
    return pltpu.CompilerParams(dimension_semantics=sem, vmem_limit_bytes=VMEM_LIMIT)


def _full(shape):
    n = len(shape)
    return pl.BlockSpec(shape, lambda *_: (0,) * n)


def _dot(a, b):
    return jnp.dot(a, b, preferred_element_type=F32)


def _dot_hi(a, b):
    return jnp.dot(a, b, precision=HI, preferred_element_type=F32)


def _dot_nt(a, b, precision=None):
    return lax.dot_general(a, b, (((1,), (1,)), ((), ())), precision=precision, preferred_element_type=F32)


def _dot_tn(a, b, precision=None):
    return lax.dot_general(a, b, (((0,), (0,)), ((), ())), precision=precision, preferred_element_type=F32)


def _dot_ones(m, x):
    hi = x.astype(BF16)
    r = x - hi.astype(F32)
    mid = r.astype(BF16)
    low = (r - mid.astype(F32)).astype(BF16)
    return _dot(m, hi) + (_dot(m, mid) + _dot(m, low))


def _sigmoid(x):
    return 1.0 / (1.0 + jnp.exp(-x))


def _silu(x):
    return x * _sigmoid(x)


def _softplus(x):
    return jnp.maximum(x, 0.0) + jnp.log(1.0 + jnp.exp(-jnp.abs(x)))


def _log_sigmoid(x):
    return -_softplus(-x)


def _adaln_kernel(c_ref, w_ref, b_ref, o_ref):
    c = c_ref[...]
    o_ref[0] = _dot_hi(_silu(c), w_ref[0]) + b_ref[0]


def adaln(c, ada_w, ada_b):
    depth, d, n = ada_w.shape
    b = c.shape[0]
    cp = jnp.zeros((8, d), F32).at[:b].set(c)
    tn = 1536
    out = pl.pallas_call(
        _adaln_kernel,
        grid=(depth, n // tn),
        in_specs=[_full((8, d)),
                  pl.BlockSpec((1, d, tn), lambda l, j: (l, 0, j)),
                  pl.BlockSpec((1, 1, tn), lambda l, j: (l, 0, j))],
        out_specs=pl.BlockSpec((1, 8, tn), lambda l, j: (l, 0, j)),
        out_shape=jax.ShapeDtypeStruct((depth, 8, n), F32),
        compiler_params=_cparams(("arbitrary", "arbitrary")),
        name="adaln",
    )(cp, ada_w, ada_b.reshape(depth, 1, n))
    return out[:, :b]


def _ln_kernel(x_ref, g_ref, sc_ref, sh_ref, o_ref):
    x = x_ref[0]
    y = x * lax.rsqrt(jnp.mean(x * x, axis=-1, keepdims=True) + EPS) * g_ref[...]
    o_ref[0] = (y * (1.0 + sc_ref[0]) + sh_ref[0]).astype(o_ref.dtype)


def ln_mod(x, g, sc, sh, out_dtype, tm=512):
    b, t, d = x.shape
    return pl.pallas_call(
        _ln_kernel,
        grid=(b, t // tm),
        in_specs=[pl.BlockSpec((1, tm, d), lambda i, j: (i, j, 0)),
                  _full((1, d)),
                  pl.BlockSpec((1, 1, d), lambda i, j: (i, 0, 0)),
                  pl.BlockSpec((1, 1, d), lambda i, j: (i, 0, 0))],
        out_specs=pl.BlockSpec((1, tm, d), lambda i, j: (i, j, 0)),
        out_shape=jax.ShapeDtypeStruct((b, t, d), out_dtype),
        compiler_params=_cparams(("arbitrary", "arbitrary")),
        name="ln_mod",
    )(x, g.reshape(1, d), sc.reshape(b, 1, d), sh.reshape(b, 1, d))


def proj_multi(h, groups, tm=512, name="proj"):
    b, t, d = h.shape
    widths = [g[0].shape[1] for g in groups]
    starts = np.cumsum([0] + widths)
    w_cat = jnp.concatenate([g[0] for g in groups], axis=1).astype(BF16)
    extras = [e for g in groups for e in g[3]]
    n_ex = [len(g[3]) for g in groups]
    n_out = len(groups)

    def kern(h_ref, w_ref, *rest):
        ex_refs = rest[:len(extras)]
        o_refs = rest[len(extras):]
        y = _dot(h_ref[0], w_ref[...])
        pos = 0
        for gi, (_, out_dtype, epilogue, _) in enumerate(groups):
            yg = y[:, starts[gi]:starts[gi + 1]]
            if epilogue is not None:
                yg = epilogue(yg, *[e[...] for e in ex_refs[pos:pos + n_ex[gi]]])
            pos += n_ex[gi]
            o_refs[gi][0] = yg.astype(out_dtype)

    return pl.pallas_call(
        kern,
        grid=(b, t // tm),
        in_specs=[pl.BlockSpec((1, tm, d), lambda i, j: (i, j, 0)), _full((d, int(starts[-1])))]
                 + [_full(e.shape) for e in extras],
        out_specs=[pl.BlockSpec((1, tm, n), lambda i, j: (i, j, 0)) for n in widths],
        out_shape=[jax.ShapeDtypeStruct((b, t, n), g[1]) for n, g in zip(widths, groups)],
        compiler_params=_cparams(("arbitrary", "arbitrary")),
        name=name,
    )(h, w_cat, *extras)


def _head_norm_epilogue(dh):
    inv = 1.0 / dh

    def ep(y, bd, gain):
        ssq = _dot((y * y).astype(BF16), bd)
        return y * lax.rsqrt(ssq * inv + EPS) * gain

    return ep


def _block_diag_ones(n, dh):
    i = np.arange(n) // dh
    return jnp.asarray((i[:, None] == i[None, :]).astype(np.float32), dtype=BF16)


def _norm_mod(x, gain, sc, sh):
    y = x * lax.rsqrt(jnp.mean(x * x, axis=-1, keepdims=True) + EPS) * gain
    return y * (1.0 + sc) + sh


def _outproj_kernel(y1_ref, y2_ref, wa_ref, wb_ref, x_ref, g_ref, n_ref, sc_ref, sh_ref, o_ref, h_ref):
    y = _dot(y1_ref[0], wa_ref[...]) + _dot(y2_ref[0], wb_ref[...])
    x_new = x_ref[0] + g_ref[0] * y
    o_ref[0] = x_new
    h_ref[0] = _norm_mod(x_new, n_ref[...], sc_ref[0], sh_ref[0]).astype(h_ref.dtype)


def out_proj(y1, y2, w_out, x, gate, g_norm, sc, sh, tm=512):
    b, t, d = x.shape
    n1, n2 = y1.shape[-1], y2.shape[-1]
    wa = w_out[:n1].astype(BF16)
    wb = w_out[n1:].astype(BF16)
    row = lambda n: pl.BlockSpec((1, tm, n), lambda i, j: (i, j, 0))
    per_b = lambda: pl.BlockSpec((1, 1, d), lambda i, j: (i, 0, 0))
    return pl.pallas_call(
        _outproj_kernel,
        grid=(b, t // tm),
        in_specs=[row(n1), row(n2), _full((n1, d)), _full((n2, d)), row(d), per_b(), _full((1, d)), per_b(), per_b()],
        out_specs=[row(d), row(d)],
        out_shape=[jax.ShapeDtypeStruct((b, t, d), F32), jax.ShapeDtypeStruct((b, t, d), F32)],
        compiler_params=_cparams(("arbitrary", "arbitrary")),
        name="out_proj",
    )(y1, y2, wa, wb, x, gate.reshape(b, 1, d), g_norm.reshape(1, d), sc.reshape(b, 1, d), sh.reshape(b, 1, d))


def _decay_kernel(s_ref, fb_ref, tril_ref, place_ref, o_ref, carry):
    @pl.when(pl.program_id(1) == 0)
    def _():
        carry[...] = jnp.zeros_like(carry)

    tm = s_ref.shape[1]
    lf = _log_sigmoid(s_ref[0] + fb_ref[...])
    cum = _dot_ones(tril_ref[...], lf) + carry[...]
    carry[...] = cum[tm - 1:tm, :]
    x = cum * LOG2E
    hi = x.astype(BF16)
    r1 = x - hi.astype(F32)
    mid = r1.astype(BF16)
    low = (r1 - mid.astype(F32)).astype(BF16)
    o_ref[0] = _dot(jnp.concatenate([hi, mid, low], axis=1), place_ref[...]).astype(o_ref.dtype)


def fox_decay(small, fox_fb, tm=512):
    b, t, _ = small.shape
    fb = jnp.zeros((1, LANES), F32).at[0, :FOX_HEADS].set(fox_fb)
    tril = jnp.asarray(np.tril(np.ones((tm, tm), np.float32)), dtype=BF16)
    place = np.zeros((3 * LANES, FOX_W), np.float32)
    for h in range(FOX_HEADS):
        for j in range(3):
            place[j * LANES + h, (h // 2) * LANES + (FOX_DH if h % 2 == 0 else 0) + j] = 1.0
    return pl.pallas_call(
        _decay_kernel,
        grid=(b, t // tm),
        in_specs=[pl.BlockSpec((1, tm, LANES), lambda i, j: (i, j, 0)), _full((1, LANES)), _full((tm, tm)),
                  _full((3 * LANES, FOX_W))],
        out_specs=pl.BlockSpec((1, tm, FOX_W), lambda i, j: (i, j, 0)),
        out_shape=jax.ShapeDtypeStruct((b, t, FOX_W), BF16),
        scratch_shapes=[pltpu.VMEM((1, LANES), F32)],
        compiler_params=_cparams(("arbitrary", "arbitrary")),
        name="fox_decay",
    )(small, fb, tril, jnp.asarray(place, dtype=BF16))


def _fox_kernel(q_ref, k_ref, v_ref, f_ref, o_ref, *, tq, wide):
    i = pl.program_id(2)
    lane = lax.broadcasted_iota(jnp.int32, (1, LANES), 1)
    lo = lane < FOX_DH
    coef = jnp.where((lane & (FOX_DH - 1)) < 3, -1.0, 0.0).astype(BF16)
    q = q_ref[0]
    qs = (jnp.where(lo, q, coef), jnp.where(lo, coef, q))
    causal = (lax.broadcasted_iota(jnp.int32, (tq, tq), 1) <= lax.broadcasted_iota(jnp.int32, (tq, tq), 0))
    one = jnp.ones((1, LANES), BF16)

    def step(j, carry, tw, diag=False):
        start = pl.multiple_of(j * tw, tw)
        kt = k_ref[0, pl.ds(start, tw), :]
        ft = f_ref[0, pl.ds(start, tw), :]
        vt = v_ref[0, pl.ds(start, tw), :]
        s_pair = (_dot_nt(qs[0], jnp.where(lo, kt, ft)), _dot_nt(qs[1], jnp.where(lo, ft, kt)))
        vs = (jnp.where(lo, vt, one), jnp.where(lo, one, vt))
        new = []
        for hh in range(2):
            m, acc = carry[hh]
            s = s_pair[hh]
            if diag:
                s = jnp.where(causal, s, NEG)
            m_new = jnp.maximum(m, jnp.max(s, axis=1, keepdims=True))
            p = jnp.exp2(s - m_new)
            acc = jnp.exp2(m - m_new) * acc + _dot(p.astype(BF16), vs[hh])
            new.append((m_new, acc))
        return tuple(new)

    carry = tuple((jnp.full((tq, 1), NEG, F32), jnp.zeros((tq, LANES), F32)) for _ in range(2))
    n_wide = i // wide
    carry = lax.fori_loop(0, n_wide, lambda j, c: step(j, c, wide * tq), carry)
    done = n_wide * wide
    part = wide // 2
    while part >= 1:
        carry = lax.cond((i & part) != 0, lambda c, d=done, w=part: step(d // w, c, w * tq), lambda c: c, carry)
        done = done + (i & part)
        part //= 2
    carry = step(i, carry, tq, diag=True)
    acc = jnp.where(lo, carry[0][1], carry[1][1])
    den = jnp.where(lo, carry[1][1], carry[0][1])
    o_ref[0] = (acc / pltpu.roll(den, FOX_DH, 1)).astype(o_ref.dtype)


def fox_attention(q, k, v, feat, tq=512, wide=4):
    b, t, w = q.shape
    npair = w // LANES
    nt = t // tq
    whole = lambda: pl.BlockSpec((1, t, LANES), lambda bi, p, i: (bi, 0, p))
    return pl.pallas_call(
        functools.partial(_fox_kernel, tq=tq, wide=wide),
        grid=(b, npair, nt),
        in_specs=[pl.BlockSpec((1, tq, LANES), lambda bi, p, i: (bi, i, p)), whole(), whole(), whole()],
        out_specs=pl.BlockSpec((1, tq, LANES), lambda bi, p, i: (bi, i, p)),
        out_shape=jax.ShapeDtypeStruct((b, t, w), BF16),
        compiler_params=_cparams(("arbitrary", "arbitrary", "arbitrary")),
        name="fox_attn",
    )(q, k, v, feat)


def _mm(a, b):
    return _dot(a.astype(BF16), b.astype(BF16))


def _tril_solve(a, rhs, ri, ci):
    n = a[0].shape[0]
    both = lambda f, x, y: [f(p, q) for p, q in zip(x, y)]
    eye = (ri == ci).astype(F32)
    same = lambda b: (lax.shift_right_logical(ri, int(math.log2(b)))
                      == lax.shift_right_logical(ci, int(math.log2(b))))
    base = 16
    d = [jnp.where(same(base), p, 0.0) for p in a]
    d2 = both(_mm, d, d)
    d4 = both(_mm, d2, d2)
    r1 = [eye - p + p2 - t for p, p2, t in zip(d, d2, both(_mm, d, d2))]
    d8 = both(_mm, d4, d4)
    r2 = [eye + p4 + p8 + t for p4, p8, t in zip(d4, d8, both(_mm, d4, d8))]
    t = both(_mm, r1, r2)
    b = base
    while b < n:
        join = same(2 * b) & jnp.logical_not(same(b))
        low = [jnp.where(join, p, 0.0) for p in a]
        t = [p - q for p, q in zip(t, both(_mm, both(_mm, t, low), t))]
        b *= 2
    return both(_mm, t, rhs)


GDN_BLOCK = 128


def _gdn_kernel(x_ref, sm_ref, z_ref, cw_ref, ega_ref, egb_ref, alog_ref, dtb_ref, on_ref, tril_ref,
                o_ref, s_scr, prev_scr, *, tc):
    c = GDN_BLOCK
    w = GDN_W

    @pl.when(pl.program_id(1) == 0)
    def _():
        s_scr[...] = jnp.zeros_like(s_scr)
        prev_scr[...] = jnp.zeros_like(prev_scr)

    x = x_ref[0]
    prev = prev_scr[...]
    row8 = lax.broadcasted_iota(jnp.int32, (8, 1), 0)
    acc = x * cw_ref[GDN_CONV - 1:GDN_CONV, :]
    for s in range(1, GDN_CONV):
        rolled = pltpu.roll(x, s, 0)
        head = jnp.where(row8 < s, pltpu.roll(prev, s, 0), rolled[0:8])
        shifted = jnp.concatenate([head, rolled[8:]], axis=0)
        acc = acc + shifted * cw_ref[GDN_CONV - 1 - s:GDN_CONV - s, :]
    prev_scr[...] = x[tc - 8:tc]
    xc = _silu(acc)

    sm = sm_ref[0]
    g_raw = _dot_hi(sm, ega_ref[...])
    b_raw = _dot_hi(sm, egb_ref[...])
    g = -jnp.exp(alog_ref[...]) * _softplus(g_raw + dtb_ref[...])
    beta_all = _sigmoid(b_raw)
    gc_all = _dot_ones(tril_ref[...], g)

    ri = lax.broadcasted_iota(jnp.int32, (c, c), 0)
    ci = lax.broadcasted_iota(jnp.int32, (c, c), 1)
    causal = ci <= ri
    strict = ci < ri

    nblk = tc // c
    a_l, attn_l, rhs_l, qd_l, kd_l, egl_l = [], [], [], [], [], []
    for h in range(GDN_HEADS):
        ln = slice(h * GDN_DH, (h + 1) * GDN_DH)
        qh = xc[:, h * GDN_DH:(h + 1) * GDN_DH]
        kh = xc[:, w + h * GDN_DH:w + (h + 1) * GDN_DH]
        qh = qh * lax.rsqrt(jnp.sum(qh * qh, axis=-1, keepdims=True) + EPS) * (GDN_DH ** -0.5)
        kh = kh * lax.rsqrt(jnp.sum(kh * kh, axis=-1, keepdims=True) + EPS)
        vh = xc[:, 2 * w + h * GDN_DH:2 * w + (h + 1) * GDN_DH]
        gch = gc_all[:, ln]
        gct = gch.T
        egc = jnp.exp(gch)
        bh = beta_all[:, ln]
        for n in range(nblk):
            sl = slice(n * c, (n + 1) * c)
            q, k, v, gc, be = qh[sl], kh[sl], vh[sl], gch[sl], bh[sl]
            decay = jnp.exp(jnp.where(causal, gc - gct[:, sl], NEG))
            kb = k * be
            kk = _dot_nt(jnp.concatenate([kb, q], axis=0).astype(BF16), k.astype(BF16))
            a_l.append(jnp.where(strict, kk[:c] * decay, 0.0))
            attn_l.append(jnp.where(causal, kk[c:] * decay, 0.0))
            rhs_l.append(jnp.concatenate([v * be, kb * egc[sl]], axis=1))
            gl = gc[c - 1:c, :]
            qd_l.append(q * egc[sl])
            kd_l.append(k * jnp.exp(gl - gc))
            egl_l.append(jnp.exp(gl))
    uw_l = _tril_solve(a_l, rhs_l, ri, ci)

    states = [s_scr[h] for h in range(GDN_HEADS)]
    for n in range(nblk):
        sl = slice(n * c, (n + 1) * c)
        idx = [h * nblk + n for h in range(GDN_HEADS)]
        ws = [_mm(jnp.concatenate([uw_l[i][:, GDN_DH:], qd_l[i]], axis=0), states[h])
              for h, i in enumerate(idx)]
        v_new = [uw_l[i][:, :GDN_DH] - ws[h][:c] for h, i in enumerate(idx)]
        o = [ws[h][c:] + _mm(attn_l[i], v_new[h]) for h, i in enumerate(idx)]
        states = [states[h] * egl_l[i] + _dot_tn(kd_l[i].astype(BF16), v_new[h].astype(BF16))
                  for h, i in enumerate(idx)]
        for h in range(GDN_HEADS):
            ln = slice(h * GDN_DH, (h + 1) * GDN_DH)
            on = o[h] * lax.rsqrt(jnp.mean(o[h] * o[h], axis=-1, keepdims=True) + EPS) * on_ref[...]
            o_ref[0, sl, ln] = (on * _silu(z_ref[0, sl, ln])).astype(o_ref.dtype)
    for h in range(GDN_HEADS):
        s_scr[h] = states[h]


def gated_delta_net(x, small, z, conv_w, a_log, dt_bias, on_gain, tc=512):
    b, t, _ = x.shape
    w = GDN_W
    ega = np.zeros((LANES, w), np.float32)
    egb = np.zeros((LANES, w), np.float32)
    for h in range(GDN_HEADS):
        ega[FOX_HEADS + h, h * GDN_DH:(h + 1) * GDN_DH] = 1.0
        egb[FOX_HEADS + GDN_HEADS + h, h * GDN_DH:(h + 1) * GDN_DH] = 1.0
    alog = jnp.repeat(a_log, GDN_DH).reshape(1, w)
    dtb = jnp.repeat(dt_bias, GDN_DH).reshape(1, w)
    idx = np.arange(tc)
    tril = ((idx[:, None] >= idx[None, :]) & (idx[:, None] // GDN_BLOCK == idx[None, :] // GDN_BLOCK))
    row = lambda n: pl.BlockSpec((1, tc, n), lambda i, j: (i, j, 0))
    return pl.pallas_call(
        functools.partial(_gdn_kernel, tc=tc),
        grid=(b, t // tc),
        in_specs=[row(3 * w), row(LANES), row(w), _full((GDN_CONV, 3 * w)), _full((LANES, w)), _full((LANES, w)),
                  _full((1, w)), _full((1, w)), _full((1, GDN_DH)), _full((tc, tc))],
        out_specs=row(w),
        out_shape=jax.ShapeDtypeStruct((b, t, w), BF16),
        scratch_shapes=[pltpu.VMEM((GDN_HEADS, GDN_DH, GDN_DH), F32), pltpu.VMEM((8, 3 * w), F32)],
        compiler_params=_cparams(("arbitrary", "arbitrary")),
        name="gdn",
    )(x, small, z, conv_w, jnp.asarray(ega), jnp.asarray(egb), alog, dtb, on_gain.reshape(1, GDN_DH),
      jnp.asarray(tril.astype(np.float32), dtype=BF16))


def even_mixer(h, w_in, fox_fb, fox_qn, fox_kn, gdn_conv, gdn_a_log, gdn_dt_bias, gdn_on):
    cuts = np.cumsum((0,) + EV_SIZES)
    col = lambda i: w_in[:, cuts[i]:cuts[i + 1]]
    bd = _block_diag_ones(FOX_W, FOX_DH)
    ep = _head_norm_epilogue(FOX_DH)
    qg = (jnp.tile(fox_qn, FOX_HEADS) * (FOX_DH ** -0.5 * LOG2E)).reshape(1, FOX_W)
    kg = jnp.tile(fox_kn, FOX_HEADS).reshape(1, FOX_W)
    w_small = jnp.zeros((w_in.shape[0], LANES), F32)
    w_small = w_small.at[:, 0:8].set(col(3)).at[:, 8:12].set(col(5)).at[:, 12:16].set(col(6))
    fq, fk, fv, small, gqkv, gz = proj_multi(
        h, [(col(0), BF16, ep, (bd, qg)), (col(1), BF16, ep, (bd, kg)), (col(2), BF16, None, ()),
            (w_small, F32, None, ()), (col(4), F32, None, ()), (col(7), F32, None, ())], name="proj_even")
    feat = fox_decay(small, fox_fb)
    o_fox = fox_attention(fq, fk, fv, feat)
    o_gdn = gated_delta_net(gqkv, small, gz, gdn_conv, gdn_a_log, gdn_dt_bias, gdn_on)
    return o_fox, o_gdn


def _t5_bucket_np(dist):
    n = np.maximum(dist, 0)
    exact = REL_BUCKETS // 2
    nf = np.maximum(n, 1).astype(np.float32)
    large = exact + (np.log(nf / np.float32(exact)) / np.float32(math.log(REL_MAX_DIST / exact))
                     * np.float32(REL_BUCKETS - exact)).astype(np.int32)
    large = np.minimum(large, REL_BUCKETS - 1)
    return np.where(n < exact, n, large)


def _bias_kernel(tbl_ref, bucket_ref, o_ref):
    h = pl.program_id(0)
    bucket = bucket_ref[...]
    acc = jnp.full(bucket.shape, NEG, F32)
    for b in range(REL_BUCKETS):
        acc = jnp.where(bucket == b, tbl_ref[b, h], acc)
    o_ref[0] = acc


def _bias_table(rel_bias, dist, valid):
    shifted = (rel_bias - rel_bias[REL_BUCKETS - 1:REL_BUCKETS]) * LOG2E
    bucket = np.where(valid, _t5_bucket_np(dist), -1).astype(np.int32)
    rows, cols = int(np.prod(bucket.shape[:-1])), bucket.shape[-1]
    nh = rel_bias.shape[1]
    tb = pl.pallas_call(
        _bias_kernel,
        grid=(nh,),
        in_specs=[pl.BlockSpec(memory_space=pltpu.SMEM), _full((rows, cols))],
        out_specs=pl.BlockSpec((1, rows, cols), lambda h: (h, 0, 0)),
        out_shape=jax.ShapeDtypeStruct((nh, rows, cols), F32),
        compiler_params=_cparams(("arbitrary",)),
        name="t5_bias",
    )(shifted, jnp.asarray(bucket.reshape(rows, cols)))
    return tb.reshape((nh,) + bucket.shape)


def _cmp_kernel(r_ref, pos_ref, w1_ref, w2_ref, kn_ref, o_ref):
    m = r_ref.shape[3]
    half = r_ref.shape[4]
    r = r_ref[0, 0, 0].astype(BF16)
    a = _dot(r, w1_ref[0, :half, :])
    bm = _dot(r, w1_ref[0, half:, :])
    c = _dot(pos_ref[0].astype(BF16), w1_ref[0])
    hid = a + pltpu.roll(bm, m - 1, 0) + c[0:1, :]
    out = _dot(_silu(hid).astype(BF16), w2_ref[0])
    normed = out * lax.rsqrt(jnp.mean(out * out, axis=-1, keepdims=True) + EPS) * kn_ref[...]
    o_ref[0, 0, 0] = jnp.where(pl.program_id(0) == 0, normed, out).astype(o_ref.dtype)


def nsa_compress(kcvc, pos, w1, w2, kn):
    b, t, _ = kcvc.shape
    m = t // CMP_STRIDE
    half = CMP_STRIDE * NSA_DH
    r = kcvc.reshape(b, m, CMP_STRIDE, 2, NSA_KV_HEADS, NSA_DH).transpose(3, 0, 4, 1, 2, 5).reshape(2, b, 2, m, half)
    posf = jnp.zeros((2, 8, 2 * half), F32).at[:, 0].set(pos.reshape(2, 2 * half))
    w2d = jnp.concatenate([w2, w2], axis=-1).astype(BF16)
    knd = jnp.tile(kn, 2).reshape(1, LANES)
    return pl.pallas_call(
        _cmp_kernel,
        grid=(2, b, NSA_KV_HEADS),
        in_specs=[pl.BlockSpec((1, 1, 1, m, half), lambda s, i, k: (s, i, k, 0, 0)),
                  pl.BlockSpec((1, 8, 2 * half), lambda s, i, k: (s, 0, 0)),
                  pl.BlockSpec((1, 2 * half, CMP_HIDDEN), lambda s, i, k: (s, 0, 0)),
                  pl.BlockSpec((1, CMP_HIDDEN, LANES), lambda s, i, k: (s, 0, 0)),
                  _full((1, LANES))],
        out_specs=pl.BlockSpec((1, 1, 1, m, LANES), lambda s, i, k: (s, i, k, 0, 0)),
        out_shape=jax.ShapeDtypeStruct((2, b, NSA_KV_HEADS, m, LANES), BF16),
        compiler_params=_cparams(("arbitrary", "arbitrary", "arbitrary")),
        name="nsa_compress",
    )(r, posf, w1.astype(BF16), w2d, knd)


def _dot_split(a, b):
    hi = a.astype(BF16)
    lo = (a - hi.astype(F32)).astype(BF16)
    return _dot(hi, b) + _dot(lo, b)


def _head_q(q_ref, hh, lo):
    blk = q_ref[0, :, (hh // 2) * LANES:(hh // 2 + 1) * LANES]
    keep = lo if hh % 2 == 0 else jnp.logical_not(lo)
    return jnp.where(keep, blk, jnp.zeros_like(blk))


def _pair_heads(o, lo):
    return jnp.concatenate([jnp.where(lo, o[0], o[1]), jnp.where(lo, o[2], o[3])], axis=1)


def _nsa_sel_kernel(q_ref, kc_ref, vc_ref, ov_ref, bt_ref, o_ref, sel_ref, *, tq, nband, n_slc):
    i = pl.program_id(2)
    ncp = kc_ref.shape[3]
    nsp = ov_ref.shape[1]
    per = tq // CMP_STRIDE
    var = jnp.minimum(i, 1)
    bs = pl.multiple_of(per * jnp.maximum(i - 1, 0), per)
    lo = lax.broadcasted_iota(jnp.int32, (1, LANES), 1) < NSA_DH
    kc = kc_ref[0, 0, 0]
    vc = vc_ref[0, 0, 0]
    kcb = kc_ref[0, 0, 0, pl.ds(bs, nband), :]
    vcb = vc_ref[0, 0, 0, pl.ds(bs, nband), :]
    far_ok = lax.broadcasted_iota(jnp.int32, (1, ncp), 1) < per * (i - 1)
    hs = range(NSA_GROUP)
    qh = [_head_q(q_ref, hh, lo) for hh in hs]
    s_far = [jnp.where(far_ok, _dot_nt(q, kc), NEG) for q in qh]
    s_band = [_dot_nt(qh[hh], kcb) + bt_ref[var, hh] for hh in hs]
    m = [jnp.maximum(jnp.max(a, axis=1, keepdims=True), jnp.max(b, axis=1, keepdims=True))
         for a, b in zip(s_far, s_band)]
    m = [jnp.where(x < 0.5 * NEG, 0.0, x) for x in m]
    p_far = [jnp.exp2(a - x) for a, x in zip(s_far, m)]
    p_band = [jnp.exp2(b - x) for b, x in zip(s_band, m)]
    l = [jnp.sum(a, axis=1, keepdims=True) + jnp.sum(b, axis=1, keepdims=True) for a, b in zip(p_far, p_band)]
    inv = [1.0 / jnp.where(x == 0.0, 1.0, x) for x in l]
    outs = [(_dot(a.astype(BF16), vc) + _dot(b.astype(BF16), vcb)) * x for a, b, x in zip(p_far, p_band, inv)]
    ps_far = p_far[0] * inv[0]
    ps_band = p_band[0] * inv[0]
    for hh in range(1, NSA_GROUP):
        ps_far = ps_far + p_far[hh] * inv[hh]
        ps_band = ps_band + p_band[hh] * inv[hh]
    o_ref[0] = _pair_heads(outs, lo).astype(o_ref.dtype)

    imp = _dot_split(ps_far, ov_ref[...]) + _dot_split(ps_band, ov_ref[pl.ds(bs, nband), :])
    blk = lax.broadcasted_iota(jnp.int32, (1, nsp), 1)
    blk_f = blk.astype(F32)
    qpos = i * tq + lax.broadcasted_iota(jnp.int32, (tq, 1), 0)
    cur = lax.shift_right_logical(qpos, int(math.log2(SLC_LEN)))
    forced = (blk == 0) | (blk == cur) | (blk == cur - 1)
    work = jnp.where(forced, -jnp.inf, jnp.where(blk <= cur, imp, NEG))
    work = jnp.where(blk < n_slc, work, -jnp.inf)
    ngrp = 4
    rg = tq // ngrp
    works = [work[r * rg:(r + 1) * rg] for r in range(ngrp)]
    sels = [jnp.where(forced[r * rg:(r + 1) * rg], 1.0, jnp.zeros((rg, nsp), F32)) for r in range(ngrp)]
    for _ in range(max(min(SLC_TOPK, n_slc) - 3, 0)):
        ms = [jnp.max(w, axis=1, keepdims=True) for w in works]
        firsts = [jnp.min(jnp.where(w == m, blk_f, float(nsp)), axis=1, keepdims=True) for w, m in zip(works, ms)]
        picks = [blk_f == f for f in firsts]
        sels = [jnp.where(p, 1.0, s) for p, s in zip(picks, sels)]
        works = [jnp.where(p, -jnp.inf, w) for p, w in zip(picks, works)]
    sel = jnp.concatenate(sels, axis=0)
    for sup in range(sel_ref.shape[2]):
        col = sel[:, (sup // 2) * LANES:(sup // 2 + 1) * LANES]
        if sup % 2 == 0:
            col = pltpu.roll(col, SUP_BLOCKS, 1)
        sel_ref[0, 0, sup] = jnp.where(lo, 0.0, jnp.where(col > 0.5, 0.0, NEG)).astype(sel_ref.dtype)


def nsa_select(q, cmp_kv, rel_bias, tq=512):
    b, t, _ = q.shape
    ncp = t // CMP_STRIDE
    n_cmp = ncp - 1
    n_slc = t // SLC_LEN
    nsp = max(LANES, n_slc)
    nsup = max(1, n_slc // SUP_BLOCKS)
    per = tq // CMP_STRIDE
    nband = 2 * per
    n = np.arange(ncp)[:, None]
    s = np.arange(nsp)[None, :]
    ov = ((CMP_STRIDE * n < SLC_LEN * s + SLC_LEN) & (CMP_STRIDE * n + CMP_LEN > SLC_LEN * s)
          & (n < n_cmp) & (s < n_slc)).astype(np.float32)
    qi = np.arange(tq)[:, None]
    nj = np.arange(nband)[None, :]
    end = CMP_STRIDE * nj + CMP_LEN - 1
    dist = np.stack([qi - end, tq + qi - end])
    bt = _bias_table(rel_bias, dist, dist >= 0)
    bt = bt.reshape(NSA_KV_HEADS, NSA_GROUP, 2, tq, nband).transpose(0, 2, 1, 3, 4)
    bt = bt.reshape(NSA_KV_HEADS * 2, NSA_GROUP, tq, nband)
    gw = NSA_GROUP * NSA_DH
    return pl.pallas_call(
        functools.partial(_nsa_sel_kernel, tq=tq, nband=nband, n_slc=n_slc),
        grid=(b, NSA_KV_HEADS, t // tq),
        in_specs=[pl.BlockSpec((1, tq, gw), lambda bi, k, i: (bi, i, k)),
                  pl.BlockSpec((1, 1, 1, ncp, LANES), lambda bi, k, i: (0, bi, k, 0, 0)),
                  pl.BlockSpec((1, 1, 1, ncp, LANES), lambda bi, k, i: (1, bi, k, 0, 0)),
                  _full((ncp, nsp)),
                  pl.BlockSpec((2, NSA_GROUP, tq, nband), lambda bi, k, i: (k, 0, 0, 0))],
        out_specs=[pl.BlockSpec((1, tq, gw), lambda bi, k, i: (bi, i, k)),
                   pl.BlockSpec((1, 1, nsup, tq, LANES), lambda bi, k, i: (bi, k, 0, i, 0))],
        out_shape=[jax.ShapeDtypeStruct((b, t, NSA_W), BF16),
                   jax.ShapeDtypeStruct((b, NSA_KV_HEADS, nsup, t, LANES), BF16)],
        compiler_params=_cparams(("arbitrary", "arbitrary", "arbitrary")),
        name="nsa_select",
    )(q, cmp_kv, cmp_kv, jnp.asarray(ov, dtype=BF16), bt)


def _nsa_main_kernel(q_ref, ks_ref, vs_ref, kw0_ref, kw1_ref, kw2_ref, vw0_ref, vw1_ref, vw2_ref, sel_ref, oh_ref,
                     ocmp_ref, gate_ref, tb_ref, wm_ref, eg_ref, o_ref, *, tq, wide):
    i = pl.program_id(2)
    g = NSA_GROUP
    lo = lax.broadcasted_iota(jnp.int32, (1, LANES), 1) < NSA_DH

    def head_low(hh):
        blk = q_ref[0, :, (hh // 2) * LANES:(hh // 2 + 1) * LANES]
        if hh % 2 == 1:
            blk = pltpu.roll(blk.astype(F32), NSA_DH, 1).astype(BF16)
        return jnp.where(lo, blk, jnp.zeros_like(blk))

    qst = jnp.concatenate([head_low(hh) for hh in range(g)], axis=0)
    causal = (lax.broadcasted_iota(jnp.int32, (tq, tq), 1) <= lax.broadcasted_iota(jnp.int32, (tq, tq), 0))
    one = jnp.ones((1, LANES), BF16)
    sup_keys = SUP_BLOCKS * SLC_LEN
    all_masked = jnp.where(lo, 0.0, NEG).astype(BF16)

    def sel_step(jt, carry, near, tw):
        m, acc = carry
        key0 = jnp.maximum(jt, 0) * tw
        start = pl.multiple_of(key0, tw)
        within = pl.multiple_of(key0 & (sup_keys - 1), tw)
        mq = sel_ref[0, 0, lax.shift_right_logical(key0, int(math.log2(sup_keys)))]
        mq = jnp.where(jt >= 0, mq, all_masked)
        q_ext = qst + jnp.concatenate([mq] * g, axis=0)
        kt = jnp.where(lo, ks_ref[0, pl.ds(start, tw), :], oh_ref[pl.ds(within, tw), :])
        vt = jnp.where(lo, vs_ref[0, pl.ds(start, tw), :], one)
        s = _dot_nt(q_ext, kt)
        if near is not None:
            s = s.reshape(g, tq, tw) + tb_ref[:, :, near * tq:(near + 1) * tq]
            if near == 2:
                s = jnp.where(causal[None], s, NEG)
            s = s.reshape(g * tq, tw)
        m_new = jnp.maximum(m, jnp.max(s, axis=1, keepdims=True))
        p = jnp.exp2(s - m_new)
        acc = jnp.exp2(m - m_new) * acc + _dot(p.astype(BF16), vt)
        return m_new, acc

    n_far = jnp.maximum(i - 2, 0)
    n_wide = n_far // wide
    carry = (jnp.full((g * tq, 1), NEG, F32), jnp.zeros((g * tq, LANES), F32))
    carry = lax.fori_loop(0, n_wide, lambda j, c: sel_step(j, c, None, wide * tq), carry)
    done = n_wide * wide
    part = wide // 2
    while part >= 1:
        carry = lax.cond((n_far & part) != 0, lambda c, d=done, w=part: sel_step(d // w, c, None, w * tq),
                         lambda c: c, carry)
        done = done + (n_far & part)
        part //= 2
    for near in range(3):
        carry = sel_step(i - 2 + near, carry, near, tq)
    o_slc = (carry[1] / pltpu.roll(carry[1], NSA_DH, 1)).reshape(g, tq, LANES)

    var = jnp.minimum(i, 2)
    kws = (kw0_ref, kw1_ref, kw2_ref)
    vws = (vw0_ref, vw1_ref, vw2_ref)
    sw = []
    for near in range(3):
        s = _dot_nt(qst, kws[near][0]).reshape(g, tq, tq)
        s = s + tb_ref[:, :, near * tq:(near + 1) * tq] + wm_ref[var, :, near * tq:(near + 1) * tq][None]
        sw.append(s.reshape(g * tq, tq))
    m = jnp.maximum(jnp.maximum(jnp.max(sw[0], axis=1, keepdims=True), jnp.max(sw[1], axis=1, keepdims=True)),
                    jnp.max(sw[2], axis=1, keepdims=True))
    acc = jnp.zeros((g * tq, LANES), F32)
    for near in range(3):
        p = jnp.exp2(sw[near] - m)
        acc = acc + _dot(p.astype(BF16), jnp.where(lo, vws[near][0], one))
    o_win = (acc / pltpu.roll(acc, NSA_DH, 1)).reshape(g, tq, LANES)

    pair = lambda o: jnp.concatenate([jnp.where(lo, o[0], pltpu.roll(o[1], NSA_DH, 1)),
                                      jnp.where(lo, o[2], pltpu.roll(o[3], NSA_DH, 1))], axis=1)
    gates = _dot_hi(_sigmoid(gate_ref[0]), eg_ref[0])
    gw = g * NSA_DH
    out = (gates[:, 0:gw] * ocmp_ref[0].astype(F32)
           + gates[:, gw:2 * gw] * pair(o_slc) + gates[:, 2 * gw:3 * gw] * pair(o_win))
    o_ref[0] = out.astype(o_ref.dtype)


def nsa_main(q, ksw, vsw, sel, o_cmp, small, rel_bias, tq=256, wide=8):
    b, t, _ = q.shape
    nsup = sel.shape[2]
    g = NSA_GROUP
    gw = g * NSA_DH
    sup_keys = SUP_BLOCKS * SLC_LEN
    oh = np.zeros((sup_keys, LANES), np.float32)
    oh[np.arange(sup_keys), NSA_DH + np.arange(sup_keys) // SLC_LEN] = 1.0
    qi = np.arange(tq)[:, None]
    c = np.arange(3 * tq)[None, :]
    dist = qi + 2 * tq - c
    tb = _bias_table(rel_bias, dist, np.ones_like(dist, bool))
    wm = np.zeros((3, tq, 3 * tq), np.float32)
    for var in range(3):
        exists = c >= tq * (2 - var)
        wm[var] = np.where((dist >= 0) & (dist < WINDOW) & exists, 0.0, NEG)
    eg = np.zeros((NSA_KV_HEADS, LANES, 3 * gw), np.float32)
    for k in range(NSA_KV_HEADS):
        for hh in range(g):
            for br in range(3):
                eg[k, (k * g + hh) * 3 + br, br * gw + hh * NSA_DH:br * gw + (hh + 1) * NSA_DH] = 1.0
    near = lambda off, col: pl.BlockSpec(
        (1, tq, LANES), lambda bi, k, i: (bi, jnp.maximum(i - off, 0), col + k))
    return pl.pallas_call(
        functools.partial(_nsa_main_kernel, tq=tq, wide=wide),
        grid=(b, NSA_KV_HEADS, t // tq),
        in_specs=[pl.BlockSpec((1, tq, gw), lambda bi, k, i: (bi, i, k)),
                  pl.BlockSpec((1, t, LANES), lambda bi, k, i: (bi, 0, k)),
                  pl.BlockSpec((1, t, LANES), lambda bi, k, i: (bi, 0, k)),
                  near(2, 2), near(1, 2), near(0, 2), near(2, 2), near(1, 2), near(0, 2),
                  pl.BlockSpec((1, 1, nsup, tq, LANES), lambda bi, k, i: (bi, k, 0, i, 0)),
                  _full((sup_keys, LANES)),
                  pl.BlockSpec((1, tq, gw), lambda bi, k, i: (bi, i, k)),
                  pl.BlockSpec((1, tq, LANES), lambda bi, k, i: (bi, i, 0)),
                  pl.BlockSpec((g, tq, 3 * tq), lambda bi, k, i: (k, 0, 0)),
                  _full((3, tq, 3 * tq)),
                  pl.BlockSpec((1, LANES, 3 * gw), lambda bi, k, i: (k, 0, 0))],
        out_specs=pl.BlockSpec((1, tq, gw), lambda bi, k, i: (bi, i, k)),
        out_shape=jax.ShapeDtypeStruct((b, t, NSA_W), BF16),
        compiler_params=_cparams(("arbitrary", "arbitrary", "arbitrary")),
        name="nsa_main",
    )(q, ksw, vsw, ksw, ksw, ksw, vsw, vsw, vsw, sel, jnp.asarray(oh, dtype=BF16), o_cmp, small, tb,
      jnp.asarray(wm), jnp.asarray(eg))


def _gla_kernel(qk_ref, v_ref, r_ref, sm_ref, wg_ref, bg_ref, on_ref, tril_ref, o_ref, s_scr, *, tc):
    c = GLA_CHUNK

    @pl.when(pl.program_id(1) == 0)
    def _():
        s_scr[...] = jnp.zeros_like(s_scr)

    kw = GLA_KW
    log_a = _log_sigmoid(_dot_hi(sm_ref[0], wg_ref[...]) + bg_ref[...]) * (1.0 / GLA_TAU)
    gcum = _dot_ones(tril_ref[...], log_a)
    q = qk_ref[0, :, 0:kw] * (GLA_DK ** -0.5)
    k = qk_ref[0, :, kw:2 * kw]
    q_dec = (q * jnp.exp(gcum)).astype(BF16)
    k_inv = (k * jnp.exp(-gcum)).astype(BF16)
    ri = lax.broadcasted_iota(jnp.int32, (c, c), 0)
    ci = lax.broadcasted_iota(jnp.int32, (c, c), 1)
    causal = ci <= ri
    lo = lax.broadcasted_iota(jnp.int32, (1, LANES), 1) < GLA_DK
    nchunk = tc // c
    heads = range(GLA_HEADS)
    zero = jnp.zeros((c, LANES), BF16)

    qm, vb, kd, egl = {}, {}, {}, {}
    for n in range(nchunk):
        sl = slice(n * c, (n + 1) * c)
        gl = gcum[n * c + c - 1:n * c + c, :]
        kdn = (k[sl] * jnp.exp(gl - gcum[sl])).astype(BF16)
        for h in heads:
            pr = slice((h // 2) * LANES, (h // 2 + 1) * LANES)
            keep = lo if h % 2 == 0 else jnp.logical_not(lo)
            qm[n, h] = jnp.where(keep, q_dec[sl, pr], zero)
            vb[n, h] = v_ref[0, sl, h * GLA_DV:(h + 1) * GLA_DV].astype(BF16)
            kd[n, h] = kdn[:, pr]
            egl[n, h] = jnp.exp(gl[:, pr])
    idx = [(n, h) for n in range(nchunk) for h in heads]
    attn = {i: jnp.where(causal, _dot_nt(qm[i], k_inv[i[0] * c:(i[0] + 1) * c, (i[1] // 2) * LANES:
                                                       (i[1] // 2 + 1) * LANES]), 0.0).astype(BF16) for i in idx}
    o_intra = {i: _dot(attn[i], vb[i]) for i in idx}
    kv = {i: _dot_tn(vb[i], kd[i]) for i in idx}

    st = [s_scr[h] for h in heads]
    for n in range(nchunk):
        sl = slice(n * c, (n + 1) * c)
        o = [o_intra[n, h] + _dot_nt(qm[n, h], st[h].astype(BF16)) for h in heads]
        st = [st[h] * egl[n, h] + kv[n, h] for h in heads]
        for h in heads:
            on = o[h] * lax.rsqrt(jnp.mean(o[h] * o[h], axis=-1, keepdims=True) + EPS) * on_ref[...]
            o_ref[0, sl, h * GLA_DV:(h + 1) * GLA_DV] = (
                on * _silu(r_ref[0, sl, h * GLA_DV:(h + 1) * GLA_DV])).astype(o_ref.dtype)
    for h in heads:
        s_scr[h] = st[h]


def gated_linear_attention(qkvr, small, wg_up, bg, on_gain, tc=512):
    b, t, _ = qkvr.shape
    wg = jnp.zeros((LANES, GLA_KW), F32).at[3 * NSA_HEADS:3 * NSA_HEADS + GLA_GATE_RANK].set(wg_up)
    idx = np.arange(tc)
    tril = ((idx[:, None] >= idx[None, :]) & (idx[:, None] // GLA_CHUNK == idx[None, :] // GLA_CHUNK))
    return pl.pallas_call(
        functools.partial(_gla_kernel, tc=tc),
        grid=(b, t // tc),
        in_specs=[pl.BlockSpec((1, tc, 2 * GLA_KW), lambda i, j: (i, j, 0)),
                  pl.BlockSpec((1, tc, GLA_W), lambda i, j: (i, j, 1)),
                  pl.BlockSpec((1, tc, GLA_W), lambda i, j: (i, j, 2)),
                  pl.BlockSpec((1, tc, LANES), lambda i, j: (i, j, 0)),
                  _full((LANES, GLA_KW)), _full((1, GLA_KW)), _full((1, GLA_DV)), _full((tc, tc))],
        out_specs=pl.BlockSpec((1, tc, GLA_W), lambda i, j: (i, j, 0)),
        out_shape=jax.ShapeDtypeStruct((b, t, GLA_W), BF16),
        scratch_shapes=[pltpu.VMEM((GLA_HEADS, GLA_DV, LANES), F32)],
        compiler_params=_cparams(("arbitrary", "arbitrary")),
        name="gla",
    )(qkvr, qkvr, qkvr, small, wg, bg.reshape(1, GLA_KW), on_gain.reshape(1, GLA_DV),
      jnp.asarray(tril.astype(np.float32), dtype=BF16))


def odd_mixer(h, w_in, nsa_qn, nsa_kn, nsa_pos, nsa_cmp_w1, nsa_cmp_w2, gla_wg_up, gla_bg, gla_on, rel_bias):
    cuts = np.cumsum((0,) + OD_SIZES)
    col = lambda i: w_in[:, cuts[i]:cuts[i + 1]]
    dup = lambda a: jnp.concatenate([a[:, :NSA_DH], a[:, :NSA_DH], a[:, NSA_DH:], a[:, NSA_DH:]], axis=1)
    ep = _head_norm_epilogue(NSA_DH)
    bd = _block_diag_ones(NSA_W, NSA_DH)
    qg = (jnp.tile(nsa_qn, NSA_HEADS) * (NSA_DH ** -0.5 * LOG2E)).reshape(1, NSA_W)
    kg = jnp.tile(nsa_kn, NSA_HEADS).reshape(1, NSA_W)
    w_small = jnp.zeros((w_in.shape[0], LANES), F32)
    w_small = w_small.at[:, 0:24].set(col(7)).at[:, 24:40].set(col(11))
    nq, kcvc, ksw, vsw, small, qkvr = proj_multi(
        h, [(col(0), BF16, ep, (bd, qg)),
            (jnp.concatenate([col(1), col(2)], axis=1), F32, None, ()),
            (jnp.concatenate([dup(col(3)), dup(col(5))], axis=1), BF16, ep, (bd, kg)),
            (jnp.concatenate([dup(col(4)), dup(col(6))], axis=1), BF16, None, ()),
            (w_small, F32, None, ()),
            (jnp.concatenate([col(8), col(9), col(10), col(12)], axis=1), F32, None, ())], name="proj_odd")
    cmp_kv = nsa_compress(kcvc, nsa_pos, nsa_cmp_w1, nsa_cmp_w2, nsa_kn)
    o_cmp, sel = nsa_select(nq, cmp_kv, rel_bias)
    o_nsa = nsa_main(nq, ksw, vsw, sel, o_cmp, small, rel_bias)
    o_gla = gated_linear_attention(qkvr, small, gla_wg_up, gla_bg, gla_on)
    return o_nsa, o_gla


MOE_TM = 256
MOE_ROWS = 512


def _first_index(mask_val, idx, big, axis):
    return jnp.min(jnp.where(mask_val, idx, big), axis=axis, keepdims=True)


def _route_kernel(h_ref, rt_ref, b_ref, up_ref, eid_ref, rank_ref, w_ref, cnt_ref, run):
    tm = h_ref.shape[0]
    ne = N_EXPERTS
    gsz = ne // N_GROUPS

    @pl.when(pl.program_id(0) == 0)
    def _():
        run[...] = jnp.zeros_like(run)

    scores = _sigmoid(_dot_nt(rt_ref[...], h_ref[...], HI))
    biased = scores + b_ref[...]
    b3 = biased.reshape(N_GROUPS, gsz, tm)
    i3 = lax.broadcasted_iota(jnp.int32, (1, gsz, 1), 1).astype(F32)
    m1 = jnp.max(b3, axis=1, keepdims=True)
    f1 = _first_index(b3 == m1, i3, float(gsz), 1)
    m2 = jnp.max(jnp.where(i3 == f1, -jnp.inf, b3), axis=1, keepdims=True)
    gs = (m1 + m2).reshape(N_GROUPS, tm)
    gidx = lax.broadcasted_iota(jnp.int32, (N_GROUPS, 1), 0).astype(F32)
    gmask = jnp.zeros((N_GROUPS, tm), F32)
    for _ in range(TOPK_GROUPS):
        m = jnp.max(gs, axis=0, keepdims=True)
        pick = gidx == _first_index(gs == m, gidx, float(N_GROUPS), 0)
        gmask = jnp.where(pick, 1.0, gmask)
        gs = jnp.where(pick, -jnp.inf, gs)
    emask = jnp.broadcast_to(gmask.reshape(N_GROUPS, 1, tm), (N_GROUPS, gsz, tm)).reshape(ne, tm)
    work = jnp.where(emask > 0.5, biased, -jnp.inf)
    eidx = lax.broadcasted_iota(jnp.int32, (ne, 1), 0).astype(F32)
    picks, eids, ws = [], [], []
    for _ in range(TOP_K):
        m = jnp.max(work, axis=0, keepdims=True)
        first = _first_index(work == m, eidx, float(ne), 0)
        pick = eidx == first
        picks.append(pick)
        eids.append(first)
        ws.append(jnp.sum(jnp.where(pick, scores, 0.0), axis=0, keepdims=True))
        work = jnp.where(pick, -jnp.inf, work)
    wsum = ws[0]
    for k in range(1, TOP_K):
        wsum = wsum + ws[k]
    chosen = jnp.zeros((ne, tm), F32)
    for pick in picks:
        chosen = jnp.where(pick, 1.0, chosen)
    pos = run[...] + _dot(chosen.astype(BF16), up_ref[...])
    run[...] = run[...] + jnp.sum(chosen, axis=1, keepdims=True)
    cnt_ref[...] = run[...]
    row = lax.broadcasted_iota(jnp.int32, (8, 1), 0)
    eid_o = jnp.zeros((8, tm), F32)
    rank_o = jnp.zeros((8, tm), F32)
    w_o = jnp.zeros((LANES, tm), F32)
    rowl = lax.broadcasted_iota(jnp.int32, (LANES, 1), 0)
    for k in range(TOP_K):
        rk = jnp.sum(jnp.where(picks[k], pos, 0.0), axis=0, keepdims=True)
        eid_o = jnp.where(row == k, eids[k], eid_o)
        rank_o = jnp.where(row == k, rk, rank_o)
        w_o = jnp.where(rowl == k, ws[k] / wsum * ROUTE_SCALE, w_o)
    eid_ref[0] = eid_o.astype(jnp.int32)
    rank_ref[0] = rank_o.astype(jnp.int32)
    w_ref[...] = w_o.T


def moe_route(h2, router, e_bias, tm=MOE_TM):
    nt, d = h2.shape
    ne = N_EXPERTS
    up = jnp.asarray(np.triu(np.ones((tm, tm), np.float32), 1), dtype=BF16)
    nb = nt // tm
    return pl.pallas_call(
        _route_kernel,
        grid=(nb,),
        in_specs=[pl.BlockSpec((tm, d), lambda i: (i, 0)), _full((ne, d)), _full((ne, 1)), _full((tm, tm))],
        out_specs=[pl.BlockSpec((1, 8, tm), lambda i: (i, 0, 0)),
                   pl.BlockSpec((1, 8, tm), lambda i: (i, 0, 0)),
                   pl.BlockSpec((tm, LANES), lambda i: (i, 0)),
                   _full((ne, 1))],
        out_shape=[jax.ShapeDtypeStruct((nb, 8, tm), jnp.int32), jax.ShapeDtypeStruct((nb, 8, tm), jnp.int32),
                   jax.ShapeDtypeStruct((nt, LANES), F32), jax.ShapeDtypeStruct((ne, 1), F32)],
        scratch_shapes=[pltpu.VMEM((ne, 1), F32)],
        compiler_params=_cparams(("arbitrary",)),
        name="moe_route",
    )(h2, router.T, e_bias.reshape(ne, 1), up)


def _dispatch_kernel(dest_ref, h_ref, xs_ref, sem):
    tm = h_ref.shape[0]

    def copy(t, row):
        return pltpu.make_async_copy(h_ref.at[pl.ds(t, 1), :], xs_ref.at[pl.ds(row, 1), :], sem)

    def issue(t, _):
        for k in range(TOP_K):
            copy(t, dest_ref[0, t, k]).start(priority=k % 2)
        return 0

    def drain(t, _):
        for k in range(TOP_K):
            copy(0, 0).wait()
        return 0

    lax.fori_loop(0, tm, issue, 0, unroll=4)
    lax.fori_loop(0, tm, drain, 0, unroll=4)


def moe_dispatch(h2, dest, tm=MOE_TM):
    nt, d = h2.shape
    return pl.pallas_call(
        _dispatch_kernel,
        grid=(nt // tm,),
        in_specs=[pl.BlockSpec((1, tm, 8), lambda i: (i, 0, 0), memory_space=pltpu.SMEM),
                  pl.BlockSpec((tm, d), lambda i: (i, 0))],
        out_specs=pl.BlockSpec(memory_space=pl.ANY),
        scratch_shapes=[pltpu.SemaphoreType.DMA(())],
        out_shape=jax.ShapeDtypeStruct((nt * TOP_K, d), F32),
        compiler_params=_cparams(("arbitrary",)),
        name="moe_dispatch",
    )(dest, h2)


def _ffn_kernel(blk_ref, exp_ref, lo_ref, hi_ref, first_ref, valid_ref, x_ref, wg_ref, wu_ref, wd_ref, o_ref,
                wg_b, wu_b, wd_b):
    i = pl.program_id(0)
    rows = x_ref.shape[0]

    @pl.when((i == 0) | (exp_ref[i] != exp_ref[jnp.maximum(i - 1, 0)]))
    def _():
        wg_b[...] = wg_ref[0].astype(BF16)
        wu_b[...] = wu_ref[0].astype(BF16)
        wd_b[...] = wd_ref[0].astype(BF16)

    @pl.when(valid_ref[i] == 1)
    def _():
        x = x_ref[...].astype(BF16)
        a = _dot(x, wg_b[...])
        u = _dot(x, wu_b[...])
        y = _dot((_silu(a) * u).astype(BF16), wd_b[...])
        r = blk_ref[i] * rows + lax.broadcasted_iota(jnp.int32, (rows, 1), 0)
        y = jnp.where((r >= lo_ref[i]) & (r < hi_ref[i]), y, 0.0)

        @pl.when(first_ref[i] == 1)
        def _():
            o_ref[...] = y

        @pl.when(first_ref[i] == 0)
        def _():
            o_ref[...] = o_ref[...] + y


def _items_kernel(cnt_ref, starts_ref, blk_ref, exp_ref, lo_ref, hi_ref, first_ref, valid_ref, *, rows, n_items):
    shift = int(math.log2(rows))

    def expert(e, carry):
        start, n = carry
        c = cnt_ref[e]
        starts_ref[e] = start
        end = start + c
        first_blk = lax.shift_right_logical(start, shift)
        n_blk = jnp.where(c > 0, lax.shift_right_logical(jnp.maximum(end - 1, 0), shift) - first_blk + 1, 0)

        def item(k, n):
            b = first_blk + k
            lo = jnp.maximum(start, b * rows)
            blk_ref[n] = b
            exp_ref[n] = e
            lo_ref[n] = lo
            hi_ref[n] = jnp.minimum(end, (b + 1) * rows)
            first_ref[n] = (lo == b * rows).astype(jnp.int32)
            valid_ref[n] = 1
            return n + 1

        return end, lax.fori_loop(0, n_blk, item, n)

    _, total = lax.fori_loop(0, N_EXPERTS, expert, (jnp.int32(0), jnp.int32(0)))
    last = jnp.maximum(total - 1, 0)

    def fill(k, _):
        blk_ref[k] = blk_ref[last]
        exp_ref[k] = exp_ref[last]
        lo_ref[k] = 0
        hi_ref[k] = 0
        first_ref[k] = 0
        valid_ref[k] = 0
        return 0

    lax.fori_loop(total, n_items, fill, 0)


def _ffn_items(counts, n_rows, rows):
    n_items = n_rows // rows + N_EXPERTS - 1
    smem = pl.BlockSpec(memory_space=pltpu.SMEM)
    out = pl.pallas_call(
        functools.partial(_items_kernel, rows=rows, n_items=n_items),
        in_specs=[smem],
        out_specs=[smem] * 7,
        out_shape=[jax.ShapeDtypeStruct((N_EXPERTS,), jnp.int32)]
                  + [jax.ShapeDtypeStruct((n_items,), jnp.int32)] * 6,
        name="moe_items",
    )(counts)
    return out[0], tuple(out[1:])


def moe_ffn_sorted(xs, items, wg, wu, wd, layer, rows=MOE_ROWS):
    n_rows, d = xs.shape
    n_items = items[0].shape[0]
    de = wg.shape[-1]
    return pl.pallas_call(
        _ffn_kernel,
        grid_spec=pltpu.PrefetchScalarGridSpec(
            num_scalar_prefetch=6,
            grid=(n_items,),
            in_specs=[pl.BlockSpec((rows, d), lambda i, blk, e, *_: (blk[i], 0)),
                      pl.BlockSpec((None, 1, d, de), lambda i, blk, e, *_: (layer, e[i], 0, 0)),
                      pl.BlockSpec((None, 1, d, de), lambda i, blk, e, *_: (layer, e[i], 0, 0)),
                      pl.BlockSpec((None, 1, de, d), lambda i, blk, e, *_: (layer, e[i], 0, 0))],
            out_specs=pl.BlockSpec((rows, d), lambda i, blk, e, *_: (blk[i], 0)),
            scratch_shapes=[pltpu.VMEM((d, de), BF16), pltpu.VMEM((d, de), BF16), pltpu.VMEM((de, d), BF16)]),
        out_shape=jax.ShapeDtypeStruct((n_rows, d), F32),
        compiler_params=_cparams(("arbitrary",)),
        name="moe_ffn",
    )(*items, xs, wg, wu, wd)


def _combine_kernel(dest_ref, ys_ref, w_ref, h_ref, x_ref, g_ref, sg_ref, su_ref, sd_ref, *rest):
    if len(rest) == 3:
        nxt, (o_ref, buf, sem) = None, rest
    else:
        nxt, (o_ref, hn_ref, buf, sem) = rest[:3], rest[3:]
    tm = h_ref.shape[0]

    def copy(t, k, row):
        return pltpu.make_async_copy(ys_ref.at[pl.ds(row, 1), :], buf.at[k, pl.ds(t, 1), :], sem)

    def issue(t, _):
        for k in range(TOP_K):
            copy(t, k, dest_ref[0, t, k]).start(priority=k % 2)
        return 0

    def drain(t, _):
        for k in range(TOP_K):
            copy(0, 0, 0).wait()
        return 0

    lax.fori_loop(0, tm, issue, 0, unroll=4)
    hb = h_ref[...].astype(BF16)
    y = _dot((_silu(_dot(hb, sg_ref[...])) * _dot(hb, su_ref[...])).astype(BF16), sd_ref[...])
    lax.fori_loop(0, tm, drain, 0, unroll=4)
    w = w_ref[...]
    for k in range(TOP_K):
        y = y + w[:, k:k + 1] * buf[k]
    x_new = x_ref[...] + g_ref[0] * y
    o_ref[...] = x_new
    if nxt is not None:
        hn_ref[...] = _norm_mod(x_new, nxt[0][...], nxt[1][0], nxt[2][0]).astype(hn_ref.dtype)


def moe_combine(ys, dest, w, h2, x2, gate, sg, su, sd, seq, next_norm=None, tm=MOE_TM):
    nt, d = h2.shape
    ds_ = sg.shape[-1]
    per_b = seq // tm
    tile = lambda: pl.BlockSpec((tm, d), lambda i: (i, 0))
    batch = lambda: pl.BlockSpec((1, 1, d), lambda i: (i // per_b, 0, 0))
    nb = gate.shape[0]
    extra_in, extra_specs = [], []
    out_specs, out_shape = tile(), jax.ShapeDtypeStruct((nt, d), F32)
    if next_norm is not None:
        g_n, sc_n, sh_n = next_norm
        extra_in = [g_n.reshape(1, d), sc_n.reshape(nb, 1, d), sh_n.reshape(nb, 1, d)]
        extra_specs = [_full((1, d)), batch(), batch()]
        out_specs = [tile(), tile()]
        out_shape = [out_shape, jax.ShapeDtypeStruct((nt, d), BF16)]
    return pl.pallas_call(
        _combine_kernel,
        grid=(nt // tm,),
        in_specs=[pl.BlockSpec((1, tm, 8), lambda i: (i, 0, 0), memory_space=pltpu.SMEM),
                  pl.BlockSpec(memory_space=pl.ANY),
                  pl.BlockSpec((tm, LANES), lambda i: (i, 0)), tile(), tile(), batch(),
                  _full((d, ds_)), _full((d, ds_)), _full((ds_, d))] + extra_specs,
        out_specs=out_specs,
        scratch_shapes=[pltpu.VMEM((TOP_K, tm, d), F32), pltpu.SemaphoreType.DMA(())],
        out_shape=out_shape,
        compiler_params=_cparams(("arbitrary",)),
        name="moe_combine",
    )(dest, ys, w, h2, x2, gate, sg.astype(BF16), su.astype(BF16), sd.astype(BF16), *extra_in)


def moe_layer(x, h, gate, router, e_bias, wg, wu, wd, layer, sg, su, sd, next_norm=None):
    b, t, d = x.shape
    nt = b * t
    h2 = h.reshape(nt, d)
    eid, rank, w, counts = moe_route(h2, router, e_bias)
    starts, items = _ffn_items(counts.reshape(-1).astype(jnp.int32), nt * TOP_K, MOE_ROWS)
    hit = eid[..., None] == jnp.arange(N_EXPERTS, dtype=jnp.int32)
    dest = jnp.sum(jnp.where(hit, starts.astype(jnp.int32), 0), axis=-1) + rank
    dest = jnp.swapaxes(dest, 1, 2)
    xs = moe_dispatch(h2, dest)
    ys = moe_ffn_sorted(xs, items, wg, wu, wd, layer)
    out = moe_combine(ys, dest, w, h2, x.reshape(nt, d), gate.reshape(b, 1, d), sg, su, sd, t, next_norm)
    if next_norm is None:
        return out.reshape(b, t, d), None
    return out[0].reshape(b, t, d), out[1].reshape(b, t, d)


def kernel(x, c, ada_w, ada_b, norm_mix, norm_ffn, rel_bias, ev_w_in, ev_w_out, fox_fb, fox_qn, fox_kn, gdn_conv, gdn_a_log, gdn_dt_bias, gdn_on, od_w_in, od_w_out, nsa_qn, nsa_kn, nsa_pos, nsa_cmp_w1, nsa_cmp_w2, gla_wg_up, gla_bg, gla_on, moe_router, moe_bias, moe_wg, moe_wu, moe_wd, sh_wg, sh_wu, sh_wd):
    d = x.shape[-1]
    depth = ada_w.shape[0]
    mod = adaln(c, ada_w, ada_b)
    mods = [[mod[layer, :, i * d:(i + 1) * d] for i in range(6)] for layer in range(depth)]
    h = ln_mod(x, norm_mix[0], mods[0][1], mods[0][0], BF16)
    for layer in range(depth):
        sh1, sc1, g1, sh2, sc2, g2 = mods[layer]
        j = layer // 2
        if layer % 2 == 0:
            y1, y2 = even_mixer(h, ev_w_in[j], fox_fb[j], fox_qn[j], fox_kn[j], gdn_conv[j], gdn_a_log[j],
                                gdn_dt_bias[j], gdn_on[j])
            w_out = ev_w_out[j]
        else:
            y1, y2 = odd_mixer(h, od_w_in[j], nsa_qn[j], nsa_kn[j], nsa_pos[j], nsa_cmp_w1[j], nsa_cmp_w2[j],
                               gla_wg_up[j], gla_bg[j], gla_on[j], rel_bias)
            w_out = od_w_out[j]
        x, h_ffn = out_proj(y1, y2, w_out, x, g1, norm_ffn[layer], sc2, sh2)
        nxt = None if layer + 1 == depth else (norm_mix[layer + 1], mods[layer + 1][1], mods[layer + 1][0])
        x, h = moe_layer(x, h_ffn, g2, moe_router[layer], moe_bias[layer], moe_wg, moe_wu, moe_wd, layer,
                         sh_wg[layer], sh_wu[layer], sh_wd[layer], nxt)
    return x
```

```python
import functools
import math

import numpy as np
import jax
import jax.numpy as jnp
from jax import lax
from jax.experimental import pallas as pl
from jax.experimental.pallas import tpu as pltpu

F32 = jnp.float32
BF16 = jnp.bfloat16
HI = lax.Precision.HIGHEST

EPS = 1e-6
LOG2E = math.log2(math.e)
NEG = -1e30

FOX_HEADS, FOX_DH = 8, 64
GDN_HEADS, GDN_DH, GDN_CONV = 4, 128, 4
NSA_HEADS, NSA_KV_HEADS, NSA_DH = 8, 2, 64
NSA_GROUP = NSA_HEADS // NSA_KV_HEADS
CMP_LEN, CMP_STRIDE, CMP_HIDDEN = 32, 16, 256
SLC_LEN, SLC_TOPK, WINDOW = 64, 16, 512
GLA_HEADS, GLA_DK, GLA_DV, GLA_GATE_RANK, GLA_TAU, GLA_CHUNK = 4, 64, 128, 16, 16.0, 64
REL_BUCKETS, REL_MAX_DIST = 32, 128
N_EXPERTS, TOP_K, D_EXPERT, D_SHARED = 64, 6, 256, 256
N_GROUPS, TOPK_GROUPS, ROUTE_SCALE = 8, 4, 2.5

FOX_W = FOX_HEADS * FOX_DH
GDN_W = GDN_HEADS * GDN_DH
NSA_W = NSA_HEADS * NSA_DH
NSA_KV_W = NSA_KV_HEADS * NSA_DH
GLA_KW = GLA_HEADS * GLA_DK
GLA_W = GLA_HEADS * GLA_DV
EV_SIZES = (FOX_W, FOX_W, FOX_W, FOX_HEADS, 3 * GDN_W, GDN_HEADS, GDN_HEADS, GDN_W)
OD_SIZES = (NSA_W,) + (NSA_KV_W,) * 6 + (3 * NSA_HEADS, GLA_KW, GLA_KW, GLA_W, GLA_GATE_RANK, GLA_W)

LANES = 128
SUP_BLOCKS = 64
VMEM_LIMIT = 56 * 1024 * 1024


def _cparams(sem):
    return pltpu.CompilerParams(dimension_semantics=sem, vmem_limit_bytes=VMEM_LIMIT)


def _full(shape):
    n = len(shape)
    return pl.BlockSpec(shape, lambda *_: (0,) * n)


def _dot(a, b):
    return jnp.dot(a, b, preferred_element_type=F32)


def _dot_hi(a, b):
    return jnp.dot(a, b, precision=HI, preferred_element_type=F32)


def _dot_nt(a, b, precision=None):
    return lax.dot_general(a, b, (((1,), (1,)), ((), ())), precision=precision, preferred_element_type=F32)


def _dot_tn(a, b, precision=None):
    return lax.dot_general(a, b, (((0,), (0,)), ((), ())), precision=precision, preferred_element_type=F32)


def _dot_ones(m, x):
    hi = x.astype(BF16)
    r = x - hi.astype(F32)
    mid = r.astype(BF16)
    low = (r - mid.astype(F32)).astype(BF16)
    return _dot(m, hi) + (_dot(m, mid) + _dot(m, low))


def _sigmoid(x):
    return 1.0 / (1.0 + jnp.exp(-x))


def _silu(x):
    return x * _sigmoid(x)


def _softplus(x):
    return jnp.maximum(x, 0.0) + jnp.log(1.0 + jnp.exp(-jnp.abs(x)))


def _log_sigmoid(x):
    return -_softplus(-x)


def _adaln_kernel(c_ref, w_ref, b_ref, o_ref):
    c = c_ref[...]
    o_ref[0] = _dot_hi(_silu(c), w_ref[0]) + b_ref[0]


def adaln(c, ada_w, ada_b):
    depth, d, n = ada_w.shape
    b = c.shape[0]
    cp = jnp.zeros((8, d), F32).at[:b].set(c)
    tn = 1536
    out = pl.pallas_call(
        _adaln_kernel,
        grid=(depth, n // tn),
        in_specs=[_full((8, d)),
                  pl.BlockSpec((1, d, tn), lambda l, j: (l, 0, j)),
                  pl.BlockSpec((1, 1, tn), lambda l, j: (l, 0, j))],
        out_specs=pl.BlockSpec((1, 8, tn), lambda l, j: (l, 0, j)),
        out_shape=jax.ShapeDtypeStruct((depth, 8, n), F32),
        compiler_params=_cparams(("arbitrary", "arbitrary")),
        name="adaln",
    )(cp, ada_w, ada_b.reshape(depth, 1, n))
    return out[:, :b]


def _ln_kernel(x_ref, g_ref, sc_ref, sh_ref, o_ref):
    x = x_ref[0]
    y = x * lax.rsqrt(jnp.mean(x * x, axis=-1, keepdims=True) + EPS) * g_ref[...]
    o_ref[0] = (y * (1.0 + sc_ref[0]) + sh_ref[0]).astype(o_ref.dtype)


def ln_mod(x, g, sc, sh, out_dtype, tm=512):
    b, t, d = x.shape
    return pl.pallas_call(
        _ln_kernel,
        grid=(b, t // tm),
        in_specs=[pl.BlockSpec((1, tm, d), lambda i, j: (i, j, 0)),
                  _full((1, d)),
                  pl.BlockSpec((1, 1, d), lambda i, j: (i, 0, 0)),
                  pl.BlockSpec((1, 1, d), lambda i, j: (i, 0, 0))],
        out_specs=pl.BlockSpec((1, tm, d), lambda i, j: (i, j, 0)),
        out_shape=jax.ShapeDtypeStruct((b, t, d), out_dtype),
        compiler_params=_cparams(("arbitrary", "arbitrary")),
        name="ln_mod",
    )(x, g.reshape(1, d), sc.reshape(b, 1, d), sh.reshape(b, 1, d))


def proj_multi(h, groups, tm=512, name="proj"):
    b, t, d = h.shape
    widths = [g[0].shape[1] for g in groups]
    starts = np.cumsum([0] + widths)
    w_cat = jnp.concatenate([g[0] for g in groups], axis=1).astype(BF16)
    extras = [e for g in groups for e in g[3]]
    n_ex = [len(g[3]) for g in groups]
    n_out = len(groups)

    def kern(h_ref, w_ref, *rest):
        ex_refs = rest[:len(extras)]
        o_refs = rest[len(extras):]
        y = _dot(h_ref[0], w_ref[...])
        pos = 0
        for gi, (_, out_dtype, epilogue, _) in enumerate(groups):
            yg = y[:, starts[gi]:starts[gi + 1]]
            if epilogue is not None:
                yg = epilogue(yg, *[e[...] for e in ex_refs[pos:pos + n_ex[gi]]])
            pos += n_ex[gi]
            o_refs[gi][0] = yg.astype(out_dtype)

    return pl.pallas_call(
        kern,
        grid=(b, t // tm),
        in_specs=[pl.BlockSpec((1, tm, d), lambda i, j: (i, j, 0)), _full((d, int(starts[-1])))]
                 + [_full(e.shape) for e in extras],
        out_specs=[pl.BlockSpec((1, tm, n), lambda i, j: (i, j, 0)) for n in widths],
        out_shape=[jax.ShapeDtypeStruct((b, t, n), g[1]) for n, g in zip(widths, groups)],
        compiler_params=_cparams(("arbitrary", "arbitrary")),
        name=name,
    )(h, w_cat, *extras)


def _head_norm_epilogue(dh):
    inv = 1.0 / dh

    def ep(y, bd, gain):
        ssq = _dot((y * y).astype(BF16), bd)
        return y * lax.rsqrt(ssq * inv + EPS) * gain

    return ep


def _block_diag_ones(n, dh):
    i = np.arange(n) // dh
    return jnp.asarray((i[:, None] == i[None, :]).astype(np.float32), dtype=BF16)


def _norm_mod(x, gain, sc, sh):
    y = x * lax.rsqrt(jnp.mean(x * x, axis=-1, keepdims=True) + EPS) * gain
    return y * (1.0 + sc) + sh


def _outproj_kernel(y1_ref, y2_ref, wa_ref, wb_ref, x_ref, g_ref, n_ref, sc_ref, sh_ref, o_ref, h_ref):
    y = _dot(y1_ref[0], wa_ref[...]) + _dot(y2_ref[0], wb_ref[...])
    x_new = x_ref[0] + g_ref[0] * y
    o_ref[0] = x_new
    h_ref[0] = _norm_mod(x_new, n_ref[...], sc_ref[0], sh_ref[0]).astype(h_ref.dtype)


def out_proj(y1, y2, w_out, x, gate, g_norm, sc, sh, tm=512):
    b, t, d = x.shape
    n1, n2 = y1.shape[-1], y2.shape[-1]
    wa = w_out[:n1].astype(BF16)
    wb = w_out[n1:].astype(BF16)
    row = lambda n: pl.BlockSpec((1, tm, n), lambda i, j: (i, j, 0))
    per_b = lambda: pl.BlockSpec((1, 1, d), lambda i, j: (i, 0, 0))
    return pl.pallas_call(
        _outproj_kernel,
        grid=(b, t // tm),
        in_specs=[row(n1), row(n2), _full((n1, d)), _full((n2, d)), row(d), per_b(), _full((1, d)), per_b(), per_b()],
        out_specs=[row(d), row(d)],
        out_shape=[jax.ShapeDtypeStruct((b, t, d), F32), jax.ShapeDtypeStruct((b, t, d), F32)],
        compiler_params=_cparams(("arbitrary", "arbitrary")),
        name="out_proj",
    )(y1, y2, wa, wb, x, gate.reshape(b, 1, d), g_norm.reshape(1, d), sc.reshape(b, 1, d), sh.reshape(b, 1, d))


def _decay_kernel(s_ref, fb_ref, tril_ref, place_ref, o_ref, carry):
    @pl.when(pl.program_id(1) == 0)
    def _():
        carry[...] = jnp.zeros_like(carry)

    tm = s_ref.shape[1]
    lf = _log_sigmoid(s_ref[0] + fb_ref[...])
    cum = _dot_ones(tril_ref[...], lf) + carry[...]
    carry[...] = cum[tm - 1:tm, :]
    x = cum * LOG2E
    hi = x.astype(BF16)
    r1 = x - hi.astype(F32)
    mid = r1.astype(BF16)
    low = (r1 - mid.astype(F32)).astype(BF16)
    o_ref[0] = _dot(jnp.concatenate([hi, mid, low], axis=1), place_ref[...]).astype(o_ref.dtype)


def fox_decay(small, fox_fb, tm=512):
    b, t, _ = small.shape
    fb = jnp.zeros((1, LANES), F32).at[0, :FOX_HEADS].set(fox_fb)
    tril = jnp.asarray(np.tril(np.ones((tm, tm), np.float32)), dtype=BF16)
    place = np.zeros((3 * LANES, FOX_W), np.float32)
    for h in range(FOX_HEADS):
        for j in range(3):
            place[j * LANES + h, (h // 2) * LANES + (FOX_DH if h % 2 == 0 else 0) + j] = 1.0
    return pl.pallas_call(
        _decay_kernel,
        grid=(b, t // tm),
        in_specs=[pl.BlockSpec((1, tm, LANES), lambda i, j: (i, j, 0)), _full((1, LANES)), _full((tm, tm)),
                  _full((3 * LANES, FOX_W))],
        out_specs=pl.BlockSpec((1, tm, FOX_W), lambda i, j: (i, j, 0)),
        out_shape=jax.ShapeDtypeStruct((b, t, FOX_W), BF16),
        scratch_shapes=[pltpu.VMEM((1, LANES), F32)],
        compiler_params=_cparams(("arbitrary", "arbitrary")),
        name="fox_decay",
    )(small, fb, tril, jnp.asarray(place, dtype=BF16))


def _fox_kernel(q_ref, k_ref, v_ref, f_ref, o_ref, *, tq, wide):
    i = pl.program_id(2)
    lane = lax.broadcasted_iota(jnp.int32, (1, LANES), 1)
    lo = lane < FOX_DH
    coef = jnp.where((lane & (FOX_DH - 1)) < 3, -1.0, 0.0).astype(BF16)
    q = q_ref[0]
    qs = (jnp.where(lo, q, coef), jnp.where(lo, coef, q))
    causal = (lax.broadcasted_iota(jnp.int32, (tq, tq), 1) <= lax.broadcasted_iota(jnp.int32, (tq, tq), 0))
    one = jnp.ones((1, LANES), BF16)

    def step(j, carry, tw, diag=False):
        start = pl.multiple_of(j * tw, tw)
        kt = k_ref[0, pl.ds(start, tw), :]
        ft = f_ref[0, pl.ds(start, tw), :]
        vt = v_ref[0, pl.ds(start, tw), :]
        s_pair = (_dot_nt(qs[0], jnp.where(lo, kt, ft)), _dot_nt(qs[1], jnp.where(lo, ft, kt)))
        vs = (jnp.where(lo, vt, one), jnp.where(lo, one, vt))
        new = []
        for hh in range(2):
            m, acc = carry[hh]
            s = s_pair[hh]
            if diag:
                s = jnp.where(causal, s, NEG)
            m_new = jnp.maximum(m, jnp.max(s, axis=1, keepdims=True))
            p = jnp.exp2(s - m_new)
            acc = jnp.exp2(m - m_new) * acc + _dot(p.astype(BF16), vs[hh])
            new.append((m_new, acc))
        return tuple(new)

    carry = tuple((jnp.full((tq, 1), NEG, F32), jnp.zeros((tq, LANES), F32)) for _ in range(2))
    n_wide = i // wide
    carry = lax.fori_loop(0, n_wide, lambda j, c: step(j, c, wide * tq), carry)
    done = n_wide * wide
    part = wide // 2
    while part >= 1:
        carry = lax.cond((i & part) != 0, lambda c, d=done, w=part: step(d // w, c, w * tq), lambda c: c, carry)
        done = done + (i & part)
        part //= 2
    carry = step(i, carry, tq, diag=True)
    acc = jnp.where(lo, carry[0][1], carry[1][1])
    den = jnp.where(lo, carry[1][1], carry[0][1])
    o_ref[0] = (acc / pltpu.roll(den, FOX_DH, 1)).astype(o_ref.dtype)


def fox_attention(q, k, v, feat, tq=512, wide=4):
    b, t, w = q.shape
    npair = w // LANES
    nt = t // tq
    whole = lambda: pl.BlockSpec((1, t, LANES), lambda bi, p, i: (bi, 0, p))
    return pl.pallas_call(
        functools.partial(_fox_kernel, tq=tq, wide=wide),
        grid=(b, npair, nt),
        in_specs=[pl.BlockSpec((1, tq, LANES), lambda bi, p, i: (bi, i, p)), whole(), whole(), whole()],
        out_specs=pl.BlockSpec((1, tq, LANES), lambda bi, p, i: (bi, i, p)),
        out_shape=jax.ShapeDtypeStruct((b, t, w), BF16),
        compiler_params=_cparams(("arbitrary", "arbitrary", "arbitrary")),
        name="fox_attn",
    )(q, k, v, feat)


def _mm(a, b):
    return _dot(a.astype(BF16), b.astype(BF16))


def _tril_solve(a, rhs, ri, ci):
    n = a[0].shape[0]
    both = lambda f, x, y: [f(p, q) for p, q in zip(x, y)]
    eye = (ri == ci).astype(F32)
    same = lambda b: (lax.shift_right_logical(ri, int(math.log2(b)))
                      == lax.shift_right_logical(ci, int(math.log2(b))))
    base = 16
    d = [jnp.where(same(base), p, 0.0) for p in a]
    d2 = both(_mm, d, d)
    d4 = both(_mm, d2, d2)
    r1 = [eye - p + p2 - t for p, p2, t in zip(d, d2, both(_mm, d, d2))]
    d8 = both(_mm, d4, d4)
    r2 = [eye + p4 + p8 + t for p4, p8, t in zip(d4, d8, both(_mm, d4, d8))]
    t = both(_mm, r1, r2)
    b = base
    while b < n:
        join = same(2 * b) & jnp.logical_not(same(b))
        low = [jnp.where(join, p, 0.0) for p in a]
        t = [p - q for p, q in zip(t, both(_mm, both(_mm, t, low), t))]
        b *= 2
    return both(_mm, t, rhs)


GDN_BLOCK = 128


def _gdn_kernel(x_ref, sm_ref, z_ref, cw_ref, ega_ref, egb_ref, alog_ref, dtb_ref, on_ref, tril_ref,
                o_ref, s_scr, prev_scr, *, tc):
    c = GDN_BLOCK
    w = GDN_W

    @pl.when(pl.program_id(1) == 0)
    def _():
        s_scr[...] = jnp.zeros_like(s_scr)
        prev_scr[...] = jnp.zeros_like(prev_scr)

    x = x_ref[0]
    prev = prev_scr[...]
    row8 = lax.broadcasted_iota(jnp.int32, (8, 1), 0)
    acc = x * cw_ref[GDN_CONV - 1:GDN_CONV, :]
    for s in range(1, GDN_CONV):
        rolled = pltpu.roll(x, s, 0)
        head = jnp.where(row8 < s, pltpu.roll(prev, s, 0), rolled[0:8])
        shifted = jnp.concatenate([head, rolled[8:]], axis=0)
        acc = acc + shifted * cw_ref[GDN_CONV - 1 - s:GDN_CONV - s, :]
    prev_scr[...] = x[tc - 8:tc]
    xc = _silu(acc)

    sm = sm_ref[0]
    g_raw = _dot_hi(sm, ega_ref[...])
    b_raw = _dot_hi(sm, egb_ref[...])
    g = -jnp.exp(alog_ref[...]) * _softplus(g_raw + dtb_ref[...])
    beta_all = _sigmoid(b_raw)
    gc_all = _dot_ones(tril_ref[...], g)

    ri = lax.broadcasted_iota(jnp.int32, (c, c), 0)
    ci = lax.broadcasted_iota(jnp.int32, (c, c), 1)
    causal = ci <= ri
    strict = ci < ri

    nblk = tc // c
    a_l, attn_l, rhs_l, qd_l, kd_l, egl_l = [], [], [], [], [], []
    for h in range(GDN_HEADS):
        ln = slice(h * GDN_DH, (h + 1) * GDN_DH)
        qh = xc[:, h * GDN_DH:(h + 1) * GDN_DH]
        kh = xc[:, w + h * GDN_DH:w + (h + 1) * GDN_DH]
        qh = qh * lax.rsqrt(jnp.sum(qh * qh, axis=-1, keepdims=True) + EPS) * (GDN_DH ** -0.5)
        kh = kh * lax.rsqrt(jnp.sum(kh * kh, axis=-1, keepdims=True) + EPS)
        vh = xc[:, 2 * w + h * GDN_DH:2 * w + (h + 1) * GDN_DH]
        gch = gc_all[:, ln]
        gct = gch.T
        egc = jnp.exp(gch)
        bh = beta_all[:, ln]
        for n in range(nblk):
            sl = slice(n * c, (n + 1) * c)
            q, k, v, gc, be = qh[sl], kh[sl], vh[sl], gch[sl], bh[sl]
            decay = jnp.exp(jnp.where(causal, gc - gct[:, sl], NEG))
            kb = k * be
            kk = _dot_nt(jnp.concatenate([kb, q], axis=0).astype(BF16), k.astype(BF16))
            a_l.append(jnp.where(strict, kk[:c] * decay, 0.0))
            attn_l.append(jnp.where(causal, kk[c:] * decay, 0.0))
            rhs_l.append(jnp.concatenate([v * be, kb * egc[sl]], axis=1))
            gl = gc[c - 1:c, :]
            qd_l.append(q * egc[sl])
            kd_l.append(k * jnp.exp(gl - gc))
            egl_l.append(jnp.exp(gl))
    uw_l = _tril_solve(a_l, rhs_l, ri, ci)

    states = [s_scr[h] for h in range(GDN_HEADS)]
    for n in range(nblk):
        sl = slice(n * c, (n + 1) * c)
        idx = [h * nblk + n for h in range(GDN_HEADS)]
        ws = [_mm(jnp.concatenate([uw_l[i][:, GDN_DH:], qd_l[i]], axis=0), states[h])
              for h, i in enumerate(idx)]
        v_new = [uw_l[i][:, :GDN_DH] - ws[h][:c] for h, i in enumerate(idx)]
        o = [ws[h][c:] + _mm(attn_l[i], v_new[h]) for h, i in enumerate(idx)]
        states = [states[h] * egl_l[i] + _dot_tn(kd_l[i].astype(BF16), v_new[h].astype(BF16))
                  for h, i in enumerate(idx)]
        for h in range(GDN_HEADS):
            ln = slice(h * GDN_DH, (h + 1) * GDN_DH)
            on = o[h] * lax.rsqrt(jnp.mean(o[h] * o[h], axis=-1, keepdims=True) + EPS) * on_ref[...]
            o_ref[0, sl, ln] = (on * _silu(z_ref[0, sl, ln])).astype(o_ref.dtype)
    for h in range(GDN_HEADS):
        s_scr[h] = states[h]


def gated_delta_net(x, small, z, conv_w, a_log, dt_bias, on_gain, tc=512):
    b, t, _ = x.shape
    w = GDN_W
    ega = np.zeros((LANES, w), np.float32)
    egb = np.zeros((LANES, w), np.float32)
    for h in range(GDN_HEADS):
        ega[FOX_HEADS + h, h * GDN_DH:(h + 1) * GDN_DH] = 1.0
        egb[FOX_HEADS + GDN_HEADS + h, h * GDN_DH:(h + 1) * GDN_DH] = 1.0
    alog = jnp.repeat(a_log, GDN_DH).reshape(1, w)
    dtb = jnp.repeat(dt_bias, GDN_DH).reshape(1, w)
    idx = np.arange(tc)
    tril = ((idx[:, None] >= idx[None, :]) & (idx[:, None] // GDN_BLOCK == idx[None, :] // GDN_BLOCK))
    row = lambda n: pl.BlockSpec((1, tc, n), lambda i, j: (i, j, 0))
    return pl.pallas_call(
        functools.partial(_gdn_kernel, tc=tc),
        grid=(b, t // tc),
        in_specs=[row(3 * w), row(LANES), row(w), _full((GDN_CONV, 3 * w)), _full((LANES, w)), _full((LANES, w)),
                  _full((1, w)), _full((1, w)), _full((1, GDN_DH)), _full((tc, tc))],
        out_specs=row(w),
        out_shape=jax.ShapeDtypeStruct((b, t, w), BF16),
        scratch_shapes=[pltpu.VMEM((GDN_HEADS, GDN_DH, GDN_DH), F32), pltpu.VMEM((8, 3 * w), F32)],
        compiler_params=_cparams(("arbitrary", "arbitrary")),
        name="gdn",
    )(x, small, z, conv_w, jnp.asarray(ega), jnp.asarray(egb), alog, dtb, on_gain.reshape(1, GDN_DH),
      jnp.asarray(tril.astype(np.float32), dtype=BF16))


def even_mixer(h, w_in, fox_fb, fox_qn, fox_kn, gdn_conv, gdn_a_log, gdn_dt_bias, gdn_on):
    cuts = np.cumsum((0,) + EV_SIZES)
    col = lambda i: w_in[:, cuts[i]:cuts[i + 1]]
    bd = _block_diag_ones(FOX_W, FOX_DH)
    ep = _head_norm_epilogue(FOX_DH)
    qg = (jnp.tile(fox_qn, FOX_HEADS) * (FOX_DH ** -0.5 * LOG2E)).reshape(1, FOX_W)
    kg = jnp.tile(fox_kn, FOX_HEADS).reshape(1, FOX_W)
    w_small = jnp.zeros((w_in.shape[0], LANES), F32)
    w_small = w_small.at[:, 0:8].set(col(3)).at[:, 8:12].set(col(5)).at[:, 12:16].set(col(6))
    fq, fk, fv, small, gqkv, gz = proj_multi(
        h, [(col(0), BF16, ep, (bd, qg)), (col(1), BF16, ep, (bd, kg)), (col(2), BF16, None, ()),
            (w_small, F32, None, ()), (col(4), F32, None, ()), (col(7), F32, None, ())], name="proj_even")
    feat = fox_decay(small, fox_fb)
    o_fox = fox_attention(fq, fk, fv, feat)
    o_gdn = gated_delta_net(gqkv, small, gz, gdn_conv, gdn_a_log, gdn_dt_bias, gdn_on)
    return o_fox, o_gdn


def _t5_bucket_np(dist):
    n = np.maximum(dist, 0)
    exact = REL_BUCKETS // 2
    nf = np.maximum(n, 1).astype(np.float32)
    large = exact + (np.log(nf / np.float32(exact)) / np.float32(math.log(REL_MAX_DIST / exact))
                     * np.float32(REL_BUCKETS - exact)).astype(np.int32)
    large = np.minimum(large, REL_BUCKETS - 1)
    return np.where(n < exact, n, large)


def _bias_kernel(tbl_ref, bucket_ref, o_ref):
    h = pl.program_id(0)
    bucket = bucket_ref[...]
    acc = jnp.full(bucket.shape, NEG, F32)
    for b in range(REL_BUCKETS):
        acc = jnp.where(bucket == b, tbl_ref[b, h], acc)
    o_ref[0] = acc


def _bias_table(rel_bias, dist, valid):
    shifted = (rel_bias - rel_bias[REL_BUCKETS - 1:REL_BUCKETS]) * LOG2E
    bucket = np.where(valid, _t5_bucket_np(dist), -1).astype(np.int32)
    rows, cols = int(np.prod(bucket.shape[:-1])), bucket.shape[-1]
    nh = rel_bias.shape[1]
    tb = pl.pallas_call(
        _bias_kernel,
        grid=(nh,),
        in_specs=[pl.BlockSpec(memory_space=pltpu.SMEM), _full((rows, cols))],
        out_specs=pl.BlockSpec((1, rows, cols), lambda h: (h, 0, 0)),
        out_shape=jax.ShapeDtypeStruct((nh, rows, cols), F32),
        compiler_params=_cparams(("arbitrary",)),
        name="t5_bias",
    )(shifted, jnp.asarray(bucket.reshape(rows, cols)))
    return tb.reshape((nh,) + bucket.shape)


def _cmp_kernel(r_ref, pos_ref, w1_ref, w2_ref, kn_ref, o_ref):
    m = r_ref.shape[3]
    half = r_ref.shape[4]
    r = r_ref[0, 0, 0].astype(BF16)
    a = _dot(r, w1_ref[0, :half, :])
    bm = _dot(r, w1_ref[0, half:, :])
    c = _dot(pos_ref[0].astype(BF16), w1_ref[0])
    hid = a + pltpu.roll(bm, m - 1, 0) + c[0:1, :]
    out = _dot(_silu(hid).astype(BF16), w2_ref[0])
    normed = out * lax.rsqrt(jnp.mean(out * out, axis=-1, keepdims=True) + EPS) * kn_ref[...]
    o_ref[0, 0, 0] = jnp.where(pl.program_id(0) == 0, normed, out).astype(o_ref.dtype)


def nsa_compress(kcvc, pos, w1, w2, kn):
    b, t, _ = kcvc.shape
    m = t // CMP_STRIDE
    half = CMP_STRIDE * NSA_DH
    r = kcvc.reshape(b, m, CMP_STRIDE, 2, NSA_KV_HEADS, NSA_DH).transpose(3, 0, 4, 1, 2, 5).reshape(2, b, 2, m, half)
    posf = jnp.zeros((2, 8, 2 * half), F32).at[:, 0].set(pos.reshape(2, 2 * half))
    w2d = jnp.concatenate([w2, w2], axis=-1).astype(BF16)
    knd = jnp.tile(kn, 2).reshape(1, LANES)
    return pl.pallas_call(
        _cmp_kernel,
        grid=(2, b, NSA_KV_HEADS),
        in_specs=[pl.BlockSpec((1, 1, 1, m, half), lambda s, i, k: (s, i, k, 0, 0)),
                  pl.BlockSpec((1, 8, 2 * half), lambda s, i, k: (s, 0, 0)),
                  pl.BlockSpec((1, 2 * half, CMP_HIDDEN), lambda s, i, k: (s, 0, 0)),
                  pl.BlockSpec((1, CMP_HIDDEN, LANES), lambda s, i, k: (s, 0, 0)),
                  _full((1, LANES))],
        out_specs=pl.BlockSpec((1, 1, 1, m, LANES), lambda s, i, k: (s, i, k, 0, 0)),
        out_shape=jax.ShapeDtypeStruct((2, b, NSA_KV_HEADS, m, LANES), BF16),
        compiler_params=_cparams(("arbitrary", "arbitrary", "arbitrary")),
        name="nsa_compress",
    )(r, posf, w1.astype(BF16), w2d, knd)


def _dot_split(a, b):
    hi = a.astype(BF16)
    lo = (a - hi.astype(F32)).astype(BF16)
    return _dot(hi, b) + _dot(lo, b)


def _head_q(q_ref, hh, lo):
    blk = q_ref[0, :, (hh // 2) * LANES:(hh // 2 + 1) * LANES]
    keep = lo if hh % 2 == 0 else jnp.logical_not(lo)
    return jnp.where(keep, blk, jnp.zeros_like(blk))


def _pair_heads(o, lo):
    return jnp.concatenate([jnp.where(lo, o[0], o[1]), jnp.where(lo, o[2], o[3])], axis=1)


def _nsa_sel_kernel(q_ref, kc_ref, vc_ref, ov_ref, bt_ref, o_ref, sel_ref, *, tq, nband, n_slc):
    i = pl.program_id(2)
    ncp = kc_ref.shape[3]
    nsp = ov_ref.shape[1]
    per = tq // CMP_STRIDE
    var = jnp.minimum(i, 1)
    bs = pl.multiple_of(per * jnp.maximum(i - 1, 0), per)
    lo = lax.broadcasted_iota(jnp.int32, (1, LANES), 1) < NSA_DH
    kc = kc_ref[0, 0, 0]
    vc = vc_ref[0, 0, 0]
    kcb = kc_ref[0, 0, 0, pl.ds(bs, nband), :]
    vcb = vc_ref[0, 0, 0, pl.ds(bs, nband), :]
    far_ok = lax.broadcasted_iota(jnp.int32, (1, ncp), 1) < per * (i - 1)
    hs = range(NSA_GROUP)
    qh = [_head_q(q_ref, hh, lo) for hh in hs]
    s_far = [jnp.where(far_ok, _dot_nt(q, kc), NEG) for q in qh]
    s_band = [_dot_nt(qh[hh], kcb) + bt_ref[var, hh] for hh in hs]
    m = [jnp.maximum(jnp.max(a, axis=1, keepdims=True), jnp.max(b, axis=1, keepdims=True))
         for a, b in zip(s_far, s_band)]
    m = [jnp.where(x < 0.5 * NEG, 0.0, x) for x in m]
    p_far = [jnp.exp2(a - x) for a, x in zip(s_far, m)]
    p_band = [jnp.exp2(b - x) for b, x in zip(s_band, m)]
    l = [jnp.sum(a, axis=1, keepdims=True) + jnp.sum(b, axis=1, keepdims=True) for a, b in zip(p_far, p_band)]
    inv = [1.0 / jnp.where(x == 0.0, 1.0, x) for x in l]
    outs = [(_dot(a.astype(BF16), vc) + _dot(b.astype(BF16), vcb)) * x for a, b, x in zip(p_far, p_band, inv)]
    ps_far = p_far[0] * inv[0]
    ps_band = p_band[0] * inv[0]
    for hh in range(1, NSA_GROUP):
        ps_far = ps_far + p_far[hh] * inv[hh]
        ps_band = ps_band + p_band[hh] * inv[hh]
    o_ref[0] = _pair_heads(outs, lo).astype(o_ref.dtype)

    imp = _dot_split(ps_far, ov_ref[...]) + _dot_split(ps_band, ov_ref[pl.ds(bs, nband), :])
    blk = lax.broadcasted_iota(jnp.int32, (1, nsp), 1)
    blk_f = blk.astype(F32)
    qpos = i * tq + lax.broadcasted_iota(jnp.int32, (tq, 1), 0)
    cur = lax.shift_right_logical(qpos, int(math.log2(SLC_LEN)))
    forced = (blk == 0) | (blk == cur) | (blk == cur - 1)
    work = jnp.where(forced, -jnp.inf, jnp.where(blk <= cur, imp, NEG))
    work = jnp.where(blk < n_slc, work, -jnp.inf)
    ngrp = 4
    rg = tq // ngrp
    works = [work[r * rg:(r + 1) * rg] for r in range(ngrp)]
    sels = [jnp.where(forced[r * rg:(r + 1) * rg], 1.0, jnp.zeros((rg, nsp), F32)) for r in range(ngrp)]
    for _ in range(max(min(SLC_TOPK, n_slc) - 3, 0)):
        ms = [jnp.max(w, axis=1, keepdims=True) for w in works]
        firsts = [jnp.min(jnp.where(w == m, blk_f, float(nsp)), axis=1, keepdims=True) for w, m in zip(works, ms)]
        picks = [blk_f == f for f in firsts]
        sels = [jnp.where(p, 1.0, s) for p, s in zip(picks, sels)]
        works = [jnp.where(p, -jnp.inf, w) for p, w in zip(picks, works)]
    sel = jnp.concatenate(sels, axis=0)
    for sup in range(sel_ref.shape[2]):
        col = sel[:, (sup // 2) * LANES:(sup // 2 + 1) * LANES]
        if sup % 2 == 0:
            col = pltpu.roll(col, SUP_BLOCKS, 1)
        sel_ref[0, 0, sup] = jnp.where(lo, 0.0, jnp.where(col > 0.5, 0.0, NEG)).astype(sel_ref.dtype)


def nsa_select(q, cmp_kv, rel_bias, tq=512):
    b, t, _ = q.shape
    ncp = t // CMP_STRIDE
    n_cmp = ncp - 1
    n_slc = t // SLC_LEN
    nsp = max(LANES, n_slc)
    nsup = max(1, n_slc // SUP_BLOCKS)
    per = tq // CMP_STRIDE
    nband = 2 * per
    n = np.arange(ncp)[:, None]
    s = np.arange(nsp)[None, :]
    ov = ((CMP_STRIDE * n < SLC_LEN * s + SLC_LEN) & (CMP_STRIDE * n + CMP_LEN > SLC_LEN * s)
          & (n < n_cmp) & (s < n_slc)).astype(np.float32)
    qi = np.arange(tq)[:, None]
    nj = np.arange(nband)[None, :]
    end = CMP_STRIDE * nj + CMP_LEN - 1
    dist = np.stack([qi - end, tq + qi - end])
    bt = _bias_table(rel_bias, dist, dist >= 0)
    bt = bt.reshape(NSA_KV_HEADS, NSA_GROUP, 2, tq, nband).transpose(0, 2, 1, 3, 4)
    bt = bt.reshape(NSA_KV_HEADS * 2, NSA_GROUP, tq, nband)
    gw = NSA_GROUP * NSA_DH
    return pl.pallas_call(
        functools.partial(_nsa_sel_kernel, tq=tq, nband=nband, n_slc=n_slc),
        grid=(b, NSA_KV_HEADS, t // tq),
        in_specs=[pl.BlockSpec((1, tq, gw), lambda bi, k, i: (bi, i, k)),
                  pl.BlockSpec((1, 1, 1, ncp, LANES), lambda bi, k, i: (0, bi, k, 0, 0)),
                  pl.BlockSpec((1, 1, 1, ncp, LANES), lambda bi, k, i: (1, bi, k, 0, 0)),
                  _full((ncp, nsp)),
                  pl.BlockSpec((2, NSA_GROUP, tq, nband), lambda bi, k, i: (k, 0, 0, 0))],
        out_specs=[pl.BlockSpec((1, tq, gw), lambda bi, k, i: (bi, i, k)),
                   pl.BlockSpec((1, 1, nsup, tq, LANES), lambda bi, k, i: (bi, k, 0, i, 0))],
        out_shape=[jax.ShapeDtypeStruct((b, t, NSA_W), BF16),
                   jax.ShapeDtypeStruct((b, NSA_KV_HEADS, nsup, t, LANES), BF16)],
        compiler_params=_cparams(("arbitrary", "arbitrary", "arbitrary")),
        name="nsa_select",
    )(q, cmp_kv, cmp_kv, jnp.asarray(ov, dtype=BF16), bt)


def _nsa_main_kernel(q_ref, ks_ref, vs_ref, kw0_ref, kw1_ref, kw2_ref, vw0_ref, vw1_ref, vw2_ref, sel_ref, oh_ref,
                     ocmp_ref, gate_ref, tb_ref, wm_ref, eg_ref, o_ref, *, tq, wide):
    i = pl.program_id(2)
    g = NSA_GROUP
    lo = lax.broadcasted_iota(jnp.int32, (1, LANES), 1) < NSA_DH

    def head_low(hh):
        blk = q_ref[0, :, (hh // 2) * LANES:(hh // 2 + 1) * LANES]
        if hh % 2 == 1:
            blk = pltpu.roll(blk.astype(F32), NSA_DH, 1).astype(BF16)
        return jnp.where(lo, blk, jnp.zeros_like(blk))

    qst = jnp.concatenate([head_low(hh) for hh in range(g)], axis=0)
    causal = (lax.broadcasted_iota(jnp.int32, (tq, tq), 1) <= lax.broadcasted_iota(jnp.int32, (tq, tq), 0))
    one = jnp.ones((1, LANES), BF16)
    sup_keys = SUP_BLOCKS * SLC_LEN
    all_masked = jnp.where(lo, 0.0, NEG).astype(BF16)

    def sel_step(jt, carry, near, tw):
        m, acc = carry
        key0 = jnp.maximum(jt, 0) * tw
        start = pl.multiple_of(key0, tw)
        within = pl.multiple_of(key0 & (sup_keys - 1), tw)
        mq = sel_ref[0, 0, lax.shift_right_logical(key0, int(math.log2(sup_keys)))]
        mq = jnp.where(jt >= 0, mq, all_masked)
        q_ext = qst + jnp.concatenate([mq] * g, axis=0)
        kt = jnp.where(lo, ks_ref[0, pl.ds(start, tw), :], oh_ref[pl.ds(within, tw), :])
        vt = jnp.where(lo, vs_ref[0, pl.ds(start, tw), :], one)
        s = _dot_nt(q_ext, kt)
        if near is not None:
            s = s.reshape(g, tq, tw) + tb_ref[:, :, near * tq:(near + 1) * tq]
            if near == 2:
                s = jnp.where(causal[None], s, NEG)
            s = s.reshape(g * tq, tw)
        m_new = jnp.maximum(m, jnp.max(s, axis=1, keepdims=True))
        p = jnp.exp2(s - m_new)
        acc = jnp.exp2(m - m_new) * acc + _dot(p.astype(BF16), vt)
        return m_new, acc

    n_far = jnp.maximum(i - 2, 0)
    n_wide = n_far // wide
    carry = (jnp.full((g * tq, 1), NEG, F32), jnp.zeros((g * tq, LANES), F32))
    carry = lax.fori_loop(0, n_wide, lambda j, c: sel_step(j, c, None, wide * tq), carry)
    done = n_wide * wide
    part = wide // 2
    while part >= 1:
        carry = lax.cond((n_far & part) != 0, lambda c, d=done, w=part: sel_step(d // w, c, None, w * tq),
                         lambda c: c, carry)
        done = done + (n_far & part)
        part //= 2
    for near in range(3):
        carry = sel_step(i - 2 + near, carry, near, tq)
    o_slc = (carry[1] / pltpu.roll(carry[1], NSA_DH, 1)).reshape(g, tq, LANES)

    var = jnp.minimum(i, 2)
    kws = (kw0_ref, kw1_ref, kw2_ref)
    vws = (vw0_ref, vw1_ref, vw2_ref)
    sw = []
    for near in range(3):
        s = _dot_nt(qst, kws[near][0]).reshape(g, tq, tq)
        s = s + tb_ref[:, :, near * tq:(near + 1) * tq] + wm_ref[var, :, near * tq:(near + 1) * tq][None]
        sw.append(s.reshape(g * tq, tq))
    m = jnp.maximum(jnp.maximum(jnp.max(sw[0], axis=1, keepdims=True), jnp.max(sw[1], axis=1, keepdims=True)),
                    jnp.max(sw[2], axis=1, keepdims=True))
    acc = jnp.zeros((g * tq, LANES), F32)
    for near in range(3):
        p = jnp.exp2(sw[near] - m)
        acc = acc + _dot(p.astype(BF16), jnp.where(lo, vws[near][0], one))
    o_win = (acc / pltpu.roll(acc, NSA_DH, 1)).reshape(g, tq, LANES)

    pair = lambda o: jnp.concatenate([jnp.where(lo, o[0], pltpu.roll(o[1], NSA_DH, 1)),
                                      jnp.where(lo, o[2], pltpu.roll(o[3], NSA_DH, 1))], axis=1)
    gates = _dot_hi(_sigmoid(gate_ref[0]), eg_ref[0])
    gw = g * NSA_DH
    out = (gates[:, 0:gw] * ocmp_ref[0].astype(F32)
           + gates[:, gw:2 * gw] * pair(o_slc) + gates[:, 2 * gw:3 * gw] * pair(o_win))
    o_ref[0] = out.astype(o_ref.dtype)


def nsa_main(q, ksw, vsw, sel, o_cmp, small, rel_bias, tq=256, wide=8):
    b, t, _ = q.shape
    nsup = sel.shape[2]
    g = NSA_GROUP
    gw = g * NSA_DH
    sup_keys = SUP_BLOCKS * SLC_LEN
    oh = np.zeros((sup_keys, LANES), np.float32)
    oh[np.arange(sup_keys), NSA_DH + np.arange(sup_keys) // SLC_LEN] = 1.0
    qi = np.arange(tq)[:, None]
    c = np.arange(3 * tq)[None, :]
    dist = qi + 2 * tq - c
    tb = _bias_table(rel_bias, dist, np.ones_like(dist, bool))
    wm = np.zeros((3, tq, 3 * tq), np.float32)
    for var in range(3):
        exists = c >= tq * (2 - var)
        wm[var] = np.where((dist >= 0) & (dist < WINDOW) & exists, 0.0, NEG)
    eg = np.zeros((NSA_KV_HEADS, LANES, 3 * gw), np.float32)
    for k in range(NSA_KV_HEADS):
        for hh in range(g):
            for br in range(3):
                eg[k, (k * g + hh) * 3 + br, br * gw + hh * NSA_DH:br * gw + (hh + 1) * NSA_DH] = 1.0
    near = lambda off, col: pl.BlockSpec(
        (1, tq, LANES), lambda bi, k, i: (bi, jnp.maximum(i - off, 0), col + k))
    return pl.pallas_call(
        functools.partial(_nsa_main_kernel, tq=tq, wide=wide),
        grid=(b, NSA_KV_HEADS, t // tq),
        in_specs=[pl.BlockSpec((1, tq, gw), lambda bi, k, i: (bi, i, k)),
                  pl.BlockSpec((1, t, LANES), lambda bi, k, i: (bi, 0, k)),
                  pl.BlockSpec((1, t, LANES), lambda bi, k, i: (bi, 0, k)),
                  near(2, 2), near(1, 2), near(0, 2), near(2, 2), near(1, 2), near(0, 2),
                  pl.BlockSpec((1, 1, nsup, tq, LANES), lambda bi, k, i: (bi, k, 0, i, 0)),
                  _full((sup_keys, LANES)),
                  pl.BlockSpec((1, tq, gw), lambda bi, k, i: (bi, i, k)),
                  pl.BlockSpec((1, tq, LANES), lambda bi, k, i: (bi, i, 0)),
                  pl.BlockSpec((g, tq, 3 * tq), lambda bi, k, i: (k, 0, 0)),
                  _full((3, tq, 3 * tq)),
                  pl.BlockSpec((1, LANES, 3 * gw), lambda bi, k, i: (k, 0, 0))],
        out_specs=pl.BlockSpec((1, tq, gw), lambda bi, k, i: (bi, i, k)),
        out_shape=jax.ShapeDtypeStruct((b, t, NSA_W), BF16),
        compiler_params=_cparams(("arbitrary", "arbitrary", "arbitrary")),
        name="nsa_main",
    )(q, ksw, vsw, ksw, ksw, ksw, vsw, vsw, vsw, sel, jnp.asarray(oh, dtype=BF16), o_cmp, small, tb,
      jnp.asarray(wm), jnp.asarray(eg))


def _gla_kernel(qk_ref, v_ref, r_ref, sm_ref, wg_ref, bg_ref, on_ref, tril_ref, o_ref, s_scr, *, tc):
    c = GLA_CHUNK

    @pl.when(pl.program_id(1) == 0)
    def _():
        s_scr[...] = jnp.zeros_like(s_scr)

    kw = GLA_KW
    log_a = _log_sigmoid(_dot_hi(sm_ref[0], wg_ref[...]) + bg_ref[...]) * (1.0 / GLA_TAU)
    gcum = _dot_ones(tril_ref[...], log_a)
    q = qk_ref[0, :, 0:kw] * (GLA_DK ** -0.5)
    k = qk_ref[0, :, kw:2 * kw]
    q_dec = (q * jnp.exp(gcum)).astype(BF16)
    k_inv = (k * jnp.exp(-gcum)).astype(BF16)
    ri = lax.broadcasted_iota(jnp.int32, (c, c), 0)
    ci = lax.broadcasted_iota(jnp.int32, (c, c), 1)
    causal = ci <= ri
    lo = lax.broadcasted_iota(jnp.int32, (1, LANES), 1) < GLA_DK
    nchunk = tc // c
    heads = range(GLA_HEADS)
    zero = jnp.zeros((c, LANES), BF16)

    qm, vb, kd, egl = {}, {}, {}, {}
    for n in range(nchunk):
        sl = slice(n * c, (n + 1) * c)
        gl = gcum[n * c + c - 1:n * c + c, :]
        kdn = (k[sl] * jnp.exp(gl - gcum[sl])).astype(BF16)
        for h in heads:
            pr = slice((h // 2) * LANES, (h // 2 + 1) * LANES)
            keep = lo if h % 2 == 0 else jnp.logical_not(lo)
            qm[n, h] = jnp.where(keep, q_dec[sl, pr], zero)
            vb[n, h] = v_ref[0, sl, h * GLA_DV:(h + 1) * GLA_DV].astype(BF16)
            kd[n, h] = kdn[:, pr]
            egl[n, h] = jnp.exp(gl[:, pr])
    idx = [(n, h) for n in range(nchunk) for h in heads]
    attn = {i: jnp.where(causal, _dot_nt(qm[i], k_inv[i[0] * c:(i[0] + 1) * c, (i[1] // 2) * LANES:
                                                       (i[1] // 2 + 1) * LANES]), 0.0).astype(BF16) for i in idx}
    o_intra = {i: _dot(attn[i], vb[i]) for i in idx}
    kv = {i: _dot_tn(vb[i], kd[i]) for i in idx}

    st = [s_scr[h] for h in heads]
    for n in range(nchunk):
        sl = slice(n * c, (n + 1) * c)
        o = [o_intra[n, h] + _dot_nt(qm[n, h], st[h].astype(BF16)) for h in heads]
        st = [st[h] * egl[n, h] + kv[n, h] for h in heads]
        for h in heads:
            on = o[h] * lax.rsqrt(jnp.mean(o[h] * o[h], axis=-1, keepdims=True) + EPS) * on_ref[...]
            o_ref[0, sl, h * GLA_DV:(h + 1) * GLA_DV] = (
                on * _silu(r_ref[0, sl, h * GLA_DV:(h + 1) * GLA_DV])).astype(o_ref.dtype)
    for h in heads:
        s_scr[h] = st[h]


def gated_linear_attention(qkvr, small, wg_up, bg, on_gain, tc=512):
    b, t, _ = qkvr.shape
    wg = jnp.zeros((LANES, GLA_KW), F32).at[3 * NSA_HEADS:3 * NSA_HEADS + GLA_GATE_RANK].set(wg_up)
    idx = np.arange(tc)
    tril = ((idx[:, None] >= idx[None, :]) & (idx[:, None] // GLA_CHUNK == idx[None, :] // GLA_CHUNK))
    return pl.pallas_call(
        functools.partial(_gla_kernel, tc=tc),
        grid=(b, t // tc),
        in_specs=[pl.BlockSpec((1, tc, 2 * GLA_KW), lambda i, j: (i, j, 0)),
                  pl.BlockSpec((1, tc, GLA_W), lambda i, j: (i, j, 1)),
                  pl.BlockSpec((1, tc, GLA_W), lambda i, j: (i, j, 2)),
                  pl.BlockSpec((1, tc, LANES), lambda i, j: (i, j, 0)),
                  _full((LANES, GLA_KW)), _full((1, GLA_KW)), _full((1, GLA_DV)), _full((tc, tc))],
        out_specs=pl.BlockSpec((1, tc, GLA_W), lambda i, j: (i, j, 0)),
        out_shape=jax.ShapeDtypeStruct((b, t, GLA_W), BF16),
        scratch_shapes=[pltpu.VMEM((GLA_HEADS, GLA_DV, LANES), F32)],
        compiler_params=_cparams(("arbitrary", "arbitrary")),
        name="gla",
    )(qkvr, qkvr, qkvr, small, wg, bg.reshape(1, GLA_KW), on_gain.reshape(1, GLA_DV),
      jnp.asarray(tril.astype(np.float32), dtype=BF16))


def odd_mixer(h, w_in, nsa_qn, nsa_kn, nsa_pos, nsa_cmp_w1, nsa_cmp_w2, gla_wg_up, gla_bg, gla_on, rel_bias):
    cuts = np.cumsum((0,) + OD_SIZES)
    col = lambda i: w_in[:, cuts[i]:cuts[i + 1]]
    dup = lambda a: jnp.concatenate([a[:, :NSA_DH], a[:, :NSA_DH], a[:, NSA_DH:], a[:, NSA_DH:]], axis=1)
    ep = _head_norm_epilogue(NSA_DH)
    bd = _block_diag_ones(NSA_W, NSA_DH)
    qg = (jnp.tile(nsa_qn, NSA_HEADS) * (NSA_DH ** -0.5 * LOG2E)).reshape(1, NSA_W)
    kg = jnp.tile(nsa_kn, NSA_HEADS).reshape(1, NSA_W)
    w_small = jnp.zeros((w_in.shape[0], LANES), F32)
    w_small = w_small.at[:, 0:24].set(col(7)).at[:, 24:40].set(col(11))
    nq, kcvc, ksw, vsw, small, qkvr = proj_multi(
        h, [(col(0), BF16, ep, (bd, qg)),
            (jnp.concatenate([col(1), col(2)], axis=1), F32, None, ()),
            (jnp.concatenate([dup(col(3)), dup(col(5))], axis=1), BF16, ep, (bd, kg)),
            (jnp.concatenate([dup(col(4)), dup(col(6))], axis=1), BF16, None, ()),
            (w_small, F32, None, ()),
            (jnp.concatenate([col(8), col(9), col(10), col(12)], axis=1), F32, None, ())], name="proj_odd")
    cmp_kv = nsa_compress(kcvc, nsa_pos, nsa_cmp_w1, nsa_cmp_w2, nsa_kn)
    o_cmp, sel = nsa_select(nq, cmp_kv, rel_bias)
    o_nsa = nsa_main(nq, ksw, vsw, sel, o_cmp, small, rel_bias)
    o_gla = gated_linear_attention(qkvr, small, gla_wg_up, gla_bg, gla_on)
    return o_nsa, o_gla


MOE_TM = 256
ROUTE_TM = 512
MOE_ROWS = 512


def _first_index(mask_val, idx, big, axis):
    return jnp.min(jnp.where(mask_val, idx, big), axis=axis, keepdims=True)


def _route_kernel(h_ref, rt_ref, b_ref, up_ref, eid_ref, rank_ref, w_ref, cnt_ref, run):
    tm = h_ref.shape[0]
    ne = N_EXPERTS
    gsz = ne // N_GROUPS

    @pl.when(pl.program_id(0) == 0)
    def _():
        run[...] = jnp.zeros_like(run)

    scores = _sigmoid(_dot_nt(rt_ref[...], h_ref[...], HI))
    biased = scores + b_ref[...]
    b3 = biased.reshape(N_GROUPS, gsz, tm)
    i3 = lax.broadcasted_iota(jnp.int32, (1, gsz, 1), 1).astype(F32)
    m1 = jnp.max(b3, axis=1, keepdims=True)
    f1 = _first_index(b3 == m1, i3, float(gsz), 1)
    m2 = jnp.max(jnp.where(i3 == f1, -jnp.inf, b3), axis=1, keepdims=True)
    gs = (m1 + m2).reshape(N_GROUPS, tm)
    gidx = lax.broadcasted_iota(jnp.int32, (N_GROUPS, 1), 0).astype(F32)
    gmask = jnp.zeros((N_GROUPS, tm), F32)
    for _ in range(TOPK_GROUPS):
        m = jnp.max(gs, axis=0, keepdims=True)
        pick = gidx == _first_index(gs == m, gidx, float(N_GROUPS), 0)
        gmask = jnp.where(pick, 1.0, gmask)
        gs = jnp.where(pick, -jnp.inf, gs)
    emask = jnp.broadcast_to(gmask.reshape(N_GROUPS, 1, tm), (N_GROUPS, gsz, tm)).reshape(ne, tm)
    work = jnp.where(emask > 0.5, biased, -jnp.inf)
    eidx = lax.broadcasted_iota(jnp.int32, (ne, 1), 0).astype(F32)
    picks, eids, ws = [], [], []
    for _ in range(TOP_K):
        m = jnp.max(work, axis=0, keepdims=True)
        first = _first_index(work == m, eidx, float(ne), 0)
        pick = eidx == first
        picks.append(pick)
        eids.append(first)
        ws.append(jnp.sum(jnp.where(pick, scores, 0.0), axis=0, keepdims=True))
        work = jnp.where(pick, -jnp.inf, work)
    wsum = ws[0]
    for k in range(1, TOP_K):
        wsum = wsum + ws[k]
    chosen = jnp.zeros((ne, tm), F32)
    for pick in picks:
        chosen = jnp.where(pick, 1.0, chosen)
    pos = run[...] + _dot(chosen.astype(BF16), up_ref[...])
    run[...] = run[...] + jnp.sum(chosen, axis=1, keepdims=True)
    cnt_ref[...] = run[...]
    row = lax.broadcasted_iota(jnp.int32, (8, 1), 0)
    eid_o = jnp.zeros((8, tm), F32)
    rank_o = jnp.zeros((8, tm), F32)
    w_o = jnp.zeros((LANES, tm), F32)
    rowl = lax.broadcasted_iota(jnp.int32, (LANES, 1), 0)
    for k in range(TOP_K):
        rk = jnp.sum(jnp.where(picks[k], pos, 0.0), axis=0, keepdims=True)
        eid_o = jnp.where(row == k, eids[k], eid_o)
        rank_o = jnp.where(row == k, rk, rank_o)
        w_o = jnp.where(rowl == k, ws[k] / wsum * ROUTE_SCALE, w_o)
    eid_ref[0] = eid_o.astype(jnp.int32)
    rank_ref[0] = rank_o.astype(jnp.int32)
    w_ref[...] = w_o.T


def moe_route(h2, router, e_bias, tm=MOE_TM):
    nt, d = h2.shape
    ne = N_EXPERTS
    up = jnp.asarray(np.triu(np.ones((tm, tm), np.float32), 1), dtype=BF16)
    nb = nt // tm
    return pl.pallas_call(
        _route_kernel,
        grid=(nb,),
        in_specs=[pl.BlockSpec((tm, d), lambda i: (i, 0)), _full((ne, d)), _full((ne, 1)), _full((tm, tm))],
        out_specs=[pl.BlockSpec((1, 8, tm), lambda i: (i, 0, 0)),
                   pl.BlockSpec((1, 8, tm), lambda i: (i, 0, 0)),
                   pl.BlockSpec((tm, LANES), lambda i: (i, 0)),
                   _full((ne, 1))],
        out_shape=[jax.ShapeDtypeStruct((nb, 8, tm), jnp.int32), jax.ShapeDtypeStruct((nb, 8, tm), jnp.int32),
                   jax.ShapeDtypeStruct((nt, LANES), F32), jax.ShapeDtypeStruct((ne, 1), F32)],
        scratch_shapes=[pltpu.VMEM((ne, 1), F32)],
        compiler_params=_cparams(("arbitrary",)),
        name="moe_route",
    )(h2, router.T, e_bias.reshape(ne, 1), up)


def _dispatch_kernel(dest_ref, h_ref, xs_ref, sem):
    tm = h_ref.shape[0]

    def copy(t, row):
        return pltpu.make_async_copy(h_ref.at[pl.ds(t, 1), :], xs_ref.at[pl.ds(row, 1), :], sem)

    def issue(t, _):
        for k in range(TOP_K):
            copy(t, dest_ref[0, t, k]).start(priority=k % 2)
        return 0

    def drain(t, _):
        for k in range(TOP_K):
            copy(0, 0).wait()
        return 0

    lax.fori_loop(0, tm, issue, 0, unroll=4)
    lax.fori_loop(0, tm, drain, 0, unroll=4)


def moe_dispatch(h2, dest, tm=MOE_TM):
    nt, d = h2.shape
    return pl.pallas_call(
        _dispatch_kernel,
        grid=(nt // tm,),
        in_specs=[pl.BlockSpec((1, tm, 8), lambda i: (i, 0, 0), memory_space=pltpu.SMEM),
                  pl.BlockSpec((tm, d), lambda i: (i, 0))],
        out_specs=pl.BlockSpec(memory_space=pl.ANY),
        scratch_shapes=[pltpu.SemaphoreType.DMA(())],
        out_shape=jax.ShapeDtypeStruct((nt * TOP_K, d), F32),
        compiler_params=_cparams(("arbitrary",)),
        name="moe_dispatch",
    )(dest, h2)


def _ffn_kernel(blk_ref, exp_ref, lo_ref, hi_ref, first_ref, valid_ref, x_ref, wg_ref, wu_ref, wd_ref, o_ref,
                wg_b, wu_b, wd_b):
    i = pl.program_id(0)
    rows = x_ref.shape[0]

    @pl.when((i == 0) | (exp_ref[i] != exp_ref[jnp.maximum(i - 1, 0)]))
    def _():
        wg_b[...] = wg_ref[0].astype(BF16)
        wu_b[...] = wu_ref[0].astype(BF16)
        wd_b[...] = wd_ref[0].astype(BF16)

    @pl.when(valid_ref[i] == 1)
    def _():
        x = x_ref[...].astype(BF16)
        a = _dot(x, wg_b[...])
        u = _dot(x, wu_b[...])
        y = _dot((_silu(a) * u).astype(BF16), wd_b[...])
        r = blk_ref[i] * rows + lax.broadcasted_iota(jnp.int32, (rows, 1), 0)
        y = jnp.where((r >= lo_ref[i]) & (r < hi_ref[i]), y, 0.0)

        @pl.when(first_ref[i] == 1)
        def _():
            o_ref[...] = y

        @pl.when(first_ref[i] == 0)
        def _():
            o_ref[...] = o_ref[...] + y


def _items_kernel(cnt_ref, starts_ref, blk_ref, exp_ref, lo_ref, hi_ref, first_ref, valid_ref, *, rows, n_items):
    shift = int(math.log2(rows))

    def expert(e, carry):
        start, n = carry
        c = cnt_ref[e]
        starts_ref[e] = start
        end = start + c
        first_blk = lax.shift_right_logical(start, shift)
        n_blk = jnp.where(c > 0, lax.shift_right_logical(jnp.maximum(end - 1, 0), shift) - first_blk + 1, 0)

        def item(k, n):
            b = first_blk + k
            lo = jnp.maximum(start, b * rows)
            blk_ref[n] = b
            exp_ref[n] = e
            lo_ref[n] = lo
            hi_ref[n] = jnp.minimum(end, (b + 1) * rows)
            first_ref[n] = (lo == b * rows).astype(jnp.int32)
            valid_ref[n] = 1
            return n + 1

        return end, lax.fori_loop(0, n_blk, item, n)

    _, total = lax.fori_loop(0, N_EXPERTS, expert, (jnp.int32(0), jnp.int32(0)))
    last = jnp.maximum(total - 1, 0)

    def fill(k, _):
        blk_ref[k] = blk_ref[last]
        exp_ref[k] = exp_ref[last]
        lo_ref[k] = 0
        hi_ref[k] = 0
        first_ref[k] = 0
        valid_ref[k] = 0
        return 0

    lax.fori_loop(total, n_items, fill, 0)


def _ffn_items(counts, n_rows, rows):
    n_items = n_rows // rows + N_EXPERTS - 1
    smem = pl.BlockSpec(memory_space=pltpu.SMEM)
    out = pl.pallas_call(
        functools.partial(_items_kernel, rows=rows, n_items=n_items),
        in_specs=[smem],
        out_specs=[smem] * 7,
        out_shape=[jax.ShapeDtypeStruct((N_EXPERTS,), jnp.int32)]
                  + [jax.ShapeDtypeStruct((n_items,), jnp.int32)] * 6,
        name="moe_items",
    )(counts)
    return out[0], tuple(out[1:])


def moe_ffn_sorted(xs, items, wg, wu, wd, layer, rows=MOE_ROWS):
    n_rows, d = xs.shape
    n_items = items[0].shape[0]
    de = wg.shape[-1]
    return pl.pallas_call(
        _ffn_kernel,
        grid_spec=pltpu.PrefetchScalarGridSpec(
            num_scalar_prefetch=6,
            grid=(n_items,),
            in_specs=[pl.BlockSpec((rows, d), lambda i, blk, e, *_: (blk[i], 0)),
                      pl.BlockSpec((None, 1, d, de), lambda i, blk, e, *_: (layer, e[i], 0, 0)),
                      pl.BlockSpec((None, 1, d, de), lambda i, blk, e, *_: (layer, e[i], 0, 0)),
                      pl.BlockSpec((None, 1, de, d), lambda i, blk, e, *_: (layer, e[i], 0, 0))],
            out_specs=pl.BlockSpec((rows, d), lambda i, blk, e, *_: (blk[i], 0)),
            scratch_shapes=[pltpu.VMEM((d, de), BF16), pltpu.VMEM((d, de), BF16), pltpu.VMEM((de, d), BF16)]),
        out_shape=jax.ShapeDtypeStruct((n_rows, d), F32),
        compiler_params=_cparams(("arbitrary",)),
        name="moe_ffn",
    )(*items, xs, wg, wu, wd)


def _combine_kernel(dest_ref, ys_ref, w_ref, h_ref, x_ref, g_ref, sg_ref, su_ref, sd_ref, *rest):
    if len(rest) == 3:
        nxt, (o_ref, buf, sem) = None, rest
    else:
        nxt, (o_ref, hn_ref, buf, sem) = rest[:3], rest[3:]
    tm = h_ref.shape[0]

    def copy(t, k, row):
        return pltpu.make_async_copy(ys_ref.at[pl.ds(row, 1), :], buf.at[k, pl.ds(t, 1), :], sem)

    def issue(t, _):
        for k in range(TOP_K):
            copy(t, k, dest_ref[0, t, k]).start(priority=k % 2)
        return 0

    def drain(t, _):
        for k in range(TOP_K):
            copy(0, 0, 0).wait()
        return 0

    lax.fori_loop(0, tm, issue, 0, unroll=4)
    hb = h_ref[...].astype(BF16)
    y = _dot((_silu(_dot(hb, sg_ref[...])) * _dot(hb, su_ref[...])).astype(BF16), sd_ref[...])
    lax.fori_loop(0, tm, drain, 0, unroll=4)
    w = w_ref[...]
    for k in range(TOP_K):
        y = y + w[:, k:k + 1] * buf[k]
    x_new = x_ref[...] + g_ref[0] * y
    o_ref[...] = x_new
    if nxt is not None:
        hn_ref[...] = _norm_mod(x_new, nxt[0][...], nxt[1][0], nxt[2][0]).astype(hn_ref.dtype)


def moe_combine(ys, dest, w, h2, x2, gate, sg, su, sd, seq, next_norm=None, tm=MOE_TM):
    nt, d = h2.shape
    ds_ = sg.shape[-1]
    per_b = seq // tm
    tile = lambda: pl.BlockSpec((tm, d), lambda i: (i, 0))
    batch = lambda: pl.BlockSpec((1, 1, d), lambda i: (i // per_b, 0, 0))
    nb = gate.shape[0]
    extra_in, extra_specs = [], []
    out_specs, out_shape = tile(), jax.ShapeDtypeStruct((nt, d), F32)
    if next_norm is not None:
        g_n, sc_n, sh_n = next_norm
        extra_in = [g_n.reshape(1, d), sc_n.reshape(nb, 1, d), sh_n.reshape(nb, 1, d)]
        extra_specs = [_full((1, d)), batch(), batch()]
        out_specs = [tile(), tile()]
        out_shape = [out_shape, jax.ShapeDtypeStruct((nt, d), BF16)]
    return pl.pallas_call(
        _combine_kernel,
        grid=(nt // tm,),
        in_specs=[pl.BlockSpec((1, tm, 8), lambda i: (i, 0, 0), memory_space=pltpu.SMEM),
                  pl.BlockSpec(memory_space=pl.ANY),
                  pl.BlockSpec((tm, LANES), lambda i: (i, 0)), tile(), tile(), batch(),
                  _full((d, ds_)), _full((d, ds_)), _full((ds_, d))] + extra_specs,
        out_specs=out_specs,
        scratch_shapes=[pltpu.VMEM((TOP_K, tm, d), F32), pltpu.SemaphoreType.DMA(())],
        out_shape=out_shape,
        compiler_params=_cparams(("arbitrary",)),
        name="moe_combine",
    )(dest, ys, w, h2, x2, gate, sg.astype(BF16), su.astype(BF16), sd.astype(BF16), *extra_in)


def moe_layer(x, h, gate, router, e_bias, wg, wu, wd, layer, sg, su, sd, next_norm=None):
    b, t, d = x.shape
    nt = b * t
    h2 = h.reshape(nt, d)
    eid, rank, w, counts = moe_route(h2, router, e_bias, tm=ROUTE_TM)
    recut = lambda a: a.reshape(-1, 8, ROUTE_TM // MOE_TM, MOE_TM).transpose(0, 2, 1, 3).reshape(-1, 8, MOE_TM)
    eid, rank = recut(eid), recut(rank)
    starts, items = _ffn_items(counts.reshape(-1).astype(jnp.int32), nt * TOP_K, MOE_ROWS)
    hit = eid[..., None] == jnp.arange(N_EXPERTS, dtype=jnp.int32)
    dest = jnp.sum(jnp.where(hit, starts.astype(jnp.int32), 0), axis=-1) + rank
    dest = jnp.swapaxes(dest, 1, 2)
    xs = moe_dispatch(h2, dest)
    ys = moe_ffn_sorted(xs, items, wg, wu, wd, layer)
    out = moe_combine(ys, dest, w, h2, x.reshape(nt, d), gate.reshape(b, 1, d), sg, su, sd, t, next_norm)
    if next_norm is None:
        return out.reshape(b, t, d), None
    return out[0].reshape(b, t, d), out[1].reshape(b, t, d)


def kernel(x, c, ada_w, ada_b, norm_mix, norm_ffn, rel_bias, ev_w_in, ev_w_out, fox_fb, fox_qn, fox_kn, gdn_conv, gdn_a_log, gdn_dt_bias, gdn_on, od_w_in, od_w_out, nsa_qn, nsa_kn, nsa_pos, nsa_cmp_w1, nsa_cmp_w2, gla_wg_up, gla_bg, gla_on, moe_router, moe_bias, moe_wg, moe_wu, moe_wd, sh_wg, sh_wu, sh_wd):
    d = x.shape[-1]
    depth = ada_w.shape[0]
    mod = adaln(c, ada_w, ada_b)
    mods = [[mod[layer, :, i * d:(i + 1) * d] for i in range(6)] for layer in range(depth)]
    h = ln_mod(x, norm_mix[0], mods[0][1], mods[0][0], BF16)
    for layer in range(depth):
        sh1, sc1, g1, sh2, sc2, g2 = mods[layer]
        j = layer // 2
        if layer % 2 == 0:
            y1, y2 = even_mixer(h, ev_w_in[j], fox_fb[j], fox_qn[j], fox_kn[j], gdn_conv[j], gdn_a_log[j],
                                gdn_dt_bias[j], gdn_on[j])
            w_out = ev_w_out[j]
        else:
            y1, y2 = odd_mixer(h, od_w_in[j], nsa_qn[j], nsa_kn[j], nsa_pos[j], nsa_cmp_w1[j], nsa_cmp_w2[j],
                               gla_wg_up[j], gla_bg[j], gla_on[j], rel_bias)
            w_out = od_w_out[j]
        x, h_ffn = out_proj(y1, y2, w_out, x, g1, norm_ffn[layer], sc2, sh2)
        nxt = None if layer + 1 == depth else (norm_mix[layer + 1], mods[layer + 1][1], mods[layer + 1][0])
        x, h = moe_layer(x, h_ffn, g2, moe_router[layer], moe_bias[layer], moe_wg, moe_wu, moe_wd, layer,
                         sh_wg[layer], sh_wu[layer], sh_wd[layer], nxt)
    return x
```
